```python
import math
import jax, jax.numpy as jnp
from jax import lax
import numpy as np

D_MODEL = 2048
BATCH = 8
SEQ = 8192
DEPTH = 1

GLA_HEADS = 4
GLA_HEAD_V = D_MODEL // 8
GLA_HEAD_K = GLA_HEAD_V // 2
GLA_DK = GLA_HEADS * GLA_HEAD_K
GLA_DV = GLA_HEADS * GLA_HEAD_V
GLA_GATE_RANK = 16
GLA_GATE_TAU = 16.0
GLA_CHUNK = 32

S5_GROUP_CH = 16
S5_WIDTH = D_MODEL // 4
S5_GROUPS = S5_WIDTH // S5_GROUP_CH
S5_STATE = 64
S5_DT_MIN = 1e-3
S5_DT_MAX = 1e-1

D_FF = ((8 * D_MODEL // 3 + 255) // 256) * 256

NORM_EPS = 1e-6
IN_WIDTHS = (GLA_DK, GLA_DK, GLA_DV, GLA_DV, GLA_GATE_RANK, S5_WIDTH, D_MODEL, D_MODEL)
D_IN = sum(IN_WIDTHS)

kernel_name = "hybrid_gla_s5_gated_block"


def _rmsnorm(x, g):
    xf = x.astype(jnp.float32)
    y = xf * lax.rsqrt(jnp.mean(xf * xf, axis=-1, keepdims=True) + NORM_EPS)
    return (y * g.astype(jnp.float32)).astype(x.dtype)


def _gla_branch(q, k, v, r, a_low, w_a2, b_a2, g_hn):
    f32 = jnp.float32
    bsz, seq, _ = q.shape
    n_chunks = seq // GLA_CHUNK
    z = (a_low @ w_a2 + b_a2).astype(f32)
    log_a = jax.nn.log_sigmoid(z) / GLA_GATE_TAU

    def chunks(t, hd):
        return t.astype(f32).reshape(bsz, n_chunks, GLA_CHUNK, GLA_HEADS, hd).transpose(0, 3, 1, 2, 4)

    qc = chunks(q, GLA_HEAD_K) * (GLA_HEAD_K ** -0.5)
    kc = chunks(k, GLA_HEAD_K)
    vc = chunks(v, GLA_HEAD_V)
    bc = jnp.cumsum(chunks(log_a, GLA_HEAD_K), axis=3)
    b_last = bc[..., -1:, :]
    q_dec = qc * jnp.exp(bc)
    k_intra = kc * jnp.exp(-bc)
    k_state = kc * jnp.exp(b_last - bc)

    causal = jnp.tril(jnp.ones((GLA_CHUNK, GLA_CHUNK), dtype=bool))
    scores = jnp.einsum('bhncd,bhnsd->bhncs', q_dec, k_intra)
    scores = jnp.where(causal, scores, 0.0)
    o_intra = jnp.einsum('bhncs,bhnsv->bhncv', scores, vc)

    def step(state, xs):
        qd, ks, vv, dl = xs
        o = jnp.einsum('bhcd,bhdv->bhcv', qd, state)
        state = dl[..., :, None] * state + jnp.einsum('bhcd,bhcv->bhdv', ks, vv)
        return state, o

    xs = (jnp.moveaxis(q_dec, 2, 0), jnp.moveaxis(k_state, 2, 0), jnp.moveaxis(vc, 2, 0),
          jnp.moveaxis(jnp.exp(b_last[..., 0, :]), 2, 0))
    s0 = jnp.zeros((bsz, GLA_HEADS, GLA_HEAD_K, GLA_HEAD_V), f32)
    _, o_inter = lax.scan(step, s0, xs)
    o = o_intra + jnp.moveaxis(o_inter, 0, 2)
    o = o.transpose(0, 2, 3, 1, 4).reshape(bsz, seq, GLA_HEADS, GLA_HEAD_V)
    o = o * lax.rsqrt(jnp.mean(o * o, axis=-1, keepdims=True) + NORM_EPS)
    o = o.reshape(bsz, seq, GLA_DV) * g_hn.astype(f32)
    return (jax.nn.silu(r.astype(f32)) * o).astype(q.dtype)


def _s5_branch(u, lam_re, lam_im, log_dt, b_re, b_im, c_re, c_im, d_skip, w_glu, b_glu):
    f32 = jnp.float32
    bsz, seq, _ = u.shape
    uf = u.astype(f32).reshape(bsz, seq, S5_GROUPS, S5_GROUP_CH)
    dt = jnp.exp(log_dt.astype(f32))[:, None]
    lr, li = lam_re.astype(f32), lam_im.astype(f32)
    mag = jnp.exp(lr * dt)
    abar_re = mag * jnp.cos(li * dt)
    abar_im = mag * jnp.sin(li * dt)
    den = lr * lr + li * li
    am1 = abar_re - 1.0
    f_re = ((am1 * lr + abar_im * li) / den)[..., None]
    f_im = ((abar_im * lr - am1 * li) / den)[..., None]
    br, bi = b_re.astype(f32), b_im.astype(f32)
    bbar_re = f_re * br - f_im * bi
    bbar_im = f_re * bi + f_im * br
    bu_re = jnp.einsum('btgc,gpc->tbgp', uf, bbar_re)
    bu_im = jnp.einsum('btgc,gpc->tbgp', uf, bbar_im)
    a_re = jnp.broadcast_to(abar_re, (seq, 1, S5_GROUPS, S5_STATE))
    a_im = jnp.broadcast_to(abar_im, (seq, 1, S5_GROUPS, S5_STATE))

    def combine(left, right):
        ar_l, ai_l, br_l, bi_l = left
        ar_r, ai_r, br_r, bi_r = right
        return (ar_r * ar_l - ai_r * ai_l,
                ar_r * ai_l + ai_r * ar_l,
                ar_r * br_l - ai_r * bi_l + br_r,
                ar_r * bi_l + ai_r * br_l + bi_r)

    _, _, xr, xi = lax.associative_scan(combine, (a_re, a_im, bu_re, bu_im), axis=0)
    y = (jnp.einsum('tbgp,gcp->btgc', xr, c_re.astype(f32))
         - jnp.einsum('tbgp,gcp->btgc', xi, c_im.astype(f32)))
    y = y.reshape(bsz, seq, S5_WIDTH) + d_skip.astype(f32) * u.astype(f32)
    h = jax.nn.gelu(y)
    out = h * jax.nn.sigmoid(h @ w_glu.astype(f32) + b_glu.astype(f32))
    return out.astype(u.dtype)


def _fwd_setup_inputs(seed: int = 0) -> dict:
    key = jax.random.key(seed)
    ks = jax.random.split(key, 24)

    def nrm(k, shape, scale):
        return jax.random.normal(k, (DEPTH,) + shape, jnp.float32) * scale

    def gain(k, shape):
        return 1.0 + 0.02 * jax.random.normal(k, (DEPTH,) + shape, jnp.float32)

    n_idx = jnp.arange(S5_STATE, dtype=jnp.float32)
    lam_re = -0.5 + 0.01 * jax.random.normal(ks[5], (DEPTH, S5_GROUPS, S5_STATE), jnp.float32)
    lam_im = math.pi * n_idx + 0.01 * jax.random.normal(ks[6], (DEPTH, S5_GROUPS, S5_STATE), jnp.float32)
    log_dt = jax.random.uniform(ks[7], (DEPTH, S5_GROUPS), jnp.float32,
                                math.log(S5_DT_MIN), math.log(S5_DT_MAX))
    return {
        "x": jax.random.normal(ks[0], (BATCH, SEQ, D_MODEL), jnp.float32),
        "norm1_g": gain(ks[1], (D_MODEL,)),
        "w_in": nrm(ks[2], (D_MODEL, D_IN), D_MODEL ** -0.5),
        "w_a2": nrm(ks[3], (GLA_GATE_RANK, GLA_DK), GLA_GATE_RANK ** -0.5),
        "b_a2": nrm(ks[4], (GLA_DK,), 0.1),
        "gla_norm_g": gain(ks[8], (GLA_DV,)),
        "lam_re": lam_re,
        "lam_im": lam_im,
        "log_dt": log_dt,
        "s5_b_re": nrm(ks[9], (S5_GROUPS, S5_STATE, S5_GROUP_CH), (2 * S5_GROUP_CH) ** -0.5),
        "s5_b_im": nrm(ks[10], (S5_GROUPS, S5_STATE, S5_GROUP_CH), (2 * S5_GROUP_CH) ** -0.5),
        "s5_c_re": nrm(ks[11], (S5_GROUPS, S5_GROUP_CH, S5_STATE), (2 * S5_STATE) ** -0.5),
        "s5_c_im": nrm(ks[12], (S5_GROUPS, S5_GROUP_CH, S5_STATE), (2 * S5_STATE) ** -0.5),
        "s5_d": nrm(ks[13], (S5_WIDTH,), 1.0),
        "w_glu": nrm(ks[14], (S5_WIDTH, S5_WIDTH), S5_WIDTH ** -0.5),
        "b_glu": nrm(ks[15], (S5_WIDTH,), 0.02),
        "w_branch_a": nrm(ks[16], (GLA_DV, D_MODEL), GLA_DV ** -0.5),
        "w_branch_b": nrm(ks[17], (S5_WIDTH, D_MODEL), S5_WIDTH ** -0.5),
        "w_out": nrm(ks[18], (D_MODEL, D_MODEL), D_MODEL ** -0.5),
        "norm2_g": gain(ks[19], (D_MODEL,)),
        "w_ffn_in": nrm(ks[20], (D_MODEL, 2 * D_FF), D_MODEL ** -0.5),
        "w_ffn_out": nrm(ks[21], (D_FF, D_MODEL), D_FF ** -0.5),
        "final_norm_g": gain(ks[22], (D_MODEL,)),
    }


def _fwd_reference(x, norm1_g, w_in, w_a2, b_a2, gla_norm_g, lam_re, lam_im, log_dt,
              s5_b_re, s5_b_im, s5_c_re, s5_c_im, s5_d, w_glu, b_glu,
              w_branch_a, w_branch_b, w_out, norm2_g, w_ffn_in, w_ffn_out, final_norm_g):
    splits = np.cumsum(IN_WIDTHS)[:-1].tolist()
    for l in range(DEPTH):
        h = _rmsnorm(x, norm1_g[l])
        proj = h @ w_in[l]
        q, k, v, r, a_low, u, gate_a, gate_b = jnp.split(proj, splits, axis=-1)
        o_a = _gla_branch(q, k, v, r, a_low, w_a2[l], b_a2[l], gla_norm_g[l])
        o_b = _s5_branch(u, lam_re[l], lam_im[l], log_dt[l], s5_b_re[l], s5_b_im[l],
                         s5_c_re[l], s5_c_im[l], s5_d[l], w_glu[l], b_glu[l])
        mix = (jax.nn.sigmoid(gate_a) * (o_a @ w_branch_a[l])
               + jax.nn.sigmoid(gate_b) * (o_b @ w_branch_b[l]))
        x = x + mix @ w_out[l]
        h = _rmsnorm(x, norm2_g[l])
        gate, up = jnp.split(h @ w_ffn_in[l], 2, axis=-1)
        x = x + (jax.nn.silu(gate) * up) @ w_ffn_out[l]
    return _rmsnorm(x, final_norm_g)


import jax as _jax
import jax.numpy as _jnp

TWIN_FORMAT = 'train_step'
FWD_PARAMS = ['x', 'norm1_g', 'w_in', 'w_a2', 'b_a2', 'gla_norm_g', 'lam_re', 'lam_im', 'log_dt', 's5_b_re', 's5_b_im', 's5_c_re', 's5_c_im', 's5_d', 'w_glu', 'b_glu', 'w_branch_a', 'w_branch_b', 'w_out', 'norm2_g', 'w_ffn_in', 'w_ffn_out', 'final_norm_g']
TWIN_WEIGHTS = ['norm1_g', 'w_in', 'w_a2', 'b_a2', 'gla_norm_g', 'lam_re', 'lam_im', 'log_dt', 's5_b_re', 's5_b_im', 's5_c_re', 's5_c_im', 's5_d', 'w_glu', 'b_glu', 'w_branch_a', 'w_branch_b', 'w_out', 'norm2_g', 'w_ffn_in', 'w_ffn_out', 'final_norm_g']
TWIN_DIFF_INPUT = 'x'
TWIN_INPUTS = ['x', 'norm1_g', 'w_in', 'w_a2', 'b_a2', 'gla_norm_g', 'lam_re', 'lam_im', 'log_dt', 's5_b_re', 's5_b_im', 's5_c_re', 's5_c_im', 's5_d', 'w_glu', 'b_glu', 'w_branch_a', 'w_branch_b', 'w_out', 'norm2_g', 'w_ffn_in', 'w_ffn_out', 'final_norm_g', 'loss_target', 'm_norm1_g', 'm_w_in', 'm_w_a2', 'm_b_a2', 'm_gla_norm_g', 'm_lam_re', 'm_lam_im', 'm_log_dt', 'm_s5_b_re', 'm_s5_b_im', 'm_s5_c_re', 'm_s5_c_im', 'm_s5_d', 'm_w_glu', 'm_b_glu', 'm_w_branch_a', 'm_w_branch_b', 'm_w_out', 'm_norm2_g', 'm_w_ffn_in', 'm_w_ffn_out', 'm_final_norm_g', 'v_norm1_g', 'v_w_in', 'v_w_a2', 'v_b_a2', 'v_gla_norm_g', 'v_lam_re', 'v_lam_im', 'v_log_dt', 'v_s5_b_re', 'v_s5_b_im', 'v_s5_c_re', 'v_s5_c_im', 'v_s5_d', 'v_w_glu', 'v_b_glu', 'v_w_branch_a', 'v_w_branch_b', 'v_w_out', 'v_norm2_g', 'v_w_ffn_in', 'v_w_ffn_out', 'v_final_norm_g']
TWIN_OUTPUTS = ['loss', 'grad_x', 'grad_norm1_g', 'grad_w_in', 'grad_w_a2', 'grad_b_a2', 'grad_gla_norm_g', 'grad_lam_re', 'grad_lam_im', 'grad_log_dt', 'grad_s5_b_re', 'grad_s5_b_im', 'grad_s5_c_re', 'grad_s5_c_im', 'grad_s5_d', 'grad_w_glu', 'grad_b_glu', 'grad_w_branch_a', 'grad_w_branch_b', 'grad_w_out', 'grad_norm2_g', 'grad_w_ffn_in', 'grad_w_ffn_out', 'grad_final_norm_g', 'delta_norm1_g', 'delta_w_in', 'delta_w_a2', 'delta_b_a2', 'delta_gla_norm_g', 'delta_lam_re', 'delta_lam_im', 'delta_log_dt', 'delta_s5_b_re', 'delta_s5_b_im', 'delta_s5_c_re', 'delta_s5_c_im', 'delta_s5_d', 'delta_w_glu', 'delta_b_glu', 'delta_w_branch_a', 'delta_w_branch_b', 'delta_w_out', 'delta_norm2_g', 'delta_w_ffn_in', 'delta_w_ffn_out', 'delta_final_norm_g', 'new_m_norm1_g', 'new_m_w_in', 'new_m_w_a2', 'new_m_b_a2', 'new_m_gla_norm_g', 'new_m_lam_re', 'new_m_lam_im', 'new_m_log_dt', 'new_m_s5_b_re', 'new_m_s5_b_im', 'new_m_s5_c_re', 'new_m_s5_c_im', 'new_m_s5_d', 'new_m_w_glu', 'new_m_b_glu', 'new_m_w_branch_a', 'new_m_w_branch_b', 'new_m_w_out', 'new_m_norm2_g', 'new_m_w_ffn_in', 'new_m_w_ffn_out', 'new_m_final_norm_g', 'new_v_norm1_g', 'new_v_w_in', 'new_v_w_a2', 'new_v_b_a2', 'new_v_gla_norm_g', 'new_v_lam_re', 'new_v_lam_im', 'new_v_log_dt', 'new_v_s5_b_re', 'new_v_s5_b_im', 'new_v_s5_c_re', 'new_v_s5_c_im', 'new_v_s5_d', 'new_v_w_glu', 'new_v_b_glu', 'new_v_w_branch_a', 'new_v_w_branch_b', 'new_v_w_out', 'new_v_norm2_g', 'new_v_w_ffn_in', 'new_v_w_ffn_out', 'new_v_final_norm_g']
TWIN_LEAF_KINDS = {'loss': 'loss', 'grad_x': 'grad_x', 'grad_norm1_g': 'grad_w', 'grad_w_in': 'grad_w', 'grad_w_a2': 'grad_w', 'grad_b_a2': 'grad_w', 'grad_gla_norm_g': 'grad_w', 'grad_lam_re': 'grad_w', 'grad_lam_im': 'grad_w', 'grad_log_dt': 'grad_w', 'grad_s5_b_re': 'grad_w', 'grad_s5_b_im': 'grad_w', 'grad_s5_c_re': 'grad_w', 'grad_s5_c_im': 'grad_w', 'grad_s5_d': 'grad_w', 'grad_w_glu': 'grad_w', 'grad_b_glu': 'grad_w', 'grad_w_branch_a': 'grad_w', 'grad_w_branch_b': 'grad_w', 'grad_w_out': 'grad_w', 'grad_norm2_g': 'grad_w', 'grad_w_ffn_in': 'grad_w', 'grad_w_ffn_out': 'grad_w', 'grad_final_norm_g': 'grad_w', 'delta_norm1_g': 'delta_w', 'delta_w_in': 'delta_w', 'delta_w_a2': 'delta_w', 'delta_b_a2': 'delta_w', 'delta_gla_norm_g': 'delta_w', 'delta_lam_re': 'delta_w', 'delta_lam_im': 'delta_w', 'delta_log_dt': 'delta_w', 'delta_s5_b_re': 'delta_w', 'delta_s5_b_im': 'delta_w', 'delta_s5_c_re': 'delta_w', 'delta_s5_c_im': 'delta_w', 'delta_s5_d': 'delta_w', 'delta_w_glu': 'delta_w', 'delta_b_glu': 'delta_w', 'delta_w_branch_a': 'delta_w', 'delta_w_branch_b': 'delta_w', 'delta_w_out': 'delta_w', 'delta_norm2_g': 'delta_w', 'delta_w_ffn_in': 'delta_w', 'delta_w_ffn_out': 'delta_w', 'delta_final_norm_g': 'delta_w', 'new_m_norm1_g': 'new_m', 'new_m_w_in': 'new_m', 'new_m_w_a2': 'new_m', 'new_m_b_a2': 'new_m', 'new_m_gla_norm_g': 'new_m', 'new_m_lam_re': 'new_m', 'new_m_lam_im': 'new_m', 'new_m_log_dt': 'new_m', 'new_m_s5_b_re': 'new_m', 'new_m_s5_b_im': 'new_m', 'new_m_s5_c_re': 'new_m', 'new_m_s5_c_im': 'new_m', 'new_m_s5_d': 'new_m', 'new_m_w_glu': 'new_m', 'new_m_b_glu': 'new_m', 'new_m_w_branch_a': 'new_m', 'new_m_w_branch_b': 'new_m', 'new_m_w_out': 'new_m', 'new_m_norm2_g': 'new_m', 'new_m_w_ffn_in': 'new_m', 'new_m_w_ffn_out': 'new_m', 'new_m_final_norm_g': 'new_m', 'new_v_norm1_g': 'new_v', 'new_v_w_in': 'new_v', 'new_v_w_a2': 'new_v', 'new_v_b_a2': 'new_v', 'new_v_gla_norm_g': 'new_v', 'new_v_lam_re': 'new_v', 'new_v_lam_im': 'new_v', 'new_v_log_dt': 'new_v', 'new_v_s5_b_re': 'new_v', 'new_v_s5_b_im': 'new_v', 'new_v_s5_c_re': 'new_v', 'new_v_s5_c_im': 'new_v', 'new_v_s5_d': 'new_v', 'new_v_w_glu': 'new_v', 'new_v_b_glu': 'new_v', 'new_v_w_branch_a': 'new_v', 'new_v_w_branch_b': 'new_v', 'new_v_w_out': 'new_v', 'new_v_norm2_g': 'new_v', 'new_v_w_ffn_in': 'new_v', 'new_v_w_ffn_out': 'new_v', 'new_v_final_norm_g': 'new_v'}


def _forward(args):
    return _fwd_reference(*[args[k] for k in FWD_PARAMS])


def _output_shape():
    def fwd():
        inp = _fwd_setup_inputs(0)
        return _fwd_reference(*[inp[k] for k in FWD_PARAMS])
    out = _jax.eval_shape(fwd)
    return out.shape, out.dtype

N_MICROBATCH = 1
ADAM_LR = 0.001
ADAM_B1 = 0.9
ADAM_B2 = 0.999
ADAM_EPS = 1e-08
ADAM_WD = 0.01
ADAM_STEP = 10
PER_EXAMPLE_BATCH_AXIS = {'x': 0, 'loss_target': 0}
SHARED_INPUTS = []
_WEIGHT_DTYPES = {'norm1_g': _jnp.float32, 'w_in': _jnp.float32, 'w_a2': _jnp.float32, 'b_a2': _jnp.float32, 'gla_norm_g': _jnp.float32, 'lam_re': _jnp.float32, 'lam_im': _jnp.float32, 'log_dt': _jnp.float32, 's5_b_re': _jnp.float32, 's5_b_im': _jnp.float32, 's5_c_re': _jnp.float32, 's5_c_im': _jnp.float32, 's5_d': _jnp.float32, 'w_glu': _jnp.float32, 'b_glu': _jnp.float32, 'w_branch_a': _jnp.float32, 'w_branch_b': _jnp.float32, 'w_out': _jnp.float32, 'norm2_g': _jnp.float32, 'w_ffn_in': _jnp.float32, 'w_ffn_out': _jnp.float32, 'final_norm_g': _jnp.float32}
MOMENT_SCALE = {'norm1_g': 9.547281e-02, 'w_in': 4.841621e-02, 'w_a2': 9.909188e-03, 'b_a2': 3.693068e-02, 'gla_norm_g': 6.286031e-02, 'lam_re': 2.880000e-03, 'lam_im': 2.883144e-03, 'log_dt': 2.150775e+00, 's5_b_re': 1.677960e-03, 's5_b_im': 1.782223e-03, 's5_c_re': 3.586905e-03, 's5_c_im': 3.481461e-03, 's5_d': 5.339572e-02, 'w_glu': 1.514036e-02, 'b_glu': 2.121756e-02, 'w_branch_a': 4.259238e-02, 'w_branch_b': 2.561745e-02, 'w_out': 4.961501e-02, 'norm2_g': 8.808453e-02, 'w_ffn_in': 3.742930e-02, 'w_ffn_out': 6.112442e-02, 'final_norm_g': 3.195281e+01}


def _to_microbatches(a, axis):
    t = _jnp.moveaxis(a, axis, 0)
    t = t.reshape((N_MICROBATCH, t.shape[0] // N_MICROBATCH) + t.shape[1:])
    return _jnp.moveaxis(t, 1, axis + 1)


def setup_inputs(seed: int = 0) -> dict:
    inp = _fwd_setup_inputs(seed)
    key = _jax.random.fold_in(_jax.random.key(seed), 7919)
    shape, _ = _output_shape()
    out = dict(inp)
    out["loss_target"] = _jax.random.normal(_jax.random.fold_in(key, 0), shape, _jnp.float32)
    for i, name in enumerate(TWIN_WEIGHTS):
        w = inp[name].astype(_jnp.float32)
        if MOMENT_SCALE is None:
            s = _jnp.sqrt(_jnp.mean(_jnp.square(w)) + 1e-30)
        else:
            s = MOMENT_SCALE[name]
        km, kv = _jax.random.split(_jax.random.fold_in(key, i + 1))
        out[name] = w
        out["m_" + name] = s * _jax.random.normal(km, w.shape, _jnp.float32)
        out["v_" + name] = (s * s) * _jax.random.uniform(kv, w.shape, _jnp.float32, 0.5, 1.5)
    if N_MICROBATCH > 1:
        for name, axis in PER_EXAMPLE_BATCH_AXIS.items():
            out[name] = _to_microbatches(out[name], axis)
    return {'x': out['x'], 'norm1_g': out['norm1_g'], 'w_in': out['w_in'], 'w_a2': out['w_a2'], 'b_a2': out['b_a2'], 'gla_norm_g': out['gla_norm_g'], 'lam_re': out['lam_re'], 'lam_im': out['lam_im'], 'log_dt': out['log_dt'], 's5_b_re': out['s5_b_re'], 's5_b_im': out['s5_b_im'], 's5_c_re': out['s5_c_re'], 's5_c_im': out['s5_c_im'], 's5_d': out['s5_d'], 'w_glu': out['w_glu'], 'b_glu': out['b_glu'], 'w_branch_a': out['w_branch_a'], 'w_branch_b': out['w_branch_b'], 'w_out': out['w_out'], 'norm2_g': out['norm2_g'], 'w_ffn_in': out['w_ffn_in'], 'w_ffn_out': out['w_ffn_out'], 'final_norm_g': out['final_norm_g'], 'loss_target': out['loss_target'], 'm_norm1_g': out['m_norm1_g'], 'm_w_in': out['m_w_in'], 'm_w_a2': out['m_w_a2'], 'm_b_a2': out['m_b_a2'], 'm_gla_norm_g': out['m_gla_norm_g'], 'm_lam_re': out['m_lam_re'], 'm_lam_im': out['m_lam_im'], 'm_log_dt': out['m_log_dt'], 'm_s5_b_re': out['m_s5_b_re'], 'm_s5_b_im': out['m_s5_b_im'], 'm_s5_c_re': out['m_s5_c_re'], 'm_s5_c_im': out['m_s5_c_im'], 'm_s5_d': out['m_s5_d'], 'm_w_glu': out['m_w_glu'], 'm_b_glu': out['m_b_glu'], 'm_w_branch_a': out['m_w_branch_a'], 'm_w_branch_b': out['m_w_branch_b'], 'm_w_out': out['m_w_out'], 'm_norm2_g': out['m_norm2_g'], 'm_w_ffn_in': out['m_w_ffn_in'], 'm_w_ffn_out': out['m_w_ffn_out'], 'm_final_norm_g': out['m_final_norm_g'], 'v_norm1_g': out['v_norm1_g'], 'v_w_in': out['v_w_in'], 'v_w_a2': out['v_w_a2'], 'v_b_a2': out['v_b_a2'], 'v_gla_norm_g': out['v_gla_norm_g'], 'v_lam_re': out['v_lam_re'], 'v_lam_im': out['v_lam_im'], 'v_log_dt': out['v_log_dt'], 'v_s5_b_re': out['v_s5_b_re'], 'v_s5_b_im': out['v_s5_b_im'], 'v_s5_c_re': out['v_s5_c_re'], 'v_s5_c_im': out['v_s5_c_im'], 'v_s5_d': out['v_s5_d'], 'v_w_glu': out['v_w_glu'], 'v_b_glu': out['v_b_glu'], 'v_w_branch_a': out['v_w_branch_a'], 'v_w_branch_b': out['v_w_branch_b'], 'v_w_out': out['v_w_out'], 'v_norm2_g': out['v_norm2_g'], 'v_w_ffn_in': out['v_w_ffn_in'], 'v_w_ffn_out': out['v_w_ffn_out'], 'v_final_norm_g': out['v_final_norm_g']}


def _loss(weights, diff, rest, loss_target):
    with _jax.named_scope("forward"):
        args = {**rest, TWIN_DIFF_INPUT: diff, **{k: w.astype(_WEIGHT_DTYPES[k]) for k, w in weights.items()}}
        y = _forward(args)
    with _jax.named_scope("loss_head"):
        err = _jnp.square(y.astype(_jnp.float32) - loss_target)
        return 0.5 * _jnp.sum(_jnp.mean(err, axis=-1)) if err.ndim else 0.5 * err


def _adamw(w, g, m, v):
    m = ADAM_B1 * m + (1.0 - ADAM_B1) * g
    v = ADAM_B2 * v + (1.0 - ADAM_B2) * _jnp.square(g)
    m_hat = m / (1.0 - ADAM_B1 ** ADAM_STEP)
    v_hat = v / (1.0 - ADAM_B2 ** ADAM_STEP)
    delta = -ADAM_LR * (m_hat / (_jnp.sqrt(v_hat) + ADAM_EPS) + ADAM_WD * w)
    return delta, m, v


def reference(x, norm1_g, w_in, w_a2, b_a2, gla_norm_g, lam_re, lam_im, log_dt, s5_b_re, s5_b_im, s5_c_re, s5_c_im, s5_d, w_glu, b_glu, w_branch_a, w_branch_b, w_out, norm2_g, w_ffn_in, w_ffn_out, final_norm_g, loss_target, m_norm1_g, m_w_in, m_w_a2, m_b_a2, m_gla_norm_g, m_lam_re, m_lam_im, m_log_dt, m_s5_b_re, m_s5_b_im, m_s5_c_re, m_s5_c_im, m_s5_d, m_w_glu, m_b_glu, m_w_branch_a, m_w_branch_b, m_w_out, m_norm2_g, m_w_ffn_in, m_w_ffn_out, m_final_norm_g, v_norm1_g, v_w_in, v_w_a2, v_b_a2, v_gla_norm_g, v_lam_re, v_lam_im, v_log_dt, v_s5_b_re, v_s5_b_im, v_s5_c_re, v_s5_c_im, v_s5_d, v_w_glu, v_b_glu, v_w_branch_a, v_w_branch_b, v_w_out, v_norm2_g, v_w_ffn_in, v_w_ffn_out, v_final_norm_g):
    given = dict(x=x, norm1_g=norm1_g, w_in=w_in, w_a2=w_a2, b_a2=b_a2, gla_norm_g=gla_norm_g, lam_re=lam_re, lam_im=lam_im, log_dt=log_dt, s5_b_re=s5_b_re, s5_b_im=s5_b_im, s5_c_re=s5_c_re, s5_c_im=s5_c_im, s5_d=s5_d, w_glu=w_glu, b_glu=b_glu, w_branch_a=w_branch_a, w_branch_b=w_branch_b, w_out=w_out, norm2_g=norm2_g, w_ffn_in=w_ffn_in, w_ffn_out=w_ffn_out, final_norm_g=final_norm_g, loss_target=loss_target, m_norm1_g=m_norm1_g, m_w_in=m_w_in, m_w_a2=m_w_a2, m_b_a2=m_b_a2, m_gla_norm_g=m_gla_norm_g, m_lam_re=m_lam_re, m_lam_im=m_lam_im, m_log_dt=m_log_dt, m_s5_b_re=m_s5_b_re, m_s5_b_im=m_s5_b_im, m_s5_c_re=m_s5_c_re, m_s5_c_im=m_s5_c_im, m_s5_d=m_s5_d, m_w_glu=m_w_glu, m_b_glu=m_b_glu, m_w_branch_a=m_w_branch_a, m_w_branch_b=m_w_branch_b, m_w_out=m_w_out, m_norm2_g=m_norm2_g, m_w_ffn_in=m_w_ffn_in, m_w_ffn_out=m_w_ffn_out, m_final_norm_g=m_final_norm_g, v_norm1_g=v_norm1_g, v_w_in=v_w_in, v_w_a2=v_w_a2, v_b_a2=v_b_a2, v_gla_norm_g=v_gla_norm_g, v_lam_re=v_lam_re, v_lam_im=v_lam_im, v_log_dt=v_log_dt, v_s5_b_re=v_s5_b_re, v_s5_b_im=v_s5_b_im, v_s5_c_re=v_s5_c_re, v_s5_c_im=v_s5_c_im, v_s5_d=v_s5_d, v_w_glu=v_w_glu, v_b_glu=v_b_glu, v_w_branch_a=v_w_branch_a, v_w_branch_b=v_w_branch_b, v_w_out=v_w_out, v_norm2_g=v_norm2_g, v_w_ffn_in=v_w_ffn_in, v_w_ffn_out=v_w_ffn_out, v_final_norm_g=v_final_norm_g)
    weights = {n: given[n] for n in TWIN_WEIGHTS}
    shared = {n: given[n] for n in SHARED_INPUTS}
    per_example = {n: given[n] for n in ['x']}
    grad_fn = _jax.value_and_grad(_loss, argnums=(0, 1))

    def one_microbatch(ex, loss_target):
        ex = dict(ex)
        diff = ex.pop(TWIN_DIFF_INPUT)
        return grad_fn(weights, diff, {**shared, **ex}, loss_target)

    if N_MICROBATCH == 1:
        loss, (grad_w, grad_x) = one_microbatch(per_example, given["loss_target"])
    else:
        def body(carry, xs):
            loss_sum, grad_sum = carry
            l_k, (gw_k, gx_k) = one_microbatch(xs[0], xs[1])
            with _jax.named_scope("update"):
                return (loss_sum + l_k, _jax.tree.map(_jnp.add, grad_sum, gw_k)), gx_k

        init = (_jnp.zeros((), _jnp.float32), _jax.tree.map(_jnp.zeros_like, weights))
        (loss, grad_w), grad_x = _jax.lax.scan(body, init, (per_example, given["loss_target"]))
    with _jax.named_scope("update"):
        delta_w, new_m, new_v = {}, {}, {}
        for n in TWIN_WEIGHTS:
            delta_w[n], new_m[n], new_v[n] = _adamw(weights[n], grad_w[n], given["m_" + n], given["v_" + n])
    return (loss, grad_x, *[grad_w[n] for n in TWIN_WEIGHTS], *[delta_w[n] for n in TWIN_WEIGHTS],
            *[new_m[n] for n in TWIN_WEIGHTS], *[new_v[n] for n in TWIN_WEIGHTS])
```

```python
import functools
import math

import jax
import jax.numpy as jnp
from jax import lax
from jax.experimental import pallas as pl
from jax.experimental.pallas import tpu as pltpu

F32 = jnp.float32
BF16 = jnp.bfloat16

NORM_EPS = 1e-6
N_DEV = 8
GLA_HEADS = 4
GLA_CHUNK = 32
GLA_CHUNK_SHIFT = 5
GLA_TAU = 16.0
GLA_RANK = 16
GLA_BLOCK = 256
S5_GC = 16
S5_P = 64
S5_L = 32
LANE = 128
PACK_COLS = 1024
PACK_ROW_ALIGN = 256
V7X_VMEM_LIMIT = 56 * 1024 * 1024

ADAM_LR = 0.001
ADAM_B1 = 0.9
ADAM_B2 = 0.999
ADAM_EPS = 1e-08
ADAM_WD = 0.01
ADAM_STEP = 10

GELU_C = math.sqrt(2.0 / math.pi)
GELU_A = 0.044715

MESH = pl.DeviceIdType.MESH


def _cparams(*sem):
    return pltpu.CompilerParams(dimension_semantics=sem, vmem_limit_bytes=V7X_VMEM_LIMIT)


def _tile(n, target, align=LANE):
    if n <= target:
        return n
    t = (target // align) * align
    while t >= align:
        if n % t == 0:
            return t
        t -= align
    return n


def _sigmoid(x):
    return 1.0 / (1.0 + jnp.exp(-x))


def _mm(a, b, *, ta=False, tb=False, out_dtype=F32, res=None, name):
    m, k = (a.shape[1], a.shape[0]) if ta else a.shape
    k2, n = (b.shape[1], b.shape[0]) if tb else b.shape
    assert k == k2, (a.shape, b.shape, ta, tb)
    tm, tn, tk = _tile(m, 1024), _tile(n, 1024), _tile(k, 512)
    nk = k // tk
    dims = (((0,) if ta else (1,), (1,) if tb else (0,)), ((), ()))
    a_spec = pl.BlockSpec((tk, tm), lambda i, j, kk: (kk, i)) if ta else pl.BlockSpec((tm, tk), lambda i, j, kk: (i, kk))
    b_spec = pl.BlockSpec((tn, tk), lambda i, j, kk: (j, kk)) if tb else pl.BlockSpec((tk, tn), lambda i, j, kk: (kk, j))
    o_spec = pl.BlockSpec((tm, tn), lambda i, j, kk: (i, j))
    has_res = res is not None

    def body(*refs):
        if has_res:
            a_ref, b_ref, r_ref, o_ref, acc = refs
        else:
            a_ref, b_ref, o_ref, acc = refs
        kk = pl.program_id(2)

        @pl.when(kk == 0)
        def _():
            acc[...] = jnp.zeros_like(acc)

        acc[...] += lax.dot_general(a_ref[...], b_ref[...], dims, preferred_element_type=F32)

        @pl.when(kk == nk - 1)
        def _():
            out = acc[...]
            if has_res:
                out = out + r_ref[...].astype(F32)
            o_ref[...] = out.astype(out_dtype)

    return pl.pallas_call(
        body, name=name,
        grid=(m // tm, n // tn, nk),
        in_specs=[a_spec, b_spec] + ([o_spec] if has_res else []),
        out_specs=o_spec,
        out_shape=jax.ShapeDtypeStruct((m, n), out_dtype),
        scratch_shapes=[pltpu.VMEM((tm, tn), F32)],
        compiler_params=_cparams("parallel", "parallel", "arbitrary"),
    )(*((a, b, res) if has_res else (a, b)))


def _rowwise(fn, ins, row_outs, acc_outs, *, rows, tb, name):
    in_specs, args = [], []
    for spec in ins:
        kind, arr = spec[0], spec[1]
        if kind == "row":
            in_specs.append(pl.BlockSpec((tb, arr.shape[1]), lambda i: (i, 0)))
        elif kind == "win":
            width, cb = spec[2], spec[3]
            in_specs.append(pl.BlockSpec((tb, width), functools.partial(lambda i, cb: (i, cb), cb=cb)))
        else:
            in_specs.append(pl.BlockSpec(arr.shape, lambda i: (0, 0)))
        args.append(arr)
    out_specs = [pl.BlockSpec((tb, c), lambda i: (i, 0)) for c, _ in row_outs]
    out_specs += [pl.BlockSpec(shape, lambda i: (0, 0)) for shape in acc_outs]
    out_shape = [jax.ShapeDtypeStruct((rows, c), dt) for c, dt in row_outs]
    out_shape += [jax.ShapeDtypeStruct(shape, F32) for shape in acc_outs]
    n_in, n_row = len(ins), len(row_outs)

    def body(*refs):
        vals = [r[...] for r in refs[:n_in]]
        outs = fn(*vals)
        if not isinstance(outs, (tuple, list)):
            outs = (outs,)
        out_refs = refs[n_in:]
        for o_ref, val in zip(out_refs[:n_row], outs[:n_row]):
            o_ref[...] = val.astype(o_ref.dtype)
        first = pl.program_id(0) == 0
        for o_ref, val in zip(out_refs[n_row:], outs[n_row:]):
            @pl.when(first)
            def _(o_ref=o_ref):
                o_ref[...] = jnp.zeros_like(o_ref)
            o_ref[...] += val

    res = pl.pallas_call(
        body, name=name,
        grid=(rows // tb,),
        in_specs=in_specs, out_specs=out_specs, out_shape=out_shape,
        compiler_params=_cparams("arbitrary"),
    )(*args)
    return res


def _rms_fwd(x, g, name):
    def fn(xv, gv):
        r = lax.rsqrt(jnp.mean(xv * xv, axis=-1, keepdims=True) + NORM_EPS)
        return (xv * r * gv,)
    return _rowwise(fn, [("row", x), ("full", g)], [(x.shape[1], BF16)], [], rows=x.shape[0], tb=_tile(x.shape[0], 512), name=name)[0]


def _rms_bwd(x, g, dh, dres, name, want_bf16):
    d = x.shape[1]

    def fn(xv, gv, dhv, drv):
        r = lax.rsqrt(jnp.mean(xv * xv, axis=-1, keepdims=True) + NORM_EPS)
        xhat = xv * r
        dhv = dhv.astype(F32)
        dxhat = dhv * gv
        dx = drv + r * (dxhat - xhat * jnp.mean(dxhat * xhat, axis=-1, keepdims=True))
        dg = jnp.sum(dhv * xhat, axis=0, keepdims=True)
        return (dx, dx, dg) if want_bf16 else (dx, dg)

    row_outs = [(d, F32), (d, BF16)] if want_bf16 else [(d, F32)]
    return _rowwise(fn, [("row", x), ("full", g), ("row", dh), ("row", dres)], row_outs, [(1, d)],
                    rows=x.shape[0], tb=_tile(x.shape[0], 256), name=name)


def _loss_head(x2, g, target, name):
    d = x2.shape[1]

    def fn(xv, gv, tv):
        r = lax.rsqrt(jnp.mean(xv * xv, axis=-1, keepdims=True) + NORM_EPS)
        xhat = xv * r
        diff = xhat * gv - tv
        loss = 0.5 * jnp.sum(jnp.mean(diff * diff, axis=-1, keepdims=True), axis=0, keepdims=True)
        dy = diff * (1.0 / d)
        dxhat = dy * gv
        dx = r * (dxhat - xhat * jnp.mean(dxhat * xhat, axis=-1, keepdims=True))
        dg = jnp.sum(dy * xhat, axis=0, keepdims=True)
        return dx, dx, dg, jnp.broadcast_to(loss, (1, LANE))

    return _rowwise(fn, [("row", x2), ("full", g), ("row", target)], [(d, F32), (d, BF16)], [(1, d), (1, LANE)],
                    rows=x2.shape[0], tb=_tile(x2.shape[0], 256), name=name)


def _swiglu_fwd(gate, up, name):
    def fn(gv, uv):
        gv = gv.astype(F32)
        return (gv * _sigmoid(gv) * uv.astype(F32),)
    return _rowwise(fn, [("row", gate), ("row", up)], [(gate.shape[1], BF16)], [], rows=gate.shape[0],
                    tb=_tile(gate.shape[0], 256), name=name)[0]


def _swiglu_bwd(dact, gate, up, name):
    def fn(dv, gv, uv):
        dv, gv, uv = dv.astype(F32), gv.astype(F32), uv.astype(F32)
        s = _sigmoid(gv)
        dgate = dv * uv * (s * (1.0 + gv * (1.0 - s)))
        dup = dv * (gv * s)
        return dgate, dup
    c = gate.shape[1]
    return _rowwise(fn, [("row", dact), ("row", gate), ("row", up)], [(c, BF16), (c, BF16)], [], rows=gate.shape[0],
                    tb=_tile(gate.shape[0], 256), name=name)


def _mix_fwd(proj, pa, pb, d, name):
    def fn(ga, gb, av, bv):
        return (_sigmoid(ga.astype(F32)) * av.astype(F32) + _sigmoid(gb.astype(F32)) * bv.astype(F32),)
    return _rowwise(fn, [("win", proj, d, 0), ("win", proj, d, 1), ("row", pa), ("row", pb)], [(d, BF16)], [],
                    rows=pa.shape[0], tb=_tile(pa.shape[0], 512), name=name)[0]


def _mix_bwd(proj, pa, pb, dmix, d, name):
    def fn(ga, gb, av, bv, dm):
        dm = dm.astype(F32)
        sa, sb = _sigmoid(ga.astype(F32)), _sigmoid(gb.astype(F32))
        av, bv = av.astype(F32), bv.astype(F32)
        return dm * sa, dm * sb, dm * av * sa * (1.0 - sa), dm * bv * sb * (1.0 - sb)
    return _rowwise(fn, [("win", proj, d, 0), ("win", proj, d, 1), ("row", pa), ("row", pb), ("row", dmix)],
                    [(d, BF16)] * 4, [], rows=pa.shape[0], tb=_tile(pa.shape[0], 512), name=name)


def _chunk_masks(tb):
    r = lax.broadcasted_iota(jnp.int32, (tb, tb), 0)
    c = lax.broadcasted_iota(jnp.int32, (tb, tb), 1)
    same = lax.shift_right_logical(r, GLA_CHUNK_SHIFT) == lax.shift_right_logical(c, GLA_CHUNK_SHIFT)
    return same, same & (c <= r), same & (r <= c)


def _mask_bf16(mask):
    return jnp.where(mask, 1.0, 0.0).astype(BF16)


def _split_dot(mask_bf, x, terms):
    acc, rem = None, x
    for _ in range(terms):
        hi = rem.astype(BF16)
        part = jnp.dot(mask_bf, hi, preferred_element_type=F32)
        acc = part if acc is None else acc + part
        rem = rem - hi.astype(F32)
    return acc


def _gla_decay(al, wa2, ba2, same_bf, causal_bf):
    z = jnp.dot(al.astype(BF16), wa2, preferred_element_type=F32) + ba2
    la = (jnp.minimum(z, 0.0) - jnp.log(1.0 + jnp.exp(-jnp.abs(z)))) * (1.0 / GLA_TAU)
    bc = _split_dot(causal_bf, la, 3)
    bl = _split_dot(same_bf, la, 3)
    return z, bc, bl


def _dot_t(a, b, ca, cb):
    return lax.dot_general(a, b, (((ca,), (cb,)), ((), ())), preferred_element_type=F32)


def _gla_fwd(proj, alow, wa2, ba2, ghn, *, dk, dv, name):
    t = proj.shape[0]
    tb = min(GLA_BLOCK, t)
    nch = tb // GLA_CHUNK
    hk, hv = dk // GLA_HEADS, dv // GLA_HEADS
    scale = hk ** -0.5
    v_cb, r_cb = (8 * dk) // dv, (8 * dk) // dv + 1
    q_cb, k_cb = (8 * dk + 2 * dv) // dk, (8 * dk + 2 * dv) // dk + 1

    def body(q_ref, k_ref, v_ref, r_ref, al_ref, wa2_ref, ba2_ref, ghn_ref, oa_ref, opre_ref, s_ref, st_scr):
        @pl.when(pl.program_id(0) == 0)
        def _():
            st_scr[...] = jnp.zeros_like(st_scr)

        same, causal, _ = _chunk_masks(tb)
        same_bf, causal_bf = _mask_bf16(same), _mask_bf16(causal)
        _, bc, bl = _gla_decay(al_ref[...], wa2_ref[...], ba2_ref[...], same_bf, causal_bf)
        q = q_ref[...].astype(F32) * scale
        k = k_ref[...].astype(F32)
        qd = (q * jnp.exp(bc)).astype(BF16)
        ki = (k * jnp.exp(-bc)).astype(BF16)
        ks = (k * jnp.exp(bl - bc)).astype(BF16)
        dl = jnp.exp(bl)
        for h in range(GLA_HEADS):
            ksl = slice(h * hk, (h + 1) * hk)
            vsl = slice(h * hv, (h + 1) * hv)
            v_h = v_ref[:, vsl]
            sc = jnp.where(causal, _dot_t(qd[:, ksl], ki[:, ksl], 1, 1), 0.0)
            o_intra = jnp.dot(sc.astype(BF16), v_h, preferred_element_type=F32)
            for c in range(nch):
                rows = slice(c * GLA_CHUNK, (c + 1) * GLA_CHUNK)
                st = st_scr[h]
                s_ref[c, h] = st
                o_c = o_intra[rows] + _dot_t(qd[rows, ksl], st.astype(BF16), 1, 1)
                opre_ref[rows, vsl] = o_c
                st_scr[h] = dl[c * GLA_CHUNK:c * GLA_CHUNK + 1, ksl] * st + _dot_t(v_h[rows], ks[rows, ksl], 0, 0)
        for h in range(GLA_HEADS):
            vsl = slice(h * hv, (h + 1) * hv)
            o = opre_ref[:, vsl]
            rs = lax.rsqrt(jnp.mean(o * o, axis=-1, keepdims=True) + NORM_EPS)
            rv = r_ref[:, vsl].astype(F32)
            oa_ref[:, vsl] = (rv * _sigmoid(rv) * (o * rs * ghn_ref[:, vsl])).astype(BF16)

    nchunks = t // GLA_CHUNK
    return pl.pallas_call(
        body, name=name,
        grid=(t // tb,),
        in_specs=[
            pl.BlockSpec((tb, dk), lambda i: (i, q_cb)),
            pl.BlockSpec((tb, dk), lambda i: (i, k_cb)),
            pl.BlockSpec((tb, dv), lambda i: (i, v_cb)),
            pl.BlockSpec((tb, dv), lambda i: (i, r_cb)),
            pl.BlockSpec((tb, LANE), lambda i: (i, 0)),
            pl.BlockSpec(wa2.shape, lambda i: (0, 0)),
            pl.BlockSpec(ba2.shape, lambda i: (0, 0)),
            pl.BlockSpec(ghn.shape, lambda i: (0, 0)),
        ],
        out_specs=[
            pl.BlockSpec((tb, dv), lambda i: (i, 0)),
            pl.BlockSpec((tb, dv), lambda i: (i, 0)),
            pl.BlockSpec((nch, GLA_HEADS, hv, hk), lambda i: (i, 0, 0, 0)),
        ],
        out_shape=[
            jax.ShapeDtypeStruct((t, dv), BF16),
            jax.ShapeDtypeStruct((t, dv), F32),
            jax.ShapeDtypeStruct((nchunks, GLA_HEADS, hv, hk), F32),
        ],
        scratch_shapes=[pltpu.VMEM((GLA_HEADS, hv, hk), F32)],
        compiler_params=_cparams("arbitrary"),
    )(proj, proj, proj, proj, alow, wa2, ba2, ghn)


def _gla_bwd(proj, alow, wa2, ba2, ghn, opre, states, doa, *, dk, dv, name):
    t = proj.shape[0]
    tb = min(GLA_BLOCK, t)
    nb = t // tb
    nch = tb // GLA_CHUNK
    hk, hv = dk // GLA_HEADS, dv // GLA_HEADS
    scale = hk ** -0.5
    v_cb, r_cb = (8 * dk) // dv, (8 * dk) // dv + 1
    q_cb, k_cb = (8 * dk + 2 * dv) // dk, (8 * dk + 2 * dv) // dk + 1

    def body(q_ref, k_ref, v_ref, r_ref, al_ref, wa2_ref, ba2_ref, ghn_ref, opre_ref, s_ref, doa_ref,
             dq_ref, dk_ref, dv_ref, dr_ref, dal_ref, dwa2_ref, dba2_ref, dghn_ref,
             dst_scr, dqd_scr, dki_scr, dks_scr, ddl_scr):
        @pl.when(pl.program_id(0) == 0)
        def _():
            dst_scr[...] = jnp.zeros_like(dst_scr)
            dwa2_ref[...] = jnp.zeros_like(dwa2_ref)
            dba2_ref[...] = jnp.zeros_like(dba2_ref)
            dghn_ref[...] = jnp.zeros_like(dghn_ref)

        same, causal, anti = _chunk_masks(tb)
        same_bf, causal_bf, anti_bf = _mask_bf16(same), _mask_bf16(causal), _mask_bf16(anti)
        al = al_ref[...]
        wa2v = wa2_ref[...]
        z, bc, bl = _gla_decay(al, wa2v, ba2_ref[...], same_bf, causal_bf)
        e_bc, e_nbc, e_st = jnp.exp(bc), jnp.exp(-bc), jnp.exp(bl - bc)
        q = q_ref[...].astype(F32) * scale
        k = k_ref[...].astype(F32)
        qd_f, ki_f, ks_f = q * e_bc, k * e_nbc, k * e_st
        qd, ki, ks = qd_f.astype(BF16), ki_f.astype(BF16), ks_f.astype(BF16)
        dl = jnp.exp(bl)
        for h in range(GLA_HEADS):
            ksl = slice(h * hk, (h + 1) * hk)
            vsl = slice(h * hv, (h + 1) * hv)
            o = opre_ref[:, vsl]
            rs = lax.rsqrt(jnp.mean(o * o, axis=-1, keepdims=True) + NORM_EPS)
            ohat = o * rs
            g_h = ghn_ref[:, vsl]
            rv = r_ref[:, vsl].astype(F32)
            sg = _sigmoid(rv)
            d_oa = doa_ref[:, vsl].astype(F32)
            don = d_oa * (rv * sg)
            dr_ref[:, vsl] = (d_oa * (ohat * g_h) * (sg * (1.0 + rv * (1.0 - sg)))).astype(BF16)
            dghn_ref[:, vsl] += jnp.sum(don * ohat, axis=0, keepdims=True)
            dohat = don * g_h
            do_f = rs * (dohat - ohat * jnp.mean(dohat * ohat, axis=-1, keepdims=True))
            do = do_f.astype(BF16)
            v_h = v_ref[:, vsl]
            p = jnp.where(causal, _dot_t(do, v_h, 1, 1), 0.0).astype(BF16)
            dqd_intra = jnp.dot(p, ki[:, ksl], preferred_element_type=F32)
            dki_scr[:, ksl] = _dot_t(p, qd[:, ksl], 0, 0)
            sc = jnp.where(causal, _dot_t(qd[:, ksl], ki[:, ksl], 1, 1), 0.0).astype(BF16)
            dv_intra = _dot_t(sc, do, 0, 0)
            for c in reversed(range(nch)):
                rows = slice(c * GLA_CHUNK, (c + 1) * GLA_CHUNK)
                dst = dst_scr[h]
                st = s_ref[c, h]
                dst_bf = dst.astype(BF16)
                dv_ref[rows, vsl] = (dv_intra[rows] + _dot_t(ks[rows, ksl], dst_bf, 1, 1)).astype(BF16)
                dks_scr[rows, ksl] = jnp.dot(v_h[rows], dst_bf, preferred_element_type=F32)
                dl_c = dl[c * GLA_CHUNK:c * GLA_CHUNK + 1, ksl]
                ddl = jnp.sum(dst * st, axis=0, keepdims=True) * dl_c
                ddl_scr[rows, ksl] = jnp.broadcast_to(ddl, (GLA_CHUNK, hk))
                dqd_scr[rows, ksl] = dqd_intra[rows] + jnp.dot(do[rows], st.astype(BF16), preferred_element_type=F32)
                dst_scr[h] = dl_c * dst + _dot_t(do[rows], qd[rows, ksl], 0, 0)
        dqd, dki, dks = dqd_scr[...], dki_scr[...], dks_scr[...]
        dq_ref[...] = (dqd * (scale * e_bc)).astype(BF16)
        dk_ref[...] = (dki * e_nbc + dks * e_st).astype(BF16)
        dks_ks = dks * ks_f
        dbc = dqd * qd_f - dki * ki_f - dks_ks
        dla = _split_dot(anti_bf, dbc, 2) + _split_dot(same_bf, dks_ks, 2) + ddl_scr[...]
        dz = (dla * (1.0 / GLA_TAU) * (1.0 - _sigmoid(z)))
        dz_bf = dz.astype(BF16)
        dal_ref[...] = _dot_t(dz_bf, wa2v, 1, 1).astype(BF16)
        dwa2_ref[...] += _dot_t(al.astype(BF16), dz_bf, 0, 0)
        dba2_ref[...] += jnp.sum(dz, axis=0, keepdims=True)

    rev = lambda i: nb - 1 - i
    return pl.pallas_call(
        body, name=name,
        grid=(nb,),
        in_specs=[
            pl.BlockSpec((tb, dk), lambda i: (rev(i), q_cb)),
            pl.BlockSpec((tb, dk), lambda i: (rev(i), k_cb)),
            pl.BlockSpec((tb, dv), lambda i: (rev(i), v_cb)),
            pl.BlockSpec((tb, dv), lambda i: (rev(i), r_cb)),
            pl.BlockSpec((tb, LANE), lambda i: (rev(i), 0)),
            pl.BlockSpec(wa2.shape, lambda i: (0, 0)),
            pl.BlockSpec(ba2.shape, lambda i: (0, 0)),
            pl.BlockSpec(ghn.shape, lambda i: (0, 0)),
            pl.BlockSpec((tb, dv), lambda i: (rev(i), 0)),
            pl.BlockSpec((nch, GLA_HEADS, hv, hk), lambda i: (rev(i), 0, 0, 0)),
            pl.BlockSpec((tb, dv), lambda i: (rev(i), 0)),
        ],
        out_specs=[
            pl.BlockSpec((tb, dk), lambda i: (rev(i), 0)),
            pl.BlockSpec((tb, dk), lambda i: (rev(i), 0)),
            pl.BlockSpec((tb, dv), lambda i: (rev(i), 0)),
            pl.BlockSpec((tb, dv), lambda i: (rev(i), 0)),
            pl.BlockSpec((tb, LANE), lambda i: (rev(i), 0)),
            pl.BlockSpec(wa2.shape, lambda i: (0, 0)),
            pl.BlockSpec(ba2.shape, lambda i: (0, 0)),
            pl.BlockSpec(ghn.shape, lambda i: (0, 0)),
        ],
        out_shape=[
            jax.ShapeDtypeStruct((t, dk), BF16),
            jax.ShapeDtypeStruct((t, dk), BF16),
            jax.ShapeDtypeStruct((t, dv), BF16),
            jax.ShapeDtypeStruct((t, dv), BF16),
            jax.ShapeDtypeStruct((t, LANE), BF16),
            jax.ShapeDtypeStruct(wa2.shape, F32),
            jax.ShapeDtypeStruct(ba2.shape, F32),
            jax.ShapeDtypeStruct(ghn.shape, F32),
        ],
        scratch_shapes=[pltpu.VMEM((GLA_HEADS, hv, hk), F32)] + [pltpu.VMEM((tb, dk), F32)] * 4,
        compiler_params=_cparams("arbitrary"),
    )(proj, proj, proj, proj, alow, wa2, ba2, ghn, opre, states, doa)


def _s5_tables(lam_re, lam_im, log_dt, b_re, b_im, c_re, c_im):
    hp = lax.Precision.HIGHEST
    g, p = lam_re.shape
    ln = S5_L
    dt = jnp.exp(log_dt)[:, None]
    lr, li = lam_re, lam_im
    mag = jnp.exp(lr * dt)
    ar, ai = mag * jnp.cos(li * dt), mag * jnp.sin(li * dt)
    den = lr * lr + li * li
    am1 = ar - 1.0
    f_re = ((am1 * lr + ai * li) / den)[..., None]
    f_im = ((ai * lr - am1 * li) / den)[..., None]
    bb_re = f_re * b_re - f_im * b_im
    bb_im = f_re * b_im + f_im * b_re
    j = jnp.arange(ln + 1, dtype=F32)[None, :, None]
    pm = jnp.exp(j * (lr * dt)[:, None, :])
    ang = j * (li * dt)[:, None, :]
    pw_re, pw_im = pm * jnp.cos(ang), pm * jnp.sin(ang)
    cp_re = c_re[:, None] * pw_re[:, :, None, :] - c_im[:, None] * pw_im[:, :, None, :]
    cp_im = c_re[:, None] * pw_im[:, :, None, :] + c_im[:, None] * pw_re[:, :, None, :]
    kj = (jnp.einsum("gjcp,gpd->gjcd", cp_re[:, :ln], bb_re, precision=hp)
          - jnp.einsum("gjcp,gpd->gjcd", cp_im[:, :ln], bb_im, precision=hp))
    s_i = jnp.arange(ln)[None, :, None]
    t_i = jnp.arange(ln)[None, None, :]
    j_i = jnp.arange(ln)[:, None, None]
    shift = (t_i - s_i == j_i).astype(F32)
    m = jnp.einsum("jst,gjcd->gsdtc", shift, kj, precision=hp).reshape(g, ln * S5_GC, ln * S5_GC)
    rp_re, rp_im = pw_re[:, ln - 1::-1][:, :ln], pw_im[:, ln - 1::-1][:, :ln]
    bst_re = rp_re[:, :, None, :] * bb_re.transpose(0, 2, 1)[:, None] - rp_im[:, :, None, :] * bb_im.transpose(0, 2, 1)[:, None]
    bst_im = rp_re[:, :, None, :] * bb_im.transpose(0, 2, 1)[:, None] + rp_im[:, :, None, :] * bb_re.transpose(0, 2, 1)[:, None]
    bst = jnp.concatenate([bst_re, bst_im], axis=-1).reshape(g, ln * S5_GC, 2 * p)
    cst = jnp.concatenate([cp_re[:, 1:].transpose(0, 3, 1, 2), -cp_im[:, 1:].transpose(0, 3, 1, 2)], axis=1)
    cst = cst.reshape(g, 2 * p, ln * S5_GC)
    a = jnp.stack([jnp.concatenate([pw_re[:, ln], pw_re[:, ln]], axis=-1),
                   jnp.concatenate([-pw_im[:, ln], pw_im[:, ln]], axis=-1)], axis=1)
    return m, bst, cst, a


def _state_scan(v, pr, pi, reverse):
    n = v.shape[0]
    half = v.shape[1] // 2
    row = lax.broadcasted_iota(jnp.int32, v.shape, 0)
    z, s = v, 1
    while s < n:
        if reverse:
            zs = jnp.where(row < n - s, pltpu.roll(z, n - s, 0), 0.0)
        else:
            zs = jnp.where(row >= s, pltpu.roll(z, s, 0), 0.0)
        z = z + zs * pr + pltpu.roll(zs, half, 1) * pi
        pr, pi = pr * pr - pi * pi, 2.0 * pr * pi
        s *= 2
    return z


def _s5_core_fwd(u_g, m, bst, cst, a, name):
    g, nc, w = u_g.shape
    p2 = bst.shape[2]

    def body(u_ref, m_ref, b_ref, c_ref, a_ref, y_ref, x_ref):
        u = u_ref[0]
        v = jnp.dot(u, b_ref[0], preferred_element_type=F32)
        z = _state_scan(v, a_ref[0, 0:1, :], a_ref[0, 1:2, :], reverse=False)
        row = lax.broadcasted_iota(jnp.int32, z.shape, 0)
        x = jnp.where(row >= 1, pltpu.roll(z, 1, 0), 0.0)
        x_ref[0] = x
        y_ref[0] = (jnp.dot(u, m_ref[0], preferred_element_type=F32)
                    + jnp.dot(x.astype(BF16), c_ref[0], preferred_element_type=F32))

    per_g = lambda shape: pl.BlockSpec((1,) + shape, lambda i: (i, 0, 0))
    return pl.pallas_call(
        body, name=name, grid=(g,),
        in_specs=[per_g((nc, w)), per_g((w, w)), per_g((w, p2)), per_g((p2, w)), per_g((2, p2))],
        out_specs=[per_g((nc, w)), per_g((nc, p2))],
        out_shape=[jax.ShapeDtypeStruct((g, nc, w), F32), jax.ShapeDtypeStruct((g, nc, p2), F32)],
        compiler_params=_cparams("parallel"),
    )(u_g, m, bst, cst, a)


def _s5_core_bwd(dy_g, u_g, x_g, m, bst, cst, a, name):
    g, nc, w = u_g.shape
    p2 = bst.shape[2]

    def body(dy_ref, u_ref, x_ref, m_ref, b_ref, c_ref, a_ref, du_ref, dm_ref, db_ref, dc_ref, da_ref):
        dy, u, x = dy_ref[0], u_ref[0], x_ref[0]
        gx = _dot_t(dy, c_ref[0], 1, 1)
        rtot = _state_scan(gx, a_ref[0, 0:1, :], -a_ref[0, 1:2, :], reverse=True)
        row = lax.broadcasted_iota(jnp.int32, rtot.shape, 0)
        dv = jnp.where(row < nc - 1, pltpu.roll(rtot, nc - 1, 0), 0.0)
        dv_bf = dv.astype(BF16)
        du_ref[0] = _dot_t(dy, m_ref[0], 1, 1) + _dot_t(dv_bf, b_ref[0], 1, 1)
        dm_ref[0] = _dot_t(u, dy, 0, 0)
        dc_ref[0] = _dot_t(x.astype(BF16), dy, 0, 0)
        db_ref[0] = _dot_t(u, dv_bf, 0, 0)
        x_sw = pltpu.roll(x, p2 // 2, 1)
        da_ref[0, 0:1, :] = jnp.sum(dv * x, axis=0, keepdims=True)
        da_ref[0, 1:2, :] = jnp.sum(dv * x_sw, axis=0, keepdims=True)

    per_g = lambda shape: pl.BlockSpec((1,) + shape, lambda i: (i, 0, 0))
    return pl.pallas_call(
        body, name=name, grid=(g,),
        in_specs=[per_g((nc, w)), per_g((nc, w)), per_g((nc, p2)), per_g((w, w)), per_g((w, p2)), per_g((p2, w)), per_g((2, p2))],
        out_specs=[per_g((nc, w)), per_g((w, w)), per_g((w, p2)), per_g((p2, w)), per_g((2, p2))],
        out_shape=[jax.ShapeDtypeStruct((g, nc, w), F32), jax.ShapeDtypeStruct((g, w, w), F32),
                   jax.ShapeDtypeStruct((g, w, p2), F32), jax.ShapeDtypeStruct((g, p2, w), F32),
                   jax.ShapeDtypeStruct((g, 2, p2), F32)],
        compiler_params=_cparams("parallel"),
    )(dy_g, u_g, x_g, m, bst, cst, a)


def _gelu_parts(y):
    inner = GELU_C * (y + GELU_A * y * y * y)
    th = jnp.tanh(inner)
    return th, 0.5 * y * (1.0 + th)


def _s5_post_fwd(y_raw, proj, u_cb, s5d, wglu, bglu, name):
    w = y_raw.shape[1]

    def fn(yr, u, dsk, wg, bg):
        y = yr + dsk * u.astype(F32)
        _, h = _gelu_parts(y)
        gl = jnp.dot(h.astype(BF16), wg, preferred_element_type=F32) + bg
        return (h * _sigmoid(gl),)

    return _rowwise(fn, [("row", y_raw), ("win", proj, w, u_cb), ("full", s5d), ("full", wglu), ("full", bglu)],
                    [(w, BF16)], [], rows=y_raw.shape[0], tb=_tile(y_raw.shape[0], 512), name=name)[0]


def _s5_post_bwd(y_raw, proj, u_cb, s5d, wglu, bglu, dob, name):
    w = y_raw.shape[1]

    def fn(yr, u, dsk, wg, bg, dov):
        u = u.astype(F32)
        dov = dov.astype(F32)
        y = yr + dsk * u
        th, h = _gelu_parts(y)
        h_bf = h.astype(BF16)
        gl = jnp.dot(h_bf, wg, preferred_element_type=F32) + bg
        sg = _sigmoid(gl)
        dgl = dov * h * sg * (1.0 - sg)
        dgl_bf = dgl.astype(BF16)
        dh = dov * sg + _dot_t(dgl_bf, wg, 1, 1)
        dgelu = 0.5 * (1.0 + th) + 0.5 * y * (1.0 - th * th) * GELU_C * (1.0 + 3.0 * GELU_A * y * y)
        dy = dh * dgelu
        return (dy, dy * dsk,
                _dot_t(h_bf, dgl_bf, 0, 0), jnp.sum(dgl, axis=0, keepdims=True), jnp.sum(dy * u, axis=0, keepdims=True))

    return _rowwise(fn, [("row", y_raw), ("win", proj, w, u_cb), ("full", s5d), ("full", wglu), ("full", bglu), ("row", dob)],
                    [(w, BF16), (w, F32)], [(w, w), (1, w), (1, w)], rows=y_raw.shape[0], tb=_tile(y_raw.shape[0], 512), name=name)


def _to_groups(a, dtype):
    t, w = a.shape
    g = w // S5_GC
    return a.reshape(t // S5_L, S5_L, g, S5_GC).transpose(2, 0, 1, 3).reshape(g, t // S5_L, S5_L * S5_GC).astype(dtype)


def _from_groups(a):
    g, nc, _ = a.shape
    return a.reshape(g, nc, S5_L, S5_GC).transpose(1, 2, 0, 3).reshape(nc * S5_L, g * S5_GC)


def _adamw(w, g, m, v, name):
    rows, cols = w.shape
    tb = _tile(rows, 256, align=8)
    c1 = 1.0 - ADAM_B1 ** ADAM_STEP
    c2 = 1.0 - ADAM_B2 ** ADAM_STEP

    def body(w_ref, g_ref, m_ref, v_ref, d_ref, nm_ref, nv_ref):
        gv = g_ref[...]
        nm = ADAM_B1 * m_ref[...] + (1.0 - ADAM_B1) * gv
        nv = ADAM_B2 * v_ref[...] + (1.0 - ADAM_B2) * (gv * gv)
        d_ref[...] = -ADAM_LR * ((nm / c1) / (jnp.sqrt(nv / c2) + ADAM_EPS) + ADAM_WD * w_ref[...])
        nm_ref[...] = nm
        nv_ref[...] = nv

    spec = pl.BlockSpec((tb, cols), lambda i: (i, 0))
    return pl.pallas_call(
        body, name=name, grid=(rows // tb,),
        in_specs=[spec] * 4, out_specs=[spec] * 3,
        out_shape=[jax.ShapeDtypeStruct((rows, cols), F32)] * 3,
        compiler_params=_cparams("parallel"),
    )(w, g, m, v)


def _slot_sum(x, name):
    _, rows, cols = x.shape
    tb = _tile(rows, 512, align=8)

    def body(x_ref, o_ref):
        acc = x_ref[0].astype(F32)
        for s in range(1, N_DEV):
            acc = acc + x_ref[s].astype(F32)
        o_ref[...] = acc

    return pl.pallas_call(
        body, name=name, grid=(rows // tb,),
        in_specs=[pl.BlockSpec((N_DEV, tb, cols), lambda i: (0, i, 0))],
        out_specs=pl.BlockSpec((tb, cols), lambda i: (i, 0)),
        out_shape=jax.ShapeDtypeStruct((rows, cols), F32),
        compiler_params=_cparams("parallel"),
    )(x)


_HBM_SPEC = pl.BlockSpec(memory_space=pltpu.HBM)


def _all_gather(xs, name):
    rows, cols = xs.shape

    def body(x_ref, out_ref, send_sems, recv_sems, local_sem):
        x, y, c = lax.axis_index("x"), lax.axis_index("y"), lax.axis_index("c")
        me, sibling = (x, y, c), (x, y, 1 - c)
        chips = [(1 - x, y), (x, 1 - y), (1 - x, 1 - y)]

        def slot(px, py, pc):
            return out_ref.at[4 * px + 2 * py + pc]

        def copy(k, block, to, src=None):
            return pltpu.make_async_remote_copy(
                src_ref=slot(*block) if src is None else src, dst_ref=slot(*block),
                send_sem=send_sems.at[k], recv_sem=recv_sems.at[k], device_id=to, device_id_type=MESH)

        mine = pltpu.make_async_copy(x_ref, slot(*me), local_sem)
        mine.start()
        first = [copy(0, me, sibling, src=x_ref)]
        first += [copy(1 + j, me, (*chip, c), src=x_ref) for j, chip in enumerate(chips)]
        for cp in first:
            cp.start()
        passed = [copy(4 + j, (*chip, c), sibling) for j, chip in enumerate(chips)]
        for j, chip in enumerate(chips):
            copy(1 + j, (*chip, c), me).wait_recv()
            passed[j].start()
        copy(0, sibling, me).wait_recv()
        for j, chip in enumerate(chips):
            copy(4 + j, (*chip, 1 - c), me).wait_recv()
        for cp in first + passed:
            cp.wait_send()
        mine.wait()

    return pl.pallas_call(
        body, name=name,
        out_shape=jax.ShapeDtypeStruct((N_DEV, rows, cols), xs.dtype),
        in_specs=[_HBM_SPEC], out_specs=_HBM_SPEC,
        scratch_shapes=[pltpu.SemaphoreType.DMA((7,)), pltpu.SemaphoreType.DMA((7,)), pltpu.SemaphoreType.DMA],
    )(xs)


def _all_to_all(xs, name):
    _, rows, cols = xs.shape

    def body(x_ref, out_ref, send_sems, recv_sems, local_sem):
        x, y, c = lax.axis_index("x"), lax.axis_index("y"), lax.axis_index("c")
        my = 4 * x + 2 * y + c

        def flip(v, bit):
            return 1 - v if bit else v

        mine = pltpu.make_async_copy(x_ref.at[my], out_ref.at[my], local_sem)
        mine.start()
        copies = []
        for k in range(1, N_DEV):
            px, py, pc = flip(x, k & 4), flip(y, k & 2), flip(c, k & 1)
            peer = 4 * px + 2 * py + pc
            copies.append(pltpu.make_async_remote_copy(
                src_ref=x_ref.at[peer], dst_ref=out_ref.at[my],
                send_sem=send_sems.at[k - 1], recv_sem=recv_sems.at[k - 1], device_id=(px, py, pc), device_id_type=MESH))
        for cp in copies:
            cp.start()
        for k in range(1, N_DEV):
            px, py, pc = flip(x, k & 4), flip(y, k & 2), flip(c, k & 1)
            peer = 4 * px + 2 * py + pc
            pltpu.make_async_remote_copy(
                src_ref=x_ref.at[peer], dst_ref=out_ref.at[peer],
                send_sem=send_sems.at[k - 1], recv_sem=recv_sems.at[k - 1], device_id=(px, py, pc), device_id_type=MESH).wait_recv()
        for cp in copies:
            cp.wait_send()
        mine.wait()

    return pl.pallas_call(
        body, name=name,
        out_shape=jax.ShapeDtypeStruct(xs.shape, xs.dtype),
        in_specs=[_HBM_SPEC], out_specs=_HBM_SPEC,
        scratch_shapes=[pltpu.SemaphoreType.DMA((7,)), pltpu.SemaphoreType.DMA((7,)), pltpu.SemaphoreType.DMA],
    )(xs)


def _pack_rows(n):
    rows = -(-n // PACK_COLS)
    return -(-rows // PACK_ROW_ALIGN) * PACK_ROW_ALIGN


def _pack_flat(parts, dtype):
    lead = parts[0].shape[:-1]
    flat = jnp.concatenate([p.astype(dtype) for p in parts], axis=-1)
    n = flat.shape[-1]
    rows = _pack_rows(n)
    flat = jnp.pad(flat, [(0, 0)] * len(lead) + [(0, rows * PACK_COLS - n)])
    return flat.reshape(lead + (rows, PACK_COLS))


def _shard_split(full, axis):
    r, c = full.shape
    if axis == 0:
        return full.reshape(N_DEV, (r // N_DEV) * c)
    return full.reshape(r, N_DEV, c // N_DEV).transpose(1, 0, 2).reshape(N_DEV, r * (c // N_DEV))


def _shard_join(parts, shard_shape, axis):
    r, c = shard_shape
    p = parts.reshape(N_DEV, r, c)
    if axis == 0:
        return p.reshape(N_DEV * r, c)
    return p.transpose(1, 0, 2).reshape(r, N_DEV * c)


def _local_step(x, target, w, small):
    t, d = x.shape
    dk, dv, s5w = d // 4, d // 2, d // 4
    dff = w["w_ffn_out"].shape[0]
    w_in = w["w_in"]
    o_q, o_k, o_v, o_r, o_al, o_u, o_ga, o_gb = 0, dk, 2 * dk, 2 * dk + dv, 2 * dk + 2 * dv, 2 * dk + 2 * dv + GLA_RANK, \
        2 * dk + 2 * dv + GLA_RANK + s5w, 2 * dk + 2 * dv + GLA_RANK + s5w + d
    cols = lambda a, o, n: a[:, o:o + n]
    w_main = jnp.concatenate([cols(w_in, o_ga, d), cols(w_in, o_gb, d), cols(w_in, o_v, dv), cols(w_in, o_r, dv),
                              cols(w_in, o_q, dk), cols(w_in, o_k, dk), cols(w_in, o_u, s5w)], axis=1)
    w_al = jnp.pad(cols(w_in, o_al, GLA_RANK), ((0, 0), (0, LANE - GLA_RANK)))
    wa2 = jnp.pad(w["w_a2"], ((0, LANE - GLA_RANK), (0, 0)))
    w_gate, w_up = w["w_ffn_in"][:, :dff], w["w_ffn_in"][:, dff:]
    u_cb = (2 * d + 2 * dv + 2 * dk) // s5w

    h1 = _rms_fwd(x, small["norm1_g"], "norm1_fwd")
    proj = _mm(h1, w_main, out_dtype=BF16, name="in_proj")
    alow = _mm(h1, w_al, out_dtype=BF16, name="in_proj_gate_rank")
    o_a, o_pre, states = _gla_fwd(proj, alow, wa2, small["b_a2"], small["gla_norm_g"], dk=dk, dv=dv, name="gla_fwd")

    s5_params = (small["lam_re"], small["lam_im"], small["log_dt"][0], small["s5_b_re"], small["s5_b_im"],
                 small["s5_c_re"], small["s5_c_im"])
    (tm, tbst, tcst, ta), tables_vjp = jax.vjp(_s5_tables, *s5_params)
    tm_bf, tbst_bf, tcst_bf = tm.astype(BF16), tbst.astype(BF16), tcst.astype(BF16)
    u_g = _to_groups(proj[:, u_cb * s5w:(u_cb + 1) * s5w], BF16)
    y_g, x_g = _s5_core_fwd(u_g, tm_bf, tbst_bf, tcst_bf, ta, "s5_core_fwd")
    y_raw = _from_groups(y_g)
    o_b = _s5_post_fwd(y_raw, proj, u_cb, small["s5_d"], w["w_glu"], small["b_glu"], "s5_post_fwd")

    pa = _mm(o_a, w["w_branch_a"], out_dtype=BF16, name="branch_a")
    pb = _mm(o_b, w["w_branch_b"], out_dtype=BF16, name="branch_b")
    mix = _mix_fwd(proj, pa, pb, d, "mix_fwd")
    x1 = _mm(mix, w["w_out"], res=x, name="out_proj")
    h2 = _rms_fwd(x1, small["norm2_g"], "norm2_fwd")
    gate = _mm(h2, w_gate, out_dtype=BF16, name="ffn_gate")
    up = _mm(h2, w_up, out_dtype=BF16, name="ffn_up")
    act = _swiglu_fwd(gate, up, "swiglu_fwd")
    x2 = _mm(act, w["w_ffn_out"], res=x1, name="ffn_out")
    dx2, dx2_bf, d_final_g, loss = _loss_head(x2, small["final_norm_g"], target, "loss_head")

    dact = _mm(dx2_bf, w["w_ffn_out"], tb=True, out_dtype=BF16, name="d_act")
    g_ffn_out = _mm(act, dx2_bf, ta=True, name="g_w_ffn_out")
    dgate, dup = _swiglu_bwd(dact, gate, up, "swiglu_bwd")
    g_gate = _mm(h2, dgate, ta=True, name="g_w_ffn_gate")
    g_up = _mm(h2, dup, ta=True, name="g_w_ffn_up")
    dh2 = _mm(dgate, w_gate, tb=True, name="d_h2_gate")
    dh2 = _mm(dup, w_up, tb=True, res=dh2, name="d_h2_up")
    dx1, dx1_bf, d_norm2_g = _rms_bwd(x1, small["norm2_g"], dh2, dx2, "norm2_bwd", True)
    dmix = _mm(dx1_bf, w["w_out"], tb=True, out_dtype=BF16, name="d_mix")
    g_out = _mm(mix, dx1_bf, ta=True, name="g_w_out")
    dpa, dpb, dga, dgb = _mix_bwd(proj, pa, pb, dmix, d, "mix_bwd")
    doa = _mm(dpa, w["w_branch_a"], tb=True, out_dtype=BF16, name="d_o_a")
    dob = _mm(dpb, w["w_branch_b"], tb=True, out_dtype=BF16, name="d_o_b")
    g_branch_a = _mm(o_a, dpa, ta=True, name="g_w_branch_a")
    g_branch_b = _mm(o_b, dpb, ta=True, name="g_w_branch_b")

    dy_s5, du_direct, g_glu, g_bglu, g_s5d = _s5_post_bwd(y_raw, proj, u_cb, small["s5_d"], w["w_glu"], small["b_glu"], dob, "s5_post_bwd")
    du_g, d_tm, d_tbst, d_tcst, d_ta = _s5_core_bwd(_to_groups(dy_s5, BF16), u_g, x_g, tm_bf, tbst_bf, tcst_bf, ta, "s5_core_bwd")
    g_lam_re, g_lam_im, g_log_dt, g_b_re, g_b_im, g_c_re, g_c_im = tables_vjp((d_tm, d_tbst, d_tcst, d_ta))
    du = (_from_groups(du_g) + du_direct).astype(BF16)

    dq, dkk, dvv, dr, dal, g_wa2, g_ba2, g_ghn = _gla_bwd(proj, alow, wa2, small["b_a2"], small["gla_norm_g"], o_pre, states, doa,
                                                        dk=dk, dv=dv, name="gla_bwd")
    dproj = jnp.concatenate([dga, dgb, dvv, dr, dq, dkk, du], axis=1)
    g_main = _mm(h1, dproj, ta=True, name="g_w_in_main")
    g_al = _mm(h1, dal, ta=True, name="g_w_in_gate_rank")
    dh1 = _mm(dproj, w_main, tb=True, name="d_h1_main")
    dh1 = _mm(dal, w_al, tb=True, res=dh1, name="d_h1_gate_rank")
    grad_x, d_norm1_g = _rms_bwd(x, small["norm1_g"], dh1, dx1, "norm1_bwd", False)

    mcols = lambda o, n: g_main[:, o:o + n]
    g_w_in = jnp.concatenate([mcols(2 * d + 2 * dv, dk), mcols(2 * d + 2 * dv + dk, dk), mcols(2 * d, dv), mcols(2 * d + dv, dv),
                              g_al[:, :GLA_RANK], mcols(2 * d + 2 * dv + 2 * dk, s5w), mcols(0, d), mcols(d, d)], axis=1)
    big = {
        "w_in": g_w_in, "w_a2": g_wa2[:GLA_RANK], "w_glu": g_glu, "w_branch_a": g_branch_a, "w_branch_b": g_branch_b,
        "w_out": g_out, "w_ffn_in": jnp.concatenate([g_gate, g_up], axis=1), "w_ffn_out": g_ffn_out,
    }
    small_g = {
        "norm1_g": d_norm1_g, "b_a2": g_ba2, "gla_norm_g": g_ghn, "lam_re": g_lam_re, "lam_im": g_lam_im,
        "log_dt": g_log_dt[None], "s5_b_re": g_b_re, "s5_b_im": g_b_im, "s5_c_re": g_c_re, "s5_c_im": g_c_im,
        "s5_d": g_s5d, "b_glu": g_bglu, "norm2_g": d_norm2_g, "final_norm_g": d_final_g,
    }
    return loss[0, 0], grad_x, big, small_g


_BIG = (("w_in", 1), ("w_a2", 1), ("w_glu", 0), ("w_branch_a", 1), ("w_branch_b", 1), ("w_out", 0), ("w_ffn_in", 1), ("w_ffn_out", 0))
_SMALL = ("norm1_g", "b_a2", "gla_norm_g", "lam_re", "lam_im", "log_dt", "s5_b_re", "s5_b_im", "s5_c_re", "s5_c_im",
          "s5_d", "b_glu", "norm2_g", "final_norm_g")
_ORDER = ("norm1_g", "w_in", "w_a2", "b_a2", "gla_norm_g", "lam_re", "lam_im", "log_dt", "s5_b_re", "s5_b_im", "s5_c_re",
          "s5_c_im", "s5_d", "w_glu", "b_glu", "w_branch_a", "w_branch_b", "w_out", "norm2_g", "w_ffn_in", "w_ffn_out", "final_norm_g")


def _small_2d(name, a):
    a = a[0]
    return a[None] if a.ndim == 1 else a


def kernel(x, norm1_g, w_in, w_a2, b_a2, gla_norm_g, lam_re, lam_im, log_dt, s5_b_re, s5_b_im, s5_c_re, s5_c_im, s5_d, w_glu, b_glu, w_branch_a, w_branch_b, w_out, norm2_g, w_ffn_in, w_ffn_out, final_norm_g, loss_target, m_norm1_g, m_w_in, m_w_a2, m_b_a2, m_gla_norm_g, m_lam_re, m_lam_im, m_log_dt, m_s5_b_re, m_s5_b_im, m_s5_c_re, m_s5_c_im, m_s5_d, m_w_glu, m_b_glu, m_w_branch_a, m_w_branch_b, m_w_out, m_norm2_g, m_w_ffn_in, m_w_ffn_out, m_final_norm_g, v_norm1_g, v_w_in, v_w_a2, v_b_a2, v_gla_norm_g, v_lam_re, v_lam_im, v_log_dt, v_s5_b_re, v_s5_b_im, v_s5_c_re, v_s5_c_im, v_s5_d, v_w_glu, v_b_glu, v_w_branch_a, v_w_branch_b, v_w_out, v_norm2_g, v_w_ffn_in, v_w_ffn_out, v_final_norm_g):
    args = dict(locals())
    weights = {n: args[n] for n in _ORDER}
    m_in = {n: args["m_" + n] for n in _ORDER}
    v_in = {n: args["v_" + n] for n in _ORDER}

    shards = {n: weights[n][0] for n, _ in _BIG}
    sizes = [shards[n].size for n, _ in _BIG]
    offs = [sum(sizes[:i]) for i in range(len(sizes))]
    packed = _pack_flat([shards[n].reshape(-1) for n, _ in _BIG], BF16)
    gathered = _all_gather(packed, "weights_all_gather").reshape(N_DEV, -1)
    full_w = {n: _shard_join(gathered[:, o:o + s], shards[n].shape, ax) for (n, ax), o, s in zip(_BIG, offs, sizes)}

    small = {n: _small_2d(n, weights[n]) for n in _SMALL}
    loss_local, grad_x, big_g, small_g = _local_step(x[0], loss_target[0], full_w, small)

    g_slices = _pack_flat([_shard_split(big_g[n], ax) for n, ax in _BIG], BF16)
    g_red = _slot_sum(_all_to_all(g_slices, "grads_all_to_all"), "grads_slot_sum").reshape(-1)
    grads = {n: g_red[o:o + s].reshape(shards[n].shape) for (n, _), o, s in zip(_BIG, offs, sizes)}

    s_sizes = [small_g[n].size for n in _SMALL]
    s_offs = [sum(s_sizes[:i]) for i in range(len(s_sizes))]
    s_rows = -(-(-(-(sum(s_sizes) + 1) // LANE)) // LANE) * LANE
    s_flat = jnp.concatenate([small_g[n].reshape(-1) for n in _SMALL] + [loss_local.reshape(1)])
    s_flat = jnp.pad(s_flat, (0, s_rows * LANE - s_flat.size)).reshape(s_rows, LANE)
    s_red = _slot_sum(_all_gather(s_flat, "small_grads_all_gather"), "small_grads_slot_sum").reshape(-1)
    loss = s_red[sum(s_sizes)]
    for n, o, s in zip(_SMALL, s_offs, s_sizes):
        grads[n] = s_red[o:o + s].reshape(weights[n].shape[1:])

    delta, new_m, new_v = {}, {}, {}
    for n, _ in _BIG:
        delta[n], new_m[n], new_v[n] = _adamw(weights[n][0], grads[n], m_in[n][0], v_in[n][0], "adamw_" + n)
    pack_small = lambda tree: jnp.pad(jnp.concatenate([tree[n].reshape(-1) for n in _SMALL]),
                                      (0, s_rows * LANE - sum(s_sizes))).reshape(s_rows, LANE)
    s_grad = jnp.pad(s_red[:sum(s_sizes)], (0, s_rows * LANE - sum(s_sizes))).reshape(s_rows, LANE)
    sd, sm, sv = _adamw(pack_small(weights), s_grad, pack_small(m_in), pack_small(v_in), "adamw_small")
    for n, o, s in zip(_SMALL, s_offs, s_sizes):
        shape = weights[n].shape[1:]
        delta[n], new_m[n], new_v[n] = (a.reshape(-1)[o:o + s].reshape(shape) for a in (sd, sm, sv))

    out = [loss, grad_x[None]]
    for tree in (grads, delta, new_m, new_v):
        out += [tree[n].reshape(weights[n].shape) for n in _ORDER]
    return tuple(out)
```

```python
import functools
import math

import jax
import jax.numpy as jnp
from jax import lax
from jax.experimental import pallas as pl
from jax.experimental.pallas import tpu as pltpu

F32 = jnp.float32
BF16 = jnp.bfloat16

NORM_EPS = 1e-6
N_DEV = 8
N_PEER = N_DEV - 1
GLA_HEADS = 4
GLA_CHUNK = 32
GLA_CHUNK_SHIFT = 5
GLA_TAU = 16.0
GLA_RANK = 16
GLA_BLOCK = 256
S5_GC = 16
S5_P = 64
S5_L = 32
LANE = 128
V7X_VMEM_LIMIT = 56 * 1024 * 1024
V7X_MM_VMEM_BUDGET = 40 * 1024 * 1024
V7X_MM_TILE_MN = 1408
V7X_MM_TILE_K = 2048

ADAM_LR = 0.001
ADAM_B1 = 0.9
ADAM_B2 = 0.999
ADAM_EPS = 1e-08
ADAM_WD = 0.01
ADAM_STEP = 10

GELU_C = math.sqrt(2.0 / math.pi)
GELU_A = 0.044715

MESH = pl.DeviceIdType.MESH


def _cparams(*sem):
    return pltpu.CompilerParams(dimension_semantics=sem, vmem_limit_bytes=V7X_VMEM_LIMIT)


def _divisors_down(n, start, align=LANE):
    t = (min(start, n) // align) * align
    found = False
    while t >= align:
        if n % t == 0:
            found = True
            yield t
        t -= align
    if not found:
        yield n


def _tile(n, target, align=LANE):
    return next(_divisors_down(n, target, align))


def _sigmoid(x):
    return 1.0 / (1.0 + jnp.exp(-x))


_HBM_SPEC = pl.BlockSpec(memory_space=pltpu.HBM)


def _exchange_scratch(n):
    return [pltpu.SemaphoreType.DMA((n * N_PEER,)), pltpu.SemaphoreType.DMA((n * N_PEER,)), pltpu.SemaphoreType.DMA((n,))]


def _ag_phases(x_refs, out_refs, send_sems, recv_sems, local_sems):
    n = len(x_refs)
    x, y, c = lax.axis_index("x"), lax.axis_index("y"), lax.axis_index("c")
    me, sibling = (x, y, c), (x, y, 1 - c)
    chips = [(1 - x, y), (x, 1 - y), (1 - x, 1 - y)]

    def copy(a, k, block, to, from_input=False):
        dst = out_refs[a].at[4 * block[0] + 2 * block[1] + block[2]]
        return pltpu.make_async_remote_copy(
            src_ref=x_refs[a] if from_input else dst, dst_ref=dst,
            send_sem=send_sems.at[a * N_PEER + k], recv_sem=recv_sems.at[a * N_PEER + k], device_id=to, device_id_type=MESH)

    def local(a):
        return pltpu.make_async_copy(x_refs[a], out_refs[a].at[4 * x + 2 * y + c], local_sems.at[a])

    def first(a):
        return [copy(a, 0, me, sibling, True)] + [copy(a, 1 + j, me, (*chip, c), True) for j, chip in enumerate(chips)]

    def start():
        for a in range(n):
            local(a).start()
            for cp in first(a):
                cp.start()

    def relay():
        for j, chip in enumerate(chips):
            for a in range(n):
                copy(a, 1 + j, (*chip, c), me).wait_recv()
                copy(a, 4 + j, (*chip, c), sibling).start()

    def finish():
        for a in range(n):
            copy(a, 0, sibling, me).wait_recv()
            for j, chip in enumerate(chips):
                copy(a, 4 + j, (*chip, 1 - c), me).wait_recv()
        for a in range(n):
            for cp in first(a) + [copy(a, 4 + j, (*chip, c), sibling) for j, chip in enumerate(chips)]:
                cp.wait_send()
            local(a).wait()

    return start, relay, finish


def _a2a_phases(x_refs, out_refs, send_sems, recv_sems, local_sems):
    n = len(x_refs)
    x, y, c = lax.axis_index("x"), lax.axis_index("y"), lax.axis_index("c")
    my = 4 * x + 2 * y + c

    def copy(a, k, incoming):
        px, py, pc = (1 - x if k & 4 else x), (1 - y if k & 2 else y), (1 - c if k & 1 else c)
        pidx = 4 * px + 2 * py + pc
        return pltpu.make_async_remote_copy(
            src_ref=x_refs[a].at[pidx], dst_ref=out_refs[a].at[pidx if incoming else my],
            send_sem=send_sems.at[a * N_PEER + k - 1], recv_sem=recv_sems.at[a * N_PEER + k - 1],
            device_id=(px, py, pc), device_id_type=MESH)

    def local(a):
        return pltpu.make_async_copy(x_refs[a].at[my], out_refs[a].at[my], local_sems.at[a])

    def start():
        for a in range(n):
            local(a).start()
            for k in range(1, N_DEV):
                copy(a, k, False).start()

    def relay():
        pass

    def finish():
        for a in range(n):
            for k in range(1, N_DEV):
                copy(a, k, True).wait_recv()
        for a in range(n):
            for k in range(1, N_DEV):
                copy(a, k, False).wait_send()
            local(a).wait()

    return start, relay, finish


def _exchange_out_shapes(kind, arrays):
    if kind == "ag":
        return [jax.ShapeDtypeStruct((N_DEV,) + a.shape, a.dtype) for a in arrays]
    return [jax.ShapeDtypeStruct(a.shape, a.dtype) for a in arrays]


def _exchange(kind, arrays, name):
    n = len(arrays)
    phases = _ag_phases if kind == "ag" else _a2a_phases

    def body(*refs):
        start, relay, finish = phases(refs[:n], refs[n:2 * n], *refs[2 * n:])
        start()
        relay()
        finish()

    return pl.pallas_call(
        body, name=name,
        out_shape=_exchange_out_shapes(kind, arrays),
        in_specs=[_HBM_SPEC] * n, out_specs=[_HBM_SPEC] * n,
        scratch_shapes=_exchange_scratch(n),
    )(*arrays)


def _mm_tiles(m, n, k, out_bytes, res_bytes):
    tm = _tile(m, V7X_MM_TILE_MN)
    tn = _tile(n, V7X_MM_TILE_MN)
    for tk in _divisors_down(k, V7X_MM_TILE_K):
        need = 2 * 2 * (tm * tk + tk * tn) + (4 + 2 * out_bytes + 2 * res_bytes) * tm * tn
        if need <= V7X_MM_VMEM_BUDGET:
            return tm, tn, tk
    return tm, tn, _tile(k, LANE)


def _mm(a, b, *, ta=False, tb=False, out_dtype=F32, res=None, carry=None, name):
    m, k = (a.shape[1], a.shape[0]) if ta else a.shape
    k2, n = (b.shape[1], b.shape[0]) if tb else b.shape
    assert k == k2, (a.shape, b.shape, ta, tb)
    has_res = res is not None
    tm, tn, tk = _mm_tiles(m, n, k, jnp.dtype(out_dtype).itemsize, res.dtype.itemsize if has_res else 0)
    ni, nj, nk = m // tm, n // tn, k // tk
    dims = (((0,) if ta else (1,), (1,) if tb else (0,)), ((), ()))
    a_spec = pl.BlockSpec((tk, tm), lambda i, j, kk: (kk, i)) if ta else pl.BlockSpec((tm, tk), lambda i, j, kk: (i, kk))
    b_spec = pl.BlockSpec((tn, tk), lambda i, j, kk: (j, kk)) if tb else pl.BlockSpec((tk, tn), lambda i, j, kk: (kk, j))
    o_spec = pl.BlockSpec((tm, tn), lambda i, j, kk: (i, j))
    n_in = 3 if has_res else 2
    c_kind, c_arrays = carry if carry is not None else (None, [])
    nc = len(c_arrays)
    last_step = ni * nj * nk - 1

    def body(*refs):
        a_ref, b_ref = refs[0], refs[1]
        r_ref = refs[2] if has_res else None
        x_refs = refs[n_in:n_in + nc]
        o_ref = refs[n_in + nc]
        out_refs = refs[n_in + nc + 1:n_in + 2 * nc + 1]
        scratch = refs[n_in + 2 * nc + 1:]
        acc = scratch[0] if nk > 1 else None
        kk = pl.program_id(2)
        step = (pl.program_id(0) * nj + pl.program_id(1)) * nk + kk
        if nc:
            phases = _ag_phases if c_kind == "ag" else _a2a_phases
            start, relay, finish = phases(x_refs, out_refs, *scratch[-3:])
            pl.when(step == 0)(start)

        def emit(val):
            if has_res:
                val = val + r_ref[...].astype(F32)
            o_ref[...] = val.astype(out_dtype)

        part = lax.dot_general(a_ref[...], b_ref[...], dims, preferred_element_type=F32)
        if nk == 1:
            emit(part)
        else:
            @pl.when(kk == 0)
            def _():
                acc[...] = part

            @pl.when(kk > 0)
            def _():
                acc[...] += part

            @pl.when(kk == nk - 1)
            def _():
                emit(acc[...])

        if nc:
            if c_kind == "ag":
                pl.when(step == last_step // 2)(relay)
            pl.when(step == last_step)(finish)

    scratch_shapes = ([pltpu.VMEM((tm, tn), F32)] if nk > 1 else []) + (_exchange_scratch(nc) if nc else [])
    sem = ("arbitrary",) * 3 if nc else ("parallel", "parallel", "arbitrary")
    outs = pl.pallas_call(
        body, name=name,
        grid=(ni, nj, nk),
        in_specs=[a_spec, b_spec] + ([o_spec] if has_res else []) + [_HBM_SPEC] * nc,
        out_specs=[o_spec] + [_HBM_SPEC] * nc,
        out_shape=[jax.ShapeDtypeStruct((m, n), out_dtype)] + _exchange_out_shapes(c_kind, c_arrays),
        scratch_shapes=scratch_shapes,
        compiler_params=_cparams(*sem),
    )(*((a, b, res) if has_res else (a, b)), *c_arrays)
    return (outs[0], list(outs[1:])) if nc else outs[0]


def _rowwise(fn, ins, row_outs, acc_outs, *, rows, tb, name):
    in_specs, args = [], []
    for spec in ins:
        kind, arr = spec[0], spec[1]
        if kind == "row":
            in_specs.append(pl.BlockSpec((tb, arr.shape[1]), lambda i: (i, 0)))
        elif kind == "win":
            width, cb = spec[2], spec[3]
            in_specs.append(pl.BlockSpec((tb, width), functools.partial(lambda i, cb: (i, cb), cb=cb)))
        else:
            in_specs.append(pl.BlockSpec(arr.shape, lambda i: (0, 0)))
        args.append(arr)
    out_specs = [pl.BlockSpec((tb, c), lambda i: (i, 0)) for c, _ in row_outs]
    out_specs += [pl.BlockSpec(shape, lambda i: (0, 0)) for shape in acc_outs]
    out_shape = [jax.ShapeDtypeStruct((rows, c), dt) for c, dt in row_outs]
    out_shape += [jax.ShapeDtypeStruct(shape, F32) for shape in acc_outs]
    n_in, n_row = len(ins), len(row_outs)

    def body(*refs):
        vals = [r[...] for r in refs[:n_in]]
        outs = fn(*vals)
        if not isinstance(outs, (tuple, list)):
            outs = (outs,)
        out_refs = refs[n_in:]
        for o_ref, val in zip(out_refs[:n_row], outs[:n_row]):
            o_ref[...] = val.astype(o_ref.dtype)
        first = pl.program_id(0) == 0
        for o_ref, val in zip(out_refs[n_row:], outs[n_row:]):
            @pl.when(first)
            def _(o_ref=o_ref):
                o_ref[...] = jnp.zeros_like(o_ref)
            o_ref[...] += val

    res = pl.pallas_call(
        body, name=name,
        grid=(rows // tb,),
        in_specs=in_specs, out_specs=out_specs, out_shape=out_shape,
        compiler_params=_cparams("arbitrary"),
    )(*args)
    return res


def _rms_fwd(x, g, name):
    def fn(xv, gv):
        r = lax.rsqrt(jnp.mean(xv * xv, axis=-1, keepdims=True) + NORM_EPS)
        return (xv * r * gv,)
    return _rowwise(fn, [("row", x), ("full", g)], [(x.shape[1], BF16)], [], rows=x.shape[0], tb=_tile(x.shape[0], 512), name=name)[0]


def _rms_bwd(x, g, dh, dres, name, want_bf16):
    d = x.shape[1]

    def fn(xv, gv, dhv, drv):
        r = lax.rsqrt(jnp.mean(xv * xv, axis=-1, keepdims=True) + NORM_EPS)
        xhat = xv * r
        dhv = dhv.astype(F32)
        dxhat = dhv * gv
        dx = drv + r * (dxhat - xhat * jnp.mean(dxhat * xhat, axis=-1, keepdims=True))
        dg = jnp.sum(dhv * xhat, axis=0, keepdims=True)
        return (dx, dx, dg) if want_bf16 else (dx, dg)

    row_outs = [(d, F32), (d, BF16)] if want_bf16 else [(d, F32)]
    return _rowwise(fn, [("row", x), ("full", g), ("row", dh), ("row", dres)], row_outs, [(1, d)],
                    rows=x.shape[0], tb=_tile(x.shape[0], 256), name=name)


def _loss_head(x2, g, target, name):
    d = x2.shape[1]

    def fn(xv, gv, tv):
        r = lax.rsqrt(jnp.mean(xv * xv, axis=-1, keepdims=True) + NORM_EPS)
        xhat = xv * r
        diff = xhat * gv - tv
        loss = 0.5 * jnp.sum(jnp.mean(diff * diff, axis=-1, keepdims=True), axis=0, keepdims=True)
        dy = diff * (1.0 / d)
        dxhat = dy * gv
        dx = r * (dxhat - xhat * jnp.mean(dxhat * xhat, axis=-1, keepdims=True))
        dg = jnp.sum(dy * xhat, axis=0, keepdims=True)
        return dx, dx, dg, jnp.broadcast_to(loss, (1, LANE))

    return _rowwise(fn, [("row", x2), ("full", g), ("row", target)], [(d, F32), (d, BF16)], [(1, d), (1, LANE)],
                    rows=x2.shape[0], tb=_tile(x2.shape[0], 256), name=name)


def _swiglu_fwd(gu, name):
    c = gu.shape[1] // 2

    def fn(gv, uv):
        gv = gv.astype(F32)
        return (gv * _sigmoid(gv) * uv.astype(F32),)
    return _rowwise(fn, [("win", gu, c, 0), ("win", gu, c, 1)], [(c, BF16)], [], rows=gu.shape[0],
                    tb=_tile(gu.shape[0], 256), name=name)[0]


def _swiglu_bwd(dact, gu, name):
    c = gu.shape[1] // 2

    def fn(dv, gv, uv):
        dv, gv, uv = dv.astype(F32), gv.astype(F32), uv.astype(F32)
        s = _sigmoid(gv)
        dgate = dv * uv * (s * (1.0 + gv * (1.0 - s)))
        dup = dv * (gv * s)
        return (jnp.concatenate([dgate.astype(BF16), dup.astype(BF16)], axis=1),)
    return _rowwise(fn, [("row", dact), ("win", gu, c, 0), ("win", gu, c, 1)], [(2 * c, BF16)], [], rows=gu.shape[0],
                    tb=_tile(gu.shape[0], 256), name=name)[0]


def _mix_fwd(proj, pa, pb, d, name):
    def fn(ga, gb, av, bv):
        return (_sigmoid(ga.astype(F32)) * av.astype(F32) + _sigmoid(gb.astype(F32)) * bv.astype(F32),)
    return _rowwise(fn, [("win", proj, d, 0), ("win", proj, d, 1), ("row", pa), ("row", pb)], [(d, BF16)], [],
                    rows=pa.shape[0], tb=_tile(pa.shape[0], 512), name=name)[0]


def _mix_bwd(proj, pa, pb, dmix, d, name):
    def fn(ga, gb, av, bv, dm):
        dm = dm.astype(F32)
        sa, sb = _sigmoid(ga.astype(F32)), _sigmoid(gb.astype(F32))
        av, bv = av.astype(F32), bv.astype(F32)
        return dm * sa, dm * sb, dm * av * sa * (1.0 - sa), dm * bv * sb * (1.0 - sb)
    return _rowwise(fn, [("win", proj, d, 0), ("win", proj, d, 1), ("row", pa), ("row", pb), ("row", dmix)],
                    [(d, BF16)] * 4, [], rows=pa.shape[0], tb=_tile(pa.shape[0], 512), name=name)


def _chunk_masks(tb):
    r = lax.broadcasted_iota(jnp.int32, (tb, tb), 0)
    c = lax.broadcasted_iota(jnp.int32, (tb, tb), 1)
    same = lax.shift_right_logical(r, GLA_CHUNK_SHIFT) == lax.shift_right_logical(c, GLA_CHUNK_SHIFT)
    return same, same & (c <= r), same & (r <= c)


def _mask_bf16(mask):
    return jnp.where(mask, 1.0, 0.0).astype(BF16)


def _split_dot(mask_bf, x, terms):
    acc, rem = None, x
    for _ in range(terms):
        hi = rem.astype(BF16)
        part = jnp.dot(mask_bf, hi, preferred_element_type=F32)
        acc = part if acc is None else acc + part
        rem = rem - hi.astype(F32)
    return acc


def _gla_decay(al, wa2, ba2, same_bf, causal_bf):
    z = jnp.dot(al.astype(BF16), wa2, preferred_element_type=F32) + ba2
    la = (jnp.minimum(z, 0.0) - jnp.log(1.0 + jnp.exp(-jnp.abs(z)))) * (1.0 / GLA_TAU)
    bc = _split_dot(causal_bf, la, 3)
    bl = _split_dot(same_bf, la, 3)
    return z, bc, bl


def _dot_t(a, b, ca, cb):
    return lax.dot_general(a, b, (((ca,), (cb,)), ((), ())), preferred_element_type=F32)


def _gla_fwd(proj, alow, wa2, ba2, ghn, *, dk, dv, name):
    t = proj.shape[0]
    tb = min(GLA_BLOCK, t)
    nch = tb // GLA_CHUNK
    hk, hv = dk // GLA_HEADS, dv // GLA_HEADS
    scale = hk ** -0.5
    v_cb, r_cb = (8 * dk) // dv, (8 * dk) // dv + 1
    q_cb, k_cb = (8 * dk + 2 * dv) // dk, (8 * dk + 2 * dv) // dk + 1

    def body(q_ref, k_ref, v_ref, r_ref, al_ref, wa2_ref, ba2_ref, ghn_ref, oa_ref, opre_ref, s_ref, st_scr):
        @pl.when(pl.program_id(0) == 0)
        def _():
            st_scr[...] = jnp.zeros_like(st_scr)

        same, causal, _ = _chunk_masks(tb)
        same_bf, causal_bf = _mask_bf16(same), _mask_bf16(causal)
        _, bc, bl = _gla_decay(al_ref[...], wa2_ref[...], ba2_ref[...], same_bf, causal_bf)
        q = q_ref[...].astype(F32) * scale
        k = k_ref[...].astype(F32)
        qd = (q * jnp.exp(bc)).astype(BF16)
        ki = (k * jnp.exp(-bc)).astype(BF16)
        ks = (k * jnp.exp(bl - bc)).astype(BF16)
        dl = jnp.exp(bl)
        for h in range(GLA_HEADS):
            ksl = slice(h * hk, (h + 1) * hk)
            vsl = slice(h * hv, (h + 1) * hv)
            v_h = v_ref[:, vsl]
            sc = jnp.where(causal, _dot_t(qd[:, ksl], ki[:, ksl], 1, 1), 0.0)
            o_intra = jnp.dot(sc.astype(BF16), v_h, preferred_element_type=F32)
            for c in range(nch):
                rows = slice(c * GLA_CHUNK, (c + 1) * GLA_CHUNK)
                st = st_scr[h]
                s_ref[c, h] = st
                o_c = o_intra[rows] + _dot_t(qd[rows, ksl], st.astype(BF16), 1, 1)
                opre_ref[rows, vsl] = o_c
                st_scr[h] = dl[c * GLA_CHUNK:c * GLA_CHUNK + 1, ksl] * st + _dot_t(v_h[rows], ks[rows, ksl], 0, 0)
        for h in range(GLA_HEADS):
            vsl = slice(h * hv, (h + 1) * hv)
            o = opre_ref[:, vsl]
            rs = lax.rsqrt(jnp.mean(o * o, axis=-1, keepdims=True) + NORM_EPS)
            rv = r_ref[:, vsl].astype(F32)
            oa_ref[:, vsl] = (rv * _sigmoid(rv) * (o * rs * ghn_ref[:, vsl])).astype(BF16)

    nchunks = t // GLA_CHUNK
    return pl.pallas_call(
        body, name=name,
        grid=(t // tb,),
        in_specs=[
            pl.BlockSpec((tb, dk), lambda i: (i, q_cb)),
            pl.BlockSpec((tb, dk), lambda i: (i, k_cb)),
            pl.BlockSpec((tb, dv), lambda i: (i, v_cb)),
            pl.BlockSpec((tb, dv), lambda i: (i, r_cb)),
            pl.BlockSpec((tb, LANE), lambda i: (i, 0)),
            pl.BlockSpec(wa2.shape, lambda i: (0, 0)),
            pl.BlockSpec(ba2.shape, lambda i: (0, 0)),
            pl.BlockSpec(ghn.shape, lambda i: (0, 0)),
        ],
        out_specs=[
            pl.BlockSpec((tb, dv), lambda i: (i, 0)),
            pl.BlockSpec((tb, dv), lambda i: (i, 0)),
            pl.BlockSpec((nch, GLA_HEADS, hv, hk), lambda i: (i, 0, 0, 0)),
        ],
        out_shape=[
            jax.ShapeDtypeStruct((t, dv), BF16),
            jax.ShapeDtypeStruct((t, dv), F32),
            jax.ShapeDtypeStruct((nchunks, GLA_HEADS, hv, hk), F32),
        ],
        scratch_shapes=[pltpu.VMEM((GLA_HEADS, hv, hk), F32)],
        compiler_params=_cparams("arbitrary"),
    )(proj, proj, proj, proj, alow, wa2, ba2, ghn)


def _gla_bwd(proj, alow, wa2, ba2, ghn, opre, states, doa, *, dk, dv, name):
    t = proj.shape[0]
    tb = min(GLA_BLOCK, t)
    nb = t // tb
    nch = tb // GLA_CHUNK
    hk, hv = dk // GLA_HEADS, dv // GLA_HEADS
    scale = hk ** -0.5
    v_cb, r_cb = (8 * dk) // dv, (8 * dk) // dv + 1
    q_cb, k_cb = (8 * dk + 2 * dv) // dk, (8 * dk + 2 * dv) // dk + 1

    def body(q_ref, k_ref, v_ref, r_ref, al_ref, wa2_ref, ba2_ref, ghn_ref, opre_ref, s_ref, doa_ref,
             dq_ref, dk_ref, dv_ref, dr_ref, dal_ref, dwa2_ref, dba2_ref, dghn_ref,
             dst_scr, dqd_scr, dki_scr, dks_scr, ddl_scr):
        @pl.when(pl.program_id(0) == 0)
        def _():
            dst_scr[...] = jnp.zeros_like(dst_scr)
            dwa2_ref[...] = jnp.zeros_like(dwa2_ref)
            dba2_ref[...] = jnp.zeros_like(dba2_ref)
            dghn_ref[...] = jnp.zeros_like(dghn_ref)

        same, causal, anti = _chunk_masks(tb)
        same_bf, causal_bf, anti_bf = _mask_bf16(same), _mask_bf16(causal), _mask_bf16(anti)
        al = al_ref[...]
        wa2v = wa2_ref[...]
        z, bc, bl = _gla_decay(al, wa2v, ba2_ref[...], same_bf, causal_bf)
        e_bc, e_nbc, e_st = jnp.exp(bc), jnp.exp(-bc), jnp.exp(bl - bc)
        q = q_ref[...].astype(F32) * scale
        k = k_ref[...].astype(F32)
        qd_f, ki_f, ks_f = q * e_bc, k * e_nbc, k * e_st
        qd, ki, ks = qd_f.astype(BF16), ki_f.astype(BF16), ks_f.astype(BF16)
        dl = jnp.exp(bl)
        for h in range(GLA_HEADS):
            ksl = slice(h * hk, (h + 1) * hk)
            vsl = slice(h * hv, (h + 1) * hv)
            o = opre_ref[:, vsl]
            rs = lax.rsqrt(jnp.mean(o * o, axis=-1, keepdims=True) + NORM_EPS)
            ohat = o * rs
            g_h = ghn_ref[:, vsl]
            rv = r_ref[:, vsl].astype(F32)
            sg = _sigmoid(rv)
            d_oa = doa_ref[:, vsl].astype(F32)
            don = d_oa * (rv * sg)
            dr_ref[:, vsl] = (d_oa * (ohat * g_h) * (sg * (1.0 + rv * (1.0 - sg)))).astype(BF16)
            dghn_ref[:, vsl] += jnp.sum(don * ohat, axis=0, keepdims=True)
            dohat = don * g_h
            do_f = rs * (dohat - ohat * jnp.mean(dohat * ohat, axis=-1, keepdims=True))
            do = do_f.astype(BF16)
            v_h = v_ref[:, vsl]
            p = jnp.where(causal, _dot_t(do, v_h, 1, 1), 0.0).astype(BF16)
            dqd_intra = jnp.dot(p, ki[:, ksl], preferred_element_type=F32)
            dki_scr[:, ksl] = _dot_t(p, qd[:, ksl], 0, 0)
            sc = jnp.where(causal, _dot_t(qd[:, ksl], ki[:, ksl], 1, 1), 0.0).astype(BF16)
            dv_intra = _dot_t(sc, do, 0, 0)
            for c in reversed(range(nch)):
                rows = slice(c * GLA_CHUNK, (c + 1) * GLA_CHUNK)
                dst = dst_scr[h]
                st = s_ref[c, h]
                dst_bf = dst.astype(BF16)
                dv_ref[rows, vsl] = (dv_intra[rows] + _dot_t(ks[rows, ksl], dst_bf, 1, 1)).astype(BF16)
                dks_scr[rows, ksl] = jnp.dot(v_h[rows], dst_bf, preferred_element_type=F32)
                dl_c = dl[c * GLA_CHUNK:c * GLA_CHUNK + 1, ksl]
                ddl = jnp.sum(dst * st, axis=0, keepdims=True) * dl_c
                ddl_scr[rows, ksl] = jnp.broadcast_to(ddl, (GLA_CHUNK, hk))
                dqd_scr[rows, ksl] = dqd_intra[rows] + jnp.dot(do[rows], st.astype(BF16), preferred_element_type=F32)
                dst_scr[h] = dl_c * dst + _dot_t(do[rows], qd[rows, ksl], 0, 0)
        dqd, dki, dks = dqd_scr[...], dki_scr[...], dks_scr[...]
        dq_ref[...] = (dqd * (scale * e_bc)).astype(BF16)
        dk_ref[...] = (dki * e_nbc + dks * e_st).astype(BF16)
        dks_ks = dks * ks_f
        dbc = dqd * qd_f - dki * ki_f - dks_ks
        dla = _split_dot(anti_bf, dbc, 2) + _split_dot(same_bf, dks_ks, 2) + ddl_scr[...]
        dz = (dla * (1.0 / GLA_TAU) * (1.0 - _sigmoid(z)))
        dz_bf = dz.astype(BF16)
        dal_ref[...] = _dot_t(dz_bf, wa2v, 1, 1).astype(BF16)
        dwa2_ref[...] += _dot_t(al.astype(BF16), dz_bf, 0, 0)
        dba2_ref[...] += jnp.sum(dz, axis=0, keepdims=True)

    rev = lambda i: nb - 1 - i
    return pl.pallas_call(
        body, name=name,
        grid=(nb,),
        in_specs=[
            pl.BlockSpec((tb, dk), lambda i: (rev(i), q_cb)),
            pl.BlockSpec((tb, dk), lambda i: (rev(i), k_cb)),
            pl.BlockSpec((tb, dv), lambda i: (rev(i), v_cb)),
            pl.BlockSpec((tb, dv), lambda i: (rev(i), r_cb)),
            pl.BlockSpec((tb, LANE), lambda i: (rev(i), 0)),
            pl.BlockSpec(wa2.shape, lambda i: (0, 0)),
            pl.BlockSpec(ba2.shape, lambda i: (0, 0)),
            pl.BlockSpec(ghn.shape, lambda i: (0, 0)),
            pl.BlockSpec((tb, dv), lambda i: (rev(i), 0)),
            pl.BlockSpec((nch, GLA_HEADS, hv, hk), lambda i: (rev(i), 0, 0, 0)),
            pl.BlockSpec((tb, dv), lambda i: (rev(i), 0)),
        ],
        out_specs=[
            pl.BlockSpec((tb, dk), lambda i: (rev(i), 0)),
            pl.BlockSpec((tb, dk), lambda i: (rev(i), 0)),
            pl.BlockSpec((tb, dv), lambda i: (rev(i), 0)),
            pl.BlockSpec((tb, dv), lambda i: (rev(i), 0)),
            pl.BlockSpec((tb, LANE), lambda i: (rev(i), 0)),
            pl.BlockSpec(wa2.shape, lambda i: (0, 0)),
            pl.BlockSpec(ba2.shape, lambda i: (0, 0)),
            pl.BlockSpec(ghn.shape, lambda i: (0, 0)),
        ],
        out_shape=[
            jax.ShapeDtypeStruct((t, dk), BF16),
            jax.ShapeDtypeStruct((t, dk), BF16),
            jax.ShapeDtypeStruct((t, dv), BF16),
            jax.ShapeDtypeStruct((t, dv), BF16),
            jax.ShapeDtypeStruct((t, LANE), BF16),
            jax.ShapeDtypeStruct(wa2.shape, F32),
            jax.ShapeDtypeStruct(ba2.shape, F32),
            jax.ShapeDtypeStruct(ghn.shape, F32),
        ],
        scratch_shapes=[pltpu.VMEM((GLA_HEADS, hv, hk), F32)] + [pltpu.VMEM((tb, dk), F32)] * 4,
        compiler_params=_cparams("arbitrary"),
    )(proj, proj, proj, proj, alow, wa2, ba2, ghn, opre, states, doa)


def _s5_tables(lam_re, lam_im, log_dt, b_re, b_im, c_re, c_im):
    hp = lax.Precision.HIGHEST
    g, p = lam_re.shape
    ln = S5_L
    dt = jnp.exp(log_dt)[:, None]
    lr, li = lam_re, lam_im
    mag = jnp.exp(lr * dt)
    ar, ai = mag * jnp.cos(li * dt), mag * jnp.sin(li * dt)
    den = lr * lr + li * li
    am1 = ar - 1.0
    f_re = ((am1 * lr + ai * li) / den)[..., None]
    f_im = ((ai * lr - am1 * li) / den)[..., None]
    bb_re = f_re * b_re - f_im * b_im
    bb_im = f_re * b_im + f_im * b_re
    j = jnp.arange(ln + 1, dtype=F32)[None, :, None]
    pm = jnp.exp(j * (lr * dt)[:, None, :])
    ang = j * (li * dt)[:, None, :]
    pw_re, pw_im = pm * jnp.cos(ang), pm * jnp.sin(ang)
    cp_re = c_re[:, None] * pw_re[:, :, None, :] - c_im[:, None] * pw_im[:, :, None, :]
    cp_im = c_re[:, None] * pw_im[:, :, None, :] + c_im[:, None] * pw_re[:, :, None, :]
    kj = (jnp.einsum("gjcp,gpd->gjcd", cp_re[:, :ln], bb_re, precision=hp)
          - jnp.einsum("gjcp,gpd->gjcd", cp_im[:, :ln], bb_im, precision=hp))
    s_i = jnp.arange(ln)[None, :, None]
    t_i = jnp.arange(ln)[None, None, :]
    j_i = jnp.arange(ln)[:, None, None]
    shift = (t_i - s_i == j_i).astype(F32)
    m = jnp.einsum("jst,gjcd->gsdtc", shift, kj, precision=hp).reshape(g, ln * S5_GC, ln * S5_GC)
    rp_re, rp_im = pw_re[:, ln - 1::-1], pw_im[:, ln - 1::-1]
    bbt_re, bbt_im = bb_re.transpose(0, 2, 1)[:, None], bb_im.transpose(0, 2, 1)[:, None]
    bst_re = rp_re[:, :, None, :] * bbt_re - rp_im[:, :, None, :] * bbt_im
    bst_im = rp_re[:, :, None, :] * bbt_im + rp_im[:, :, None, :] * bbt_re
    bst = jnp.concatenate([bst_re, bst_im], axis=-1).reshape(g, ln * S5_GC, 2 * p)
    cst = jnp.concatenate([cp_re[:, 1:].transpose(0, 3, 1, 2), -cp_im[:, 1:].transpose(0, 3, 1, 2)], axis=1)
    cst = cst.reshape(g, 2 * p, ln * S5_GC)
    a = jnp.stack([jnp.concatenate([pw_re[:, ln], pw_re[:, ln]], axis=-1),
                   jnp.concatenate([-pw_im[:, ln], pw_im[:, ln]], axis=-1)], axis=1)
    return m, bst, cst, a


def _state_scan(v, pr, pi, reverse):
    n = v.shape[0]
    half = v.shape[1] // 2
    row = lax.broadcasted_iota(jnp.int32, v.shape, 0)
    z, s = v, 1
    while s < n:
        if reverse:
            zs = jnp.where(row < n - s, pltpu.roll(z, n - s, 0), 0.0)
        else:
            zs = jnp.where(row >= s, pltpu.roll(z, s, 0), 0.0)
        z = z + zs * pr + pltpu.roll(zs, half, 1) * pi
        pr, pi = pr * pr - pi * pi, 2.0 * pr * pi
        s *= 2
    return z


def _s5_core_fwd(u_g, m, bst, cst, a, name):
    g, nc, w = u_g.shape
    p2 = bst.shape[2]

    def body(u_ref, m_ref, b_ref, c_ref, a_ref, y_ref, x_ref):
        u = u_ref[0]
        v = jnp.dot(u, b_ref[0], preferred_element_type=F32)
        z = _state_scan(v, a_ref[0, 0:1, :], a_ref[0, 1:2, :], reverse=False)
        row = lax.broadcasted_iota(jnp.int32, z.shape, 0)
        x = jnp.where(row >= 1, pltpu.roll(z, 1, 0), 0.0)
        x_ref[0] = x
        y_ref[0] = (jnp.dot(u, m_ref[0], preferred_element_type=F32)
                    + jnp.dot(x.astype(BF16), c_ref[0], preferred_element_type=F32))

    per_g = lambda shape: pl.BlockSpec((1,) + shape, lambda i: (i, 0, 0))
    return pl.pallas_call(
        body, name=name, grid=(g,),
        in_specs=[per_g((nc, w)), per_g((w, w)), per_g((w, p2)), per_g((p2, w)), per_g((2, p2))],
        out_specs=[per_g((nc, w)), per_g((nc, p2))],
        out_shape=[jax.ShapeDtypeStruct((g, nc, w), F32), jax.ShapeDtypeStruct((g, nc, p2), F32)],
        compiler_params=_cparams("parallel"),
    )(u_g, m, bst, cst, a)


def _s5_core_bwd(dy_g, u_g, x_g, m, bst, cst, a, name):
    g, nc, w = u_g.shape
    p2 = bst.shape[2]

    def body(dy_ref, u_ref, x_ref, m_ref, b_ref, c_ref, a_ref, du_ref, dm_ref, db_ref, dc_ref, da_ref):
        dy, u, x = dy_ref[0], u_ref[0], x_ref[0]
        gx = _dot_t(dy, c_ref[0], 1, 1)
        rtot = _state_scan(gx, a_ref[0, 0:1, :], -a_ref[0, 1:2, :], reverse=True)
        row = lax.broadcasted_iota(jnp.int32, rtot.shape, 0)
        dv = jnp.where(row < nc - 1, pltpu.roll(rtot, nc - 1, 0), 0.0)
        dv_bf = dv.astype(BF16)
        du_ref[0] = _dot_t(dy, m_ref[0], 1, 1) + _dot_t(dv_bf, b_ref[0], 1, 1)
        dm_ref[0] = _dot_t(u, dy, 0, 0)
        dc_ref[0] = _dot_t(x.astype(BF16), dy, 0, 0)
        db_ref[0] = _dot_t(u, dv_bf, 0, 0)
        x_sw = pltpu.roll(x, p2 // 2, 1)
        da_ref[0, 0:1, :] = jnp.sum(dv * x, axis=0, keepdims=True)
        da_ref[0, 1:2, :] = jnp.sum(dv * x_sw, axis=0, keepdims=True)

    per_g = lambda shape: pl.BlockSpec((1,) + shape, lambda i: (i, 0, 0))
    return pl.pallas_call(
        body, name=name, grid=(g,),
        in_specs=[per_g((nc, w)), per_g((nc, w)), per_g((nc, p2)), per_g((w, w)), per_g((w, p2)), per_g((p2, w)), per_g((2, p2))],
        out_specs=[per_g((nc, w)), per_g((w, w)), per_g((w, p2)), per_g((p2, w)), per_g((2, p2))],
        out_shape=[jax.ShapeDtypeStruct((g, nc, w), F32), jax.ShapeDtypeStruct((g, w, w), F32),
                   jax.ShapeDtypeStruct((g, w, p2), F32), jax.ShapeDtypeStruct((g, p2, w), F32),
                   jax.ShapeDtypeStruct((g, 2, p2), F32)],
        compiler_params=_cparams("parallel"),
    )(dy_g, u_g, x_g, m, bst, cst, a)


def _gelu_parts(y):
    inner = GELU_C * (y + GELU_A * y * y * y)
    th = jnp.tanh(inner)
    return th, 0.5 * y * (1.0 + th)


def _s5_post_fwd(y_raw, proj, u_cb, s5d, wglu, bglu, name):
    w = y_raw.shape[1]

    def fn(yr, u, dsk, wg, bg):
        y = yr + dsk * u.astype(F32)
        _, h = _gelu_parts(y)
        gl = jnp.dot(h.astype(BF16), wg, preferred_element_type=F32) + bg
        return (h * _sigmoid(gl),)

    return _rowwise(fn, [("row", y_raw), ("win", proj, w, u_cb), ("full", s5d), ("full", wglu), ("full", bglu)],
                    [(w, BF16)], [], rows=y_raw.shape[0], tb=_tile(y_raw.shape[0], 512), name=name)[0]


def _s5_post_bwd(y_raw, proj, u_cb, s5d, wglu, bglu, dob, name):
    w = y_raw.shape[1]

    def fn(yr, u, dsk, wg, bg, dov):
        u = u.astype(F32)
        dov = dov.astype(F32)
        y = yr + dsk * u
        th, h = _gelu_parts(y)
        h_bf = h.astype(BF16)
        gl = jnp.dot(h_bf, wg, preferred_element_type=F32) + bg
        sg = _sigmoid(gl)
        dgl = dov * h * sg * (1.0 - sg)
        dgl_bf = dgl.astype(BF16)
        dh = dov * sg + _dot_t(dgl_bf, wg, 1, 1)
        dgelu = 0.5 * (1.0 + th) + 0.5 * y * (1.0 - th * th) * GELU_C * (1.0 + 3.0 * GELU_A * y * y)
        dy = dh * dgelu
        return (dy, dy * dsk,
                _dot_t(h_bf, dgl_bf, 0, 0), jnp.sum(dgl, axis=0, keepdims=True), jnp.sum(dy * u, axis=0, keepdims=True))

    return _rowwise(fn, [("row", y_raw), ("win", proj, w, u_cb), ("full", s5d), ("full", wglu), ("full", bglu), ("row", dob)],
                    [(w, BF16), (w, F32)], [(w, w), (1, w), (1, w)], rows=y_raw.shape[0], tb=_tile(y_raw.shape[0], 512), name=name)


def _to_groups(a, dtype):
    t, w = a.shape
    g = w // S5_GC
    return a.reshape(t // S5_L, S5_L, g, S5_GC).transpose(2, 0, 1, 3).reshape(g, t // S5_L, S5_L * S5_GC).astype(dtype)


def _from_groups(a):
    g, nc, _ = a.shape
    return a.reshape(g, nc, S5_L, S5_GC).transpose(1, 2, 0, 3).reshape(nc * S5_L, g * S5_GC)


def _adamw(w, g, m, v, name):
    rows, cols = w.shape
    tb = _tile(rows, 256, align=16)
    slots = g.ndim == 3
    c1 = 1.0 - ADAM_B1 ** ADAM_STEP
    c2 = 1.0 - ADAM_B2 ** ADAM_STEP

    def body(w_ref, g_ref, m_ref, v_ref, *out_refs):
        if slots:
            gv = g_ref[0].astype(F32)
            for s in range(1, N_DEV):
                gv = gv + g_ref[s].astype(F32)
            out_refs[0][...] = gv
        else:
            gv = g_ref[...]
        d_ref, nm_ref, nv_ref = out_refs[-3:]
        nm = ADAM_B1 * m_ref[...] + (1.0 - ADAM_B1) * gv
        nv = ADAM_B2 * v_ref[...] + (1.0 - ADAM_B2) * (gv * gv)
        d_ref[...] = -ADAM_LR * ((nm / c1) / (jnp.sqrt(nv / c2) + ADAM_EPS) + ADAM_WD * w_ref[...])
        nm_ref[...] = nm
        nv_ref[...] = nv

    spec = pl.BlockSpec((tb, cols), lambda i: (i, 0))
    g_spec = pl.BlockSpec((N_DEV, tb, cols), lambda i: (0, i, 0)) if slots else spec
    n_out = 4 if slots else 3
    return pl.pallas_call(
        body, name=name, grid=(rows // tb,),
        in_specs=[spec, g_spec, spec, spec], out_specs=[spec] * n_out,
        out_shape=[jax.ShapeDtypeStruct((rows, cols), F32)] * n_out,
        compiler_params=_cparams("parallel"),
    )(w, g, m, v)


def _slot_sum(x, name):
    _, rows, cols = x.shape
    if rows % 8 == 0:
        tr, tc = _tile(rows, 512, align=8), cols
    else:
        tr, tc = rows, _tile(cols, 256)

    def body(x_ref, o_ref):
        acc = x_ref[0].astype(F32)
        for s in range(1, N_DEV):
            acc = acc + x_ref[s].astype(F32)
        o_ref[...] = acc

    return pl.pallas_call(
        body, name=name, grid=(rows // tr, cols // tc),
        in_specs=[pl.BlockSpec((N_DEV, tr, tc), lambda i, j: (0, i, j))],
        out_specs=pl.BlockSpec((tr, tc), lambda i, j: (i, j)),
        out_shape=jax.ShapeDtypeStruct((rows, cols), F32),
        compiler_params=_cparams("parallel", "parallel"),
    )(x)


_REST = (("w_a2", 1), ("w_glu", 0), ("w_branch_a", 1), ("w_branch_b", 1), ("w_out", 0), ("w_ffn_in", 1), ("w_ffn_out", 0))
_SMALL = ("norm1_g", "b_a2", "gla_norm_g", "lam_re", "lam_im", "log_dt", "s5_b_re", "s5_b_im", "s5_c_re", "s5_c_im",
          "s5_d", "b_glu", "norm2_g", "final_norm_g")
_ORDER = ("norm1_g", "w_in", "w_a2", "b_a2", "gla_norm_g", "lam_re", "lam_im", "log_dt", "s5_b_re", "s5_b_im", "s5_c_re",
          "s5_c_im", "s5_d", "w_glu", "b_glu", "w_branch_a", "w_branch_b", "w_out", "norm2_g", "w_ffn_in", "w_ffn_out", "final_norm_g")


def _join_slots(slots, axis):
    _, r, c = slots.shape
    if axis == 0:
        return slots.reshape(N_DEV * r, c)
    return slots.transpose(1, 0, 2).reshape(r, N_DEV * c)


def _to_slots(full, axis):
    r, c = full.shape
    if axis == 0:
        return full.reshape(N_DEV, r // N_DEV, c)
    return full.reshape(r, N_DEV, c // N_DEV).transpose(1, 0, 2)


def _local_step(x, target, w_in_t, small, rest):
    t, d = x.shape
    dk, dv, s5w = d // 4, d // 2, d // 4
    dist = not isinstance(rest, dict)
    o_q, o_k, o_v, o_r, o_al = 0, dk, 2 * dk, 2 * dk + dv, 2 * dk + 2 * dv
    o_u = o_al + GLA_RANK
    o_ga, o_gb = o_u + s5w, o_u + s5w + d
    rows = lambda a, o, n: a[o:o + n]
    w_main_t = jnp.concatenate([rows(w_in_t, o_ga, d), rows(w_in_t, o_gb, d), rows(w_in_t, o_v, dv), rows(w_in_t, o_r, dv),
                                rows(w_in_t, o_q, dk), rows(w_in_t, o_k, dk), rows(w_in_t, o_u, s5w)], axis=0)
    w_al_t = jnp.pad(rows(w_in_t, o_al, GLA_RANK), ((0, LANE - GLA_RANK), (0, 0)))
    u_cb = (2 * d + 2 * dv + 2 * dk) // s5w

    h1 = _rms_fwd(x, small["norm1_g"], "norm1_fwd")
    if dist:
        proj, gathered = _mm(h1, w_main_t, tb=True, out_dtype=BF16, carry=("ag", rest), name="in_proj")
        w = {n: _join_slots(g, ax) for (n, ax), g in zip(_REST, gathered)}
    else:
        proj = _mm(h1, w_main_t, tb=True, out_dtype=BF16, name="in_proj")
        w = rest
    wa2 = jnp.pad(w["w_a2"], ((0, LANE - GLA_RANK), (0, 0)))
    alow = _mm(h1, w_al_t, tb=True, out_dtype=BF16, name="in_proj_gate_rank")
    o_a, o_pre, states = _gla_fwd(proj, alow, wa2, small["b_a2"], small["gla_norm_g"], dk=dk, dv=dv, name="gla_fwd")

    s5_params = (small["lam_re"], small["lam_im"], small["log_dt"][0], small["s5_b_re"], small["s5_b_im"],
                 small["s5_c_re"], small["s5_c_im"])
    (tm, tbst, tcst, ta), tables_vjp = jax.vjp(_s5_tables, *s5_params)
    tm_bf, tbst_bf, tcst_bf = tm.astype(BF16), tbst.astype(BF16), tcst.astype(BF16)
    u_g = _to_groups(proj[:, u_cb * s5w:(u_cb + 1) * s5w], BF16)
    y_g, x_g = _s5_core_fwd(u_g, tm_bf, tbst_bf, tcst_bf, ta, "s5_core_fwd")
    y_raw = _from_groups(y_g)
    o_b = _s5_post_fwd(y_raw, proj, u_cb, small["s5_d"], w["w_glu"], small["b_glu"], "s5_post_fwd")

    pa = _mm(o_a, w["w_branch_a"], out_dtype=BF16, name="branch_a")
    pb = _mm(o_b, w["w_branch_b"], out_dtype=BF16, name="branch_b")
    mix = _mix_fwd(proj, pa, pb, d, "mix_fwd")
    x1 = _mm(mix, w["w_out"], res=x, name="out_proj")
    h2 = _rms_fwd(x1, small["norm2_g"], "norm2_fwd")
    gu = _mm(h2, w["w_ffn_in"], out_dtype=BF16, name="ffn_in")
    act = _swiglu_fwd(gu, "swiglu_fwd")
    x2 = _mm(act, w["w_ffn_out"], res=x1, name="ffn_out")
    dx2, dx2_bf, d_final_g, loss = _loss_head(x2, small["final_norm_g"], target, "loss_head")

    recv = {}
    dact = _mm(dx2_bf, w["w_ffn_out"], tb=True, out_dtype=BF16, name="d_act")
    g_ffn_out = _mm(act, dx2_bf, ta=True, out_dtype=BF16, name="g_w_ffn_out")
    dgu = _swiglu_bwd(dact, gu, "swiglu_bwd")
    g_ffn_in = _mm(h2, dgu, ta=True, out_dtype=BF16, name="g_w_ffn_in")
    if dist:
        dh2, (recv["w_ffn_out"], recv["w_ffn_in"]) = _mm(
            dgu, w["w_ffn_in"], tb=True, carry=("a2a", [_to_slots(g_ffn_out, 0), _to_slots(g_ffn_in, 1)]), name="d_h2")
    else:
        dh2 = _mm(dgu, w["w_ffn_in"], tb=True, name="d_h2")
    dx1, dx1_bf, d_norm2_g = _rms_bwd(x1, small["norm2_g"], dh2, dx2, "norm2_bwd", True)
    dmix = _mm(dx1_bf, w["w_out"], tb=True, out_dtype=BF16, name="d_mix")
    g_out = _mm(mix, dx1_bf, ta=True, out_dtype=BF16, name="g_w_out")
    dpa, dpb, dga, dgb = _mix_bwd(proj, pa, pb, dmix, d, "mix_bwd")
    doa = _mm(dpa, w["w_branch_a"], tb=True, out_dtype=BF16, name="d_o_a")
    dob = _mm(dpb, w["w_branch_b"], tb=True, out_dtype=BF16, name="d_o_b")
    g_branch_a = _mm(o_a, dpa, ta=True, out_dtype=BF16, name="g_w_branch_a")
    g_branch_b = _mm(o_b, dpb, ta=True, out_dtype=BF16, name="g_w_branch_b")

    dy_s5, du_direct, g_glu, g_bglu, g_s5d = _s5_post_bwd(y_raw, proj, u_cb, small["s5_d"], w["w_glu"], small["b_glu"], dob, "s5_post_bwd")
    du_g, d_tm, d_tbst, d_tcst, d_ta = _s5_core_bwd(_to_groups(dy_s5, BF16), u_g, x_g, tm_bf, tbst_bf, tcst_bf, ta, "s5_core_bwd")
    g_lam_re, g_lam_im, g_log_dt, g_b_re, g_b_im, g_c_re, g_c_im = tables_vjp((d_tm, d_tbst, d_tcst, d_ta))
    du = (_from_groups(du_g) + du_direct).astype(BF16)

    dq, dkk, dvv, dr, dal, g_wa2, g_ba2, g_ghn = _gla_bwd(proj, alow, wa2, small["b_a2"], small["gla_norm_g"], o_pre, states, doa,
                                                        dk=dk, dv=dv, name="gla_bwd")
    dproj = jnp.concatenate([dga, dgb, dvv, dr, dq, dkk, du], axis=1)
    mid = {"w_out": g_out, "w_branch_a": g_branch_a, "w_branch_b": g_branch_b, "w_glu": g_glu.astype(BF16),
           "w_a2": g_wa2[:GLA_RANK].astype(BF16)}
    if dist:
        axes = dict(_REST)
        g_main_t, got = _mm(dproj, h1, ta=True, out_dtype=BF16, carry=("a2a", [_to_slots(mid[n], axes[n]) for n in mid]),
                            name="g_w_in_main")
        recv.update(zip(mid, got))
    else:
        g_main_t = _mm(dproj, h1, ta=True, out_dtype=BF16, name="g_w_in_main")
    g_al_t = _mm(dal, h1, ta=True, out_dtype=BF16, name="g_w_in_gate_rank")
    mrows = lambda o, n: g_main_t[o:o + n]
    g_w_in_t = jnp.concatenate([mrows(2 * d + 2 * dv, dk), mrows(2 * d + 2 * dv + dk, dk), mrows(2 * d, dv), mrows(2 * d + dv, dv),
                                g_al_t[:GLA_RANK], mrows(2 * d + 2 * dv + 2 * dk, s5w), mrows(0, d), mrows(d, d)], axis=0)
    if dist:
        dh1, (recv["w_in"],) = _mm(dproj, w_main_t, carry=("a2a", [_to_slots(g_w_in_t, 0)]), name="d_h1_main")
    else:
        dh1 = _mm(dproj, w_main_t, name="d_h1_main")
    dh1 = _mm(dal, w_al_t, res=dh1, name="d_h1_gate_rank")
    grad_x, d_norm1_g = _rms_bwd(x, small["norm1_g"], dh1, dx1, "norm1_bwd", False)

    small_g = {
        "norm1_g": d_norm1_g, "b_a2": g_ba2, "gla_norm_g": g_ghn, "lam_re": g_lam_re, "lam_im": g_lam_im,
        "log_dt": g_log_dt[None], "s5_b_re": g_b_re, "s5_b_im": g_b_im, "s5_c_re": g_c_re, "s5_c_im": g_c_im,
        "s5_d": g_s5d, "b_glu": g_bglu, "norm2_g": d_norm2_g, "final_norm_g": d_final_g,
    }
    if not dist:
        recv = dict(mid, w_in=g_w_in_t, w_ffn_in=g_ffn_in, w_ffn_out=g_ffn_out)
    return loss[0, 0], grad_x, recv, small_g


def _small_2d(name, a):
    a = a[0]
    return a[None] if a.ndim == 1 else a


def kernel(x, norm1_g, w_in, w_a2, b_a2, gla_norm_g, lam_re, lam_im, log_dt, s5_b_re, s5_b_im, s5_c_re, s5_c_im, s5_d, w_glu, b_glu, w_branch_a, w_branch_b, w_out, norm2_g, w_ffn_in, w_ffn_out, final_norm_g, loss_target, m_norm1_g, m_w_in, m_w_a2, m_b_a2, m_gla_norm_g, m_lam_re, m_lam_im, m_log_dt, m_s5_b_re, m_s5_b_im, m_s5_c_re, m_s5_c_im, m_s5_d, m_w_glu, m_b_glu, m_w_branch_a, m_w_branch_b, m_w_out, m_norm2_g, m_w_ffn_in, m_w_ffn_out, m_final_norm_g, v_norm1_g, v_w_in, v_w_a2, v_b_a2, v_gla_norm_g, v_lam_re, v_lam_im, v_log_dt, v_s5_b_re, v_s5_b_im, v_s5_c_re, v_s5_c_im, v_s5_d, v_w_glu, v_b_glu, v_w_branch_a, v_w_branch_b, v_w_out, v_norm2_g, v_w_ffn_in, v_w_ffn_out, v_final_norm_g):
    args = dict(locals())
    weights = {n: args[n] for n in _ORDER}
    m_in = {n: args["m_" + n] for n in _ORDER}
    v_in = {n: args["v_" + n] for n in _ORDER}
    d = x.shape[-1]

    w_in_t_shard = weights["w_in"][0].T.astype(BF16)
    w_in_t = _exchange("ag", [w_in_t_shard], "w_in_all_gather")[0].reshape(-1, d)
    rest = [weights[n][0].astype(BF16) for n, _ in _REST]
    small = {n: _small_2d(n, weights[n]) for n in _SMALL}
    loss_local, grad_x, recv, small_g = _local_step(x[0], loss_target[0], w_in_t, small, rest)

    grads, delta, new_m, new_v = {}, {}, {}, {}
    for n, _ in _REST:
        grads[n], delta[n], new_m[n], new_v[n] = _adamw(weights[n][0], recv[n], m_in[n][0], v_in[n][0], "adamw_" + n)
    grads["w_in"] = _slot_sum(recv["w_in"], "w_in_grad_slot_sum").T
    delta["w_in"], new_m["w_in"], new_v["w_in"] = _adamw(weights["w_in"][0], grads["w_in"], m_in["w_in"][0], v_in["w_in"][0], "adamw_w_in")

    s_sizes = [small_g[n].size for n in _SMALL]
    s_offs = [sum(s_sizes[:i]) for i in range(len(s_sizes))]
    s_total = sum(s_sizes)
    s_rows = -(-(-(-(s_total + 1) // LANE)) // LANE) * LANE

    def pack_small(parts):
        flat = jnp.concatenate([p.reshape(-1) for p in parts])
        return jnp.pad(flat, (0, s_rows * LANE - flat.size)).reshape(s_rows, LANE)

    s_flat = pack_small([small_g[n] for n in _SMALL] + [loss_local])
    s_red = _slot_sum(_exchange("ag", [s_flat], "small_grads_all_gather")[0], "small_grads_slot_sum")
    loss = s_red.reshape(-1)[s_total]
    sd, sm, sv = _adamw(pack_small([weights[n] for n in _SMALL]), s_red, pack_small([m_in[n] for n in _SMALL]),
                        pack_small([v_in[n] for n in _SMALL]), "adamw_small")
    for n, o, s in zip(_SMALL, s_offs, s_sizes):
        shape = weights[n].shape[1:]
        grads[n], delta[n], new_m[n], new_v[n] = (a.reshape(-1)[o:o + s].reshape(shape) for a in (s_red, sd, sm, sv))

    out = [loss, grad_x[None]]
    for tree in (grads, delta, new_m, new_v):
        out += [tree[n].reshape(weights[n].shape) for n in _ORDER]
    return tuple(out)
```

```python
import functools
import math

import jax
import jax.numpy as jnp
from jax import lax
from jax.experimental import pallas as pl
from jax.experimental.pallas import tpu as pltpu

F32 = jnp.float32
BF16 = jnp.bfloat16

NORM_EPS = 1e-6
N_DEV = 8
N_PEER = N_DEV - 1
GLA_HEADS = 4
GLA_CHUNK = 32
GLA_CHUNK_SHIFT = 5
GLA_TAU = 16.0
GLA_RANK = 16
GLA_BLOCK = 256
S5_GC = 16
S5_P = 64
S5_L = 16
LANE = 128
V7X_VMEM_LIMIT = 56 * 1024 * 1024
V7X_MM_VMEM_BUDGET = 40 * 1024 * 1024
V7X_MM_TILE_MN = 1408
V7X_MM_TILE_K = 2048

ADAM_LR = 0.001
ADAM_B1 = 0.9
ADAM_B2 = 0.999
ADAM_EPS = 1e-08
ADAM_WD = 0.01
ADAM_STEP = 10

GELU_C = math.sqrt(2.0 / math.pi)
GELU_A = 0.044715

MESH = pl.DeviceIdType.MESH


def _cparams(*sem):
    return pltpu.CompilerParams(dimension_semantics=sem, vmem_limit_bytes=V7X_VMEM_LIMIT)


def _divisors_down(n, start, align=LANE):
    t = (min(start, n) // align) * align
    found = False
    while t >= align:
        if n % t == 0:
            found = True
            yield t
        t -= align
    if not found:
        yield n


def _tile(n, target, align=LANE):
    return next(_divisors_down(n, target, align))


def _sigmoid(x):
    return 1.0 / (1.0 + jnp.exp(-x))


_HBM_SPEC = pl.BlockSpec(memory_space=pltpu.HBM)


def _exchange_scratch(n):
    return [pltpu.SemaphoreType.DMA((n * N_PEER,)), pltpu.SemaphoreType.DMA((n * N_PEER,)), pltpu.SemaphoreType.DMA((n,))]


def _ag_phases(x_refs, out_refs, send_sems, recv_sems, local_sems):
    n = len(x_refs)
    x, y, c = lax.axis_index("x"), lax.axis_index("y"), lax.axis_index("c")
    me, sibling = (x, y, c), (x, y, 1 - c)
    chips = [(1 - x, y), (x, 1 - y), (1 - x, 1 - y)]

    def copy(a, k, block, to, from_input=False):
        dst = out_refs[a].at[4 * block[0] + 2 * block[1] + block[2]]
        return pltpu.make_async_remote_copy(
            src_ref=x_refs[a] if from_input else dst, dst_ref=dst,
            send_sem=send_sems.at[a * N_PEER + k], recv_sem=recv_sems.at[a * N_PEER + k], device_id=to, device_id_type=MESH)

    def local(a):
        return pltpu.make_async_copy(x_refs[a], out_refs[a].at[4 * x + 2 * y + c], local_sems.at[a])

    def first(a):
        return [copy(a, 0, me, sibling, True)] + [copy(a, 1 + j, me, (*chip, c), True) for j, chip in enumerate(chips)]

    def start():
        for a in range(n):
            local(a).start()
            for cp in first(a):
                cp.start()

    def relay():
        for j, chip in enumerate(chips):
            for a in range(n):
                copy(a, 1 + j, (*chip, c), me).wait_recv()
                copy(a, 4 + j, (*chip, c), sibling).start()

    def finish():
        for a in range(n):
            copy(a, 0, sibling, me).wait_recv()
            for j, chip in enumerate(chips):
                copy(a, 4 + j, (*chip, 1 - c), me).wait_recv()
        for a in range(n):
            for cp in first(a) + [copy(a, 4 + j, (*chip, c), sibling) for j, chip in enumerate(chips)]:
                cp.wait_send()
            local(a).wait()

    return start, relay, finish


def _a2a_phases(x_refs, out_refs, send_sems, recv_sems, local_sems):
    n = len(x_refs)
    x, y, c = lax.axis_index("x"), lax.axis_index("y"), lax.axis_index("c")
    my = 4 * x + 2 * y + c

    def copy(a, k, incoming):
        px, py, pc = (1 - x if k & 4 else x), (1 - y if k & 2 else y), (1 - c if k & 1 else c)
        pidx = 4 * px + 2 * py + pc
        return pltpu.make_async_remote_copy(
            src_ref=x_refs[a].at[pidx], dst_ref=out_refs[a].at[pidx if incoming else my],
            send_sem=send_sems.at[a * N_PEER + k - 1], recv_sem=recv_sems.at[a * N_PEER + k - 1],
            device_id=(px, py, pc), device_id_type=MESH)

    def local(a):
        return pltpu.make_async_copy(x_refs[a].at[my], out_refs[a].at[my], local_sems.at[a])

    def start():
        for a in range(n):
            local(a).start()
            for k in range(1, N_DEV):
                copy(a, k, False).start()

    def relay():
        pass

    def finish():
        for a in range(n):
            for k in range(1, N_DEV):
                copy(a, k, True).wait_recv()
        for a in range(n):
            for k in range(1, N_DEV):
                copy(a, k, False).wait_send()
            local(a).wait()

    return start, relay, finish


def _exchange_out_shapes(kind, arrays):
    if kind == "ag":
        return [jax.ShapeDtypeStruct((N_DEV,) + a.shape, a.dtype) for a in arrays]
    return [jax.ShapeDtypeStruct(a.shape, a.dtype) for a in arrays]


def _exchange(kind, arrays, name):
    n = len(arrays)
    phases = _ag_phases if kind == "ag" else _a2a_phases

    def body(*refs):
        start, relay, finish = phases(refs[:n], refs[n:2 * n], *refs[2 * n:])
        start()
        relay()
        finish()

    return pl.pallas_call(
        body, name=name,
        out_shape=_exchange_out_shapes(kind, arrays),
        in_specs=[_HBM_SPEC] * n, out_specs=[_HBM_SPEC] * n,
        scratch_shapes=_exchange_scratch(n),
    )(*arrays)


def _mm_tiles(m, n_unit, k_unit, out_bytes, res_bytes):
    tm = _tile(m, V7X_MM_TILE_MN)
    tn = _tile(n_unit, V7X_MM_TILE_MN)
    for tk in _divisors_down(k_unit, V7X_MM_TILE_K):
        need = 2 * 2 * (tm * tk + tk * tn) + (4 + 2 * out_bytes + 2 * res_bytes) * tm * tn
        if need <= V7X_MM_VMEM_BUDGET:
            return tm, tn, tk
    return tm, tn, _tile(k_unit, LANE)


def _mm(a, b, *, ta=False, tb=False, out_dtype=F32, res=None, carry=None, b_slots=False, out_slots=0, name):
    m, k = (a.shape[1], a.shape[0]) if ta else a.shape
    if b_slots:
        n_slot, b_r, b_c = b.shape
        k2, n = (n_slot * b_c, b_r) if tb else (b_r, n_slot * b_c)
    else:
        k2, n = (b.shape[1], b.shape[0]) if tb else b.shape
    assert k == k2, (a.shape, b.shape, ta, tb)
    assert not (b_slots and not tb and out_slots)
    has_res = res is not None
    n_unit = n // out_slots if out_slots else (b.shape[2] if b_slots and not tb else n)
    k_unit = b.shape[2] if b_slots and tb else k
    tm, tn, tk = _mm_tiles(m, n_unit, k_unit, jnp.dtype(out_dtype).itemsize, res.dtype.itemsize if has_res else 0)
    ni, nj, nk = m // tm, n // tn, k // tk
    n_per, k_per = n_unit // tn, k_unit // tk
    dims = (((0,) if ta else (1,), (1,) if tb else (0,)), ((), ()))
    a_spec = pl.BlockSpec((tk, tm), lambda i, j, kk: (kk, i)) if ta else pl.BlockSpec((tm, tk), lambda i, j, kk: (i, kk))
    if b_slots and tb:
        b_spec = pl.BlockSpec((None, tn, tk), lambda i, j, kk: (kk // k_per, j, kk % k_per))
    elif b_slots:
        b_spec = pl.BlockSpec((None, tk, tn), lambda i, j, kk: (j // n_per, kk, j % n_per))
    else:
        b_spec = pl.BlockSpec((tn, tk), lambda i, j, kk: (j, kk)) if tb else pl.BlockSpec((tk, tn), lambda i, j, kk: (kk, j))
    if out_slots:
        o_spec = pl.BlockSpec((None, tm, tn), lambda i, j, kk: (j // n_per, i, j % n_per))
    else:
        o_spec = pl.BlockSpec((tm, tn), lambda i, j, kk: (i, j))
    o_shape = (out_slots, m, n_unit) if out_slots else (m, n)
    n_in = 3 if has_res else 2
    c_kind, c_arrays = carry if carry is not None else (None, [])
    nc = len(c_arrays)
    last_step = ni * nj * nk - 1

    def body(*refs):
        a_ref, b_ref = refs[0], refs[1]
        r_ref = refs[2] if has_res else None
        x_refs = refs[n_in:n_in + nc]
        o_ref = refs[n_in + nc]
        out_refs = refs[n_in + nc + 1:n_in + 2 * nc + 1]
        scratch = refs[n_in + 2 * nc + 1:]
        acc = scratch[0] if nk > 1 else None
        kk = pl.program_id(2)
        step = (pl.program_id(0) * nj + pl.program_id(1)) * nk + kk
        if nc:
            phases = _ag_phases if c_kind == "ag" else _a2a_phases
            start, relay, finish = phases(x_refs, out_refs, *scratch[-3:])
            pl.when(step == 0)(start)

        def emit(val):
            if has_res:
                val = val + r_ref[...].astype(F32)
            o_ref[...] = val.astype(out_dtype)

        part = lax.dot_general(a_ref[...], b_ref[...], dims, preferred_element_type=F32)
        if nk == 1:
            emit(part)
        else:
            @pl.when(kk == 0)
            def _():
                acc[...] = part

            @pl.when(kk > 0)
            def _():
                acc[...] += part

            @pl.when(kk == nk - 1)
            def _():
                emit(acc[...])

        if nc:
            if c_kind == "ag":
                pl.when(step == last_step // 2)(relay)
            pl.when(step == last_step)(finish)

    scratch_shapes = ([pltpu.VMEM((tm, tn), F32)] if nk > 1 else []) + (_exchange_scratch(nc) if nc else [])
    sem = ("arbitrary",) * 3 if nc else ("parallel", "parallel", "arbitrary")
    outs = pl.pallas_call(
        body, name=name,
        grid=(ni, nj, nk),
        in_specs=[a_spec, b_spec] + ([o_spec] if has_res else []) + [_HBM_SPEC] * nc,
        out_specs=[o_spec] + [_HBM_SPEC] * nc,
        out_shape=[jax.ShapeDtypeStruct(o_shape, out_dtype)] + _exchange_out_shapes(c_kind, c_arrays),
        scratch_shapes=scratch_shapes,
        compiler_params=_cparams(*sem),
    )(*((a, b, res) if has_res else (a, b)), *c_arrays)
    return (outs[0], list(outs[1:])) if nc else outs[0]


def _rowwise(fn, ins, row_outs, acc_outs, *, rows, tb, name):
    in_specs, args = [], []
    for spec in ins:
        kind, arr = spec[0], spec[1]
        if kind == "row":
            in_specs.append(pl.BlockSpec((tb, arr.shape[1]), lambda i: (i, 0)))
        elif kind == "win":
            width, cb = spec[2], spec[3]
            in_specs.append(pl.BlockSpec((tb, width), functools.partial(lambda i, cb: (i, cb), cb=cb)))
        else:
            in_specs.append(pl.BlockSpec(arr.shape, lambda i: (0, 0)))
        args.append(arr)
    out_specs = [pl.BlockSpec((tb, c), lambda i: (i, 0)) for c, _ in row_outs]
    out_specs += [pl.BlockSpec(shape, lambda i: (0, 0)) for shape in acc_outs]
    out_shape = [jax.ShapeDtypeStruct((rows, c), dt) for c, dt in row_outs]
    out_shape += [jax.ShapeDtypeStruct(shape, F32) for shape in acc_outs]
    n_in, n_row = len(ins), len(row_outs)

    def body(*refs):
        vals = [r[...] for r in refs[:n_in]]
        outs = fn(*vals)
        if not isinstance(outs, (tuple, list)):
            outs = (outs,)
        out_refs = refs[n_in:]
        for o_ref, val in zip(out_refs[:n_row], outs[:n_row]):
            o_ref[...] = val.astype(o_ref.dtype)
        first = pl.program_id(0) == 0
        for o_ref, val in zip(out_refs[n_row:], outs[n_row:]):
            @pl.when(first)
            def _(o_ref=o_ref):
                o_ref[...] = jnp.zeros_like(o_ref)
            o_ref[...] += val

    res = pl.pallas_call(
        body, name=name,
        grid=(rows // tb,),
        in_specs=in_specs, out_specs=out_specs, out_shape=out_shape,
        compiler_params=_cparams("arbitrary"),
    )(*args)
    return res


def _rms_fwd(x, g, name):
    def fn(xv, gv):
        r = lax.rsqrt(jnp.mean(xv * xv, axis=-1, keepdims=True) + NORM_EPS)
        return (xv * r * gv,)
    return _rowwise(fn, [("row", x), ("full", g)], [(x.shape[1], BF16)], [], rows=x.shape[0], tb=_tile(x.shape[0], 512), name=name)[0]


def _rms_bwd(x, g, dh, dres, name, want_bf16):
    d = x.shape[1]

    def fn(xv, gv, dhv, drv):
        r = lax.rsqrt(jnp.mean(xv * xv, axis=-1, keepdims=True) + NORM_EPS)
        xhat = xv * r
        dhv = dhv.astype(F32)
        dxhat = dhv * gv
        dx = drv + r * (dxhat - xhat * jnp.mean(dxhat * xhat, axis=-1, keepdims=True))
        dg = jnp.sum(dhv * xhat, axis=0, keepdims=True)
        return (dx, dx, dg) if want_bf16 else (dx, dg)

    row_outs = [(d, F32), (d, BF16)] if want_bf16 else [(d, F32)]
    return _rowwise(fn, [("row", x), ("full", g), ("row", dh), ("row", dres)], row_outs, [(1, d)],
                    rows=x.shape[0], tb=_tile(x.shape[0], 256), name=name)


def _loss_head(x2, g, target, name):
    d = x2.shape[1]

    def fn(xv, gv, tv):
        r = lax.rsqrt(jnp.mean(xv * xv, axis=-1, keepdims=True) + NORM_EPS)
        xhat = xv * r
        diff = xhat * gv - tv
        loss = 0.5 * jnp.sum(jnp.mean(diff * diff, axis=-1, keepdims=True), axis=0, keepdims=True)
        dy = diff * (1.0 / d)
        dxhat = dy * gv
        dx = r * (dxhat - xhat * jnp.mean(dxhat * xhat, axis=-1, keepdims=True))
        dg = jnp.sum(dy * xhat, axis=0, keepdims=True)
        return dx, dx, dg, jnp.broadcast_to(loss, (1, LANE))

    return _rowwise(fn, [("row", x2), ("full", g), ("row", target)], [(d, F32), (d, BF16)], [(1, d), (1, LANE)],
                    rows=x2.shape[0], tb=_tile(x2.shape[0], 256), name=name)


def _swiglu_fwd(gu, name):
    c = gu.shape[1] // 2

    def fn(gv, uv):
        gv = gv.astype(F32)
        return (gv * _sigmoid(gv) * uv.astype(F32),)
    return _rowwise(fn, [("win", gu, c, 0), ("win", gu, c, 1)], [(c, BF16)], [], rows=gu.shape[0],
                    tb=_tile(gu.shape[0], 256), name=name)[0]


def _swiglu_bwd(dact, gu, name):
    c = gu.shape[1] // 2

    def fn(dv, gv, uv):
        dv, gv, uv = dv.astype(F32), gv.astype(F32), uv.astype(F32)
        s = _sigmoid(gv)
        dgate = dv * uv * (s * (1.0 + gv * (1.0 - s)))
        dup = dv * (gv * s)
        return (jnp.concatenate([dgate.astype(BF16), dup.astype(BF16)], axis=1),)
    return _rowwise(fn, [("row", dact), ("win", gu, c, 0), ("win", gu, c, 1)], [(2 * c, BF16)], [], rows=gu.shape[0],
                    tb=_tile(gu.shape[0], 256), name=name)[0]


def _mix_fwd(proj, pa, pb, d, name):
    def fn(ga, gb, av, bv):
        return (_sigmoid(ga.astype(F32)) * av.astype(F32) + _sigmoid(gb.astype(F32)) * bv.astype(F32),)
    return _rowwise(fn, [("win", proj, d, 0), ("win", proj, d, 1), ("row", pa), ("row", pb)], [(d, BF16)], [],
                    rows=pa.shape[0], tb=_tile(pa.shape[0], 512), name=name)[0]


def _mix_bwd(proj, pa, pb, dmix, d, name):
    def fn(ga, gb, av, bv, dm):
        dm = dm.astype(F32)
        sa, sb = _sigmoid(ga.astype(F32)), _sigmoid(gb.astype(F32))
        av, bv = av.astype(F32), bv.astype(F32)
        return dm * sa, dm * sb, dm * av * sa * (1.0 - sa), dm * bv * sb * (1.0 - sb)
    return _rowwise(fn, [("win", proj, d, 0), ("win", proj, d, 1), ("row", pa), ("row", pb), ("row", dmix)],
                    [(d, BF16)] * 4, [], rows=pa.shape[0], tb=_tile(pa.shape[0], 512), name=name)


def _chunk_masks(tb):
    r = lax.broadcasted_iota(jnp.int32, (tb, tb), 0)
    c = lax.broadcasted_iota(jnp.int32, (tb, tb), 1)
    same = lax.shift_right_logical(r, GLA_CHUNK_SHIFT) == lax.shift_right_logical(c, GLA_CHUNK_SHIFT)
    return same, same & (c <= r), same & (r <= c)


def _mask_bf16(mask):
    return jnp.where(mask, 1.0, 0.0).astype(BF16)


def _split_dot(mask_bf, x, terms):
    acc, rem = None, x
    for _ in range(terms):
        hi = rem.astype(BF16)
        part = jnp.dot(mask_bf, hi, preferred_element_type=F32)
        acc = part if acc is None else acc + part
        rem = rem - hi.astype(F32)
    return acc


def _gla_decay(al, wa2, ba2, same_bf, causal_bf):
    z = jnp.dot(al.astype(BF16), wa2, preferred_element_type=F32) + ba2
    la = (jnp.minimum(z, 0.0) - jnp.log(1.0 + jnp.exp(-jnp.abs(z)))) * (1.0 / GLA_TAU)
    bc = _split_dot(causal_bf, la, 3)
    bl = _split_dot(same_bf, la, 3)
    return z, bc, bl


def _dot_t(a, b, ca, cb):
    return lax.dot_general(a, b, (((ca,), (cb,)), ((), ())), preferred_element_type=F32)


def _gla_fwd(proj, alow, wa2, ba2, ghn, *, dk, dv, name):
    t = proj.shape[0]
    tb = min(GLA_BLOCK, t)
    nch = tb // GLA_CHUNK
    hk, hv = dk // GLA_HEADS, dv // GLA_HEADS
    scale = hk ** -0.5
    v_cb, r_cb = (8 * dk) // dv, (8 * dk) // dv + 1
    q_cb, k_cb = (8 * dk + 2 * dv) // dk, (8 * dk + 2 * dv) // dk + 1

    def body(q_ref, k_ref, v_ref, r_ref, al_ref, wa2_ref, ba2_ref, ghn_ref, oa_ref, opre_ref, s_ref, st_scr):
        @pl.when(pl.program_id(0) == 0)
        def _():
            st_scr[...] = jnp.zeros_like(st_scr)

        same, causal, _ = _chunk_masks(tb)
        same_bf, causal_bf = _mask_bf16(same), _mask_bf16(causal)
        _, bc, bl = _gla_decay(al_ref[...], wa2_ref[...], ba2_ref[...], same_bf, causal_bf)
        q = q_ref[...].astype(F32) * scale
        k = k_ref[...].astype(F32)
        qd = (q * jnp.exp(bc)).astype(BF16)
        ki = (k * jnp.exp(-bc)).astype(BF16)
        ks = (k * jnp.exp(bl - bc)).astype(BF16)
        dl = jnp.exp(bl)
        for h in range(GLA_HEADS):
            ksl = slice(h * hk, (h + 1) * hk)
            vsl = slice(h * hv, (h + 1) * hv)
            v_h = v_ref[:, vsl]
            sc = jnp.where(causal, _dot_t(qd[:, ksl], ki[:, ksl], 1, 1), 0.0)
            o_intra = jnp.dot(sc.astype(BF16), v_h, preferred_element_type=F32)
            for c in range(nch):
                rows = slice(c * GLA_CHUNK, (c + 1) * GLA_CHUNK)
                st = st_scr[h]
                s_ref[c, h] = st
                o_c = o_intra[rows] + _dot_t(qd[rows, ksl], st.astype(BF16), 1, 1)
                opre_ref[rows, vsl] = o_c
                st_scr[h] = dl[c * GLA_CHUNK:c * GLA_CHUNK + 1, ksl] * st + _dot_t(v_h[rows], ks[rows, ksl], 0, 0)
        for h in range(GLA_HEADS):
            vsl = slice(h * hv, (h + 1) * hv)
            o = opre_ref[:, vsl]
            rs = lax.rsqrt(jnp.mean(o * o, axis=-1, keepdims=True) + NORM_EPS)
            rv = r_ref[:, vsl].astype(F32)
            oa_ref[:, vsl] = (rv * _sigmoid(rv) * (o * rs * ghn_ref[:, vsl])).astype(BF16)

    nchunks = t // GLA_CHUNK
    return pl.pallas_call(
        body, name=name,
        grid=(t // tb,),
        in_specs=[
            pl.BlockSpec((tb, dk), lambda i: (i, q_cb)),
            pl.BlockSpec((tb, dk), lambda i: (i, k_cb)),
            pl.BlockSpec((tb, dv), lambda i: (i, v_cb)),
            pl.BlockSpec((tb, dv), lambda i: (i, r_cb)),
            pl.BlockSpec((tb, LANE), lambda i: (i, 0)),
            pl.BlockSpec(wa2.shape, lambda i: (0, 0)),
            pl.BlockSpec(ba2.shape, lambda i: (0, 0)),
            pl.BlockSpec(ghn.shape, lambda i: (0, 0)),
        ],
        out_specs=[
            pl.BlockSpec((tb, dv), lambda i: (i, 0)),
            pl.BlockSpec((tb, dv), lambda i: (i, 0)),
            pl.BlockSpec((nch, GLA_HEADS, hv, hk), lambda i: (i, 0, 0, 0)),
        ],
        out_shape=[
            jax.ShapeDtypeStruct((t, dv), BF16),
            jax.ShapeDtypeStruct((t, dv), F32),
            jax.ShapeDtypeStruct((nchunks, GLA_HEADS, hv, hk), F32),
        ],
        scratch_shapes=[pltpu.VMEM((GLA_HEADS, hv, hk), F32)],
        compiler_params=_cparams("arbitrary"),
    )(proj, proj, proj, proj, alow, wa2, ba2, ghn)


def _gla_bwd(proj, alow, wa2, ba2, ghn, opre, states, doa, *, dk, dv, name):
    t = proj.shape[0]
    tb = min(GLA_BLOCK, t)
    nb = t // tb
    nch = tb // GLA_CHUNK
    hk, hv = dk // GLA_HEADS, dv // GLA_HEADS
    scale = hk ** -0.5
    v_cb, r_cb = (8 * dk) // dv, (8 * dk) // dv + 1
    q_cb, k_cb = (8 * dk + 2 * dv) // dk, (8 * dk + 2 * dv) // dk + 1

    def body(q_ref, k_ref, v_ref, r_ref, al_ref, wa2_ref, ba2_ref, ghn_ref, opre_ref, s_ref, doa_ref,
             dq_ref, dk_ref, dv_ref, dr_ref, dal_ref, dwa2_ref, dba2_ref, dghn_ref,
             dst_scr, dqd_scr, dki_scr, dks_scr, ddl_scr):
        @pl.when(pl.program_id(0) == 0)
        def _():
            dst_scr[...] = jnp.zeros_like(dst_scr)
            dwa2_ref[...] = jnp.zeros_like(dwa2_ref)
            dba2_ref[...] = jnp.zeros_like(dba2_ref)
            dghn_ref[...] = jnp.zeros_like(dghn_ref)

        same, causal, anti = _chunk_masks(tb)
        same_bf, causal_bf, anti_bf = _mask_bf16(same), _mask_bf16(causal), _mask_bf16(anti)
        al = al_ref[...]
        wa2v = wa2_ref[...]
        z, bc, bl = _gla_decay(al, wa2v, ba2_ref[...], same_bf, causal_bf)
        e_bc, e_nbc, e_st = jnp.exp(bc), jnp.exp(-bc), jnp.exp(bl - bc)
        q = q_ref[...].astype(F32) * scale
        k = k_ref[...].astype(F32)
        qd_f, ki_f, ks_f = q * e_bc, k * e_nbc, k * e_st
        qd, ki, ks = qd_f.astype(BF16), ki_f.astype(BF16), ks_f.astype(BF16)
        dl = jnp.exp(bl)
        for h in range(GLA_HEADS):
            ksl = slice(h * hk, (h + 1) * hk)
            vsl = slice(h * hv, (h + 1) * hv)
            o = opre_ref[:, vsl]
            rs = lax.rsqrt(jnp.mean(o * o, axis=-1, keepdims=True) + NORM_EPS)
            ohat = o * rs
            g_h = ghn_ref[:, vsl]
            rv = r_ref[:, vsl].astype(F32)
            sg = _sigmoid(rv)
            d_oa = doa_ref[:, vsl].astype(F32)
            don = d_oa * (rv * sg)
            dr_ref[:, vsl] = (d_oa * (ohat * g_h) * (sg * (1.0 + rv * (1.0 - sg)))).astype(BF16)
            dghn_ref[:, vsl] += jnp.sum(don * ohat, axis=0, keepdims=True)
            dohat = don * g_h
            do_f = rs * (dohat - ohat * jnp.mean(dohat * ohat, axis=-1, keepdims=True))
            do = do_f.astype(BF16)
            v_h = v_ref[:, vsl]
            p = jnp.where(causal, _dot_t(do, v_h, 1, 1), 0.0).astype(BF16)
            dqd_intra = jnp.dot(p, ki[:, ksl], preferred_element_type=F32)
            dki_scr[:, ksl] = _dot_t(p, qd[:, ksl], 0, 0)
            sc = jnp.where(causal, _dot_t(qd[:, ksl], ki[:, ksl], 1, 1), 0.0).astype(BF16)
            dv_intra = _dot_t(sc, do, 0, 0)
            for c in reversed(range(nch)):
                rows = slice(c * GLA_CHUNK, (c + 1) * GLA_CHUNK)
                dst = dst_scr[h]
                st = s_ref[c, h]
                dst_bf = dst.astype(BF16)
                dv_ref[rows, vsl] = (dv_intra[rows] + _dot_t(ks[rows, ksl], dst_bf, 1, 1)).astype(BF16)
                dks_scr[rows, ksl] = jnp.dot(v_h[rows], dst_bf, preferred_element_type=F32)
                dl_c = dl[c * GLA_CHUNK:c * GLA_CHUNK + 1, ksl]
                ddl = jnp.sum(dst * st, axis=0, keepdims=True) * dl_c
                ddl_scr[rows, ksl] = jnp.broadcast_to(ddl, (GLA_CHUNK, hk))
                dqd_scr[rows, ksl] = dqd_intra[rows] + jnp.dot(do[rows], st.astype(BF16), preferred_element_type=F32)
                dst_scr[h] = dl_c * dst + _dot_t(do[rows], qd[rows, ksl], 0, 0)
        dqd, dki, dks = dqd_scr[...], dki_scr[...], dks_scr[...]
        dq_ref[...] = (dqd * (scale * e_bc)).astype(BF16)
        dk_ref[...] = (dki * e_nbc + dks * e_st).astype(BF16)
        dks_ks = dks * ks_f
        dbc = dqd * qd_f - dki * ki_f - dks_ks
        dla = _split_dot(anti_bf, dbc, 2) + _split_dot(same_bf, dks_ks, 2) + ddl_scr[...]
        dz = (dla * (1.0 / GLA_TAU) * (1.0 - _sigmoid(z)))
        dz_bf = dz.astype(BF16)
        dal_ref[...] = _dot_t(dz_bf, wa2v, 1, 1).astype(BF16)
        dwa2_ref[...] += _dot_t(al.astype(BF16), dz_bf, 0, 0)
        dba2_ref[...] += jnp.sum(dz, axis=0, keepdims=True)

    rev = lambda i: nb - 1 - i
    return pl.pallas_call(
        body, name=name,
        grid=(nb,),
        in_specs=[
            pl.BlockSpec((tb, dk), lambda i: (rev(i), q_cb)),
            pl.BlockSpec((tb, dk), lambda i: (rev(i), k_cb)),
            pl.BlockSpec((tb, dv), lambda i: (rev(i), v_cb)),
            pl.BlockSpec((tb, dv), lambda i: (rev(i), r_cb)),
            pl.BlockSpec((tb, LANE), lambda i: (rev(i), 0)),
            pl.BlockSpec(wa2.shape, lambda i: (0, 0)),
            pl.BlockSpec(ba2.shape, lambda i: (0, 0)),
            pl.BlockSpec(ghn.shape, lambda i: (0, 0)),
            pl.BlockSpec((tb, dv), lambda i: (rev(i), 0)),
            pl.BlockSpec((nch, GLA_HEADS, hv, hk), lambda i: (rev(i), 0, 0, 0)),
            pl.BlockSpec((tb, dv), lambda i: (rev(i), 0)),
        ],
        out_specs=[
            pl.BlockSpec((tb, dk), lambda i: (rev(i), 0)),
            pl.BlockSpec((tb, dk), lambda i: (rev(i), 0)),
            pl.BlockSpec((tb, dv), lambda i: (rev(i), 0)),
            pl.BlockSpec((tb, dv), lambda i: (rev(i), 0)),
            pl.BlockSpec((tb, LANE), lambda i: (rev(i), 0)),
            pl.BlockSpec(wa2.shape, lambda i: (0, 0)),
            pl.BlockSpec(ba2.shape, lambda i: (0, 0)),
            pl.BlockSpec(ghn.shape, lambda i: (0, 0)),
        ],
        out_shape=[
            jax.ShapeDtypeStruct((t, dk), BF16),
            jax.ShapeDtypeStruct((t, dk), BF16),
            jax.ShapeDtypeStruct((t, dv), BF16),
            jax.ShapeDtypeStruct((t, dv), BF16),
            jax.ShapeDtypeStruct((t, LANE), BF16),
            jax.ShapeDtypeStruct(wa2.shape, F32),
            jax.ShapeDtypeStruct(ba2.shape, F32),
            jax.ShapeDtypeStruct(ghn.shape, F32),
        ],
        scratch_shapes=[pltpu.VMEM((GLA_HEADS, hv, hk), F32)] + [pltpu.VMEM((tb, dk), F32)] * 4,
        compiler_params=_cparams("arbitrary"),
    )(proj, proj, proj, proj, alow, wa2, ba2, ghn, opre, states, doa)


def _s5_tables(lam_re, lam_im, log_dt, b_re, b_im, c_re, c_im):
    hp = lax.Precision.HIGHEST
    g, p = lam_re.shape
    ln = S5_L
    dt = jnp.exp(log_dt)[:, None]
    lr, li = lam_re, lam_im
    mag = jnp.exp(lr * dt)
    ar, ai = mag * jnp.cos(li * dt), mag * jnp.sin(li * dt)
    den = lr * lr + li * li
    am1 = ar - 1.0
    f_re = ((am1 * lr + ai * li) / den)[..., None]
    f_im = ((ai * lr - am1 * li) / den)[..., None]
    bb_re = f_re * b_re - f_im * b_im
    bb_im = f_re * b_im + f_im * b_re
    j = jnp.arange(ln + 1, dtype=F32)[None, :, None]
    pm = jnp.exp(j * (lr * dt)[:, None, :])
    ang = j * (li * dt)[:, None, :]
    pw_re, pw_im = pm * jnp.cos(ang), pm * jnp.sin(ang)
    cp_re = c_re[:, None] * pw_re[:, :, None, :] - c_im[:, None] * pw_im[:, :, None, :]
    cp_im = c_re[:, None] * pw_im[:, :, None, :] + c_im[:, None] * pw_re[:, :, None, :]
    kj = (jnp.einsum("gjcp,gpd->gjcd", cp_re[:, :ln], bb_re, precision=hp)
          - jnp.einsum("gjcp,gpd->gjcd", cp_im[:, :ln], bb_im, precision=hp))
    s_i = jnp.arange(ln)[None, :, None]
    t_i = jnp.arange(ln)[None, None, :]
    j_i = jnp.arange(ln)[:, None, None]
    shift = (t_i - s_i == j_i).astype(F32)
    m = jnp.einsum("jst,gjcd->gsdtc", shift, kj, precision=hp).reshape(g, ln * S5_GC, ln * S5_GC)
    rp_re, rp_im = pw_re[:, ln - 1::-1], pw_im[:, ln - 1::-1]
    bbt_re, bbt_im = bb_re.transpose(0, 2, 1)[:, None], bb_im.transpose(0, 2, 1)[:, None]
    bst_re = rp_re[:, :, None, :] * bbt_re - rp_im[:, :, None, :] * bbt_im
    bst_im = rp_re[:, :, None, :] * bbt_im + rp_im[:, :, None, :] * bbt_re
    bst = jnp.concatenate([bst_re, bst_im], axis=-1).reshape(g, ln * S5_GC, 2 * p)
    cst = jnp.concatenate([cp_re[:, 1:].transpose(0, 3, 1, 2), -cp_im[:, 1:].transpose(0, 3, 1, 2)], axis=1)
    cst = cst.reshape(g, 2 * p, ln * S5_GC)
    a = jnp.stack([jnp.concatenate([pw_re[:, ln], pw_re[:, ln]], axis=-1),
                   jnp.concatenate([-pw_im[:, ln], pw_im[:, ln]], axis=-1)], axis=1)
    return m, bst, cst, a


def _state_scan(v, pr, pi, reverse):
    n = v.shape[0]
    half = v.shape[1] // 2
    row = lax.broadcasted_iota(jnp.int32, v.shape, 0)
    z, s = v, 1
    while s < n:
        if reverse:
            zs = jnp.where(row < n - s, pltpu.roll(z, n - s, 0), 0.0)
        else:
            zs = jnp.where(row >= s, pltpu.roll(z, s, 0), 0.0)
        z = z + zs * pr + pltpu.roll(zs, half, 1) * pi
        pr, pi = pr * pr - pi * pi, 2.0 * pr * pi
        s *= 2
    return z


def _s5_core_fwd(u_g, m, bst, cst, a, name):
    g, nc, w = u_g.shape
    p2 = bst.shape[2]

    def body(u_ref, m_ref, b_ref, c_ref, a_ref, y_ref, x_ref):
        u = u_ref[0]
        v = jnp.dot(u, b_ref[0], preferred_element_type=F32)
        z = _state_scan(v, a_ref[0, 0:1, :], a_ref[0, 1:2, :], reverse=False)
        row = lax.broadcasted_iota(jnp.int32, z.shape, 0)
        x = jnp.where(row >= 1, pltpu.roll(z, 1, 0), 0.0)
        x_ref[0] = x
        y_ref[0] = (jnp.dot(u, m_ref[0], preferred_element_type=F32)
                    + jnp.dot(x.astype(BF16), c_ref[0], preferred_element_type=F32))

    per_g = lambda shape: pl.BlockSpec((1,) + shape, lambda i: (i, 0, 0))
    return pl.pallas_call(
        body, name=name, grid=(g,),
        in_specs=[per_g((nc, w)), per_g((w, w)), per_g((w, p2)), per_g((p2, w)), per_g((2, p2))],
        out_specs=[per_g((nc, w)), per_g((nc, p2))],
        out_shape=[jax.ShapeDtypeStruct((g, nc, w), F32), jax.ShapeDtypeStruct((g, nc, p2), F32)],
        compiler_params=_cparams("parallel"),
    )(u_g, m, bst, cst, a)


def _s5_core_bwd(dy_g, u_g, x_g, m, bst, cst, a, name):
    g, nc, w = u_g.shape
    p2 = bst.shape[2]

    def body(dy_ref, u_ref, x_ref, m_ref, b_ref, c_ref, a_ref, du_ref, dm_ref, db_ref, dc_ref, da_ref):
        dy, u, x = dy_ref[0], u_ref[0], x_ref[0]
        gx = _dot_t(dy, c_ref[0], 1, 1)
        rtot = _state_scan(gx, a_ref[0, 0:1, :], -a_ref[0, 1:2, :], reverse=True)
        row = lax.broadcasted_iota(jnp.int32, rtot.shape, 0)
        dv = jnp.where(row < nc - 1, pltpu.roll(rtot, nc - 1, 0), 0.0)
        dv_bf = dv.astype(BF16)
        du_ref[0] = _dot_t(dy, m_ref[0], 1, 1) + _dot_t(dv_bf, b_ref[0], 1, 1)
        dm_ref[0] = _dot_t(u, dy, 0, 0)
        dc_ref[0] = _dot_t(x.astype(BF16), dy, 0, 0)
        db_ref[0] = _dot_t(u, dv_bf, 0, 0)
        x_sw = pltpu.roll(x, p2 // 2, 1)
        da_ref[0, 0:1, :] = jnp.sum(dv * x, axis=0, keepdims=True)
        da_ref[0, 1:2, :] = jnp.sum(dv * x_sw, axis=0, keepdims=True)

    per_g = lambda shape: pl.BlockSpec((1,) + shape, lambda i: (i, 0, 0))
    return pl.pallas_call(
        body, name=name, grid=(g,),
        in_specs=[per_g((nc, w)), per_g((nc, w)), per_g((nc, p2)), per_g((w, w)), per_g((w, p2)), per_g((p2, w)), per_g((2, p2))],
        out_specs=[per_g((nc, w)), per_g((w, w)), per_g((w, p2)), per_g((p2, w)), per_g((2, p2))],
        out_shape=[jax.ShapeDtypeStruct((g, nc, w), F32), jax.ShapeDtypeStruct((g, w, w), F32),
                   jax.ShapeDtypeStruct((g, w, p2), F32), jax.ShapeDtypeStruct((g, p2, w), F32),
                   jax.ShapeDtypeStruct((g, 2, p2), F32)],
        compiler_params=_cparams("parallel"),
    )(dy_g, u_g, x_g, m, bst, cst, a)


def _gelu_parts(y):
    inner = GELU_C * (y + GELU_A * y * y * y)
    th = jnp.tanh(inner)
    return th, 0.5 * y * (1.0 + th)


def _s5_post_fwd(y_raw, proj, u_cb, s5d, wglu, bglu, name):
    w = y_raw.shape[1]

    def fn(yr, u, dsk, wg, bg):
        y = yr + dsk * u.astype(F32)
        _, h = _gelu_parts(y)
        gl = jnp.dot(h.astype(BF16), wg, preferred_element_type=F32) + bg
        return (h * _sigmoid(gl),)

    return _rowwise(fn, [("row", y_raw), ("win", proj, w, u_cb), ("full", s5d), ("full", wglu), ("full", bglu)],
                    [(w, BF16)], [], rows=y_raw.shape[0], tb=_tile(y_raw.shape[0], 512), name=name)[0]


def _s5_post_bwd(y_raw, proj, u_cb, s5d, wglu, bglu, dob, name):
    w = y_raw.shape[1]

    def fn(yr, u, dsk, wg, bg, dov):
        u = u.astype(F32)
        dov = dov.astype(F32)
        y = yr + dsk * u
        th, h = _gelu_parts(y)
        h_bf = h.astype(BF16)
        gl = jnp.dot(h_bf, wg, preferred_element_type=F32) + bg
        sg = _sigmoid(gl)
        dgl = dov * h * sg * (1.0 - sg)
        dgl_bf = dgl.astype(BF16)
        dh = dov * sg + _dot_t(dgl_bf, wg, 1, 1)
        dgelu = 0.5 * (1.0 + th) + 0.5 * y * (1.0 - th * th) * GELU_C * (1.0 + 3.0 * GELU_A * y * y)
        dy = dh * dgelu
        return (dy, dy * dsk,
                _dot_t(h_bf, dgl_bf, 0, 0), jnp.sum(dgl, axis=0, keepdims=True), jnp.sum(dy * u, axis=0, keepdims=True))

    return _rowwise(fn, [("row", y_raw), ("win", proj, w, u_cb), ("full", s5d), ("full", wglu), ("full", bglu), ("row", dob)],
                    [(w, BF16), (w, F32)], [(w, w), (1, w), (1, w)], rows=y_raw.shape[0], tb=_tile(y_raw.shape[0], 512), name=name)


def _to_groups(a, dtype):
    t, w = a.shape
    g = w // S5_GC
    return a.reshape(t // S5_L, S5_L, g, S5_GC).transpose(2, 0, 1, 3).reshape(g, t // S5_L, S5_L * S5_GC).astype(dtype)


def _from_groups(a):
    g, nc, _ = a.shape
    return a.reshape(g, nc, S5_L, S5_GC).transpose(1, 2, 0, 3).reshape(nc * S5_L, g * S5_GC)


def _adamw(w, g, m, v, name):
    rows, cols = w.shape
    tb = _tile(rows, 256, align=16)
    slots = g.ndim == 3
    c1 = 1.0 - ADAM_B1 ** ADAM_STEP
    c2 = 1.0 - ADAM_B2 ** ADAM_STEP

    def body(w_ref, g_ref, m_ref, v_ref, *out_refs):
        if slots:
            gv = g_ref[0].astype(F32)
            for s in range(1, N_DEV):
                gv = gv + g_ref[s].astype(F32)
            out_refs[0][...] = gv
        else:
            gv = g_ref[...]
        d_ref, nm_ref, nv_ref = out_refs[-3:]
        nm = ADAM_B1 * m_ref[...] + (1.0 - ADAM_B1) * gv
        nv = ADAM_B2 * v_ref[...] + (1.0 - ADAM_B2) * (gv * gv)
        d_ref[...] = -ADAM_LR * ((nm / c1) / (jnp.sqrt(nv / c2) + ADAM_EPS) + ADAM_WD * w_ref[...])
        nm_ref[...] = nm
        nv_ref[...] = nv

    spec = pl.BlockSpec((tb, cols), lambda i: (i, 0))
    g_spec = pl.BlockSpec((N_DEV, tb, cols), lambda i: (0, i, 0)) if slots else spec
    n_out = 4 if slots else 3
    return pl.pallas_call(
        body, name=name, grid=(rows // tb,),
        in_specs=[spec, g_spec, spec, spec], out_specs=[spec] * n_out,
        out_shape=[jax.ShapeDtypeStruct((rows, cols), F32)] * n_out,
        compiler_params=_cparams("parallel"),
    )(w, g, m, v)


def _slot_sum(x, name):
    _, rows, cols = x.shape
    if rows % 8 == 0:
        tr, tc = _tile(rows, 512, align=8), cols
    else:
        tr, tc = rows, _tile(cols, 256)

    def body(x_ref, o_ref):
        acc = x_ref[0].astype(F32)
        for s in range(1, N_DEV):
            acc = acc + x_ref[s].astype(F32)
        o_ref[...] = acc

    return pl.pallas_call(
        body, name=name, grid=(rows // tr, cols // tc),
        in_specs=[pl.BlockSpec((N_DEV, tr, tc), lambda i, j: (0, i, j))],
        out_specs=pl.BlockSpec((tr, tc), lambda i, j: (i, j)),
        out_shape=jax.ShapeDtypeStruct((rows, cols), F32),
        compiler_params=_cparams("parallel", "parallel"),
    )(x)


_REST = (("w_a2", 1), ("w_glu", 0), ("w_branch_a", 1), ("w_branch_b", 1), ("w_out", 0), ("w_ffn_in", 1), ("w_ffn_out", 0))
_SMALL = ("norm1_g", "b_a2", "gla_norm_g", "lam_re", "lam_im", "log_dt", "s5_b_re", "s5_b_im", "s5_c_re", "s5_c_im",
          "s5_d", "b_glu", "norm2_g", "final_norm_g")
_ORDER = ("norm1_g", "w_in", "w_a2", "b_a2", "gla_norm_g", "lam_re", "lam_im", "log_dt", "s5_b_re", "s5_b_im", "s5_c_re",
          "s5_c_im", "s5_d", "w_glu", "b_glu", "w_branch_a", "w_branch_b", "w_out", "norm2_g", "w_ffn_in", "w_ffn_out", "final_norm_g")


def _join_slots(slots, axis):
    _, r, c = slots.shape
    if axis == 0:
        return slots.reshape(N_DEV * r, c)
    return slots.transpose(1, 0, 2).reshape(r, N_DEV * c)


def _to_slots(full, axis):
    r, c = full.shape
    if axis == 0:
        return full.reshape(N_DEV, r // N_DEV, c)
    return full.reshape(r, N_DEV, c // N_DEV).transpose(1, 0, 2)


def _local_step(x, target, w_in_t, small, rest):
    t, d = x.shape
    dk, dv, s5w = d // 4, d // 2, d // 4
    dist = not isinstance(rest, dict)
    o_q, o_k, o_v, o_r, o_al = 0, dk, 2 * dk, 2 * dk + dv, 2 * dk + 2 * dv
    o_u = o_al + GLA_RANK
    o_ga, o_gb = o_u + s5w, o_u + s5w + d
    rows = lambda a, o, n: a[o:o + n]
    w_main_t = jnp.concatenate([rows(w_in_t, o_ga, d), rows(w_in_t, o_gb, d), rows(w_in_t, o_v, dv), rows(w_in_t, o_r, dv),
                                rows(w_in_t, o_q, dk), rows(w_in_t, o_k, dk), rows(w_in_t, o_u, s5w)], axis=0)
    w_al_t = jnp.pad(rows(w_in_t, o_al, GLA_RANK), ((0, LANE - GLA_RANK), (0, 0)))
    u_cb = (2 * d + 2 * dv + 2 * dk) // s5w

    h1 = _rms_fwd(x, small["norm1_g"], "norm1_fwd")
    if dist:
        proj, gathered = _mm(h1, w_main_t, tb=True, out_dtype=BF16, carry=("ag", rest[:-1]), name="in_proj")
        w = {n: _join_slots(g, ax) for (n, ax), g in zip(_REST[:-2], gathered[:-1])}
        w_ffn_in_s = gathered[-1]
    else:
        proj = _mm(h1, w_main_t, tb=True, out_dtype=BF16, name="in_proj")
        w = rest
        w_ffn_in_s = _to_slots(rest["w_ffn_in"], 1)
    wa2 = jnp.pad(w["w_a2"], ((0, LANE - GLA_RANK), (0, 0)))
    alow = _mm(h1, w_al_t, tb=True, out_dtype=BF16, name="in_proj_gate_rank")
    o_a, o_pre, states = _gla_fwd(proj, alow, wa2, small["b_a2"], small["gla_norm_g"], dk=dk, dv=dv, name="gla_fwd")

    s5_params = (small["lam_re"], small["lam_im"], small["log_dt"][0], small["s5_b_re"], small["s5_b_im"],
                 small["s5_c_re"], small["s5_c_im"])
    (tm, tbst, tcst, ta), tables_vjp = jax.vjp(_s5_tables, *s5_params)
    tm_bf, tbst_bf, tcst_bf = tm.astype(BF16), tbst.astype(BF16), tcst.astype(BF16)
    u_g = _to_groups(proj[:, u_cb * s5w:(u_cb + 1) * s5w], BF16)
    y_g, x_g = _s5_core_fwd(u_g, tm_bf, tbst_bf, tcst_bf, ta, "s5_core_fwd")
    y_raw = _from_groups(y_g)
    o_b = _s5_post_fwd(y_raw, proj, u_cb, small["s5_d"], w["w_glu"], small["b_glu"], "s5_post_fwd")

    pa = _mm(o_a, w["w_branch_a"], out_dtype=BF16, name="branch_a")
    pb = _mm(o_b, w["w_branch_b"], out_dtype=BF16, name="branch_b")
    mix = _mix_fwd(proj, pa, pb, d, "mix_fwd")
    x1 = _mm(mix, w["w_out"], res=x, name="out_proj")
    h2 = _rms_fwd(x1, small["norm2_g"], "norm2_fwd")
    if dist:
        gu, (w_ffn_out_s,) = _mm(h2, w_ffn_in_s, b_slots=True, out_dtype=BF16, carry=("ag", rest[-1:]), name="ffn_in")
        w_ffn_out = _join_slots(w_ffn_out_s, 0)
    else:
        gu = _mm(h2, w_ffn_in_s, b_slots=True, out_dtype=BF16, name="ffn_in")
        w_ffn_out = rest["w_ffn_out"]
    act = _swiglu_fwd(gu, "swiglu_fwd")
    x2 = _mm(act, w_ffn_out, res=x1, name="ffn_out")
    dx2, dx2_bf, d_final_g, loss = _loss_head(x2, small["final_norm_g"], target, "loss_head")

    recv = {}
    dact = _mm(dx2_bf, w_ffn_out, tb=True, out_dtype=BF16, name="d_act")
    g_ffn_out = _mm(act, dx2_bf, ta=True, out_dtype=BF16, name="g_w_ffn_out")
    dgu = _swiglu_bwd(dact, gu, "swiglu_bwd")
    if dist:
        g_ffn_in_s, (recv["w_ffn_out"],) = _mm(h2, dgu, ta=True, out_dtype=BF16, out_slots=N_DEV,
                                               carry=("a2a", [_to_slots(g_ffn_out, 0)]), name="g_w_ffn_in")
        dh2, (recv["w_ffn_in"],) = _mm(dgu, w_ffn_in_s, tb=True, b_slots=True, carry=("a2a", [g_ffn_in_s]), name="d_h2")
    else:
        g_ffn_in_s = _mm(h2, dgu, ta=True, out_dtype=BF16, out_slots=N_DEV, name="g_w_ffn_in")
        dh2 = _mm(dgu, w_ffn_in_s, tb=True, b_slots=True, name="d_h2")
    dx1, dx1_bf, d_norm2_g = _rms_bwd(x1, small["norm2_g"], dh2, dx2, "norm2_bwd", True)
    dmix = _mm(dx1_bf, w["w_out"], tb=True, out_dtype=BF16, name="d_mix")
    g_out = _mm(mix, dx1_bf, ta=True, out_dtype=BF16, name="g_w_out")
    dpa, dpb, dga, dgb = _mix_bwd(proj, pa, pb, dmix, d, "mix_bwd")
    doa = _mm(dpa, w["w_branch_a"], tb=True, out_dtype=BF16, name="d_o_a")
    dob = _mm(dpb, w["w_branch_b"], tb=True, out_dtype=BF16, name="d_o_b")
    g_branch_a = _mm(o_a, dpa, ta=True, out_dtype=BF16, name="g_w_branch_a")
    g_branch_b = _mm(o_b, dpb, ta=True, out_dtype=BF16, name="g_w_branch_b")

    dy_s5, du_direct, g_glu, g_bglu, g_s5d = _s5_post_bwd(y_raw, proj, u_cb, small["s5_d"], w["w_glu"], small["b_glu"], dob, "s5_post_bwd")
    du_g, d_tm, d_tbst, d_tcst, d_ta = _s5_core_bwd(_to_groups(dy_s5, BF16), u_g, x_g, tm_bf, tbst_bf, tcst_bf, ta, "s5_core_bwd")
    g_lam_re, g_lam_im, g_log_dt, g_b_re, g_b_im, g_c_re, g_c_im = tables_vjp((d_tm, d_tbst, d_tcst, d_ta))
    du = (_from_groups(du_g) + du_direct).astype(BF16)

    dq, dkk, dvv, dr, dal, g_wa2, g_ba2, g_ghn = _gla_bwd(proj, alow, wa2, small["b_a2"], small["gla_norm_g"], o_pre, states, doa,
                                                        dk=dk, dv=dv, name="gla_bwd")
    dproj = jnp.concatenate([dga, dgb, dvv, dr, dq, dkk, du], axis=1)
    mid = {"w_out": g_out, "w_branch_a": g_branch_a, "w_branch_b": g_branch_b, "w_glu": g_glu.astype(BF16),
           "w_a2": g_wa2[:GLA_RANK].astype(BF16)}
    if dist:
        axes = dict(_REST)
        g_main_t, got = _mm(dproj, h1, ta=True, out_dtype=BF16, carry=("a2a", [_to_slots(mid[n], axes[n]) for n in mid]),
                            name="g_w_in_main")
        recv.update(zip(mid, got))
    else:
        g_main_t = _mm(dproj, h1, ta=True, out_dtype=BF16, name="g_w_in_main")
    g_al_t = _mm(dal, h1, ta=True, out_dtype=BF16, name="g_w_in_gate_rank")
    mrows = lambda o, n: g_main_t[o:o + n]
    g_w_in_t = jnp.concatenate([mrows(2 * d + 2 * dv, dk), mrows(2 * d + 2 * dv + dk, dk), mrows(2 * d, dv), mrows(2 * d + dv, dv),
                                g_al_t[:GLA_RANK], mrows(2 * d + 2 * dv + 2 * dk, s5w), mrows(0, d), mrows(d, d)], axis=0)
    if dist:
        dh1, (recv["w_in"],) = _mm(dproj, w_main_t, carry=("a2a", [_to_slots(g_w_in_t, 0)]), name="d_h1_main")
    else:
        dh1 = _mm(dproj, w_main_t, name="d_h1_main")
    dh1 = _mm(dal, w_al_t, res=dh1, name="d_h1_gate_rank")
    grad_x, d_norm1_g = _rms_bwd(x, small["norm1_g"], dh1, dx1, "norm1_bwd", False)

    small_g = {
        "norm1_g": d_norm1_g, "b_a2": g_ba2, "gla_norm_g": g_ghn, "lam_re": g_lam_re, "lam_im": g_lam_im,
        "log_dt": g_log_dt[None], "s5_b_re": g_b_re, "s5_b_im": g_b_im, "s5_c_re": g_c_re, "s5_c_im": g_c_im,
        "s5_d": g_s5d, "b_glu": g_bglu, "norm2_g": d_norm2_g, "final_norm_g": d_final_g,
    }
    if not dist:
        recv = dict(mid, w_in=g_w_in_t, w_ffn_in=_join_slots(g_ffn_in_s, 1), w_ffn_out=g_ffn_out)
    return loss[0, 0], grad_x, recv, small_g


def _small_2d(name, a):
    a = a[0]
    return a[None] if a.ndim == 1 else a


def kernel(x, norm1_g, w_in, w_a2, b_a2, gla_norm_g, lam_re, lam_im, log_dt, s5_b_re, s5_b_im, s5_c_re, s5_c_im, s5_d, w_glu, b_glu, w_branch_a, w_branch_b, w_out, norm2_g, w_ffn_in, w_ffn_out, final_norm_g, loss_target, m_norm1_g, m_w_in, m_w_a2, m_b_a2, m_gla_norm_g, m_lam_re, m_lam_im, m_log_dt, m_s5_b_re, m_s5_b_im, m_s5_c_re, m_s5_c_im, m_s5_d, m_w_glu, m_b_glu, m_w_branch_a, m_w_branch_b, m_w_out, m_norm2_g, m_w_ffn_in, m_w_ffn_out, m_final_norm_g, v_norm1_g, v_w_in, v_w_a2, v_b_a2, v_gla_norm_g, v_lam_re, v_lam_im, v_log_dt, v_s5_b_re, v_s5_b_im, v_s5_c_re, v_s5_c_im, v_s5_d, v_w_glu, v_b_glu, v_w_branch_a, v_w_branch_b, v_w_out, v_norm2_g, v_w_ffn_in, v_w_ffn_out, v_final_norm_g):
    args = dict(locals())
    weights = {n: args[n] for n in _ORDER}
    m_in = {n: args["m_" + n] for n in _ORDER}
    v_in = {n: args["v_" + n] for n in _ORDER}
    d = x.shape[-1]

    w_in_t_shard = weights["w_in"][0].T.astype(BF16)
    w_in_t = _exchange("ag", [w_in_t_shard], "w_in_all_gather")[0].reshape(-1, d)
    rest = [weights[n][0].astype(BF16) for n, _ in _REST]
    small = {n: _small_2d(n, weights[n]) for n in _SMALL}
    loss_local, grad_x, recv, small_g = _local_step(x[0], loss_target[0], w_in_t, small, rest)

    grads, delta, new_m, new_v = {}, {}, {}, {}
    for n, _ in _REST:
        grads[n], delta[n], new_m[n], new_v[n] = _adamw(weights[n][0], recv[n], m_in[n][0], v_in[n][0], "adamw_" + n)
    grads["w_in"] = _slot_sum(recv["w_in"], "w_in_grad_slot_sum").T
    delta["w_in"], new_m["w_in"], new_v["w_in"] = _adamw(weights["w_in"][0], grads["w_in"], m_in["w_in"][0], v_in["w_in"][0], "adamw_w_in")

    s_sizes = [small_g[n].size for n in _SMALL]
    s_offs = [sum(s_sizes[:i]) for i in range(len(s_sizes))]
    s_total = sum(s_sizes)
    s_rows = -(-(-(-(s_total + 1) // LANE)) // LANE) * LANE

    def pack_small(parts):
        flat = jnp.concatenate([p.reshape(-1) for p in parts])
        return jnp.pad(flat, (0, s_rows * LANE - flat.size)).reshape(s_rows, LANE)

    s_flat = pack_small([small_g[n] for n in _SMALL] + [loss_local])
    s_red = _slot_sum(_exchange("ag", [s_flat], "small_grads_all_gather")[0], "small_grads_slot_sum")
    loss = s_red.reshape(-1)[s_total]
    sd, sm, sv = _adamw(pack_small([weights[n] for n in _SMALL]), s_red, pack_small([m_in[n] for n in _SMALL]),
                        pack_small([v_in[n] for n in _SMALL]), "adamw_small")
    for n, o, s in zip(_SMALL, s_offs, s_sizes):
        shape = weights[n].shape[1:]
        grads[n], delta[n], new_m[n], new_v[n] = (a.reshape(-1)[o:o + s].reshape(shape) for a in (s_red, sd, sm, sv))

    out = [loss, grad_x[None]]
    for tree in (grads, delta, new_m, new_v):
        out += [tree[n].reshape(weights[n].shape) for n in _ORDER]
    return tuple(out)
```

```python
import functools
import math

import jax
import jax.numpy as jnp
from jax import lax
from jax.experimental import pallas as pl
from jax.experimental.pallas import tpu as pltpu

F32 = jnp.float32
BF16 = jnp.bfloat16

NORM_EPS = 1e-6
N_DEV = 8
N_PEER = N_DEV - 1
GLA_HEADS = 4
GLA_CHUNK = 32
GLA_CHUNK_SHIFT = 5
GLA_TAU = 16.0
GLA_RANK = 16
GLA_BLOCK = 256
S5_GC = 16
S5_P = 64
S5_L = 16
LANE = 128
V7X_VMEM_LIMIT = 56 * 1024 * 1024
V7X_MM_VMEM_BUDGET = 40 * 1024 * 1024
V7X_MM_TILE_MN = 1408
V7X_MM_TILE_K = 2048

ADAM_LR = 0.001
ADAM_B1 = 0.9
ADAM_B2 = 0.999
ADAM_EPS = 1e-08
ADAM_WD = 0.01
ADAM_STEP = 10

GELU_C = math.sqrt(2.0 / math.pi)
GELU_A = 0.044715

MESH = pl.DeviceIdType.MESH


def _cparams(*sem):
    return pltpu.CompilerParams(dimension_semantics=sem, vmem_limit_bytes=V7X_VMEM_LIMIT)


def _divisors_down(n, start, align=LANE):
    t = (min(start, n) // align) * align
    found = False
    while t >= align:
        if n % t == 0:
            found = True
            yield t
        t -= align
    if not found:
        yield n


def _tile(n, target, align=LANE):
    return next(_divisors_down(n, target, align))


def _sigmoid(x):
    return 1.0 / (1.0 + jnp.exp(-x))


_HBM_SPEC = pl.BlockSpec(memory_space=pltpu.HBM)


def _exchange_scratch(n):
    return [pltpu.SemaphoreType.DMA((n * N_PEER,)), pltpu.SemaphoreType.DMA((n * N_PEER,)), pltpu.SemaphoreType.DMA((n,))]


def _ag_phases(x_refs, out_refs, send_sems, recv_sems, local_sems):
    n = len(x_refs)
    x, y, c = lax.axis_index("x"), lax.axis_index("y"), lax.axis_index("c")
    me, sibling = (x, y, c), (x, y, 1 - c)
    chips = [(1 - x, y), (x, 1 - y), (1 - x, 1 - y)]

    def copy(a, k, block, to, from_input=False):
        dst = out_refs[a].at[4 * block[0] + 2 * block[1] + block[2]]
        return pltpu.make_async_remote_copy(
            src_ref=x_refs[a] if from_input else dst, dst_ref=dst,
            send_sem=send_sems.at[a * N_PEER + k], recv_sem=recv_sems.at[a * N_PEER + k], device_id=to, device_id_type=MESH)

    def local(a):
        return pltpu.make_async_copy(x_refs[a], out_refs[a].at[4 * x + 2 * y + c], local_sems.at[a])

    def first(a):
        return [copy(a, 0, me, sibling, True)] + [copy(a, 1 + j, me, (*chip, c), True) for j, chip in enumerate(chips)]

    def start():
        for a in range(n):
            local(a).start()
            for cp in first(a):
                cp.start()

    def relay():
        for j, chip in enumerate(chips):
            for a in range(n):
                copy(a, 1 + j, (*chip, c), me).wait_recv()
                copy(a, 4 + j, (*chip, c), sibling).start()

    def finish():
        for a in range(n):
            copy(a, 0, sibling, me).wait_recv()
            for j, chip in enumerate(chips):
                copy(a, 4 + j, (*chip, 1 - c), me).wait_recv()
        for a in range(n):
            for cp in first(a) + [copy(a, 4 + j, (*chip, c), sibling) for j, chip in enumerate(chips)]:
                cp.wait_send()
            local(a).wait()

    return start, relay, finish


def _a2a_phases(x_refs, out_refs, send_sems, recv_sems, local_sems):
    n = len(x_refs)
    x, y, c = lax.axis_index("x"), lax.axis_index("y"), lax.axis_index("c")
    my = 4 * x + 2 * y + c

    def copy(a, k, incoming):
        px, py, pc = (1 - x if k & 4 else x), (1 - y if k & 2 else y), (1 - c if k & 1 else c)
        pidx = 4 * px + 2 * py + pc
        return pltpu.make_async_remote_copy(
            src_ref=x_refs[a].at[pidx], dst_ref=out_refs[a].at[pidx if incoming else my],
            send_sem=send_sems.at[a * N_PEER + k - 1], recv_sem=recv_sems.at[a * N_PEER + k - 1],
            device_id=(px, py, pc), device_id_type=MESH)

    def local(a):
        return pltpu.make_async_copy(x_refs[a].at[my], out_refs[a].at[my], local_sems.at[a])

    def start():
        for a in range(n):
            local(a).start()
            for k in range(1, N_DEV):
                copy(a, k, False).start()

    def relay():
        pass

    def finish():
        for a in range(n):
            for k in range(1, N_DEV):
                copy(a, k, True).wait_recv()
        for a in range(n):
            for k in range(1, N_DEV):
                copy(a, k, False).wait_send()
            local(a).wait()

    return start, relay, finish


def _exchange_out_shapes(kind, arrays):
    if kind == "ag":
        return [jax.ShapeDtypeStruct((N_DEV,) + a.shape, a.dtype) for a in arrays]
    return [jax.ShapeDtypeStruct(a.shape, a.dtype) for a in arrays]


def _exchange(kind, arrays, name):
    n = len(arrays)
    phases = _ag_phases if kind == "ag" else _a2a_phases

    def body(*refs):
        start, relay, finish = phases(refs[:n], refs[n:2 * n], *refs[2 * n:])
        start()
        relay()
        finish()

    return pl.pallas_call(
        body, name=name,
        out_shape=_exchange_out_shapes(kind, arrays),
        in_specs=[_HBM_SPEC] * n, out_specs=[_HBM_SPEC] * n,
        scratch_shapes=_exchange_scratch(n),
    )(*arrays)


def _mm_tiles(m, n_unit, k_unit, tile_bytes):
    tm = _tile(m, V7X_MM_TILE_MN)
    tn = _tile(n_unit, V7X_MM_TILE_MN)
    for tk in _divisors_down(k_unit, V7X_MM_TILE_K):
        if 2 * 2 * (tm * tk + tk * tn) + tile_bytes * tm * tn <= V7X_MM_VMEM_BUDGET:
            return tm, tn, tk
    return tm, tn, _tile(k_unit, LANE)


def _carry_parts(carry):
    kind, arrays = carry if carry is not None else (None, [])
    n = len(arrays)
    return kind, arrays, [_HBM_SPEC] * n, _exchange_out_shapes(kind, arrays), (_exchange_scratch(n) if n else [])


def _carry_hooks(kind, x_refs, out_refs, sems, step, last_step):
    if not x_refs:
        return lambda: None
    start, relay, finish = (_ag_phases if kind == "ag" else _a2a_phases)(x_refs, out_refs, *sems)
    pl.when(step == 0)(start)

    def after():
        if kind == "ag":
            pl.when(step == (last_step * 7) // 8)(relay)
        pl.when(step == last_step)(finish)

    return after


def _mm(a, b, *, ta=False, tb=False, out_dtype=F32, res=None, carry=None, a_slots=False, b_slots=False, out_slots=0,
        epi=None, name):
    if a_slots:
        assert not ta
        a_n, m, a_c = a.shape
        k = a_n * a_c
    else:
        m, k = (a.shape[1], a.shape[0]) if ta else a.shape
    if b_slots:
        b_n, b_r, b_c = b.shape
        k2, n = (b_n * b_c, b_r) if tb else (b_r, b_n * b_c)
    else:
        k2, n = (b.shape[1], b.shape[0]) if tb else b.shape
    assert k == k2, (a.shape, b.shape, ta, tb)
    has_res = res is not None
    assert not (has_res and (out_slots or epi))
    n_units = [n] + ([n // out_slots] if out_slots else []) + ([b_c] if b_slots and not tb else [])
    k_units = [k] + ([a_c] if a_slots else []) + ([b_c] if b_slots and tb else [])
    n_unit, k_unit = min(n_units), min(k_units)
    assert all(u % n_unit == 0 for u in n_units) and all(u % k_unit == 0 for u in k_units)
    epi_fn, epi_ins, epi_outs = epi if epi is not None else (None, [], [])
    tile_bytes = 4 + (2 * res.dtype.itemsize if has_res else 0)
    tile_bytes += sum(2 * e.shape[0] * e.dtype.itemsize for e in epi_ins)
    tile_bytes += sum(2 * l * jnp.dtype(dt).itemsize for l, dt in epi_outs) if epi else 2 * jnp.dtype(out_dtype).itemsize
    tm, tn, tk = _mm_tiles(m, n_unit, k_unit, tile_bytes)
    ni, nj, nk = m // tm, n // tn, k // tk
    dims = (((0,) if ta else (1,), (1,) if tb else (0,)), ((), ()))

    def slot_map(per, pos):
        if pos == "k_cols":
            return lambda i, j, kk: (kk // per, i, kk % per)
        if pos == "k_cols_j":
            return lambda i, j, kk: (kk // per, j, kk % per)
        if pos == "n_cols_k":
            return lambda i, j, kk: (j // per, kk, j % per)
        return lambda i, j, kk: (j // per, i, j % per)

    if a_slots:
        a_spec = pl.BlockSpec((None, tm, tk), slot_map(a_c // tk, "k_cols"))
    else:
        a_spec = pl.BlockSpec((tk, tm), lambda i, j, kk: (kk, i)) if ta else pl.BlockSpec((tm, tk), lambda i, j, kk: (i, kk))
    if b_slots and tb:
        b_spec = pl.BlockSpec((None, tn, tk), slot_map(b_c // tk, "k_cols_j"))
    elif b_slots:
        b_spec = pl.BlockSpec((None, tk, tn), slot_map(b_c // tn, "n_cols_k"))
    else:
        b_spec = pl.BlockSpec((tn, tk), lambda i, j, kk: (j, kk)) if tb else pl.BlockSpec((tk, tn), lambda i, j, kk: (kk, j))
    if epi:
        lead_spec = lambda l: pl.BlockSpec((l, tm, tn), lambda i, j, kk: (0, i, j))
        o_specs = [lead_spec(l) for l, _ in epi_outs]
        o_shapes = [jax.ShapeDtypeStruct((l, m, n), dt) for l, dt in epi_outs]
    elif out_slots:
        o_specs = [pl.BlockSpec((None, tm, tn), slot_map((n // out_slots) // tn, "n_cols_i"))]
        o_shapes = [jax.ShapeDtypeStruct((out_slots, m, n // out_slots), out_dtype)]
    else:
        o_specs = [pl.BlockSpec((tm, tn), lambda i, j, kk: (i, j))]
        o_shapes = [jax.ShapeDtypeStruct((m, n), out_dtype)]
    extra_ins = ([res] if has_res else []) + list(epi_ins)
    extra_specs = ([o_specs[0]] if has_res else []) + [pl.BlockSpec((e.shape[0], tm, tn), lambda i, j, kk: (0, i, j)) for e in epi_ins]
    n_in, n_out = 2 + len(extra_ins), len(o_specs)
    c_kind, c_arrays, c_specs, c_shapes, c_scratch = _carry_parts(carry)
    nc = len(c_arrays)
    last_step = ni * nj * nk - 1

    def body(*refs):
        a_ref, b_ref = refs[0], refs[1]
        e_refs = refs[2:n_in]
        x_refs = refs[n_in:n_in + nc]
        o_refs = refs[n_in + nc:n_in + nc + n_out]
        out_refs = refs[n_in + nc + n_out:n_in + 2 * nc + n_out]
        scratch = refs[n_in + 2 * nc + n_out:]
        acc = scratch[0] if nk > 1 else None
        kk = pl.program_id(2)
        step = (pl.program_id(0) * nj + pl.program_id(1)) * nk + kk
        after = _carry_hooks(c_kind, x_refs, out_refs, scratch[-3:], step, last_step)

        def emit(val):
            if has_res:
                val = val + e_refs[0][...].astype(F32)
            if epi:
                for o_ref, parts in zip(o_refs, epi_fn(val, *[e[...] for e in e_refs])):
                    for l, v in enumerate(parts):
                        o_ref[l] = v.astype(o_ref.dtype)
            else:
                o_refs[0][...] = val.astype(out_dtype)

        part = lax.dot_general(a_ref[...], b_ref[...], dims, preferred_element_type=F32)
        if nk == 1:
            emit(part)
        else:
            @pl.when(kk == 0)
            def _():
                acc[...] = part

            @pl.when(kk > 0)
            def _():
                acc[...] += part

            @pl.when(kk == nk - 1)
            def _():
                emit(acc[...])

        after()

    sem = ("arbitrary",) * 3 if nc else ("parallel", "parallel", "arbitrary")
    outs = pl.pallas_call(
        body, name=name,
        grid=(ni, nj, nk),
        in_specs=[a_spec, b_spec] + extra_specs + c_specs,
        out_specs=o_specs + c_specs,
        out_shape=o_shapes + c_shapes,
        scratch_shapes=([pltpu.VMEM((tm, tn), F32)] if nk > 1 else []) + c_scratch,
        compiler_params=_cparams(*sem),
    )(a, b, *extra_ins, *c_arrays)
    main = list(outs[:n_out]) if epi else outs[0]
    return (main, list(outs[n_out:])) if nc else main


def _ffn_in_fused(h2, w_s, *, carry=None, name):
    t, d = h2.shape
    n_slot, _, c = w_s.shape
    half = n_slot // 2
    tm = _tile(t, 512)
    c_kind, c_arrays, c_specs, c_shapes, c_scratch = _carry_parts(carry)
    nc = len(c_arrays)
    last_step = (t // tm) * half - 1

    def body(h_ref, wg_ref, wu_ref, *refs):
        x_refs, (gu_ref, act_ref), out_refs, sems = refs[:nc], refs[nc:nc + 2], refs[nc + 2:2 * nc + 2], refs[2 * nc + 2:]
        step = pl.program_id(0) * half + pl.program_id(1)
        after = _carry_hooks(c_kind, x_refs, out_refs, sems, step, last_step)
        h = h_ref[...]
        g = jnp.dot(h, wg_ref[...], preferred_element_type=F32)
        u = jnp.dot(h, wu_ref[...], preferred_element_type=F32)
        gu_ref[0] = g.astype(BF16)
        gu_ref[1] = u.astype(BF16)
        act_ref[...] = (g * _sigmoid(g) * u).astype(BF16)
        after()

    outs = pl.pallas_call(
        body, name=name,
        grid=(t // tm, half),
        in_specs=[pl.BlockSpec((tm, d), lambda i, j: (i, 0)),
                  pl.BlockSpec((None, d, c), lambda i, j: (j, 0, 0)),
                  pl.BlockSpec((None, d, c), lambda i, j: (half + j, 0, 0))] + c_specs,
        out_specs=[pl.BlockSpec((2, tm, c), lambda i, j: (0, i, j)), pl.BlockSpec((tm, c), lambda i, j: (i, j))] + c_specs,
        out_shape=[jax.ShapeDtypeStruct((2, t, half * c), BF16), jax.ShapeDtypeStruct((t, half * c), BF16)] + c_shapes,
        scratch_shapes=c_scratch,
        compiler_params=_cparams(*(("arbitrary",) * 2 if nc else ("parallel", "parallel"))),
    )(h2, w_s, w_s, *c_arrays)
    return (outs[0], outs[1], list(outs[2:])) if nc else (outs[0], outs[1])


def _rowwise(fn, ins, row_outs, acc_outs, *, rows, tb, name):
    in_specs, args = [], []
    for spec in ins:
        kind, arr = spec[0], spec[1]
        if kind == "row":
            in_specs.append(pl.BlockSpec((tb, arr.shape[1]), lambda i: (i, 0)))
        elif kind == "win":
            width, cb = spec[2], spec[3]
            in_specs.append(pl.BlockSpec((tb, width), functools.partial(lambda i, cb: (i, cb), cb=cb)))
        else:
            in_specs.append(pl.BlockSpec(arr.shape, lambda i: (0, 0)))
        args.append(arr)
    out_specs = [pl.BlockSpec((tb, c), lambda i: (i, 0)) for c, _ in row_outs]
    out_specs += [pl.BlockSpec(shape, lambda i: (0, 0)) for shape in acc_outs]
    out_shape = [jax.ShapeDtypeStruct((rows, c), dt) for c, dt in row_outs]
    out_shape += [jax.ShapeDtypeStruct(shape, F32) for shape in acc_outs]
    n_in, n_row = len(ins), len(row_outs)

    def body(*refs):
        vals = [r[...] for r in refs[:n_in]]
        outs = fn(*vals)
        if not isinstance(outs, (tuple, list)):
            outs = (outs,)
        out_refs = refs[n_in:]
        for o_ref, val in zip(out_refs[:n_row], outs[:n_row]):
            o_ref[...] = val.astype(o_ref.dtype)
        first = pl.program_id(0) == 0
        for o_ref, val in zip(out_refs[n_row:], outs[n_row:]):
            @pl.when(first)
            def _(o_ref=o_ref):
                o_ref[...] = jnp.zeros_like(o_ref)
            o_ref[...] += val

    res = pl.pallas_call(
        body, name=name,
        grid=(rows // tb,),
        in_specs=in_specs, out_specs=out_specs, out_shape=out_shape,
        compiler_params=_cparams("arbitrary"),
    )(*args)
    return res


def _rms_fwd(x, g, name):
    def fn(xv, gv):
        r = lax.rsqrt(jnp.mean(xv * xv, axis=-1, keepdims=True) + NORM_EPS)
        return (xv * r * gv,)
    return _rowwise(fn, [("row", x), ("full", g)], [(x.shape[1], BF16)], [], rows=x.shape[0], tb=_tile(x.shape[0], 512), name=name)[0]


def _rms_bwd(x, g, dh, dres, name, want_bf16):
    d = x.shape[1]

    def fn(xv, gv, dhv, drv):
        r = lax.rsqrt(jnp.mean(xv * xv, axis=-1, keepdims=True) + NORM_EPS)
        xhat = xv * r
        dhv = dhv.astype(F32)
        dxhat = dhv * gv
        dx = drv + r * (dxhat - xhat * jnp.mean(dxhat * xhat, axis=-1, keepdims=True))
        dg = jnp.sum(dhv * xhat, axis=0, keepdims=True)
        return (dx, dx, dg) if want_bf16 else (dx, dg)

    row_outs = [(d, F32), (d, BF16)] if want_bf16 else [(d, F32)]
    return _rowwise(fn, [("row", x), ("full", g), ("row", dh), ("row", dres)], row_outs, [(1, d)],
                    rows=x.shape[0], tb=_tile(x.shape[0], 256), name=name)


def _loss_head(x2, g, target, name):
    d = x2.shape[1]

    def fn(xv, gv, tv):
        r = lax.rsqrt(jnp.mean(xv * xv, axis=-1, keepdims=True) + NORM_EPS)
        xhat = xv * r
        diff = xhat * gv - tv
        loss = 0.5 * jnp.sum(jnp.mean(diff * diff, axis=-1, keepdims=True), axis=0, keepdims=True)
        dy = diff * (1.0 / d)
        dxhat = dy * gv
        dx = r * (dxhat - xhat * jnp.mean(dxhat * xhat, axis=-1, keepdims=True))
        dg = jnp.sum(dy * xhat, axis=0, keepdims=True)
        return dx, dx, dg, jnp.broadcast_to(loss, (1, LANE))

    return _rowwise(fn, [("row", x2), ("full", g), ("row", target)], [(d, F32), (d, BF16)], [(1, d), (1, LANE)],
                    rows=x2.shape[0], tb=_tile(x2.shape[0], 256), name=name)


def _swiglu_bwd_tile(dact, gu):
    g, u = gu[0].astype(F32), gu[1].astype(F32)
    sg = _sigmoid(g)
    return ((dact * u * (sg * (1.0 + g * (1.0 - sg))), dact * (g * sg)),)


def _mix_fwd(proj, pa, pb, d, name):
    def fn(ga, gb, av, bv):
        return (_sigmoid(ga.astype(F32)) * av.astype(F32) + _sigmoid(gb.astype(F32)) * bv.astype(F32),)
    return _rowwise(fn, [("win", proj, d, 0), ("win", proj, d, 1), ("row", pa), ("row", pb)], [(d, BF16)], [],
                    rows=pa.shape[0], tb=_tile(pa.shape[0], 512), name=name)[0]


def _mix_bwd(proj, pa, pb, dmix, d, name):
    def fn(ga, gb, av, bv, dm):
        dm = dm.astype(F32)
        sa, sb = _sigmoid(ga.astype(F32)), _sigmoid(gb.astype(F32))
        av, bv = av.astype(F32), bv.astype(F32)
        return dm * sa, dm * sb, dm * av * sa * (1.0 - sa), dm * bv * sb * (1.0 - sb)
    return _rowwise(fn, [("win", proj, d, 0), ("win", proj, d, 1), ("row", pa), ("row", pb), ("row", dmix)],
                    [(d, BF16)] * 4, [], rows=pa.shape[0], tb=_tile(pa.shape[0], 512), name=name)


def _chunk_masks(tb):
    r = lax.broadcasted_iota(jnp.int32, (tb, tb), 0)
    c = lax.broadcasted_iota(jnp.int32, (tb, tb), 1)
    same = lax.shift_right_logical(r, GLA_CHUNK_SHIFT) == lax.shift_right_logical(c, GLA_CHUNK_SHIFT)
    return same, same & (c <= r), same & (r <= c)


def _mask_bf16(mask):
    return jnp.where(mask, 1.0, 0.0).astype(BF16)


def _split_dot(mask_bf, x, terms):
    acc, rem = None, x
    for _ in range(terms):
        hi = rem.astype(BF16)
        part = jnp.dot(mask_bf, hi, preferred_element_type=F32)
        acc = part if acc is None else acc + part
        rem = rem - hi.astype(F32)
    return acc


def _gla_decay(al, wa2, ba2, same_bf, causal_bf):
    z = jnp.dot(al.astype(BF16), wa2, preferred_element_type=F32) + ba2
    la = (jnp.minimum(z, 0.0) - jnp.log(1.0 + jnp.exp(-jnp.abs(z)))) * (1.0 / GLA_TAU)
    bc = _split_dot(causal_bf, la, 3)
    bl = _split_dot(same_bf, la, 3)
    return z, bc, bl


def _dot_t(a, b, ca, cb):
    return lax.dot_general(a, b, (((ca,), (cb,)), ((), ())), preferred_element_type=F32)


def _gla_fwd(proj, alow, wa2, ba2, ghn, *, dk, dv, name):
    t = proj.shape[0]
    tb = min(GLA_BLOCK, t)
    nch = tb // GLA_CHUNK
    hk, hv = dk // GLA_HEADS, dv // GLA_HEADS
    scale = hk ** -0.5
    v_cb, r_cb = (8 * dk) // dv, (8 * dk) // dv + 1
    q_cb, k_cb = (8 * dk + 2 * dv) // dk, (8 * dk + 2 * dv) // dk + 1

    def body(q_ref, k_ref, v_ref, r_ref, al_ref, wa2_ref, ba2_ref, ghn_ref, oa_ref, opre_ref, s_ref, st_scr):
        @pl.when(pl.program_id(0) == 0)
        def _():
            st_scr[...] = jnp.zeros_like(st_scr)

        same, causal, _ = _chunk_masks(tb)
        same_bf, causal_bf = _mask_bf16(same), _mask_bf16(causal)
        _, bc, bl = _gla_decay(al_ref[...], wa2_ref[...], ba2_ref[...], same_bf, causal_bf)
        q = q_ref[...].astype(F32) * scale
        k = k_ref[...].astype(F32)
        qd = (q * jnp.exp(bc)).astype(BF16)
        ki = (k * jnp.exp(-bc)).astype(BF16)
        ks = (k * jnp.exp(bl - bc)).astype(BF16)
        dl = jnp.exp(bl)
        ksls = [slice(h * hk, (h + 1) * hk) for h in range(GLA_HEADS)]
        vsls = [slice(h * hv, (h + 1) * hv) for h in range(GLA_HEADS)]
        v_hs = [v_ref[:, vsl] for vsl in vsls]
        o_intras = []
        for ksl, v_h in zip(ksls, v_hs):
            sc = jnp.where(causal, _dot_t(qd[:, ksl], ki[:, ksl], 1, 1), 0.0)
            o_intras.append(jnp.dot(sc.astype(BF16), v_h, preferred_element_type=F32))
        for c in range(nch):
            rows = slice(c * GLA_CHUNK, (c + 1) * GLA_CHUNK)
            for h, (ksl, vsl) in enumerate(zip(ksls, vsls)):
                st = st_scr[h]
                s_ref[c, h] = st
                opre_ref[rows, vsl] = o_intras[h][rows] + _dot_t(qd[rows, ksl], st.astype(BF16), 1, 1)
                st_scr[h] = dl[c * GLA_CHUNK:c * GLA_CHUNK + 1, ksl] * st + _dot_t(v_hs[h][rows], ks[rows, ksl], 0, 0)
        for h in range(GLA_HEADS):
            vsl = slice(h * hv, (h + 1) * hv)
            o = opre_ref[:, vsl]
            rs = lax.rsqrt(jnp.mean(o * o, axis=-1, keepdims=True) + NORM_EPS)
            rv = r_ref[:, vsl].astype(F32)
            oa_ref[:, vsl] = (rv * _sigmoid(rv) * (o * rs * ghn_ref[:, vsl])).astype(BF16)

    nchunks = t // GLA_CHUNK
    return pl.pallas_call(
        body, name=name,
        grid=(t // tb,),
        in_specs=[
            pl.BlockSpec((tb, dk), lambda i: (i, q_cb)),
            pl.BlockSpec((tb, dk), lambda i: (i, k_cb)),
            pl.BlockSpec((tb, dv), lambda i: (i, v_cb)),
            pl.BlockSpec((tb, dv), lambda i: (i, r_cb)),
            pl.BlockSpec((tb, LANE), lambda i: (i, 0)),
            pl.BlockSpec(wa2.shape, lambda i: (0, 0)),
            pl.BlockSpec(ba2.shape, lambda i: (0, 0)),
            pl.BlockSpec(ghn.shape, lambda i: (0, 0)),
        ],
        out_specs=[
            pl.BlockSpec((tb, dv), lambda i: (i, 0)),
            pl.BlockSpec((tb, dv), lambda i: (i, 0)),
            pl.BlockSpec((nch, GLA_HEADS, hv, hk), lambda i: (i, 0, 0, 0)),
        ],
        out_shape=[
            jax.ShapeDtypeStruct((t, dv), BF16),
            jax.ShapeDtypeStruct((t, dv), F32),
            jax.ShapeDtypeStruct((nchunks, GLA_HEADS, hv, hk), F32),
        ],
        scratch_shapes=[pltpu.VMEM((GLA_HEADS, hv, hk), F32)],
        compiler_params=_cparams("arbitrary"),
    )(proj, proj, proj, proj, alow, wa2, ba2, ghn)


def _gla_bwd(proj, alow, wa2, ba2, ghn, opre, states, doa, *, dk, dv, name):
    t = proj.shape[0]
    tb = min(GLA_BLOCK, t)
    nb = t // tb
    nch = tb // GLA_CHUNK
    hk, hv = dk // GLA_HEADS, dv // GLA_HEADS
    scale = hk ** -0.5
    v_cb, r_cb = (8 * dk) // dv, (8 * dk) // dv + 1
    q_cb, k_cb = (8 * dk + 2 * dv) // dk, (8 * dk + 2 * dv) // dk + 1

    def body(q_ref, k_ref, v_ref, r_ref, al_ref, wa2_ref, ba2_ref, ghn_ref, opre_ref, s_ref, doa_ref,
             dq_ref, dk_ref, dv_ref, dr_ref, dal_ref, dwa2_ref, dba2_ref, dghn_ref,
             dst_scr, dqd_scr, dki_scr, dks_scr, ddl_scr):
        @pl.when(pl.program_id(0) == 0)
        def _():
            dst_scr[...] = jnp.zeros_like(dst_scr)
            dwa2_ref[...] = jnp.zeros_like(dwa2_ref)
            dba2_ref[...] = jnp.zeros_like(dba2_ref)
            dghn_ref[...] = jnp.zeros_like(dghn_ref)

        same, causal, anti = _chunk_masks(tb)
        same_bf, causal_bf, anti_bf = _mask_bf16(same), _mask_bf16(causal), _mask_bf16(anti)
        al = al_ref[...]
        wa2v = wa2_ref[...]
        z, bc, bl = _gla_decay(al, wa2v, ba2_ref[...], same_bf, causal_bf)
        e_bc, e_nbc, e_st = jnp.exp(bc), jnp.exp(-bc), jnp.exp(bl - bc)
        q = q_ref[...].astype(F32) * scale
        k = k_ref[...].astype(F32)
        qd_f, ki_f, ks_f = q * e_bc, k * e_nbc, k * e_st
        qd, ki, ks = qd_f.astype(BF16), ki_f.astype(BF16), ks_f.astype(BF16)
        dl = jnp.exp(bl)
        per_head = []
        for h in range(GLA_HEADS):
            ksl = slice(h * hk, (h + 1) * hk)
            vsl = slice(h * hv, (h + 1) * hv)
            o = opre_ref[:, vsl]
            rs = lax.rsqrt(jnp.mean(o * o, axis=-1, keepdims=True) + NORM_EPS)
            ohat = o * rs
            g_h = ghn_ref[:, vsl]
            rv = r_ref[:, vsl].astype(F32)
            sg = _sigmoid(rv)
            d_oa = doa_ref[:, vsl].astype(F32)
            don = d_oa * (rv * sg)
            dr_ref[:, vsl] = (d_oa * (ohat * g_h) * (sg * (1.0 + rv * (1.0 - sg)))).astype(BF16)
            dghn_ref[:, vsl] += jnp.sum(don * ohat, axis=0, keepdims=True)
            dohat = don * g_h
            do_f = rs * (dohat - ohat * jnp.mean(dohat * ohat, axis=-1, keepdims=True))
            do = do_f.astype(BF16)
            v_h = v_ref[:, vsl]
            p = jnp.where(causal, _dot_t(do, v_h, 1, 1), 0.0).astype(BF16)
            dqd_intra = jnp.dot(p, ki[:, ksl], preferred_element_type=F32)
            dki_scr[:, ksl] = _dot_t(p, qd[:, ksl], 0, 0)
            sc = jnp.where(causal, _dot_t(qd[:, ksl], ki[:, ksl], 1, 1), 0.0).astype(BF16)
            dv_intra = _dot_t(sc, do, 0, 0)
            per_head.append((ksl, vsl, v_h, do, dqd_intra, dv_intra))
        for c in reversed(range(nch)):
            rows = slice(c * GLA_CHUNK, (c + 1) * GLA_CHUNK)
            for h, (ksl, vsl, v_h, do, dqd_intra, dv_intra) in enumerate(per_head):
                dst = dst_scr[h]
                st = s_ref[c, h]
                dst_bf = dst.astype(BF16)
                dv_ref[rows, vsl] = (dv_intra[rows] + _dot_t(ks[rows, ksl], dst_bf, 1, 1)).astype(BF16)
                dks_scr[rows, ksl] = jnp.dot(v_h[rows], dst_bf, preferred_element_type=F32)
                dl_c = dl[c * GLA_CHUNK:c * GLA_CHUNK + 1, ksl]
                ddl = jnp.sum(dst * st, axis=0, keepdims=True) * dl_c
                ddl_scr[rows, ksl] = jnp.broadcast_to(ddl, (GLA_CHUNK, hk))
                dqd_scr[rows, ksl] = dqd_intra[rows] + jnp.dot(do[rows], st.astype(BF16), preferred_element_type=F32)
                dst_scr[h] = dl_c * dst + _dot_t(do[rows], qd[rows, ksl], 0, 0)
        dqd, dki, dks = dqd_scr[...], dki_scr[...], dks_scr[...]
        dq_ref[...] = (dqd * (scale * e_bc)).astype(BF16)
        dk_ref[...] = (dki * e_nbc + dks * e_st).astype(BF16)
        dks_ks = dks * ks_f
        dbc = dqd * qd_f - dki * ki_f - dks_ks
        dla = _split_dot(anti_bf, dbc, 2) + _split_dot(same_bf, dks_ks, 2) + ddl_scr[...]
        dz = (dla * (1.0 / GLA_TAU) * (1.0 - _sigmoid(z)))
        dz_bf = dz.astype(BF16)
        dal_ref[...] = _dot_t(dz_bf, wa2v, 1, 1).astype(BF16)
        dwa2_ref[...] += _dot_t(al.astype(BF16), dz_bf, 0, 0)
        dba2_ref[...] += jnp.sum(dz, axis=0, keepdims=True)

    rev = lambda i: nb - 1 - i
    return pl.pallas_call(
        body, name=name,
        grid=(nb,),
        in_specs=[
            pl.BlockSpec((tb, dk), lambda i: (rev(i), q_cb)),
            pl.BlockSpec((tb, dk), lambda i: (rev(i), k_cb)),
            pl.BlockSpec((tb, dv), lambda i: (rev(i), v_cb)),
            pl.BlockSpec((tb, dv), lambda i: (rev(i), r_cb)),
            pl.BlockSpec((tb, LANE), lambda i: (rev(i), 0)),
            pl.BlockSpec(wa2.shape, lambda i: (0, 0)),
            pl.BlockSpec(ba2.shape, lambda i: (0, 0)),
            pl.BlockSpec(ghn.shape, lambda i: (0, 0)),
            pl.BlockSpec((tb, dv), lambda i: (rev(i), 0)),
            pl.BlockSpec((nch, GLA_HEADS, hv, hk), lambda i: (rev(i), 0, 0, 0)),
            pl.BlockSpec((tb, dv), lambda i: (rev(i), 0)),
        ],
        out_specs=[
            pl.BlockSpec((tb, dk), lambda i: (rev(i), 0)),
            pl.BlockSpec((tb, dk), lambda i: (rev(i), 0)),
            pl.BlockSpec((tb, dv), lambda i: (rev(i), 0)),
            pl.BlockSpec((tb, dv), lambda i: (rev(i), 0)),
            pl.BlockSpec((tb, LANE), lambda i: (rev(i), 0)),
            pl.BlockSpec(wa2.shape, lambda i: (0, 0)),
            pl.BlockSpec(ba2.shape, lambda i: (0, 0)),
            pl.BlockSpec(ghn.shape, lambda i: (0, 0)),
        ],
        out_shape=[
            jax.ShapeDtypeStruct((t, dk), BF16),
            jax.ShapeDtypeStruct((t, dk), BF16),
            jax.ShapeDtypeStruct((t, dv), BF16),
            jax.ShapeDtypeStruct((t, dv), BF16),
            jax.ShapeDtypeStruct((t, LANE), BF16),
            jax.ShapeDtypeStruct(wa2.shape, F32),
            jax.ShapeDtypeStruct(ba2.shape, F32),
            jax.ShapeDtypeStruct(ghn.shape, F32),
        ],
        scratch_shapes=[pltpu.VMEM((GLA_HEADS, hv, hk), F32)] + [pltpu.VMEM((tb, dk), F32)] * 4,
        compiler_params=_cparams("arbitrary"),
    )(proj, proj, proj, proj, alow, wa2, ba2, ghn, opre, states, doa)


def _s5_tables(lam_re, lam_im, log_dt, b_re, b_im, c_re, c_im):
    hp = lax.Precision.HIGHEST
    g, p = lam_re.shape
    ln = S5_L
    dt = jnp.exp(log_dt)[:, None]
    lr, li = lam_re, lam_im
    mag = jnp.exp(lr * dt)
    ar, ai = mag * jnp.cos(li * dt), mag * jnp.sin(li * dt)
    den = lr * lr + li * li
    am1 = ar - 1.0
    f_re = ((am1 * lr + ai * li) / den)[..., None]
    f_im = ((ai * lr - am1 * li) / den)[..., None]
    bb_re = f_re * b_re - f_im * b_im
    bb_im = f_re * b_im + f_im * b_re
    j = jnp.arange(ln + 1, dtype=F32)[None, :, None]
    pm = jnp.exp(j * (lr * dt)[:, None, :])
    ang = j * (li * dt)[:, None, :]
    pw_re, pw_im = pm * jnp.cos(ang), pm * jnp.sin(ang)
    cp_re = c_re[:, None] * pw_re[:, :, None, :] - c_im[:, None] * pw_im[:, :, None, :]
    cp_im = c_re[:, None] * pw_im[:, :, None, :] + c_im[:, None] * pw_re[:, :, None, :]
    kj = (jnp.einsum("gjcp,gpd->gjcd", cp_re[:, :ln], bb_re, precision=hp)
          - jnp.einsum("gjcp,gpd->gjcd", cp_im[:, :ln], bb_im, precision=hp))
    s_i = jnp.arange(ln)[None, :, None]
    t_i = jnp.arange(ln)[None, None, :]
    j_i = jnp.arange(ln)[:, None, None]
    shift = (t_i - s_i == j_i).astype(F32)
    m = jnp.einsum("jst,gjcd->gsdtc", shift, kj, precision=hp).reshape(g, ln * S5_GC, ln * S5_GC)
    rp_re, rp_im = pw_re[:, ln - 1::-1], pw_im[:, ln - 1::-1]
    bbt_re, bbt_im = bb_re.transpose(0, 2, 1)[:, None], bb_im.transpose(0, 2, 1)[:, None]
    bst_re = rp_re[:, :, None, :] * bbt_re - rp_im[:, :, None, :] * bbt_im
    bst_im = rp_re[:, :, None, :] * bbt_im + rp_im[:, :, None, :] * bbt_re
    bst = jnp.concatenate([bst_re, bst_im], axis=-1).reshape(g, ln * S5_GC, 2 * p)
    cst = jnp.concatenate([cp_re[:, 1:].transpose(0, 3, 1, 2), -cp_im[:, 1:].transpose(0, 3, 1, 2)], axis=1)
    cst = cst.reshape(g, 2 * p, ln * S5_GC)
    a = jnp.stack([jnp.concatenate([pw_re[:, ln], pw_re[:, ln]], axis=-1),
                   jnp.concatenate([-pw_im[:, ln], pw_im[:, ln]], axis=-1)], axis=1)
    return m, bst, cst, a


def _state_scan(v, pr, pi, reverse):
    n = v.shape[0]
    half = v.shape[1] // 2
    row = lax.broadcasted_iota(jnp.int32, v.shape, 0)
    z, s = v, 1
    while s < n:
        if reverse:
            zs = jnp.where(row < n - s, pltpu.roll(z, n - s, 0), 0.0)
        else:
            zs = jnp.where(row >= s, pltpu.roll(z, s, 0), 0.0)
        z = z + zs * pr + pltpu.roll(zs, half, 1) * pi
        pr, pi = pr * pr - pi * pi, 2.0 * pr * pi
        s *= 2
    return z


def _s5_core_fwd(u_g, m, bst, cst, a, name):
    g, nc, w = u_g.shape
    p2 = bst.shape[2]

    def body(u_ref, m_ref, b_ref, c_ref, a_ref, y_ref, x_ref):
        u = u_ref[0]
        v = jnp.dot(u, b_ref[0], preferred_element_type=F32)
        z = _state_scan(v, a_ref[0, 0:1, :], a_ref[0, 1:2, :], reverse=False)
        row = lax.broadcasted_iota(jnp.int32, z.shape, 0)
        x = jnp.where(row >= 1, pltpu.roll(z, 1, 0), 0.0)
        x_ref[0] = x
        y_ref[0] = (jnp.dot(u, m_ref[0], preferred_element_type=F32)
                    + jnp.dot(x.astype(BF16), c_ref[0], preferred_element_type=F32))

    per_g = lambda shape: pl.BlockSpec((1,) + shape, lambda i: (i, 0, 0))
    return pl.pallas_call(
        body, name=name, grid=(g,),
        in_specs=[per_g((nc, w)), per_g((w, w)), per_g((w, p2)), per_g((p2, w)), per_g((2, p2))],
        out_specs=[per_g((nc, w)), per_g((nc, p2))],
        out_shape=[jax.ShapeDtypeStruct((g, nc, w), F32), jax.ShapeDtypeStruct((g, nc, p2), F32)],
        compiler_params=_cparams("parallel"),
    )(u_g, m, bst, cst, a)


def _s5_core_bwd(dy_g, u_g, x_g, m, bst, cst, a, name):
    g, nc, w = u_g.shape
    p2 = bst.shape[2]

    def body(dy_ref, u_ref, x_ref, m_ref, b_ref, c_ref, a_ref, du_ref, dm_ref, db_ref, dc_ref, da_ref):
        dy, u, x = dy_ref[0], u_ref[0], x_ref[0]
        gx = _dot_t(dy, c_ref[0], 1, 1)
        rtot = _state_scan(gx, a_ref[0, 0:1, :], -a_ref[0, 1:2, :], reverse=True)
        row = lax.broadcasted_iota(jnp.int32, rtot.shape, 0)
        dv = jnp.where(row < nc - 1, pltpu.roll(rtot, nc - 1, 0), 0.0)
        dv_bf = dv.astype(BF16)
        du_ref[0] = (_dot_t(dy, m_ref[0], 1, 1) + _dot_t(dv_bf, b_ref[0], 1, 1)).astype(BF16)
        dm_ref[0] = _dot_t(u, dy, 0, 0)
        dc_ref[0] = _dot_t(x.astype(BF16), dy, 0, 0)
        db_ref[0] = _dot_t(u, dv_bf, 0, 0)
        x_sw = pltpu.roll(x, p2 // 2, 1)
        da_ref[0, 0:1, :] = jnp.sum(dv * x, axis=0, keepdims=True)
        da_ref[0, 1:2, :] = jnp.sum(dv * x_sw, axis=0, keepdims=True)

    per_g = lambda shape: pl.BlockSpec((1,) + shape, lambda i: (i, 0, 0))
    return pl.pallas_call(
        body, name=name, grid=(g,),
        in_specs=[per_g((nc, w)), per_g((nc, w)), per_g((nc, p2)), per_g((w, w)), per_g((w, p2)), per_g((p2, w)), per_g((2, p2))],
        out_specs=[per_g((nc, w)), per_g((w, w)), per_g((w, p2)), per_g((p2, w)), per_g((2, p2))],
        out_shape=[jax.ShapeDtypeStruct((g, nc, w), BF16), jax.ShapeDtypeStruct((g, w, w), F32),
                   jax.ShapeDtypeStruct((g, w, p2), F32), jax.ShapeDtypeStruct((g, p2, w), F32),
                   jax.ShapeDtypeStruct((g, 2, p2), F32)],
        compiler_params=_cparams("parallel"),
    )(dy_g, u_g, x_g, m, bst, cst, a)


def _gelu_parts(y):
    inner = GELU_C * (y + GELU_A * y * y * y)
    th = jnp.tanh(inner)
    return th, 0.5 * y * (1.0 + th)


def _s5_post_fwd(y_raw, proj, u_cb, s5d, wglu, bglu, name):
    w = y_raw.shape[1]

    def fn(yr, u, dsk, wg, bg):
        y = yr + dsk * u.astype(F32)
        _, h = _gelu_parts(y)
        gl = jnp.dot(h.astype(BF16), wg, preferred_element_type=F32) + bg
        return (h * _sigmoid(gl),)

    return _rowwise(fn, [("row", y_raw), ("win", proj, w, u_cb), ("full", s5d), ("full", wglu), ("full", bglu)],
                    [(w, BF16)], [], rows=y_raw.shape[0], tb=_tile(y_raw.shape[0], 512), name=name)[0]


def _s5_post_bwd(y_raw, proj, u_cb, s5d, wglu, bglu, dob, name):
    w = y_raw.shape[1]

    def fn(yr, u, dsk, wg, bg, dov):
        u = u.astype(F32)
        dov = dov.astype(F32)
        y = yr + dsk * u
        th, h = _gelu_parts(y)
        h_bf = h.astype(BF16)
        gl = jnp.dot(h_bf, wg, preferred_element_type=F32) + bg
        sg = _sigmoid(gl)
        dgl = dov * h * sg * (1.0 - sg)
        dgl_bf = dgl.astype(BF16)
        dh = dov * sg + _dot_t(dgl_bf, wg, 1, 1)
        dgelu = 0.5 * (1.0 + th) + 0.5 * y * (1.0 - th * th) * GELU_C * (1.0 + 3.0 * GELU_A * y * y)
        dy = dh * dgelu
        return (dy, dy * dsk,
                _dot_t(h_bf, dgl_bf, 0, 0), jnp.sum(dgl, axis=0, keepdims=True), jnp.sum(dy * u, axis=0, keepdims=True))

    return _rowwise(fn, [("row", y_raw), ("win", proj, w, u_cb), ("full", s5d), ("full", wglu), ("full", bglu), ("row", dob)],
                    [(w, BF16), (w, BF16)], [(w, w), (1, w), (1, w)], rows=y_raw.shape[0], tb=_tile(y_raw.shape[0], 512), name=name)


def _to_groups(a, dtype):
    t, w = a.shape
    g = w // S5_GC
    return a.reshape(t // S5_L, S5_L, g, S5_GC).transpose(2, 0, 1, 3).reshape(g, t // S5_L, S5_L * S5_GC).astype(dtype)


def _from_groups(a):
    g, nc, _ = a.shape
    return a.reshape(g, nc, S5_L, S5_GC).transpose(1, 2, 0, 3).reshape(nc * S5_L, g * S5_GC)


def _adamw(w, g, m, v, name):
    _, rows, cols = w.shape
    tb = _tile(rows, 256, align=16)
    slots = g.ndim == 3
    c1 = 1.0 - ADAM_B1 ** ADAM_STEP
    c2 = 1.0 - ADAM_B2 ** ADAM_STEP

    def body(w_ref, g_ref, m_ref, v_ref, *out_refs):
        if slots:
            gv = g_ref[0].astype(F32)
            for s in range(1, N_DEV):
                gv = gv + g_ref[s].astype(F32)
            out_refs[0][...] = gv
        else:
            gv = g_ref[...]
        d_ref, nm_ref, nv_ref = out_refs[-3:]
        nm = ADAM_B1 * m_ref[...] + (1.0 - ADAM_B1) * gv
        nv = ADAM_B2 * v_ref[...] + (1.0 - ADAM_B2) * (gv * gv)
        d_ref[...] = -ADAM_LR * ((nm / c1) / (jnp.sqrt(nv / c2) + ADAM_EPS) + ADAM_WD * w_ref[...])
        nm_ref[...] = nm
        nv_ref[...] = nv

    spec = pl.BlockSpec((None, tb, cols), lambda i: (0, i, 0))
    g_spec = pl.BlockSpec((N_DEV, tb, cols), lambda i: (0, i, 0)) if slots else pl.BlockSpec((tb, cols), lambda i: (i, 0))
    n_out = 4 if slots else 3
    return pl.pallas_call(
        body, name=name, grid=(rows // tb,),
        in_specs=[spec, g_spec, spec, spec], out_specs=[spec] * n_out,
        out_shape=[jax.ShapeDtypeStruct((1, rows, cols), F32)] * n_out,
        compiler_params=_cparams("parallel"),
    )(w, g, m, v)


def _slot_sum(x, name):
    _, rows, cols = x.shape
    if rows % 8 == 0:
        tr, tc = _tile(rows, 512, align=8), cols
    else:
        tr, tc = rows, _tile(cols, 256)

    def body(x_ref, o_ref):
        acc = x_ref[0].astype(F32)
        for s in range(1, N_DEV):
            acc = acc + x_ref[s].astype(F32)
        o_ref[...] = acc

    return pl.pallas_call(
        body, name=name, grid=(rows // tr, cols // tc),
        in_specs=[pl.BlockSpec((N_DEV, tr, tc), lambda i, j: (0, i, j))],
        out_specs=pl.BlockSpec((tr, tc), lambda i, j: (i, j)),
        out_shape=jax.ShapeDtypeStruct((rows, cols), F32),
        compiler_params=_cparams("parallel", "parallel"),
    )(x)


_REST = (("w_a2", 1), ("w_glu", 0), ("w_branch_a", 1), ("w_branch_b", 1), ("w_out", 0), ("w_ffn_in", 1), ("w_ffn_out", 0))
_SMALL = ("norm1_g", "b_a2", "gla_norm_g", "lam_re", "lam_im", "log_dt", "s5_b_re", "s5_b_im", "s5_c_re", "s5_c_im",
          "s5_d", "b_glu", "norm2_g", "final_norm_g")
_ORDER = ("norm1_g", "w_in", "w_a2", "b_a2", "gla_norm_g", "lam_re", "lam_im", "log_dt", "s5_b_re", "s5_b_im", "s5_c_re",
          "s5_c_im", "s5_d", "w_glu", "b_glu", "w_branch_a", "w_branch_b", "w_out", "norm2_g", "w_ffn_in", "w_ffn_out", "final_norm_g")


def _join_slots(slots, axis):
    _, r, c = slots.shape
    if axis == 0:
        return slots.reshape(N_DEV * r, c)
    return slots.transpose(1, 0, 2).reshape(r, N_DEV * c)


def _to_slots(full, axis):
    r, c = full.shape
    if axis == 0:
        return full.reshape(N_DEV, r // N_DEV, c)
    return full.reshape(r, N_DEV, c // N_DEV).transpose(1, 0, 2)


def _local_step(x, target, w_in_t, small, rest):
    t, d = x.shape
    dk, dv, s5w = d // 4, d // 2, d // 4
    dist = not isinstance(rest, dict)
    o_q, o_k, o_v, o_r, o_al = 0, dk, 2 * dk, 2 * dk + dv, 2 * dk + 2 * dv
    o_u = o_al + GLA_RANK
    o_ga, o_gb = o_u + s5w, o_u + s5w + d
    rows = lambda a, o, n: a[o:o + n]
    w_main_t = jnp.concatenate([rows(w_in_t, o_ga, d), rows(w_in_t, o_gb, d), rows(w_in_t, o_v, dv), rows(w_in_t, o_r, dv),
                                rows(w_in_t, o_q, dk), rows(w_in_t, o_k, dk), rows(w_in_t, o_u, s5w)], axis=0)
    w_al_t = jnp.pad(rows(w_in_t, o_al, GLA_RANK), ((0, LANE - GLA_RANK), (0, 0)))
    u_cb = (2 * d + 2 * dv + 2 * dk) // s5w

    h1 = _rms_fwd(x, small["norm1_g"], "norm1_fwd")
    if dist:
        proj, gathered = _mm(h1, w_main_t, tb=True, out_dtype=BF16, carry=("ag", rest[:-1]), name="in_proj")
        w = {n: _join_slots(g, ax) for (n, ax), g in zip(_REST[:-2], gathered[:-1])}
        w_ffn_in_s = gathered[-1]
    else:
        proj = _mm(h1, w_main_t, tb=True, out_dtype=BF16, name="in_proj")
        w = rest
        w_ffn_in_s = _to_slots(rest["w_ffn_in"], 1)
    wa2 = jnp.pad(w["w_a2"], ((0, LANE - GLA_RANK), (0, 0)))
    alow = _mm(h1, w_al_t, tb=True, out_dtype=BF16, name="in_proj_gate_rank")
    o_a, o_pre, states = _gla_fwd(proj, alow, wa2, small["b_a2"], small["gla_norm_g"], dk=dk, dv=dv, name="gla_fwd")

    s5_params = (small["lam_re"], small["lam_im"], small["log_dt"][0], small["s5_b_re"], small["s5_b_im"],
                 small["s5_c_re"], small["s5_c_im"])
    (tm, tbst, tcst, ta), tables_vjp = jax.vjp(_s5_tables, *s5_params)
    tm_bf, tbst_bf, tcst_bf = tm.astype(BF16), tbst.astype(BF16), tcst.astype(BF16)
    u_g = _to_groups(proj[:, u_cb * s5w:(u_cb + 1) * s5w], BF16)
    y_g, x_g = _s5_core_fwd(u_g, tm_bf, tbst_bf, tcst_bf, ta, "s5_core_fwd")
    y_raw = _from_groups(y_g)
    o_b = _s5_post_fwd(y_raw, proj, u_cb, small["s5_d"], w["w_glu"], small["b_glu"], "s5_post_fwd")

    pa = _mm(o_a, w["w_branch_a"], out_dtype=BF16, name="branch_a")
    pb = _mm(o_b, w["w_branch_b"], out_dtype=BF16, name="branch_b")
    mix = _mix_fwd(proj, pa, pb, d, "mix_fwd")
    x1 = _mm(mix, w["w_out"], res=x, name="out_proj")
    h2 = _rms_fwd(x1, small["norm2_g"], "norm2_fwd")
    if dist:
        gu, act, (w_ffn_out_s,) = _ffn_in_fused(h2, w_ffn_in_s, carry=("ag", rest[-1:]), name="ffn_in")
        w_ffn_out = _join_slots(w_ffn_out_s, 0)
    else:
        gu, act = _ffn_in_fused(h2, w_ffn_in_s, name="ffn_in")
        w_ffn_out = rest["w_ffn_out"]
    x2 = _mm(act, w_ffn_out, res=x1, name="ffn_out")
    dx2, dx2_bf, d_final_g, loss = _loss_head(x2, small["final_norm_g"], target, "loss_head")

    recv = {}
    dgu, = _mm(dx2_bf, w_ffn_out, tb=True, epi=(_swiglu_bwd_tile, [gu], [(2, BF16)]), name="d_act")
    g_ffn_out = _mm(act, dx2_bf, ta=True, out_dtype=BF16, name="g_w_ffn_out")
    if dist:
        g_ffn_in_s, (recv["w_ffn_out"],) = _mm(h2, dgu, ta=True, b_slots=True, out_dtype=BF16, out_slots=N_DEV,
                                               carry=("a2a", [_to_slots(g_ffn_out, 0)]), name="g_w_ffn_in")
        dh2, (recv["w_ffn_in"],) = _mm(dgu, w_ffn_in_s, tb=True, a_slots=True, b_slots=True, carry=("a2a", [g_ffn_in_s]),
                                       name="d_h2")
    else:
        g_ffn_in_s = _mm(h2, dgu, ta=True, b_slots=True, out_dtype=BF16, out_slots=N_DEV, name="g_w_ffn_in")
        dh2 = _mm(dgu, w_ffn_in_s, tb=True, a_slots=True, b_slots=True, name="d_h2")
    dx1, dx1_bf, d_norm2_g = _rms_bwd(x1, small["norm2_g"], dh2, dx2, "norm2_bwd", True)
    dmix = _mm(dx1_bf, w["w_out"], tb=True, out_dtype=BF16, name="d_mix")
    g_out = _mm(mix, dx1_bf, ta=True, out_dtype=BF16, name="g_w_out")
    dpa, dpb, dga, dgb = _mix_bwd(proj, pa, pb, dmix, d, "mix_bwd")
    doa = _mm(dpa, w["w_branch_a"], tb=True, out_dtype=BF16, name="d_o_a")
    dob = _mm(dpb, w["w_branch_b"], tb=True, out_dtype=BF16, name="d_o_b")
    g_branch_a = _mm(o_a, dpa, ta=True, out_dtype=BF16, name="g_w_branch_a")
    g_branch_b = _mm(o_b, dpb, ta=True, out_dtype=BF16, name="g_w_branch_b")

    dy_s5, du_direct, g_glu, g_bglu, g_s5d = _s5_post_bwd(y_raw, proj, u_cb, small["s5_d"], w["w_glu"], small["b_glu"], dob, "s5_post_bwd")
    du_g, d_tm, d_tbst, d_tcst, d_ta = _s5_core_bwd(_to_groups(dy_s5, BF16), u_g, x_g, tm_bf, tbst_bf, tcst_bf, ta, "s5_core_bwd")
    g_lam_re, g_lam_im, g_log_dt, g_b_re, g_b_im, g_c_re, g_c_im = tables_vjp((d_tm, d_tbst, d_tcst, d_ta))
    du = _from_groups(du_g) + du_direct

    dq, dkk, dvv, dr, dal, g_wa2, g_ba2, g_ghn = _gla_bwd(proj, alow, wa2, small["b_a2"], small["gla_norm_g"], o_pre, states, doa,
                                                        dk=dk, dv=dv, name="gla_bwd")
    dproj = jnp.concatenate([dga, dgb, dvv, dr, dq, dkk, du], axis=1)
    mid = {"w_out": g_out, "w_branch_a": g_branch_a, "w_branch_b": g_branch_b, "w_glu": g_glu.astype(BF16),
           "w_a2": g_wa2[:GLA_RANK].astype(BF16)}
    if dist:
        axes = dict(_REST)
        g_main_t, got = _mm(dproj, h1, ta=True, out_dtype=BF16, carry=("a2a", [_to_slots(mid[n], axes[n]) for n in mid]),
                            name="g_w_in_main")
        recv.update(zip(mid, got))
    else:
        g_main_t = _mm(dproj, h1, ta=True, out_dtype=BF16, name="g_w_in_main")
    g_al_t = _mm(dal, h1, ta=True, out_dtype=BF16, name="g_w_in_gate_rank")
    mrows = lambda o, n: g_main_t[o:o + n]
    g_w_in_t = jnp.concatenate([mrows(2 * d + 2 * dv, dk), mrows(2 * d + 2 * dv + dk, dk), mrows(2 * d, dv), mrows(2 * d + dv, dv),
                                g_al_t[:GLA_RANK], mrows(2 * d + 2 * dv + 2 * dk, s5w), mrows(0, d), mrows(d, d)], axis=0)
    if dist:
        dh1, (recv["w_in"],) = _mm(dproj, w_main_t, carry=("a2a", [_to_slots(g_w_in_t, 0)]), name="d_h1_main")
    else:
        dh1 = _mm(dproj, w_main_t, name="d_h1_main")
    dh1 = _mm(dal, w_al_t, res=dh1, name="d_h1_gate_rank")
    grad_x, d_norm1_g = _rms_bwd(x, small["norm1_g"], dh1, dx1, "norm1_bwd", False)

    small_g = {
        "norm1_g": d_norm1_g, "b_a2": g_ba2, "gla_norm_g": g_ghn, "lam_re": g_lam_re, "lam_im": g_lam_im,
        "log_dt": g_log_dt[None], "s5_b_re": g_b_re, "s5_b_im": g_b_im, "s5_c_re": g_c_re, "s5_c_im": g_c_im,
        "s5_d": g_s5d, "b_glu": g_bglu, "norm2_g": d_norm2_g, "final_norm_g": d_final_g,
    }
    if not dist:
        recv = dict(mid, w_in=g_w_in_t, w_ffn_in=_join_slots(g_ffn_in_s, 1), w_ffn_out=g_ffn_out)
    return loss[0, 0], grad_x, recv, small_g


def _small_2d(name, a):
    a = a[0]
    return a[None] if a.ndim == 1 else a


def kernel(x, norm1_g, w_in, w_a2, b_a2, gla_norm_g, lam_re, lam_im, log_dt, s5_b_re, s5_b_im, s5_c_re, s5_c_im, s5_d, w_glu, b_glu, w_branch_a, w_branch_b, w_out, norm2_g, w_ffn_in, w_ffn_out, final_norm_g, loss_target, m_norm1_g, m_w_in, m_w_a2, m_b_a2, m_gla_norm_g, m_lam_re, m_lam_im, m_log_dt, m_s5_b_re, m_s5_b_im, m_s5_c_re, m_s5_c_im, m_s5_d, m_w_glu, m_b_glu, m_w_branch_a, m_w_branch_b, m_w_out, m_norm2_g, m_w_ffn_in, m_w_ffn_out, m_final_norm_g, v_norm1_g, v_w_in, v_w_a2, v_b_a2, v_gla_norm_g, v_lam_re, v_lam_im, v_log_dt, v_s5_b_re, v_s5_b_im, v_s5_c_re, v_s5_c_im, v_s5_d, v_w_glu, v_b_glu, v_w_branch_a, v_w_branch_b, v_w_out, v_norm2_g, v_w_ffn_in, v_w_ffn_out, v_final_norm_g):
    args = dict(locals())
    weights = {n: args[n] for n in _ORDER}
    m_in = {n: args["m_" + n] for n in _ORDER}
    v_in = {n: args["v_" + n] for n in _ORDER}
    d = x.shape[-1]

    w_in_t_shard = weights["w_in"][0].T.astype(BF16)
    w_in_t = _exchange("ag", [w_in_t_shard], "w_in_all_gather")[0].reshape(-1, d)
    rest = [weights[n][0].astype(BF16) for n, _ in _REST]
    small = {n: _small_2d(n, weights[n]) for n in _SMALL}
    loss_local, grad_x, recv, small_g = _local_step(x[0], loss_target[0], w_in_t, small, rest)

    grads, delta, new_m, new_v = {}, {}, {}, {}
    for n, _ in _REST:
        grads[n], delta[n], new_m[n], new_v[n] = _adamw(weights[n], recv[n], m_in[n], v_in[n], "adamw_" + n)
    grads["w_in"] = _slot_sum(recv["w_in"], "w_in_grad_slot_sum").T
    delta["w_in"], new_m["w_in"], new_v["w_in"] = _adamw(weights["w_in"], grads["w_in"], m_in["w_in"], v_in["w_in"], "adamw_w_in")

    s_sizes = [small_g[n].size for n in _SMALL]
    s_offs = [sum(s_sizes[:i]) for i in range(len(s_sizes))]
    s_total = sum(s_sizes)
    s_rows = -(-(-(-(s_total + 1) // LANE)) // LANE) * LANE

    def pack_small(parts):
        flat = jnp.concatenate([p.reshape(-1) for p in parts])
        return jnp.pad(flat, (0, s_rows * LANE - flat.size)).reshape(s_rows, LANE)

    s_flat = pack_small([small_g[n] for n in _SMALL] + [loss_local])
    s_red = _slot_sum(_exchange("ag", [s_flat], "small_grads_all_gather")[0], "small_grads_slot_sum")
    loss = s_red.reshape(-1)[s_total]
    sd, sm, sv = _adamw(pack_small([weights[n] for n in _SMALL])[None], s_red, pack_small([m_in[n] for n in _SMALL])[None],
                        pack_small([v_in[n] for n in _SMALL])[None], "adamw_small")
    for n, o, s in zip(_SMALL, s_offs, s_sizes):
        shape = weights[n].shape[1:]
        grads[n], delta[n], new_m[n], new_v[n] = (a.reshape(-1)[o:o + s].reshape(shape) for a in (s_red, sd, sm, sv))

    out = [loss, grad_x[None]]
    for tree in (grads, delta, new_m, new_v):
        out += [tree[n].reshape(weights[n].shape) for n in _ORDER]
    return tuple(out)
```

```python
import functools
import math

import jax
import jax.numpy as jnp
from jax import lax
from jax.experimental import pallas as pl
from jax.experimental.pallas import tpu as pltpu

F32 = jnp.float32
BF16 = jnp.bfloat16

NORM_EPS = 1e-6
N_DEV = 8
N_PEER = N_DEV - 1
GLA_HEADS = 4
GLA_CHUNK = 32
GLA_CHUNK_SHIFT = 5
GLA_TAU = 16.0
GLA_RANK = 16
GLA_BLOCK = 256
S5_GC = 16
S5_P = 64
S5_L = 16
LANE = 128
V7X_VMEM_LIMIT = 56 * 1024 * 1024
V7X_MM_VMEM_BUDGET = 40 * 1024 * 1024
V7X_MM_TILE_MN = 1408
V7X_MM_TILE_K = 2048
V7X_EPI_ROW_CHUNKS = 4

ADAM_LR = 0.001
ADAM_B1 = 0.9
ADAM_B2 = 0.999
ADAM_EPS = 1e-08
ADAM_WD = 0.01
ADAM_STEP = 10

GELU_C = math.sqrt(2.0 / math.pi)
GELU_A = 0.044715

MESH = pl.DeviceIdType.MESH


def _cparams(*sem):
    return pltpu.CompilerParams(dimension_semantics=sem, vmem_limit_bytes=V7X_VMEM_LIMIT)


def _divisors_down(n, start, align=LANE):
    t = (min(start, n) // align) * align
    found = False
    while t >= align:
        if n % t == 0:
            found = True
            yield t
        t -= align
    if not found:
        yield n


def _tile(n, target, align=LANE):
    return next(_divisors_down(n, target, align))


def _sigmoid(x):
    return 1.0 / (1.0 + jnp.exp(-x))


_HBM_SPEC = pl.BlockSpec(memory_space=pltpu.HBM)


def _exchange_scratch(n):
    return [pltpu.SemaphoreType.DMA((n * N_PEER,)), pltpu.SemaphoreType.DMA((n * N_PEER,)), pltpu.SemaphoreType.DMA((n,))]


def _ag_phases(x_refs, out_refs, send_sems, recv_sems, local_sems):
    n = len(x_refs)
    x, y, c = lax.axis_index("x"), lax.axis_index("y"), lax.axis_index("c")
    me, sibling = (x, y, c), (x, y, 1 - c)
    chips = [(1 - x, y), (x, 1 - y), (1 - x, 1 - y)]

    def copy(a, k, block, to, from_input=False):
        dst = out_refs[a].at[4 * block[0] + 2 * block[1] + block[2]]
        return pltpu.make_async_remote_copy(
            src_ref=x_refs[a] if from_input else dst, dst_ref=dst,
            send_sem=send_sems.at[a * N_PEER + k], recv_sem=recv_sems.at[a * N_PEER + k], device_id=to, device_id_type=MESH)

    def local(a):
        return pltpu.make_async_copy(x_refs[a], out_refs[a].at[4 * x + 2 * y + c], local_sems.at[a])

    def first(a):
        return [copy(a, 0, me, sibling, True)] + [copy(a, 1 + j, me, (*chip, c), True) for j, chip in enumerate(chips)]

    def start():
        for a in range(n):
            local(a).start()
            for cp in first(a):
                cp.start()

    def relay():
        for j, chip in enumerate(chips):
            for a in range(n):
                copy(a, 1 + j, (*chip, c), me).wait_recv()
                copy(a, 4 + j, (*chip, c), sibling).start()

    def finish():
        for a in range(n):
            copy(a, 0, sibling, me).wait_recv()
            for j, chip in enumerate(chips):
                copy(a, 4 + j, (*chip, 1 - c), me).wait_recv()
        for a in range(n):
            for cp in first(a) + [copy(a, 4 + j, (*chip, c), sibling) for j, chip in enumerate(chips)]:
                cp.wait_send()
            local(a).wait()

    return start, relay, finish


_ALL_K = tuple(range(N_DEV))


def _a2a_phases(x_refs, out_refs, send_sems, recv_sems, local_sems, ks_list=None):
    n = len(x_refs)
    ks_list = ks_list or [_ALL_K] * n
    x, y, c = lax.axis_index("x"), lax.axis_index("y"), lax.axis_index("c")
    my = 4 * x + 2 * y + c

    def copy(a, k):
        px, py, pc = (1 - x if k & 4 else x), (1 - y if k & 2 else y), (1 - c if k & 1 else c)
        return pltpu.make_async_remote_copy(
            src_ref=x_refs[a].at[4 * px + 2 * py + pc], dst_ref=out_refs[a].at[ks_list[a].index(k)],
            send_sem=send_sems.at[a * N_PEER + k - 1], recv_sem=recv_sems.at[a * N_PEER + k - 1],
            device_id=(px, py, pc), device_id_type=MESH)

    def local(a):
        return pltpu.make_async_copy(x_refs[a].at[my], out_refs[a].at[ks_list[a].index(0)], local_sems.at[a])

    def start():
        for a in range(n):
            for k in ks_list[a]:
                (copy(a, k) if k else local(a)).start()

    def relay():
        pass

    def finish():
        for a in range(n):
            for k in ks_list[a]:
                if k:
                    copy(a, k).wait_recv()
        for a in range(n):
            for k in ks_list[a]:
                if k:
                    copy(a, k).wait_send()
                else:
                    local(a).wait()

    return start, relay, finish


def _exchange_out_shapes(kind, arrays, ks_list=None):
    if kind == "ag":
        return [jax.ShapeDtypeStruct((N_DEV,) + a.shape, a.dtype) for a in arrays]
    ks_list = ks_list or [_ALL_K] * len(arrays)
    return [jax.ShapeDtypeStruct((len(ks),) + a.shape[1:], a.dtype) for a, ks in zip(arrays, ks_list)]


def _exchange(kind, arrays, name):
    n = len(arrays)
    phases = _ag_phases if kind == "ag" else _a2a_phases

    def body(*refs):
        start, relay, finish = phases(refs[:n], refs[n:2 * n], *refs[2 * n:])
        start()
        relay()
        finish()

    return pl.pallas_call(
        body, name=name,
        out_shape=_exchange_out_shapes(kind, arrays),
        in_specs=[_HBM_SPEC] * n, out_specs=[_HBM_SPEC] * n,
        scratch_shapes=_exchange_scratch(n),
    )(*arrays)


def _mm_tiles(m, n_unit, k_unit, tile_bytes):
    tm = _tile(m, V7X_MM_TILE_MN)
    tn = _tile(n_unit, V7X_MM_TILE_MN)
    for tk in _divisors_down(k_unit, V7X_MM_TILE_K):
        if 2 * 2 * (tm * tk + tk * tn) + tile_bytes * tm * tn <= V7X_MM_VMEM_BUDGET:
            return tm, tn, tk
    return tm, tn, _tile(k_unit, LANE)


def _carry_parts(carry):
    kind, arrays, ks_list = (tuple(carry) + (None,))[:3] if carry is not None else (None, [], None)
    n = len(arrays)
    kind = (kind, ks_list)
    return kind, arrays, [_HBM_SPEC] * n, _exchange_out_shapes(kind[0], arrays, ks_list), (_exchange_scratch(n) if n else [])


def _carry_hooks(kind, x_refs, out_refs, sems, step, last_step):
    if not x_refs:
        return lambda: None
    kind, ks_list = kind
    if kind == "ag":
        start, relay, finish = _ag_phases(x_refs, out_refs, *sems)
    else:
        start, relay, finish = _a2a_phases(x_refs, out_refs, *sems, ks_list=ks_list)
    pl.when(step == 0)(start)

    def after():
        if kind == "ag":
            pl.when(step == (last_step * 7) // 8)(relay)
        pl.when(step == last_step)(finish)

    return after


def _mm(a, b, *, ta=False, tb=False, out_dtype=F32, res=None, carry=None, a_slots=False, b_slots=False, out_slots=0,
        epi=None, name):
    if a_slots:
        assert not ta
        a_n, m, a_c = a.shape
        k = a_n * a_c
    else:
        m, k = (a.shape[1], a.shape[0]) if ta else a.shape
    if b_slots:
        b_n, b_r, b_c = b.shape
        k2, n = (b_n * b_c, b_r) if tb else (b_r, b_n * b_c)
    else:
        k2, n = (b.shape[1], b.shape[0]) if tb else b.shape
    assert k == k2, (a.shape, b.shape, ta, tb)
    has_res = res is not None
    assert not (has_res and (out_slots or epi))
    n_units = [n] + ([n // out_slots] if out_slots else []) + ([b_c] if b_slots and not tb else [])
    k_units = [k] + ([a_c] if a_slots else []) + ([b_c] if b_slots and tb else [])
    n_unit, k_unit = min(n_units), min(k_units)
    assert all(u % n_unit == 0 for u in n_units) and all(u % k_unit == 0 for u in k_units)
    epi_fn, epi_ins, epi_outs = epi if epi is not None else (None, [], [])
    tile_bytes = 4 + (2 * res.dtype.itemsize if has_res else 0)
    tile_bytes += sum(2 * e.shape[0] * e.dtype.itemsize for e in epi_ins)
    tile_bytes += sum(2 * l * jnp.dtype(dt).itemsize for l, dt in epi_outs) if epi else 2 * jnp.dtype(out_dtype).itemsize
    tm, tn, tk = _mm_tiles(m, n_unit, k_unit, tile_bytes)
    ni, nj, nk = m // tm, n // tn, k // tk
    dims = (((0,) if ta else (1,), (1,) if tb else (0,)), ((), ()))

    def slot_map(per, pos):
        if pos == "k_cols":
            return lambda i, j, kk: (kk // per, i, kk % per)
        if pos == "k_cols_j":
            return lambda i, j, kk: (kk // per, j, kk % per)
        if pos == "n_cols_k":
            return lambda i, j, kk: (j // per, kk, j % per)
        return lambda i, j, kk: (j // per, i, j % per)

    if a_slots:
        a_spec = pl.BlockSpec((None, tm, tk), slot_map(a_c // tk, "k_cols"))
    else:
        a_spec = pl.BlockSpec((tk, tm), lambda i, j, kk: (kk, i)) if ta else pl.BlockSpec((tm, tk), lambda i, j, kk: (i, kk))
    if b_slots and tb:
        b_spec = pl.BlockSpec((None, tn, tk), slot_map(b_c // tk, "k_cols_j"))
    elif b_slots:
        b_spec = pl.BlockSpec((None, tk, tn), slot_map(b_c // tn, "n_cols_k"))
    else:
        b_spec = pl.BlockSpec((tn, tk), lambda i, j, kk: (j, kk)) if tb else pl.BlockSpec((tk, tn), lambda i, j, kk: (kk, j))
    if epi:
        lead_spec = lambda l: pl.BlockSpec((l, tm, tn), lambda i, j, kk: (0, i, j))
        o_specs = [lead_spec(l) for l, _ in epi_outs]
        o_shapes = [jax.ShapeDtypeStruct((l, m, n), dt) for l, dt in epi_outs]
    elif out_slots:
        o_specs = [pl.BlockSpec((None, tm, tn), slot_map((n // out_slots) // tn, "n_cols_i"))]
        o_shapes = [jax.ShapeDtypeStruct((out_slots, m, n // out_slots), out_dtype)]
    else:
        o_specs = [pl.BlockSpec((tm, tn), lambda i, j, kk: (i, j))]
        o_shapes = [jax.ShapeDtypeStruct((m, n), out_dtype)]
    extra_ins = ([res] if has_res else []) + list(epi_ins)
    extra_specs = ([o_specs[0]] if has_res else []) + [pl.BlockSpec((e.shape[0], tm, tn), lambda i, j, kk: (0, i, j)) for e in epi_ins]
    n_in, n_out = 2 + len(extra_ins), len(o_specs)
    c_kind, c_arrays, c_specs, c_shapes, c_scratch = _carry_parts(carry)
    nc = len(c_arrays)
    last_step = ni * nj * nk - 1

    def body(*refs):
        a_ref, b_ref = refs[0], refs[1]
        e_refs = refs[2:n_in]
        x_refs = refs[n_in:n_in + nc]
        o_refs = refs[n_in + nc:n_in + nc + n_out]
        out_refs = refs[n_in + nc + n_out:n_in + 2 * nc + n_out]
        scratch = refs[n_in + 2 * nc + n_out:]
        acc = scratch[0] if nk > 1 else None
        kk = pl.program_id(2)
        step = (pl.program_id(0) * nj + pl.program_id(1)) * nk + kk
        after = _carry_hooks(c_kind, x_refs, out_refs, scratch[-3:], step, last_step)

        def emit(val):
            if has_res:
                val = val + e_refs[0][...].astype(F32)
            if epi:
                for o_ref, parts in zip(o_refs, epi_fn(val, *[e[...] for e in e_refs])):
                    for l, v in enumerate(parts):
                        o_ref[l] = v.astype(o_ref.dtype)
            else:
                o_refs[0][...] = val.astype(out_dtype)

        if epi and nk == 1 and not ta:
            rc = tm // V7X_EPI_ROW_CHUNKS
            for r in range(V7X_EPI_ROW_CHUNKS):
                rows = slice(r * rc, (r + 1) * rc)
                val = lax.dot_general(a_ref[rows, :], b_ref[...], dims, preferred_element_type=F32)
                for o_ref, parts in zip(o_refs, epi_fn(val, *[e[:, rows, :] for e in e_refs])):
                    for l, v in enumerate(parts):
                        o_ref[l, rows, :] = v.astype(o_ref.dtype)
            after()
            return
        part = lax.dot_general(a_ref[...], b_ref[...], dims, preferred_element_type=F32)
        if nk == 1:
            emit(part)
        else:
            @pl.when(kk == 0)
            def _():
                acc[...] = part

            @pl.when(kk > 0)
            def _():
                acc[...] += part

            @pl.when(kk == nk - 1)
            def _():
                emit(acc[...])

        after()

    sem = ("arbitrary",) * 3 if nc else ("parallel", "parallel", "arbitrary")
    outs = pl.pallas_call(
        body, name=name,
        grid=(ni, nj, nk),
        in_specs=[a_spec, b_spec] + extra_specs + c_specs,
        out_specs=o_specs + c_specs,
        out_shape=o_shapes + c_shapes,
        scratch_shapes=([pltpu.VMEM((tm, tn), F32)] if nk > 1 else []) + c_scratch,
        compiler_params=_cparams(*sem),
    )(a, b, *extra_ins, *c_arrays)
    main = list(outs[:n_out]) if epi else outs[0]
    return (main, list(outs[n_out:])) if nc else main


def _ffn_in_fused(h2, w_s, *, carry=None, name):
    t, d = h2.shape
    n_slot, _, c = w_s.shape
    half = n_slot // 2
    tm = _tile(t, 512)
    c_kind, c_arrays, c_specs, c_shapes, c_scratch = _carry_parts(carry)
    nc = len(c_arrays)
    last_step = (t // tm) * half - 1

    def body(h_ref, wg_ref, wu_ref, *refs):
        x_refs, (gu_ref, act_ref), out_refs, sems = refs[:nc], refs[nc:nc + 2], refs[nc + 2:2 * nc + 2], refs[2 * nc + 2:]
        step = pl.program_id(0) * half + pl.program_id(1)
        after = _carry_hooks(c_kind, x_refs, out_refs, sems, step, last_step)
        h = h_ref[...]
        g = jnp.dot(h, wg_ref[...], preferred_element_type=F32)
        u = jnp.dot(h, wu_ref[...], preferred_element_type=F32)
        gu_ref[0] = g.astype(BF16)
        gu_ref[1] = u.astype(BF16)
        act_ref[...] = (g * _sigmoid(g) * u).astype(BF16)
        after()

    outs = pl.pallas_call(
        body, name=name,
        grid=(t // tm, half),
        in_specs=[pl.BlockSpec((tm, d), lambda i, j: (i, 0)),
                  pl.BlockSpec((None, d, c), lambda i, j: (j, 0, 0)),
                  pl.BlockSpec((None, d, c), lambda i, j: (half + j, 0, 0))] + c_specs,
        out_specs=[pl.BlockSpec((2, tm, c), lambda i, j: (0, i, j)), pl.BlockSpec((tm, c), lambda i, j: (i, j))] + c_specs,
        out_shape=[jax.ShapeDtypeStruct((2, t, half * c), BF16), jax.ShapeDtypeStruct((t, half * c), BF16)] + c_shapes,
        scratch_shapes=c_scratch,
        compiler_params=_cparams(*(("arbitrary",) * 2 if nc else ("parallel", "parallel"))),
    )(h2, w_s, w_s, *c_arrays)
    return (outs[0], outs[1], list(outs[2:])) if nc else (outs[0], outs[1])


def _rowwise(fn, ins, row_outs, acc_outs, *, rows, tb, name, carry=None):
    in_specs, args = [], []
    for spec in ins:
        kind, arr = spec[0], spec[1]
        if kind == "row":
            in_specs.append(pl.BlockSpec((tb, arr.shape[1]), lambda i: (i, 0)))
        elif kind == "win":
            width, cb = spec[2], spec[3]
            in_specs.append(pl.BlockSpec((tb, width), functools.partial(lambda i, cb: (i, cb), cb=cb)))
        else:
            in_specs.append(pl.BlockSpec(arr.shape, lambda i: (0, 0)))
        args.append(arr)
    out_specs = [pl.BlockSpec((tb, c), lambda i: (i, 0)) for c, _ in row_outs]
    out_specs += [pl.BlockSpec(shape, lambda i: (0, 0)) for shape in acc_outs]
    out_shape = [jax.ShapeDtypeStruct((rows, c), dt) for c, dt in row_outs]
    out_shape += [jax.ShapeDtypeStruct(shape, F32) for shape in acc_outs]
    n_in, n_row, n_out = len(ins), len(row_outs), len(row_outs) + len(acc_outs)
    c_kind, c_arrays, c_specs, c_shapes, c_scratch = _carry_parts(carry)
    nc = len(c_arrays)

    def body(*refs):
        after = _carry_hooks(c_kind, refs[n_in:n_in + nc], refs[n_in + nc + n_out:n_in + 2 * nc + n_out],
                             refs[n_in + 2 * nc + n_out:], pl.program_id(0), rows // tb - 1)
        vals = [r[...] for r in refs[:n_in]]
        outs = fn(*vals)
        if not isinstance(outs, (tuple, list)):
            outs = (outs,)
        out_refs = refs[n_in + nc:n_in + nc + n_out]
        for o_ref, val in zip(out_refs[:n_row], outs[:n_row]):
            o_ref[...] = val.astype(o_ref.dtype)
        first = pl.program_id(0) == 0
        for o_ref, val in zip(out_refs[n_row:], outs[n_row:]):
            @pl.when(first)
            def _(o_ref=o_ref):
                o_ref[...] = jnp.zeros_like(o_ref)
            o_ref[...] += val
        after()

    res = pl.pallas_call(
        body, name=name,
        grid=(rows // tb,),
        in_specs=in_specs + c_specs, out_specs=out_specs + c_specs, out_shape=out_shape + c_shapes,
        scratch_shapes=c_scratch,
        compiler_params=_cparams("arbitrary"),
    )(*args, *c_arrays)
    return (list(res[:n_out]), list(res[n_out:])) if nc else res


def _rms_fwd(x, g, name, carry=None):
    def fn(xv, gv):
        r = lax.rsqrt(jnp.mean(xv * xv, axis=-1, keepdims=True) + NORM_EPS)
        return (xv * r * gv,)
    res = _rowwise(fn, [("row", x), ("full", g)], [(x.shape[1], BF16)], [], rows=x.shape[0], tb=_tile(x.shape[0], 512),
                   name=name, carry=carry)
    return (res[0][0], res[1]) if carry is not None else res[0]


def _rms_bwd(x, g, dh, dres, name, want_bf16):
    d = x.shape[1]

    def fn(xv, gv, dhv, drv):
        r = lax.rsqrt(jnp.mean(xv * xv, axis=-1, keepdims=True) + NORM_EPS)
        xhat = xv * r
        dhv = dhv.astype(F32)
        dxhat = dhv * gv
        dx = drv + r * (dxhat - xhat * jnp.mean(dxhat * xhat, axis=-1, keepdims=True))
        dg = jnp.sum(dhv * xhat, axis=0, keepdims=True)
        return (dx, dx, dg) if want_bf16 else (dx, dg)

    row_outs = [(d, F32), (d, BF16)] if want_bf16 else [(d, F32)]
    return _rowwise(fn, [("row", x), ("full", g), ("row", dh), ("row", dres)], row_outs, [(1, d)],
                    rows=x.shape[0], tb=_tile(x.shape[0], 256), name=name)


def _loss_head(x2, g, target, name):
    d = x2.shape[1]

    def fn(xv, gv, tv):
        r = lax.rsqrt(jnp.mean(xv * xv, axis=-1, keepdims=True) + NORM_EPS)
        xhat = xv * r
        diff = xhat * gv - tv
        loss = 0.5 * jnp.sum(jnp.mean(diff * diff, axis=-1, keepdims=True), axis=0, keepdims=True)
        dy = diff * (1.0 / d)
        dxhat = dy * gv
        dx = r * (dxhat - xhat * jnp.mean(dxhat * xhat, axis=-1, keepdims=True))
        dg = jnp.sum(dy * xhat, axis=0, keepdims=True)
        return dx, dx, dg, jnp.broadcast_to(loss, (1, LANE))

    return _rowwise(fn, [("row", x2), ("full", g), ("row", target)], [(d, F32), (d, BF16)], [(1, d), (1, LANE)],
                    rows=x2.shape[0], tb=_tile(x2.shape[0], 256), name=name)


def _swiglu_bwd_tile(dact, gu):
    g, u = gu[0].astype(F32), gu[1].astype(F32)
    sg = _sigmoid(g)
    return ((dact * u * (sg * (1.0 + g * (1.0 - sg))), dact * (g * sg)),)


def _mix_fwd(proj, pa, pb, d, name):
    def fn(ga, gb, av, bv):
        return (_sigmoid(ga.astype(F32)) * av.astype(F32) + _sigmoid(gb.astype(F32)) * bv.astype(F32),)
    return _rowwise(fn, [("win", proj, d, 0), ("win", proj, d, 1), ("row", pa), ("row", pb)], [(d, BF16)], [],
                    rows=pa.shape[0], tb=_tile(pa.shape[0], 512), name=name)[0]


def _mix_bwd(proj, pa, pb, dmix, d, name):
    def fn(ga, gb, av, bv, dm):
        dm = dm.astype(F32)
        sa, sb = _sigmoid(ga.astype(F32)), _sigmoid(gb.astype(F32))
        av, bv = av.astype(F32), bv.astype(F32)
        return dm * sa, dm * sb, dm * av * sa * (1.0 - sa), dm * bv * sb * (1.0 - sb)
    return _rowwise(fn, [("win", proj, d, 0), ("win", proj, d, 1), ("row", pa), ("row", pb), ("row", dmix)],
                    [(d, BF16)] * 4, [], rows=pa.shape[0], tb=_tile(pa.shape[0], 512), name=name)


def _chunk_masks(tb):
    r = lax.broadcasted_iota(jnp.int32, (tb, tb), 0)
    c = lax.broadcasted_iota(jnp.int32, (tb, tb), 1)
    same = lax.shift_right_logical(r, GLA_CHUNK_SHIFT) == lax.shift_right_logical(c, GLA_CHUNK_SHIFT)
    return same, same & (c <= r), same & (r <= c)


def _mask_bf16(mask):
    return jnp.where(mask, 1.0, 0.0).astype(BF16)


def _split_dot(mask_bf, x, terms):
    acc, rem = None, x
    for _ in range(terms):
        hi = rem.astype(BF16)
        part = jnp.dot(mask_bf, hi, preferred_element_type=F32)
        acc = part if acc is None else acc + part
        rem = rem - hi.astype(F32)
    return acc


def _gla_decay(al, wa2, ba2, same_bf, causal_bf):
    z = jnp.dot(al.astype(BF16), wa2, preferred_element_type=F32) + ba2
    la = (jnp.minimum(z, 0.0) - jnp.log(1.0 + jnp.exp(-jnp.abs(z)))) * (1.0 / GLA_TAU)
    bc = _split_dot(causal_bf, la, 3)
    bl = _split_dot(same_bf, la, 3)
    return z, bc, bl


def _dot_t(a, b, ca, cb):
    return lax.dot_general(a, b, (((ca,), (cb,)), ((), ())), preferred_element_type=F32)


def _gla_fwd(proj, alow, wa2, ba2, ghn, *, dk, dv, name):
    t = proj.shape[0]
    tb = min(GLA_BLOCK, t)
    nch = tb // GLA_CHUNK
    hk, hv = dk // GLA_HEADS, dv // GLA_HEADS
    scale = hk ** -0.5
    v_cb, r_cb = (8 * dk) // dv, (8 * dk) // dv + 1
    q_cb, k_cb = (8 * dk + 2 * dv) // dk, (8 * dk + 2 * dv) // dk + 1

    def body(q_ref, k_ref, v_ref, r_ref, al_ref, wa2_ref, ba2_ref, ghn_ref, oa_ref, opre_ref, s_ref, st_scr):
        @pl.when(pl.program_id(0) == 0)
        def _():
            st_scr[...] = jnp.zeros_like(st_scr)

        same, causal, _ = _chunk_masks(tb)
        same_bf, causal_bf = _mask_bf16(same), _mask_bf16(causal)
        _, bc, bl = _gla_decay(al_ref[...], wa2_ref[...], ba2_ref[...], same_bf, causal_bf)
        q = q_ref[...].astype(F32) * scale
        k = k_ref[...].astype(F32)
        qd = (q * jnp.exp(bc)).astype(BF16)
        ki = (k * jnp.exp(-bc)).astype(BF16)
        ks = (k * jnp.exp(bl - bc)).astype(BF16)
        dl = jnp.exp(bl)
        ksls = [slice(h * hk, (h + 1) * hk) for h in range(GLA_HEADS)]
        vsls = [slice(h * hv, (h + 1) * hv) for h in range(GLA_HEADS)]
        v_hs = [v_ref[:, vsl] for vsl in vsls]
        o_intras = []
        for ksl, v_h in zip(ksls, v_hs):
            sc = jnp.where(causal, _dot_t(qd[:, ksl], ki[:, ksl], 1, 1), 0.0)
            o_intras.append(jnp.dot(sc.astype(BF16), v_h, preferred_element_type=F32))
        for c in range(nch):
            rows = slice(c * GLA_CHUNK, (c + 1) * GLA_CHUNK)
            for h, (ksl, vsl) in enumerate(zip(ksls, vsls)):
                st = st_scr[h]
                s_ref[c, h] = st
                opre_ref[rows, vsl] = o_intras[h][rows] + _dot_t(qd[rows, ksl], st.astype(BF16), 1, 1)
                st_scr[h] = dl[c * GLA_CHUNK:c * GLA_CHUNK + 1, ksl] * st + _dot_t(v_hs[h][rows], ks[rows, ksl], 0, 0)
        for h in range(GLA_HEADS):
            vsl = slice(h * hv, (h + 1) * hv)
            o = opre_ref[:, vsl]
            rs = lax.rsqrt(jnp.mean(o * o, axis=-1, keepdims=True) + NORM_EPS)
            rv = r_ref[:, vsl].astype(F32)
            oa_ref[:, vsl] = (rv * _sigmoid(rv) * (o * rs * ghn_ref[:, vsl])).astype(BF16)

    nchunks = t // GLA_CHUNK
    return pl.pallas_call(
        body, name=name,
        grid=(t // tb,),
        in_specs=[
            pl.BlockSpec((tb, dk), lambda i: (i, q_cb)),
            pl.BlockSpec((tb, dk), lambda i: (i, k_cb)),
            pl.BlockSpec((tb, dv), lambda i: (i, v_cb)),
            pl.BlockSpec((tb, dv), lambda i: (i, r_cb)),
            pl.BlockSpec((tb, LANE), lambda i: (i, 0)),
            pl.BlockSpec(wa2.shape, lambda i: (0, 0)),
            pl.BlockSpec(ba2.shape, lambda i: (0, 0)),
            pl.BlockSpec(ghn.shape, lambda i: (0, 0)),
        ],
        out_specs=[
            pl.BlockSpec((tb, dv), lambda i: (i, 0)),
            pl.BlockSpec((tb, dv), lambda i: (i, 0)),
            pl.BlockSpec((nch, GLA_HEADS, hv, hk), lambda i: (i, 0, 0, 0)),
        ],
        out_shape=[
            jax.ShapeDtypeStruct((t, dv), BF16),
            jax.ShapeDtypeStruct((t, dv), F32),
            jax.ShapeDtypeStruct((nchunks, GLA_HEADS, hv, hk), F32),
        ],
        scratch_shapes=[pltpu.VMEM((GLA_HEADS, hv, hk), F32)],
        compiler_params=_cparams("arbitrary"),
    )(proj, proj, proj, proj, alow, wa2, ba2, ghn)


def _gla_bwd(proj, alow, wa2, ba2, ghn, opre, states, doa, *, dk, dv, name):
    t = proj.shape[0]
    tb = min(GLA_BLOCK, t)
    nb = t // tb
    nch = tb // GLA_CHUNK
    hk, hv = dk // GLA_HEADS, dv // GLA_HEADS
    scale = hk ** -0.5
    v_cb, r_cb = (8 * dk) // dv, (8 * dk) // dv + 1
    q_cb, k_cb = (8 * dk + 2 * dv) // dk, (8 * dk + 2 * dv) // dk + 1

    def body(q_ref, k_ref, v_ref, r_ref, al_ref, wa2_ref, ba2_ref, ghn_ref, opre_ref, s_ref, doa_ref,
             dq_ref, dk_ref, dv_ref, dr_ref, dal_ref, dwa2_ref, dba2_ref, dghn_ref,
             dst_scr, dqd_scr, dki_scr, dks_scr, ddl_scr):
        @pl.when(pl.program_id(0) == 0)
        def _():
            dst_scr[...] = jnp.zeros_like(dst_scr)
            dwa2_ref[...] = jnp.zeros_like(dwa2_ref)
            dba2_ref[...] = jnp.zeros_like(dba2_ref)
            dghn_ref[...] = jnp.zeros_like(dghn_ref)

        same, causal, anti = _chunk_masks(tb)
        same_bf, causal_bf, anti_bf = _mask_bf16(same), _mask_bf16(causal), _mask_bf16(anti)
        al = al_ref[...]
        wa2v = wa2_ref[...]
        z, bc, bl = _gla_decay(al, wa2v, ba2_ref[...], same_bf, causal_bf)
        e_bc, e_nbc, e_st = jnp.exp(bc), jnp.exp(-bc), jnp.exp(bl - bc)
        q = q_ref[...].astype(F32) * scale
        k = k_ref[...].astype(F32)
        qd_f, ki_f, ks_f = q * e_bc, k * e_nbc, k * e_st
        qd, ki, ks = qd_f.astype(BF16), ki_f.astype(BF16), ks_f.astype(BF16)
        dl = jnp.exp(bl)
        per_head = []
        for h in range(GLA_HEADS):
            ksl = slice(h * hk, (h + 1) * hk)
            vsl = slice(h * hv, (h + 1) * hv)
            o = opre_ref[:, vsl]
            rs = lax.rsqrt(jnp.mean(o * o, axis=-1, keepdims=True) + NORM_EPS)
            ohat = o * rs
            g_h = ghn_ref[:, vsl]
            rv = r_ref[:, vsl].astype(F32)
            sg = _sigmoid(rv)
            d_oa = doa_ref[:, vsl].astype(F32)
            don = d_oa * (rv * sg)
            dr_ref[:, vsl] = (d_oa * (ohat * g_h) * (sg * (1.0 + rv * (1.0 - sg)))).astype(BF16)
            dghn_ref[:, vsl] += jnp.sum(don * ohat, axis=0, keepdims=True)
            dohat = don * g_h
            do_f = rs * (dohat - ohat * jnp.mean(dohat * ohat, axis=-1, keepdims=True))
            do = do_f.astype(BF16)
            v_h = v_ref[:, vsl]
            p = jnp.where(causal, _dot_t(do, v_h, 1, 1), 0.0).astype(BF16)
            dqd_intra = jnp.dot(p, ki[:, ksl], preferred_element_type=F32)
            dki_scr[:, ksl] = _dot_t(p, qd[:, ksl], 0, 0)
            sc = jnp.where(causal, _dot_t(qd[:, ksl], ki[:, ksl], 1, 1), 0.0).astype(BF16)
            dv_intra = _dot_t(sc, do, 0, 0)
            per_head.append((ksl, vsl, v_h, do, dqd_intra, dv_intra))
        for c in reversed(range(nch)):
            rows = slice(c * GLA_CHUNK, (c + 1) * GLA_CHUNK)
            for h, (ksl, vsl, v_h, do, dqd_intra, dv_intra) in enumerate(per_head):
                dst = dst_scr[h]
                st = s_ref[c, h]
                dst_bf = dst.astype(BF16)
                dv_ref[rows, vsl] = (dv_intra[rows] + _dot_t(ks[rows, ksl], dst_bf, 1, 1)).astype(BF16)
                dks_scr[rows, ksl] = jnp.dot(v_h[rows], dst_bf, preferred_element_type=F32)
                dl_c = dl[c * GLA_CHUNK:c * GLA_CHUNK + 1, ksl]
                ddl = jnp.sum(dst * st, axis=0, keepdims=True) * dl_c
                ddl_scr[rows, ksl] = jnp.broadcast_to(ddl, (GLA_CHUNK, hk))
                dqd_scr[rows, ksl] = dqd_intra[rows] + jnp.dot(do[rows], st.astype(BF16), preferred_element_type=F32)
                dst_scr[h] = dl_c * dst + _dot_t(do[rows], qd[rows, ksl], 0, 0)
        dqd, dki, dks = dqd_scr[...], dki_scr[...], dks_scr[...]
        dq_ref[...] = (dqd * (scale * e_bc)).astype(BF16)
        dk_ref[...] = (dki * e_nbc + dks * e_st).astype(BF16)
        dks_ks = dks * ks_f
        dbc = dqd * qd_f - dki * ki_f - dks_ks
        dla = _split_dot(anti_bf, dbc, 2) + _split_dot(same_bf, dks_ks, 2) + ddl_scr[...]
        dz = (dla * (1.0 / GLA_TAU) * (1.0 - _sigmoid(z)))
        dz_bf = dz.astype(BF16)
        dal_ref[...] = _dot_t(dz_bf, wa2v, 1, 1).astype(BF16)
        dwa2_ref[...] += _dot_t(al.astype(BF16), dz_bf, 0, 0)
        dba2_ref[...] += jnp.sum(dz, axis=0, keepdims=True)

    rev = lambda i: nb - 1 - i
    return pl.pallas_call(
        body, name=name,
        grid=(nb,),
        in_specs=[
            pl.BlockSpec((tb, dk), lambda i: (rev(i), q_cb)),
            pl.BlockSpec((tb, dk), lambda i: (rev(i), k_cb)),
            pl.BlockSpec((tb, dv), lambda i: (rev(i), v_cb)),
            pl.BlockSpec((tb, dv), lambda i: (rev(i), r_cb)),
            pl.BlockSpec((tb, LANE), lambda i: (rev(i), 0)),
            pl.BlockSpec(wa2.shape, lambda i: (0, 0)),
            pl.BlockSpec(ba2.shape, lambda i: (0, 0)),
            pl.BlockSpec(ghn.shape, lambda i: (0, 0)),
            pl.BlockSpec((tb, dv), lambda i: (rev(i), 0)),
            pl.BlockSpec((nch, GLA_HEADS, hv, hk), lambda i: (rev(i), 0, 0, 0)),
            pl.BlockSpec((tb, dv), lambda i: (rev(i), 0)),
        ],
        out_specs=[
            pl.BlockSpec((tb, dk), lambda i: (rev(i), 0)),
            pl.BlockSpec((tb, dk), lambda i: (rev(i), 0)),
            pl.BlockSpec((tb, dv), lambda i: (rev(i), 0)),
            pl.BlockSpec((tb, dv), lambda i: (rev(i), 0)),
            pl.BlockSpec((tb, LANE), lambda i: (rev(i), 0)),
            pl.BlockSpec(wa2.shape, lambda i: (0, 0)),
            pl.BlockSpec(ba2.shape, lambda i: (0, 0)),
            pl.BlockSpec(ghn.shape, lambda i: (0, 0)),
        ],
        out_shape=[
            jax.ShapeDtypeStruct((t, dk), BF16),
            jax.ShapeDtypeStruct((t, dk), BF16),
            jax.ShapeDtypeStruct((t, dv), BF16),
            jax.ShapeDtypeStruct((t, dv), BF16),
            jax.ShapeDtypeStruct((t, LANE), BF16),
            jax.ShapeDtypeStruct(wa2.shape, F32),
            jax.ShapeDtypeStruct(ba2.shape, F32),
            jax.ShapeDtypeStruct(ghn.shape, F32),
        ],
        scratch_shapes=[pltpu.VMEM((GLA_HEADS, hv, hk), F32)] + [pltpu.VMEM((tb, dk), F32)] * 4,
        compiler_params=_cparams("arbitrary"),
    )(proj, proj, proj, proj, alow, wa2, ba2, ghn, opre, states, doa)


def _s5_tables(lam_re, lam_im, log_dt, b_re, b_im, c_re, c_im):
    hp = lax.Precision.HIGHEST
    g, p = lam_re.shape
    ln = S5_L
    dt = jnp.exp(log_dt)[:, None]
    lr, li = lam_re, lam_im
    mag = jnp.exp(lr * dt)
    ar, ai = mag * jnp.cos(li * dt), mag * jnp.sin(li * dt)
    den = lr * lr + li * li
    am1 = ar - 1.0
    f_re = ((am1 * lr + ai * li) / den)[..., None]
    f_im = ((ai * lr - am1 * li) / den)[..., None]
    bb_re = f_re * b_re - f_im * b_im
    bb_im = f_re * b_im + f_im * b_re
    j = jnp.arange(ln + 1, dtype=F32)[None, :, None]
    pm = jnp.exp(j * (lr * dt)[:, None, :])
    ang = j * (li * dt)[:, None, :]
    pw_re, pw_im = pm * jnp.cos(ang), pm * jnp.sin(ang)
    cp_re = c_re[:, None] * pw_re[:, :, None, :] - c_im[:, None] * pw_im[:, :, None, :]
    cp_im = c_re[:, None] * pw_im[:, :, None, :] + c_im[:, None] * pw_re[:, :, None, :]
    kj = (jnp.einsum("gjcp,gpd->gjcd", cp_re[:, :ln], bb_re, precision=hp)
          - jnp.einsum("gjcp,gpd->gjcd", cp_im[:, :ln], bb_im, precision=hp))
    s_i = jnp.arange(ln)[None, :, None]
    t_i = jnp.arange(ln)[None, None, :]
    j_i = jnp.arange(ln)[:, None, None]
    shift = (t_i - s_i == j_i).astype(F32)
    m = jnp.einsum("jst,gjcd->gsdtc", shift, kj, precision=hp).reshape(g, ln * S5_GC, ln * S5_GC)
    rp_re, rp_im = pw_re[:, ln - 1::-1], pw_im[:, ln - 1::-1]
    bbt_re, bbt_im = bb_re.transpose(0, 2, 1)[:, None], bb_im.transpose(0, 2, 1)[:, None]
    bst_re = rp_re[:, :, None, :] * bbt_re - rp_im[:, :, None, :] * bbt_im
    bst_im = rp_re[:, :, None, :] * bbt_im + rp_im[:, :, None, :] * bbt_re
    bst = jnp.concatenate([bst_re, bst_im], axis=-1).reshape(g, ln * S5_GC, 2 * p)
    cst = jnp.concatenate([cp_re[:, 1:].transpose(0, 3, 1, 2), -cp_im[:, 1:].transpose(0, 3, 1, 2)], axis=1)
    cst = cst.reshape(g, 2 * p, ln * S5_GC)
    a = jnp.stack([jnp.concatenate([pw_re[:, ln], pw_re[:, ln]], axis=-1),
                   jnp.concatenate([-pw_im[:, ln], pw_im[:, ln]], axis=-1)], axis=1)
    return m, bst, cst, a


def _state_scan(v, pr, pi, reverse):
    n = v.shape[0]
    half = v.shape[1] // 2
    row = lax.broadcasted_iota(jnp.int32, v.shape, 0)
    z, s = v, 1
    while s < n:
        if reverse:
            zs = jnp.where(row < n - s, pltpu.roll(z, n - s, 0), 0.0)
        else:
            zs = jnp.where(row >= s, pltpu.roll(z, s, 0), 0.0)
        z = z + zs * pr + pltpu.roll(zs, half, 1) * pi
        pr, pi = pr * pr - pi * pi, 2.0 * pr * pi
        s *= 2
    return z


def _s5_core_fwd(u_g, m, bst, cst, a, name):
    g, nc, w = u_g.shape
    p2 = bst.shape[2]

    def body(u_ref, m_ref, b_ref, c_ref, a_ref, y_ref, x_ref):
        u = u_ref[0]
        v = jnp.dot(u, b_ref[0], preferred_element_type=F32)
        z = _state_scan(v, a_ref[0, 0:1, :], a_ref[0, 1:2, :], reverse=False)
        row = lax.broadcasted_iota(jnp.int32, z.shape, 0)
        x = jnp.where(row >= 1, pltpu.roll(z, 1, 0), 0.0)
        x_ref[0] = x
        y_ref[0] = (jnp.dot(u, m_ref[0], preferred_element_type=F32)
                    + jnp.dot(x.astype(BF16), c_ref[0], preferred_element_type=F32))

    per_g = lambda shape: pl.BlockSpec((1,) + shape, lambda i: (i, 0, 0))
    return pl.pallas_call(
        body, name=name, grid=(g,),
        in_specs=[per_g((nc, w)), per_g((w, w)), per_g((w, p2)), per_g((p2, w)), per_g((2, p2))],
        out_specs=[per_g((nc, w)), per_g((nc, p2))],
        out_shape=[jax.ShapeDtypeStruct((g, nc, w), F32), jax.ShapeDtypeStruct((g, nc, p2), F32)],
        compiler_params=_cparams("parallel"),
    )(u_g, m, bst, cst, a)


def _s5_core_bwd(dy_g, u_g, x_g, m, bst, cst, a, name):
    g, nc, w = u_g.shape
    p2 = bst.shape[2]

    def body(dy_ref, u_ref, x_ref, m_ref, b_ref, c_ref, a_ref, du_ref, dm_ref, db_ref, dc_ref, da_ref):
        dy, u, x = dy_ref[0], u_ref[0], x_ref[0]
        gx = _dot_t(dy, c_ref[0], 1, 1)
        rtot = _state_scan(gx, a_ref[0, 0:1, :], -a_ref[0, 1:2, :], reverse=True)
        row = lax.broadcasted_iota(jnp.int32, rtot.shape, 0)
        dv = jnp.where(row < nc - 1, pltpu.roll(rtot, nc - 1, 0), 0.0)
        dv_bf = dv.astype(BF16)
        du_ref[0] = (_dot_t(dy, m_ref[0], 1, 1) + _dot_t(dv_bf, b_ref[0], 1, 1)).astype(BF16)
        dm_ref[0] = _dot_t(u, dy, 0, 0)
        dc_ref[0] = _dot_t(x.astype(BF16), dy, 0, 0)
        db_ref[0] = _dot_t(u, dv_bf, 0, 0)
        x_sw = pltpu.roll(x, p2 // 2, 1)
        da_ref[0, 0:1, :] = jnp.sum(dv * x, axis=0, keepdims=True)
        da_ref[0, 1:2, :] = jnp.sum(dv * x_sw, axis=0, keepdims=True)

    per_g = lambda shape: pl.BlockSpec((1,) + shape, lambda i: (i, 0, 0))
    return pl.pallas_call(
        body, name=name, grid=(g,),
        in_specs=[per_g((nc, w)), per_g((nc, w)), per_g((nc, p2)), per_g((w, w)), per_g((w, p2)), per_g((p2, w)), per_g((2, p2))],
        out_specs=[per_g((nc, w)), per_g((w, w)), per_g((w, p2)), per_g((p2, w)), per_g((2, p2))],
        out_shape=[jax.ShapeDtypeStruct((g, nc, w), BF16), jax.ShapeDtypeStruct((g, w, w), F32),
                   jax.ShapeDtypeStruct((g, w, p2), F32), jax.ShapeDtypeStruct((g, p2, w), F32),
                   jax.ShapeDtypeStruct((g, 2, p2), F32)],
        compiler_params=_cparams("parallel"),
    )(dy_g, u_g, x_g, m, bst, cst, a)


def _gelu_parts(y):
    inner = GELU_C * (y + GELU_A * y * y * y)
    th = jnp.tanh(inner)
    return th, 0.5 * y * (1.0 + th)


def _s5_post_fwd(y_raw, proj, u_cb, s5d, wglu, bglu, name):
    w = y_raw.shape[1]

    def fn(yr, u, dsk, wg, bg):
        y = yr + dsk * u.astype(F32)
        _, h = _gelu_parts(y)
        gl = jnp.dot(h.astype(BF16), wg, preferred_element_type=F32) + bg
        return (h * _sigmoid(gl),)

    return _rowwise(fn, [("row", y_raw), ("win", proj, w, u_cb), ("full", s5d), ("full", wglu), ("full", bglu)],
                    [(w, BF16)], [], rows=y_raw.shape[0], tb=_tile(y_raw.shape[0], 512), name=name)[0]


def _s5_post_bwd(y_raw, proj, u_cb, s5d, wglu, bglu, dob, name):
    w = y_raw.shape[1]

    def fn(yr, u, dsk, wg, bg, dov):
        u = u.astype(F32)
        dov = dov.astype(F32)
        y = yr + dsk * u
        th, h = _gelu_parts(y)
        h_bf = h.astype(BF16)
        gl = jnp.dot(h_bf, wg, preferred_element_type=F32) + bg
        sg = _sigmoid(gl)
        dgl = dov * h * sg * (1.0 - sg)
        dgl_bf = dgl.astype(BF16)
        dh = dov * sg + _dot_t(dgl_bf, wg, 1, 1)
        dgelu = 0.5 * (1.0 + th) + 0.5 * y * (1.0 - th * th) * GELU_C * (1.0 + 3.0 * GELU_A * y * y)
        dy = dh * dgelu
        return (dy, dy * dsk,
                _dot_t(h_bf, dgl_bf, 0, 0), jnp.sum(dgl, axis=0, keepdims=True), jnp.sum(dy * u, axis=0, keepdims=True))

    return _rowwise(fn, [("row", y_raw), ("win", proj, w, u_cb), ("full", s5d), ("full", wglu), ("full", bglu), ("row", dob)],
                    [(w, BF16), (w, BF16)], [(w, w), (1, w), (1, w)], rows=y_raw.shape[0], tb=_tile(y_raw.shape[0], 512), name=name)


def _to_groups(a, dtype):
    t, w = a.shape
    g = w // S5_GC
    return a.reshape(t // S5_L, S5_L, g, S5_GC).transpose(2, 0, 1, 3).reshape(g, t // S5_L, S5_L * S5_GC).astype(dtype)


def _from_groups(a):
    g, nc, _ = a.shape
    return a.reshape(g, nc, S5_L, S5_GC).transpose(1, 2, 0, 3).reshape(nc * S5_L, g * S5_GC)


def _adamw(w, g, m, v, name):
    _, rows, cols = w.shape
    tr, tc = (_tile(rows, 256, align=16), cols) if rows % 16 == 0 else (rows, _tile(cols, 256))
    slots = isinstance(g, (list, tuple))
    gs = list(g) if slots else [g]
    c1 = 1.0 - ADAM_B1 ** ADAM_STEP
    c2 = 1.0 - ADAM_B2 ** ADAM_STEP

    def body(w_ref, m_ref, v_ref, *refs):
        g_refs, out_refs = refs[:len(gs)], refs[len(gs):]
        if slots:
            parts = [g_ref[s].astype(F32) for g_ref in g_refs for s in range(g_ref.shape[0])]
            gv = parts[0]
            for p in parts[1:]:
                gv = gv + p
            out_refs[0][...] = gv
        else:
            gv = g_refs[0][...]
        d_ref, nm_ref, nv_ref = out_refs[-3:]
        nm = ADAM_B1 * m_ref[...] + (1.0 - ADAM_B1) * gv
        nv = ADAM_B2 * v_ref[...] + (1.0 - ADAM_B2) * (gv * gv)
        d_ref[...] = -ADAM_LR * ((nm / c1) / (jnp.sqrt(nv / c2) + ADAM_EPS) + ADAM_WD * w_ref[...])
        nm_ref[...] = nm
        nv_ref[...] = nv

    spec = pl.BlockSpec((None, tr, tc), lambda i, j: (0, i, j))
    g_specs = [pl.BlockSpec((a.shape[0], tr, tc), lambda i, j: (0, i, j)) for a in gs] if slots else [pl.BlockSpec((tr, tc), lambda i, j: (i, j))]
    n_out = 4 if slots else 3
    return pl.pallas_call(
        body, name=name, grid=(rows // tr, cols // tc),
        in_specs=[spec, spec, spec] + g_specs, out_specs=[spec] * n_out,
        out_shape=[jax.ShapeDtypeStruct((1, rows, cols), F32)] * n_out,
        compiler_params=_cparams("parallel", "parallel"),
    )(w, m, v, *gs)


def _slot_sum(x, name):
    _, rows, cols = x.shape
    if rows % 8 == 0:
        tr, tc = _tile(rows, 512, align=8), cols
    else:
        tr, tc = rows, _tile(cols, 256)

    def body(x_ref, o_ref):
        acc = x_ref[0].astype(F32)
        for s in range(1, N_DEV):
            acc = acc + x_ref[s].astype(F32)
        o_ref[...] = acc

    return pl.pallas_call(
        body, name=name, grid=(rows // tr, cols // tc),
        in_specs=[pl.BlockSpec((N_DEV, tr, tc), lambda i, j: (0, i, j))],
        out_specs=pl.BlockSpec((tr, tc), lambda i, j: (i, j)),
        out_shape=jax.ShapeDtypeStruct((rows, cols), F32),
        compiler_params=_cparams("parallel", "parallel"),
    )(x)


_REST = (("w_a2", 1), ("w_glu", 0), ("w_branch_a", 1), ("w_branch_b", 1), ("w_out", 0), ("w_ffn_in", 1), ("w_ffn_out", 0))
_SMALL = ("norm1_g", "b_a2", "gla_norm_g", "lam_re", "lam_im", "log_dt", "s5_b_re", "s5_b_im", "s5_c_re", "s5_c_im",
          "s5_d", "b_glu", "norm2_g", "final_norm_g")
_ORDER = ("norm1_g", "w_in", "w_a2", "b_a2", "gla_norm_g", "lam_re", "lam_im", "log_dt", "s5_b_re", "s5_b_im", "s5_c_re",
          "s5_c_im", "s5_d", "w_glu", "b_glu", "w_branch_a", "w_branch_b", "w_out", "norm2_g", "w_ffn_in", "w_ffn_out", "final_norm_g")


def _join_slots(slots, axis):
    _, r, c = slots.shape
    if axis == 0:
        return slots.reshape(N_DEV * r, c)
    return slots.transpose(1, 0, 2).reshape(r, N_DEV * c)


def _to_slots(full, axis):
    r, c = full.shape
    if axis == 0:
        return full.reshape(N_DEV, r // N_DEV, c)
    return full.reshape(r, N_DEV, c // N_DEV).transpose(1, 0, 2)


def _local_step(x, target, w_in_t, small, rest):
    t, d = x.shape
    dk, dv, s5w = d // 4, d // 2, d // 4
    dist = not isinstance(rest, dict)
    if dist:
        h1, (w_in_slots,) = _rms_fwd(x, small["norm1_g"], "norm1_fwd", carry=("ag", [w_in_t]))
        w_in_t = w_in_slots.reshape(-1, d)
    else:
        h1 = _rms_fwd(x, small["norm1_g"], "norm1_fwd")
    o_q, o_k, o_v, o_r, o_al = 0, dk, 2 * dk, 2 * dk + dv, 2 * dk + 2 * dv
    o_u = o_al + GLA_RANK
    o_ga, o_gb = o_u + s5w, o_u + s5w + d
    rows = lambda a, o, n: a[o:o + n]
    w_main_t = jnp.concatenate([rows(w_in_t, o_ga, d), rows(w_in_t, o_gb, d), rows(w_in_t, o_v, dv), rows(w_in_t, o_r, dv),
                                rows(w_in_t, o_q, dk), rows(w_in_t, o_k, dk), rows(w_in_t, o_u, s5w)], axis=0)
    w_al_t = jnp.pad(rows(w_in_t, o_al, GLA_RANK), ((0, LANE - GLA_RANK), (0, 0)))
    u_cb = (2 * d + 2 * dv + 2 * dk) // s5w

    if dist:
        proj, gathered = _mm(h1, w_main_t, tb=True, out_dtype=BF16, carry=("ag", rest[:-1]), name="in_proj")
        w = {n: _join_slots(g, ax) for (n, ax), g in zip(_REST[:-2], gathered[:-1])}
        w_ffn_in_s = gathered[-1]
    else:
        proj = _mm(h1, w_main_t, tb=True, out_dtype=BF16, name="in_proj")
        w = rest
        w_ffn_in_s = _to_slots(rest["w_ffn_in"], 1)
    wa2 = jnp.pad(w["w_a2"], ((0, LANE - GLA_RANK), (0, 0)))
    alow = _mm(h1, w_al_t, tb=True, out_dtype=BF16, name="in_proj_gate_rank")
    o_a, o_pre, states = _gla_fwd(proj, alow, wa2, small["b_a2"], small["gla_norm_g"], dk=dk, dv=dv, name="gla_fwd")

    s5_params = (small["lam_re"], small["lam_im"], small["log_dt"][0], small["s5_b_re"], small["s5_b_im"],
                 small["s5_c_re"], small["s5_c_im"])
    (tm, tbst, tcst, ta), tables_vjp = jax.vjp(_s5_tables, *s5_params)
    tm_bf, tbst_bf, tcst_bf = tm.astype(BF16), tbst.astype(BF16), tcst.astype(BF16)
    u_g = _to_groups(proj[:, u_cb * s5w:(u_cb + 1) * s5w], BF16)
    y_g, x_g = _s5_core_fwd(u_g, tm_bf, tbst_bf, tcst_bf, ta, "s5_core_fwd")
    y_raw = _from_groups(y_g)
    o_b = _s5_post_fwd(y_raw, proj, u_cb, small["s5_d"], w["w_glu"], small["b_glu"], "s5_post_fwd")

    pa = _mm(o_a, w["w_branch_a"], out_dtype=BF16, name="branch_a")
    pb = _mm(o_b, w["w_branch_b"], out_dtype=BF16, name="branch_b")
    mix = _mix_fwd(proj, pa, pb, d, "mix_fwd")
    x1 = _mm(mix, w["w_out"], res=x, name="out_proj")
    h2 = _rms_fwd(x1, small["norm2_g"], "norm2_fwd")
    if dist:
        gu, act, (w_ffn_out_s,) = _ffn_in_fused(h2, w_ffn_in_s, carry=("ag", rest[-1:]), name="ffn_in")
        w_ffn_out = _join_slots(w_ffn_out_s, 0)
    else:
        gu, act = _ffn_in_fused(h2, w_ffn_in_s, name="ffn_in")
        w_ffn_out = rest["w_ffn_out"]
    x2 = _mm(act, w_ffn_out, res=x1, name="ffn_out")
    dx2, dx2_bf, d_final_g, loss = _loss_head(x2, small["final_norm_g"], target, "loss_head")

    recv = {}
    dgu, = _mm(dx2_bf, w_ffn_out, tb=True, epi=(_swiglu_bwd_tile, [gu], [(2, BF16)]), name="d_act")
    g_ffn_out = _mm(act, dx2_bf, ta=True, out_dtype=BF16, name="g_w_ffn_out")
    if dist:
        g_ffn_in_s, recv["w_ffn_out"] = _mm(h2, dgu, ta=True, b_slots=True, out_dtype=BF16, out_slots=N_DEV,
                                            carry=("a2a", [_to_slots(g_ffn_out, 0)]), name="g_w_ffn_in")
        dh2, (recv_ffn_in,) = _mm(dgu, w_ffn_in_s, tb=True, a_slots=True, b_slots=True,
                                  carry=("a2a", [g_ffn_in_s], [_ALL_K[:-1]]), name="d_h2")
    else:
        g_ffn_in_s = _mm(h2, dgu, ta=True, b_slots=True, out_dtype=BF16, out_slots=N_DEV, name="g_w_ffn_in")
        dh2 = _mm(dgu, w_ffn_in_s, tb=True, a_slots=True, b_slots=True, name="d_h2")
    dx1, dx1_bf, d_norm2_g = _rms_bwd(x1, small["norm2_g"], dh2, dx2, "norm2_bwd", True)
    dmix = _mm(dx1_bf, w["w_out"], tb=True, out_dtype=BF16, name="d_mix")
    g_out = _mm(mix, dx1_bf, ta=True, out_dtype=BF16, name="g_w_out")
    dpa, dpb, dga, dgb = _mix_bwd(proj, pa, pb, dmix, d, "mix_bwd")
    doa = _mm(dpa, w["w_branch_a"], tb=True, out_dtype=BF16, name="d_o_a")
    dob = _mm(dpb, w["w_branch_b"], tb=True, out_dtype=BF16, name="d_o_b")
    g_branch_a = _mm(o_a, dpa, ta=True, out_dtype=BF16, name="g_w_branch_a")
    g_branch_b = _mm(o_b, dpb, ta=True, out_dtype=BF16, name="g_w_branch_b")

    dy_s5, du_direct, g_glu, g_bglu, g_s5d = _s5_post_bwd(y_raw, proj, u_cb, small["s5_d"], w["w_glu"], small["b_glu"], dob, "s5_post_bwd")
    du_g, d_tm, d_tbst, d_tcst, d_ta = _s5_core_bwd(_to_groups(dy_s5, BF16), u_g, x_g, tm_bf, tbst_bf, tcst_bf, ta, "s5_core_bwd")
    g_lam_re, g_lam_im, g_log_dt, g_b_re, g_b_im, g_c_re, g_c_im = tables_vjp((d_tm, d_tbst, d_tcst, d_ta))
    du = _from_groups(du_g) + du_direct

    dq, dkk, dvv, dr, dal, g_wa2, g_ba2, g_ghn = _gla_bwd(proj, alow, wa2, small["b_a2"], small["gla_norm_g"], o_pre, states, doa,
                                                        dk=dk, dv=dv, name="gla_bwd")
    dproj = jnp.concatenate([dga, dgb, dvv, dr, dq, dkk, du], axis=1)
    mid = {"w_out": g_out, "w_branch_a": g_branch_a, "w_branch_b": g_branch_b, "w_glu": g_glu.astype(BF16),
           "w_a2": g_wa2[:GLA_RANK].astype(BF16)}
    if dist:
        axes = dict(_REST)
        g_main_t, got = _mm(dproj, h1, ta=True, out_dtype=BF16, name="g_w_in_main",
                            carry=("a2a", [_to_slots(mid[n], axes[n]) for n in mid] + [g_ffn_in_s], [_ALL_K] * len(mid) + [_ALL_K[-1:]]))
        recv.update(zip(mid, [[g] for g in got[:-1]]))
        recv["w_ffn_in"] = [recv_ffn_in, got[-1]]
    else:
        g_main_t = _mm(dproj, h1, ta=True, out_dtype=BF16, name="g_w_in_main")
    g_al_t = _mm(dal, h1, ta=True, out_dtype=BF16, name="g_w_in_gate_rank")
    mrows = lambda o, n: g_main_t[o:o + n]
    g_w_in_t = jnp.concatenate([mrows(2 * d + 2 * dv, dk), mrows(2 * d + 2 * dv + dk, dk), mrows(2 * d, dv), mrows(2 * d + dv, dv),
                                g_al_t[:GLA_RANK], mrows(2 * d + 2 * dv + 2 * dk, s5w), mrows(0, d), mrows(d, d)], axis=0)
    if dist:
        dh1, recv["w_in"] = _mm(dproj, w_main_t, carry=("a2a", [_to_slots(g_w_in_t, 0)]), name="d_h1_main")
    else:
        dh1 = _mm(dproj, w_main_t, name="d_h1_main")
    dh1 = _mm(dal, w_al_t, res=dh1, name="d_h1_gate_rank")
    grad_x, d_norm1_g = _rms_bwd(x, small["norm1_g"], dh1, dx1, "norm1_bwd", False)

    small_g = {
        "norm1_g": d_norm1_g, "b_a2": g_ba2, "gla_norm_g": g_ghn, "lam_re": g_lam_re, "lam_im": g_lam_im,
        "log_dt": g_log_dt[None], "s5_b_re": g_b_re, "s5_b_im": g_b_im, "s5_c_re": g_c_re, "s5_c_im": g_c_im,
        "s5_d": g_s5d, "b_glu": g_bglu, "norm2_g": d_norm2_g, "final_norm_g": d_final_g,
    }
    if not dist:
        recv = dict(mid, w_in=g_w_in_t, w_ffn_in=_join_slots(g_ffn_in_s, 1), w_ffn_out=g_ffn_out)
    return loss[0, 0], grad_x, recv, small_g


def _small_2d(name, a):
    a = a[0]
    return a[None] if a.ndim == 1 else a


def kernel(x, norm1_g, w_in, w_a2, b_a2, gla_norm_g, lam_re, lam_im, log_dt, s5_b_re, s5_b_im, s5_c_re, s5_c_im, s5_d, w_glu, b_glu, w_branch_a, w_branch_b, w_out, norm2_g, w_ffn_in, w_ffn_out, final_norm_g, loss_target, m_norm1_g, m_w_in, m_w_a2, m_b_a2, m_gla_norm_g, m_lam_re, m_lam_im, m_log_dt, m_s5_b_re, m_s5_b_im, m_s5_c_re, m_s5_c_im, m_s5_d, m_w_glu, m_b_glu, m_w_branch_a, m_w_branch_b, m_w_out, m_norm2_g, m_w_ffn_in, m_w_ffn_out, m_final_norm_g, v_norm1_g, v_w_in, v_w_a2, v_b_a2, v_gla_norm_g, v_lam_re, v_lam_im, v_log_dt, v_s5_b_re, v_s5_b_im, v_s5_c_re, v_s5_c_im, v_s5_d, v_w_glu, v_b_glu, v_w_branch_a, v_w_branch_b, v_w_out, v_norm2_g, v_w_ffn_in, v_w_ffn_out, v_final_norm_g):
    args = dict(locals())
    weights = {n: args[n] for n in _ORDER}
    m_in = {n: args["m_" + n] for n in _ORDER}
    v_in = {n: args["v_" + n] for n in _ORDER}
    transposed = lambda a: a[0].T[None]
    rest = [weights[n][0].astype(BF16) for n, _ in _REST]
    small = {n: _small_2d(n, weights[n]) for n in _SMALL}
    loss_local, grad_x, recv, small_g = _local_step(x[0], loss_target[0], transposed(weights["w_in"])[0].astype(BF16), small, rest)

    grads, delta, new_m, new_v = {}, {}, {}, {}
    for n, _ in _REST:
        grads[n], delta[n], new_m[n], new_v[n] = _adamw(weights[n], recv[n], m_in[n], v_in[n], "adamw_" + n)
    w_in_out = _adamw(transposed(weights["w_in"]), recv["w_in"], transposed(m_in["w_in"]), transposed(v_in["w_in"]), "adamw_w_in")
    grads["w_in"], delta["w_in"], new_m["w_in"], new_v["w_in"] = (transposed(a) for a in w_in_out)

    s_sizes = [small_g[n].size for n in _SMALL]
    s_offs = [sum(s_sizes[:i]) for i in range(len(s_sizes))]
    s_total = sum(s_sizes)
    s_rows = -(-(-(-(s_total + 1) // LANE)) // LANE) * LANE

    def pack_small(parts):
        flat = jnp.concatenate([p.reshape(-1) for p in parts])
        return jnp.pad(flat, (0, s_rows * LANE - flat.size)).reshape(s_rows, LANE)

    s_flat = pack_small([small_g[n] for n in _SMALL] + [loss_local])
    s_red = _slot_sum(_exchange("ag", [s_flat], "small_grads_all_gather")[0], "small_grads_slot_sum")
    loss = s_red.reshape(-1)[s_total]
    sd, sm, sv = _adamw(pack_small([weights[n] for n in _SMALL])[None], s_red, pack_small([m_in[n] for n in _SMALL])[None],
                        pack_small([v_in[n] for n in _SMALL])[None], "adamw_small")
    sd, sm, sv = sd[0], sm[0], sv[0]
    for n, o, s in zip(_SMALL, s_offs, s_sizes):
        shape = weights[n].shape[1:]
        grads[n], delta[n], new_m[n], new_v[n] = (a.reshape(-1)[o:o + s].reshape(shape) for a in (s_red, sd, sm, sv))

    out = [loss, grad_x[None]]
    for tree in (grads, delta, new_m, new_v):
        out += [tree[n].reshape(weights[n].shape) for n in _ORDER]
    return tuple(out)
```

```python
import functools
import math

import jax
import jax.numpy as jnp
from jax import lax
from jax.experimental import pallas as pl
from jax.experimental.pallas import tpu as pltpu

F32 = jnp.float32
BF16 = jnp.bfloat16

NORM_EPS = 1e-6
N_DEV = 8
N_PEER = N_DEV - 1
GLA_HEADS = 4
GLA_CHUNK = 32
GLA_CHUNK_SHIFT = 5
GLA_TAU = 16.0
GLA_RANK = 16
GLA_BLOCK = 256
S5_GC = 16
S5_P = 64
S5_L = 16
LANE = 128
V7X_VMEM_LIMIT = 56 * 1024 * 1024
V7X_MM_VMEM_BUDGET = 40 * 1024 * 1024
V7X_MM_TILE_MN = 1408
V7X_MM_TILE_MN_WHOLE_K = 512
V7X_MM_TILE_K = 2048
V7X_EPI_ROW_CHUNKS = 4

ADAM_LR = 0.001
ADAM_B1 = 0.9
ADAM_B2 = 0.999
ADAM_EPS = 1e-08
ADAM_WD = 0.01
ADAM_STEP = 10

GELU_C = math.sqrt(2.0 / math.pi)
GELU_A = 0.044715

MESH = pl.DeviceIdType.MESH


def _cparams(*sem):
    return pltpu.CompilerParams(dimension_semantics=sem, vmem_limit_bytes=V7X_VMEM_LIMIT)


def _divisors_down(n, start, align=LANE):
    t = (min(start, n) // align) * align
    found = False
    while t >= align:
        if n % t == 0:
            found = True
            yield t
        t -= align
    if not found:
        yield n


def _tile(n, target, align=LANE):
    return next(_divisors_down(n, target, align))


def _sigmoid(x):
    return 1.0 / (1.0 + jnp.exp(-x))


_HBM_SPEC = pl.BlockSpec(memory_space=pltpu.HBM)


def _exchange_scratch(n):
    return [pltpu.SemaphoreType.DMA((n * N_PEER,)), pltpu.SemaphoreType.DMA((n * N_PEER,)), pltpu.SemaphoreType.DMA((n,))]


def _ag_phases(x_refs, out_refs, send_sems, recv_sems, local_sems):
    n = len(x_refs)
    x, y, c = lax.axis_index("x"), lax.axis_index("y"), lax.axis_index("c")
    me, sibling = (x, y, c), (x, y, 1 - c)
    chips = [(1 - x, y), (x, 1 - y), (1 - x, 1 - y)]

    def copy(a, k, block, to, from_input=False):
        dst = out_refs[a].at[4 * block[0] + 2 * block[1] + block[2]]
        return pltpu.make_async_remote_copy(
            src_ref=x_refs[a] if from_input else dst, dst_ref=dst,
            send_sem=send_sems.at[a * N_PEER + k], recv_sem=recv_sems.at[a * N_PEER + k], device_id=to, device_id_type=MESH)

    def local(a):
        return pltpu.make_async_copy(x_refs[a], out_refs[a].at[4 * x + 2 * y + c], local_sems.at[a])

    def first(a):
        return [copy(a, 0, me, sibling, True)] + [copy(a, 1 + j, me, (*chip, c), True) for j, chip in enumerate(chips)]

    def start():
        for a in range(n):
            local(a).start()
            for cp in first(a):
                cp.start()

    def relay():
        for j, chip in enumerate(chips):
            for a in range(n):
                copy(a, 1 + j, (*chip, c), me).wait_recv()
                copy(a, 4 + j, (*chip, c), sibling).start()

    def finish():
        for a in range(n):
            copy(a, 0, sibling, me).wait_recv()
            for j, chip in enumerate(chips):
                copy(a, 4 + j, (*chip, 1 - c), me).wait_recv()
        for a in range(n):
            for cp in first(a) + [copy(a, 4 + j, (*chip, c), sibling) for j, chip in enumerate(chips)]:
                cp.wait_send()
            local(a).wait()

    return start, relay, finish


_ALL_K = tuple(range(N_DEV))


def _a2a_phases(x_refs, out_refs, send_sems, recv_sems, local_sems, ks_list=None):
    n = len(x_refs)
    ks_list = ks_list or [_ALL_K] * n
    x, y, c = lax.axis_index("x"), lax.axis_index("y"), lax.axis_index("c")
    my = 4 * x + 2 * y + c

    def copy(a, k):
        px, py, pc = (1 - x if k & 4 else x), (1 - y if k & 2 else y), (1 - c if k & 1 else c)
        return pltpu.make_async_remote_copy(
            src_ref=x_refs[a].at[4 * px + 2 * py + pc], dst_ref=out_refs[a].at[ks_list[a].index(k)],
            send_sem=send_sems.at[a * N_PEER + k - 1], recv_sem=recv_sems.at[a * N_PEER + k - 1],
            device_id=(px, py, pc), device_id_type=MESH)

    def local(a):
        return pltpu.make_async_copy(x_refs[a].at[my], out_refs[a].at[ks_list[a].index(0)], local_sems.at[a])

    def start():
        for a in range(n):
            for k in ks_list[a]:
                (copy(a, k) if k else local(a)).start()

    def relay():
        pass

    def finish():
        for a in range(n):
            for k in ks_list[a]:
                if k:
                    copy(a, k).wait_recv()
        for a in range(n):
            for k in ks_list[a]:
                if k:
                    copy(a, k).wait_send()
                else:
                    local(a).wait()

    return start, relay, finish


def _exchange_out_shapes(kind, arrays, ks_list=None):
    if kind == "ag":
        return [jax.ShapeDtypeStruct((N_DEV,) + a.shape, a.dtype) for a in arrays]
    ks_list = ks_list or [_ALL_K] * len(arrays)
    return [jax.ShapeDtypeStruct((len(ks),) + a.shape[1:], a.dtype) for a, ks in zip(arrays, ks_list)]


def _exchange(kind, arrays, name):
    n = len(arrays)
    phases = _ag_phases if kind == "ag" else _a2a_phases

    def body(*refs):
        start, relay, finish = phases(refs[:n], refs[n:2 * n], *refs[2 * n:])
        start()
        relay()
        finish()

    return pl.pallas_call(
        body, name=name,
        out_shape=_exchange_out_shapes(kind, arrays),
        in_specs=[_HBM_SPEC] * n, out_specs=[_HBM_SPEC] * n,
        scratch_shapes=_exchange_scratch(n),
    )(*arrays)


def _mm_tiles(m, n_unit, k_unit, tile_bytes, small_tiles_ok=True):
    fits = lambda tm, tn, tk: 2 * 2 * (tm * tk + tk * tn) + tile_bytes * tm * tn <= V7X_MM_VMEM_BUDGET
    for cap in (V7X_MM_TILE_MN, V7X_MM_TILE_MN_WHOLE_K) if small_tiles_ok else (V7X_MM_TILE_MN,):
        tm, tn = _tile(m, cap), _tile(n_unit, cap)
        if fits(tm, tn, k_unit) and (tn >= V7X_MM_TILE_MN_WHOLE_K or tn == n_unit):
            return tm, tn, k_unit
    tm, tn = _tile(m, V7X_MM_TILE_MN), _tile(n_unit, V7X_MM_TILE_MN)
    for tk in _divisors_down(k_unit, V7X_MM_TILE_K):
        if fits(tm, tn, tk):
            return tm, tn, tk
    return tm, tn, _tile(k_unit, LANE)


def _carry_parts(carry):
    kind, arrays, ks_list = (tuple(carry) + (None,))[:3] if carry is not None else (None, [], None)
    n = len(arrays)
    kind = (kind, ks_list)
    return kind, arrays, [_HBM_SPEC] * n, _exchange_out_shapes(kind[0], arrays, ks_list), (_exchange_scratch(n) if n else [])


def _carry_hooks(kind, x_refs, out_refs, sems, step, last_step):
    if not x_refs:
        return lambda: None
    kind, ks_list = kind
    if kind == "ag":
        start, relay, finish = _ag_phases(x_refs, out_refs, *sems)
    else:
        start, relay, finish = _a2a_phases(x_refs, out_refs, *sems, ks_list=ks_list)
    pl.when(step == 0)(start)

    def after():
        if kind == "ag":
            pl.when(step == (last_step * 7) // 8)(relay)
        pl.when(step == last_step)(finish)

    return after


def _mm(a, b, *, ta=False, tb=False, out_dtype=F32, res=None, carry=None, a_slots=False, b_slots=False, out_slots=0,
        epi=None, name):
    if a_slots:
        assert not ta
        a_n, m, a_c = a.shape
        k = a_n * a_c
    else:
        m, k = (a.shape[1], a.shape[0]) if ta else a.shape
    if b_slots:
        b_n, b_r, b_c = b.shape
        k2, n = (b_n * b_c, b_r) if tb else (b_r, b_n * b_c)
    else:
        k2, n = (b.shape[1], b.shape[0]) if tb else b.shape
    assert k == k2, (a.shape, b.shape, ta, tb)
    has_res = res is not None
    assert not (has_res and (out_slots or epi))
    n_units = [n] + ([n // out_slots] if out_slots else []) + ([b_c] if b_slots and not tb else [])
    k_units = [k] + ([a_c] if a_slots else []) + ([b_c] if b_slots and tb else [])
    n_unit, k_unit = min(n_units), min(k_units)
    assert all(u % n_unit == 0 for u in n_units) and all(u % k_unit == 0 for u in k_units)
    epi_fn, epi_ins, epi_outs = epi if epi is not None else (None, [], [])
    tile_bytes = 4 + (2 * res.dtype.itemsize if has_res else 0)
    tile_bytes += sum(2 * e.shape[0] * e.dtype.itemsize for e in epi_ins)
    tile_bytes += sum(2 * l * jnp.dtype(dt).itemsize for l, dt in epi_outs) if epi else 2 * jnp.dtype(out_dtype).itemsize
    tm, tn, tk = _mm_tiles(m, n_unit, k_unit, tile_bytes, small_tiles_ok=not epi)
    ni, nj, nk = m // tm, n // tn, k // tk
    dims = (((0,) if ta else (1,), (1,) if tb else (0,)), ((), ()))

    def slot_map(per, pos):
        if pos == "k_cols":
            return lambda i, j, kk: (kk // per, i, kk % per)
        if pos == "k_cols_j":
            return lambda i, j, kk: (kk // per, j, kk % per)
        if pos == "n_cols_k":
            return lambda i, j, kk: (j // per, kk, j % per)
        return lambda i, j, kk: (j // per, i, j % per)

    if a_slots:
        a_spec = pl.BlockSpec((None, tm, tk), slot_map(a_c // tk, "k_cols"))
    else:
        a_spec = pl.BlockSpec((tk, tm), lambda i, j, kk: (kk, i)) if ta else pl.BlockSpec((tm, tk), lambda i, j, kk: (i, kk))
    if b_slots and tb:
        b_spec = pl.BlockSpec((None, tn, tk), slot_map(b_c // tk, "k_cols_j"))
    elif b_slots:
        b_spec = pl.BlockSpec((None, tk, tn), slot_map(b_c // tn, "n_cols_k"))
    else:
        b_spec = pl.BlockSpec((tn, tk), lambda i, j, kk: (j, kk)) if tb else pl.BlockSpec((tk, tn), lambda i, j, kk: (kk, j))
    if epi:
        lead_spec = lambda l: pl.BlockSpec((l, tm, tn), lambda i, j, kk: (0, i, j))
        o_specs = [lead_spec(l) for l, _ in epi_outs]
        o_shapes = [jax.ShapeDtypeStruct((l, m, n), dt) for l, dt in epi_outs]
    elif out_slots:
        o_specs = [pl.BlockSpec((None, tm, tn), slot_map((n // out_slots) // tn, "n_cols_i"))]
        o_shapes = [jax.ShapeDtypeStruct((out_slots, m, n // out_slots), out_dtype)]
    else:
        o_specs = [pl.BlockSpec((tm, tn), lambda i, j, kk: (i, j))]
        o_shapes = [jax.ShapeDtypeStruct((m, n), out_dtype)]
    extra_ins = ([res] if has_res else []) + list(epi_ins)
    extra_specs = ([o_specs[0]] if has_res else []) + [pl.BlockSpec((e.shape[0], tm, tn), lambda i, j, kk: (0, i, j)) for e in epi_ins]
    n_in, n_out = 2 + len(extra_ins), len(o_specs)
    c_kind, c_arrays, c_specs, c_shapes, c_scratch = _carry_parts(carry)
    nc = len(c_arrays)
    last_step = ni * nj * nk - 1

    def body(*refs):
        a_ref, b_ref = refs[0], refs[1]
        e_refs = refs[2:n_in]
        x_refs = refs[n_in:n_in + nc]
        o_refs = refs[n_in + nc:n_in + nc + n_out]
        out_refs = refs[n_in + nc + n_out:n_in + 2 * nc + n_out]
        scratch = refs[n_in + 2 * nc + n_out:]
        acc = scratch[0] if nk > 1 else None
        kk = pl.program_id(2)
        step = (pl.program_id(0) * nj + pl.program_id(1)) * nk + kk
        after = _carry_hooks(c_kind, x_refs, out_refs, scratch[-3:], step, last_step)

        def emit(val):
            if has_res:
                val = val + e_refs[0][...].astype(F32)
            if epi:
                for o_ref, parts in zip(o_refs, epi_fn(val, *[e[...] for e in e_refs])):
                    for l, v in enumerate(parts):
                        o_ref[l] = v.astype(o_ref.dtype)
            else:
                o_refs[0][...] = val.astype(out_dtype)

        if epi and nk == 1 and not ta:
            rc = tm // V7X_EPI_ROW_CHUNKS
            for r in range(V7X_EPI_ROW_CHUNKS):
                rows = slice(r * rc, (r + 1) * rc)
                val = lax.dot_general(a_ref[rows, :], b_ref[...], dims, preferred_element_type=F32)
                for o_ref, parts in zip(o_refs, epi_fn(val, *[e[:, rows, :] for e in e_refs])):
                    for l, v in enumerate(parts):
                        o_ref[l, rows, :] = v.astype(o_ref.dtype)
            after()
            return
        part = lax.dot_general(a_ref[...], b_ref[...], dims, preferred_element_type=F32)
        if nk == 1:
            emit(part)
        else:
            @pl.when(kk == 0)
            def _():
                acc[...] = part

            @pl.when(kk > 0)
            def _():
                acc[...] += part

            @pl.when(kk == nk - 1)
            def _():
                emit(acc[...])

        after()

    sem = ("arbitrary",) * 3 if nc else ("parallel", "parallel", "arbitrary")
    outs = pl.pallas_call(
        body, name=name,
        grid=(ni, nj, nk),
        in_specs=[a_spec, b_spec] + extra_specs + c_specs,
        out_specs=o_specs + c_specs,
        out_shape=o_shapes + c_shapes,
        scratch_shapes=([pltpu.VMEM((tm, tn), F32)] if nk > 1 else []) + c_scratch,
        compiler_params=_cparams(*sem),
    )(a, b, *extra_ins, *c_arrays)
    main = list(outs[:n_out]) if epi else outs[0]
    return (main, list(outs[n_out:])) if nc else main


def _ffn_in_fused(h2, w_s, *, carry=None, name):
    t, d = h2.shape
    n_slot, _, c = w_s.shape
    half = n_slot // 2
    tm = _tile(t, 512)
    c_kind, c_arrays, c_specs, c_shapes, c_scratch = _carry_parts(carry)
    nc = len(c_arrays)
    last_step = (t // tm) * half - 1

    def body(h_ref, wg_ref, wu_ref, *refs):
        x_refs, (gu_ref, act_ref), out_refs, sems = refs[:nc], refs[nc:nc + 2], refs[nc + 2:2 * nc + 2], refs[2 * nc + 2:]
        step = pl.program_id(0) * half + pl.program_id(1)
        after = _carry_hooks(c_kind, x_refs, out_refs, sems, step, last_step)
        h = h_ref[...]
        g = jnp.dot(h, wg_ref[...], preferred_element_type=F32)
        u = jnp.dot(h, wu_ref[...], preferred_element_type=F32)
        sg = _sigmoid(g)
        silu = g * sg
        gu_ref[0] = (u * (sg + silu - silu * sg)).astype(BF16)
        gu_ref[1] = silu.astype(BF16)
        act_ref[...] = (silu * u).astype(BF16)
        after()

    outs = pl.pallas_call(
        body, name=name,
        grid=(t // tm, half),
        in_specs=[pl.BlockSpec((tm, d), lambda i, j: (i, 0)),
                  pl.BlockSpec((None, d, c), lambda i, j: (j, 0, 0)),
                  pl.BlockSpec((None, d, c), lambda i, j: (half + j, 0, 0))] + c_specs,
        out_specs=[pl.BlockSpec((2, tm, c), lambda i, j: (0, i, j)), pl.BlockSpec((tm, c), lambda i, j: (i, j))] + c_specs,
        out_shape=[jax.ShapeDtypeStruct((2, t, half * c), BF16), jax.ShapeDtypeStruct((t, half * c), BF16)] + c_shapes,
        scratch_shapes=c_scratch,
        compiler_params=_cparams(*(("arbitrary",) * 2 if nc else ("parallel", "parallel"))),
    )(h2, w_s, w_s, *c_arrays)
    return (outs[0], outs[1], list(outs[2:])) if nc else (outs[0], outs[1])


def _rowwise(fn, ins, row_outs, acc_outs, *, rows, tb, name, carry=None):
    in_specs, args = [], []
    for spec in ins:
        kind, arr = spec[0], spec[1]
        if kind == "row":
            in_specs.append(pl.BlockSpec((tb, arr.shape[1]), lambda i: (i, 0)))
        elif kind == "win":
            width, cb = spec[2], spec[3]
            in_specs.append(pl.BlockSpec((tb, width), functools.partial(lambda i, cb: (i, cb), cb=cb)))
        else:
            in_specs.append(pl.BlockSpec(arr.shape, lambda i: (0, 0)))
        args.append(arr)
    out_specs = [pl.BlockSpec((tb, c), lambda i: (i, 0)) for c, _ in row_outs]
    out_specs += [pl.BlockSpec(shape, lambda i: (0, 0)) for shape in acc_outs]
    out_shape = [jax.ShapeDtypeStruct((rows, c), dt) for c, dt in row_outs]
    out_shape += [jax.ShapeDtypeStruct(shape, F32) for shape in acc_outs]
    n_in, n_row, n_out = len(ins), len(row_outs), len(row_outs) + len(acc_outs)
    c_kind, c_arrays, c_specs, c_shapes, c_scratch = _carry_parts(carry)
    nc = len(c_arrays)

    def body(*refs):
        after = _carry_hooks(c_kind, refs[n_in:n_in + nc], refs[n_in + nc + n_out:n_in + 2 * nc + n_out],
                             refs[n_in + 2 * nc + n_out:], pl.program_id(0), rows // tb - 1)
        vals = [r[...] for r in refs[:n_in]]
        outs = fn(*vals)
        if not isinstance(outs, (tuple, list)):
            outs = (outs,)
        out_refs = refs[n_in + nc:n_in + nc + n_out]
        for o_ref, val in zip(out_refs[:n_row], outs[:n_row]):
            o_ref[...] = val.astype(o_ref.dtype)
        first = pl.program_id(0) == 0
        for o_ref, val in zip(out_refs[n_row:], outs[n_row:]):
            @pl.when(first)
            def _(o_ref=o_ref):
                o_ref[...] = jnp.zeros_like(o_ref)
            o_ref[...] += val
        after()

    res = pl.pallas_call(
        body, name=name,
        grid=(rows // tb,),
        in_specs=in_specs + c_specs, out_specs=out_specs + c_specs, out_shape=out_shape + c_shapes,
        scratch_shapes=c_scratch,
        compiler_params=_cparams("arbitrary"),
    )(*args, *c_arrays)
    return (list(res[:n_out]), list(res[n_out:])) if nc else res


def _rms_fwd(x, g, name, carry=None):
    def fn(xv, gv):
        r = lax.rsqrt(jnp.mean(xv * xv, axis=-1, keepdims=True) + NORM_EPS)
        return (xv * r * gv,)
    res = _rowwise(fn, [("row", x), ("full", g)], [(x.shape[1], BF16)], [], rows=x.shape[0], tb=_tile(x.shape[0], 512),
                   name=name, carry=carry)
    return (res[0][0], res[1]) if carry is not None else res[0]


def _rms_bwd(x, g, dh, dres, name, want_bf16):
    d = x.shape[1]

    def fn(xv, gv, dhv, drv):
        r = lax.rsqrt(jnp.mean(xv * xv, axis=-1, keepdims=True) + NORM_EPS)
        xhat = xv * r
        dhv = dhv.astype(F32)
        dxhat = dhv * gv
        dx = drv + r * (dxhat - xhat * jnp.mean(dxhat * xhat, axis=-1, keepdims=True))
        dg = jnp.sum(dhv * xhat, axis=0, keepdims=True)
        return (dx, dx, dg) if want_bf16 else (dx, dg)

    row_outs = [(d, F32), (d, BF16)] if want_bf16 else [(d, F32)]
    return _rowwise(fn, [("row", x), ("full", g), ("row", dh), ("row", dres)], row_outs, [(1, d)],
                    rows=x.shape[0], tb=_tile(x.shape[0], 256), name=name)


def _loss_head(x2, g, target, name):
    d = x2.shape[1]

    def fn(xv, gv, tv):
        r = lax.rsqrt(jnp.mean(xv * xv, axis=-1, keepdims=True) + NORM_EPS)
        xhat = xv * r
        diff = xhat * gv - tv
        loss = 0.5 * jnp.sum(jnp.mean(diff * diff, axis=-1, keepdims=True), axis=0, keepdims=True)
        dy = diff * (1.0 / d)
        dxhat = dy * gv
        dx = r * (dxhat - xhat * jnp.mean(dxhat * xhat, axis=-1, keepdims=True))
        dg = jnp.sum(dy * xhat, axis=0, keepdims=True)
        return dx, dx, dg, jnp.broadcast_to(loss, (1, LANE))

    return _rowwise(fn, [("row", x2), ("full", g), ("row", target)], [(d, F32), (d, BF16)], [(1, d), (1, LANE)],
                    rows=x2.shape[0], tb=_tile(x2.shape[0], 256), name=name)


def _swiglu_bwd_tile(dact, dswiglu):
    return ((dact * dswiglu[0].astype(F32), dact * dswiglu[1].astype(F32)),)


def _mix_fwd(proj, pa, pb, d, name):
    def fn(ga, gb, av, bv):
        return (_sigmoid(ga.astype(F32)) * av.astype(F32) + _sigmoid(gb.astype(F32)) * bv.astype(F32),)
    return _rowwise(fn, [("win", proj, d, 0), ("win", proj, d, 1), ("row", pa), ("row", pb)], [(d, BF16)], [],
                    rows=pa.shape[0], tb=_tile(pa.shape[0], 512), name=name)[0]


def _mix_bwd(proj, pa, pb, dmix, d, name):
    def fn(ga, gb, av, bv, dm):
        dm = dm.astype(F32)
        sa, sb = _sigmoid(ga.astype(F32)), _sigmoid(gb.astype(F32))
        av, bv = av.astype(F32), bv.astype(F32)
        return dm * sa, dm * sb, dm * av * sa * (1.0 - sa), dm * bv * sb * (1.0 - sb)
    return _rowwise(fn, [("win", proj, d, 0), ("win", proj, d, 1), ("row", pa), ("row", pb), ("row", dmix)],
                    [(d, BF16)] * 4, [], rows=pa.shape[0], tb=_tile(pa.shape[0], 512), name=name)


def _chunk_masks(tb):
    r = lax.broadcasted_iota(jnp.int32, (tb, tb), 0)
    c = lax.broadcasted_iota(jnp.int32, (tb, tb), 1)
    same = lax.shift_right_logical(r, GLA_CHUNK_SHIFT) == lax.shift_right_logical(c, GLA_CHUNK_SHIFT)
    return same, same & (c <= r), same & (r <= c)


def _mask_bf16(mask):
    return jnp.where(mask, 1.0, 0.0).astype(BF16)


def _split_dot(mask_bf, x, terms):
    acc, rem = None, x
    for _ in range(terms):
        hi = rem.astype(BF16)
        part = jnp.dot(mask_bf, hi, preferred_element_type=F32)
        acc = part if acc is None else acc + part
        rem = rem - hi.astype(F32)
    return acc


def _gla_decay(al, wa2, ba2, same_bf, causal_bf):
    z = jnp.dot(al.astype(BF16), wa2, preferred_element_type=F32) + ba2
    la = (jnp.minimum(z, 0.0) - jnp.log(1.0 + jnp.exp(-jnp.abs(z)))) * (1.0 / GLA_TAU)
    bc = _split_dot(causal_bf, la, 3)
    bl = _split_dot(same_bf, la, 3)
    return z, bc, bl


def _dot_t(a, b, ca, cb):
    return lax.dot_general(a, b, (((ca,), (cb,)), ((), ())), preferred_element_type=F32)


def _gla_fwd(proj, alow, wa2, ba2, ghn, *, dk, dv, name):
    t = proj.shape[0]
    tb = min(GLA_BLOCK, t)
    nch = tb // GLA_CHUNK
    hk, hv = dk // GLA_HEADS, dv // GLA_HEADS
    scale = hk ** -0.5
    v_cb, r_cb = (8 * dk) // dv, (8 * dk) // dv + 1
    q_cb, k_cb = (8 * dk + 2 * dv) // dk, (8 * dk + 2 * dv) // dk + 1

    def body(q_ref, k_ref, v_ref, r_ref, al_ref, wa2_ref, ba2_ref, ghn_ref, oa_ref, opre_ref, s_ref, st_scr):
        @pl.when(pl.program_id(0) == 0)
        def _():
            st_scr[...] = jnp.zeros_like(st_scr)

        same, causal, _ = _chunk_masks(tb)
        same_bf, causal_bf = _mask_bf16(same), _mask_bf16(causal)
        _, bc, bl = _gla_decay(al_ref[...], wa2_ref[...], ba2_ref[...], same_bf, causal_bf)
        q = q_ref[...].astype(F32) * scale
        k = k_ref[...].astype(F32)
        qd = (q * jnp.exp(bc)).astype(BF16)
        ki = (k * jnp.exp(-bc)).astype(BF16)
        ks = (k * jnp.exp(bl - bc)).astype(BF16)
        dl = jnp.exp(bl)
        ksls = [slice(h * hk, (h + 1) * hk) for h in range(GLA_HEADS)]
        vsls = [slice(h * hv, (h + 1) * hv) for h in range(GLA_HEADS)]
        v_hs = [v_ref[:, vsl] for vsl in vsls]
        o_intras = []
        for ksl, v_h in zip(ksls, v_hs):
            sc = jnp.where(causal, _dot_t(qd[:, ksl], ki[:, ksl], 1, 1), 0.0)
            o_intras.append(jnp.dot(sc.astype(BF16), v_h, preferred_element_type=F32))
        for c in range(nch):
            rows = slice(c * GLA_CHUNK, (c + 1) * GLA_CHUNK)
            for h, (ksl, vsl) in enumerate(zip(ksls, vsls)):
                st = st_scr[h]
                s_ref[c, h] = st
                opre_ref[rows, vsl] = o_intras[h][rows] + _dot_t(qd[rows, ksl], st.astype(BF16), 1, 1)
                st_scr[h] = dl[c * GLA_CHUNK:c * GLA_CHUNK + 1, ksl] * st + _dot_t(v_hs[h][rows], ks[rows, ksl], 0, 0)
        for h in range(GLA_HEADS):
            vsl = slice(h * hv, (h + 1) * hv)
            o = opre_ref[:, vsl]
            rs = lax.rsqrt(jnp.mean(o * o, axis=-1, keepdims=True) + NORM_EPS)
            rv = r_ref[:, vsl].astype(F32)
            oa_ref[:, vsl] = (rv * _sigmoid(rv) * (o * rs * ghn_ref[:, vsl])).astype(BF16)

    nchunks = t // GLA_CHUNK
    return pl.pallas_call(
        body, name=name,
        grid=(t // tb,),
        in_specs=[
            pl.BlockSpec((tb, dk), lambda i: (i, q_cb)),
            pl.BlockSpec((tb, dk), lambda i: (i, k_cb)),
            pl.BlockSpec((tb, dv), lambda i: (i, v_cb)),
            pl.BlockSpec((tb, dv), lambda i: (i, r_cb)),
            pl.BlockSpec((tb, LANE), lambda i: (i, 0)),
            pl.BlockSpec(wa2.shape, lambda i: (0, 0)),
            pl.BlockSpec(ba2.shape, lambda i: (0, 0)),
            pl.BlockSpec(ghn.shape, lambda i: (0, 0)),
        ],
        out_specs=[
            pl.BlockSpec((tb, dv), lambda i: (i, 0)),
            pl.BlockSpec((tb, dv), lambda i: (i, 0)),
            pl.BlockSpec((nch, GLA_HEADS, hv, hk), lambda i: (i, 0, 0, 0)),
        ],
        out_shape=[
            jax.ShapeDtypeStruct((t, dv), BF16),
            jax.ShapeDtypeStruct((t, dv), F32),
            jax.ShapeDtypeStruct((nchunks, GLA_HEADS, hv, hk), F32),
        ],
        scratch_shapes=[pltpu.VMEM((GLA_HEADS, hv, hk), F32)],
        compiler_params=_cparams("arbitrary"),
    )(proj, proj, proj, proj, alow, wa2, ba2, ghn)


def _gla_bwd(proj, alow, wa2, ba2, ghn, opre, states, doa, *, dk, dv, name):
    t = proj.shape[0]
    tb = min(GLA_BLOCK, t)
    nb = t // tb
    nch = tb // GLA_CHUNK
    hk, hv = dk // GLA_HEADS, dv // GLA_HEADS
    scale = hk ** -0.5
    v_cb, r_cb = (8 * dk) // dv, (8 * dk) // dv + 1
    q_cb, k_cb = (8 * dk + 2 * dv) // dk, (8 * dk + 2 * dv) // dk + 1

    def body(q_ref, k_ref, v_ref, r_ref, al_ref, wa2_ref, ba2_ref, ghn_ref, opre_ref, s_ref, doa_ref,
             dq_ref, dk_ref, dv_ref, dr_ref, dal_ref, dwa2_ref, dba2_ref, dghn_ref,
             dst_scr, dqd_scr, dki_scr, dks_scr, ddl_scr):
        @pl.when(pl.program_id(0) == 0)
        def _():
            dst_scr[...] = jnp.zeros_like(dst_scr)
            dwa2_ref[...] = jnp.zeros_like(dwa2_ref)
            dba2_ref[...] = jnp.zeros_like(dba2_ref)
            dghn_ref[...] = jnp.zeros_like(dghn_ref)

        same, causal, anti = _chunk_masks(tb)
        same_bf, causal_bf, anti_bf = _mask_bf16(same), _mask_bf16(causal), _mask_bf16(anti)
        al = al_ref[...]
        wa2v = wa2_ref[...]
        z, bc, bl = _gla_decay(al, wa2v, ba2_ref[...], same_bf, causal_bf)
        e_bc, e_nbc, e_st = jnp.exp(bc), jnp.exp(-bc), jnp.exp(bl - bc)
        q = q_ref[...].astype(F32) * scale
        k = k_ref[...].astype(F32)
        qd_f, ki_f, ks_f = q * e_bc, k * e_nbc, k * e_st
        qd, ki, ks = qd_f.astype(BF16), ki_f.astype(BF16), ks_f.astype(BF16)
        dl = jnp.exp(bl)
        per_head = []
        for h in range(GLA_HEADS):
            ksl = slice(h * hk, (h + 1) * hk)
            vsl = slice(h * hv, (h + 1) * hv)
            o = opre_ref[:, vsl]
            rs = lax.rsqrt(jnp.mean(o * o, axis=-1, keepdims=True) + NORM_EPS)
            ohat = o * rs
            g_h = ghn_ref[:, vsl]
            rv = r_ref[:, vsl].astype(F32)
            sg = _sigmoid(rv)
            d_oa = doa_ref[:, vsl].astype(F32)
            don = d_oa * (rv * sg)
            dr_ref[:, vsl] = (d_oa * (ohat * g_h) * (sg * (1.0 + rv * (1.0 - sg)))).astype(BF16)
            dghn_ref[:, vsl] += jnp.sum(don * ohat, axis=0, keepdims=True)
            dohat = don * g_h
            do_f = rs * (dohat - ohat * jnp.mean(dohat * ohat, axis=-1, keepdims=True))
            do = do_f.astype(BF16)
            v_h = v_ref[:, vsl]
            p = jnp.where(causal, _dot_t(do, v_h, 1, 1), 0.0).astype(BF16)
            dqd_intra = jnp.dot(p, ki[:, ksl], preferred_element_type=F32)
            dki_scr[:, ksl] = _dot_t(p, qd[:, ksl], 0, 0)
            sc = jnp.where(causal, _dot_t(qd[:, ksl], ki[:, ksl], 1, 1), 0.0).astype(BF16)
            dv_intra = _dot_t(sc, do, 0, 0)
            per_head.append((ksl, vsl, v_h, do, dqd_intra, dv_intra))
        for c in reversed(range(nch)):
            rows = slice(c * GLA_CHUNK, (c + 1) * GLA_CHUNK)
            for h, (ksl, vsl, v_h, do, dqd_intra, dv_intra) in enumerate(per_head):
                dst = dst_scr[h]
                st = s_ref[c, h]
                dst_bf = dst.astype(BF16)
                dv_ref[rows, vsl] = (dv_intra[rows] + _dot_t(ks[rows, ksl], dst_bf, 1, 1)).astype(BF16)
                dks_scr[rows, ksl] = jnp.dot(v_h[rows], dst_bf, preferred_element_type=F32)
                dl_c = dl[c * GLA_CHUNK:c * GLA_CHUNK + 1, ksl]
                ddl = jnp.sum(dst * st, axis=0, keepdims=True) * dl_c
                ddl_scr[rows, ksl] = jnp.broadcast_to(ddl, (GLA_CHUNK, hk))
                dqd_scr[rows, ksl] = dqd_intra[rows] + jnp.dot(do[rows], st.astype(BF16), preferred_element_type=F32)
                dst_scr[h] = dl_c * dst + _dot_t(do[rows], qd[rows, ksl], 0, 0)
        dqd, dki, dks = dqd_scr[...], dki_scr[...], dks_scr[...]
        dq_ref[...] = (dqd * (scale * e_bc)).astype(BF16)
        dk_ref[...] = (dki * e_nbc + dks * e_st).astype(BF16)
        dks_ks = dks * ks_f
        dbc = dqd * qd_f - dki * ki_f - dks_ks
        dla = _split_dot(anti_bf, dbc, 2) + _split_dot(same_bf, dks_ks, 2) + ddl_scr[...]
        dz = (dla * (1.0 / GLA_TAU) * (1.0 - _sigmoid(z)))
        dz_bf = dz.astype(BF16)
        dal_ref[...] = _dot_t(dz_bf, wa2v, 1, 1).astype(BF16)
        dwa2_ref[...] += _dot_t(al.astype(BF16), dz_bf, 0, 0)
        dba2_ref[...] += jnp.sum(dz, axis=0, keepdims=True)

    rev = lambda i: nb - 1 - i
    return pl.pallas_call(
        body, name=name,
        grid=(nb,),
        in_specs=[
            pl.BlockSpec((tb, dk), lambda i: (rev(i), q_cb)),
            pl.BlockSpec((tb, dk), lambda i: (rev(i), k_cb)),
            pl.BlockSpec((tb, dv), lambda i: (rev(i), v_cb)),
            pl.BlockSpec((tb, dv), lambda i: (rev(i), r_cb)),
            pl.BlockSpec((tb, LANE), lambda i: (rev(i), 0)),
            pl.BlockSpec(wa2.shape, lambda i: (0, 0)),
            pl.BlockSpec(ba2.shape, lambda i: (0, 0)),
            pl.BlockSpec(ghn.shape, lambda i: (0, 0)),
            pl.BlockSpec((tb, dv), lambda i: (rev(i), 0)),
            pl.BlockSpec((nch, GLA_HEADS, hv, hk), lambda i: (rev(i), 0, 0, 0)),
            pl.BlockSpec((tb, dv), lambda i: (rev(i), 0)),
        ],
        out_specs=[
            pl.BlockSpec((tb, dk), lambda i: (rev(i), 0)),
            pl.BlockSpec((tb, dk), lambda i: (rev(i), 0)),
            pl.BlockSpec((tb, dv), lambda i: (rev(i), 0)),
            pl.BlockSpec((tb, dv), lambda i: (rev(i), 0)),
            pl.BlockSpec((tb, LANE), lambda i: (rev(i), 0)),
            pl.BlockSpec(wa2.shape, lambda i: (0, 0)),
            pl.BlockSpec(ba2.shape, lambda i: (0, 0)),
            pl.BlockSpec(ghn.shape, lambda i: (0, 0)),
        ],
        out_shape=[
            jax.ShapeDtypeStruct((t, dk), BF16),
            jax.ShapeDtypeStruct((t, dk), BF16),
            jax.ShapeDtypeStruct((t, dv), BF16),
            jax.ShapeDtypeStruct((t, dv), BF16),
            jax.ShapeDtypeStruct((t, LANE), BF16),
            jax.ShapeDtypeStruct(wa2.shape, F32),
            jax.ShapeDtypeStruct(ba2.shape, F32),
            jax.ShapeDtypeStruct(ghn.shape, F32),
        ],
        scratch_shapes=[pltpu.VMEM((GLA_HEADS, hv, hk), F32)] + [pltpu.VMEM((tb, dk), F32)] * 4,
        compiler_params=_cparams("arbitrary"),
    )(proj, proj, proj, proj, alow, wa2, ba2, ghn, opre, states, doa)


def _s5_tables(lam_re, lam_im, log_dt, b_re, b_im, c_re, c_im):
    hp = lax.Precision.HIGHEST
    g, p = lam_re.shape
    ln = S5_L
    dt = jnp.exp(log_dt)[:, None]
    lr, li = lam_re, lam_im
    mag = jnp.exp(lr * dt)
    ar, ai = mag * jnp.cos(li * dt), mag * jnp.sin(li * dt)
    den = lr * lr + li * li
    am1 = ar - 1.0
    f_re = ((am1 * lr + ai * li) / den)[..., None]
    f_im = ((ai * lr - am1 * li) / den)[..., None]
    bb_re = f_re * b_re - f_im * b_im
    bb_im = f_re * b_im + f_im * b_re
    j = jnp.arange(ln + 1, dtype=F32)[None, :, None]
    pm = jnp.exp(j * (lr * dt)[:, None, :])
    ang = j * (li * dt)[:, None, :]
    pw_re, pw_im = pm * jnp.cos(ang), pm * jnp.sin(ang)
    cp_re = c_re[:, None] * pw_re[:, :, None, :] - c_im[:, None] * pw_im[:, :, None, :]
    cp_im = c_re[:, None] * pw_im[:, :, None, :] + c_im[:, None] * pw_re[:, :, None, :]
    kj = (jnp.einsum("gjcp,gpd->gjcd", cp_re[:, :ln], bb_re, precision=hp)
          - jnp.einsum("gjcp,gpd->gjcd", cp_im[:, :ln], bb_im, precision=hp))
    s_i = jnp.arange(ln)[None, :, None]
    t_i = jnp.arange(ln)[None, None, :]
    j_i = jnp.arange(ln)[:, None, None]
    shift = (t_i - s_i == j_i).astype(F32)
    m = jnp.einsum("jst,gjcd->gsdtc", shift, kj, precision=hp).reshape(g, ln * S5_GC, ln * S5_GC)
    rp_re, rp_im = pw_re[:, ln - 1::-1], pw_im[:, ln - 1::-1]
    bbt_re, bbt_im = bb_re.transpose(0, 2, 1)[:, None], bb_im.transpose(0, 2, 1)[:, None]
    bst_re = rp_re[:, :, None, :] * bbt_re - rp_im[:, :, None, :] * bbt_im
    bst_im = rp_re[:, :, None, :] * bbt_im + rp_im[:, :, None, :] * bbt_re
    bst = jnp.concatenate([bst_re, bst_im], axis=-1).reshape(g, ln * S5_GC, 2 * p)
    cst = jnp.concatenate([cp_re[:, 1:].transpose(0, 3, 1, 2), -cp_im[:, 1:].transpose(0, 3, 1, 2)], axis=1)
    cst = cst.reshape(g, 2 * p, ln * S5_GC)
    a = jnp.stack([jnp.concatenate([pw_re[:, ln], pw_re[:, ln]], axis=-1),
                   jnp.concatenate([-pw_im[:, ln], pw_im[:, ln]], axis=-1)], axis=1)
    return m, bst, cst, a


def _state_scan(v, pr, pi, reverse):
    n = v.shape[0]
    half = v.shape[1] // 2
    row = lax.broadcasted_iota(jnp.int32, v.shape, 0)
    z, s = v, 1
    while s < n:
        if reverse:
            zs = jnp.where(row < n - s, pltpu.roll(z, n - s, 0), 0.0)
        else:
            zs = jnp.where(row >= s, pltpu.roll(z, s, 0), 0.0)
        z = z + zs * pr + pltpu.roll(zs, half, 1) * pi
        pr, pi = pr * pr - pi * pi, 2.0 * pr * pi
        s *= 2
    return z


def _s5_core_fwd(u_g, m, bst, cst, a, name):
    g, nc, w = u_g.shape
    p2 = bst.shape[2]

    def body(u_ref, m_ref, b_ref, c_ref, a_ref, y_ref, x_ref):
        u = u_ref[0]
        v = jnp.dot(u, b_ref[0], preferred_element_type=F32)
        z = _state_scan(v, a_ref[0, 0:1, :], a_ref[0, 1:2, :], reverse=False)
        row = lax.broadcasted_iota(jnp.int32, z.shape, 0)
        x = jnp.where(row >= 1, pltpu.roll(z, 1, 0), 0.0)
        x_ref[0] = x
        y_ref[0] = (jnp.dot(u, m_ref[0], preferred_element_type=F32)
                    + jnp.dot(x.astype(BF16), c_ref[0], preferred_element_type=F32))

    per_g = lambda shape: pl.BlockSpec((1,) + shape, lambda i: (i, 0, 0))
    return pl.pallas_call(
        body, name=name, grid=(g,),
        in_specs=[per_g((nc, w)), per_g((w, w)), per_g((w, p2)), per_g((p2, w)), per_g((2, p2))],
        out_specs=[per_g((nc, w)), per_g((nc, p2))],
        out_shape=[jax.ShapeDtypeStruct((g, nc, w), F32), jax.ShapeDtypeStruct((g, nc, p2), F32)],
        compiler_params=_cparams("parallel"),
    )(u_g, m, bst, cst, a)


def _s5_core_bwd(dy_g, u_g, x_g, m, bst, cst, a, name):
    g, nc, w = u_g.shape
    p2 = bst.shape[2]

    def body(dy_ref, u_ref, x_ref, m_ref, b_ref, c_ref, a_ref, du_ref, dm_ref, db_ref, dc_ref, da_ref):
        dy, u, x = dy_ref[0], u_ref[0], x_ref[0]
        gx = _dot_t(dy, c_ref[0], 1, 1)
        rtot = _state_scan(gx, a_ref[0, 0:1, :], -a_ref[0, 1:2, :], reverse=True)
        row = lax.broadcasted_iota(jnp.int32, rtot.shape, 0)
        dv = jnp.where(row < nc - 1, pltpu.roll(rtot, nc - 1, 0), 0.0)
        dv_bf = dv.astype(BF16)
        du_ref[0] = (_dot_t(dy, m_ref[0], 1, 1) + _dot_t(dv_bf, b_ref[0], 1, 1)).astype(BF16)
        dm_ref[0] = _dot_t(u, dy, 0, 0)
        dc_ref[0] = _dot_t(x.astype(BF16), dy, 0, 0)
        db_ref[0] = _dot_t(u, dv_bf, 0, 0)
        x_sw = pltpu.roll(x, p2 // 2, 1)
        da_ref[0, 0:1, :] = jnp.sum(dv * x, axis=0, keepdims=True)
        da_ref[0, 1:2, :] = jnp.sum(dv * x_sw, axis=0, keepdims=True)

    per_g = lambda shape: pl.BlockSpec((1,) + shape, lambda i: (i, 0, 0))
    return pl.pallas_call(
        body, name=name, grid=(g,),
        in_specs=[per_g((nc, w)), per_g((nc, w)), per_g((nc, p2)), per_g((w, w)), per_g((w, p2)), per_g((p2, w)), per_g((2, p2))],
        out_specs=[per_g((nc, w)), per_g((w, w)), per_g((w, p2)), per_g((p2, w)), per_g((2, p2))],
        out_shape=[jax.ShapeDtypeStruct((g, nc, w), BF16), jax.ShapeDtypeStruct((g, w, w), F32),
                   jax.ShapeDtypeStruct((g, w, p2), F32), jax.ShapeDtypeStruct((g, p2, w), F32),
                   jax.ShapeDtypeStruct((g, 2, p2), F32)],
        compiler_params=_cparams("parallel"),
    )(dy_g, u_g, x_g, m, bst, cst, a)


def _gelu_parts(y):
    inner = GELU_C * (y + GELU_A * y * y * y)
    th = jnp.tanh(inner)
    return th, 0.5 * y * (1.0 + th)


def _s5_post_fwd(y_raw, proj, u_cb, s5d, wglu, bglu, name):
    w = y_raw.shape[1]

    def fn(yr, u, dsk, wg, bg):
        y = yr + dsk * u.astype(F32)
        _, h = _gelu_parts(y)
        gl = jnp.dot(h.astype(BF16), wg, preferred_element_type=F32) + bg
        return (h * _sigmoid(gl),)

    return _rowwise(fn, [("row", y_raw), ("win", proj, w, u_cb), ("full", s5d), ("full", wglu), ("full", bglu)],
                    [(w, BF16)], [], rows=y_raw.shape[0], tb=_tile(y_raw.shape[0], 512), name=name)[0]


def _s5_post_bwd(y_raw, proj, u_cb, s5d, wglu, bglu, dob, name):
    w = y_raw.shape[1]

    def fn(yr, u, dsk, wg, bg, dov):
        u = u.astype(F32)
        dov = dov.astype(F32)
        y = yr + dsk * u
        th, h = _gelu_parts(y)
        h_bf = h.astype(BF16)
        gl = jnp.dot(h_bf, wg, preferred_element_type=F32) + bg
        sg = _sigmoid(gl)
        dgl = dov * h * sg * (1.0 - sg)
        dgl_bf = dgl.astype(BF16)
        dh = dov * sg + _dot_t(dgl_bf, wg, 1, 1)
        dgelu = 0.5 * (1.0 + th) + 0.5 * y * (1.0 - th * th) * GELU_C * (1.0 + 3.0 * GELU_A * y * y)
        dy = dh * dgelu
        return (dy, dy * dsk,
                _dot_t(h_bf, dgl_bf, 0, 0), jnp.sum(dgl, axis=0, keepdims=True), jnp.sum(dy * u, axis=0, keepdims=True))

    return _rowwise(fn, [("row", y_raw), ("win", proj, w, u_cb), ("full", s5d), ("full", wglu), ("full", bglu), ("row", dob)],
                    [(w, BF16), (w, BF16)], [(w, w), (1, w), (1, w)], rows=y_raw.shape[0], tb=_tile(y_raw.shape[0], 512), name=name)


def _to_groups(a, dtype):
    t, w = a.shape
    g = w // S5_GC
    return a.reshape(t // S5_L, S5_L, g, S5_GC).transpose(2, 0, 1, 3).reshape(g, t // S5_L, S5_L * S5_GC).astype(dtype)


def _from_groups(a):
    g, nc, _ = a.shape
    return a.reshape(g, nc, S5_L, S5_GC).transpose(1, 2, 0, 3).reshape(nc * S5_L, g * S5_GC)


def _adamw(w, g, m, v, name):
    _, rows, cols = w.shape
    tr, tc = (_tile(rows, 256, align=16), cols) if rows % 16 == 0 else (rows, _tile(cols, 256))
    slots = isinstance(g, (list, tuple))
    gs = list(g) if slots else [g]
    c1 = 1.0 - ADAM_B1 ** ADAM_STEP
    c2 = 1.0 - ADAM_B2 ** ADAM_STEP

    def body(w_ref, m_ref, v_ref, *refs):
        g_refs, out_refs = refs[:len(gs)], refs[len(gs):]
        if slots:
            parts = [g_ref[s].astype(F32) for g_ref in g_refs for s in range(g_ref.shape[0])]
            gv = parts[0]
            for p in parts[1:]:
                gv = gv + p
            out_refs[0][...] = gv
        else:
            gv = g_refs[0][...]
        d_ref, nm_ref, nv_ref = out_refs[-3:]
        nm = ADAM_B1 * m_ref[...] + (1.0 - ADAM_B1) * gv
        nv = ADAM_B2 * v_ref[...] + (1.0 - ADAM_B2) * (gv * gv)
        d_ref[...] = -ADAM_LR * ((nm / c1) / (jnp.sqrt(nv / c2) + ADAM_EPS) + ADAM_WD * w_ref[...])
        nm_ref[...] = nm
        nv_ref[...] = nv

    spec = pl.BlockSpec((None, tr, tc), lambda i, j: (0, i, j))
    g_specs = [pl.BlockSpec((a.shape[0], tr, tc), lambda i, j: (0, i, j)) for a in gs] if slots else [pl.BlockSpec((tr, tc), lambda i, j: (i, j))]
    n_out = 4 if slots else 3
    return pl.pallas_call(
        body, name=name, grid=(rows // tr, cols // tc),
        in_specs=[spec, spec, spec] + g_specs, out_specs=[spec] * n_out,
        out_shape=[jax.ShapeDtypeStruct((1, rows, cols), F32)] * n_out,
        compiler_params=_cparams("parallel", "parallel"),
    )(w, m, v, *gs)


def _slot_sum(x, name):
    _, rows, cols = x.shape
    if rows % 8 == 0:
        tr, tc = _tile(rows, 512, align=8), cols
    else:
        tr, tc = rows, _tile(cols, 256)

    def body(x_ref, o_ref):
        acc = x_ref[0].astype(F32)
        for s in range(1, N_DEV):
            acc = acc + x_ref[s].astype(F32)
        o_ref[...] = acc

    return pl.pallas_call(
        body, name=name, grid=(rows // tr, cols // tc),
        in_specs=[pl.BlockSpec((N_DEV, tr, tc), lambda i, j: (0, i, j))],
        out_specs=pl.BlockSpec((tr, tc), lambda i, j: (i, j)),
        out_shape=jax.ShapeDtypeStruct((rows, cols), F32),
        compiler_params=_cparams("parallel", "parallel"),
    )(x)


_REST = (("w_a2", 1), ("w_glu", 0), ("w_branch_a", 1), ("w_branch_b", 1), ("w_out", 0), ("w_ffn_in", 1), ("w_ffn_out", 0))
_SMALL = ("norm1_g", "b_a2", "gla_norm_g", "lam_re", "lam_im", "log_dt", "s5_b_re", "s5_b_im", "s5_c_re", "s5_c_im",
          "s5_d", "b_glu", "norm2_g", "final_norm_g")
_ORDER = ("norm1_g", "w_in", "w_a2", "b_a2", "gla_norm_g", "lam_re", "lam_im", "log_dt", "s5_b_re", "s5_b_im", "s5_c_re",
          "s5_c_im", "s5_d", "w_glu", "b_glu", "w_branch_a", "w_branch_b", "w_out", "norm2_g", "w_ffn_in", "w_ffn_out", "final_norm_g")


def _join_slots(slots, axis):
    _, r, c = slots.shape
    if axis == 0:
        return slots.reshape(N_DEV * r, c)
    return slots.transpose(1, 0, 2).reshape(r, N_DEV * c)


def _to_slots(full, axis):
    r, c = full.shape
    if axis == 0:
        return full.reshape(N_DEV, r // N_DEV, c)
    return full.reshape(r, N_DEV, c // N_DEV).transpose(1, 0, 2)


def _local_step(x, target, w_in_t, small, rest):
    t, d = x.shape
    dk, dv, s5w = d // 4, d // 2, d // 4
    dist = not isinstance(rest, dict)
    if dist:
        h1, (w_in_slots,) = _rms_fwd(x, small["norm1_g"], "norm1_fwd", carry=("ag", [w_in_t]))
        w_in_t = w_in_slots.reshape(-1, d)
    else:
        h1 = _rms_fwd(x, small["norm1_g"], "norm1_fwd")
    o_q, o_k, o_v, o_r, o_al = 0, dk, 2 * dk, 2 * dk + dv, 2 * dk + 2 * dv
    o_u = o_al + GLA_RANK
    o_ga, o_gb = o_u + s5w, o_u + s5w + d
    rows = lambda a, o, n: a[o:o + n]
    w_main_t = jnp.concatenate([rows(w_in_t, o_ga, d), rows(w_in_t, o_gb, d), rows(w_in_t, o_v, dv), rows(w_in_t, o_r, dv),
                                rows(w_in_t, o_q, dk), rows(w_in_t, o_k, dk), rows(w_in_t, o_u, s5w)], axis=0)
    w_al_t = jnp.pad(rows(w_in_t, o_al, GLA_RANK), ((0, LANE - GLA_RANK), (0, 0)))
    u_cb = (2 * d + 2 * dv + 2 * dk) // s5w

    if dist:
        proj, gathered = _mm(h1, w_main_t, tb=True, out_dtype=BF16, carry=("ag", rest[:-1]), name="in_proj")
        w = {n: _join_slots(g, ax) for (n, ax), g in zip(_REST[:-2], gathered[:-1])}
        w_ffn_in_s = gathered[-1]
    else:
        proj = _mm(h1, w_main_t, tb=True, out_dtype=BF16, name="in_proj")
        w = rest
        w_ffn_in_s = _to_slots(rest["w_ffn_in"], 1)
    wa2 = jnp.pad(w["w_a2"], ((0, LANE - GLA_RANK), (0, 0)))
    alow = _mm(h1, w_al_t, tb=True, out_dtype=BF16, name="in_proj_gate_rank")
    o_a, o_pre, states = _gla_fwd(proj, alow, wa2, small["b_a2"], small["gla_norm_g"], dk=dk, dv=dv, name="gla_fwd")

    s5_params = (small["lam_re"], small["lam_im"], small["log_dt"][0], small["s5_b_re"], small["s5_b_im"],
                 small["s5_c_re"], small["s5_c_im"])
    (tm, tbst, tcst, ta), tables_vjp = jax.vjp(_s5_tables, *s5_params)
    tm_bf, tbst_bf, tcst_bf = tm.astype(BF16), tbst.astype(BF16), tcst.astype(BF16)
    u_g = _to_groups(proj[:, u_cb * s5w:(u_cb + 1) * s5w], BF16)
    y_g, x_g = _s5_core_fwd(u_g, tm_bf, tbst_bf, tcst_bf, ta, "s5_core_fwd")
    y_raw = _from_groups(y_g)
    o_b = _s5_post_fwd(y_raw, proj, u_cb, small["s5_d"], w["w_glu"], small["b_glu"], "s5_post_fwd")

    pa = _mm(o_a, w["w_branch_a"], out_dtype=BF16, name="branch_a")
    pb = _mm(o_b, w["w_branch_b"], out_dtype=BF16, name="branch_b")
    mix = _mix_fwd(proj, pa, pb, d, "mix_fwd")
    x1 = _mm(mix, w["w_out"], res=x, name="out_proj")
    h2 = _rms_fwd(x1, small["norm2_g"], "norm2_fwd")
    if dist:
        gu, act, (w_ffn_out_s,) = _ffn_in_fused(h2, w_ffn_in_s, carry=("ag", rest[-1:]), name="ffn_in")
        w_ffn_out = _join_slots(w_ffn_out_s, 0)
    else:
        gu, act = _ffn_in_fused(h2, w_ffn_in_s, name="ffn_in")
        w_ffn_out = rest["w_ffn_out"]
    x2 = _mm(act, w_ffn_out, res=x1, name="ffn_out")
    dx2, dx2_bf, d_final_g, loss = _loss_head(x2, small["final_norm_g"], target, "loss_head")

    recv = {}
    dgu, = _mm(dx2_bf, w_ffn_out, tb=True, epi=(_swiglu_bwd_tile, [gu], [(2, BF16)]), name="d_act")
    g_ffn_out = _mm(act, dx2_bf, ta=True, out_dtype=BF16, name="g_w_ffn_out")
    if dist:
        g_ffn_in_s, recv["w_ffn_out"] = _mm(h2, dgu, ta=True, b_slots=True, out_dtype=BF16, out_slots=N_DEV,
                                            carry=("a2a", [_to_slots(g_ffn_out, 0)]), name="g_w_ffn_in")
        dh2, (recv_ffn_in,) = _mm(dgu, w_ffn_in_s, tb=True, a_slots=True, b_slots=True,
                                  carry=("a2a", [g_ffn_in_s], [_ALL_K[:-1]]), name="d_h2")
    else:
        g_ffn_in_s = _mm(h2, dgu, ta=True, b_slots=True, out_dtype=BF16, out_slots=N_DEV, name="g_w_ffn_in")
        dh2 = _mm(dgu, w_ffn_in_s, tb=True, a_slots=True, b_slots=True, name="d_h2")
    dx1, dx1_bf, d_norm2_g = _rms_bwd(x1, small["norm2_g"], dh2, dx2, "norm2_bwd", True)
    dmix = _mm(dx1_bf, w["w_out"], tb=True, out_dtype=BF16, name="d_mix")
    g_out = _mm(mix, dx1_bf, ta=True, out_dtype=BF16, name="g_w_out")
    dpa, dpb, dga, dgb = _mix_bwd(proj, pa, pb, dmix, d, "mix_bwd")
    doa = _mm(dpa, w["w_branch_a"], tb=True, out_dtype=BF16, name="d_o_a")
    dob = _mm(dpb, w["w_branch_b"], tb=True, out_dtype=BF16, name="d_o_b")
    g_branch_a = _mm(o_a, dpa, ta=True, out_dtype=BF16, name="g_w_branch_a")
    g_branch_b = _mm(o_b, dpb, ta=True, out_dtype=BF16, name="g_w_branch_b")

    dy_s5, du_direct, g_glu, g_bglu, g_s5d = _s5_post_bwd(y_raw, proj, u_cb, small["s5_d"], w["w_glu"], small["b_glu"], dob, "s5_post_bwd")
    du_g, d_tm, d_tbst, d_tcst, d_ta = _s5_core_bwd(_to_groups(dy_s5, BF16), u_g, x_g, tm_bf, tbst_bf, tcst_bf, ta, "s5_core_bwd")
    g_lam_re, g_lam_im, g_log_dt, g_b_re, g_b_im, g_c_re, g_c_im = tables_vjp((d_tm, d_tbst, d_tcst, d_ta))
    du = _from_groups(du_g) + du_direct

    dq, dkk, dvv, dr, dal, g_wa2, g_ba2, g_ghn = _gla_bwd(proj, alow, wa2, small["b_a2"], small["gla_norm_g"], o_pre, states, doa,
                                                        dk=dk, dv=dv, name="gla_bwd")
    dproj = jnp.concatenate([dga, dgb, dvv, dr, dq, dkk, du], axis=1)
    mid = {"w_out": g_out, "w_branch_a": g_branch_a, "w_branch_b": g_branch_b, "w_glu": g_glu.astype(BF16),
           "w_a2": g_wa2[:GLA_RANK].astype(BF16)}
    if dist:
        axes = dict(_REST)
        g_main_t, got = _mm(dproj, h1, ta=True, out_dtype=BF16, name="g_w_in_main",
                            carry=("a2a", [_to_slots(mid[n], axes[n]) for n in mid] + [g_ffn_in_s], [_ALL_K] * len(mid) + [_ALL_K[-1:]]))
        recv.update(zip(mid, [[g] for g in got[:-1]]))
        recv["w_ffn_in"] = [recv_ffn_in, got[-1]]
    else:
        g_main_t = _mm(dproj, h1, ta=True, out_dtype=BF16, name="g_w_in_main")
    g_al_t = _mm(dal, h1, ta=True, out_dtype=BF16, name="g_w_in_gate_rank")
    mrows = lambda o, n: g_main_t[o:o + n]
    g_w_in_t = jnp.concatenate([mrows(2 * d + 2 * dv, dk), mrows(2 * d + 2 * dv + dk, dk), mrows(2 * d, dv), mrows(2 * d + dv, dv),
                                g_al_t[:GLA_RANK], mrows(2 * d + 2 * dv + 2 * dk, s5w), mrows(0, d), mrows(d, d)], axis=0)
    if dist:
        dh1, recv["w_in"] = _mm(dproj, w_main_t, carry=("a2a", [_to_slots(g_w_in_t, 0)]), name="d_h1_main")
    else:
        dh1 = _mm(dproj, w_main_t, name="d_h1_main")
    dh1 = _mm(dal, w_al_t, res=dh1, name="d_h1_gate_rank")
    grad_x, d_norm1_g = _rms_bwd(x, small["norm1_g"], dh1, dx1, "norm1_bwd", False)

    small_g = {
        "norm1_g": d_norm1_g, "b_a2": g_ba2, "gla_norm_g": g_ghn, "lam_re": g_lam_re, "lam_im": g_lam_im,
        "log_dt": g_log_dt[None], "s5_b_re": g_b_re, "s5_b_im": g_b_im, "s5_c_re": g_c_re, "s5_c_im": g_c_im,
        "s5_d": g_s5d, "b_glu": g_bglu, "norm2_g": d_norm2_g, "final_norm_g": d_final_g,
    }
    if not dist:
        recv = dict(mid, w_in=g_w_in_t, w_ffn_in=_join_slots(g_ffn_in_s, 1), w_ffn_out=g_ffn_out)
    return loss[0, 0], grad_x, recv, small_g


def _small_2d(name, a):
    a = a[0]
    return a[None] if a.ndim == 1 else a


def kernel(x, norm1_g, w_in, w_a2, b_a2, gla_norm_g, lam_re, lam_im, log_dt, s5_b_re, s5_b_im, s5_c_re, s5_c_im, s5_d, w_glu, b_glu, w_branch_a, w_branch_b, w_out, norm2_g, w_ffn_in, w_ffn_out, final_norm_g, loss_target, m_norm1_g, m_w_in, m_w_a2, m_b_a2, m_gla_norm_g, m_lam_re, m_lam_im, m_log_dt, m_s5_b_re, m_s5_b_im, m_s5_c_re, m_s5_c_im, m_s5_d, m_w_glu, m_b_glu, m_w_branch_a, m_w_branch_b, m_w_out, m_norm2_g, m_w_ffn_in, m_w_ffn_out, m_final_norm_g, v_norm1_g, v_w_in, v_w_a2, v_b_a2, v_gla_norm_g, v_lam_re, v_lam_im, v_log_dt, v_s5_b_re, v_s5_b_im, v_s5_c_re, v_s5_c_im, v_s5_d, v_w_glu, v_b_glu, v_w_branch_a, v_w_branch_b, v_w_out, v_norm2_g, v_w_ffn_in, v_w_ffn_out, v_final_norm_g):
    args = dict(locals())
    weights = {n: args[n] for n in _ORDER}
    m_in = {n: args["m_" + n] for n in _ORDER}
    v_in = {n: args["v_" + n] for n in _ORDER}
    transposed = lambda a: a[0].T[None]
    rest = [weights[n][0].astype(BF16) for n, _ in _REST]
    small = {n: _small_2d(n, weights[n]) for n in _SMALL}
    loss_local, grad_x, recv, small_g = _local_step(x[0], loss_target[0], transposed(weights["w_in"])[0].astype(BF16), small, rest)

    grads, delta, new_m, new_v = {}, {}, {}, {}
    for n, _ in _REST:
        grads[n], delta[n], new_m[n], new_v[n] = _adamw(weights[n], recv[n], m_in[n], v_in[n], "adamw_" + n)
    w_in_out = _adamw(transposed(weights["w_in"]), recv["w_in"], transposed(m_in["w_in"]), transposed(v_in["w_in"]), "adamw_w_in")
    grads["w_in"], delta["w_in"], new_m["w_in"], new_v["w_in"] = (transposed(a) for a in w_in_out)

    s_sizes = [small_g[n].size for n in _SMALL]
    s_offs = [sum(s_sizes[:i]) for i in range(len(s_sizes))]
    s_total = sum(s_sizes)
    s_rows = -(-(-(-(s_total + 1) // LANE)) // LANE) * LANE

    def pack_small(parts):
        flat = jnp.concatenate([p.reshape(-1) for p in parts])
        return jnp.pad(flat, (0, s_rows * LANE - flat.size)).reshape(s_rows, LANE)

    s_flat = pack_small([small_g[n] for n in _SMALL] + [loss_local])
    s_red = _slot_sum(_exchange("ag", [s_flat], "small_grads_all_gather")[0], "small_grads_slot_sum")
    loss = s_red.reshape(-1)[s_total]
    sd, sm, sv = _adamw(pack_small([weights[n] for n in _SMALL])[None], s_red, pack_small([m_in[n] for n in _SMALL])[None],
                        pack_small([v_in[n] for n in _SMALL])[None], "adamw_small")
    sd, sm, sv = sd[0], sm[0], sv[0]
    for n, o, s in zip(_SMALL, s_offs, s_sizes):
        shape = weights[n].shape[1:]
        grads[n], delta[n], new_m[n], new_v[n] = (a.reshape(-1)[o:o + s].reshape(shape) for a in (s_red, sd, sm, sv))

    out = [loss, grad_x[None]]
    for tree in (grads, delta, new_m, new_v):
        out += [tree[n].reshape(weights[n].shape) for n in _ORDER]
    return tuple(out)
```

```python
import functools
import math

import jax
import jax.numpy as jnp
from jax import lax
from jax.experimental import pallas as pl
from jax.experimental.pallas import tpu as pltpu

F32 = jnp.float32
BF16 = jnp.bfloat16

NORM_EPS = 1e-6
N_DEV = 8
N_PEER = N_DEV - 1
GLA_HEADS = 4
GLA_CHUNK = 32
GLA_CHUNK_SHIFT = 5
GLA_TAU = 16.0
GLA_RANK = 16
GLA_BLOCK = 256
S5_GC = 16
S5_P = 64
S5_L = 16
LANE = 128
V7X_VMEM_LIMIT = 56 * 1024 * 1024
V7X_MM_VMEM_BUDGET = 40 * 1024 * 1024
V7X_MM_TILE_MN = 1408
V7X_MM_TILE_MN_WHOLE_K = 512
V7X_MM_TILE_K = 2048
V7X_EPI_ROW_CHUNKS = 4

ADAM_LR = 0.001
ADAM_B1 = 0.9
ADAM_B2 = 0.999
ADAM_EPS = 1e-08
ADAM_WD = 0.01
ADAM_STEP = 10

GELU_C = math.sqrt(2.0 / math.pi)
GELU_A = 0.044715

MESH = pl.DeviceIdType.MESH


def _cparams(*sem):
    return pltpu.CompilerParams(dimension_semantics=sem, vmem_limit_bytes=V7X_VMEM_LIMIT)


def _divisors_down(n, start, align=LANE):
    t = (min(start, n) // align) * align
    found = False
    while t >= align:
        if n % t == 0:
            found = True
            yield t
        t -= align
    if not found:
        yield n


def _tile(n, target, align=LANE):
    return next(_divisors_down(n, target, align))


def _sigmoid(x):
    return 1.0 / (1.0 + jnp.exp(-x))


_HBM_SPEC = pl.BlockSpec(memory_space=pltpu.HBM)


def _exchange_scratch(n):
    return [pltpu.SemaphoreType.DMA((n * N_PEER,)), pltpu.SemaphoreType.DMA((n * N_PEER,)), pltpu.SemaphoreType.DMA((n,))]


def _ag_phases(x_refs, out_refs, send_sems, recv_sems, local_sems):
    n = len(x_refs)
    x, y, c = lax.axis_index("x"), lax.axis_index("y"), lax.axis_index("c")
    me, sibling = (x, y, c), (x, y, 1 - c)
    chips = [(1 - x, y), (x, 1 - y), (1 - x, 1 - y)]

    def copy(a, k, block, to, from_input=False):
        dst = out_refs[a].at[4 * block[0] + 2 * block[1] + block[2]]
        return pltpu.make_async_remote_copy(
            src_ref=x_refs[a] if from_input else dst, dst_ref=dst,
            send_sem=send_sems.at[a * N_PEER + k], recv_sem=recv_sems.at[a * N_PEER + k], device_id=to, device_id_type=MESH)

    def local(a):
        return pltpu.make_async_copy(x_refs[a], out_refs[a].at[4 * x + 2 * y + c], local_sems.at[a])

    def first(a):
        return [copy(a, 0, me, sibling, True)] + [copy(a, 1 + j, me, (*chip, c), True) for j, chip in enumerate(chips)]

    def start():
        for a in range(n):
            local(a).start()
            for cp in first(a):
                cp.start()

    def relay():
        for j, chip in enumerate(chips):
            for a in range(n):
                copy(a, 1 + j, (*chip, c), me).wait_recv()
                copy(a, 4 + j, (*chip, c), sibling).start()

    def finish():
        for a in range(n):
            copy(a, 0, sibling, me).wait_recv()
            for j, chip in enumerate(chips):
                copy(a, 4 + j, (*chip, 1 - c), me).wait_recv()
        for a in range(n):
            for cp in first(a) + [copy(a, 4 + j, (*chip, c), sibling) for j, chip in enumerate(chips)]:
                cp.wait_send()
            local(a).wait()

    return start, relay, finish


_ALL_K = tuple(range(N_DEV))


def _a2a_phases(x_refs, out_refs, send_sems, recv_sems, local_sems, ks_list=None):
    n = len(x_refs)
    ks_list = ks_list or [_ALL_K] * n
    x, y, c = lax.axis_index("x"), lax.axis_index("y"), lax.axis_index("c")
    my = 4 * x + 2 * y + c

    def copy(a, k):
        px, py, pc = (1 - x if k & 4 else x), (1 - y if k & 2 else y), (1 - c if k & 1 else c)
        return pltpu.make_async_remote_copy(
            src_ref=x_refs[a].at[4 * px + 2 * py + pc], dst_ref=out_refs[a].at[ks_list[a].index(k)],
            send_sem=send_sems.at[a * N_PEER + k - 1], recv_sem=recv_sems.at[a * N_PEER + k - 1],
            device_id=(px, py, pc), device_id_type=MESH)

    def local(a):
        return pltpu.make_async_copy(x_refs[a].at[my], out_refs[a].at[ks_list[a].index(0)], local_sems.at[a])

    def start():
        for a in range(n):
            for k in ks_list[a]:
                (copy(a, k) if k else local(a)).start()

    def relay():
        pass

    def finish():
        for a in range(n):
            for k in ks_list[a]:
                if k:
                    copy(a, k).wait_recv()
        for a in range(n):
            for k in ks_list[a]:
                if k:
                    copy(a, k).wait_send()
                else:
                    local(a).wait()

    return start, relay, finish


def _exchange_out_shapes(kind, arrays, ks_list=None):
    if kind == "ag":
        return [jax.ShapeDtypeStruct((N_DEV,) + a.shape, a.dtype) for a in arrays]
    ks_list = ks_list or [_ALL_K] * len(arrays)
    return [jax.ShapeDtypeStruct((len(ks),) + a.shape[1:], a.dtype) for a, ks in zip(arrays, ks_list)]


def _exchange(kind, arrays, name):
    n = len(arrays)
    phases = _ag_phases if kind == "ag" else _a2a_phases

    def body(*refs):
        start, relay, finish = phases(refs[:n], refs[n:2 * n], *refs[2 * n:])
        start()
        relay()
        finish()

    return pl.pallas_call(
        body, name=name,
        out_shape=_exchange_out_shapes(kind, arrays),
        in_specs=[_HBM_SPEC] * n, out_specs=[_HBM_SPEC] * n,
        scratch_shapes=_exchange_scratch(n),
    )(*arrays)


def _mm_tiles(m, n_unit, k_unit, tile_bytes, small_tiles_ok=True):
    fits = lambda tm, tn, tk: 2 * 2 * (tm * tk + tk * tn) + tile_bytes * tm * tn <= V7X_MM_VMEM_BUDGET
    for cap in (V7X_MM_TILE_MN, V7X_MM_TILE_MN_WHOLE_K) if small_tiles_ok else (V7X_MM_TILE_MN,):
        tm, tn = _tile(m, cap), _tile(n_unit, cap)
        if fits(tm, tn, k_unit) and (tn >= V7X_MM_TILE_MN_WHOLE_K or tn == n_unit):
            return tm, tn, k_unit
    tm, tn = _tile(m, V7X_MM_TILE_MN), _tile(n_unit, V7X_MM_TILE_MN)
    for tk in _divisors_down(k_unit, V7X_MM_TILE_K):
        if fits(tm, tn, tk):
            return tm, tn, tk
    return tm, tn, _tile(k_unit, LANE)


def _carry_parts(carry):
    kind, arrays, ks_list = (tuple(carry) + (None,))[:3] if carry is not None else (None, [], None)
    n = len(arrays)
    kind = (kind, ks_list)
    return kind, arrays, [_HBM_SPEC] * n, _exchange_out_shapes(kind[0], arrays, ks_list), (_exchange_scratch(n) if n else [])


def _carry_hooks(kind, x_refs, out_refs, sems, step, last_step):
    if not x_refs:
        return lambda: None
    kind, ks_list = kind
    if kind == "ag":
        start, relay, finish = _ag_phases(x_refs, out_refs, *sems)
    else:
        start, relay, finish = _a2a_phases(x_refs, out_refs, *sems, ks_list=ks_list)
    pl.when(step == 0)(start)

    def after():
        if kind == "ag":
            pl.when(step == (last_step * 7) // 8)(relay)
        pl.when(step == last_step)(finish)

    return after


def _mm(a, b, *, ta=False, tb=False, out_dtype=F32, res=None, carry=None, a_slots=False, b_slots=False, b_group=0,
        out_slots=0, epi=None, name):
    if a_slots:
        assert not ta
        a_n, m, a_c = a.shape
        k = a_n * a_c
    else:
        m, k = (a.shape[1], a.shape[0]) if ta else a.shape
    if b_slots:
        b_n, b_r, b_c = b.shape
        k2, n = (b_n * b_c, b_r) if tb else (b_r, b_n * b_c)
    else:
        k2, n = (b.shape[1], b.shape[0]) if tb else b.shape
    assert k == k2, (a.shape, b.shape, ta, tb)
    has_res = res is not None
    assert not (has_res and (out_slots or epi))
    n_units = [n] + ([n // out_slots] if out_slots else []) + ([b_c] if b_slots and not tb else [])
    k_units = [k] + ([a_c] if a_slots else []) + ([b_c] if b_slots and tb else [])
    n_unit, k_unit = min(n_units), min(k_units)
    assert all(u % n_unit == 0 for u in n_units) and all(u % k_unit == 0 for u in k_units)
    epi_fn, epi_ins, epi_outs = epi if epi is not None else (None, [], [])
    tile_bytes = 4 + (2 * res.dtype.itemsize if has_res else 0)
    tile_bytes += sum(2 * e.shape[0] * e.dtype.itemsize for e in epi_ins)
    tile_bytes += sum(2 * l * jnp.dtype(dt).itemsize for l, dt in epi_outs) if epi else 2 * jnp.dtype(out_dtype).itemsize
    tm, tn, tk = _mm_tiles(m, n_unit, k_unit, tile_bytes, small_tiles_ok=not epi)
    if b_group:
        tk = b_group * b_c
        assert b_slots and tb and k % tk == 0 and (not a_slots or a_c % tk == 0)
    ni, nj, nk = m // tm, n // tn, k // tk
    dims = (((0,) if ta else (1,), (1,) if tb else (0,)), ((), ()))

    def slot_map(per, pos):
        if pos == "k_cols":
            return lambda i, j, kk: (kk // per, i, kk % per)
        if pos == "k_cols_j":
            return lambda i, j, kk: (kk // per, j, kk % per)
        if pos == "n_cols_k":
            return lambda i, j, kk: (j // per, kk, j % per)
        return lambda i, j, kk: (j // per, i, j % per)

    if a_slots:
        a_spec = pl.BlockSpec((None, tm, tk), slot_map(a_c // tk, "k_cols"))
    else:
        a_spec = pl.BlockSpec((tk, tm), lambda i, j, kk: (kk, i)) if ta else pl.BlockSpec((tm, tk), lambda i, j, kk: (i, kk))
    if b_group:
        b_spec = pl.BlockSpec((b_group, tn, b_c), lambda i, j, kk: (kk, j, 0))
    elif b_slots and tb:
        b_spec = pl.BlockSpec((None, tn, tk), slot_map(b_c // tk, "k_cols_j"))
    elif b_slots:
        b_spec = pl.BlockSpec((None, tk, tn), slot_map(b_c // tn, "n_cols_k"))
    else:
        b_spec = pl.BlockSpec((tn, tk), lambda i, j, kk: (j, kk)) if tb else pl.BlockSpec((tk, tn), lambda i, j, kk: (kk, j))
    if epi:
        lead_spec = lambda l: pl.BlockSpec((l, tm, tn), lambda i, j, kk: (0, i, j))
        o_specs = [lead_spec(l) for l, _ in epi_outs]
        o_shapes = [jax.ShapeDtypeStruct((l, m, n), dt) for l, dt in epi_outs]
    elif out_slots:
        o_specs = [pl.BlockSpec((None, tm, tn), slot_map((n // out_slots) // tn, "n_cols_i"))]
        o_shapes = [jax.ShapeDtypeStruct((out_slots, m, n // out_slots), out_dtype)]
    else:
        o_specs = [pl.BlockSpec((tm, tn), lambda i, j, kk: (i, j))]
        o_shapes = [jax.ShapeDtypeStruct((m, n), out_dtype)]
    extra_ins = ([res] if has_res else []) + list(epi_ins)
    extra_specs = ([o_specs[0]] if has_res else []) + [pl.BlockSpec((e.shape[0], tm, tn), lambda i, j, kk: (0, i, j)) for e in epi_ins]
    n_in, n_out = 2 + len(extra_ins), len(o_specs)
    c_kind, c_arrays, c_specs, c_shapes, c_scratch = _carry_parts(carry)
    nc = len(c_arrays)
    last_step = ni * nj * nk - 1

    def body(*refs):
        a_ref, b_ref = refs[0], refs[1]
        e_refs = refs[2:n_in]
        x_refs = refs[n_in:n_in + nc]
        o_refs = refs[n_in + nc:n_in + nc + n_out]
        out_refs = refs[n_in + nc + n_out:n_in + 2 * nc + n_out]
        scratch = refs[n_in + 2 * nc + n_out:]
        acc = scratch[0] if nk > 1 else None
        kk = pl.program_id(2)
        step = (pl.program_id(0) * nj + pl.program_id(1)) * nk + kk
        after = _carry_hooks(c_kind, x_refs, out_refs, scratch[-3:], step, last_step)

        def emit(val):
            if has_res:
                val = val + e_refs[0][...].astype(F32)
            if epi:
                for o_ref, parts in zip(o_refs, epi_fn(val, *[e[...] for e in e_refs])):
                    for l, v in enumerate(parts):
                        o_ref[l] = v.astype(o_ref.dtype)
            else:
                o_refs[0][...] = val.astype(out_dtype)

        if epi and nk == 1 and not ta:
            rc = tm // V7X_EPI_ROW_CHUNKS
            for r in range(V7X_EPI_ROW_CHUNKS):
                rows = slice(r * rc, (r + 1) * rc)
                val = lax.dot_general(a_ref[rows, :], b_ref[...], dims, preferred_element_type=F32)
                for o_ref, parts in zip(o_refs, epi_fn(val, *[e[:, rows, :] for e in e_refs])):
                    for l, v in enumerate(parts):
                        o_ref[l, rows, :] = v.astype(o_ref.dtype)
            after()
            return
        if b_group:
            part = sum(lax.dot_general(a_ref[:, s * b_c:(s + 1) * b_c], b_ref[s], dims, preferred_element_type=F32)
                       for s in range(b_group))
        else:
            part = lax.dot_general(a_ref[...], b_ref[...], dims, preferred_element_type=F32)
        if nk == 1:
            emit(part)
        else:
            @pl.when(kk == 0)
            def _():
                acc[...] = part

            @pl.when(kk > 0)
            def _():
                acc[...] += part

            @pl.when(kk == nk - 1)
            def _():
                emit(acc[...])

        after()

    sem = ("arbitrary",) * 3 if nc else ("parallel", "parallel", "arbitrary")
    outs = pl.pallas_call(
        body, name=name,
        grid=(ni, nj, nk),
        in_specs=[a_spec, b_spec] + extra_specs + c_specs,
        out_specs=o_specs + c_specs,
        out_shape=o_shapes + c_shapes,
        scratch_shapes=([pltpu.VMEM((tm, tn), F32)] if nk > 1 else []) + c_scratch,
        compiler_params=_cparams(*sem),
    )(a, b, *extra_ins, *c_arrays)
    main = list(outs[:n_out]) if epi else outs[0]
    return (main, list(outs[n_out:])) if nc else main


def _ffn_in_fused(h2, w_s, *, carry=None, name):
    t, d = h2.shape
    n_slot, _, c = w_s.shape
    half = n_slot // 2
    tm = _tile(t, 512)
    c_kind, c_arrays, c_specs, c_shapes, c_scratch = _carry_parts(carry)
    nc = len(c_arrays)
    last_step = (t // tm) * half - 1

    def body(h_ref, wg_ref, wu_ref, *refs):
        x_refs, (gu_ref, act_ref), out_refs, sems = refs[:nc], refs[nc:nc + 2], refs[nc + 2:2 * nc + 2], refs[2 * nc + 2:]
        step = pl.program_id(0) * half + pl.program_id(1)
        after = _carry_hooks(c_kind, x_refs, out_refs, sems, step, last_step)
        h = h_ref[...]
        g = jnp.dot(h, wg_ref[...], preferred_element_type=F32)
        u = jnp.dot(h, wu_ref[...], preferred_element_type=F32)
        sg = _sigmoid(g)
        silu = g * sg
        gu_ref[0] = (u * (sg + silu - silu * sg)).astype(BF16)
        gu_ref[1] = silu.astype(BF16)
        act_ref[...] = (silu * u).astype(BF16)
        after()

    outs = pl.pallas_call(
        body, name=name,
        grid=(t // tm, half),
        in_specs=[pl.BlockSpec((tm, d), lambda i, j: (i, 0)),
                  pl.BlockSpec((None, d, c), lambda i, j: (j, 0, 0)),
                  pl.BlockSpec((None, d, c), lambda i, j: (half + j, 0, 0))] + c_specs,
        out_specs=[pl.BlockSpec((2, tm, c), lambda i, j: (0, i, j)), pl.BlockSpec((tm, c), lambda i, j: (i, j))] + c_specs,
        out_shape=[jax.ShapeDtypeStruct((2, t, half * c), BF16), jax.ShapeDtypeStruct((t, half * c), BF16)] + c_shapes,
        scratch_shapes=c_scratch,
        compiler_params=_cparams(*(("arbitrary",) * 2 if nc else ("parallel", "parallel"))),
    )(h2, w_s, w_s, *c_arrays)
    return (outs[0], outs[1], list(outs[2:])) if nc else (outs[0], outs[1])


def _rowwise(fn, ins, row_outs, acc_outs, *, rows, tb, name, carry=None):
    in_specs, args = [], []
    for spec in ins:
        kind, arr = spec[0], spec[1]
        if kind == "row":
            in_specs.append(pl.BlockSpec((tb, arr.shape[1]), lambda i: (i, 0)))
        elif kind == "win":
            width, cb = spec[2], spec[3]
            in_specs.append(pl.BlockSpec((tb, width), functools.partial(lambda i, cb: (i, cb), cb=cb)))
        else:
            in_specs.append(pl.BlockSpec(arr.shape, lambda i: (0, 0)))
        args.append(arr)
    out_specs = [pl.BlockSpec((tb, c), lambda i: (i, 0)) for c, _ in row_outs]
    out_specs += [pl.BlockSpec(shape, lambda i: (0, 0)) for shape in acc_outs]
    out_shape = [jax.ShapeDtypeStruct((rows, c), dt) for c, dt in row_outs]
    out_shape += [jax.ShapeDtypeStruct(shape, F32) for shape in acc_outs]
    n_in, n_row, n_out = len(ins), len(row_outs), len(row_outs) + len(acc_outs)
    c_kind, c_arrays, c_specs, c_shapes, c_scratch = _carry_parts(carry)
    nc = len(c_arrays)

    def body(*refs):
        after = _carry_hooks(c_kind, refs[n_in:n_in + nc], refs[n_in + nc + n_out:n_in + 2 * nc + n_out],
                             refs[n_in + 2 * nc + n_out:], pl.program_id(0), rows // tb - 1)
        vals = [r[...] for r in refs[:n_in]]
        outs = fn(*vals)
        if not isinstance(outs, (tuple, list)):
            outs = (outs,)
        out_refs = refs[n_in + nc:n_in + nc + n_out]
        for o_ref, val in zip(out_refs[:n_row], outs[:n_row]):
            o_ref[...] = val.astype(o_ref.dtype)
        first = pl.program_id(0) == 0
        for o_ref, val in zip(out_refs[n_row:], outs[n_row:]):
            @pl.when(first)
            def _(o_ref=o_ref):
                o_ref[...] = jnp.zeros_like(o_ref)
            o_ref[...] += val
        after()

    res = pl.pallas_call(
        body, name=name,
        grid=(rows // tb,),
        in_specs=in_specs + c_specs, out_specs=out_specs + c_specs, out_shape=out_shape + c_shapes,
        scratch_shapes=c_scratch,
        compiler_params=_cparams("arbitrary"),
    )(*args, *c_arrays)
    return (list(res[:n_out]), list(res[n_out:])) if nc else res


def _rms_fwd(x, g, name, carry=None):
    def fn(xv, gv):
        r = lax.rsqrt(jnp.mean(xv * xv, axis=-1, keepdims=True) + NORM_EPS)
        return (xv * r * gv,)
    res = _rowwise(fn, [("row", x), ("full", g)], [(x.shape[1], BF16)], [], rows=x.shape[0], tb=_tile(x.shape[0], 512),
                   name=name, carry=carry)
    return (res[0][0], res[1]) if carry is not None else res[0]


def _rms_bwd(x, g, dh, dres, name, want_bf16, carry=None):
    d = x.shape[1]

    def fn(xv, gv, dhv, drv):
        r = lax.rsqrt(jnp.mean(xv * xv, axis=-1, keepdims=True) + NORM_EPS)
        xhat = xv * r
        dhv = dhv.astype(F32)
        dxhat = dhv * gv
        dx = drv + r * (dxhat - xhat * jnp.mean(dxhat * xhat, axis=-1, keepdims=True))
        dg = jnp.sum(dhv * xhat, axis=0, keepdims=True)
        return (dx, dx, dg) if want_bf16 else (dx, dg)

    row_outs = [(d, F32), (d, BF16)] if want_bf16 else [(d, F32)]
    return _rowwise(fn, [("row", x), ("full", g), ("row", dh), ("row", dres)], row_outs, [(1, d)],
                    rows=x.shape[0], tb=_tile(x.shape[0], 256), name=name, carry=carry)


def _loss_head(x2, g, target, name):
    d = x2.shape[1]

    def fn(xv, gv, tv):
        r = lax.rsqrt(jnp.mean(xv * xv, axis=-1, keepdims=True) + NORM_EPS)
        xhat = xv * r
        diff = xhat * gv - tv
        loss = 0.5 * jnp.sum(jnp.mean(diff * diff, axis=-1, keepdims=True), axis=0, keepdims=True)
        dy = diff * (1.0 / d)
        dxhat = dy * gv
        dx = r * (dxhat - xhat * jnp.mean(dxhat * xhat, axis=-1, keepdims=True))
        dg = jnp.sum(dy * xhat, axis=0, keepdims=True)
        return dx, dx, dg, jnp.broadcast_to(loss, (1, LANE))

    return _rowwise(fn, [("row", x2), ("full", g), ("row", target)], [(d, F32), (d, BF16)], [(1, d), (1, LANE)],
                    rows=x2.shape[0], tb=_tile(x2.shape[0], 256), name=name)


def _swiglu_bwd_tile(dact, dswiglu):
    return ((dact * dswiglu[0].astype(F32), dact * dswiglu[1].astype(F32)),)


def _mix_fwd(proj, pa, pb, d, name):
    def fn(ga, gb, av, bv):
        return (_sigmoid(ga.astype(F32)) * av.astype(F32) + _sigmoid(gb.astype(F32)) * bv.astype(F32),)
    return _rowwise(fn, [("win", proj, d, 0), ("win", proj, d, 1), ("row", pa), ("row", pb)], [(d, BF16)], [],
                    rows=pa.shape[0], tb=_tile(pa.shape[0], 512), name=name)[0]


def _mix_bwd(proj, pa, pb, dmix, d, name):
    def fn(ga, gb, av, bv, dm):
        dm = dm.astype(F32)
        sa, sb = _sigmoid(ga.astype(F32)), _sigmoid(gb.astype(F32))
        av, bv = av.astype(F32), bv.astype(F32)
        return dm * sa, dm * sb, dm * av * sa * (1.0 - sa), dm * bv * sb * (1.0 - sb)
    return _rowwise(fn, [("win", proj, d, 0), ("win", proj, d, 1), ("row", pa), ("row", pb), ("row", dmix)],
                    [(d, BF16)] * 4, [], rows=pa.shape[0], tb=_tile(pa.shape[0], 512), name=name)


def _chunk_masks(tb):
    r = lax.broadcasted_iota(jnp.int32, (tb, tb), 0)
    c = lax.broadcasted_iota(jnp.int32, (tb, tb), 1)
    same = lax.shift_right_logical(r, GLA_CHUNK_SHIFT) == lax.shift_right_logical(c, GLA_CHUNK_SHIFT)
    return same, same & (c <= r), same & (r <= c)


def _mask_bf16(mask):
    return jnp.where(mask, 1.0, 0.0).astype(BF16)


def _split_dot(mask_bf, x, terms):
    acc, rem = None, x
    for _ in range(terms):
        hi = rem.astype(BF16)
        part = jnp.dot(mask_bf, hi, preferred_element_type=F32)
        acc = part if acc is None else acc + part
        rem = rem - hi.astype(F32)
    return acc


def _gla_decay(al, wa2, ba2, same_bf, causal_bf):
    z = jnp.dot(al.astype(BF16), wa2, preferred_element_type=F32) + ba2
    la = (jnp.minimum(z, 0.0) - jnp.log(1.0 + jnp.exp(-jnp.abs(z)))) * (1.0 / GLA_TAU)
    bc = _split_dot(causal_bf, la, 3)
    bl = _split_dot(same_bf, la, 3)
    return z, bc, bl


def _dot_t(a, b, ca, cb):
    return lax.dot_general(a, b, (((ca,), (cb,)), ((), ())), preferred_element_type=F32)


def _gla_fwd(proj, alow, wa2, ba2, ghn, *, dk, dv, name):
    t = proj.shape[0]
    tb = min(GLA_BLOCK, t)
    nch = tb // GLA_CHUNK
    hk, hv = dk // GLA_HEADS, dv // GLA_HEADS
    scale = hk ** -0.5
    v_cb, r_cb = (8 * dk) // dv, (8 * dk) // dv + 1
    q_cb, k_cb = (8 * dk + 2 * dv) // dk, (8 * dk + 2 * dv) // dk + 1

    def body(q_ref, k_ref, v_ref, r_ref, al_ref, wa2_ref, ba2_ref, ghn_ref, oa_ref, opre_ref, s_ref, st_scr):
        @pl.when(pl.program_id(0) == 0)
        def _():
            st_scr[...] = jnp.zeros_like(st_scr)

        same, causal, _ = _chunk_masks(tb)
        same_bf, causal_bf = _mask_bf16(same), _mask_bf16(causal)
        _, bc, bl = _gla_decay(al_ref[...], wa2_ref[...], ba2_ref[...], same_bf, causal_bf)
        q = q_ref[...].astype(F32) * scale
        k = k_ref[...].astype(F32)
        qd = (q * jnp.exp(bc)).astype(BF16)
        ki = (k * jnp.exp(-bc)).astype(BF16)
        ks = (k * jnp.exp(bl - bc)).astype(BF16)
        dl = jnp.exp(bl)
        ksls = [slice(h * hk, (h + 1) * hk) for h in range(GLA_HEADS)]
        vsls = [slice(h * hv, (h + 1) * hv) for h in range(GLA_HEADS)]
        v_hs = [v_ref[:, vsl] for vsl in vsls]
        o_intras = []
        for ksl, v_h in zip(ksls, v_hs):
            sc = jnp.where(causal, _dot_t(qd[:, ksl], ki[:, ksl], 1, 1), 0.0)
            o_intras.append(jnp.dot(sc.astype(BF16), v_h, preferred_element_type=F32))
        for c in range(nch):
            rows = slice(c * GLA_CHUNK, (c + 1) * GLA_CHUNK)
            for h, (ksl, vsl) in enumerate(zip(ksls, vsls)):
                st = st_scr[h]
                s_ref[c, h] = st
                opre_ref[rows, vsl] = o_intras[h][rows] + _dot_t(qd[rows, ksl], st.astype(BF16), 1, 1)
                st_scr[h] = dl[c * GLA_CHUNK:c * GLA_CHUNK + 1, ksl] * st + _dot_t(v_hs[h][rows], ks[rows, ksl], 0, 0)
        for h in range(GLA_HEADS):
            vsl = slice(h * hv, (h + 1) * hv)
            o = opre_ref[:, vsl]
            rs = lax.rsqrt(jnp.mean(o * o, axis=-1, keepdims=True) + NORM_EPS)
            rv = r_ref[:, vsl].astype(F32)
            oa_ref[:, vsl] = (rv * _sigmoid(rv) * (o * rs * ghn_ref[:, vsl])).astype(BF16)

    nchunks = t // GLA_CHUNK
    return pl.pallas_call(
        body, name=name,
        grid=(t // tb,),
        in_specs=[
            pl.BlockSpec((tb, dk), lambda i: (i, q_cb)),
            pl.BlockSpec((tb, dk), lambda i: (i, k_cb)),
            pl.BlockSpec((tb, dv), lambda i: (i, v_cb)),
            pl.BlockSpec((tb, dv), lambda i: (i, r_cb)),
            pl.BlockSpec((tb, LANE), lambda i: (i, 0)),
            pl.BlockSpec(wa2.shape, lambda i: (0, 0)),
            pl.BlockSpec(ba2.shape, lambda i: (0, 0)),
            pl.BlockSpec(ghn.shape, lambda i: (0, 0)),
        ],
        out_specs=[
            pl.BlockSpec((tb, dv), lambda i: (i, 0)),
            pl.BlockSpec((tb, dv), lambda i: (i, 0)),
            pl.BlockSpec((nch, GLA_HEADS, hv, hk), lambda i: (i, 0, 0, 0)),
        ],
        out_shape=[
            jax.ShapeDtypeStruct((t, dv), BF16),
            jax.ShapeDtypeStruct((t, dv), F32),
            jax.ShapeDtypeStruct((nchunks, GLA_HEADS, hv, hk), F32),
        ],
        scratch_shapes=[pltpu.VMEM((GLA_HEADS, hv, hk), F32)],
        compiler_params=_cparams("arbitrary"),
    )(proj, proj, proj, proj, alow, wa2, ba2, ghn)


def _gla_bwd(proj, alow, wa2, ba2, ghn, opre, states, doa, *, dk, dv, name):
    t = proj.shape[0]
    tb = min(GLA_BLOCK, t)
    nb = t // tb
    nch = tb // GLA_CHUNK
    hk, hv = dk // GLA_HEADS, dv // GLA_HEADS
    scale = hk ** -0.5
    v_cb, r_cb = (8 * dk) // dv, (8 * dk) // dv + 1
    q_cb, k_cb = (8 * dk + 2 * dv) // dk, (8 * dk + 2 * dv) // dk + 1

    def body(q_ref, k_ref, v_ref, r_ref, al_ref, wa2_ref, ba2_ref, ghn_ref, opre_ref, s_ref, doa_ref,
             dq_ref, dk_ref, dv_ref, dr_ref, dal_ref, dwa2_ref, dba2_ref, dghn_ref,
             dst_scr, dqd_scr, dki_scr, dks_scr, ddl_scr):
        @pl.when(pl.program_id(0) == 0)
        def _():
            dst_scr[...] = jnp.zeros_like(dst_scr)
            dwa2_ref[...] = jnp.zeros_like(dwa2_ref)
            dba2_ref[...] = jnp.zeros_like(dba2_ref)
            dghn_ref[...] = jnp.zeros_like(dghn_ref)

        same, causal, anti = _chunk_masks(tb)
        same_bf, causal_bf, anti_bf = _mask_bf16(same), _mask_bf16(causal), _mask_bf16(anti)
        al = al_ref[...]
        wa2v = wa2_ref[...]
        z, bc, bl = _gla_decay(al, wa2v, ba2_ref[...], same_bf, causal_bf)
        e_bc, e_nbc, e_st = jnp.exp(bc), jnp.exp(-bc), jnp.exp(bl - bc)
        q = q_ref[...].astype(F32) * scale
        k = k_ref[...].astype(F32)
        qd_f, ki_f, ks_f = q * e_bc, k * e_nbc, k * e_st
        qd, ki, ks = qd_f.astype(BF16), ki_f.astype(BF16), ks_f.astype(BF16)
        dl = jnp.exp(bl)
        per_head = []
        for h in range(GLA_HEADS):
            ksl = slice(h * hk, (h + 1) * hk)
            vsl = slice(h * hv, (h + 1) * hv)
            o = opre_ref[:, vsl]
            rs = lax.rsqrt(jnp.mean(o * o, axis=-1, keepdims=True) + NORM_EPS)
            ohat = o * rs
            g_h = ghn_ref[:, vsl]
            rv = r_ref[:, vsl].astype(F32)
            sg = _sigmoid(rv)
            d_oa = doa_ref[:, vsl].astype(F32)
            don = d_oa * (rv * sg)
            dr_ref[:, vsl] = (d_oa * (ohat * g_h) * (sg * (1.0 + rv * (1.0 - sg)))).astype(BF16)
            dghn_ref[:, vsl] += jnp.sum(don * ohat, axis=0, keepdims=True)
            dohat = don * g_h
            do_f = rs * (dohat - ohat * jnp.mean(dohat * ohat, axis=-1, keepdims=True))
            do = do_f.astype(BF16)
            v_h = v_ref[:, vsl]
            p = jnp.where(causal, _dot_t(do, v_h, 1, 1), 0.0).astype(BF16)
            dqd_intra = jnp.dot(p, ki[:, ksl], preferred_element_type=F32)
            dki_scr[:, ksl] = _dot_t(p, qd[:, ksl], 0, 0)
            sc = jnp.where(causal, _dot_t(qd[:, ksl], ki[:, ksl], 1, 1), 0.0).astype(BF16)
            dv_intra = _dot_t(sc, do, 0, 0)
            per_head.append((ksl, vsl, v_h, do, dqd_intra, dv_intra))
        for c in reversed(range(nch)):
            rows = slice(c * GLA_CHUNK, (c + 1) * GLA_CHUNK)
            for h, (ksl, vsl, v_h, do, dqd_intra, dv_intra) in enumerate(per_head):
                dst = dst_scr[h]
                st = s_ref[c, h]
                dst_bf = dst.astype(BF16)
                dv_ref[rows, vsl] = (dv_intra[rows] + _dot_t(ks[rows, ksl], dst_bf, 1, 1)).astype(BF16)
                dks_scr[rows, ksl] = jnp.dot(v_h[rows], dst_bf, preferred_element_type=F32)
                dl_c = dl[c * GLA_CHUNK:c * GLA_CHUNK + 1, ksl]
                ddl = jnp.sum(dst * st, axis=0, keepdims=True) * dl_c
                ddl_scr[rows, ksl] = jnp.broadcast_to(ddl, (GLA_CHUNK, hk))
                dqd_scr[rows, ksl] = dqd_intra[rows] + jnp.dot(do[rows], st.astype(BF16), preferred_element_type=F32)
                dst_scr[h] = dl_c * dst + _dot_t(do[rows], qd[rows, ksl], 0, 0)
        dqd, dki, dks = dqd_scr[...], dki_scr[...], dks_scr[...]
        dq_ref[...] = (dqd * (scale * e_bc)).astype(BF16)
        dk_ref[...] = (dki * e_nbc + dks * e_st).astype(BF16)
        dks_ks = dks * ks_f
        dbc = dqd * qd_f - dki * ki_f - dks_ks
        dla = _split_dot(anti_bf, dbc, 2) + _split_dot(same_bf, dks_ks, 2) + ddl_scr[...]
        dz = (dla * (1.0 / GLA_TAU) * (1.0 - _sigmoid(z)))
        dz_bf = dz.astype(BF16)
        dal_ref[...] = _dot_t(dz_bf, wa2v, 1, 1).astype(BF16)
        dwa2_ref[...] += _dot_t(al.astype(BF16), dz_bf, 0, 0)
        dba2_ref[...] += jnp.sum(dz, axis=0, keepdims=True)

    rev = lambda i: nb - 1 - i
    return pl.pallas_call(
        body, name=name,
        grid=(nb,),
        in_specs=[
            pl.BlockSpec((tb, dk), lambda i: (rev(i), q_cb)),
            pl.BlockSpec((tb, dk), lambda i: (rev(i), k_cb)),
            pl.BlockSpec((tb, dv), lambda i: (rev(i), v_cb)),
            pl.BlockSpec((tb, dv), lambda i: (rev(i), r_cb)),
            pl.BlockSpec((tb, LANE), lambda i: (rev(i), 0)),
            pl.BlockSpec(wa2.shape, lambda i: (0, 0)),
            pl.BlockSpec(ba2.shape, lambda i: (0, 0)),
            pl.BlockSpec(ghn.shape, lambda i: (0, 0)),
            pl.BlockSpec((tb, dv), lambda i: (rev(i), 0)),
            pl.BlockSpec((nch, GLA_HEADS, hv, hk), lambda i: (rev(i), 0, 0, 0)),
            pl.BlockSpec((tb, dv), lambda i: (rev(i), 0)),
        ],
        out_specs=[
            pl.BlockSpec((tb, dk), lambda i: (rev(i), 0)),
            pl.BlockSpec((tb, dk), lambda i: (rev(i), 0)),
            pl.BlockSpec((tb, dv), lambda i: (rev(i), 0)),
            pl.BlockSpec((tb, dv), lambda i: (rev(i), 0)),
            pl.BlockSpec((tb, LANE), lambda i: (rev(i), 0)),
            pl.BlockSpec(wa2.shape, lambda i: (0, 0)),
            pl.BlockSpec(ba2.shape, lambda i: (0, 0)),
            pl.BlockSpec(ghn.shape, lambda i: (0, 0)),
        ],
        out_shape=[
            jax.ShapeDtypeStruct((t, dk), BF16),
            jax.ShapeDtypeStruct((t, dk), BF16),
            jax.ShapeDtypeStruct((t, dv), BF16),
            jax.ShapeDtypeStruct((t, dv), BF16),
            jax.ShapeDtypeStruct((t, LANE), BF16),
            jax.ShapeDtypeStruct(wa2.shape, F32),
            jax.ShapeDtypeStruct(ba2.shape, F32),
            jax.ShapeDtypeStruct(ghn.shape, F32),
        ],
        scratch_shapes=[pltpu.VMEM((GLA_HEADS, hv, hk), F32)] + [pltpu.VMEM((tb, dk), F32)] * 4,
        compiler_params=_cparams("arbitrary"),
    )(proj, proj, proj, proj, alow, wa2, ba2, ghn, opre, states, doa)


def _s5_tables(lam_re, lam_im, log_dt, b_re, b_im, c_re, c_im):
    hp = lax.Precision.HIGHEST
    g, p = lam_re.shape
    ln = S5_L
    dt = jnp.exp(log_dt)[:, None]
    lr, li = lam_re, lam_im
    mag = jnp.exp(lr * dt)
    ar, ai = mag * jnp.cos(li * dt), mag * jnp.sin(li * dt)
    den = lr * lr + li * li
    am1 = ar - 1.0
    f_re = ((am1 * lr + ai * li) / den)[..., None]
    f_im = ((ai * lr - am1 * li) / den)[..., None]
    bb_re = f_re * b_re - f_im * b_im
    bb_im = f_re * b_im + f_im * b_re
    j = jnp.arange(ln + 1, dtype=F32)[None, :, None]
    pm = jnp.exp(j * (lr * dt)[:, None, :])
    ang = j * (li * dt)[:, None, :]
    pw_re, pw_im = pm * jnp.cos(ang), pm * jnp.sin(ang)
    cp_re = c_re[:, None] * pw_re[:, :, None, :] - c_im[:, None] * pw_im[:, :, None, :]
    cp_im = c_re[:, None] * pw_im[:, :, None, :] + c_im[:, None] * pw_re[:, :, None, :]
    kj = (jnp.einsum("gjcp,gpd->gjcd", cp_re[:, :ln], bb_re, precision=hp)
          - jnp.einsum("gjcp,gpd->gjcd", cp_im[:, :ln], bb_im, precision=hp))
    s_i = jnp.arange(ln)[None, :, None]
    t_i = jnp.arange(ln)[None, None, :]
    j_i = jnp.arange(ln)[:, None, None]
    shift = (t_i - s_i == j_i).astype(F32)
    m = jnp.einsum("jst,gjcd->gsdtc", shift, kj, precision=hp).reshape(g, ln * S5_GC, ln * S5_GC)
    rp_re, rp_im = pw_re[:, ln - 1::-1], pw_im[:, ln - 1::-1]
    bbt_re, bbt_im = bb_re.transpose(0, 2, 1)[:, None], bb_im.transpose(0, 2, 1)[:, None]
    bst_re = rp_re[:, :, None, :] * bbt_re - rp_im[:, :, None, :] * bbt_im
    bst_im = rp_re[:, :, None, :] * bbt_im + rp_im[:, :, None, :] * bbt_re
    bst = jnp.concatenate([bst_re, bst_im], axis=-1).reshape(g, ln * S5_GC, 2 * p)
    cst = jnp.concatenate([cp_re[:, 1:].transpose(0, 3, 1, 2), -cp_im[:, 1:].transpose(0, 3, 1, 2)], axis=1)
    cst = cst.reshape(g, 2 * p, ln * S5_GC)
    a = jnp.stack([jnp.concatenate([pw_re[:, ln], pw_re[:, ln]], axis=-1),
                   jnp.concatenate([-pw_im[:, ln], pw_im[:, ln]], axis=-1)], axis=1)
    return m, bst, cst, a


def _state_scan(v, pr, pi, reverse):
    n = v.shape[0]
    half = v.shape[1] // 2
    row = lax.broadcasted_iota(jnp.int32, v.shape, 0)
    z, s = v, 1
    while s < n:
        if reverse:
            zs = jnp.where(row < n - s, pltpu.roll(z, n - s, 0), 0.0)
        else:
            zs = jnp.where(row >= s, pltpu.roll(z, s, 0), 0.0)
        z = z + zs * pr + pltpu.roll(zs, half, 1) * pi
        pr, pi = pr * pr - pi * pi, 2.0 * pr * pi
        s *= 2
    return z


def _s5_core_fwd(u_g, m, bst, cst, a, name):
    g, nc, w = u_g.shape
    p2 = bst.shape[2]

    def body(u_ref, m_ref, b_ref, c_ref, a_ref, y_ref, x_ref):
        u = u_ref[0]
        v = jnp.dot(u, b_ref[0], preferred_element_type=F32)
        z = _state_scan(v, a_ref[0, 0:1, :], a_ref[0, 1:2, :], reverse=False)
        row = lax.broadcasted_iota(jnp.int32, z.shape, 0)
        x = jnp.where(row >= 1, pltpu.roll(z, 1, 0), 0.0)
        x_ref[0] = x
        y_ref[0] = (jnp.dot(u, m_ref[0], preferred_element_type=F32)
                    + jnp.dot(x.astype(BF16), c_ref[0], preferred_element_type=F32))

    per_g = lambda shape: pl.BlockSpec((1,) + shape, lambda i: (i, 0, 0))
    return pl.pallas_call(
        body, name=name, grid=(g,),
        in_specs=[per_g((nc, w)), per_g((w, w)), per_g((w, p2)), per_g((p2, w)), per_g((2, p2))],
        out_specs=[per_g((nc, w)), per_g((nc, p2))],
        out_shape=[jax.ShapeDtypeStruct((g, nc, w), F32), jax.ShapeDtypeStruct((g, nc, p2), F32)],
        compiler_params=_cparams("parallel"),
    )(u_g, m, bst, cst, a)


def _s5_core_bwd(dy_g, u_g, x_g, m, bst, cst, a, name):
    g, nc, w = u_g.shape
    p2 = bst.shape[2]

    def body(dy_ref, u_ref, x_ref, m_ref, b_ref, c_ref, a_ref, du_ref, dm_ref, db_ref, dc_ref, da_ref):
        dy, u, x = dy_ref[0], u_ref[0], x_ref[0]
        gx = _dot_t(dy, c_ref[0], 1, 1)
        rtot = _state_scan(gx, a_ref[0, 0:1, :], -a_ref[0, 1:2, :], reverse=True)
        row = lax.broadcasted_iota(jnp.int32, rtot.shape, 0)
        dv = jnp.where(row < nc - 1, pltpu.roll(rtot, nc - 1, 0), 0.0)
        dv_bf = dv.astype(BF16)
        du_ref[0] = (_dot_t(dy, m_ref[0], 1, 1) + _dot_t(dv_bf, b_ref[0], 1, 1)).astype(BF16)
        dm_ref[0] = _dot_t(u, dy, 0, 0)
        dc_ref[0] = _dot_t(x.astype(BF16), dy, 0, 0)
        db_ref[0] = _dot_t(u, dv_bf, 0, 0)
        x_sw = pltpu.roll(x, p2 // 2, 1)
        da_ref[0, 0:1, :] = jnp.sum(dv * x, axis=0, keepdims=True)
        da_ref[0, 1:2, :] = jnp.sum(dv * x_sw, axis=0, keepdims=True)

    per_g = lambda shape: pl.BlockSpec((1,) + shape, lambda i: (i, 0, 0))
    return pl.pallas_call(
        body, name=name, grid=(g,),
        in_specs=[per_g((nc, w)), per_g((nc, w)), per_g((nc, p2)), per_g((w, w)), per_g((w, p2)), per_g((p2, w)), per_g((2, p2))],
        out_specs=[per_g((nc, w)), per_g((w, w)), per_g((w, p2)), per_g((p2, w)), per_g((2, p2))],
        out_shape=[jax.ShapeDtypeStruct((g, nc, w), BF16), jax.ShapeDtypeStruct((g, w, w), F32),
                   jax.ShapeDtypeStruct((g, w, p2), F32), jax.ShapeDtypeStruct((g, p2, w), F32),
                   jax.ShapeDtypeStruct((g, 2, p2), F32)],
        compiler_params=_cparams("parallel"),
    )(dy_g, u_g, x_g, m, bst, cst, a)


def _gelu_parts(y):
    inner = GELU_C * (y + GELU_A * y * y * y)
    th = jnp.tanh(inner)
    return th, 0.5 * y * (1.0 + th)


def _s5_post_fwd(y_raw, proj, u_cb, s5d, wglu, bglu, name):
    w = y_raw.shape[1]

    def fn(yr, u, dsk, wg, bg):
        y = yr + dsk * u.astype(F32)
        _, h = _gelu_parts(y)
        gl = jnp.dot(h.astype(BF16), wg, preferred_element_type=F32) + bg
        return (h * _sigmoid(gl),)

    return _rowwise(fn, [("row", y_raw), ("win", proj, w, u_cb), ("full", s5d), ("full", wglu), ("full", bglu)],
                    [(w, BF16)], [], rows=y_raw.shape[0], tb=_tile(y_raw.shape[0], 512), name=name)[0]


def _s5_post_bwd(y_raw, proj, u_cb, s5d, wglu, bglu, dob, name):
    w = y_raw.shape[1]

    def fn(yr, u, dsk, wg, bg, dov):
        u = u.astype(F32)
        dov = dov.astype(F32)
        y = yr + dsk * u
        th, h = _gelu_parts(y)
        h_bf = h.astype(BF16)
        gl = jnp.dot(h_bf, wg, preferred_element_type=F32) + bg
        sg = _sigmoid(gl)
        dgl = dov * h * sg * (1.0 - sg)
        dgl_bf = dgl.astype(BF16)
        dh = dov * sg + _dot_t(dgl_bf, wg, 1, 1)
        dgelu = 0.5 * (1.0 + th) + 0.5 * y * (1.0 - th * th) * GELU_C * (1.0 + 3.0 * GELU_A * y * y)
        dy = dh * dgelu
        return (dy, dy * dsk,
                _dot_t(h_bf, dgl_bf, 0, 0), jnp.sum(dgl, axis=0, keepdims=True), jnp.sum(dy * u, axis=0, keepdims=True))

    return _rowwise(fn, [("row", y_raw), ("win", proj, w, u_cb), ("full", s5d), ("full", wglu), ("full", bglu), ("row", dob)],
                    [(w, BF16), (w, BF16)], [(w, w), (1, w), (1, w)], rows=y_raw.shape[0], tb=_tile(y_raw.shape[0], 512), name=name)


def _to_groups(a, dtype):
    t, w = a.shape
    g = w // S5_GC
    return a.reshape(t // S5_L, S5_L, g, S5_GC).transpose(2, 0, 1, 3).reshape(g, t // S5_L, S5_L * S5_GC).astype(dtype)


def _from_groups(a):
    g, nc, _ = a.shape
    return a.reshape(g, nc, S5_L, S5_GC).transpose(1, 2, 0, 3).reshape(nc * S5_L, g * S5_GC)


def _adamw(w, g, m, v, name):
    _, rows, cols = w.shape
    tr, tc = (_tile(rows, 256, align=16), cols) if rows % 16 == 0 else (rows, _tile(cols, 256))
    slots = isinstance(g, (list, tuple))
    gs = list(g) if slots else [g]
    c1 = 1.0 - ADAM_B1 ** ADAM_STEP
    c2 = 1.0 - ADAM_B2 ** ADAM_STEP

    def body(w_ref, m_ref, v_ref, *refs):
        g_refs, out_refs = refs[:len(gs)], refs[len(gs):]
        if slots:
            parts = [g_ref[s].astype(F32) for g_ref in g_refs for s in range(g_ref.shape[0])]
            gv = parts[0]
            for p in parts[1:]:
                gv = gv + p
            out_refs[0][...] = gv
        else:
            gv = g_refs[0][...]
        d_ref, nm_ref, nv_ref = out_refs[-3:]
        nm = ADAM_B1 * m_ref[...] + (1.0 - ADAM_B1) * gv
        nv = ADAM_B2 * v_ref[...] + (1.0 - ADAM_B2) * (gv * gv)
        d_ref[...] = -ADAM_LR * ((nm / c1) / (jnp.sqrt(nv / c2) + ADAM_EPS) + ADAM_WD * w_ref[...])
        nm_ref[...] = nm
        nv_ref[...] = nv

    spec = pl.BlockSpec((None, tr, tc), lambda i, j: (0, i, j))
    g_specs = [pl.BlockSpec((a.shape[0], tr, tc), lambda i, j: (0, i, j)) for a in gs] if slots else [pl.BlockSpec((tr, tc), lambda i, j: (i, j))]
    n_out = 4 if slots else 3
    return pl.pallas_call(
        body, name=name, grid=(rows // tr, cols // tc),
        in_specs=[spec, spec, spec] + g_specs, out_specs=[spec] * n_out,
        out_shape=[jax.ShapeDtypeStruct((1, rows, cols), F32)] * n_out,
        compiler_params=_cparams("parallel", "parallel"),
    )(w, m, v, *gs)


def _slot_sum(x, name):
    _, rows, cols = x.shape
    if rows % 8 == 0:
        tr, tc = _tile(rows, 512, align=8), cols
    else:
        tr, tc = rows, _tile(cols, 256)

    def body(x_ref, o_ref):
        acc = x_ref[0].astype(F32)
        for s in range(1, N_DEV):
            acc = acc + x_ref[s].astype(F32)
        o_ref[...] = acc

    return pl.pallas_call(
        body, name=name, grid=(rows // tr, cols // tc),
        in_specs=[pl.BlockSpec((N_DEV, tr, tc), lambda i, j: (0, i, j))],
        out_specs=pl.BlockSpec((tr, tc), lambda i, j: (i, j)),
        out_shape=jax.ShapeDtypeStruct((rows, cols), F32),
        compiler_params=_cparams("parallel", "parallel"),
    )(x)


_REST = (("w_a2", 1), ("w_glu", 0), ("w_branch_a", 1), ("w_branch_b", 1), ("w_out", 0), ("w_ffn_in", 1), ("w_ffn_out", 0))
_SMALL = ("norm1_g", "b_a2", "gla_norm_g", "lam_re", "lam_im", "log_dt", "s5_b_re", "s5_b_im", "s5_c_re", "s5_c_im",
          "s5_d", "b_glu", "norm2_g", "final_norm_g")
_ORDER = ("norm1_g", "w_in", "w_a2", "b_a2", "gla_norm_g", "lam_re", "lam_im", "log_dt", "s5_b_re", "s5_b_im", "s5_c_re",
          "s5_c_im", "s5_d", "w_glu", "b_glu", "w_branch_a", "w_branch_b", "w_out", "norm2_g", "w_ffn_in", "w_ffn_out", "final_norm_g")


def _join_slots(slots, axis):
    _, r, c = slots.shape
    if axis == 0:
        return slots.reshape(N_DEV * r, c)
    return slots.transpose(1, 0, 2).reshape(r, N_DEV * c)


def _to_slots(full, axis):
    r, c = full.shape
    if axis == 0:
        return full.reshape(N_DEV, r // N_DEV, c)
    return full.reshape(r, N_DEV, c // N_DEV).transpose(1, 0, 2)


def _local_step(x, target, w_in_t, small, rest):
    t, d = x.shape
    dk, dv, s5w = d // 4, d // 2, d // 4
    dist = not isinstance(rest, dict)
    if dist:
        h1, (w_in_slots,) = _rms_fwd(x, small["norm1_g"], "norm1_fwd", carry=("ag", [w_in_t]))
        w_in_t = w_in_slots.reshape(-1, d)
    else:
        h1 = _rms_fwd(x, small["norm1_g"], "norm1_fwd")
    o_q, o_k, o_v, o_r, o_al = 0, dk, 2 * dk, 2 * dk + dv, 2 * dk + 2 * dv
    o_u = o_al + GLA_RANK
    o_ga, o_gb = o_u + s5w, o_u + s5w + d
    rows = lambda a, o, n: a[o:o + n]
    w_main_t = jnp.concatenate([rows(w_in_t, o_ga, d), rows(w_in_t, o_gb, d), rows(w_in_t, o_v, dv), rows(w_in_t, o_r, dv),
                                rows(w_in_t, o_q, dk), rows(w_in_t, o_k, dk), rows(w_in_t, o_u, s5w)], axis=0)
    w_al_t = jnp.pad(rows(w_in_t, o_al, GLA_RANK), ((0, LANE - GLA_RANK), (0, 0)))
    u_cb = (2 * d + 2 * dv + 2 * dk) // s5w

    if dist:
        proj, gathered = _mm(h1, w_main_t, tb=True, out_dtype=BF16, carry=("ag", rest[:-1]), name="in_proj")
        w = {n: _join_slots(g, ax) for (n, ax), g in zip(_REST[:-2], gathered[:-1])}
        w_ffn_in_s = gathered[-1]
    else:
        proj = _mm(h1, w_main_t, tb=True, out_dtype=BF16, name="in_proj")
        w = rest
        w_ffn_in_s = _to_slots(rest["w_ffn_in"], 1)
    wa2 = jnp.pad(w["w_a2"], ((0, LANE - GLA_RANK), (0, 0)))
    alow = _mm(h1, w_al_t, tb=True, out_dtype=BF16, name="in_proj_gate_rank")
    o_a, o_pre, states = _gla_fwd(proj, alow, wa2, small["b_a2"], small["gla_norm_g"], dk=dk, dv=dv, name="gla_fwd")

    s5_params = (small["lam_re"], small["lam_im"], small["log_dt"][0], small["s5_b_re"], small["s5_b_im"],
                 small["s5_c_re"], small["s5_c_im"])
    (tm, tbst, tcst, ta), tables_vjp = jax.vjp(_s5_tables, *s5_params)
    tm_bf, tbst_bf, tcst_bf = tm.astype(BF16), tbst.astype(BF16), tcst.astype(BF16)
    u_g = _to_groups(proj[:, u_cb * s5w:(u_cb + 1) * s5w], BF16)
    y_g, x_g = _s5_core_fwd(u_g, tm_bf, tbst_bf, tcst_bf, ta, "s5_core_fwd")
    y_raw = _from_groups(y_g)
    o_b = _s5_post_fwd(y_raw, proj, u_cb, small["s5_d"], w["w_glu"], small["b_glu"], "s5_post_fwd")

    pa = _mm(o_a, w["w_branch_a"], out_dtype=BF16, name="branch_a")
    pb = _mm(o_b, w["w_branch_b"], out_dtype=BF16, name="branch_b")
    mix = _mix_fwd(proj, pa, pb, d, "mix_fwd")
    x1 = _mm(mix, w["w_out"], res=x, name="out_proj")
    h2 = _rms_fwd(x1, small["norm2_g"], "norm2_fwd")
    if dist:
        gu, act, (w_ffn_out_s,) = _ffn_in_fused(h2, w_ffn_in_s, carry=("ag", rest[-1:]), name="ffn_in")
        w_ffn_out = _join_slots(w_ffn_out_s, 0)
    else:
        gu, act = _ffn_in_fused(h2, w_ffn_in_s, name="ffn_in")
        w_ffn_out = rest["w_ffn_out"]
    x2 = _mm(act, w_ffn_out, res=x1, name="ffn_out")
    dx2, dx2_bf, d_final_g, loss = _loss_head(x2, small["final_norm_g"], target, "loss_head")

    recv = {}
    dgu, = _mm(dx2_bf, w_ffn_out, tb=True, epi=(_swiglu_bwd_tile, [gu], [(2, BF16)]), name="d_act")
    g_ffn_out = _mm(act, dx2_bf, ta=True, out_dtype=BF16, name="g_w_ffn_out")
    if dist:
        g_ffn_in_s, recv["w_ffn_out"] = _mm(h2, dgu, ta=True, b_slots=True, out_dtype=BF16, out_slots=N_DEV,
                                            carry=("a2a", [_to_slots(g_ffn_out, 0)]), name="g_w_ffn_in")
        dh2, recv["w_ffn_in"] = _mm(dgu, w_ffn_in_s, tb=True, a_slots=True, b_slots=True, b_group=2,
                                    carry=("a2a", [g_ffn_in_s]), name="d_h2")
    else:
        g_ffn_in_s = _mm(h2, dgu, ta=True, b_slots=True, out_dtype=BF16, out_slots=N_DEV, name="g_w_ffn_in")
        dh2 = _mm(dgu, w_ffn_in_s, tb=True, a_slots=True, b_slots=True, b_group=2, name="d_h2")
    dx1, dx1_bf, d_norm2_g = _rms_bwd(x1, small["norm2_g"], dh2, dx2, "norm2_bwd", True)
    dmix = _mm(dx1_bf, w["w_out"], tb=True, out_dtype=BF16, name="d_mix")
    g_out = _mm(mix, dx1_bf, ta=True, out_dtype=BF16, name="g_w_out")
    dpa, dpb, dga, dgb = _mix_bwd(proj, pa, pb, dmix, d, "mix_bwd")
    doa = _mm(dpa, w["w_branch_a"], tb=True, out_dtype=BF16, name="d_o_a")
    dob = _mm(dpb, w["w_branch_b"], tb=True, out_dtype=BF16, name="d_o_b")
    g_branch_a = _mm(o_a, dpa, ta=True, out_dtype=BF16, name="g_w_branch_a")
    g_branch_b = _mm(o_b, dpb, ta=True, out_dtype=BF16, name="g_w_branch_b")

    dy_s5, du_direct, g_glu, g_bglu, g_s5d = _s5_post_bwd(y_raw, proj, u_cb, small["s5_d"], w["w_glu"], small["b_glu"], dob, "s5_post_bwd")
    du_g, d_tm, d_tbst, d_tcst, d_ta = _s5_core_bwd(_to_groups(dy_s5, BF16), u_g, x_g, tm_bf, tbst_bf, tcst_bf, ta, "s5_core_bwd")
    g_lam_re, g_lam_im, g_log_dt, g_b_re, g_b_im, g_c_re, g_c_im = tables_vjp((d_tm, d_tbst, d_tcst, d_ta))
    du = _from_groups(du_g) + du_direct

    dq, dkk, dvv, dr, dal, g_wa2, g_ba2, g_ghn = _gla_bwd(proj, alow, wa2, small["b_a2"], small["gla_norm_g"], o_pre, states, doa,
                                                        dk=dk, dv=dv, name="gla_bwd")
    dproj = jnp.concatenate([dga, dgb, dvv, dr, dq, dkk, du], axis=1)
    mid = {"w_out": g_out, "w_branch_a": g_branch_a, "w_branch_b": g_branch_b, "w_glu": g_glu.astype(BF16),
           "w_a2": g_wa2[:GLA_RANK].astype(BF16)}
    if dist:
        axes = dict(_REST)
        g_main_t, got = _mm(dproj, h1, ta=True, out_dtype=BF16, name="g_w_in_main",
                            carry=("a2a", [_to_slots(mid[n], axes[n]) for n in mid]))
        recv.update(zip(mid, [[g] for g in got]))
    else:
        g_main_t = _mm(dproj, h1, ta=True, out_dtype=BF16, name="g_w_in_main")
    g_al_t = _mm(dal, h1, ta=True, out_dtype=BF16, name="g_w_in_gate_rank")
    mrows = lambda o, n: g_main_t[o:o + n]
    g_w_in_t = jnp.concatenate([mrows(2 * d + 2 * dv, dk), mrows(2 * d + 2 * dv + dk, dk), mrows(2 * d, dv), mrows(2 * d + dv, dv),
                                g_al_t[:GLA_RANK], mrows(2 * d + 2 * dv + 2 * dk, s5w), mrows(0, d), mrows(d, d)], axis=0)
    if dist:
        g_w_in_s = _to_slots(g_w_in_t, 0)
        dh1, (recv_w_in,) = _mm(dproj, w_main_t, carry=("a2a", [g_w_in_s], [_ALL_K[:-1]]), name="d_h1_main")
    else:
        dh1 = _mm(dproj, w_main_t, name="d_h1_main")
    dh1 = _mm(dal, w_al_t, res=dh1, name="d_h1_gate_rank")
    if dist:
        (grad_x, d_norm1_g), (recv_w_in_last,) = _rms_bwd(x, small["norm1_g"], dh1, dx1, "norm1_bwd", False,
                                                          carry=("a2a", [g_w_in_s], [_ALL_K[-1:]]))
        recv["w_in"] = [recv_w_in, recv_w_in_last]
    else:
        grad_x, d_norm1_g = _rms_bwd(x, small["norm1_g"], dh1, dx1, "norm1_bwd", False)

    small_g = {
        "norm1_g": d_norm1_g, "b_a2": g_ba2, "gla_norm_g": g_ghn, "lam_re": g_lam_re, "lam_im": g_lam_im,
        "log_dt": g_log_dt[None], "s5_b_re": g_b_re, "s5_b_im": g_b_im, "s5_c_re": g_c_re, "s5_c_im": g_c_im,
        "s5_d": g_s5d, "b_glu": g_bglu, "norm2_g": d_norm2_g, "final_norm_g": d_final_g,
    }
    if not dist:
        recv = dict(mid, w_in=g_w_in_t, w_ffn_in=_join_slots(g_ffn_in_s, 1), w_ffn_out=g_ffn_out)
    return loss[0, 0], grad_x, recv, small_g


def _small_2d(name, a):
    a = a[0]
    return a[None] if a.ndim == 1 else a


def kernel(x, norm1_g, w_in, w_a2, b_a2, gla_norm_g, lam_re, lam_im, log_dt, s5_b_re, s5_b_im, s5_c_re, s5_c_im, s5_d, w_glu, b_glu, w_branch_a, w_branch_b, w_out, norm2_g, w_ffn_in, w_ffn_out, final_norm_g, loss_target, m_norm1_g, m_w_in, m_w_a2, m_b_a2, m_gla_norm_g, m_lam_re, m_lam_im, m_log_dt, m_s5_b_re, m_s5_b_im, m_s5_c_re, m_s5_c_im, m_s5_d, m_w_glu, m_b_glu, m_w_branch_a, m_w_branch_b, m_w_out, m_norm2_g, m_w_ffn_in, m_w_ffn_out, m_final_norm_g, v_norm1_g, v_w_in, v_w_a2, v_b_a2, v_gla_norm_g, v_lam_re, v_lam_im, v_log_dt, v_s5_b_re, v_s5_b_im, v_s5_c_re, v_s5_c_im, v_s5_d, v_w_glu, v_b_glu, v_w_branch_a, v_w_branch_b, v_w_out, v_norm2_g, v_w_ffn_in, v_w_ffn_out, v_final_norm_g):
    args = dict(locals())
    weights = {n: args[n] for n in _ORDER}
    m_in = {n: args["m_" + n] for n in _ORDER}
    v_in = {n: args["v_" + n] for n in _ORDER}
    transposed = lambda a: a[0].T[None]
    rest = [weights[n][0].astype(BF16) for n, _ in _REST]
    small = {n: _small_2d(n, weights[n]) for n in _SMALL}
    loss_local, grad_x, recv, small_g = _local_step(x[0], loss_target[0], transposed(weights["w_in"])[0].astype(BF16), small, rest)

    grads, delta, new_m, new_v = {}, {}, {}, {}
    for n, _ in _REST:
        grads[n], delta[n], new_m[n], new_v[n] = _adamw(weights[n], recv[n], m_in[n], v_in[n], "adamw_" + n)
    w_in_out = _adamw(transposed(weights["w_in"]), recv["w_in"], transposed(m_in["w_in"]), transposed(v_in["w_in"]), "adamw_w_in")
    grads["w_in"], delta["w_in"], new_m["w_in"], new_v["w_in"] = (transposed(a) for a in w_in_out)

    s_sizes = [small_g[n].size for n in _SMALL]
    s_offs = [sum(s_sizes[:i]) for i in range(len(s_sizes))]
    s_total = sum(s_sizes)
    s_rows = -(-(-(-(s_total + 1) // LANE)) // LANE) * LANE

    def pack_small(parts):
        flat = jnp.concatenate([p.reshape(-1) for p in parts])
        return jnp.pad(flat, (0, s_rows * LANE - flat.size)).reshape(s_rows, LANE)

    s_flat = pack_small([small_g[n] for n in _SMALL] + [loss_local])
    s_red = _slot_sum(_exchange("ag", [s_flat], "small_grads_all_gather")[0], "small_grads_slot_sum")
    loss = s_red.reshape(-1)[s_total]
    sd, sm, sv = _adamw(pack_small([weights[n] for n in _SMALL])[None], s_red, pack_small([m_in[n] for n in _SMALL])[None],
                        pack_small([v_in[n] for n in _SMALL])[None], "adamw_small")
    sd, sm, sv = sd[0], sm[0], sv[0]
    for n, o, s in zip(_SMALL, s_offs, s_sizes):
        shape = weights[n].shape[1:]
        grads[n], delta[n], new_m[n], new_v[n] = (a.reshape(-1)[o:o + s].reshape(shape) for a in (s_red, sd, sm, sv))

    out = [loss, grad_x[None]]
    for tree in (grads, delta, new_m, new_v):
        out += [tree[n].reshape(weights[n].shape) for n in _ORDER]
    return tuple(out)
```

```python
import functools
import math

import jax
import jax.numpy as jnp
from jax import lax
from jax.experimental import pallas as pl
from jax.experimental.pallas import tpu as pltpu

F32 = jnp.float32
BF16 = jnp.bfloat16

NORM_EPS = 1e-6
N_DEV = 8
N_PEER = N_DEV - 1
GLA_HEADS = 4
GLA_CHUNK = 32
GLA_CHUNK_SHIFT = 5
GLA_TAU = 16.0
GLA_RANK = 16
GLA_BLOCK = 256
S5_GC = 16
S5_P = 64
S5_L = 16
S5_TILE_G = 8
S5_ROW_BLOCK = 2048
LANE = 128
V7X_VMEM_LIMIT = 56 * 1024 * 1024
V7X_MM_VMEM_BUDGET = 40 * 1024 * 1024
V7X_MM_TILE_MN = 1408
V7X_MM_TILE_MN_WHOLE_K = 512
V7X_MM_TILE_K = 2048
V7X_EPI_ROW_CHUNKS = 4

ADAM_LR = 0.001
ADAM_B1 = 0.9
ADAM_B2 = 0.999
ADAM_EPS = 1e-08
ADAM_WD = 0.01
ADAM_STEP = 10

GELU_C = math.sqrt(2.0 / math.pi)
GELU_A = 0.044715

MESH = pl.DeviceIdType.MESH


def _cparams(*sem):
    return pltpu.CompilerParams(dimension_semantics=sem, vmem_limit_bytes=V7X_VMEM_LIMIT)


def _divisors_down(n, start, align=LANE):
    t = (min(start, n) // align) * align
    found = False
    while t >= align:
        if n % t == 0:
            found = True
            yield t
        t -= align
    if not found:
        yield n


def _tile(n, target, align=LANE):
    return next(_divisors_down(n, target, align))


def _sigmoid(x):
    return 1.0 / (1.0 + jnp.exp(-x))


_HBM_SPEC = pl.BlockSpec(memory_space=pltpu.HBM)


def _exchange_scratch(n):
    return [pltpu.SemaphoreType.DMA((n * N_PEER,)), pltpu.SemaphoreType.DMA((n * N_PEER,)), pltpu.SemaphoreType.DMA((n,))]


def _ag_phases(x_refs, out_refs, send_sems, recv_sems, local_sems):
    n = len(x_refs)
    x, y, c = lax.axis_index("x"), lax.axis_index("y"), lax.axis_index("c")
    me, sibling = (x, y, c), (x, y, 1 - c)
    chips = [(1 - x, y), (x, 1 - y), (1 - x, 1 - y)]

    def copy(a, k, block, to, from_input=False):
        dst = out_refs[a].at[4 * block[0] + 2 * block[1] + block[2]]
        return pltpu.make_async_remote_copy(
            src_ref=x_refs[a] if from_input else dst, dst_ref=dst,
            send_sem=send_sems.at[a * N_PEER + k], recv_sem=recv_sems.at[a * N_PEER + k], device_id=to, device_id_type=MESH)

    def local(a):
        return pltpu.make_async_copy(x_refs[a], out_refs[a].at[4 * x + 2 * y + c], local_sems.at[a])

    def first(a):
        return [copy(a, 0, me, sibling, True)] + [copy(a, 1 + j, me, (*chip, c), True) for j, chip in enumerate(chips)]

    def start():
        for a in range(n):
            local(a).start()
            for cp in first(a):
                cp.start()

    def relay():
        for j, chip in enumerate(chips):
            for a in range(n):
                copy(a, 1 + j, (*chip, c), me).wait_recv()
                copy(a, 4 + j, (*chip, c), sibling).start()

    def finish():
        for a in range(n):
            copy(a, 0, sibling, me).wait_recv()
            for j, chip in enumerate(chips):
                copy(a, 4 + j, (*chip, 1 - c), me).wait_recv()
        for a in range(n):
            for cp in first(a) + [copy(a, 4 + j, (*chip, c), sibling) for j, chip in enumerate(chips)]:
                cp.wait_send()
            local(a).wait()

    return start, relay, finish


_ALL_K = tuple(range(N_DEV))


def _a2a_phases(x_refs, out_refs, send_sems, recv_sems, local_sems, ks_list=None):
    n = len(x_refs)
    ks_list = ks_list or [_ALL_K] * n
    x, y, c = lax.axis_index("x"), lax.axis_index("y"), lax.axis_index("c")
    my = 4 * x + 2 * y + c

    def copy(a, k):
        px, py, pc = (1 - x if k & 4 else x), (1 - y if k & 2 else y), (1 - c if k & 1 else c)
        return pltpu.make_async_remote_copy(
            src_ref=x_refs[a].at[4 * px + 2 * py + pc], dst_ref=out_refs[a].at[ks_list[a].index(k)],
            send_sem=send_sems.at[a * N_PEER + k - 1], recv_sem=recv_sems.at[a * N_PEER + k - 1],
            device_id=(px, py, pc), device_id_type=MESH)

    def local(a):
        return pltpu.make_async_copy(x_refs[a].at[my], out_refs[a].at[ks_list[a].index(0)], local_sems.at[a])

    def start():
        for a in range(n):
            for k in ks_list[a]:
                (copy(a, k) if k else local(a)).start()

    def relay():
        pass

    def finish():
        for a in range(n):
            for k in ks_list[a]:
                if k:
                    copy(a, k).wait_recv()
        for a in range(n):
            for k in ks_list[a]:
                if k:
                    copy(a, k).wait_send()
                else:
                    local(a).wait()

    return start, relay, finish


def _exchange_out_shapes(kind, arrays, ks_list=None):
    if kind == "ag":
        return [jax.ShapeDtypeStruct((N_DEV,) + a.shape, a.dtype) for a in arrays]
    ks_list = ks_list or [_ALL_K] * len(arrays)
    return [jax.ShapeDtypeStruct((len(ks),) + a.shape[1:], a.dtype) for a, ks in zip(arrays, ks_list)]


def _exchange(kind, arrays, name):
    n = len(arrays)
    phases = _ag_phases if kind == "ag" else _a2a_phases

    def body(*refs):
        start, relay, finish = phases(refs[:n], refs[n:2 * n], *refs[2 * n:])
        start()
        relay()
        finish()

    return pl.pallas_call(
        body, name=name,
        out_shape=_exchange_out_shapes(kind, arrays),
        in_specs=[_HBM_SPEC] * n, out_specs=[_HBM_SPEC] * n,
        scratch_shapes=_exchange_scratch(n),
    )(*arrays)


def _mm_tiles(m, n_unit, k_unit, tile_bytes, small_tiles_ok=True):
    fits = lambda tm, tn, tk: 2 * 2 * (tm * tk + tk * tn) + tile_bytes * tm * tn <= V7X_MM_VMEM_BUDGET
    for cap in (V7X_MM_TILE_MN, V7X_MM_TILE_MN_WHOLE_K) if small_tiles_ok else (V7X_MM_TILE_MN,):
        tm, tn = _tile(m, cap), _tile(n_unit, cap)
        if fits(tm, tn, k_unit) and (tn >= V7X_MM_TILE_MN_WHOLE_K or tn == n_unit):
            return tm, tn, k_unit
    tm, tn = _tile(m, V7X_MM_TILE_MN), _tile(n_unit, V7X_MM_TILE_MN)
    for tk in _divisors_down(k_unit, V7X_MM_TILE_K):
        if fits(tm, tn, tk):
            return tm, tn, tk
    return tm, tn, _tile(k_unit, LANE)


def _carry_parts(carry):
    kind, arrays, ks_list = (tuple(carry) + (None,))[:3] if carry is not None else (None, [], None)
    n = len(arrays)
    kind = (kind, ks_list)
    return kind, arrays, [_HBM_SPEC] * n, _exchange_out_shapes(kind[0], arrays, ks_list), (_exchange_scratch(n) if n else [])


def _carry_hooks(kind, x_refs, out_refs, sems, step, last_step):
    if not x_refs:
        return lambda: None
    kind, ks_list = kind
    if kind == "ag":
        start, relay, finish = _ag_phases(x_refs, out_refs, *sems)
    else:
        start, relay, finish = _a2a_phases(x_refs, out_refs, *sems, ks_list=ks_list)
    pl.when(step == 0)(start)

    def after():
        if kind == "ag":
            pl.when(step == (last_step * 7) // 8)(relay)
        pl.when(step == last_step)(finish)

    return after


def _mm(a, b, *, ta=False, tb=False, out_dtype=F32, res=None, carry=None, a_slots=False, b_slots=False, b_group=0,
        out_slots=0, epi=None, name):
    if a_slots:
        assert not ta
        a_n, m, a_c = a.shape
        k = a_n * a_c
    else:
        m, k = (a.shape[1], a.shape[0]) if ta else a.shape
    if b_slots:
        b_n, b_r, b_c = b.shape
        k2, n = (b_n * b_c, b_r) if tb else (b_r, b_n * b_c)
    else:
        k2, n = (b.shape[1], b.shape[0]) if tb else b.shape
    assert k == k2, (a.shape, b.shape, ta, tb)
    has_res = res is not None
    assert not (has_res and (out_slots or epi))
    n_units = [n] + ([n // out_slots] if out_slots else []) + ([b_c] if b_slots and not tb else [])
    k_units = [k] + ([a_c] if a_slots else []) + ([b_c] if b_slots and tb else [])
    n_unit, k_unit = min(n_units), min(k_units)
    assert all(u % n_unit == 0 for u in n_units) and all(u % k_unit == 0 for u in k_units)
    epi_fn, epi_ins, epi_outs = epi if epi is not None else (None, [], [])
    tile_bytes = 4 + (2 * res.dtype.itemsize if has_res else 0)
    tile_bytes += sum(2 * e.shape[0] * e.dtype.itemsize for e in epi_ins)
    tile_bytes += sum(2 * l * jnp.dtype(dt).itemsize for l, dt in epi_outs) if epi else 2 * jnp.dtype(out_dtype).itemsize
    tm, tn, tk = _mm_tiles(m, n_unit, k_unit, tile_bytes, small_tiles_ok=not epi)
    if b_group:
        tk = b_group * b_c
        assert b_slots and tb and k % tk == 0 and (not a_slots or a_c % tk == 0)
    ni, nj, nk = m // tm, n // tn, k // tk
    dims = (((0,) if ta else (1,), (1,) if tb else (0,)), ((), ()))

    def slot_map(per, pos):
        if pos == "k_cols":
            return lambda i, j, kk: (kk // per, i, kk % per)
        if pos == "k_cols_j":
            return lambda i, j, kk: (kk // per, j, kk % per)
        if pos == "n_cols_k":
            return lambda i, j, kk: (j // per, kk, j % per)
        return lambda i, j, kk: (j // per, i, j % per)

    if a_slots:
        a_spec = pl.BlockSpec((None, tm, tk), slot_map(a_c // tk, "k_cols"))
    else:
        a_spec = pl.BlockSpec((tk, tm), lambda i, j, kk: (kk, i)) if ta else pl.BlockSpec((tm, tk), lambda i, j, kk: (i, kk))
    if b_group:
        b_spec = pl.BlockSpec((b_group, tn, b_c), lambda i, j, kk: (kk, j, 0))
    elif b_slots and tb:
        b_spec = pl.BlockSpec((None, tn, tk), slot_map(b_c // tk, "k_cols_j"))
    elif b_slots:
        b_spec = pl.BlockSpec((None, tk, tn), slot_map(b_c // tn, "n_cols_k"))
    else:
        b_spec = pl.BlockSpec((tn, tk), lambda i, j, kk: (j, kk)) if tb else pl.BlockSpec((tk, tn), lambda i, j, kk: (kk, j))
    if epi:
        lead_spec = lambda l: pl.BlockSpec((l, tm, tn), lambda i, j, kk: (0, i, j))
        o_specs = [lead_spec(l) for l, _ in epi_outs]
        o_shapes = [jax.ShapeDtypeStruct((l, m, n), dt) for l, dt in epi_outs]
    elif out_slots:
        o_specs = [pl.BlockSpec((None, tm, tn), slot_map((n // out_slots) // tn, "n_cols_i"))]
        o_shapes = [jax.ShapeDtypeStruct((out_slots, m, n // out_slots), out_dtype)]
    else:
        o_specs = [pl.BlockSpec((tm, tn), lambda i, j, kk: (i, j))]
        o_shapes = [jax.ShapeDtypeStruct((m, n), out_dtype)]
    extra_ins = ([res] if has_res else []) + list(epi_ins)
    extra_specs = ([o_specs[0]] if has_res else []) + [pl.BlockSpec((e.shape[0], tm, tn), lambda i, j, kk: (0, i, j)) for e in epi_ins]
    n_in, n_out = 2 + len(extra_ins), len(o_specs)
    c_kind, c_arrays, c_specs, c_shapes, c_scratch = _carry_parts(carry)
    nc = len(c_arrays)
    last_step = ni * nj * nk - 1

    def body(*refs):
        a_ref, b_ref = refs[0], refs[1]
        e_refs = refs[2:n_in]
        x_refs = refs[n_in:n_in + nc]
        o_refs = refs[n_in + nc:n_in + nc + n_out]
        out_refs = refs[n_in + nc + n_out:n_in + 2 * nc + n_out]
        scratch = refs[n_in + 2 * nc + n_out:]
        acc = scratch[0] if nk > 1 else None
        kk = pl.program_id(2)
        step = (pl.program_id(0) * nj + pl.program_id(1)) * nk + kk
        after = _carry_hooks(c_kind, x_refs, out_refs, scratch[-3:], step, last_step)

        def emit(val):
            if has_res:
                val = val + e_refs[0][...].astype(F32)
            if epi:
                for o_ref, parts in zip(o_refs, epi_fn(val, *[e[...] for e in e_refs])):
                    for l, v in enumerate(parts):
                        o_ref[l] = v.astype(o_ref.dtype)
            else:
                o_refs[0][...] = val.astype(out_dtype)

        if epi and nk == 1 and not ta:
            rc = tm // V7X_EPI_ROW_CHUNKS
            for r in range(V7X_EPI_ROW_CHUNKS):
                rows = slice(r * rc, (r + 1) * rc)
                val = lax.dot_general(a_ref[rows, :], b_ref[...], dims, preferred_element_type=F32)
                for o_ref, parts in zip(o_refs, epi_fn(val, *[e[:, rows, :] for e in e_refs])):
                    for l, v in enumerate(parts):
                        o_ref[l, rows, :] = v.astype(o_ref.dtype)
            after()
            return
        if b_group:
            part = sum(lax.dot_general(a_ref[:, s * b_c:(s + 1) * b_c], b_ref[s], dims, preferred_element_type=F32)
                       for s in range(b_group))
        else:
            part = lax.dot_general(a_ref[...], b_ref[...], dims, preferred_element_type=F32)
        if nk == 1:
            emit(part)
        else:
            @pl.when(kk == 0)
            def _():
                acc[...] = part

            @pl.when(kk > 0)
            def _():
                acc[...] += part

            @pl.when(kk == nk - 1)
            def _():
                emit(acc[...])

        after()

    sem = ("arbitrary",) * 3 if nc else ("parallel", "parallel", "arbitrary")
    outs = pl.pallas_call(
        body, name=name,
        grid=(ni, nj, nk),
        in_specs=[a_spec, b_spec] + extra_specs + c_specs,
        out_specs=o_specs + c_specs,
        out_shape=o_shapes + c_shapes,
        scratch_shapes=([pltpu.VMEM((tm, tn), F32)] if nk > 1 else []) + c_scratch,
        compiler_params=_cparams(*sem),
    )(a, b, *extra_ins, *c_arrays)
    main = list(outs[:n_out]) if epi else outs[0]
    return (main, list(outs[n_out:])) if nc else main


def _ffn_in_fused(h2, w_s, *, carry=None, name):
    t, d = h2.shape
    n_slot, _, c = w_s.shape
    half = n_slot // 2
    tm = _tile(t, 512)
    c_kind, c_arrays, c_specs, c_shapes, c_scratch = _carry_parts(carry)
    nc = len(c_arrays)
    last_step = (t // tm) * half - 1

    def body(h_ref, wg_ref, wu_ref, *refs):
        x_refs, (gu_ref, act_ref), out_refs, sems = refs[:nc], refs[nc:nc + 2], refs[nc + 2:2 * nc + 2], refs[2 * nc + 2:]
        step = pl.program_id(0) * half + pl.program_id(1)
        after = _carry_hooks(c_kind, x_refs, out_refs, sems, step, last_step)
        h = h_ref[...]
        g = jnp.dot(h, wg_ref[...], preferred_element_type=F32)
        u = jnp.dot(h, wu_ref[...], preferred_element_type=F32)
        sg = _sigmoid(g)
        silu = g * sg
        gu_ref[0] = (u * (sg + silu - silu * sg)).astype(BF16)
        gu_ref[1] = silu.astype(BF16)
        act_ref[...] = (silu * u).astype(BF16)
        after()

    outs = pl.pallas_call(
        body, name=name,
        grid=(t // tm, half),
        in_specs=[pl.BlockSpec((tm, d), lambda i, j: (i, 0)),
                  pl.BlockSpec((None, d, c), lambda i, j: (j, 0, 0)),
                  pl.BlockSpec((None, d, c), lambda i, j: (half + j, 0, 0))] + c_specs,
        out_specs=[pl.BlockSpec((2, tm, c), lambda i, j: (0, i, j)), pl.BlockSpec((tm, c), lambda i, j: (i, j))] + c_specs,
        out_shape=[jax.ShapeDtypeStruct((2, t, half * c), BF16), jax.ShapeDtypeStruct((t, half * c), BF16)] + c_shapes,
        scratch_shapes=c_scratch,
        compiler_params=_cparams(*(("arbitrary",) * 2 if nc else ("parallel", "parallel"))),
    )(h2, w_s, w_s, *c_arrays)
    return (outs[0], outs[1], list(outs[2:])) if nc else (outs[0], outs[1])


def _rowwise(fn, ins, row_outs, acc_outs, *, rows, tb, name, carry=None):
    in_specs, args = [], []
    for spec in ins:
        kind, arr = spec[0], spec[1]
        if kind == "row":
            in_specs.append(pl.BlockSpec((tb, arr.shape[1]), lambda i: (i, 0)))
        elif kind == "win":
            width, cb = spec[2], spec[3]
            in_specs.append(pl.BlockSpec((tb, width), functools.partial(lambda i, cb: (i, cb), cb=cb)))
        else:
            in_specs.append(pl.BlockSpec(arr.shape, lambda i: (0, 0)))
        args.append(arr)
    out_specs = [pl.BlockSpec((tb, c), lambda i: (i, 0)) for c, _ in row_outs]
    out_specs += [pl.BlockSpec(shape, lambda i: (0, 0)) for shape in acc_outs]
    out_shape = [jax.ShapeDtypeStruct((rows, c), dt) for c, dt in row_outs]
    out_shape += [jax.ShapeDtypeStruct(shape, F32) for shape in acc_outs]
    n_in, n_row, n_out = len(ins), len(row_outs), len(row_outs) + len(acc_outs)
    c_kind, c_arrays, c_specs, c_shapes, c_scratch = _carry_parts(carry)
    nc = len(c_arrays)

    def body(*refs):
        after = _carry_hooks(c_kind, refs[n_in:n_in + nc], refs[n_in + nc + n_out:n_in + 2 * nc + n_out],
                             refs[n_in + 2 * nc + n_out:], pl.program_id(0), rows // tb - 1)
        vals = [r[...] for r in refs[:n_in]]
        outs = fn(*vals)
        if not isinstance(outs, (tuple, list)):
            outs = (outs,)
        out_refs = refs[n_in + nc:n_in + nc + n_out]
        for o_ref, val in zip(out_refs[:n_row], outs[:n_row]):
            o_ref[...] = val.astype(o_ref.dtype)
        first = pl.program_id(0) == 0
        for o_ref, val in zip(out_refs[n_row:], outs[n_row:]):
            @pl.when(first)
            def _(o_ref=o_ref):
                o_ref[...] = jnp.zeros_like(o_ref)
            o_ref[...] += val
        after()

    res = pl.pallas_call(
        body, name=name,
        grid=(rows // tb,),
        in_specs=in_specs + c_specs, out_specs=out_specs + c_specs, out_shape=out_shape + c_shapes,
        scratch_shapes=c_scratch,
        compiler_params=_cparams("arbitrary"),
    )(*args, *c_arrays)
    return (list(res[:n_out]), list(res[n_out:])) if nc else res


def _rms_fwd(x, g, name, carry=None):
    def fn(xv, gv):
        r = lax.rsqrt(jnp.mean(xv * xv, axis=-1, keepdims=True) + NORM_EPS)
        return (xv * r * gv,)
    res = _rowwise(fn, [("row", x), ("full", g)], [(x.shape[1], BF16)], [], rows=x.shape[0], tb=_tile(x.shape[0], 512),
                   name=name, carry=carry)
    return (res[0][0], res[1]) if carry is not None else res[0]


def _rms_bwd(x, g, dh, dres, name, want_bf16, carry=None):
    d = x.shape[1]

    def fn(xv, gv, dhv, drv):
        r = lax.rsqrt(jnp.mean(xv * xv, axis=-1, keepdims=True) + NORM_EPS)
        xhat = xv * r
        dhv = dhv.astype(F32)
        dxhat = dhv * gv
        dx = drv + r * (dxhat - xhat * jnp.mean(dxhat * xhat, axis=-1, keepdims=True))
        dg = jnp.sum(dhv * xhat, axis=0, keepdims=True)
        return (dx, dx, dg) if want_bf16 else (dx, dg)

    row_outs = [(d, F32), (d, BF16)] if want_bf16 else [(d, F32)]
    return _rowwise(fn, [("row", x), ("full", g), ("row", dh), ("row", dres)], row_outs, [(1, d)],
                    rows=x.shape[0], tb=_tile(x.shape[0], 256), name=name, carry=carry)


def _loss_head(x2, g, target, name):
    d = x2.shape[1]

    def fn(xv, gv, tv):
        r = lax.rsqrt(jnp.mean(xv * xv, axis=-1, keepdims=True) + NORM_EPS)
        xhat = xv * r
        diff = xhat * gv - tv
        loss = 0.5 * jnp.sum(jnp.mean(diff * diff, axis=-1, keepdims=True), axis=0, keepdims=True)
        dy = diff * (1.0 / d)
        dxhat = dy * gv
        dx = r * (dxhat - xhat * jnp.mean(dxhat * xhat, axis=-1, keepdims=True))
        dg = jnp.sum(dy * xhat, axis=0, keepdims=True)
        return dx, dx, dg, jnp.broadcast_to(loss, (1, LANE))

    return _rowwise(fn, [("row", x2), ("full", g), ("row", target)], [(d, F32), (d, BF16)], [(1, d), (1, LANE)],
                    rows=x2.shape[0], tb=_tile(x2.shape[0], 256), name=name)


def _swiglu_bwd_tile(dact, dswiglu):
    return ((dact * dswiglu[0].astype(F32), dact * dswiglu[1].astype(F32)),)


def _mix_fwd(proj, pa, pb, d, name):
    def fn(ga, gb, av, bv):
        return (_sigmoid(ga.astype(F32)) * av.astype(F32) + _sigmoid(gb.astype(F32)) * bv.astype(F32),)
    return _rowwise(fn, [("win", proj, d, 0), ("win", proj, d, 1), ("row", pa), ("row", pb)], [(d, BF16)], [],
                    rows=pa.shape[0], tb=_tile(pa.shape[0], 512), name=name)[0]


def _mix_bwd(proj, pa, pb, dmix, d, name):
    def fn(ga, gb, av, bv, dm):
        dm = dm.astype(F32)
        sa, sb = _sigmoid(ga.astype(F32)), _sigmoid(gb.astype(F32))
        av, bv = av.astype(F32), bv.astype(F32)
        return dm * sa, dm * sb, dm * av * sa * (1.0 - sa), dm * bv * sb * (1.0 - sb)
    return _rowwise(fn, [("win", proj, d, 0), ("win", proj, d, 1), ("row", pa), ("row", pb), ("row", dmix)],
                    [(d, BF16)] * 4, [], rows=pa.shape[0], tb=_tile(pa.shape[0], 512), name=name)


def _chunk_masks(tb):
    r = lax.broadcasted_iota(jnp.int32, (tb, tb), 0)
    c = lax.broadcasted_iota(jnp.int32, (tb, tb), 1)
    same = lax.shift_right_logical(r, GLA_CHUNK_SHIFT) == lax.shift_right_logical(c, GLA_CHUNK_SHIFT)
    return same, same & (c <= r), same & (r <= c)


def _mask_bf16(mask):
    return jnp.where(mask, 1.0, 0.0).astype(BF16)


def _split_dot(mask_bf, x, terms):
    acc, rem = None, x
    for _ in range(terms):
        hi = rem.astype(BF16)
        part = jnp.dot(mask_bf, hi, preferred_element_type=F32)
        acc = part if acc is None else acc + part
        rem = rem - hi.astype(F32)
    return acc


def _gla_decay(al, wa2, ba2, same_bf, causal_bf):
    z = jnp.dot(al.astype(BF16), wa2, preferred_element_type=F32) + ba2
    la = (jnp.minimum(z, 0.0) - jnp.log(1.0 + jnp.exp(-jnp.abs(z)))) * (1.0 / GLA_TAU)
    bc = _split_dot(causal_bf, la, 3)
    bl = _split_dot(same_bf, la, 3)
    return z, bc, bl


def _dot_t(a, b, ca, cb):
    return lax.dot_general(a, b, (((ca,), (cb,)), ((), ())), preferred_element_type=F32)


def _gla_fwd(proj, alow, wa2, ba2, ghn, *, dk, dv, name):
    t = proj.shape[0]
    tb = min(GLA_BLOCK, t)
    nch = tb // GLA_CHUNK
    hk, hv = dk // GLA_HEADS, dv // GLA_HEADS
    scale = hk ** -0.5
    v_cb, r_cb = (8 * dk) // dv, (8 * dk) // dv + 1
    q_cb, k_cb = (8 * dk + 2 * dv) // dk, (8 * dk + 2 * dv) // dk + 1

    def body(q_ref, k_ref, v_ref, r_ref, al_ref, wa2_ref, ba2_ref, ghn_ref, oa_ref, opre_ref, s_ref, st_scr):
        @pl.when(pl.program_id(0) == 0)
        def _():
            st_scr[...] = jnp.zeros_like(st_scr)

        same, causal, _ = _chunk_masks(tb)
        same_bf, causal_bf = _mask_bf16(same), _mask_bf16(causal)
        _, bc, bl = _gla_decay(al_ref[...], wa2_ref[...], ba2_ref[...], same_bf, causal_bf)
        q = q_ref[...].astype(F32) * scale
        k = k_ref[...].astype(F32)
        qd = (q * jnp.exp(bc)).astype(BF16)
        ki = (k * jnp.exp(-bc)).astype(BF16)
        ks = (k * jnp.exp(bl - bc)).astype(BF16)
        dl = jnp.exp(bl)
        ksls = [slice(h * hk, (h + 1) * hk) for h in range(GLA_HEADS)]
        vsls = [slice(h * hv, (h + 1) * hv) for h in range(GLA_HEADS)]
        v_hs = [v_ref[:, vsl] for vsl in vsls]
        o_intras = []
        for ksl, v_h in zip(ksls, v_hs):
            sc = jnp.where(causal, _dot_t(qd[:, ksl], ki[:, ksl], 1, 1), 0.0)
            o_intras.append(jnp.dot(sc.astype(BF16), v_h, preferred_element_type=F32))
        for c in range(nch):
            rows = slice(c * GLA_CHUNK, (c + 1) * GLA_CHUNK)
            for h, (ksl, vsl) in enumerate(zip(ksls, vsls)):
                st = st_scr[h]
                s_ref[c, h] = st
                opre_ref[rows, vsl] = o_intras[h][rows] + _dot_t(qd[rows, ksl], st.astype(BF16), 1, 1)
                st_scr[h] = dl[c * GLA_CHUNK:c * GLA_CHUNK + 1, ksl] * st + _dot_t(v_hs[h][rows], ks[rows, ksl], 0, 0)
        for h in range(GLA_HEADS):
            vsl = slice(h * hv, (h + 1) * hv)
            o = opre_ref[:, vsl]
            rs = lax.rsqrt(jnp.mean(o * o, axis=-1, keepdims=True) + NORM_EPS)
            rv = r_ref[:, vsl].astype(F32)
            oa_ref[:, vsl] = (rv * _sigmoid(rv) * (o * rs * ghn_ref[:, vsl])).astype(BF16)

    nchunks = t // GLA_CHUNK
    return pl.pallas_call(
        body, name=name,
        grid=(t // tb,),
        in_specs=[
            pl.BlockSpec((tb, dk), lambda i: (i, q_cb)),
            pl.BlockSpec((tb, dk), lambda i: (i, k_cb)),
            pl.BlockSpec((tb, dv), lambda i: (i, v_cb)),
            pl.BlockSpec((tb, dv), lambda i: (i, r_cb)),
            pl.BlockSpec((tb, LANE), lambda i: (i, 0)),
            pl.BlockSpec(wa2.shape, lambda i: (0, 0)),
            pl.BlockSpec(ba2.shape, lambda i: (0, 0)),
            pl.BlockSpec(ghn.shape, lambda i: (0, 0)),
        ],
        out_specs=[
            pl.BlockSpec((tb, dv), lambda i: (i, 0)),
            pl.BlockSpec((tb, dv), lambda i: (i, 0)),
            pl.BlockSpec((nch, GLA_HEADS, hv, hk), lambda i: (i, 0, 0, 0)),
        ],
        out_shape=[
            jax.ShapeDtypeStruct((t, dv), BF16),
            jax.ShapeDtypeStruct((t, dv), F32),
            jax.ShapeDtypeStruct((nchunks, GLA_HEADS, hv, hk), F32),
        ],
        scratch_shapes=[pltpu.VMEM((GLA_HEADS, hv, hk), F32)],
        compiler_params=_cparams("arbitrary"),
    )(proj, proj, proj, proj, alow, wa2, ba2, ghn)


def _gla_bwd(proj, alow, wa2, ba2, ghn, opre, states, doa, *, dk, dv, name):
    t = proj.shape[0]
    tb = min(GLA_BLOCK, t)
    nb = t // tb
    nch = tb // GLA_CHUNK
    hk, hv = dk // GLA_HEADS, dv // GLA_HEADS
    scale = hk ** -0.5
    v_cb, r_cb = (8 * dk) // dv, (8 * dk) // dv + 1
    q_cb, k_cb = (8 * dk + 2 * dv) // dk, (8 * dk + 2 * dv) // dk + 1

    def body(q_ref, k_ref, v_ref, r_ref, al_ref, wa2_ref, ba2_ref, ghn_ref, opre_ref, s_ref, doa_ref,
             dq_ref, dk_ref, dv_ref, dr_ref, dal_ref, dwa2_ref, dba2_ref, dghn_ref,
             dst_scr, dqd_scr, dki_scr, dks_scr, ddl_scr):
        @pl.when(pl.program_id(0) == 0)
        def _():
            dst_scr[...] = jnp.zeros_like(dst_scr)
            dwa2_ref[...] = jnp.zeros_like(dwa2_ref)
            dba2_ref[...] = jnp.zeros_like(dba2_ref)
            dghn_ref[...] = jnp.zeros_like(dghn_ref)

        same, causal, anti = _chunk_masks(tb)
        same_bf, causal_bf, anti_bf = _mask_bf16(same), _mask_bf16(causal), _mask_bf16(anti)
        al = al_ref[...]
        wa2v = wa2_ref[...]
        z, bc, bl = _gla_decay(al, wa2v, ba2_ref[...], same_bf, causal_bf)
        e_bc, e_nbc, e_st = jnp.exp(bc), jnp.exp(-bc), jnp.exp(bl - bc)
        q = q_ref[...].astype(F32) * scale
        k = k_ref[...].astype(F32)
        qd_f, ki_f, ks_f = q * e_bc, k * e_nbc, k * e_st
        qd, ki, ks = qd_f.astype(BF16), ki_f.astype(BF16), ks_f.astype(BF16)
        dl = jnp.exp(bl)
        per_head = []
        for h in range(GLA_HEADS):
            ksl = slice(h * hk, (h + 1) * hk)
            vsl = slice(h * hv, (h + 1) * hv)
            o = opre_ref[:, vsl]
            rs = lax.rsqrt(jnp.mean(o * o, axis=-1, keepdims=True) + NORM_EPS)
            ohat = o * rs
            g_h = ghn_ref[:, vsl]
            rv = r_ref[:, vsl].astype(F32)
            sg = _sigmoid(rv)
            d_oa = doa_ref[:, vsl].astype(F32)
            don = d_oa * (rv * sg)
            dr_ref[:, vsl] = (d_oa * (ohat * g_h) * (sg * (1.0 + rv * (1.0 - sg)))).astype(BF16)
            dghn_ref[:, vsl] += jnp.sum(don * ohat, axis=0, keepdims=True)
            dohat = don * g_h
            do_f = rs * (dohat - ohat * jnp.mean(dohat * ohat, axis=-1, keepdims=True))
            do = do_f.astype(BF16)
            v_h = v_ref[:, vsl]
            p = jnp.where(causal, _dot_t(do, v_h, 1, 1), 0.0).astype(BF16)
            dqd_intra = jnp.dot(p, ki[:, ksl], preferred_element_type=F32)
            dki_scr[:, ksl] = _dot_t(p, qd[:, ksl], 0, 0)
            sc = jnp.where(causal, _dot_t(qd[:, ksl], ki[:, ksl], 1, 1), 0.0).astype(BF16)
            dv_intra = _dot_t(sc, do, 0, 0)
            per_head.append((ksl, vsl, v_h, do, dqd_intra, dv_intra))
        for c in reversed(range(nch)):
            rows = slice(c * GLA_CHUNK, (c + 1) * GLA_CHUNK)
            for h, (ksl, vsl, v_h, do, dqd_intra, dv_intra) in enumerate(per_head):
                dst = dst_scr[h]
                st = s_ref[c, h]
                dst_bf = dst.astype(BF16)
                dv_ref[rows, vsl] = (dv_intra[rows] + _dot_t(ks[rows, ksl], dst_bf, 1, 1)).astype(BF16)
                dks_scr[rows, ksl] = jnp.dot(v_h[rows], dst_bf, preferred_element_type=F32)
                dl_c = dl[c * GLA_CHUNK:c * GLA_CHUNK + 1, ksl]
                ddl = jnp.sum(dst * st, axis=0, keepdims=True) * dl_c
                ddl_scr[rows, ksl] = jnp.broadcast_to(ddl, (GLA_CHUNK, hk))
                dqd_scr[rows, ksl] = dqd_intra[rows] + jnp.dot(do[rows], st.astype(BF16), preferred_element_type=F32)
                dst_scr[h] = dl_c * dst + _dot_t(do[rows], qd[rows, ksl], 0, 0)
        dqd, dki, dks = dqd_scr[...], dki_scr[...], dks_scr[...]
        dq_ref[...] = (dqd * (scale * e_bc)).astype(BF16)
        dk_ref[...] = (dki * e_nbc + dks * e_st).astype(BF16)
        dks_ks = dks * ks_f
        dbc = dqd * qd_f - dki * ki_f - dks_ks
        dla = _split_dot(anti_bf, dbc, 2) + _split_dot(same_bf, dks_ks, 2) + ddl_scr[...]
        dz = (dla * (1.0 / GLA_TAU) * (1.0 - _sigmoid(z)))
        dz_bf = dz.astype(BF16)
        dal_ref[...] = _dot_t(dz_bf, wa2v, 1, 1).astype(BF16)
        dwa2_ref[...] += _dot_t(al.astype(BF16), dz_bf, 0, 0)
        dba2_ref[...] += jnp.sum(dz, axis=0, keepdims=True)

    rev = lambda i: nb - 1 - i
    return pl.pallas_call(
        body, name=name,
        grid=(nb,),
        in_specs=[
            pl.BlockSpec((tb, dk), lambda i: (rev(i), q_cb)),
            pl.BlockSpec((tb, dk), lambda i: (rev(i), k_cb)),
            pl.BlockSpec((tb, dv), lambda i: (rev(i), v_cb)),
            pl.BlockSpec((tb, dv), lambda i: (rev(i), r_cb)),
            pl.BlockSpec((tb, LANE), lambda i: (rev(i), 0)),
            pl.BlockSpec(wa2.shape, lambda i: (0, 0)),
            pl.BlockSpec(ba2.shape, lambda i: (0, 0)),
            pl.BlockSpec(ghn.shape, lambda i: (0, 0)),
            pl.BlockSpec((tb, dv), lambda i: (rev(i), 0)),
            pl.BlockSpec((nch, GLA_HEADS, hv, hk), lambda i: (rev(i), 0, 0, 0)),
            pl.BlockSpec((tb, dv), lambda i: (rev(i), 0)),
        ],
        out_specs=[
            pl.BlockSpec((tb, dk), lambda i: (rev(i), 0)),
            pl.BlockSpec((tb, dk), lambda i: (rev(i), 0)),
            pl.BlockSpec((tb, dv), lambda i: (rev(i), 0)),
            pl.BlockSpec((tb, dv), lambda i: (rev(i), 0)),
            pl.BlockSpec((tb, LANE), lambda i: (rev(i), 0)),
            pl.BlockSpec(wa2.shape, lambda i: (0, 0)),
            pl.BlockSpec(ba2.shape, lambda i: (0, 0)),
            pl.BlockSpec(ghn.shape, lambda i: (0, 0)),
        ],
        out_shape=[
            jax.ShapeDtypeStruct((t, dk), BF16),
            jax.ShapeDtypeStruct((t, dk), BF16),
            jax.ShapeDtypeStruct((t, dv), BF16),
            jax.ShapeDtypeStruct((t, dv), BF16),
            jax.ShapeDtypeStruct((t, LANE), BF16),
            jax.ShapeDtypeStruct(wa2.shape, F32),
            jax.ShapeDtypeStruct(ba2.shape, F32),
            jax.ShapeDtypeStruct(ghn.shape, F32),
        ],
        scratch_shapes=[pltpu.VMEM((GLA_HEADS, hv, hk), F32)] + [pltpu.VMEM((tb, dk), F32)] * 4,
        compiler_params=_cparams("arbitrary"),
    )(proj, proj, proj, proj, alow, wa2, ba2, ghn, opre, states, doa)


def _s5_tables(lam_re, lam_im, log_dt, b_re, b_im, c_re, c_im):
    hp = lax.Precision.HIGHEST
    g, p = lam_re.shape
    ln = S5_L
    dt = jnp.exp(log_dt)[:, None]
    lr, li = lam_re, lam_im
    mag = jnp.exp(lr * dt)
    ar, ai = mag * jnp.cos(li * dt), mag * jnp.sin(li * dt)
    den = lr * lr + li * li
    am1 = ar - 1.0
    f_re = ((am1 * lr + ai * li) / den)[..., None]
    f_im = ((ai * lr - am1 * li) / den)[..., None]
    bb_re = f_re * b_re - f_im * b_im
    bb_im = f_re * b_im + f_im * b_re
    j = jnp.arange(ln + 1, dtype=F32)[None, :, None]
    pm = jnp.exp(j * (lr * dt)[:, None, :])
    ang = j * (li * dt)[:, None, :]
    pw_re, pw_im = pm * jnp.cos(ang), pm * jnp.sin(ang)
    cp_re = c_re[:, None] * pw_re[:, :, None, :] - c_im[:, None] * pw_im[:, :, None, :]
    cp_im = c_re[:, None] * pw_im[:, :, None, :] + c_im[:, None] * pw_re[:, :, None, :]
    kj = (jnp.einsum("gjcp,gpd->gjcd", cp_re[:, :ln], bb_re, precision=hp)
          - jnp.einsum("gjcp,gpd->gjcd", cp_im[:, :ln], bb_im, precision=hp))
    eye = jnp.eye(S5_TILE_G, dtype=F32)
    nt = g // S5_TILE_G
    k8 = jnp.einsum("jglcd,gh->jlgdhc", kj.reshape(nt, S5_TILE_G, ln, S5_GC, S5_GC), eye).reshape(nt, ln, LANE, LANE)
    rp_re, rp_im = pw_re[:, ln - 1::-1], pw_im[:, ln - 1::-1]
    bbt_re, bbt_im = bb_re.transpose(0, 2, 1)[:, None], bb_im.transpose(0, 2, 1)[:, None]
    bst = jnp.stack([rp_re[:, :, None, :] * bbt_re - rp_im[:, :, None, :] * bbt_im,
                     rp_re[:, :, None, :] * bbt_im + rp_im[:, :, None, :] * bbt_re], axis=3)
    b8 = jnp.einsum("jgscrp,gh->jsgcrhp", bst.reshape(nt, S5_TILE_G, ln, S5_GC, 2, p), eye)
    b8 = b8.reshape(nt, ln, LANE, 2 * S5_TILE_G * p)
    cst = jnp.stack([cp_re[:, 1:], -cp_im[:, 1:]], axis=2)
    c8 = jnp.einsum("jgtrcp,gh->jtrgphc", cst.reshape(nt, S5_TILE_G, ln, 2, S5_GC, p), eye)
    c8 = c8.reshape(nt, ln, 2 * S5_TILE_G * p, LANE)
    aw_re, aw_im = pw_re[:, ln].reshape(nt, S5_TILE_G * p), pw_im[:, ln].reshape(nt, S5_TILE_G * p)
    a8 = jnp.stack([jnp.concatenate([aw_re, aw_re], axis=-1), jnp.concatenate([-aw_im, aw_im], axis=-1)], axis=1)
    return k8, b8, c8, a8


def _state_scan(v, pr, pi, reverse):
    n = v.shape[0]
    half = v.shape[1] // 2
    row = lax.broadcasted_iota(jnp.int32, v.shape, 0)
    z, s = v, 1
    while s < n:
        if reverse:
            zs = jnp.where(row < n - s, pltpu.roll(z, n - s, 0), 0.0)
        else:
            zs = jnp.where(row >= s, pltpu.roll(z, s, 0), 0.0)
        z = z + zs * pr + pltpu.roll(zs, half, 1) * pi
        pr, pi = pr * pr - pi * pi, 2.0 * pr * pi
        s *= 2
    return z


def _s5_fwd(proj, u_cb, k8, b8, c8, a8, name):
    t = proj.shape[0]
    nt, ln, _, w = b8.shape
    nc = t // ln
    rb = min(t, S5_ROW_BLOCK)

    def body(u_ref, k_ref, b_ref, c_ref, a_ref, y_ref, x_ref, uf_ref):
        uf_ref[...] = u_ref[...].astype(F32)
        pos = lax.broadcasted_iota(jnp.int32, (rb, LANE), 0) & (ln - 1)
        for r0 in range(0, t, rb):
            u = uf_ref[r0:r0 + rb, :]
            acc = jnp.dot(u.astype(BF16), k_ref[0], preferred_element_type=F32)
            for lag in range(1, ln):
                us = jnp.where(pos >= lag, pltpu.roll(u, lag, 0), 0.0).astype(BF16)
                acc = acc + jnp.dot(us, k_ref[lag], preferred_element_type=F32)
            y_ref[r0:r0 + rb, :] = acc
        v = jnp.dot(uf_ref[pl.ds(0, nc, stride=ln), :].astype(BF16), b_ref[0], preferred_element_type=F32)
        for s in range(1, ln):
            v = v + jnp.dot(uf_ref[pl.ds(s, nc, stride=ln), :].astype(BF16), b_ref[s], preferred_element_type=F32)
        z = _state_scan(v, a_ref[0:1, :], a_ref[1:2, :], reverse=False)
        row = lax.broadcasted_iota(jnp.int32, z.shape, 0)
        x = jnp.where(row >= 1, pltpu.roll(z, 1, 0), 0.0)
        x_ref[...] = x
        x_bf = x.astype(BF16)
        for tt in range(ln):
            y_ref[pl.ds(tt, nc, stride=ln), :] += jnp.dot(x_bf, c_ref[tt], preferred_element_type=F32)

    tile = lambda shape: pl.BlockSpec((None,) + shape, lambda j: (j,) + (0,) * len(shape))
    return pl.pallas_call(
        body, name=name, grid=(nt,),
        in_specs=[pl.BlockSpec((t, LANE), lambda j: (0, u_cb + j)), tile((ln, LANE, LANE)), tile((ln, LANE, w)),
                  tile((ln, w, LANE)), tile((2, w))],
        out_specs=[pl.BlockSpec((t, LANE), lambda j: (0, j)), tile((nc, w))],
        out_shape=[jax.ShapeDtypeStruct((t, nt * LANE), F32), jax.ShapeDtypeStruct((nt, nc, w), F32)],
        scratch_shapes=[pltpu.VMEM((t, LANE), F32)],
        compiler_params=_cparams("parallel"),
    )(proj, k8, b8, c8, a8)


def _s5_bwd_data(dy, x_st, k8, b8, c8, a8, name):
    t = dy.shape[0]
    nt, ln, _, w = b8.shape
    nc = t // ln
    rb = min(t, S5_ROW_BLOCK)

    def body(dy_ref, x_ref, k_ref, b_ref, c_ref, a_ref, du_ref, dv_ref, da_ref, dyf_ref, duf_ref):
        dyf_ref[...] = dy_ref[...].astype(F32)
        pos = lax.broadcasted_iota(jnp.int32, (rb, LANE), 0) & (ln - 1)
        for r0 in range(0, t, rb):
            g = dyf_ref[r0:r0 + rb, :]
            acc = _dot_t(g.astype(BF16), k_ref[0], 1, 1)
            for lag in range(1, ln):
                gs = jnp.where(pos < ln - lag, pltpu.roll(g, rb - lag, 0), 0.0).astype(BF16)
                acc = acc + _dot_t(gs, k_ref[lag], 1, 1)
            duf_ref[r0:r0 + rb, :] = acc
        gx = _dot_t(dyf_ref[pl.ds(0, nc, stride=ln), :].astype(BF16), c_ref[0], 1, 1)
        for tt in range(1, ln):
            gx = gx + _dot_t(dyf_ref[pl.ds(tt, nc, stride=ln), :].astype(BF16), c_ref[tt], 1, 1)
        rtot = _state_scan(gx, a_ref[0:1, :], -a_ref[1:2, :], reverse=True)
        row = lax.broadcasted_iota(jnp.int32, rtot.shape, 0)
        dv = jnp.where(row < nc - 1, pltpu.roll(rtot, nc - 1, 0), 0.0)
        dv_ref[...] = dv
        dv_bf = dv.astype(BF16)
        for s in range(ln):
            duf_ref[pl.ds(s, nc, stride=ln), :] += _dot_t(dv_bf, b_ref[s], 1, 1)
        du_ref[...] = duf_ref[...].astype(BF16)
        x = x_ref[...]
        da_ref[0:1, :] = jnp.sum(dv * x, axis=0, keepdims=True)
        da_ref[1:2, :] = jnp.sum(dv * pltpu.roll(x, w // 2, 1), axis=0, keepdims=True)

    tile = lambda shape: pl.BlockSpec((None,) + shape, lambda j: (j,) + (0,) * len(shape))
    return pl.pallas_call(
        body, name=name, grid=(nt,),
        in_specs=[pl.BlockSpec((t, LANE), lambda j: (0, j)), tile((nc, w)), tile((ln, LANE, LANE)), tile((ln, LANE, w)),
                  tile((ln, w, LANE)), tile((2, w))],
        out_specs=[pl.BlockSpec((t, LANE), lambda j: (0, j)), tile((nc, w)), tile((2, w))],
        out_shape=[jax.ShapeDtypeStruct((t, nt * LANE), BF16), jax.ShapeDtypeStruct((nt, nc, w), F32),
                   jax.ShapeDtypeStruct((nt, 2, w), F32)],
        scratch_shapes=[pltpu.VMEM((t, LANE), F32), pltpu.VMEM((t, LANE), F32)],
        compiler_params=_cparams("parallel"),
    )(dy, x_st, k8, b8, c8, a8)


def _s5_bwd_tables(dy, proj, u_cb, x_st, dv, ln, name):
    t = dy.shape[0]
    nt, nc, w = x_st.shape
    rb = min(t, S5_ROW_BLOCK)

    def body(dy_ref, u_ref, x_ref, dv_ref, dk_ref, db_ref, dc_ref, dyf_ref, uf_ref):
        s = pl.program_id(1)

        @pl.when(s == 0)
        def _():
            dyf_ref[...] = dy_ref[...].astype(F32)
            uf_ref[...] = u_ref[...].astype(F32)
            pos = lax.broadcasted_iota(jnp.int32, (rb, LANE), 0) & (ln - 1)
            for r0 in range(0, t, rb):
                u, g_bf = uf_ref[r0:r0 + rb, :], dy_ref[r0:r0 + rb, :]
                for lag in range(ln):
                    us = u if lag == 0 else jnp.where(pos >= lag, pltpu.roll(u, lag, 0), 0.0)
                    part = _dot_t(us.astype(BF16), g_bf, 0, 0)
                    if r0 == 0:
                        dk_ref[lag] = part
                    else:
                        dk_ref[lag] += part

        rows = pl.ds(s, nc, stride=ln)
        db_ref[...] = _dot_t(uf_ref[rows, :].astype(BF16), dv_ref[...].astype(BF16), 0, 0)
        dc_ref[...] = _dot_t(x_ref[...].astype(BF16), dyf_ref[rows, :].astype(BF16), 0, 0)

    tile = lambda shape: pl.BlockSpec((None,) + shape, lambda j, s: (j,) + (0,) * len(shape))
    per_s = lambda shape: pl.BlockSpec((None, None) + shape, lambda j, s: (j, s, 0, 0))
    return pl.pallas_call(
        body, name=name, grid=(nt, ln),
        in_specs=[pl.BlockSpec((t, LANE), lambda j, s: (0, j)), pl.BlockSpec((t, LANE), lambda j, s: (0, u_cb + j)),
                  tile((nc, w)), tile((nc, w))],
        out_specs=[tile((ln, LANE, LANE)), per_s((LANE, w)), per_s((w, LANE))],
        out_shape=[jax.ShapeDtypeStruct((nt, ln, LANE, LANE), F32), jax.ShapeDtypeStruct((nt, ln, LANE, w), F32),
                   jax.ShapeDtypeStruct((nt, ln, w, LANE), F32)],
        scratch_shapes=[pltpu.VMEM((t, LANE), F32), pltpu.VMEM((t, LANE), F32)],
        compiler_params=_cparams("parallel", "arbitrary"),
    )(dy, proj, x_st, dv)


def _gelu_parts(y):
    inner = GELU_C * (y + GELU_A * y * y * y)
    th = jnp.tanh(inner)
    return th, 0.5 * y * (1.0 + th)


def _s5_post_fwd(y_raw, proj, u_cb, s5d, wglu, bglu, name):
    w = y_raw.shape[1]

    def fn(yr, u, dsk, wg, bg):
        y = yr + dsk * u.astype(F32)
        _, h = _gelu_parts(y)
        gl = jnp.dot(h.astype(BF16), wg, preferred_element_type=F32) + bg
        return (h * _sigmoid(gl),)

    return _rowwise(fn, [("row", y_raw), ("win", proj, w, u_cb), ("full", s5d), ("full", wglu), ("full", bglu)],
                    [(w, BF16)], [], rows=y_raw.shape[0], tb=_tile(y_raw.shape[0], 512), name=name)[0]


def _s5_post_bwd(y_raw, proj, u_cb, s5d, wglu, bglu, dob, name):
    w = y_raw.shape[1]

    def fn(yr, u, dsk, wg, bg, dov):
        u = u.astype(F32)
        dov = dov.astype(F32)
        y = yr + dsk * u
        th, h = _gelu_parts(y)
        h_bf = h.astype(BF16)
        gl = jnp.dot(h_bf, wg, preferred_element_type=F32) + bg
        sg = _sigmoid(gl)
        dgl = dov * h * sg * (1.0 - sg)
        dgl_bf = dgl.astype(BF16)
        dh = dov * sg + _dot_t(dgl_bf, wg, 1, 1)
        dgelu = 0.5 * (1.0 + th) + 0.5 * y * (1.0 - th * th) * GELU_C * (1.0 + 3.0 * GELU_A * y * y)
        dy = dh * dgelu
        return (dy, dy * dsk,
                _dot_t(h_bf, dgl_bf, 0, 0), jnp.sum(dgl, axis=0, keepdims=True), jnp.sum(dy * u, axis=0, keepdims=True))

    return _rowwise(fn, [("row", y_raw), ("win", proj, w, u_cb), ("full", s5d), ("full", wglu), ("full", bglu), ("row", dob)],
                    [(w, BF16), (w, BF16)], [(w, w), (1, w), (1, w)], rows=y_raw.shape[0], tb=_tile(y_raw.shape[0], 512), name=name)


def _adamw(w, g, m, v, name):
    _, rows, cols = w.shape
    tr, tc = (_tile(rows, 256, align=16), cols) if rows % 16 == 0 else (rows, _tile(cols, 256))
    slots = isinstance(g, (list, tuple))
    gs = list(g) if slots else [g]
    c1 = 1.0 - ADAM_B1 ** ADAM_STEP
    c2 = 1.0 - ADAM_B2 ** ADAM_STEP

    def body(w_ref, m_ref, v_ref, *refs):
        g_refs, out_refs = refs[:len(gs)], refs[len(gs):]
        if slots:
            parts = [g_ref[s].astype(F32) for g_ref in g_refs for s in range(g_ref.shape[0])]
            gv = parts[0]
            for p in parts[1:]:
                gv = gv + p
            out_refs[0][...] = gv
        else:
            gv = g_refs[0][...]
        d_ref, nm_ref, nv_ref = out_refs[-3:]
        nm = ADAM_B1 * m_ref[...] + (1.0 - ADAM_B1) * gv
        nv = ADAM_B2 * v_ref[...] + (1.0 - ADAM_B2) * (gv * gv)
        d_ref[...] = -ADAM_LR * ((nm / c1) / (jnp.sqrt(nv / c2) + ADAM_EPS) + ADAM_WD * w_ref[...])
        nm_ref[...] = nm
        nv_ref[...] = nv

    spec = pl.BlockSpec((None, tr, tc), lambda i, j: (0, i, j))
    g_specs = [pl.BlockSpec((a.shape[0], tr, tc), lambda i, j: (0, i, j)) for a in gs] if slots else [pl.BlockSpec((tr, tc), lambda i, j: (i, j))]
    n_out = 4 if slots else 3
    return pl.pallas_call(
        body, name=name, grid=(rows // tr, cols // tc),
        in_specs=[spec, spec, spec] + g_specs, out_specs=[spec] * n_out,
        out_shape=[jax.ShapeDtypeStruct((1, rows, cols), F32)] * n_out,
        compiler_params=_cparams("parallel", "parallel"),
    )(w, m, v, *gs)


def _slot_sum(x, name):
    _, rows, cols = x.shape
    if rows % 8 == 0:
        tr, tc = _tile(rows, 512, align=8), cols
    else:
        tr, tc = rows, _tile(cols, 256)

    def body(x_ref, o_ref):
        acc = x_ref[0].astype(F32)
        for s in range(1, N_DEV):
            acc = acc + x_ref[s].astype(F32)
        o_ref[...] = acc

    return pl.pallas_call(
        body, name=name, grid=(rows // tr, cols // tc),
        in_specs=[pl.BlockSpec((N_DEV, tr, tc), lambda i, j: (0, i, j))],
        out_specs=pl.BlockSpec((tr, tc), lambda i, j: (i, j)),
        out_shape=jax.ShapeDtypeStruct((rows, cols), F32),
        compiler_params=_cparams("parallel", "parallel"),
    )(x)


_REST = (("w_a2", 1), ("w_glu", 0), ("w_branch_a", 1), ("w_branch_b", 1), ("w_out", 0), ("w_ffn_in", 1), ("w_ffn_out", 0))
_SMALL = ("norm1_g", "b_a2", "gla_norm_g", "lam_re", "lam_im", "log_dt", "s5_b_re", "s5_b_im", "s5_c_re", "s5_c_im",
          "s5_d", "b_glu", "norm2_g", "final_norm_g")
_ORDER = ("norm1_g", "w_in", "w_a2", "b_a2", "gla_norm_g", "lam_re", "lam_im", "log_dt", "s5_b_re", "s5_b_im", "s5_c_re",
          "s5_c_im", "s5_d", "w_glu", "b_glu", "w_branch_a", "w_branch_b", "w_out", "norm2_g", "w_ffn_in", "w_ffn_out", "final_norm_g")


def _join_slots(slots, axis):
    _, r, c = slots.shape
    if axis == 0:
        return slots.reshape(N_DEV * r, c)
    return slots.transpose(1, 0, 2).reshape(r, N_DEV * c)


def _to_slots(full, axis):
    r, c = full.shape
    if axis == 0:
        return full.reshape(N_DEV, r // N_DEV, c)
    return full.reshape(r, N_DEV, c // N_DEV).transpose(1, 0, 2)


def _local_step(x, target, w_in_t, small, rest):
    t, d = x.shape
    dk, dv, s5w = d // 4, d // 2, d // 4
    dist = not isinstance(rest, dict)
    if dist:
        h1, (w_in_slots,) = _rms_fwd(x, small["norm1_g"], "norm1_fwd", carry=("ag", [w_in_t]))
        w_in_t = w_in_slots.reshape(-1, d)
    else:
        h1 = _rms_fwd(x, small["norm1_g"], "norm1_fwd")
    o_q, o_k, o_v, o_r, o_al = 0, dk, 2 * dk, 2 * dk + dv, 2 * dk + 2 * dv
    o_u = o_al + GLA_RANK
    o_ga, o_gb = o_u + s5w, o_u + s5w + d
    rows = lambda a, o, n: a[o:o + n]
    w_main_t = jnp.concatenate([rows(w_in_t, o_ga, d), rows(w_in_t, o_gb, d), rows(w_in_t, o_v, dv), rows(w_in_t, o_r, dv),
                                rows(w_in_t, o_q, dk), rows(w_in_t, o_k, dk), rows(w_in_t, o_u, s5w)], axis=0)
    w_al_t = jnp.pad(rows(w_in_t, o_al, GLA_RANK), ((0, LANE - GLA_RANK), (0, 0)))
    u_cb = (2 * d + 2 * dv + 2 * dk) // s5w

    if dist:
        proj, gathered = _mm(h1, w_main_t, tb=True, out_dtype=BF16, carry=("ag", rest[:-1]), name="in_proj")
        w = {n: _join_slots(g, ax) for (n, ax), g in zip(_REST[:-2], gathered[:-1])}
        w_ffn_in_s = gathered[-1]
    else:
        proj = _mm(h1, w_main_t, tb=True, out_dtype=BF16, name="in_proj")
        w = rest
        w_ffn_in_s = _to_slots(rest["w_ffn_in"], 1)
    wa2 = jnp.pad(w["w_a2"], ((0, LANE - GLA_RANK), (0, 0)))
    alow = _mm(h1, w_al_t, tb=True, out_dtype=BF16, name="in_proj_gate_rank")
    o_a, o_pre, states = _gla_fwd(proj, alow, wa2, small["b_a2"], small["gla_norm_g"], dk=dk, dv=dv, name="gla_fwd")

    s5_params = (small["lam_re"], small["lam_im"], small["log_dt"][0], small["s5_b_re"], small["s5_b_im"],
                 small["s5_c_re"], small["s5_c_im"])
    (k8, b8, c8, a8), tables_vjp = jax.vjp(_s5_tables, *s5_params)
    k8_bf, b8_bf, c8_bf = k8.astype(BF16), b8.astype(BF16), c8.astype(BF16)
    u_lane_cb = u_cb * s5w // LANE
    y_raw, x_st = _s5_fwd(proj, u_lane_cb, k8_bf, b8_bf, c8_bf, a8, "s5_scan_fwd")
    o_b = _s5_post_fwd(y_raw, proj, u_cb, small["s5_d"], w["w_glu"], small["b_glu"], "s5_post_fwd")

    pa = _mm(o_a, w["w_branch_a"], out_dtype=BF16, name="branch_a")
    pb = _mm(o_b, w["w_branch_b"], out_dtype=BF16, name="branch_b")
    mix = _mix_fwd(proj, pa, pb, d, "mix_fwd")
    x1 = _mm(mix, w["w_out"], res=x, name="out_proj")
    h2 = _rms_fwd(x1, small["norm2_g"], "norm2_fwd")
    if dist:
        gu, act, (w_ffn_out_s,) = _ffn_in_fused(h2, w_ffn_in_s, carry=("ag", rest[-1:]), name="ffn_in")
        w_ffn_out = _join_slots(w_ffn_out_s, 0)
    else:
        gu, act = _ffn_in_fused(h2, w_ffn_in_s, name="ffn_in")
        w_ffn_out = rest["w_ffn_out"]
    x2 = _mm(act, w_ffn_out, res=x1, name="ffn_out")
    dx2, dx2_bf, d_final_g, loss = _loss_head(x2, small["final_norm_g"], target, "loss_head")

    recv = {}
    dgu, = _mm(dx2_bf, w_ffn_out, tb=True, epi=(_swiglu_bwd_tile, [gu], [(2, BF16)]), name="d_act")
    g_ffn_out = _mm(act, dx2_bf, ta=True, out_dtype=BF16, name="g_w_ffn_out")
    if dist:
        g_ffn_in_s, recv["w_ffn_out"] = _mm(h2, dgu, ta=True, b_slots=True, out_dtype=BF16, out_slots=N_DEV,
                                            carry=("a2a", [_to_slots(g_ffn_out, 0)]), name="g_w_ffn_in")
        dh2, recv["w_ffn_in"] = _mm(dgu, w_ffn_in_s, tb=True, a_slots=True, b_slots=True, b_group=2,
                                    carry=("a2a", [g_ffn_in_s]), name="d_h2")
    else:
        g_ffn_in_s = _mm(h2, dgu, ta=True, b_slots=True, out_dtype=BF16, out_slots=N_DEV, name="g_w_ffn_in")
        dh2 = _mm(dgu, w_ffn_in_s, tb=True, a_slots=True, b_slots=True, b_group=2, name="d_h2")
    dx1, dx1_bf, d_norm2_g = _rms_bwd(x1, small["norm2_g"], dh2, dx2, "norm2_bwd", True)
    dmix = _mm(dx1_bf, w["w_out"], tb=True, out_dtype=BF16, name="d_mix")
    g_out = _mm(mix, dx1_bf, ta=True, out_dtype=BF16, name="g_w_out")
    dpa, dpb, dga, dgb = _mix_bwd(proj, pa, pb, dmix, d, "mix_bwd")
    doa = _mm(dpa, w["w_branch_a"], tb=True, out_dtype=BF16, name="d_o_a")
    dob = _mm(dpb, w["w_branch_b"], tb=True, out_dtype=BF16, name="d_o_b")
    g_branch_a = _mm(o_a, dpa, ta=True, out_dtype=BF16, name="g_w_branch_a")
    g_branch_b = _mm(o_b, dpb, ta=True, out_dtype=BF16, name="g_w_branch_b")

    dy_s5, du_direct, g_glu, g_bglu, g_s5d = _s5_post_bwd(y_raw, proj, u_cb, small["s5_d"], w["w_glu"], small["b_glu"], dob, "s5_post_bwd")
    du_scan, dv_st, d_a8 = _s5_bwd_data(dy_s5, x_st, k8_bf, b8_bf, c8_bf, a8, "s5_scan_bwd")
    d_k8, d_b8, d_c8 = _s5_bwd_tables(dy_s5, proj, u_lane_cb, x_st, dv_st, S5_L, "s5_scan_bwd_tables")
    g_lam_re, g_lam_im, g_log_dt, g_b_re, g_b_im, g_c_re, g_c_im = tables_vjp((d_k8, d_b8, d_c8, d_a8))
    du = du_scan + du_direct

    dq, dkk, dvv, dr, dal, g_wa2, g_ba2, g_ghn = _gla_bwd(proj, alow, wa2, small["b_a2"], small["gla_norm_g"], o_pre, states, doa,
                                                        dk=dk, dv=dv, name="gla_bwd")
    dproj = jnp.concatenate([dga, dgb, dvv, dr, dq, dkk, du], axis=1)
    mid = {"w_out": g_out, "w_branch_a": g_branch_a, "w_branch_b": g_branch_b, "w_glu": g_glu.astype(BF16),
           "w_a2": g_wa2[:GLA_RANK].astype(BF16)}
    if dist:
        axes = dict(_REST)
        g_main_t, got = _mm(dproj, h1, ta=True, out_dtype=BF16, name="g_w_in_main",
                            carry=("a2a", [_to_slots(mid[n], axes[n]) for n in mid]))
        recv.update(zip(mid, [[g] for g in got]))
    else:
        g_main_t = _mm(dproj, h1, ta=True, out_dtype=BF16, name="g_w_in_main")
    g_al_t = _mm(dal, h1, ta=True, out_dtype=BF16, name="g_w_in_gate_rank")
    mrows = lambda o, n: g_main_t[o:o + n]
    g_w_in_t = jnp.concatenate([mrows(2 * d + 2 * dv, dk), mrows(2 * d + 2 * dv + dk, dk), mrows(2 * d, dv), mrows(2 * d + dv, dv),
                                g_al_t[:GLA_RANK], mrows(2 * d + 2 * dv + 2 * dk, s5w), mrows(0, d), mrows(d, d)], axis=0)
    if dist:
        g_w_in_s = _to_slots(g_w_in_t, 0)
        dh1, (recv_w_in,) = _mm(dproj, w_main_t, carry=("a2a", [g_w_in_s], [_ALL_K[:-1]]), name="d_h1_main")
    else:
        dh1 = _mm(dproj, w_main_t, name="d_h1_main")
    dh1 = _mm(dal, w_al_t, res=dh1, name="d_h1_gate_rank")
    if dist:
        (grad_x, d_norm1_g), (recv_w_in_last,) = _rms_bwd(x, small["norm1_g"], dh1, dx1, "norm1_bwd", False,
                                                          carry=("a2a", [g_w_in_s], [_ALL_K[-1:]]))
        recv["w_in"] = [recv_w_in, recv_w_in_last]
    else:
        grad_x, d_norm1_g = _rms_bwd(x, small["norm1_g"], dh1, dx1, "norm1_bwd", False)

    small_g = {
        "norm1_g": d_norm1_g, "b_a2": g_ba2, "gla_norm_g": g_ghn, "lam_re": g_lam_re, "lam_im": g_lam_im,
        "log_dt": g_log_dt[None], "s5_b_re": g_b_re, "s5_b_im": g_b_im, "s5_c_re": g_c_re, "s5_c_im": g_c_im,
        "s5_d": g_s5d, "b_glu": g_bglu, "norm2_g": d_norm2_g, "final_norm_g": d_final_g,
    }
    if not dist:
        recv = dict(mid, w_in=g_w_in_t, w_ffn_in=_join_slots(g_ffn_in_s, 1), w_ffn_out=g_ffn_out)
    return loss[0, 0], grad_x, recv, small_g


def _small_2d(name, a):
    a = a[0]
    return a[None] if a.ndim == 1 else a


def kernel(x, norm1_g, w_in, w_a2, b_a2, gla_norm_g, lam_re, lam_im, log_dt, s5_b_re, s5_b_im, s5_c_re, s5_c_im, s5_d, w_glu, b_glu, w_branch_a, w_branch_b, w_out, norm2_g, w_ffn_in, w_ffn_out, final_norm_g, loss_target, m_norm1_g, m_w_in, m_w_a2, m_b_a2, m_gla_norm_g, m_lam_re, m_lam_im, m_log_dt, m_s5_b_re, m_s5_b_im, m_s5_c_re, m_s5_c_im, m_s5_d, m_w_glu, m_b_glu, m_w_branch_a, m_w_branch_b, m_w_out, m_norm2_g, m_w_ffn_in, m_w_ffn_out, m_final_norm_g, v_norm1_g, v_w_in, v_w_a2, v_b_a2, v_gla_norm_g, v_lam_re, v_lam_im, v_log_dt, v_s5_b_re, v_s5_b_im, v_s5_c_re, v_s5_c_im, v_s5_d, v_w_glu, v_b_glu, v_w_branch_a, v_w_branch_b, v_w_out, v_norm2_g, v_w_ffn_in, v_w_ffn_out, v_final_norm_g):
    args = dict(locals())
    weights = {n: args[n] for n in _ORDER}
    m_in = {n: args["m_" + n] for n in _ORDER}
    v_in = {n: args["v_" + n] for n in _ORDER}
    transposed = lambda a: a[0].T[None]
    rest = [weights[n][0].astype(BF16) for n, _ in _REST]
    small = {n: _small_2d(n, weights[n]) for n in _SMALL}
    loss_local, grad_x, recv, small_g = _local_step(x[0], loss_target[0], transposed(weights["w_in"])[0].astype(BF16), small, rest)

    grads, delta, new_m, new_v = {}, {}, {}, {}
    for n, _ in _REST:
        grads[n], delta[n], new_m[n], new_v[n] = _adamw(weights[n], recv[n], m_in[n], v_in[n], "adamw_" + n)
    w_in_out = _adamw(transposed(weights["w_in"]), recv["w_in"], transposed(m_in["w_in"]), transposed(v_in["w_in"]), "adamw_w_in")
    grads["w_in"], delta["w_in"], new_m["w_in"], new_v["w_in"] = (transposed(a) for a in w_in_out)

    s_sizes = [small_g[n].size for n in _SMALL]
    s_offs = [sum(s_sizes[:i]) for i in range(len(s_sizes))]
    s_total = sum(s_sizes)
    s_rows = -(-(-(-(s_total + 1) // LANE)) // LANE) * LANE

    def pack_small(parts):
        flat = jnp.concatenate([p.reshape(-1) for p in parts])
        return jnp.pad(flat, (0, s_rows * LANE - flat.size)).reshape(s_rows, LANE)

    s_flat = pack_small([small_g[n] for n in _SMALL] + [loss_local])
    s_red = _slot_sum(_exchange("ag", [s_flat], "small_grads_all_gather")[0], "small_grads_slot_sum")
    loss = s_red.reshape(-1)[s_total]
    sd, sm, sv = _adamw(pack_small([weights[n] for n in _SMALL])[None], s_red, pack_small([m_in[n] for n in _SMALL])[None],
                        pack_small([v_in[n] for n in _SMALL])[None], "adamw_small")
    sd, sm, sv = sd[0], sm[0], sv[0]
    for n, o, s in zip(_SMALL, s_offs, s_sizes):
        shape = weights[n].shape[1:]
        grads[n], delta[n], new_m[n], new_v[n] = (a.reshape(-1)[o:o + s].reshape(shape) for a in (s_red, sd, sm, sv))

    out = [loss, grad_x[None]]
    for tree in (grads, delta, new_m, new_v):
        out += [tree[n].reshape(weights[n].shape) for n in _ORDER]
    return tuple(out)
```

```python
import functools
import math

import jax
import jax.numpy as jnp
from jax import lax
from jax.experimental import pallas as pl
from jax.experimental.pallas import tpu as pltpu

F32 = jnp.float32
BF16 = jnp.bfloat16

NORM_EPS = 1e-6
N_DEV = 8
N_PEER = N_DEV - 1
GLA_HEADS = 4
GLA_CHUNK = 32
GLA_CHUNK_SHIFT = 5
GLA_TAU = 16.0
GLA_RANK = 16
GLA_BLOCK = 256
S5_GC = 16
S5_P = 64
S5_L = 16
S5_TILE_G = 8
S5_ROW_BLOCK = 2048
LANE = 128
V7X_VMEM_LIMIT = 56 * 1024 * 1024
V7X_MM_VMEM_BUDGET = 40 * 1024 * 1024
V7X_MM_TILE_MN = 1408
V7X_MM_TILE_MN_WHOLE_K = 512
V7X_MM_TILE_K = 2048
V7X_EPI_ROW_CHUNKS = 4

ADAM_LR = 0.001
ADAM_B1 = 0.9
ADAM_B2 = 0.999
ADAM_EPS = 1e-08
ADAM_WD = 0.01
ADAM_STEP = 10

GELU_C = math.sqrt(2.0 / math.pi)
GELU_A = 0.044715

MESH = pl.DeviceIdType.MESH


def _cparams(*sem):
    return pltpu.CompilerParams(dimension_semantics=sem, vmem_limit_bytes=V7X_VMEM_LIMIT)


def _divisors_down(n, start, align=LANE):
    t = (min(start, n) // align) * align
    found = False
    while t >= align:
        if n % t == 0:
            found = True
            yield t
        t -= align
    if not found:
        yield n


def _tile(n, target, align=LANE):
    return next(_divisors_down(n, target, align))


def _sigmoid(x):
    return 1.0 / (1.0 + jnp.exp(-x))


_HBM_SPEC = pl.BlockSpec(memory_space=pltpu.HBM)


def _exchange_scratch(n):
    return [pltpu.SemaphoreType.DMA((n * N_PEER,)), pltpu.SemaphoreType.DMA((n * N_PEER,)), pltpu.SemaphoreType.DMA((n,))]


def _ag_phases(x_refs, out_refs, send_sems, recv_sems, local_sems):
    n = len(x_refs)
    x, y, c = lax.axis_index("x"), lax.axis_index("y"), lax.axis_index("c")
    me, sibling = (x, y, c), (x, y, 1 - c)
    chips = [(1 - x, y), (x, 1 - y), (1 - x, 1 - y)]

    def copy(a, k, block, to, from_input=False):
        dst = out_refs[a].at[4 * block[0] + 2 * block[1] + block[2]]
        return pltpu.make_async_remote_copy(
            src_ref=x_refs[a] if from_input else dst, dst_ref=dst,
            send_sem=send_sems.at[a * N_PEER + k], recv_sem=recv_sems.at[a * N_PEER + k], device_id=to, device_id_type=MESH)

    def local(a):
        return pltpu.make_async_copy(x_refs[a], out_refs[a].at[4 * x + 2 * y + c], local_sems.at[a])

    def first(a):
        return [copy(a, 0, me, sibling, True)] + [copy(a, 1 + j, me, (*chip, c), True) for j, chip in enumerate(chips)]

    def start():
        for a in range(n):
            local(a).start()
            for cp in first(a):
                cp.start()

    def relay():
        for j, chip in enumerate(chips):
            for a in range(n):
                copy(a, 1 + j, (*chip, c), me).wait_recv()
                copy(a, 4 + j, (*chip, c), sibling).start()

    def finish():
        for a in range(n):
            copy(a, 0, sibling, me).wait_recv()
            for j, chip in enumerate(chips):
                copy(a, 4 + j, (*chip, 1 - c), me).wait_recv()
        for a in range(n):
            for cp in first(a) + [copy(a, 4 + j, (*chip, c), sibling) for j, chip in enumerate(chips)]:
                cp.wait_send()
            local(a).wait()

    return start, relay, finish


_ALL_K = tuple(range(N_DEV))


def _a2a_phases(x_refs, out_refs, send_sems, recv_sems, local_sems, ks_list=None):
    n = len(x_refs)
    ks_list = ks_list or [_ALL_K] * n
    x, y, c = lax.axis_index("x"), lax.axis_index("y"), lax.axis_index("c")
    my = 4 * x + 2 * y + c

    def copy(a, k):
        px, py, pc = (1 - x if k & 4 else x), (1 - y if k & 2 else y), (1 - c if k & 1 else c)
        return pltpu.make_async_remote_copy(
            src_ref=x_refs[a].at[4 * px + 2 * py + pc], dst_ref=out_refs[a].at[ks_list[a].index(k)],
            send_sem=send_sems.at[a * N_PEER + k - 1], recv_sem=recv_sems.at[a * N_PEER + k - 1],
            device_id=(px, py, pc), device_id_type=MESH)

    def local(a):
        return pltpu.make_async_copy(x_refs[a].at[my], out_refs[a].at[ks_list[a].index(0)], local_sems.at[a])

    def start():
        for a in range(n):
            for k in ks_list[a]:
                (copy(a, k) if k else local(a)).start()

    def relay():
        pass

    def finish():
        for a in range(n):
            for k in ks_list[a]:
                if k:
                    copy(a, k).wait_recv()
        for a in range(n):
            for k in ks_list[a]:
                if k:
                    copy(a, k).wait_send()
                else:
                    local(a).wait()

    return start, relay, finish


def _exchange_out_shapes(kind, arrays, ks_list=None):
    if kind == "ag":
        return [jax.ShapeDtypeStruct((N_DEV,) + a.shape, a.dtype) for a in arrays]
    ks_list = ks_list or [_ALL_K] * len(arrays)
    return [jax.ShapeDtypeStruct((len(ks),) + a.shape[1:], a.dtype) for a, ks in zip(arrays, ks_list)]


def _exchange(kind, arrays, name):
    n = len(arrays)
    phases = _ag_phases if kind == "ag" else _a2a_phases

    def body(*refs):
        start, relay, finish = phases(refs[:n], refs[n:2 * n], *refs[2 * n:])
        start()
        relay()
        finish()

    return pl.pallas_call(
        body, name=name,
        out_shape=_exchange_out_shapes(kind, arrays),
        in_specs=[_HBM_SPEC] * n, out_specs=[_HBM_SPEC] * n,
        scratch_shapes=_exchange_scratch(n),
    )(*arrays)


def _mm_tiles(m, n_unit, k_unit, tile_bytes, small_tiles_ok=True):
    fits = lambda tm, tn, tk: 2 * 2 * (tm * tk + tk * tn) + tile_bytes * tm * tn <= V7X_MM_VMEM_BUDGET
    for cap in (V7X_MM_TILE_MN, V7X_MM_TILE_MN_WHOLE_K) if small_tiles_ok else (V7X_MM_TILE_MN,):
        tm, tn = _tile(m, cap), _tile(n_unit, cap)
        if fits(tm, tn, k_unit) and (tn >= V7X_MM_TILE_MN_WHOLE_K or tn == n_unit):
            return tm, tn, k_unit
    tm, tn = _tile(m, V7X_MM_TILE_MN), _tile(n_unit, V7X_MM_TILE_MN)
    for tk in _divisors_down(k_unit, V7X_MM_TILE_K):
        if fits(tm, tn, tk):
            return tm, tn, tk
    return tm, tn, _tile(k_unit, LANE)


def _carry_parts(carry):
    kind, arrays, ks_list = (tuple(carry) + (None,))[:3] if carry is not None else (None, [], None)
    n = len(arrays)
    kind = (kind, ks_list)
    return kind, arrays, [_HBM_SPEC] * n, _exchange_out_shapes(kind[0], arrays, ks_list), (_exchange_scratch(n) if n else [])


def _carry_hooks(kind, x_refs, out_refs, sems, step, last_step):
    if not x_refs:
        return lambda: None
    kind, ks_list = kind
    if kind == "ag":
        start, relay, finish = _ag_phases(x_refs, out_refs, *sems)
    else:
        start, relay, finish = _a2a_phases(x_refs, out_refs, *sems, ks_list=ks_list)
    pl.when(step == 0)(start)

    def after():
        if kind == "ag":
            pl.when(step == (last_step * 7) // 8)(relay)
        pl.when(step == last_step)(finish)

    return after


def _mm(a, b, *, ta=False, tb=False, out_dtype=F32, res=None, carry=None, a_slots=False, b_slots=False, b_group=0,
        out_slots=0, epi=None, name):
    if a_slots:
        assert not ta
        a_n, m, a_c = a.shape
        k = a_n * a_c
    else:
        m, k = (a.shape[1], a.shape[0]) if ta else a.shape
    if b_slots:
        b_n, b_r, b_c = b.shape
        k2, n = (b_n * b_c, b_r) if tb else (b_r, b_n * b_c)
    else:
        k2, n = (b.shape[1], b.shape[0]) if tb else b.shape
    assert k == k2, (a.shape, b.shape, ta, tb)
    has_res = res is not None
    assert not (has_res and (out_slots or epi))
    n_units = [n] + ([n // out_slots] if out_slots else []) + ([b_c] if b_slots and not tb else [])
    k_units = [k] + ([a_c] if a_slots else []) + ([b_c] if b_slots and tb else [])
    n_unit, k_unit = min(n_units), min(k_units)
    assert all(u % n_unit == 0 for u in n_units) and all(u % k_unit == 0 for u in k_units)
    epi_fn, epi_ins, epi_outs = epi if epi is not None else (None, [], [])
    tile_bytes = 4 + (2 * res.dtype.itemsize if has_res else 0)
    tile_bytes += sum(2 * e.shape[0] * e.dtype.itemsize for e in epi_ins)
    tile_bytes += sum(2 * l * jnp.dtype(dt).itemsize for l, dt in epi_outs) if epi else 2 * jnp.dtype(out_dtype).itemsize
    tm, tn, tk = _mm_tiles(m, n_unit, k_unit, tile_bytes, small_tiles_ok=not epi)
    if b_group:
        tk = b_group * b_c
        assert b_slots and tb and k % tk == 0 and (not a_slots or a_c % tk == 0)
    ni, nj, nk = m // tm, n // tn, k // tk
    dims = (((0,) if ta else (1,), (1,) if tb else (0,)), ((), ()))

    def slot_map(per, pos):
        if pos == "k_cols":
            return lambda i, j, kk: (kk // per, i, kk % per)
        if pos == "k_cols_j":
            return lambda i, j, kk: (kk // per, j, kk % per)
        if pos == "n_cols_k":
            return lambda i, j, kk: (j // per, kk, j % per)
        return lambda i, j, kk: (j // per, i, j % per)

    if a_slots:
        a_spec = pl.BlockSpec((None, tm, tk), slot_map(a_c // tk, "k_cols"))
    else:
        a_spec = pl.BlockSpec((tk, tm), lambda i, j, kk: (kk, i)) if ta else pl.BlockSpec((tm, tk), lambda i, j, kk: (i, kk))
    if b_group:
        b_spec = pl.BlockSpec((b_group, tn, b_c), lambda i, j, kk: (kk, j, 0))
    elif b_slots and tb:
        b_spec = pl.BlockSpec((None, tn, tk), slot_map(b_c // tk, "k_cols_j"))
    elif b_slots:
        b_spec = pl.BlockSpec((None, tk, tn), slot_map(b_c // tn, "n_cols_k"))
    else:
        b_spec = pl.BlockSpec((tn, tk), lambda i, j, kk: (j, kk)) if tb else pl.BlockSpec((tk, tn), lambda i, j, kk: (kk, j))
    if epi:
        lead_spec = lambda l: pl.BlockSpec((l, tm, tn), lambda i, j, kk: (0, i, j))
        o_specs = [lead_spec(l) for l, _ in epi_outs]
        o_shapes = [jax.ShapeDtypeStruct((l, m, n), dt) for l, dt in epi_outs]
    elif out_slots:
        o_specs = [pl.BlockSpec((None, tm, tn), slot_map((n // out_slots) // tn, "n_cols_i"))]
        o_shapes = [jax.ShapeDtypeStruct((out_slots, m, n // out_slots), out_dtype)]
    else:
        o_specs = [pl.BlockSpec((tm, tn), lambda i, j, kk: (i, j))]
        o_shapes = [jax.ShapeDtypeStruct((m, n), out_dtype)]
    extra_ins = ([res] if has_res else []) + list(epi_ins)
    extra_specs = ([o_specs[0]] if has_res else []) + [pl.BlockSpec((e.shape[0], tm, tn), lambda i, j, kk: (0, i, j)) for e in epi_ins]
    n_in, n_out = 2 + len(extra_ins), len(o_specs)
    c_kind, c_arrays, c_specs, c_shapes, c_scratch = _carry_parts(carry)
    nc = len(c_arrays)
    last_step = ni * nj * nk - 1

    def body(*refs):
        a_ref, b_ref = refs[0], refs[1]
        e_refs = refs[2:n_in]
        x_refs = refs[n_in:n_in + nc]
        o_refs = refs[n_in + nc:n_in + nc + n_out]
        out_refs = refs[n_in + nc + n_out:n_in + 2 * nc + n_out]
        scratch = refs[n_in + 2 * nc + n_out:]
        acc = scratch[0] if nk > 1 else None
        kk = pl.program_id(2)
        step = (pl.program_id(0) * nj + pl.program_id(1)) * nk + kk
        after = _carry_hooks(c_kind, x_refs, out_refs, scratch[-3:], step, last_step)

        def emit(val):
            if has_res:
                val = val + e_refs[0][...].astype(F32)
            if epi:
                for o_ref, parts in zip(o_refs, epi_fn(val, *[e[...] for e in e_refs])):
                    for l, v in enumerate(parts):
                        o_ref[l] = v.astype(o_ref.dtype)
            else:
                o_refs[0][...] = val.astype(out_dtype)

        if epi and nk == 1 and not ta:
            rc = tm // V7X_EPI_ROW_CHUNKS
            for r in range(V7X_EPI_ROW_CHUNKS):
                rows = slice(r * rc, (r + 1) * rc)
                val = lax.dot_general(a_ref[rows, :], b_ref[...], dims, preferred_element_type=F32)
                for o_ref, parts in zip(o_refs, epi_fn(val, *[e[:, rows, :] for e in e_refs])):
                    for l, v in enumerate(parts):
                        o_ref[l, rows, :] = v.astype(o_ref.dtype)
            after()
            return
        if b_group:
            part = sum(lax.dot_general(a_ref[:, s * b_c:(s + 1) * b_c], b_ref[s], dims, preferred_element_type=F32)
                       for s in range(b_group))
        else:
            part = lax.dot_general(a_ref[...], b_ref[...], dims, preferred_element_type=F32)
        if nk == 1:
            emit(part)
        else:
            @pl.when(kk == 0)
            def _():
                acc[...] = part

            @pl.when(kk > 0)
            def _():
                acc[...] += part

            @pl.when(kk == nk - 1)
            def _():
                emit(acc[...])

        after()

    sem = ("arbitrary",) * 3 if nc else ("parallel", "parallel", "arbitrary")
    outs = pl.pallas_call(
        body, name=name,
        grid=(ni, nj, nk),
        in_specs=[a_spec, b_spec] + extra_specs + c_specs,
        out_specs=o_specs + c_specs,
        out_shape=o_shapes + c_shapes,
        scratch_shapes=([pltpu.VMEM((tm, tn), F32)] if nk > 1 else []) + c_scratch,
        compiler_params=_cparams(*sem),
    )(a, b, *extra_ins, *c_arrays)
    main = list(outs[:n_out]) if epi else outs[0]
    return (main, list(outs[n_out:])) if nc else main


def _ffn_in_fused(h2, w_s, *, carry=None, name):
    t, d = h2.shape
    n_slot, _, c = w_s.shape
    half = n_slot // 2
    tm = _tile(t, 512)
    c_kind, c_arrays, c_specs, c_shapes, c_scratch = _carry_parts(carry)
    nc = len(c_arrays)
    last_step = (t // tm) * half - 1

    def body(h_ref, wg_ref, wu_ref, *refs):
        x_refs, (gu_ref, act_ref), out_refs, sems = refs[:nc], refs[nc:nc + 2], refs[nc + 2:2 * nc + 2], refs[2 * nc + 2:]
        step = pl.program_id(0) * half + pl.program_id(1)
        after = _carry_hooks(c_kind, x_refs, out_refs, sems, step, last_step)
        h = h_ref[...]
        g = jnp.dot(h, wg_ref[...], preferred_element_type=F32)
        u = jnp.dot(h, wu_ref[...], preferred_element_type=F32)
        sg = _sigmoid(g)
        silu = g * sg
        gu_ref[0] = (u * (sg + silu - silu * sg)).astype(BF16)
        gu_ref[1] = silu.astype(BF16)
        act_ref[...] = (silu * u).astype(BF16)
        after()

    outs = pl.pallas_call(
        body, name=name,
        grid=(t // tm, half),
        in_specs=[pl.BlockSpec((tm, d), lambda i, j: (i, 0)),
                  pl.BlockSpec((None, d, c), lambda i, j: (j, 0, 0)),
                  pl.BlockSpec((None, d, c), lambda i, j: (half + j, 0, 0))] + c_specs,
        out_specs=[pl.BlockSpec((2, tm, c), lambda i, j: (0, i, j)), pl.BlockSpec((tm, c), lambda i, j: (i, j))] + c_specs,
        out_shape=[jax.ShapeDtypeStruct((2, t, half * c), BF16), jax.ShapeDtypeStruct((t, half * c), BF16)] + c_shapes,
        scratch_shapes=c_scratch,
        compiler_params=_cparams(*(("arbitrary",) * 2 if nc else ("parallel", "parallel"))),
    )(h2, w_s, w_s, *c_arrays)
    return (outs[0], outs[1], list(outs[2:])) if nc else (outs[0], outs[1])


def _rowwise(fn, ins, row_outs, acc_outs, *, rows, tb, name, carry=None):
    in_specs, args = [], []
    for spec in ins:
        kind, arr = spec[0], spec[1]
        if kind == "row":
            in_specs.append(pl.BlockSpec((tb, arr.shape[1]), lambda i: (i, 0)))
        elif kind == "win":
            width, cb = spec[2], spec[3]
            in_specs.append(pl.BlockSpec((tb, width), functools.partial(lambda i, cb: (i, cb), cb=cb)))
        else:
            in_specs.append(pl.BlockSpec(arr.shape, lambda i: (0, 0)))
        args.append(arr)
    out_specs = [pl.BlockSpec((tb, c), lambda i: (i, 0)) for c, _ in row_outs]
    out_specs += [pl.BlockSpec(shape, lambda i: (0, 0)) for shape in acc_outs]
    out_shape = [jax.ShapeDtypeStruct((rows, c), dt) for c, dt in row_outs]
    out_shape += [jax.ShapeDtypeStruct(shape, F32) for shape in acc_outs]
    n_in, n_row, n_out = len(ins), len(row_outs), len(row_outs) + len(acc_outs)
    c_kind, c_arrays, c_specs, c_shapes, c_scratch = _carry_parts(carry)
    nc = len(c_arrays)

    def body(*refs):
        after = _carry_hooks(c_kind, refs[n_in:n_in + nc], refs[n_in + nc + n_out:n_in + 2 * nc + n_out],
                             refs[n_in + 2 * nc + n_out:], pl.program_id(0), rows // tb - 1)
        vals = [r[...] for r in refs[:n_in]]
        outs = fn(*vals)
        if not isinstance(outs, (tuple, list)):
            outs = (outs,)
        out_refs = refs[n_in + nc:n_in + nc + n_out]
        for o_ref, val in zip(out_refs[:n_row], outs[:n_row]):
            o_ref[...] = val.astype(o_ref.dtype)
        first = pl.program_id(0) == 0
        for o_ref, val in zip(out_refs[n_row:], outs[n_row:]):
            @pl.when(first)
            def _(o_ref=o_ref):
                o_ref[...] = jnp.zeros_like(o_ref)
            o_ref[...] += val
        after()

    res = pl.pallas_call(
        body, name=name,
        grid=(rows // tb,),
        in_specs=in_specs + c_specs, out_specs=out_specs + c_specs, out_shape=out_shape + c_shapes,
        scratch_shapes=c_scratch,
        compiler_params=_cparams("arbitrary"),
    )(*args, *c_arrays)
    return (list(res[:n_out]), list(res[n_out:])) if nc else res


def _rms_fwd(x, g, name, carry=None):
    def fn(xv, gv):
        r = lax.rsqrt(jnp.mean(xv * xv, axis=-1, keepdims=True) + NORM_EPS)
        return (xv * r * gv,)
    res = _rowwise(fn, [("row", x), ("full", g)], [(x.shape[1], BF16)], [], rows=x.shape[0], tb=_tile(x.shape[0], 512),
                   name=name, carry=carry)
    return (res[0][0], res[1]) if carry is not None else res[0]


def _rms_bwd(x, g, dh, dres, name, want_bf16, carry=None):
    d = x.shape[1]

    def fn(xv, gv, dhv, drv):
        r = lax.rsqrt(jnp.mean(xv * xv, axis=-1, keepdims=True) + NORM_EPS)
        xhat = xv * r
        dhv = dhv.astype(F32)
        dxhat = dhv * gv
        dx = drv + r * (dxhat - xhat * jnp.mean(dxhat * xhat, axis=-1, keepdims=True))
        dg = jnp.sum(dhv * xhat, axis=0, keepdims=True)
        return (dx, dx, dg) if want_bf16 else (dx, dg)

    row_outs = [(d, F32), (d, BF16)] if want_bf16 else [(d, F32)]
    return _rowwise(fn, [("row", x), ("full", g), ("row", dh), ("row", dres)], row_outs, [(1, d)],
                    rows=x.shape[0], tb=_tile(x.shape[0], 256), name=name, carry=carry)


def _loss_head(x2, g, target, name):
    d = x2.shape[1]

    def fn(xv, gv, tv):
        r = lax.rsqrt(jnp.mean(xv * xv, axis=-1, keepdims=True) + NORM_EPS)
        xhat = xv * r
        diff = xhat * gv - tv
        loss = 0.5 * jnp.sum(jnp.mean(diff * diff, axis=-1, keepdims=True), axis=0, keepdims=True)
        dy = diff * (1.0 / d)
        dxhat = dy * gv
        dx = r * (dxhat - xhat * jnp.mean(dxhat * xhat, axis=-1, keepdims=True))
        dg = jnp.sum(dy * xhat, axis=0, keepdims=True)
        return dx, dx, dg, jnp.broadcast_to(loss, (1, LANE))

    return _rowwise(fn, [("row", x2), ("full", g), ("row", target)], [(d, F32), (d, BF16)], [(1, d), (1, LANE)],
                    rows=x2.shape[0], tb=_tile(x2.shape[0], 256), name=name)


def _swiglu_bwd_tile(dact, dswiglu):
    return ((dact * dswiglu[0].astype(F32), dact * dswiglu[1].astype(F32)),)


def _mix_fwd(proj, pa, pb, d, name):
    def fn(ga, gb, av, bv):
        return (_sigmoid(ga.astype(F32)) * av.astype(F32) + _sigmoid(gb.astype(F32)) * bv.astype(F32),)
    return _rowwise(fn, [("win", proj, d, 0), ("win", proj, d, 1), ("row", pa), ("row", pb)], [(d, BF16)], [],
                    rows=pa.shape[0], tb=_tile(pa.shape[0], 512), name=name)[0]


def _mix_bwd(proj, pa, pb, dmix, d, name):
    def fn(ga, gb, av, bv, dm):
        dm = dm.astype(F32)
        sa, sb = _sigmoid(ga.astype(F32)), _sigmoid(gb.astype(F32))
        av, bv = av.astype(F32), bv.astype(F32)
        return dm * sa, dm * sb, dm * av * sa * (1.0 - sa), dm * bv * sb * (1.0 - sb)
    return _rowwise(fn, [("win", proj, d, 0), ("win", proj, d, 1), ("row", pa), ("row", pb), ("row", dmix)],
                    [(d, BF16)] * 4, [], rows=pa.shape[0], tb=_tile(pa.shape[0], 512), name=name)


def _chunk_masks(tb):
    r = lax.broadcasted_iota(jnp.int32, (tb, tb), 0)
    c = lax.broadcasted_iota(jnp.int32, (tb, tb), 1)
    same = lax.shift_right_logical(r, GLA_CHUNK_SHIFT) == lax.shift_right_logical(c, GLA_CHUNK_SHIFT)
    return same, same & (c <= r), same & (r <= c)


def _mask_bf16(mask):
    return jnp.where(mask, 1.0, 0.0).astype(BF16)


def _split_dot(mask_bf, x, terms):
    acc, rem = None, x
    for _ in range(terms):
        hi = rem.astype(BF16)
        part = jnp.dot(mask_bf, hi, preferred_element_type=F32)
        acc = part if acc is None else acc + part
        rem = rem - hi.astype(F32)
    return acc


def _gla_decay(al, wa2, ba2, same_bf, causal_bf):
    z = jnp.dot(al.astype(BF16), wa2, preferred_element_type=F32) + ba2
    la = (jnp.minimum(z, 0.0) - jnp.log(1.0 + jnp.exp(-jnp.abs(z)))) * (1.0 / GLA_TAU)
    bc = _split_dot(causal_bf, la, 3)
    bl = _split_dot(same_bf, la, 3)
    return z, bc, bl


def _dot_t(a, b, ca, cb):
    return lax.dot_general(a, b, (((ca,), (cb,)), ((), ())), preferred_element_type=F32)


def _gla_fwd(proj, alow, wa2, ba2, ghn, *, dk, dv, name):
    t = proj.shape[0]
    tb = min(GLA_BLOCK, t)
    nch = tb // GLA_CHUNK
    hk, hv = dk // GLA_HEADS, dv // GLA_HEADS
    scale = hk ** -0.5
    v_cb, r_cb = (8 * dk) // dv, (8 * dk) // dv + 1
    q_cb, k_cb = (8 * dk + 2 * dv) // dk, (8 * dk + 2 * dv) // dk + 1

    def body(q_ref, k_ref, v_ref, r_ref, al_ref, wa2_ref, ba2_ref, ghn_ref, oa_ref, opre_ref, s_ref, st_scr):
        @pl.when(pl.program_id(0) == 0)
        def _():
            st_scr[...] = jnp.zeros_like(st_scr)

        same, causal, _ = _chunk_masks(tb)
        same_bf, causal_bf = _mask_bf16(same), _mask_bf16(causal)
        _, bc, bl = _gla_decay(al_ref[...], wa2_ref[...], ba2_ref[...], same_bf, causal_bf)
        q = q_ref[...].astype(F32) * scale
        k = k_ref[...].astype(F32)
        qd = (q * jnp.exp(bc)).astype(BF16)
        ki = (k * jnp.exp(-bc)).astype(BF16)
        ks = (k * jnp.exp(bl - bc)).astype(BF16)
        dl = jnp.exp(bl)
        ksls = [slice(h * hk, (h + 1) * hk) for h in range(GLA_HEADS)]
        vsls = [slice(h * hv, (h + 1) * hv) for h in range(GLA_HEADS)]
        v_hs = [v_ref[:, vsl] for vsl in vsls]
        o_intras = []
        for ksl, v_h in zip(ksls, v_hs):
            sc = jnp.where(causal, _dot_t(qd[:, ksl], ki[:, ksl], 1, 1), 0.0)
            o_intras.append(jnp.dot(sc.astype(BF16), v_h, preferred_element_type=F32))
        for c in range(nch):
            rows = slice(c * GLA_CHUNK, (c + 1) * GLA_CHUNK)
            for h, (ksl, vsl) in enumerate(zip(ksls, vsls)):
                st = st_scr[h]
                s_ref[c, h] = st
                opre_ref[rows, vsl] = o_intras[h][rows] + _dot_t(qd[rows, ksl], st.astype(BF16), 1, 1)
                st_scr[h] = dl[c * GLA_CHUNK:c * GLA_CHUNK + 1, ksl] * st + _dot_t(v_hs[h][rows], ks[rows, ksl], 0, 0)
        for h in range(GLA_HEADS):
            vsl = slice(h * hv, (h + 1) * hv)
            o = opre_ref[:, vsl]
            rs = lax.rsqrt(jnp.mean(o * o, axis=-1, keepdims=True) + NORM_EPS)
            rv = r_ref[:, vsl].astype(F32)
            oa_ref[:, vsl] = (rv * _sigmoid(rv) * (o * rs * ghn_ref[:, vsl])).astype(BF16)

    nchunks = t // GLA_CHUNK
    return pl.pallas_call(
        body, name=name,
        grid=(t // tb,),
        in_specs=[
            pl.BlockSpec((tb, dk), lambda i: (i, q_cb)),
            pl.BlockSpec((tb, dk), lambda i: (i, k_cb)),
            pl.BlockSpec((tb, dv), lambda i: (i, v_cb)),
            pl.BlockSpec((tb, dv), lambda i: (i, r_cb)),
            pl.BlockSpec((tb, LANE), lambda i: (i, 0)),
            pl.BlockSpec(wa2.shape, lambda i: (0, 0)),
            pl.BlockSpec(ba2.shape, lambda i: (0, 0)),
            pl.BlockSpec(ghn.shape, lambda i: (0, 0)),
        ],
        out_specs=[
            pl.BlockSpec((tb, dv), lambda i: (i, 0)),
            pl.BlockSpec((tb, dv), lambda i: (i, 0)),
            pl.BlockSpec((nch, GLA_HEADS, hv, hk), lambda i: (i, 0, 0, 0)),
        ],
        out_shape=[
            jax.ShapeDtypeStruct((t, dv), BF16),
            jax.ShapeDtypeStruct((t, dv), F32),
            jax.ShapeDtypeStruct((nchunks, GLA_HEADS, hv, hk), F32),
        ],
        scratch_shapes=[pltpu.VMEM((GLA_HEADS, hv, hk), F32)],
        compiler_params=_cparams("arbitrary"),
    )(proj, proj, proj, proj, alow, wa2, ba2, ghn)


def _gla_bwd(proj, alow, wa2, ba2, ghn, opre, states, doa, *, dk, dv, name):
    t = proj.shape[0]
    tb = min(GLA_BLOCK, t)
    nb = t // tb
    nch = tb // GLA_CHUNK
    hk, hv = dk // GLA_HEADS, dv // GLA_HEADS
    scale = hk ** -0.5
    v_cb, r_cb = (8 * dk) // dv, (8 * dk) // dv + 1
    q_cb, k_cb = (8 * dk + 2 * dv) // dk, (8 * dk + 2 * dv) // dk + 1

    def body(q_ref, k_ref, v_ref, r_ref, al_ref, wa2_ref, ba2_ref, ghn_ref, opre_ref, s_ref, doa_ref,
             dq_ref, dk_ref, dv_ref, dr_ref, dal_ref, dwa2_ref, dba2_ref, dghn_ref,
             dst_scr, dqd_scr, dki_scr, dks_scr, ddl_scr):
        @pl.when(pl.program_id(0) == 0)
        def _():
            dst_scr[...] = jnp.zeros_like(dst_scr)
            dwa2_ref[...] = jnp.zeros_like(dwa2_ref)
            dba2_ref[...] = jnp.zeros_like(dba2_ref)
            dghn_ref[...] = jnp.zeros_like(dghn_ref)

        same, causal, anti = _chunk_masks(tb)
        same_bf, causal_bf, anti_bf = _mask_bf16(same), _mask_bf16(causal), _mask_bf16(anti)
        al = al_ref[...]
        wa2v = wa2_ref[...]
        z, bc, bl = _gla_decay(al, wa2v, ba2_ref[...], same_bf, causal_bf)
        e_bc, e_nbc, e_st = jnp.exp(bc), jnp.exp(-bc), jnp.exp(bl - bc)
        q = q_ref[...].astype(F32) * scale
        k = k_ref[...].astype(F32)
        qd_f, ki_f, ks_f = q * e_bc, k * e_nbc, k * e_st
        qd, ki, ks = qd_f.astype(BF16), ki_f.astype(BF16), ks_f.astype(BF16)
        dl = jnp.exp(bl)
        per_head = []
        for h in range(GLA_HEADS):
            ksl = slice(h * hk, (h + 1) * hk)
            vsl = slice(h * hv, (h + 1) * hv)
            o = opre_ref[:, vsl]
            rs = lax.rsqrt(jnp.mean(o * o, axis=-1, keepdims=True) + NORM_EPS)
            ohat = o * rs
            g_h = ghn_ref[:, vsl]
            rv = r_ref[:, vsl].astype(F32)
            sg = _sigmoid(rv)
            d_oa = doa_ref[:, vsl].astype(F32)
            don = d_oa * (rv * sg)
            dr_ref[:, vsl] = (d_oa * (ohat * g_h) * (sg * (1.0 + rv * (1.0 - sg)))).astype(BF16)
            dghn_ref[:, vsl] += jnp.sum(don * ohat, axis=0, keepdims=True)
            dohat = don * g_h
            do_f = rs * (dohat - ohat * jnp.mean(dohat * ohat, axis=-1, keepdims=True))
            do = do_f.astype(BF16)
            v_h = v_ref[:, vsl]
            p = jnp.where(causal, _dot_t(do, v_h, 1, 1), 0.0).astype(BF16)
            dqd_intra = jnp.dot(p, ki[:, ksl], preferred_element_type=F32)
            dki_scr[:, ksl] = _dot_t(p, qd[:, ksl], 0, 0)
            sc = jnp.where(causal, _dot_t(qd[:, ksl], ki[:, ksl], 1, 1), 0.0).astype(BF16)
            dv_intra = _dot_t(sc, do, 0, 0)
            per_head.append((ksl, vsl, v_h, do, dqd_intra, dv_intra))
        for c in reversed(range(nch)):
            rows = slice(c * GLA_CHUNK, (c + 1) * GLA_CHUNK)
            for h, (ksl, vsl, v_h, do, dqd_intra, dv_intra) in enumerate(per_head):
                dst = dst_scr[h]
                st = s_ref[c, h]
                dst_bf = dst.astype(BF16)
                dv_ref[rows, vsl] = (dv_intra[rows] + _dot_t(ks[rows, ksl], dst_bf, 1, 1)).astype(BF16)
                dks_scr[rows, ksl] = jnp.dot(v_h[rows], dst_bf, preferred_element_type=F32)
                dl_c = dl[c * GLA_CHUNK:c * GLA_CHUNK + 1, ksl]
                ddl = jnp.sum(dst * st, axis=0, keepdims=True) * dl_c
                ddl_scr[rows, ksl] = jnp.broadcast_to(ddl, (GLA_CHUNK, hk))
                dqd_scr[rows, ksl] = dqd_intra[rows] + jnp.dot(do[rows], st.astype(BF16), preferred_element_type=F32)
                dst_scr[h] = dl_c * dst + _dot_t(do[rows], qd[rows, ksl], 0, 0)
        dqd, dki, dks = dqd_scr[...], dki_scr[...], dks_scr[...]
        dq_ref[...] = (dqd * (scale * e_bc)).astype(BF16)
        dk_ref[...] = (dki * e_nbc + dks * e_st).astype(BF16)
        dks_ks = dks * ks_f
        dbc = dqd * qd_f - dki * ki_f - dks_ks
        dla = _split_dot(anti_bf, dbc, 2) + _split_dot(same_bf, dks_ks, 2) + ddl_scr[...]
        dz = (dla * (1.0 / GLA_TAU) * (1.0 - _sigmoid(z)))
        dz_bf = dz.astype(BF16)
        dal_ref[...] = _dot_t(dz_bf, wa2v, 1, 1).astype(BF16)
        dwa2_ref[...] += _dot_t(al.astype(BF16), dz_bf, 0, 0)
        dba2_ref[...] += jnp.sum(dz, axis=0, keepdims=True)

    rev = lambda i: nb - 1 - i
    return pl.pallas_call(
        body, name=name,
        grid=(nb,),
        in_specs=[
            pl.BlockSpec((tb, dk), lambda i: (rev(i), q_cb)),
            pl.BlockSpec((tb, dk), lambda i: (rev(i), k_cb)),
            pl.BlockSpec((tb, dv), lambda i: (rev(i), v_cb)),
            pl.BlockSpec((tb, dv), lambda i: (rev(i), r_cb)),
            pl.BlockSpec((tb, LANE), lambda i: (rev(i), 0)),
            pl.BlockSpec(wa2.shape, lambda i: (0, 0)),
            pl.BlockSpec(ba2.shape, lambda i: (0, 0)),
            pl.BlockSpec(ghn.shape, lambda i: (0, 0)),
            pl.BlockSpec((tb, dv), lambda i: (rev(i), 0)),
            pl.BlockSpec((nch, GLA_HEADS, hv, hk), lambda i: (rev(i), 0, 0, 0)),
            pl.BlockSpec((tb, dv), lambda i: (rev(i), 0)),
        ],
        out_specs=[
            pl.BlockSpec((tb, dk), lambda i: (rev(i), 0)),
            pl.BlockSpec((tb, dk), lambda i: (rev(i), 0)),
            pl.BlockSpec((tb, dv), lambda i: (rev(i), 0)),
            pl.BlockSpec((tb, dv), lambda i: (rev(i), 0)),
            pl.BlockSpec((tb, LANE), lambda i: (rev(i), 0)),
            pl.BlockSpec(wa2.shape, lambda i: (0, 0)),
            pl.BlockSpec(ba2.shape, lambda i: (0, 0)),
            pl.BlockSpec(ghn.shape, lambda i: (0, 0)),
        ],
        out_shape=[
            jax.ShapeDtypeStruct((t, dk), BF16),
            jax.ShapeDtypeStruct((t, dk), BF16),
            jax.ShapeDtypeStruct((t, dv), BF16),
            jax.ShapeDtypeStruct((t, dv), BF16),
            jax.ShapeDtypeStruct((t, LANE), BF16),
            jax.ShapeDtypeStruct(wa2.shape, F32),
            jax.ShapeDtypeStruct(ba2.shape, F32),
            jax.ShapeDtypeStruct(ghn.shape, F32),
        ],
        scratch_shapes=[pltpu.VMEM((GLA_HEADS, hv, hk), F32)] + [pltpu.VMEM((tb, dk), F32)] * 4,
        compiler_params=_cparams("arbitrary"),
    )(proj, proj, proj, proj, alow, wa2, ba2, ghn, opre, states, doa)


def _s5_tables(lam_re, lam_im, log_dt, b_re, b_im, c_re, c_im):
    hp = lax.Precision.HIGHEST
    g, p = lam_re.shape
    ln = S5_L
    dt = jnp.exp(log_dt)[:, None]
    lr, li = lam_re, lam_im
    mag = jnp.exp(lr * dt)
    ar, ai = mag * jnp.cos(li * dt), mag * jnp.sin(li * dt)
    den = lr * lr + li * li
    am1 = ar - 1.0
    f_re = ((am1 * lr + ai * li) / den)[..., None]
    f_im = ((ai * lr - am1 * li) / den)[..., None]
    bb_re = f_re * b_re - f_im * b_im
    bb_im = f_re * b_im + f_im * b_re
    j = jnp.arange(ln + 1, dtype=F32)[None, :, None]
    pm = jnp.exp(j * (lr * dt)[:, None, :])
    ang = j * (li * dt)[:, None, :]
    pw_re, pw_im = pm * jnp.cos(ang), pm * jnp.sin(ang)
    cp_re = c_re[:, None] * pw_re[:, :, None, :] - c_im[:, None] * pw_im[:, :, None, :]
    cp_im = c_re[:, None] * pw_im[:, :, None, :] + c_im[:, None] * pw_re[:, :, None, :]
    kj = (jnp.einsum("gjcp,gpd->gjcd", cp_re[:, :ln], bb_re, precision=hp)
          - jnp.einsum("gjcp,gpd->gjcd", cp_im[:, :ln], bb_im, precision=hp))
    eye = jnp.eye(S5_TILE_G, dtype=F32)
    nt = g // S5_TILE_G
    k8 = jnp.einsum("jglcd,gh->jlgdhc", kj.reshape(nt, S5_TILE_G, ln, S5_GC, S5_GC), eye).reshape(nt, ln, LANE, LANE)
    rp_re, rp_im = pw_re[:, ln - 1::-1], pw_im[:, ln - 1::-1]
    bbt_re, bbt_im = bb_re.transpose(0, 2, 1)[:, None], bb_im.transpose(0, 2, 1)[:, None]
    bst = jnp.stack([rp_re[:, :, None, :] * bbt_re - rp_im[:, :, None, :] * bbt_im,
                     rp_re[:, :, None, :] * bbt_im + rp_im[:, :, None, :] * bbt_re], axis=3)
    bc = bst.reshape(nt, S5_TILE_G, ln, S5_GC, 2 * p).transpose(0, 2, 1, 3, 4).reshape(nt, ln, LANE, 2 * p)
    cst = jnp.stack([cp_re[:, 1:], -cp_im[:, 1:]], axis=2)
    cc = cst.reshape(nt, S5_TILE_G, ln, 2, S5_GC, p).transpose(0, 2, 3, 5, 1, 4).reshape(nt, ln, 2 * p, LANE)
    a8 = jnp.stack([jnp.concatenate([pw_re[:, ln], pw_re[:, ln]], axis=-1),
                    jnp.concatenate([-pw_im[:, ln], pw_im[:, ln]], axis=-1)], axis=1)
    a8 = a8.reshape(nt, S5_TILE_G, 2, 2 * p).transpose(0, 2, 1, 3).reshape(nt, 2, S5_TILE_G * 2 * p)
    return k8, bc, cc, a8


def _swap_re_im(x):
    w = x.shape[1]
    if w == LANE:
        return pltpu.roll(x, LANE // 2, 1)
    first_half = (lax.broadcasted_iota(jnp.int32, x.shape, 1) & (LANE // 2)) == 0
    return jnp.where(first_half, pltpu.roll(x, w - LANE // 2, 1), pltpu.roll(x, LANE // 2, 1))


def _s5_expand(bc, cc, w):
    reps = w // LANE
    mask_b = (lax.broadcasted_iota(jnp.int32, (LANE, w), 0) // S5_GC) == (lax.broadcasted_iota(jnp.int32, (LANE, w), 1) // LANE)
    mask_c = (lax.broadcasted_iota(jnp.int32, (w, LANE), 0) // LANE) == (lax.broadcasted_iota(jnp.int32, (w, LANE), 1) // S5_GC)
    b8 = None if bc is None else jnp.where(mask_b, jnp.concatenate([bc] * reps, axis=1), jnp.zeros((), bc.dtype))
    c8 = None if cc is None else jnp.where(mask_c, jnp.concatenate([cc] * reps, axis=0), jnp.zeros((), cc.dtype))
    return b8, c8, mask_b, mask_c


def _state_scan(v, pr, pi, reverse):
    n = v.shape[0]
    row = lax.broadcasted_iota(jnp.int32, v.shape, 0)
    z, s = v, 1
    while s < n:
        if reverse:
            zs = jnp.where(row < n - s, pltpu.roll(z, n - s, 0), 0.0)
        else:
            zs = jnp.where(row >= s, pltpu.roll(z, s, 0), 0.0)
        z = z + zs * pr + _swap_re_im(zs) * pi
        pr, pi = pr * pr - pi * pi, 2.0 * pr * pi
        s *= 2
    return z


def _s5_fwd(proj, u_cb, k8, bc, cc, a8, name):
    t = proj.shape[0]
    nt, ln = bc.shape[:2]
    w = a8.shape[2]
    nc = t // ln
    rb = min(t, S5_ROW_BLOCK)

    def body(u_ref, k_ref, b_ref, c_ref, a_ref, y_ref, x_ref, uf_ref):
        uf_ref[...] = u_ref[...].astype(F32)
        pos = lax.broadcasted_iota(jnp.int32, (rb, LANE), 0) & (ln - 1)
        for r0 in range(0, t, rb):
            u = uf_ref[r0:r0 + rb, :]
            acc = jnp.dot(u.astype(BF16), k_ref[0], preferred_element_type=F32)
            for lag in range(1, ln):
                us = jnp.where(pos >= lag, pltpu.roll(u, lag, 0), 0.0).astype(BF16)
                acc = acc + jnp.dot(us, k_ref[lag], preferred_element_type=F32)
            y_ref[r0:r0 + rb, :] = acc
        v = None
        for s in range(ln):
            part = jnp.dot(uf_ref[pl.ds(s, nc, stride=ln), :].astype(BF16), _s5_expand(b_ref[s], None, w)[0],
                           preferred_element_type=F32)
            v = part if v is None else v + part
        z = _state_scan(v, a_ref[0:1, :], a_ref[1:2, :], reverse=False)
        row = lax.broadcasted_iota(jnp.int32, z.shape, 0)
        x = jnp.where(row >= 1, pltpu.roll(z, 1, 0), 0.0)
        x_ref[...] = x
        x_bf = x.astype(BF16)
        for tt in range(ln):
            y_ref[pl.ds(tt, nc, stride=ln), :] += jnp.dot(x_bf, _s5_expand(None, c_ref[tt], w)[1], preferred_element_type=F32)

    tile = lambda shape: pl.BlockSpec((None,) + shape, lambda j: (j,) + (0,) * len(shape))
    return pl.pallas_call(
        body, name=name, grid=(nt,),
        in_specs=[pl.BlockSpec((t, LANE), lambda j: (0, u_cb + j)), tile((ln, LANE, LANE)), tile(bc.shape[1:]),
                  tile(cc.shape[1:]), tile((2, w))],
        out_specs=[pl.BlockSpec((t, LANE), lambda j: (0, j)), tile((nc, w))],
        out_shape=[jax.ShapeDtypeStruct((t, nt * LANE), F32), jax.ShapeDtypeStruct((nt, nc, w), F32)],
        scratch_shapes=[pltpu.VMEM((t, LANE), F32)],
        compiler_params=_cparams("parallel"),
    )(proj, k8, bc, cc, a8)


def _s5_bwd_data(dy, x_st, k8, bc, cc, a8, name):
    t = dy.shape[0]
    nt, ln = bc.shape[:2]
    w = a8.shape[2]
    nc = t // ln
    rb = min(t, S5_ROW_BLOCK)

    def body(dy_ref, x_ref, k_ref, b_ref, c_ref, a_ref, du_ref, dv_ref, da_ref, dyf_ref, duf_ref):
        dyf_ref[...] = dy_ref[...].astype(F32)
        pos = lax.broadcasted_iota(jnp.int32, (rb, LANE), 0) & (ln - 1)
        for r0 in range(0, t, rb):
            g = dyf_ref[r0:r0 + rb, :]
            acc = _dot_t(g.astype(BF16), k_ref[0], 1, 1)
            for lag in range(1, ln):
                gs = jnp.where(pos < ln - lag, pltpu.roll(g, rb - lag, 0), 0.0).astype(BF16)
                acc = acc + _dot_t(gs, k_ref[lag], 1, 1)
            duf_ref[r0:r0 + rb, :] = acc
        gx = None
        for tt in range(ln):
            part = _dot_t(dyf_ref[pl.ds(tt, nc, stride=ln), :].astype(BF16), _s5_expand(None, c_ref[tt], w)[1], 1, 1)
            gx = part if gx is None else gx + part
        rtot = _state_scan(gx, a_ref[0:1, :], -a_ref[1:2, :], reverse=True)
        row = lax.broadcasted_iota(jnp.int32, rtot.shape, 0)
        dv = jnp.where(row < nc - 1, pltpu.roll(rtot, nc - 1, 0), 0.0)
        dv_ref[...] = dv
        dv_bf = dv.astype(BF16)
        for s in range(ln):
            duf_ref[pl.ds(s, nc, stride=ln), :] += _dot_t(dv_bf, _s5_expand(b_ref[s], None, w)[0], 1, 1)
        du_ref[...] = duf_ref[...].astype(BF16)
        x = x_ref[...]
        da_ref[0:1, :] = jnp.sum(dv * x, axis=0, keepdims=True)
        da_ref[1:2, :] = jnp.sum(dv * _swap_re_im(x), axis=0, keepdims=True)

    tile = lambda shape: pl.BlockSpec((None,) + shape, lambda j: (j,) + (0,) * len(shape))
    return pl.pallas_call(
        body, name=name, grid=(nt,),
        in_specs=[pl.BlockSpec((t, LANE), lambda j: (0, j)), tile((nc, w)), tile((ln, LANE, LANE)), tile(bc.shape[1:]),
                  tile(cc.shape[1:]), tile((2, w))],
        out_specs=[pl.BlockSpec((t, LANE), lambda j: (0, j)), tile((nc, w)), tile((2, w))],
        out_shape=[jax.ShapeDtypeStruct((t, nt * LANE), BF16), jax.ShapeDtypeStruct((nt, nc, w), F32),
                   jax.ShapeDtypeStruct((nt, 2, w), F32)],
        scratch_shapes=[pltpu.VMEM((t, LANE), F32), pltpu.VMEM((t, LANE), F32)],
        compiler_params=_cparams("parallel"),
    )(dy, x_st, k8, bc, cc, a8)


def _s5_bwd_tables(dy, proj, u_cb, x_st, dv, ln, name):
    t = dy.shape[0]
    nt, nc, w = x_st.shape
    rb = min(t, S5_ROW_BLOCK)

    def body(dy_ref, u_ref, x_ref, dv_ref, dk_ref, db_ref, dc_ref, dyf_ref, uf_ref):
        s = pl.program_id(1)

        @pl.when(s == 0)
        def _():
            dyf_ref[...] = dy_ref[...].astype(F32)
            uf_ref[...] = u_ref[...].astype(F32)
            pos = lax.broadcasted_iota(jnp.int32, (rb, LANE), 0) & (ln - 1)
            for r0 in range(0, t, rb):
                u, g_bf = uf_ref[r0:r0 + rb, :], dy_ref[r0:r0 + rb, :]
                for lag in range(ln):
                    us = u if lag == 0 else jnp.where(pos >= lag, pltpu.roll(u, lag, 0), 0.0)
                    part = _dot_t(us.astype(BF16), g_bf, 0, 0)
                    if r0 == 0:
                        dk_ref[lag] = part
                    else:
                        dk_ref[lag] += part

        rows = pl.ds(s, nc, stride=ln)
        _, _, mask_b, mask_c = _s5_expand(None, None, w)
        db = jnp.where(mask_b, _dot_t(uf_ref[rows, :].astype(BF16), dv_ref[...].astype(BF16), 0, 0), 0.0)
        dc = jnp.where(mask_c, _dot_t(x_ref[...].astype(BF16), dyf_ref[rows, :].astype(BF16), 0, 0), 0.0)
        db_ref[...] = sum(db[:, h * LANE:(h + 1) * LANE] for h in range(w // LANE))
        dc_ref[...] = sum(dc[h * LANE:(h + 1) * LANE, :] for h in range(w // LANE))

    tile = lambda shape: pl.BlockSpec((None,) + shape, lambda j, s: (j,) + (0,) * len(shape))
    per_s = lambda shape: pl.BlockSpec((None, None) + shape, lambda j, s: (j, s, 0, 0))
    return pl.pallas_call(
        body, name=name, grid=(nt, ln),
        in_specs=[pl.BlockSpec((t, LANE), lambda j, s: (0, j)), pl.BlockSpec((t, LANE), lambda j, s: (0, u_cb + j)),
                  tile((nc, w)), tile((nc, w))],
        out_specs=[tile((ln, LANE, LANE)), per_s((LANE, LANE)), per_s((LANE, LANE))],
        out_shape=[jax.ShapeDtypeStruct((nt, ln, LANE, LANE), F32)] * 3,
        scratch_shapes=[pltpu.VMEM((t, LANE), F32), pltpu.VMEM((t, LANE), F32)],
        compiler_params=_cparams("parallel", "arbitrary"),
    )(dy, proj, x_st, dv)


def _gelu_parts(y):
    inner = GELU_C * (y + GELU_A * y * y * y)
    th = jnp.tanh(inner)
    return th, 0.5 * y * (1.0 + th)


def _s5_post_fwd(y_raw, proj, u_cb, s5d, wglu, bglu, name):
    w = y_raw.shape[1]

    def fn(yr, u, dsk, wg, bg):
        y = yr + dsk * u.astype(F32)
        _, h = _gelu_parts(y)
        gl = jnp.dot(h.astype(BF16), wg, preferred_element_type=F32) + bg
        return (h * _sigmoid(gl),)

    return _rowwise(fn, [("row", y_raw), ("win", proj, w, u_cb), ("full", s5d), ("full", wglu), ("full", bglu)],
                    [(w, BF16)], [], rows=y_raw.shape[0], tb=_tile(y_raw.shape[0], 512), name=name)[0]


def _s5_post_bwd(y_raw, proj, u_cb, s5d, wglu, bglu, dob, name):
    w = y_raw.shape[1]

    def fn(yr, u, dsk, wg, bg, dov):
        u = u.astype(F32)
        dov = dov.astype(F32)
        y = yr + dsk * u
        th, h = _gelu_parts(y)
        h_bf = h.astype(BF16)
        gl = jnp.dot(h_bf, wg, preferred_element_type=F32) + bg
        sg = _sigmoid(gl)
        dgl = dov * h * sg * (1.0 - sg)
        dgl_bf = dgl.astype(BF16)
        dh = dov * sg + _dot_t(dgl_bf, wg, 1, 1)
        dgelu = 0.5 * (1.0 + th) + 0.5 * y * (1.0 - th * th) * GELU_C * (1.0 + 3.0 * GELU_A * y * y)
        dy = dh * dgelu
        return (dy, dy * dsk,
                _dot_t(h_bf, dgl_bf, 0, 0), jnp.sum(dgl, axis=0, keepdims=True), jnp.sum(dy * u, axis=0, keepdims=True))

    return _rowwise(fn, [("row", y_raw), ("win", proj, w, u_cb), ("full", s5d), ("full", wglu), ("full", bglu), ("row", dob)],
                    [(w, BF16), (w, BF16)], [(w, w), (1, w), (1, w)], rows=y_raw.shape[0], tb=_tile(y_raw.shape[0], 512), name=name)


def _adamw(w, g, m, v, name):
    _, rows, cols = w.shape
    tr, tc = (_tile(rows, 256, align=16), cols) if rows % 16 == 0 else (rows, _tile(cols, 256))
    slots = isinstance(g, (list, tuple))
    gs = list(g) if slots else [g]
    c1 = 1.0 - ADAM_B1 ** ADAM_STEP
    c2 = 1.0 - ADAM_B2 ** ADAM_STEP

    def body(w_ref, m_ref, v_ref, *refs):
        g_refs, out_refs = refs[:len(gs)], refs[len(gs):]
        if slots:
            parts = [g_ref[s].astype(F32) for g_ref in g_refs for s in range(g_ref.shape[0])]
            gv = parts[0]
            for p in parts[1:]:
                gv = gv + p
            out_refs[0][...] = gv
        else:
            gv = g_refs[0][...]
        d_ref, nm_ref, nv_ref = out_refs[-3:]
        nm = ADAM_B1 * m_ref[...] + (1.0 - ADAM_B1) * gv
        nv = ADAM_B2 * v_ref[...] + (1.0 - ADAM_B2) * (gv * gv)
        d_ref[...] = -ADAM_LR * ((nm / c1) / (jnp.sqrt(nv / c2) + ADAM_EPS) + ADAM_WD * w_ref[...])
        nm_ref[...] = nm
        nv_ref[...] = nv

    spec = pl.BlockSpec((None, tr, tc), lambda i, j: (0, i, j))
    g_specs = [pl.BlockSpec((a.shape[0], tr, tc), lambda i, j: (0, i, j)) for a in gs] if slots else [pl.BlockSpec((tr, tc), lambda i, j: (i, j))]
    n_out = 4 if slots else 3
    return pl.pallas_call(
        body, name=name, grid=(rows // tr, cols // tc),
        in_specs=[spec, spec, spec] + g_specs, out_specs=[spec] * n_out,
        out_shape=[jax.ShapeDtypeStruct((1, rows, cols), F32)] * n_out,
        compiler_params=_cparams("parallel", "parallel"),
    )(w, m, v, *gs)


def _slot_sum(x, name):
    _, rows, cols = x.shape
    if rows % 8 == 0:
        tr, tc = _tile(rows, 512, align=8), cols
    else:
        tr, tc = rows, _tile(cols, 256)

    def body(x_ref, o_ref):
        acc = x_ref[0].astype(F32)
        for s in range(1, N_DEV):
            acc = acc + x_ref[s].astype(F32)
        o_ref[...] = acc

    return pl.pallas_call(
        body, name=name, grid=(rows // tr, cols // tc),
        in_specs=[pl.BlockSpec((N_DEV, tr, tc), lambda i, j: (0, i, j))],
        out_specs=pl.BlockSpec((tr, tc), lambda i, j: (i, j)),
        out_shape=jax.ShapeDtypeStruct((rows, cols), F32),
        compiler_params=_cparams("parallel", "parallel"),
    )(x)


_REST = (("w_a2", 1), ("w_glu", 0), ("w_branch_a", 1), ("w_branch_b", 1), ("w_out", 0), ("w_ffn_in", 1), ("w_ffn_out", 0))
_SMALL = ("norm1_g", "b_a2", "gla_norm_g", "lam_re", "lam_im", "log_dt", "s5_b_re", "s5_b_im", "s5_c_re", "s5_c_im",
          "s5_d", "b_glu", "norm2_g", "final_norm_g")
_ORDER = ("norm1_g", "w_in", "w_a2", "b_a2", "gla_norm_g", "lam_re", "lam_im", "log_dt", "s5_b_re", "s5_b_im", "s5_c_re",
          "s5_c_im", "s5_d", "w_glu", "b_glu", "w_branch_a", "w_branch_b", "w_out", "norm2_g", "w_ffn_in", "w_ffn_out", "final_norm_g")


def _join_slots(slots, axis):
    _, r, c = slots.shape
    if axis == 0:
        return slots.reshape(N_DEV * r, c)
    return slots.transpose(1, 0, 2).reshape(r, N_DEV * c)


def _to_slots(full, axis):
    r, c = full.shape
    if axis == 0:
        return full.reshape(N_DEV, r // N_DEV, c)
    return full.reshape(r, N_DEV, c // N_DEV).transpose(1, 0, 2)


def _local_step(x, target, w_in_t, small, rest):
    t, d = x.shape
    dk, dv, s5w = d // 4, d // 2, d // 4
    dist = not isinstance(rest, dict)
    if dist:
        h1, (w_in_slots,) = _rms_fwd(x, small["norm1_g"], "norm1_fwd", carry=("ag", [w_in_t]))
        w_in_t = w_in_slots.reshape(-1, d)
    else:
        h1 = _rms_fwd(x, small["norm1_g"], "norm1_fwd")
    o_q, o_k, o_v, o_r, o_al = 0, dk, 2 * dk, 2 * dk + dv, 2 * dk + 2 * dv
    o_u = o_al + GLA_RANK
    o_ga, o_gb = o_u + s5w, o_u + s5w + d
    rows = lambda a, o, n: a[o:o + n]
    w_main_t = jnp.concatenate([rows(w_in_t, o_ga, d), rows(w_in_t, o_gb, d), rows(w_in_t, o_v, dv), rows(w_in_t, o_r, dv),
                                rows(w_in_t, o_q, dk), rows(w_in_t, o_k, dk), rows(w_in_t, o_u, s5w)], axis=0)
    w_al_t = jnp.pad(rows(w_in_t, o_al, GLA_RANK), ((0, LANE - GLA_RANK), (0, 0)))
    u_cb = (2 * d + 2 * dv + 2 * dk) // s5w

    if dist:
        proj, gathered = _mm(h1, w_main_t, tb=True, out_dtype=BF16, carry=("ag", rest[:-1]), name="in_proj")
        w = {n: _join_slots(g, ax) for (n, ax), g in zip(_REST[:-2], gathered[:-1])}
        w_ffn_in_s = gathered[-1]
    else:
        proj = _mm(h1, w_main_t, tb=True, out_dtype=BF16, name="in_proj")
        w = rest
        w_ffn_in_s = _to_slots(rest["w_ffn_in"], 1)
    wa2 = jnp.pad(w["w_a2"], ((0, LANE - GLA_RANK), (0, 0)))
    alow = _mm(h1, w_al_t, tb=True, out_dtype=BF16, name="in_proj_gate_rank")
    o_a, o_pre, states = _gla_fwd(proj, alow, wa2, small["b_a2"], small["gla_norm_g"], dk=dk, dv=dv, name="gla_fwd")

    s5_params = (small["lam_re"], small["lam_im"], small["log_dt"][0], small["s5_b_re"], small["s5_b_im"],
                 small["s5_c_re"], small["s5_c_im"])
    (k8, bc, cc, a8), tables_vjp = jax.vjp(_s5_tables, *s5_params)
    k8_bf, b8_bf, c8_bf = k8.astype(BF16), bc.astype(BF16), cc.astype(BF16)
    u_lane_cb = u_cb * s5w // LANE
    y_raw, x_st = _s5_fwd(proj, u_lane_cb, k8_bf, b8_bf, c8_bf, a8, "s5_scan_fwd")
    o_b = _s5_post_fwd(y_raw, proj, u_cb, small["s5_d"], w["w_glu"], small["b_glu"], "s5_post_fwd")

    pa = _mm(o_a, w["w_branch_a"], out_dtype=BF16, name="branch_a")
    pb = _mm(o_b, w["w_branch_b"], out_dtype=BF16, name="branch_b")
    mix = _mix_fwd(proj, pa, pb, d, "mix_fwd")
    x1 = _mm(mix, w["w_out"], res=x, name="out_proj")
    h2 = _rms_fwd(x1, small["norm2_g"], "norm2_fwd")
    if dist:
        gu, act, (w_ffn_out_s,) = _ffn_in_fused(h2, w_ffn_in_s, carry=("ag", rest[-1:]), name="ffn_in")
        w_ffn_out = _join_slots(w_ffn_out_s, 0)
    else:
        gu, act = _ffn_in_fused(h2, w_ffn_in_s, name="ffn_in")
        w_ffn_out = rest["w_ffn_out"]
    x2 = _mm(act, w_ffn_out, res=x1, name="ffn_out")
    dx2, dx2_bf, d_final_g, loss = _loss_head(x2, small["final_norm_g"], target, "loss_head")

    recv = {}
    dgu, = _mm(dx2_bf, w_ffn_out, tb=True, epi=(_swiglu_bwd_tile, [gu], [(2, BF16)]), name="d_act")
    g_ffn_out = _mm(act, dx2_bf, ta=True, out_dtype=BF16, name="g_w_ffn_out")
    if dist:
        g_ffn_in_s, recv["w_ffn_out"] = _mm(h2, dgu, ta=True, b_slots=True, out_dtype=BF16, out_slots=N_DEV,
                                            carry=("a2a", [_to_slots(g_ffn_out, 0)]), name="g_w_ffn_in")
        dh2, recv["w_ffn_in"] = _mm(dgu, w_ffn_in_s, tb=True, a_slots=True, b_slots=True, b_group=2,
                                    carry=("a2a", [g_ffn_in_s]), name="d_h2")
    else:
        g_ffn_in_s = _mm(h2, dgu, ta=True, b_slots=True, out_dtype=BF16, out_slots=N_DEV, name="g_w_ffn_in")
        dh2 = _mm(dgu, w_ffn_in_s, tb=True, a_slots=True, b_slots=True, b_group=2, name="d_h2")
    dx1, dx1_bf, d_norm2_g = _rms_bwd(x1, small["norm2_g"], dh2, dx2, "norm2_bwd", True)
    dmix = _mm(dx1_bf, w["w_out"], tb=True, out_dtype=BF16, name="d_mix")
    g_out = _mm(mix, dx1_bf, ta=True, out_dtype=BF16, name="g_w_out")
    dpa, dpb, dga, dgb = _mix_bwd(proj, pa, pb, dmix, d, "mix_bwd")
    doa = _mm(dpa, w["w_branch_a"], tb=True, out_dtype=BF16, name="d_o_a")
    dob = _mm(dpb, w["w_branch_b"], tb=True, out_dtype=BF16, name="d_o_b")
    g_branch_a = _mm(o_a, dpa, ta=True, out_dtype=BF16, name="g_w_branch_a")
    g_branch_b = _mm(o_b, dpb, ta=True, out_dtype=BF16, name="g_w_branch_b")

    dy_s5, du_direct, g_glu, g_bglu, g_s5d = _s5_post_bwd(y_raw, proj, u_cb, small["s5_d"], w["w_glu"], small["b_glu"], dob, "s5_post_bwd")
    du_scan, dv_st, d_a8 = _s5_bwd_data(dy_s5, x_st, k8_bf, b8_bf, c8_bf, a8, "s5_scan_bwd")
    d_k8, d_b8, d_c8 = _s5_bwd_tables(dy_s5, proj, u_lane_cb, x_st, dv_st, S5_L, "s5_scan_bwd_tables")
    g_lam_re, g_lam_im, g_log_dt, g_b_re, g_b_im, g_c_re, g_c_im = tables_vjp((d_k8, d_b8, d_c8, d_a8))
    du = du_scan + du_direct

    dq, dkk, dvv, dr, dal, g_wa2, g_ba2, g_ghn = _gla_bwd(proj, alow, wa2, small["b_a2"], small["gla_norm_g"], o_pre, states, doa,
                                                        dk=dk, dv=dv, name="gla_bwd")
    dproj = jnp.concatenate([dga, dgb, dvv, dr, dq, dkk, du], axis=1)
    mid = {"w_out": g_out, "w_branch_a": g_branch_a, "w_branch_b": g_branch_b, "w_glu": g_glu.astype(BF16),
           "w_a2": g_wa2[:GLA_RANK].astype(BF16)}
    if dist:
        axes = dict(_REST)
        g_main_t, got = _mm(dproj, h1, ta=True, out_dtype=BF16, name="g_w_in_main",
                            carry=("a2a", [_to_slots(mid[n], axes[n]) for n in mid]))
        recv.update(zip(mid, [[g] for g in got]))
    else:
        g_main_t = _mm(dproj, h1, ta=True, out_dtype=BF16, name="g_w_in_main")
    g_al_t = _mm(dal, h1, ta=True, out_dtype=BF16, name="g_w_in_gate_rank")
    mrows = lambda o, n: g_main_t[o:o + n]
    g_w_in_t = jnp.concatenate([mrows(2 * d + 2 * dv, dk), mrows(2 * d + 2 * dv + dk, dk), mrows(2 * d, dv), mrows(2 * d + dv, dv),
                                g_al_t[:GLA_RANK], mrows(2 * d + 2 * dv + 2 * dk, s5w), mrows(0, d), mrows(d, d)], axis=0)
    if dist:
        g_w_in_s = _to_slots(g_w_in_t, 0)
        dh1, (recv_w_in,) = _mm(dproj, w_main_t, carry=("a2a", [g_w_in_s], [_ALL_K[:-1]]), name="d_h1_main")
    else:
        dh1 = _mm(dproj, w_main_t, name="d_h1_main")
    dh1 = _mm(dal, w_al_t, res=dh1, name="d_h1_gate_rank")
    if dist:
        (grad_x, d_norm1_g), (recv_w_in_last,) = _rms_bwd(x, small["norm1_g"], dh1, dx1, "norm1_bwd", False,
                                                          carry=("a2a", [g_w_in_s], [_ALL_K[-1:]]))
        recv["w_in"] = [recv_w_in, recv_w_in_last]
    else:
        grad_x, d_norm1_g = _rms_bwd(x, small["norm1_g"], dh1, dx1, "norm1_bwd", False)

    small_g = {
        "norm1_g": d_norm1_g, "b_a2": g_ba2, "gla_norm_g": g_ghn, "lam_re": g_lam_re, "lam_im": g_lam_im,
        "log_dt": g_log_dt[None], "s5_b_re": g_b_re, "s5_b_im": g_b_im, "s5_c_re": g_c_re, "s5_c_im": g_c_im,
        "s5_d": g_s5d, "b_glu": g_bglu, "norm2_g": d_norm2_g, "final_norm_g": d_final_g,
    }
    if not dist:
        recv = dict(mid, w_in=g_w_in_t, w_ffn_in=_join_slots(g_ffn_in_s, 1), w_ffn_out=g_ffn_out)
    return loss[0, 0], grad_x, recv, small_g


def _small_2d(name, a):
    a = a[0]
    return a[None] if a.ndim == 1 else a


def kernel(x, norm1_g, w_in, w_a2, b_a2, gla_norm_g, lam_re, lam_im, log_dt, s5_b_re, s5_b_im, s5_c_re, s5_c_im, s5_d, w_glu, b_glu, w_branch_a, w_branch_b, w_out, norm2_g, w_ffn_in, w_ffn_out, final_norm_g, loss_target, m_norm1_g, m_w_in, m_w_a2, m_b_a2, m_gla_norm_g, m_lam_re, m_lam_im, m_log_dt, m_s5_b_re, m_s5_b_im, m_s5_c_re, m_s5_c_im, m_s5_d, m_w_glu, m_b_glu, m_w_branch_a, m_w_branch_b, m_w_out, m_norm2_g, m_w_ffn_in, m_w_ffn_out, m_final_norm_g, v_norm1_g, v_w_in, v_w_a2, v_b_a2, v_gla_norm_g, v_lam_re, v_lam_im, v_log_dt, v_s5_b_re, v_s5_b_im, v_s5_c_re, v_s5_c_im, v_s5_d, v_w_glu, v_b_glu, v_w_branch_a, v_w_branch_b, v_w_out, v_norm2_g, v_w_ffn_in, v_w_ffn_out, v_final_norm_g):
    args = dict(locals())
    weights = {n: args[n] for n in _ORDER}
    m_in = {n: args["m_" + n] for n in _ORDER}
    v_in = {n: args["v_" + n] for n in _ORDER}
    transposed = lambda a: a[0].T[None]
    rest = [weights[n][0].astype(BF16) for n, _ in _REST]
    small = {n: _small_2d(n, weights[n]) for n in _SMALL}
    loss_local, grad_x, recv, small_g = _local_step(x[0], loss_target[0], transposed(weights["w_in"])[0].astype(BF16), small, rest)

    grads, delta, new_m, new_v = {}, {}, {}, {}
    for n, _ in _REST:
        grads[n], delta[n], new_m[n], new_v[n] = _adamw(weights[n], recv[n], m_in[n], v_in[n], "adamw_" + n)
    w_in_out = _adamw(transposed(weights["w_in"]), recv["w_in"], transposed(m_in["w_in"]), transposed(v_in["w_in"]), "adamw_w_in")
    grads["w_in"], delta["w_in"], new_m["w_in"], new_v["w_in"] = (transposed(a) for a in w_in_out)

    s_sizes = [small_g[n].size for n in _SMALL]
    s_offs = [sum(s_sizes[:i]) for i in range(len(s_sizes))]
    s_total = sum(s_sizes)
    s_rows = -(-(-(-(s_total + 1) // LANE)) // LANE) * LANE

    def pack_small(parts):
        flat = jnp.concatenate([p.reshape(-1) for p in parts])
        return jnp.pad(flat, (0, s_rows * LANE - flat.size)).reshape(s_rows, LANE)

    s_flat = pack_small([small_g[n] for n in _SMALL] + [loss_local])
    s_red = _slot_sum(_exchange("ag", [s_flat], "small_grads_all_gather")[0], "small_grads_slot_sum")
    loss = s_red.reshape(-1)[s_total]
    sd, sm, sv = _adamw(pack_small([weights[n] for n in _SMALL])[None], s_red, pack_small([m_in[n] for n in _SMALL])[None],
                        pack_small([v_in[n] for n in _SMALL])[None], "adamw_small")
    sd, sm, sv = sd[0], sm[0], sv[0]
    for n, o, s in zip(_SMALL, s_offs, s_sizes):
        shape = weights[n].shape[1:]
        grads[n], delta[n], new_m[n], new_v[n] = (a.reshape(-1)[o:o + s].reshape(shape) for a in (s_red, sd, sm, sv))

    out = [loss, grad_x[None]]
    for tree in (grads, delta, new_m, new_v):
        out += [tree[n].reshape(weights[n].shape) for n in _ORDER]
    return tuple(out)
```

```python
import functools
import math

import jax
import jax.numpy as jnp
from jax import lax
from jax.experimental import pallas as pl
from jax.experimental.pallas import tpu as pltpu

F32 = jnp.float32
BF16 = jnp.bfloat16

NORM_EPS = 1e-6
N_DEV = 8
N_PEER = N_DEV - 1
GLA_HEADS = 4
GLA_CHUNK = 32
GLA_CHUNK_SHIFT = 5
GLA_TAU = 16.0
GLA_RANK = 16
GLA_BLOCK = 256
S5_GC = 16
S5_P = 64
S5_L = 16
S5_TILE_G = 8
S5_ROW_BLOCK = 2048
LANE = 128
V7X_VMEM_LIMIT = 56 * 1024 * 1024
V7X_MM_VMEM_BUDGET = 40 * 1024 * 1024
V7X_MM_TILE_MN = 1408
V7X_MM_TILE_MN_WHOLE_K = 512
V7X_MM_TILE_K = 2048
V7X_EPI_ROW_CHUNKS = 4

ADAM_LR = 0.001
ADAM_B1 = 0.9
ADAM_B2 = 0.999
ADAM_EPS = 1e-08
ADAM_WD = 0.01
ADAM_STEP = 10

GELU_C = math.sqrt(2.0 / math.pi)
GELU_A = 0.044715

MESH = pl.DeviceIdType.MESH


def _cparams(*sem):
    return pltpu.CompilerParams(dimension_semantics=sem, vmem_limit_bytes=V7X_VMEM_LIMIT)


def _divisors_down(n, start, align=LANE):
    t = (min(start, n) // align) * align
    found = False
    while t >= align:
        if n % t == 0:
            found = True
            yield t
        t -= align
    if not found:
        yield n


def _tile(n, target, align=LANE):
    return next(_divisors_down(n, target, align))


def _sigmoid(x):
    return 1.0 / (1.0 + jnp.exp(-x))


_HBM_SPEC = pl.BlockSpec(memory_space=pltpu.HBM)


def _exchange_scratch(n):
    return [pltpu.SemaphoreType.DMA((n * N_PEER,)), pltpu.SemaphoreType.DMA((n * N_PEER,)), pltpu.SemaphoreType.DMA((n,))]


def _ag_phases(x_refs, out_refs, send_sems, recv_sems, local_sems):
    n = len(x_refs)
    x, y, c = lax.axis_index("x"), lax.axis_index("y"), lax.axis_index("c")
    me, sibling = (x, y, c), (x, y, 1 - c)
    chips = [(1 - x, y), (x, 1 - y), (1 - x, 1 - y)]

    def copy(a, k, block, to, from_input=False):
        dst = out_refs[a].at[4 * block[0] + 2 * block[1] + block[2]]
        return pltpu.make_async_remote_copy(
            src_ref=x_refs[a] if from_input else dst, dst_ref=dst,
            send_sem=send_sems.at[a * N_PEER + k], recv_sem=recv_sems.at[a * N_PEER + k], device_id=to, device_id_type=MESH)

    def local(a):
        return pltpu.make_async_copy(x_refs[a], out_refs[a].at[4 * x + 2 * y + c], local_sems.at[a])

    def first(a):
        return [copy(a, 0, me, sibling, True)] + [copy(a, 1 + j, me, (*chip, c), True) for j, chip in enumerate(chips)]

    def start():
        for a in range(n):
            local(a).start()
            for cp in first(a):
                cp.start()

    def relay():
        for j, chip in enumerate(chips):
            for a in range(n):
                copy(a, 1 + j, (*chip, c), me).wait_recv()
                copy(a, 4 + j, (*chip, c), sibling).start()

    def finish():
        for a in range(n):
            copy(a, 0, sibling, me).wait_recv()
            for j, chip in enumerate(chips):
                copy(a, 4 + j, (*chip, 1 - c), me).wait_recv()
        for a in range(n):
            for cp in first(a) + [copy(a, 4 + j, (*chip, c), sibling) for j, chip in enumerate(chips)]:
                cp.wait_send()
            local(a).wait()

    return start, relay, finish


_ALL_K = tuple(range(N_DEV))


def _a2a_phases(x_refs, out_refs, send_sems, recv_sems, local_sems, ks_list=None):
    n = len(x_refs)
    ks_list = ks_list or [_ALL_K] * n
    x, y, c = lax.axis_index("x"), lax.axis_index("y"), lax.axis_index("c")
    my = 4 * x + 2 * y + c

    def copy(a, k):
        px, py, pc = (1 - x if k & 4 else x), (1 - y if k & 2 else y), (1 - c if k & 1 else c)
        return pltpu.make_async_remote_copy(
            src_ref=x_refs[a].at[4 * px + 2 * py + pc], dst_ref=out_refs[a].at[ks_list[a].index(k)],
            send_sem=send_sems.at[a * N_PEER + k - 1], recv_sem=recv_sems.at[a * N_PEER + k - 1],
            device_id=(px, py, pc), device_id_type=MESH)

    def local(a):
        return pltpu.make_async_copy(x_refs[a].at[my], out_refs[a].at[ks_list[a].index(0)], local_sems.at[a])

    def start():
        for a in range(n):
            for k in ks_list[a]:
                (copy(a, k) if k else local(a)).start()

    def relay():
        pass

    def finish():
        for a in range(n):
            for k in ks_list[a]:
                if k:
                    copy(a, k).wait_recv()
        for a in range(n):
            for k in ks_list[a]:
                if k:
                    copy(a, k).wait_send()
                else:
                    local(a).wait()

    return start, relay, finish


def _exchange_out_shapes(kind, arrays, ks_list=None):
    if kind == "ag":
        return [jax.ShapeDtypeStruct((N_DEV,) + a.shape, a.dtype) for a in arrays]
    ks_list = ks_list or [_ALL_K] * len(arrays)
    return [jax.ShapeDtypeStruct((len(ks),) + a.shape[1:], a.dtype) for a, ks in zip(arrays, ks_list)]


def _exchange(kind, arrays, name):
    n = len(arrays)
    phases = _ag_phases if kind == "ag" else _a2a_phases

    def body(*refs):
        start, relay, finish = phases(refs[:n], refs[n:2 * n], *refs[2 * n:])
        start()
        relay()
        finish()

    return pl.pallas_call(
        body, name=name,
        out_shape=_exchange_out_shapes(kind, arrays),
        in_specs=[_HBM_SPEC] * n, out_specs=[_HBM_SPEC] * n,
        scratch_shapes=_exchange_scratch(n),
    )(*arrays)


def _mm_tiles(m, n_unit, k_unit, tile_bytes, small_tiles_ok=True):
    fits = lambda tm, tn, tk: 2 * 2 * (tm * tk + tk * tn) + tile_bytes * tm * tn <= V7X_MM_VMEM_BUDGET
    for cap in (V7X_MM_TILE_MN, V7X_MM_TILE_MN_WHOLE_K) if small_tiles_ok else (V7X_MM_TILE_MN,):
        tm, tn = _tile(m, cap), _tile(n_unit, cap)
        if fits(tm, tn, k_unit) and (tn >= V7X_MM_TILE_MN_WHOLE_K or tn == n_unit):
            return tm, tn, k_unit
    tm, tn = _tile(m, V7X_MM_TILE_MN), _tile(n_unit, V7X_MM_TILE_MN)
    for tk in _divisors_down(k_unit, V7X_MM_TILE_K):
        if fits(tm, tn, tk):
            return tm, tn, tk
    return tm, tn, _tile(k_unit, LANE)


def _carry_parts(carry):
    kind, arrays, ks_list = (tuple(carry) + (None,))[:3] if carry is not None else (None, [], None)
    n = len(arrays)
    kind = (kind, ks_list)
    return kind, arrays, [_HBM_SPEC] * n, _exchange_out_shapes(kind[0], arrays, ks_list), (_exchange_scratch(n) if n else [])


def _carry_hooks(kind, x_refs, out_refs, sems, step, last_step):
    if not x_refs:
        return lambda: None
    kind, ks_list = kind
    if kind == "ag":
        start, relay, finish = _ag_phases(x_refs, out_refs, *sems)
    else:
        start, relay, finish = _a2a_phases(x_refs, out_refs, *sems, ks_list=ks_list)
    pl.when(step == 0)(start)

    def after():
        if kind == "ag":
            pl.when(step == (last_step * 7) // 8)(relay)
        pl.when(step == last_step)(finish)

    return after


def _mm(a, b, *, ta=False, tb=False, out_dtype=F32, res=None, carry=None, a_slots=False, b_slots=False, b_group=0,
        out_slots=0, epi=None, name):
    if a_slots:
        assert not ta
        a_n, m, a_c = a.shape
        k = a_n * a_c
    else:
        m, k = (a.shape[1], a.shape[0]) if ta else a.shape
    if b_slots:
        b_n, b_r, b_c = b.shape
        k2, n = (b_n * b_c, b_r) if tb else (b_r, b_n * b_c)
    else:
        k2, n = (b.shape[1], b.shape[0]) if tb else b.shape
    assert k == k2, (a.shape, b.shape, ta, tb)
    has_res = res is not None
    assert not (has_res and (out_slots or epi))
    n_units = [n] + ([n // out_slots] if out_slots else []) + ([b_c] if b_slots and not tb else [])
    k_units = [k] + ([a_c] if a_slots else []) + ([b_c] if b_slots and tb else [])
    n_unit, k_unit = min(n_units), min(k_units)
    assert all(u % n_unit == 0 for u in n_units) and all(u % k_unit == 0 for u in k_units)
    epi_fn, epi_ins, epi_outs = epi if epi is not None else (None, [], [])
    tile_bytes = 4 + (2 * res.dtype.itemsize if has_res else 0)
    tile_bytes += sum(2 * e.shape[0] * e.dtype.itemsize for e in epi_ins)
    tile_bytes += sum(2 * l * jnp.dtype(dt).itemsize for l, dt in epi_outs) if epi else 2 * jnp.dtype(out_dtype).itemsize
    tm, tn, tk = _mm_tiles(m, n_unit, k_unit, tile_bytes, small_tiles_ok=not epi)
    if b_group:
        tk = b_group * b_c
        assert b_slots and tb and k % tk == 0 and (not a_slots or a_c % tk == 0)
    ni, nj, nk = m // tm, n // tn, k // tk
    dims = (((0,) if ta else (1,), (1,) if tb else (0,)), ((), ()))

    def slot_map(per, pos):
        if pos == "k_cols":
            return lambda i, j, kk: (kk // per, i, kk % per)
        if pos == "k_cols_j":
            return lambda i, j, kk: (kk // per, j, kk % per)
        if pos == "n_cols_k":
            return lambda i, j, kk: (j // per, kk, j % per)
        return lambda i, j, kk: (j // per, i, j % per)

    if a_slots:
        a_spec = pl.BlockSpec((None, tm, tk), slot_map(a_c // tk, "k_cols"))
    else:
        a_spec = pl.BlockSpec((tk, tm), lambda i, j, kk: (kk, i)) if ta else pl.BlockSpec((tm, tk), lambda i, j, kk: (i, kk))
    if b_group:
        b_spec = pl.BlockSpec((b_group, tn, b_c), lambda i, j, kk: (kk, j, 0))
    elif b_slots and tb:
        b_spec = pl.BlockSpec((None, tn, tk), slot_map(b_c // tk, "k_cols_j"))
    elif b_slots:
        b_spec = pl.BlockSpec((None, tk, tn), slot_map(b_c // tn, "n_cols_k"))
    else:
        b_spec = pl.BlockSpec((tn, tk), lambda i, j, kk: (j, kk)) if tb else pl.BlockSpec((tk, tn), lambda i, j, kk: (kk, j))
    if epi:
        lead_spec = lambda l: pl.BlockSpec((l, tm, tn), lambda i, j, kk: (0, i, j))
        o_specs = [lead_spec(l) for l, _ in epi_outs]
        o_shapes = [jax.ShapeDtypeStruct((l, m, n), dt) for l, dt in epi_outs]
    elif out_slots:
        o_specs = [pl.BlockSpec((None, tm, tn), slot_map((n // out_slots) // tn, "n_cols_i"))]
        o_shapes = [jax.ShapeDtypeStruct((out_slots, m, n // out_slots), out_dtype)]
    else:
        o_specs = [pl.BlockSpec((tm, tn), lambda i, j, kk: (i, j))]
        o_shapes = [jax.ShapeDtypeStruct((m, n), out_dtype)]
    extra_ins = ([res] if has_res else []) + list(epi_ins)
    extra_specs = ([o_specs[0]] if has_res else []) + [pl.BlockSpec((e.shape[0], tm, tn), lambda i, j, kk: (0, i, j)) for e in epi_ins]
    n_in, n_out = 2 + len(extra_ins), len(o_specs)
    c_kind, c_arrays, c_specs, c_shapes, c_scratch = _carry_parts(carry)
    nc = len(c_arrays)
    last_step = ni * nj * nk - 1

    def body(*refs):
        a_ref, b_ref = refs[0], refs[1]
        e_refs = refs[2:n_in]
        x_refs = refs[n_in:n_in + nc]
        o_refs = refs[n_in + nc:n_in + nc + n_out]
        out_refs = refs[n_in + nc + n_out:n_in + 2 * nc + n_out]
        scratch = refs[n_in + 2 * nc + n_out:]
        acc = scratch[0] if nk > 1 else None
        kk = pl.program_id(2)
        step = (pl.program_id(0) * nj + pl.program_id(1)) * nk + kk
        after = _carry_hooks(c_kind, x_refs, out_refs, scratch[-3:], step, last_step)

        def emit(val):
            if has_res:
                val = val + e_refs[0][...].astype(F32)
            if epi:
                for o_ref, parts in zip(o_refs, epi_fn(val, *[e[...] for e in e_refs])):
                    for l, v in enumerate(parts):
                        o_ref[l] = v.astype(o_ref.dtype)
            else:
                o_refs[0][...] = val.astype(out_dtype)

        if epi and nk == 1 and not ta:
            rc = tm // V7X_EPI_ROW_CHUNKS
            for r in range(V7X_EPI_ROW_CHUNKS):
                rows = slice(r * rc, (r + 1) * rc)
                val = lax.dot_general(a_ref[rows, :], b_ref[...], dims, preferred_element_type=F32)
                for o_ref, parts in zip(o_refs, epi_fn(val, *[e[:, rows, :] for e in e_refs])):
                    for l, v in enumerate(parts):
                        o_ref[l, rows, :] = v.astype(o_ref.dtype)
            after()
            return
        if b_group:
            part = sum(lax.dot_general(a_ref[:, s * b_c:(s + 1) * b_c], b_ref[s], dims, preferred_element_type=F32)
                       for s in range(b_group))
        else:
            part = lax.dot_general(a_ref[...], b_ref[...], dims, preferred_element_type=F32)
        if nk == 1:
            emit(part)
        else:
            @pl.when(kk == 0)
            def _():
                acc[...] = part

            @pl.when(kk > 0)
            def _():
                acc[...] += part

            @pl.when(kk == nk - 1)
            def _():
                emit(acc[...])

        after()

    sem = ("arbitrary",) * 3 if nc else ("parallel", "parallel", "arbitrary")
    outs = pl.pallas_call(
        body, name=name,
        grid=(ni, nj, nk),
        in_specs=[a_spec, b_spec] + extra_specs + c_specs,
        out_specs=o_specs + c_specs,
        out_shape=o_shapes + c_shapes,
        scratch_shapes=([pltpu.VMEM((tm, tn), F32)] if nk > 1 else []) + c_scratch,
        compiler_params=_cparams(*sem),
    )(a, b, *extra_ins, *c_arrays)
    main = list(outs[:n_out]) if epi else outs[0]
    return (main, list(outs[n_out:])) if nc else main


def _ffn_in_fused(h2, w_s, *, carry=None, name):
    t, d = h2.shape
    n_slot, _, c = w_s.shape
    half = n_slot // 2
    tm = _tile(t, 512)
    c_kind, c_arrays, c_specs, c_shapes, c_scratch = _carry_parts(carry)
    nc = len(c_arrays)
    last_step = (t // tm) * half - 1

    def body(h_ref, wg_ref, wu_ref, *refs):
        x_refs, (gu_ref, act_ref), out_refs, sems = refs[:nc], refs[nc:nc + 2], refs[nc + 2:2 * nc + 2], refs[2 * nc + 2:]
        step = pl.program_id(0) * half + pl.program_id(1)
        after = _carry_hooks(c_kind, x_refs, out_refs, sems, step, last_step)
        h = h_ref[...]
        g = jnp.dot(h, wg_ref[...], preferred_element_type=F32)
        u = jnp.dot(h, wu_ref[...], preferred_element_type=F32)
        sg = _sigmoid(g)
        silu = g * sg
        gu_ref[0] = (u * (sg + silu - silu * sg)).astype(BF16)
        gu_ref[1] = silu.astype(BF16)
        act_ref[...] = (silu * u).astype(BF16)
        after()

    outs = pl.pallas_call(
        body, name=name,
        grid=(t // tm, half),
        in_specs=[pl.BlockSpec((tm, d), lambda i, j: (i, 0)),
                  pl.BlockSpec((None, d, c), lambda i, j: (j, 0, 0)),
                  pl.BlockSpec((None, d, c), lambda i, j: (half + j, 0, 0))] + c_specs,
        out_specs=[pl.BlockSpec((2, tm, c), lambda i, j: (0, i, j)), pl.BlockSpec((tm, c), lambda i, j: (i, j))] + c_specs,
        out_shape=[jax.ShapeDtypeStruct((2, t, half * c), BF16), jax.ShapeDtypeStruct((t, half * c), BF16)] + c_shapes,
        scratch_shapes=c_scratch,
        compiler_params=_cparams(*(("arbitrary",) * 2 if nc else ("parallel", "parallel"))),
    )(h2, w_s, w_s, *c_arrays)
    return (outs[0], outs[1], list(outs[2:])) if nc else (outs[0], outs[1])


def _rowwise(fn, ins, row_outs, acc_outs, *, rows, tb, name, carry=None):
    in_specs, args = [], []
    for spec in ins:
        kind, arr = spec[0], spec[1]
        if kind == "row":
            in_specs.append(pl.BlockSpec((tb, arr.shape[1]), lambda i: (i, 0)))
        elif kind == "win":
            width, cb = spec[2], spec[3]
            in_specs.append(pl.BlockSpec((tb, width), functools.partial(lambda i, cb: (i, cb), cb=cb)))
        else:
            in_specs.append(pl.BlockSpec(arr.shape, lambda i: (0, 0)))
        args.append(arr)
    out_specs = [pl.BlockSpec((tb, c), lambda i: (i, 0)) for c, _ in row_outs]
    out_specs += [pl.BlockSpec(shape, lambda i: (0, 0)) for shape in acc_outs]
    out_shape = [jax.ShapeDtypeStruct((rows, c), dt) for c, dt in row_outs]
    out_shape += [jax.ShapeDtypeStruct(shape, F32) for shape in acc_outs]
    n_in, n_row, n_out = len(ins), len(row_outs), len(row_outs) + len(acc_outs)
    c_kind, c_arrays, c_specs, c_shapes, c_scratch = _carry_parts(carry)
    nc = len(c_arrays)

    def body(*refs):
        after = _carry_hooks(c_kind, refs[n_in:n_in + nc], refs[n_in + nc + n_out:n_in + 2 * nc + n_out],
                             refs[n_in + 2 * nc + n_out:], pl.program_id(0), rows // tb - 1)
        vals = [r[...] for r in refs[:n_in]]
        outs = fn(*vals)
        if not isinstance(outs, (tuple, list)):
            outs = (outs,)
        out_refs = refs[n_in + nc:n_in + nc + n_out]
        for o_ref, val in zip(out_refs[:n_row], outs[:n_row]):
            o_ref[...] = val.astype(o_ref.dtype)
        first = pl.program_id(0) == 0
        for o_ref, val in zip(out_refs[n_row:], outs[n_row:]):
            @pl.when(first)
            def _(o_ref=o_ref):
                o_ref[...] = jnp.zeros_like(o_ref)
            o_ref[...] += val
        after()

    res = pl.pallas_call(
        body, name=name,
        grid=(rows // tb,),
        in_specs=in_specs + c_specs, out_specs=out_specs + c_specs, out_shape=out_shape + c_shapes,
        scratch_shapes=c_scratch,
        compiler_params=_cparams("arbitrary"),
    )(*args, *c_arrays)
    return (list(res[:n_out]), list(res[n_out:])) if nc else res


def _rms_fwd(x, g, name, carry=None):
    def fn(xv, gv):
        r = lax.rsqrt(jnp.mean(xv * xv, axis=-1, keepdims=True) + NORM_EPS)
        return (xv * r * gv,)
    res = _rowwise(fn, [("row", x), ("full", g)], [(x.shape[1], BF16)], [], rows=x.shape[0], tb=_tile(x.shape[0], 512),
                   name=name, carry=carry)
    return (res[0][0], res[1]) if carry is not None else res[0]


def _rms_bwd(x, g, dh, dres, name, want_bf16):
    d = x.shape[1]

    def fn(xv, gv, dhv, drv):
        r = lax.rsqrt(jnp.mean(xv * xv, axis=-1, keepdims=True) + NORM_EPS)
        xhat = xv * r
        dhv = dhv.astype(F32)
        dxhat = dhv * gv
        dx = drv + r * (dxhat - xhat * jnp.mean(dxhat * xhat, axis=-1, keepdims=True))
        dg = jnp.sum(dhv * xhat, axis=0, keepdims=True)
        return (dx, dx, dg) if want_bf16 else (dx, dg)

    row_outs = [(d, F32), (d, BF16)] if want_bf16 else [(d, F32)]
    return _rowwise(fn, [("row", x), ("full", g), ("row", dh), ("row", dres)], row_outs, [(1, d)],
                    rows=x.shape[0], tb=_tile(x.shape[0], 256), name=name)


def _loss_head(x2, g, target, name):
    d = x2.shape[1]

    def fn(xv, gv, tv):
        r = lax.rsqrt(jnp.mean(xv * xv, axis=-1, keepdims=True) + NORM_EPS)
        xhat = xv * r
        diff = xhat * gv - tv
        loss = 0.5 * jnp.sum(jnp.mean(diff * diff, axis=-1, keepdims=True), axis=0, keepdims=True)
        dy = diff * (1.0 / d)
        dxhat = dy * gv
        dx = r * (dxhat - xhat * jnp.mean(dxhat * xhat, axis=-1, keepdims=True))
        dg = jnp.sum(dy * xhat, axis=0, keepdims=True)
        return dx, dx, dg, jnp.broadcast_to(loss, (1, LANE))

    return _rowwise(fn, [("row", x2), ("full", g), ("row", target)], [(d, F32), (d, BF16)], [(1, d), (1, LANE)],
                    rows=x2.shape[0], tb=_tile(x2.shape[0], 256), name=name)


def _swiglu_bwd_tile(dact, dswiglu):
    return ((dact * dswiglu[0].astype(F32), dact * dswiglu[1].astype(F32)),)


def _mix_fwd(proj, pa, pb, d, name):
    def fn(ga, gb, av, bv):
        return (_sigmoid(ga.astype(F32)) * av.astype(F32) + _sigmoid(gb.astype(F32)) * bv.astype(F32),)
    return _rowwise(fn, [("win", proj, d, 0), ("win", proj, d, 1), ("row", pa), ("row", pb)], [(d, BF16)], [],
                    rows=pa.shape[0], tb=_tile(pa.shape[0], 512), name=name)[0]


def _mix_bwd(proj, pa, pb, dmix, d, name):
    def fn(ga, gb, av, bv, dm):
        dm = dm.astype(F32)
        sa, sb = _sigmoid(ga.astype(F32)), _sigmoid(gb.astype(F32))
        av, bv = av.astype(F32), bv.astype(F32)
        return dm * sa, dm * sb, dm * av * sa * (1.0 - sa), dm * bv * sb * (1.0 - sb)
    return _rowwise(fn, [("win", proj, d, 0), ("win", proj, d, 1), ("row", pa), ("row", pb), ("row", dmix)],
                    [(d, BF16)] * 4, [], rows=pa.shape[0], tb=_tile(pa.shape[0], 512), name=name)


def _chunk_masks(tb):
    r = lax.broadcasted_iota(jnp.int32, (tb, tb), 0)
    c = lax.broadcasted_iota(jnp.int32, (tb, tb), 1)
    same = lax.shift_right_logical(r, GLA_CHUNK_SHIFT) == lax.shift_right_logical(c, GLA_CHUNK_SHIFT)
    return same, same & (c <= r), same & (r <= c)


def _mask_bf16(mask):
    return jnp.where(mask, 1.0, 0.0).astype(BF16)


def _split_dot(mask_bf, x, terms):
    acc, rem = None, x
    for _ in range(terms):
        hi = rem.astype(BF16)
        part = jnp.dot(mask_bf, hi, preferred_element_type=F32)
        acc = part if acc is None else acc + part
        rem = rem - hi.astype(F32)
    return acc


def _gla_decay(al, wa2, ba2, same_bf, causal_bf):
    z = jnp.dot(al.astype(BF16), wa2, preferred_element_type=F32) + ba2
    la = (jnp.minimum(z, 0.0) - jnp.log(1.0 + jnp.exp(-jnp.abs(z)))) * (1.0 / GLA_TAU)
    bc = _split_dot(causal_bf, la, 3)
    bl = _split_dot(same_bf, la, 3)
    return z, bc, bl


def _dot_t(a, b, ca, cb):
    return lax.dot_general(a, b, (((ca,), (cb,)), ((), ())), preferred_element_type=F32)


def _gla_fwd(proj, alow, wa2, ba2, ghn, *, dk, dv, name, carry=None):
    t = proj.shape[0]
    tb = min(GLA_BLOCK, t)
    nch = tb // GLA_CHUNK
    hk, hv = dk // GLA_HEADS, dv // GLA_HEADS
    scale = hk ** -0.5
    v_cb, r_cb = (8 * dk) // dv, (8 * dk) // dv + 1
    q_cb, k_cb = (8 * dk + 2 * dv) // dk, (8 * dk + 2 * dv) // dk + 1
    c_kind, c_arrays, c_specs, c_shapes, c_scratch = _carry_parts(carry)
    nc = len(c_arrays)

    def body(q_ref, k_ref, v_ref, r_ref, al_ref, wa2_ref, ba2_ref, ghn_ref, *refs):
        x_refs, (oa_ref, opre_ref, s_ref), out_refs = refs[:nc], refs[nc:nc + 3], refs[nc + 3:2 * nc + 3]
        st_scr, sems = refs[2 * nc + 3], refs[2 * nc + 4:]
        after = _carry_hooks(c_kind, x_refs, out_refs, sems, pl.program_id(0), t // tb - 1)

        @pl.when(pl.program_id(0) == 0)
        def _():
            st_scr[...] = jnp.zeros_like(st_scr)

        same, causal, _ = _chunk_masks(tb)
        same_bf, causal_bf = _mask_bf16(same), _mask_bf16(causal)
        _, bc, bl = _gla_decay(al_ref[...], wa2_ref[...], ba2_ref[...], same_bf, causal_bf)
        q = q_ref[...].astype(F32) * scale
        k = k_ref[...].astype(F32)
        qd = (q * jnp.exp(bc)).astype(BF16)
        ki = (k * jnp.exp(-bc)).astype(BF16)
        ks = (k * jnp.exp(bl - bc)).astype(BF16)
        dl = jnp.exp(bl)
        ksls = [slice(h * hk, (h + 1) * hk) for h in range(GLA_HEADS)]
        vsls = [slice(h * hv, (h + 1) * hv) for h in range(GLA_HEADS)]
        v_hs = [v_ref[:, vsl] for vsl in vsls]
        o_intras = []
        for ksl, v_h in zip(ksls, v_hs):
            sc = jnp.where(causal, _dot_t(qd[:, ksl], ki[:, ksl], 1, 1), 0.0)
            o_intras.append(jnp.dot(sc.astype(BF16), v_h, preferred_element_type=F32))
        for c in range(nch):
            rows = slice(c * GLA_CHUNK, (c + 1) * GLA_CHUNK)
            for h, (ksl, vsl) in enumerate(zip(ksls, vsls)):
                st = st_scr[h]
                s_ref[c, h] = st
                opre_ref[rows, vsl] = o_intras[h][rows] + _dot_t(qd[rows, ksl], st.astype(BF16), 1, 1)
                st_scr[h] = dl[c * GLA_CHUNK:c * GLA_CHUNK + 1, ksl] * st + _dot_t(v_hs[h][rows], ks[rows, ksl], 0, 0)
        for h in range(GLA_HEADS):
            vsl = slice(h * hv, (h + 1) * hv)
            o = opre_ref[:, vsl]
            rs = lax.rsqrt(jnp.mean(o * o, axis=-1, keepdims=True) + NORM_EPS)
            rv = r_ref[:, vsl].astype(F32)
            oa_ref[:, vsl] = (rv * _sigmoid(rv) * (o * rs * ghn_ref[:, vsl])).astype(BF16)
        after()

    nchunks = t // GLA_CHUNK
    outs = pl.pallas_call(
        body, name=name,
        grid=(t // tb,),
        in_specs=[
            pl.BlockSpec((tb, dk), lambda i: (i, q_cb)),
            pl.BlockSpec((tb, dk), lambda i: (i, k_cb)),
            pl.BlockSpec((tb, dv), lambda i: (i, v_cb)),
            pl.BlockSpec((tb, dv), lambda i: (i, r_cb)),
            pl.BlockSpec((tb, LANE), lambda i: (i, 0)),
            pl.BlockSpec(wa2.shape, lambda i: (0, 0)),
            pl.BlockSpec(ba2.shape, lambda i: (0, 0)),
            pl.BlockSpec(ghn.shape, lambda i: (0, 0)),
        ] + c_specs,
        out_specs=[
            pl.BlockSpec((tb, dv), lambda i: (i, 0)),
            pl.BlockSpec((tb, dv), lambda i: (i, 0)),
            pl.BlockSpec((nch, GLA_HEADS, hv, hk), lambda i: (i, 0, 0, 0)),
        ] + c_specs,
        out_shape=[
            jax.ShapeDtypeStruct((t, dv), BF16),
            jax.ShapeDtypeStruct((t, dv), F32),
            jax.ShapeDtypeStruct((nchunks, GLA_HEADS, hv, hk), F32),
        ] + c_shapes,
        scratch_shapes=[pltpu.VMEM((GLA_HEADS, hv, hk), F32)] + c_scratch,
        compiler_params=_cparams("arbitrary"),
    )(proj, proj, proj, proj, alow, wa2, ba2, ghn, *c_arrays)
    return (outs[0], outs[1], outs[2], list(outs[3:])) if nc else tuple(outs)


def _gla_bwd(proj, alow, wa2, ba2, ghn, opre, states, doa, *, dk, dv, name):
    t = proj.shape[0]
    tb = min(GLA_BLOCK, t)
    nb = t // tb
    nch = tb // GLA_CHUNK
    hk, hv = dk // GLA_HEADS, dv // GLA_HEADS
    scale = hk ** -0.5
    v_cb, r_cb = (8 * dk) // dv, (8 * dk) // dv + 1
    q_cb, k_cb = (8 * dk + 2 * dv) // dk, (8 * dk + 2 * dv) // dk + 1

    def body(q_ref, k_ref, v_ref, r_ref, al_ref, wa2_ref, ba2_ref, ghn_ref, opre_ref, s_ref, doa_ref,
             dq_ref, dk_ref, dv_ref, dr_ref, dal_ref, dwa2_ref, dba2_ref, dghn_ref,
             dst_scr, dqd_scr, dki_scr, dks_scr, ddl_scr):
        @pl.when(pl.program_id(0) == 0)
        def _():
            dst_scr[...] = jnp.zeros_like(dst_scr)
            dwa2_ref[...] = jnp.zeros_like(dwa2_ref)
            dba2_ref[...] = jnp.zeros_like(dba2_ref)
            dghn_ref[...] = jnp.zeros_like(dghn_ref)

        same, causal, anti = _chunk_masks(tb)
        same_bf, causal_bf, anti_bf = _mask_bf16(same), _mask_bf16(causal), _mask_bf16(anti)
        al = al_ref[...]
        wa2v = wa2_ref[...]
        z, bc, bl = _gla_decay(al, wa2v, ba2_ref[...], same_bf, causal_bf)
        e_bc, e_nbc, e_st = jnp.exp(bc), jnp.exp(-bc), jnp.exp(bl - bc)
        q = q_ref[...].astype(F32) * scale
        k = k_ref[...].astype(F32)
        qd_f, ki_f, ks_f = q * e_bc, k * e_nbc, k * e_st
        qd, ki, ks = qd_f.astype(BF16), ki_f.astype(BF16), ks_f.astype(BF16)
        dl = jnp.exp(bl)
        per_head = []
        for h in range(GLA_HEADS):
            ksl = slice(h * hk, (h + 1) * hk)
            vsl = slice(h * hv, (h + 1) * hv)
            o = opre_ref[:, vsl]
            rs = lax.rsqrt(jnp.mean(o * o, axis=-1, keepdims=True) + NORM_EPS)
            ohat = o * rs
            g_h = ghn_ref[:, vsl]
            rv = r_ref[:, vsl].astype(F32)
            sg = _sigmoid(rv)
            d_oa = doa_ref[:, vsl].astype(F32)
            don = d_oa * (rv * sg)
            dr_ref[:, vsl] = (d_oa * (ohat * g_h) * (sg * (1.0 + rv * (1.0 - sg)))).astype(BF16)
            dghn_ref[:, vsl] += jnp.sum(don * ohat, axis=0, keepdims=True)
            dohat = don * g_h
            do_f = rs * (dohat - ohat * jnp.mean(dohat * ohat, axis=-1, keepdims=True))
            do = do_f.astype(BF16)
            v_h = v_ref[:, vsl]
            p = jnp.where(causal, _dot_t(do, v_h, 1, 1), 0.0).astype(BF16)
            dqd_intra = jnp.dot(p, ki[:, ksl], preferred_element_type=F32)
            dki_scr[:, ksl] = _dot_t(p, qd[:, ksl], 0, 0)
            sc = jnp.where(causal, _dot_t(qd[:, ksl], ki[:, ksl], 1, 1), 0.0).astype(BF16)
            dv_intra = _dot_t(sc, do, 0, 0)
            per_head.append((ksl, vsl, v_h, do, dqd_intra, dv_intra))
        for c in reversed(range(nch)):
            rows = slice(c * GLA_CHUNK, (c + 1) * GLA_CHUNK)
            for h, (ksl, vsl, v_h, do, dqd_intra, dv_intra) in enumerate(per_head):
                dst = dst_scr[h]
                st = s_ref[c, h]
                dst_bf = dst.astype(BF16)
                dv_ref[rows, vsl] = (dv_intra[rows] + _dot_t(ks[rows, ksl], dst_bf, 1, 1)).astype(BF16)
                dks_scr[rows, ksl] = jnp.dot(v_h[rows], dst_bf, preferred_element_type=F32)
                dl_c = dl[c * GLA_CHUNK:c * GLA_CHUNK + 1, ksl]
                ddl = jnp.sum(dst * st, axis=0, keepdims=True) * dl_c
                ddl_scr[rows, ksl] = jnp.broadcast_to(ddl, (GLA_CHUNK, hk))
                dqd_scr[rows, ksl] = dqd_intra[rows] + jnp.dot(do[rows], st.astype(BF16), preferred_element_type=F32)
                dst_scr[h] = dl_c * dst + _dot_t(do[rows], qd[rows, ksl], 0, 0)
        dqd, dki, dks = dqd_scr[...], dki_scr[...], dks_scr[...]
        dq_ref[...] = (dqd * (scale * e_bc)).astype(BF16)
        dk_ref[...] = (dki * e_nbc + dks * e_st).astype(BF16)
        dks_ks = dks * ks_f
        dbc = dqd * qd_f - dki * ki_f - dks_ks
        dla = _split_dot(anti_bf, dbc, 2) + _split_dot(same_bf, dks_ks, 2) + ddl_scr[...]
        dz = (dla * (1.0 / GLA_TAU) * (1.0 - _sigmoid(z)))
        dz_bf = dz.astype(BF16)
        dal_ref[...] = _dot_t(dz_bf, wa2v, 1, 1).astype(BF16)
        dwa2_ref[...] += _dot_t(al.astype(BF16), dz_bf, 0, 0)
        dba2_ref[...] += jnp.sum(dz, axis=0, keepdims=True)

    rev = lambda i: nb - 1 - i
    return pl.pallas_call(
        body, name=name,
        grid=(nb,),
        in_specs=[
            pl.BlockSpec((tb, dk), lambda i: (rev(i), q_cb)),
            pl.BlockSpec((tb, dk), lambda i: (rev(i), k_cb)),
            pl.BlockSpec((tb, dv), lambda i: (rev(i), v_cb)),
            pl.BlockSpec((tb, dv), lambda i: (rev(i), r_cb)),
            pl.BlockSpec((tb, LANE), lambda i: (rev(i), 0)),
            pl.BlockSpec(wa2.shape, lambda i: (0, 0)),
            pl.BlockSpec(ba2.shape, lambda i: (0, 0)),
            pl.BlockSpec(ghn.shape, lambda i: (0, 0)),
            pl.BlockSpec((tb, dv), lambda i: (rev(i), 0)),
            pl.BlockSpec((nch, GLA_HEADS, hv, hk), lambda i: (rev(i), 0, 0, 0)),
            pl.BlockSpec((tb, dv), lambda i: (rev(i), 0)),
        ],
        out_specs=[
            pl.BlockSpec((tb, dk), lambda i: (rev(i), 0)),
            pl.BlockSpec((tb, dk), lambda i: (rev(i), 0)),
            pl.BlockSpec((tb, dv), lambda i: (rev(i), 0)),
            pl.BlockSpec((tb, dv), lambda i: (rev(i), 0)),
            pl.BlockSpec((tb, LANE), lambda i: (rev(i), 0)),
            pl.BlockSpec(wa2.shape, lambda i: (0, 0)),
            pl.BlockSpec(ba2.shape, lambda i: (0, 0)),
            pl.BlockSpec(ghn.shape, lambda i: (0, 0)),
        ],
        out_shape=[
            jax.ShapeDtypeStruct((t, dk), BF16),
            jax.ShapeDtypeStruct((t, dk), BF16),
            jax.ShapeDtypeStruct((t, dv), BF16),
            jax.ShapeDtypeStruct((t, dv), BF16),
            jax.ShapeDtypeStruct((t, LANE), BF16),
            jax.ShapeDtypeStruct(wa2.shape, F32),
            jax.ShapeDtypeStruct(ba2.shape, F32),
            jax.ShapeDtypeStruct(ghn.shape, F32),
        ],
        scratch_shapes=[pltpu.VMEM((GLA_HEADS, hv, hk), F32)] + [pltpu.VMEM((tb, dk), F32)] * 4,
        compiler_params=_cparams("arbitrary"),
    )(proj, proj, proj, proj, alow, wa2, ba2, ghn, opre, states, doa)


def _s5_tables(lam_re, lam_im, log_dt, b_re, b_im, c_re, c_im):
    hp = lax.Precision.HIGHEST
    g, p = lam_re.shape
    ln = S5_L
    dt = jnp.exp(log_dt)[:, None]
    lr, li = lam_re, lam_im
    mag = jnp.exp(lr * dt)
    ar, ai = mag * jnp.cos(li * dt), mag * jnp.sin(li * dt)
    den = lr * lr + li * li
    am1 = ar - 1.0
    f_re = ((am1 * lr + ai * li) / den)[..., None]
    f_im = ((ai * lr - am1 * li) / den)[..., None]
    bb_re = f_re * b_re - f_im * b_im
    bb_im = f_re * b_im + f_im * b_re
    j = jnp.arange(ln + 1, dtype=F32)[None, :, None]
    pm = jnp.exp(j * (lr * dt)[:, None, :])
    ang = j * (li * dt)[:, None, :]
    pw_re, pw_im = pm * jnp.cos(ang), pm * jnp.sin(ang)
    cp_re = c_re[:, None] * pw_re[:, :, None, :] - c_im[:, None] * pw_im[:, :, None, :]
    cp_im = c_re[:, None] * pw_im[:, :, None, :] + c_im[:, None] * pw_re[:, :, None, :]
    kj = (jnp.einsum("gjcp,gpd->gjcd", cp_re[:, :ln], bb_re, precision=hp)
          - jnp.einsum("gjcp,gpd->gjcd", cp_im[:, :ln], bb_im, precision=hp))
    eye = jnp.eye(S5_TILE_G, dtype=F32)
    nt = g // S5_TILE_G
    k8 = jnp.einsum("jglcd,gh->jlgdhc", kj.reshape(nt, S5_TILE_G, ln, S5_GC, S5_GC), eye).reshape(nt, ln, LANE, LANE)
    rp_re, rp_im = pw_re[:, ln - 1::-1], pw_im[:, ln - 1::-1]
    bbt_re, bbt_im = bb_re.transpose(0, 2, 1)[:, None], bb_im.transpose(0, 2, 1)[:, None]
    bst = jnp.stack([rp_re[:, :, None, :] * bbt_re - rp_im[:, :, None, :] * bbt_im,
                     rp_re[:, :, None, :] * bbt_im + rp_im[:, :, None, :] * bbt_re], axis=3)
    bc = bst.reshape(nt, S5_TILE_G, ln, S5_GC, 2 * p).transpose(0, 2, 1, 3, 4).reshape(nt, ln, LANE, 2 * p)
    cst = jnp.stack([cp_re[:, 1:], -cp_im[:, 1:]], axis=2)
    cc = cst.reshape(nt, S5_TILE_G, ln, 2, S5_GC, p).transpose(0, 2, 3, 5, 1, 4).reshape(nt, ln, 2 * p, LANE)
    a8 = jnp.stack([jnp.concatenate([pw_re[:, ln], pw_re[:, ln]], axis=-1),
                    jnp.concatenate([-pw_im[:, ln], pw_im[:, ln]], axis=-1)], axis=1)
    a8 = a8.reshape(nt, S5_TILE_G, 2, 2 * p).transpose(0, 2, 1, 3).reshape(nt, 2, S5_TILE_G * 2 * p)
    return k8, bc, cc, a8


def _swap_re_im(x):
    w = x.shape[1]
    if w == LANE:
        return pltpu.roll(x, LANE // 2, 1)
    first_half = (lax.broadcasted_iota(jnp.int32, x.shape, 1) & (LANE // 2)) == 0
    return jnp.where(first_half, pltpu.roll(x, w - LANE // 2, 1), pltpu.roll(x, LANE // 2, 1))


def _s5_expand(bc, cc, w):
    reps = w // LANE
    mask_b = (lax.broadcasted_iota(jnp.int32, (LANE, w), 0) // S5_GC) == (lax.broadcasted_iota(jnp.int32, (LANE, w), 1) // LANE)
    mask_c = (lax.broadcasted_iota(jnp.int32, (w, LANE), 0) // LANE) == (lax.broadcasted_iota(jnp.int32, (w, LANE), 1) // S5_GC)
    b8 = None if bc is None else jnp.where(mask_b, jnp.concatenate([bc] * reps, axis=1), jnp.zeros((), bc.dtype))
    c8 = None if cc is None else jnp.where(mask_c, jnp.concatenate([cc] * reps, axis=0), jnp.zeros((), cc.dtype))
    return b8, c8, mask_b, mask_c


def _state_scan(v, pr, pi, reverse):
    n = v.shape[0]
    row = lax.broadcasted_iota(jnp.int32, v.shape, 0)
    z, s = v, 1
    while s < n:
        if reverse:
            zs = jnp.where(row < n - s, pltpu.roll(z, n - s, 0), 0.0)
        else:
            zs = jnp.where(row >= s, pltpu.roll(z, s, 0), 0.0)
        z = z + zs * pr + _swap_re_im(zs) * pi
        pr, pi = pr * pr - pi * pi, 2.0 * pr * pi
        s *= 2
    return z


def _s5_fwd(proj, u_cb, k8, bc, cc, a8, name):
    t = proj.shape[0]
    nt, ln = bc.shape[:2]
    w = a8.shape[2]
    nc = t // ln
    rb = min(t, S5_ROW_BLOCK)

    def body(u_ref, k_ref, b_ref, c_ref, a_ref, y_ref, x_ref, uf_ref):
        uf_ref[...] = u_ref[...].astype(F32)
        pos = lax.broadcasted_iota(jnp.int32, (rb, LANE), 0) & (ln - 1)
        for r0 in range(0, t, rb):
            u = uf_ref[r0:r0 + rb, :]
            acc = jnp.dot(u.astype(BF16), k_ref[0], preferred_element_type=F32)
            for lag in range(1, ln):
                us = jnp.where(pos >= lag, pltpu.roll(u, lag, 0), 0.0).astype(BF16)
                acc = acc + jnp.dot(us, k_ref[lag], preferred_element_type=F32)
            y_ref[r0:r0 + rb, :] = acc
        v = None
        for s in range(ln):
            part = jnp.dot(uf_ref[pl.ds(s, nc, stride=ln), :].astype(BF16), _s5_expand(b_ref[s], None, w)[0],
                           preferred_element_type=F32)
            v = part if v is None else v + part
        z = _state_scan(v, a_ref[0:1, :], a_ref[1:2, :], reverse=False)
        row = lax.broadcasted_iota(jnp.int32, z.shape, 0)
        x = jnp.where(row >= 1, pltpu.roll(z, 1, 0), 0.0)
        x_ref[...] = x
        x_bf = x.astype(BF16)
        for tt in range(ln):
            y_ref[pl.ds(tt, nc, stride=ln), :] += jnp.dot(x_bf, _s5_expand(None, c_ref[tt], w)[1], preferred_element_type=F32)

    tile = lambda shape: pl.BlockSpec((None,) + shape, lambda j: (j,) + (0,) * len(shape))
    return pl.pallas_call(
        body, name=name, grid=(nt,),
        in_specs=[pl.BlockSpec((t, LANE), lambda j: (0, u_cb + j)), tile((ln, LANE, LANE)), tile(bc.shape[1:]),
                  tile(cc.shape[1:]), tile((2, w))],
        out_specs=[pl.BlockSpec((t, LANE), lambda j: (0, j)), tile((nc, w))],
        out_shape=[jax.ShapeDtypeStruct((t, nt * LANE), F32), jax.ShapeDtypeStruct((nt, nc, w), F32)],
        scratch_shapes=[pltpu.VMEM((t, LANE), F32)],
        compiler_params=_cparams("parallel"),
    )(proj, k8, bc, cc, a8)


def _s5_bwd_data(dy, x_st, k8, bc, cc, a8, name):
    t = dy.shape[0]
    nt, ln = bc.shape[:2]
    w = a8.shape[2]
    nc = t // ln
    rb = min(t, S5_ROW_BLOCK)

    def body(dy_ref, x_ref, k_ref, b_ref, c_ref, a_ref, du_ref, dv_ref, da_ref, dyf_ref, duf_ref):
        dyf_ref[...] = dy_ref[...].astype(F32)
        pos = lax.broadcasted_iota(jnp.int32, (rb, LANE), 0) & (ln - 1)
        for r0 in range(0, t, rb):
            g = dyf_ref[r0:r0 + rb, :]
            acc = _dot_t(g.astype(BF16), k_ref[0], 1, 1)
            for lag in range(1, ln):
                gs = jnp.where(pos < ln - lag, pltpu.roll(g, rb - lag, 0), 0.0).astype(BF16)
                acc = acc + _dot_t(gs, k_ref[lag], 1, 1)
            duf_ref[r0:r0 + rb, :] = acc
        gx = None
        for tt in range(ln):
            part = _dot_t(dyf_ref[pl.ds(tt, nc, stride=ln), :].astype(BF16), _s5_expand(None, c_ref[tt], w)[1], 1, 1)
            gx = part if gx is None else gx + part
        rtot = _state_scan(gx, a_ref[0:1, :], -a_ref[1:2, :], reverse=True)
        row = lax.broadcasted_iota(jnp.int32, rtot.shape, 0)
        dv = jnp.where(row < nc - 1, pltpu.roll(rtot, nc - 1, 0), 0.0)
        dv_ref[...] = dv
        dv_bf = dv.astype(BF16)
        for s in range(ln):
            duf_ref[pl.ds(s, nc, stride=ln), :] += _dot_t(dv_bf, _s5_expand(b_ref[s], None, w)[0], 1, 1)
        du_ref[...] = duf_ref[...].astype(BF16)
        x = x_ref[...]
        da_ref[0:1, :] = jnp.sum(dv * x, axis=0, keepdims=True)
        da_ref[1:2, :] = jnp.sum(dv * _swap_re_im(x), axis=0, keepdims=True)

    tile = lambda shape: pl.BlockSpec((None,) + shape, lambda j: (j,) + (0,) * len(shape))
    return pl.pallas_call(
        body, name=name, grid=(nt,),
        in_specs=[pl.BlockSpec((t, LANE), lambda j: (0, j)), tile((nc, w)), tile((ln, LANE, LANE)), tile(bc.shape[1:]),
                  tile(cc.shape[1:]), tile((2, w))],
        out_specs=[pl.BlockSpec((t, LANE), lambda j: (0, j)), tile((nc, w)), tile((2, w))],
        out_shape=[jax.ShapeDtypeStruct((t, nt * LANE), BF16), jax.ShapeDtypeStruct((nt, nc, w), F32),
                   jax.ShapeDtypeStruct((nt, 2, w), F32)],
        scratch_shapes=[pltpu.VMEM((t, LANE), F32), pltpu.VMEM((t, LANE), F32)],
        compiler_params=_cparams("parallel"),
    )(dy, x_st, k8, bc, cc, a8)


def _s5_bwd_tables(dy, proj, u_cb, x_st, dv, ln, name):
    t = dy.shape[0]
    nt, nc, w = x_st.shape
    rb = min(t, S5_ROW_BLOCK)

    def body(dy_ref, u_ref, x_ref, dv_ref, dk_ref, db_ref, dc_ref, dyf_ref, uf_ref):
        s = pl.program_id(1)

        @pl.when(s == 0)
        def _():
            dyf_ref[...] = dy_ref[...].astype(F32)
            uf_ref[...] = u_ref[...].astype(F32)
            pos = lax.broadcasted_iota(jnp.int32, (rb, LANE), 0) & (ln - 1)
            for r0 in range(0, t, rb):
                u, g_bf = uf_ref[r0:r0 + rb, :], dy_ref[r0:r0 + rb, :]
                for lag in range(ln):
                    us = u if lag == 0 else jnp.where(pos >= lag, pltpu.roll(u, lag, 0), 0.0)
                    part = _dot_t(us.astype(BF16), g_bf, 0, 0)
                    if r0 == 0:
                        dk_ref[lag] = part
                    else:
                        dk_ref[lag] += part

        rows = pl.ds(s, nc, stride=ln)
        _, _, mask_b, mask_c = _s5_expand(None, None, w)
        db = jnp.where(mask_b, _dot_t(uf_ref[rows, :].astype(BF16), dv_ref[...].astype(BF16), 0, 0), 0.0)
        dc = jnp.where(mask_c, _dot_t(x_ref[...].astype(BF16), dyf_ref[rows, :].astype(BF16), 0, 0), 0.0)
        db_ref[...] = sum(db[:, h * LANE:(h + 1) * LANE] for h in range(w // LANE))
        dc_ref[...] = sum(dc[h * LANE:(h + 1) * LANE, :] for h in range(w // LANE))

    tile = lambda shape: pl.BlockSpec((None,) + shape, lambda j, s: (j,) + (0,) * len(shape))
    per_s = lambda shape: pl.BlockSpec((None, None) + shape, lambda j, s: (j, s, 0, 0))
    return pl.pallas_call(
        body, name=name, grid=(nt, ln),
        in_specs=[pl.BlockSpec((t, LANE), lambda j, s: (0, j)), pl.BlockSpec((t, LANE), lambda j, s: (0, u_cb + j)),
                  tile((nc, w)), tile((nc, w))],
        out_specs=[tile((ln, LANE, LANE)), per_s((LANE, LANE)), per_s((LANE, LANE))],
        out_shape=[jax.ShapeDtypeStruct((nt, ln, LANE, LANE), F32)] * 3,
        scratch_shapes=[pltpu.VMEM((t, LANE), F32), pltpu.VMEM((t, LANE), F32)],
        compiler_params=_cparams("parallel", "arbitrary"),
    )(dy, proj, x_st, dv)


def _gelu_parts(y):
    inner = GELU_C * (y + GELU_A * y * y * y)
    th = jnp.tanh(inner)
    return th, 0.5 * y * (1.0 + th)


def _s5_post_fwd(y_raw, proj, u_cb, s5d, wglu, bglu, name):
    w = y_raw.shape[1]

    def fn(yr, u, dsk, wg, bg):
        y = yr + dsk * u.astype(F32)
        _, h = _gelu_parts(y)
        gl = jnp.dot(h.astype(BF16), wg, preferred_element_type=F32) + bg
        return (h * _sigmoid(gl),)

    return _rowwise(fn, [("row", y_raw), ("win", proj, w, u_cb), ("full", s5d), ("full", wglu), ("full", bglu)],
                    [(w, BF16)], [], rows=y_raw.shape[0], tb=_tile(y_raw.shape[0], 512), name=name)[0]


def _s5_post_bwd(y_raw, proj, u_cb, s5d, wglu, bglu, dob, name):
    w = y_raw.shape[1]

    def fn(yr, u, dsk, wg, bg, dov):
        u = u.astype(F32)
        dov = dov.astype(F32)
        y = yr + dsk * u
        th, h = _gelu_parts(y)
        h_bf = h.astype(BF16)
        gl = jnp.dot(h_bf, wg, preferred_element_type=F32) + bg
        sg = _sigmoid(gl)
        dgl = dov * h * sg * (1.0 - sg)
        dgl_bf = dgl.astype(BF16)
        dh = dov * sg + _dot_t(dgl_bf, wg, 1, 1)
        dgelu = 0.5 * (1.0 + th) + 0.5 * y * (1.0 - th * th) * GELU_C * (1.0 + 3.0 * GELU_A * y * y)
        dy = dh * dgelu
        return (dy, dy * dsk,
                _dot_t(h_bf, dgl_bf, 0, 0), jnp.sum(dgl, axis=0, keepdims=True), jnp.sum(dy * u, axis=0, keepdims=True))

    return _rowwise(fn, [("row", y_raw), ("win", proj, w, u_cb), ("full", s5d), ("full", wglu), ("full", bglu), ("row", dob)],
                    [(w, BF16), (w, BF16)], [(w, w), (1, w), (1, w)], rows=y_raw.shape[0], tb=_tile(y_raw.shape[0], 512), name=name)


def _adamw(w, g, m, v, name):
    _, rows, cols = w.shape
    tr, tc = (_tile(rows, 256, align=16), cols) if rows % 16 == 0 else (rows, _tile(cols, 256))
    slots = isinstance(g, (list, tuple))
    gs = list(g) if slots else [g]
    c1 = 1.0 - ADAM_B1 ** ADAM_STEP
    c2 = 1.0 - ADAM_B2 ** ADAM_STEP

    def body(w_ref, m_ref, v_ref, *refs):
        g_refs, out_refs = refs[:len(gs)], refs[len(gs):]
        if slots:
            parts = [g_ref[s].astype(F32) for g_ref in g_refs for s in range(g_ref.shape[0])]
            gv = parts[0]
            for p in parts[1:]:
                gv = gv + p
            out_refs[0][...] = gv
        else:
            gv = g_refs[0][...]
        d_ref, nm_ref, nv_ref = out_refs[-3:]
        nm = ADAM_B1 * m_ref[...] + (1.0 - ADAM_B1) * gv
        nv = ADAM_B2 * v_ref[...] + (1.0 - ADAM_B2) * (gv * gv)
        d_ref[...] = -ADAM_LR * ((nm / c1) / (jnp.sqrt(nv / c2) + ADAM_EPS) + ADAM_WD * w_ref[...])
        nm_ref[...] = nm
        nv_ref[...] = nv

    spec = pl.BlockSpec((None, tr, tc), lambda i, j: (0, i, j))
    g_specs = [pl.BlockSpec((a.shape[0], tr, tc), lambda i, j: (0, i, j)) for a in gs] if slots else [pl.BlockSpec((tr, tc), lambda i, j: (i, j))]
    n_out = 4 if slots else 3
    return pl.pallas_call(
        body, name=name, grid=(rows // tr, cols // tc),
        in_specs=[spec, spec, spec] + g_specs, out_specs=[spec] * n_out,
        out_shape=[jax.ShapeDtypeStruct((1, rows, cols), F32)] * n_out,
        compiler_params=_cparams("parallel", "parallel"),
    )(w, m, v, *gs)


def _slot_sum(x, name):
    _, rows, cols = x.shape
    if rows % 8 == 0:
        tr, tc = _tile(rows, 512, align=8), cols
    else:
        tr, tc = rows, _tile(cols, 256)

    def body(x_ref, o_ref):
        acc = x_ref[0].astype(F32)
        for s in range(1, N_DEV):
            acc = acc + x_ref[s].astype(F32)
        o_ref[...] = acc

    return pl.pallas_call(
        body, name=name, grid=(rows // tr, cols // tc),
        in_specs=[pl.BlockSpec((N_DEV, tr, tc), lambda i, j: (0, i, j))],
        out_specs=pl.BlockSpec((tr, tc), lambda i, j: (i, j)),
        out_shape=jax.ShapeDtypeStruct((rows, cols), F32),
        compiler_params=_cparams("parallel", "parallel"),
    )(x)


_REST = (("w_a2", 1), ("w_glu", 0), ("w_branch_a", 1), ("w_branch_b", 1), ("w_out", 0), ("w_ffn_in", 1), ("w_ffn_out", 0))
_SMALL = ("norm1_g", "b_a2", "gla_norm_g", "lam_re", "lam_im", "log_dt", "s5_b_re", "s5_b_im", "s5_c_re", "s5_c_im",
          "s5_d", "b_glu", "norm2_g", "final_norm_g")
_ORDER = ("norm1_g", "w_in", "w_a2", "b_a2", "gla_norm_g", "lam_re", "lam_im", "log_dt", "s5_b_re", "s5_b_im", "s5_c_re",
          "s5_c_im", "s5_d", "w_glu", "b_glu", "w_branch_a", "w_branch_b", "w_out", "norm2_g", "w_ffn_in", "w_ffn_out", "final_norm_g")


def _join_slots(slots, axis):
    _, r, c = slots.shape
    if axis == 0:
        return slots.reshape(N_DEV * r, c)
    return slots.transpose(1, 0, 2).reshape(r, N_DEV * c)


def _to_slots(full, axis):
    r, c = full.shape
    if axis == 0:
        return full.reshape(N_DEV, r // N_DEV, c)
    return full.reshape(r, N_DEV, c // N_DEV).transpose(1, 0, 2)


def _local_step(x, target, w_in_t, small, rest):
    t, d = x.shape
    dk, dv, s5w = d // 4, d // 2, d // 4
    dist = not isinstance(rest, dict)
    if dist:
        h1, (w_in_slots,) = _rms_fwd(x, small["norm1_g"], "norm1_fwd", carry=("ag", [w_in_t]))
        w_in_t = w_in_slots.reshape(-1, d)
    else:
        h1 = _rms_fwd(x, small["norm1_g"], "norm1_fwd")
    o_q, o_k, o_v, o_r, o_al = 0, dk, 2 * dk, 2 * dk + dv, 2 * dk + 2 * dv
    o_u = o_al + GLA_RANK
    o_ga, o_gb = o_u + s5w, o_u + s5w + d
    rows = lambda a, o, n: a[o:o + n]
    w_main_t = jnp.concatenate([rows(w_in_t, o_ga, d), rows(w_in_t, o_gb, d), rows(w_in_t, o_v, dv), rows(w_in_t, o_r, dv),
                                rows(w_in_t, o_q, dk), rows(w_in_t, o_k, dk), rows(w_in_t, o_u, s5w)], axis=0)
    w_al_t = jnp.pad(rows(w_in_t, o_al, GLA_RANK), ((0, LANE - GLA_RANK), (0, 0)))
    u_cb = (2 * d + 2 * dv + 2 * dk) // s5w

    if dist:
        proj, (a2_s, glu_s, w_ffn_in_s) = _mm(h1, w_main_t, tb=True, out_dtype=BF16,
                                              carry=("ag", [rest[0], rest[1], rest[5]]), name="in_proj")
        w = {"w_a2": _join_slots(a2_s, 1), "w_glu": _join_slots(glu_s, 0)}
    else:
        proj = _mm(h1, w_main_t, tb=True, out_dtype=BF16, name="in_proj")
        w = rest
        w_ffn_in_s = _to_slots(rest["w_ffn_in"], 1)
    wa2 = jnp.pad(w["w_a2"], ((0, LANE - GLA_RANK), (0, 0)))
    alow = _mm(h1, w_al_t, tb=True, out_dtype=BF16, name="in_proj_gate_rank")
    if dist:
        o_a, o_pre, states, got = _gla_fwd(proj, alow, wa2, small["b_a2"], small["gla_norm_g"], dk=dk, dv=dv, name="gla_fwd",
                                           carry=("ag", rest[2:5]))
        w.update({n: _join_slots(g, ax) for (n, ax), g in zip(_REST[2:5], got)})
    else:
        o_a, o_pre, states = _gla_fwd(proj, alow, wa2, small["b_a2"], small["gla_norm_g"], dk=dk, dv=dv, name="gla_fwd")

    s5_params = (small["lam_re"], small["lam_im"], small["log_dt"][0], small["s5_b_re"], small["s5_b_im"],
                 small["s5_c_re"], small["s5_c_im"])
    (k8, bc, cc, a8), tables_vjp = jax.vjp(_s5_tables, *s5_params)
    k8_bf, b8_bf, c8_bf = k8.astype(BF16), bc.astype(BF16), cc.astype(BF16)
    u_lane_cb = u_cb * s5w // LANE
    y_raw, x_st = _s5_fwd(proj, u_lane_cb, k8_bf, b8_bf, c8_bf, a8, "s5_scan_fwd")
    o_b = _s5_post_fwd(y_raw, proj, u_cb, small["s5_d"], w["w_glu"], small["b_glu"], "s5_post_fwd")

    pa = _mm(o_a, w["w_branch_a"], out_dtype=BF16, name="branch_a")
    pb = _mm(o_b, w["w_branch_b"], out_dtype=BF16, name="branch_b")
    mix = _mix_fwd(proj, pa, pb, d, "mix_fwd")
    x1 = _mm(mix, w["w_out"], res=x, name="out_proj")
    h2 = _rms_fwd(x1, small["norm2_g"], "norm2_fwd")
    if dist:
        gu, act, (w_ffn_out_s,) = _ffn_in_fused(h2, w_ffn_in_s, carry=("ag", rest[-1:]), name="ffn_in")
        w_ffn_out = _join_slots(w_ffn_out_s, 0)
    else:
        gu, act = _ffn_in_fused(h2, w_ffn_in_s, name="ffn_in")
        w_ffn_out = rest["w_ffn_out"]
    x2 = _mm(act, w_ffn_out, res=x1, name="ffn_out")
    dx2, dx2_bf, d_final_g, loss = _loss_head(x2, small["final_norm_g"], target, "loss_head")

    recv = {}
    dgu, = _mm(dx2_bf, w_ffn_out, tb=True, epi=(_swiglu_bwd_tile, [gu], [(2, BF16)]), name="d_act")
    g_ffn_out = _mm(act, dx2_bf, ta=True, out_dtype=BF16, name="g_w_ffn_out")
    if dist:
        g_ffn_in_s, recv["w_ffn_out"] = _mm(h2, dgu, ta=True, b_slots=True, out_dtype=BF16, out_slots=N_DEV,
                                            carry=("a2a", [_to_slots(g_ffn_out, 0)]), name="g_w_ffn_in")
        dh2, recv["w_ffn_in"] = _mm(dgu, w_ffn_in_s, tb=True, a_slots=True, b_slots=True, b_group=2,
                                    carry=("a2a", [g_ffn_in_s]), name="d_h2")
    else:
        g_ffn_in_s = _mm(h2, dgu, ta=True, b_slots=True, out_dtype=BF16, out_slots=N_DEV, name="g_w_ffn_in")
        dh2 = _mm(dgu, w_ffn_in_s, tb=True, a_slots=True, b_slots=True, b_group=2, name="d_h2")
    dx1, dx1_bf, d_norm2_g = _rms_bwd(x1, small["norm2_g"], dh2, dx2, "norm2_bwd", True)
    dmix = _mm(dx1_bf, w["w_out"], tb=True, out_dtype=BF16, name="d_mix")
    g_out = _mm(mix, dx1_bf, ta=True, out_dtype=BF16, name="g_w_out")
    dpa, dpb, dga, dgb = _mix_bwd(proj, pa, pb, dmix, d, "mix_bwd")
    doa = _mm(dpa, w["w_branch_a"], tb=True, out_dtype=BF16, name="d_o_a")
    dob = _mm(dpb, w["w_branch_b"], tb=True, out_dtype=BF16, name="d_o_b")
    g_branch_a = _mm(o_a, dpa, ta=True, out_dtype=BF16, name="g_w_branch_a")
    g_branch_b = _mm(o_b, dpb, ta=True, out_dtype=BF16, name="g_w_branch_b")

    dy_s5, du_direct, g_glu, g_bglu, g_s5d = _s5_post_bwd(y_raw, proj, u_cb, small["s5_d"], w["w_glu"], small["b_glu"], dob, "s5_post_bwd")
    du_scan, dv_st, d_a8 = _s5_bwd_data(dy_s5, x_st, k8_bf, b8_bf, c8_bf, a8, "s5_scan_bwd")
    d_k8, d_b8, d_c8 = _s5_bwd_tables(dy_s5, proj, u_lane_cb, x_st, dv_st, S5_L, "s5_scan_bwd_tables")
    g_lam_re, g_lam_im, g_log_dt, g_b_re, g_b_im, g_c_re, g_c_im = tables_vjp((d_k8, d_b8, d_c8, d_a8))
    du = du_scan + du_direct

    dq, dkk, dvv, dr, dal, g_wa2, g_ba2, g_ghn = _gla_bwd(proj, alow, wa2, small["b_a2"], small["gla_norm_g"], o_pre, states, doa,
                                                        dk=dk, dv=dv, name="gla_bwd")
    dproj = jnp.concatenate([dga, dgb, dvv, dr, dq, dkk, du], axis=1)
    mid = {"w_out": g_out, "w_branch_a": g_branch_a, "w_branch_b": g_branch_b, "w_glu": g_glu.astype(BF16),
           "w_a2": g_wa2[:GLA_RANK].astype(BF16)}
    if dist:
        axes = dict(_REST)
        g_main_t, got = _mm(dproj, h1, ta=True, out_dtype=BF16, name="g_w_in_main",
                            carry=("a2a", [_to_slots(mid[n], axes[n]) for n in mid]))
        recv.update(zip(mid, [[g] for g in got]))
    else:
        g_main_t = _mm(dproj, h1, ta=True, out_dtype=BF16, name="g_w_in_main")
    g_al_t = _mm(dal, h1, ta=True, out_dtype=BF16, name="g_w_in_gate_rank")
    mrows = lambda o, n: g_main_t[o:o + n]
    g_w_in_t = jnp.concatenate([mrows(2 * d + 2 * dv, dk), mrows(2 * d + 2 * dv + dk, dk), mrows(2 * d, dv), mrows(2 * d + dv, dv),
                                g_al_t[:GLA_RANK], mrows(2 * d + 2 * dv + 2 * dk, s5w), mrows(0, d), mrows(d, d)], axis=0)
    if dist:
        dh1, recv["w_in"] = _mm(dproj, w_main_t, carry=("a2a", [_to_slots(g_w_in_t, 0)]), name="d_h1_main")
    else:
        dh1 = _mm(dproj, w_main_t, name="d_h1_main")
    dh1 = _mm(dal, w_al_t, res=dh1, name="d_h1_gate_rank")
    grad_x, d_norm1_g = _rms_bwd(x, small["norm1_g"], dh1, dx1, "norm1_bwd", False)

    small_g = {
        "norm1_g": d_norm1_g, "b_a2": g_ba2, "gla_norm_g": g_ghn, "lam_re": g_lam_re, "lam_im": g_lam_im,
        "log_dt": g_log_dt[None], "s5_b_re": g_b_re, "s5_b_im": g_b_im, "s5_c_re": g_c_re, "s5_c_im": g_c_im,
        "s5_d": g_s5d, "b_glu": g_bglu, "norm2_g": d_norm2_g, "final_norm_g": d_final_g,
    }
    if not dist:
        recv = dict(mid, w_in=g_w_in_t, w_ffn_in=_join_slots(g_ffn_in_s, 1), w_ffn_out=g_ffn_out)
    return loss[0, 0], grad_x, recv, small_g


def _small_2d(name, a):
    a = a[0]
    return a[None] if a.ndim == 1 else a


def kernel(x, norm1_g, w_in, w_a2, b_a2, gla_norm_g, lam_re, lam_im, log_dt, s5_b_re, s5_b_im, s5_c_re, s5_c_im, s5_d, w_glu, b_glu, w_branch_a, w_branch_b, w_out, norm2_g, w_ffn_in, w_ffn_out, final_norm_g, loss_target, m_norm1_g, m_w_in, m_w_a2, m_b_a2, m_gla_norm_g, m_lam_re, m_lam_im, m_log_dt, m_s5_b_re, m_s5_b_im, m_s5_c_re, m_s5_c_im, m_s5_d, m_w_glu, m_b_glu, m_w_branch_a, m_w_branch_b, m_w_out, m_norm2_g, m_w_ffn_in, m_w_ffn_out, m_final_norm_g, v_norm1_g, v_w_in, v_w_a2, v_b_a2, v_gla_norm_g, v_lam_re, v_lam_im, v_log_dt, v_s5_b_re, v_s5_b_im, v_s5_c_re, v_s5_c_im, v_s5_d, v_w_glu, v_b_glu, v_w_branch_a, v_w_branch_b, v_w_out, v_norm2_g, v_w_ffn_in, v_w_ffn_out, v_final_norm_g):
    args = dict(locals())
    weights = {n: args[n] for n in _ORDER}
    m_in = {n: args["m_" + n] for n in _ORDER}
    v_in = {n: args["v_" + n] for n in _ORDER}
    transposed = lambda a: a[0].T[None]
    rest = [weights[n][0].astype(BF16) for n, _ in _REST]
    small = {n: _small_2d(n, weights[n]) for n in _SMALL}
    loss_local, grad_x, recv, small_g = _local_step(x[0], loss_target[0], transposed(weights["w_in"])[0].astype(BF16), small, rest)

    grads, delta, new_m, new_v = {}, {}, {}, {}
    for n, _ in _REST:
        grads[n], delta[n], new_m[n], new_v[n] = _adamw(weights[n], recv[n], m_in[n], v_in[n], "adamw_" + n)
    w_in_out = _adamw(transposed(weights["w_in"]), recv["w_in"], transposed(m_in["w_in"]), transposed(v_in["w_in"]), "adamw_w_in")
    grads["w_in"], delta["w_in"], new_m["w_in"], new_v["w_in"] = (transposed(a) for a in w_in_out)

    s_sizes = [small_g[n].size for n in _SMALL]
    s_offs = [sum(s_sizes[:i]) for i in range(len(s_sizes))]
    s_total = sum(s_sizes)
    s_rows = -(-(-(-(s_total + 1) // LANE)) // LANE) * LANE

    def pack_small(parts):
        flat = jnp.concatenate([p.reshape(-1) for p in parts])
        return jnp.pad(flat, (0, s_rows * LANE - flat.size)).reshape(s_rows, LANE)

    s_flat = pack_small([small_g[n] for n in _SMALL] + [loss_local])
    s_red = _slot_sum(_exchange("ag", [s_flat], "small_grads_all_gather")[0], "small_grads_slot_sum")
    loss = s_red.reshape(-1)[s_total]
    sd, sm, sv = _adamw(pack_small([weights[n] for n in _SMALL])[None], s_red, pack_small([m_in[n] for n in _SMALL])[None],
                        pack_small([v_in[n] for n in _SMALL])[None], "adamw_small")
    sd, sm, sv = sd[0], sm[0], sv[0]
    for n, o, s in zip(_SMALL, s_offs, s_sizes):
        shape = weights[n].shape[1:]
        grads[n], delta[n], new_m[n], new_v[n] = (a.reshape(-1)[o:o + s].reshape(shape) for a in (s_red, sd, sm, sv))

    out = [loss, grad_x[None]]
    for tree in (grads, delta, new_m, new_v):
        out += [tree[n].reshape(weights[n].shape) for n in _ORDER]
    return tuple(out)
```

```python
import functools
import math

import jax
import jax.numpy as jnp
from jax import lax
from jax.experimental import pallas as pl
from jax.experimental.pallas import tpu as pltpu

F32 = jnp.float32
BF16 = jnp.bfloat16

NORM_EPS = 1e-6
N_DEV = 8
N_PEER = N_DEV - 1
GLA_HEADS = 4
GLA_CHUNK = 32
GLA_CHUNK_SHIFT = 5
GLA_TAU = 16.0
GLA_RANK = 16
GLA_BLOCK = 256
S5_GC = 16
S5_P = 64
S5_L = 16
S5_TILE_G = 8
S5_ROW_BLOCK = 2048
LANE = 128
V7X_VMEM_LIMIT = 56 * 1024 * 1024
V7X_MM_VMEM_BUDGET = 40 * 1024 * 1024
V7X_MM_TILE_MN = 1408
V7X_MM_TILE_MN_WHOLE_K = 512
V7X_MM_TILE_K = 2048
V7X_EPI_ROW_CHUNKS = 4

ADAM_LR = 0.001
ADAM_B1 = 0.9
ADAM_B2 = 0.999
ADAM_EPS = 1e-08
ADAM_WD = 0.01
ADAM_STEP = 10

GELU_C = math.sqrt(2.0 / math.pi)
GELU_A = 0.044715

MESH = pl.DeviceIdType.MESH


def _cparams(*sem):
    return pltpu.CompilerParams(dimension_semantics=sem, vmem_limit_bytes=V7X_VMEM_LIMIT)


def _divisors_down(n, start, align=LANE):
    t = (min(start, n) // align) * align
    found = False
    while t >= align:
        if n % t == 0:
            found = True
            yield t
        t -= align
    if not found:
        yield n


def _tile(n, target, align=LANE):
    return next(_divisors_down(n, target, align))


def _sigmoid(x):
    return 1.0 / (1.0 + jnp.exp(-x))


_HBM_SPEC = pl.BlockSpec(memory_space=pltpu.HBM)


def _exchange_scratch(n):
    return [pltpu.SemaphoreType.DMA((n * N_PEER,)), pltpu.SemaphoreType.DMA((n * N_PEER,)), pltpu.SemaphoreType.DMA((n,))]


def _ag_phases(x_refs, out_refs, send_sems, recv_sems, local_sems):
    n = len(x_refs)
    x, y, c = lax.axis_index("x"), lax.axis_index("y"), lax.axis_index("c")
    me, sibling = (x, y, c), (x, y, 1 - c)
    chips = [(1 - x, y), (x, 1 - y), (1 - x, 1 - y)]

    def copy(a, k, block, to, from_input=False):
        dst = out_refs[a].at[4 * block[0] + 2 * block[1] + block[2]]
        return pltpu.make_async_remote_copy(
            src_ref=x_refs[a] if from_input else dst, dst_ref=dst,
            send_sem=send_sems.at[a * N_PEER + k], recv_sem=recv_sems.at[a * N_PEER + k], device_id=to, device_id_type=MESH)

    def local(a):
        return pltpu.make_async_copy(x_refs[a], out_refs[a].at[4 * x + 2 * y + c], local_sems.at[a])

    def first(a):
        return [copy(a, 0, me, sibling, True)] + [copy(a, 1 + j, me, (*chip, c), True) for j, chip in enumerate(chips)]

    def start():
        for a in range(n):
            local(a).start()
            for cp in first(a):
                cp.start()

    def relay():
        for j, chip in enumerate(chips):
            for a in range(n):
                copy(a, 1 + j, (*chip, c), me).wait_recv()
                copy(a, 4 + j, (*chip, c), sibling).start()

    def finish():
        for a in range(n):
            copy(a, 0, sibling, me).wait_recv()
            for j, chip in enumerate(chips):
                copy(a, 4 + j, (*chip, 1 - c), me).wait_recv()
        for a in range(n):
            for cp in first(a) + [copy(a, 4 + j, (*chip, c), sibling) for j, chip in enumerate(chips)]:
                cp.wait_send()
            local(a).wait()

    return start, relay, finish


_ALL_K = tuple(range(N_DEV))


def _a2a_phases(x_refs, out_refs, send_sems, recv_sems, local_sems, ks_list=None):
    n = len(x_refs)
    ks_list = ks_list or [_ALL_K] * n
    x, y, c = lax.axis_index("x"), lax.axis_index("y"), lax.axis_index("c")
    my = 4 * x + 2 * y + c

    def copy(a, k):
        px, py, pc = (1 - x if k & 4 else x), (1 - y if k & 2 else y), (1 - c if k & 1 else c)
        return pltpu.make_async_remote_copy(
            src_ref=x_refs[a].at[4 * px + 2 * py + pc], dst_ref=out_refs[a].at[ks_list[a].index(k)],
            send_sem=send_sems.at[a * N_PEER + k - 1], recv_sem=recv_sems.at[a * N_PEER + k - 1],
            device_id=(px, py, pc), device_id_type=MESH)

    def local(a):
        return pltpu.make_async_copy(x_refs[a].at[my], out_refs[a].at[ks_list[a].index(0)], local_sems.at[a])

    def start():
        for a in range(n):
            for k in ks_list[a]:
                (copy(a, k) if k else local(a)).start()

    def relay():
        pass

    def finish():
        for a in range(n):
            for k in ks_list[a]:
                if k:
                    copy(a, k).wait_recv()
        for a in range(n):
            for k in ks_list[a]:
                if k:
                    copy(a, k).wait_send()
                else:
                    local(a).wait()

    return start, relay, finish


def _exchange_out_shapes(kind, arrays, ks_list=None):
    if kind == "ag":
        return [jax.ShapeDtypeStruct((N_DEV,) + a.shape, a.dtype) for a in arrays]
    ks_list = ks_list or [_ALL_K] * len(arrays)
    return [jax.ShapeDtypeStruct((len(ks),) + a.shape[1:], a.dtype) for a, ks in zip(arrays, ks_list)]


def _exchange(kind, arrays, name):
    n = len(arrays)
    phases = _ag_phases if kind == "ag" else _a2a_phases

    def body(*refs):
        start, relay, finish = phases(refs[:n], refs[n:2 * n], *refs[2 * n:])
        start()
        relay()
        finish()

    return pl.pallas_call(
        body, name=name,
        out_shape=_exchange_out_shapes(kind, arrays),
        in_specs=[_HBM_SPEC] * n, out_specs=[_HBM_SPEC] * n,
        scratch_shapes=_exchange_scratch(n),
    )(*arrays)


def _mm_tiles(m, n_unit, k_unit, tile_bytes, small_tiles_ok=True):
    fits = lambda tm, tn, tk: 2 * 2 * (tm * tk + tk * tn) + tile_bytes * tm * tn <= V7X_MM_VMEM_BUDGET
    for cap in (V7X_MM_TILE_MN, V7X_MM_TILE_MN_WHOLE_K) if small_tiles_ok else (V7X_MM_TILE_MN,):
        tm, tn = _tile(m, cap), _tile(n_unit, cap)
        if fits(tm, tn, k_unit) and (tn >= V7X_MM_TILE_MN_WHOLE_K or tn == n_unit):
            return tm, tn, k_unit
    tm, tn = _tile(m, V7X_MM_TILE_MN), _tile(n_unit, V7X_MM_TILE_MN)
    for tk in _divisors_down(k_unit, V7X_MM_TILE_K):
        if fits(tm, tn, tk):
            return tm, tn, tk
    return tm, tn, _tile(k_unit, LANE)


def _carry_parts(carry):
    kind, arrays, ks_list = (tuple(carry) + (None,))[:3] if carry is not None else (None, [], None)
    n = len(arrays)
    kind = (kind, ks_list)
    return kind, arrays, [_HBM_SPEC] * n, _exchange_out_shapes(kind[0], arrays, ks_list), (_exchange_scratch(n) if n else [])


def _carry_hooks(kind, x_refs, out_refs, sems, step, last_step):
    if not x_refs:
        return lambda: None
    kind, ks_list = kind
    if kind == "ag":
        start, relay, finish = _ag_phases(x_refs, out_refs, *sems)
    else:
        start, relay, finish = _a2a_phases(x_refs, out_refs, *sems, ks_list=ks_list)
    pl.when(step == 0)(start)

    def after():
        if kind == "ag":
            pl.when(step == (last_step * 7) // 8)(relay)
        pl.when(step == last_step)(finish)

    return after


def _mm(a, b, *, ta=False, tb=False, out_dtype=F32, res=None, carry=None, a_slots=False, b_slots=False, b_group=0,
        out_slots=0, epi=None, m_limit=0, n_limit=0, tm_cap=0, name):
    if a_slots:
        assert not ta
        a_n, m, a_c = a.shape
        k = a_n * a_c
    else:
        m, k = (a.shape[1], a.shape[0]) if ta else a.shape
    if b_slots:
        b_n, b_r, b_c = b.shape
        k2, n = (b_n * b_c, b_r) if tb else (b_r, b_n * b_c)
    else:
        k2, n = (b.shape[1], b.shape[0]) if tb else b.shape
    assert k == k2, (a.shape, b.shape, ta, tb)
    m, n = m_limit or m, n_limit or n
    has_res = res is not None
    assert not (has_res and (out_slots or epi))
    n_units = [n] + ([n // out_slots] if out_slots else []) + ([b_c] if b_slots and not tb else [])
    k_units = [k] + ([a_c] if a_slots else []) + ([b_c] if b_slots and tb else [])
    n_unit, k_unit = min(n_units), min(k_units)
    assert all(u % n_unit == 0 for u in n_units) and all(u % k_unit == 0 for u in k_units)
    epi_fn, epi_ins, epi_outs = epi if epi is not None else (None, [], [])
    tile_bytes = 4 + (2 * res.dtype.itemsize if has_res else 0)
    tile_bytes += sum(2 * e.shape[0] * e.dtype.itemsize for e in epi_ins)
    tile_bytes += sum(2 * l * jnp.dtype(dt).itemsize for l, dt in epi_outs) if epi else 2 * jnp.dtype(out_dtype).itemsize
    tm, tn, tk = _mm_tiles(m, n_unit, k_unit, tile_bytes, small_tiles_ok=not epi)
    if tm_cap:
        tm = _tile(m, tm_cap)
    if b_group:
        tk = b_group * b_c
        assert b_slots and tb and k % tk == 0 and (not a_slots or a_c % tk == 0)
    ni, nj, nk = m // tm, n // tn, k // tk
    dims = (((0,) if ta else (1,), (1,) if tb else (0,)), ((), ()))

    def slot_map(per, pos):
        if pos == "k_cols":
            return lambda i, j, kk: (kk // per, i, kk % per)
        if pos == "k_cols_j":
            return lambda i, j, kk: (kk // per, j, kk % per)
        if pos == "n_cols_k":
            return lambda i, j, kk: (j // per, kk, j % per)
        return lambda i, j, kk: (j // per, i, j % per)

    if a_slots:
        a_spec = pl.BlockSpec((None, tm, tk), slot_map(a_c // tk, "k_cols"))
    else:
        a_spec = pl.BlockSpec((tk, tm), lambda i, j, kk: (kk, i)) if ta else pl.BlockSpec((tm, tk), lambda i, j, kk: (i, kk))
    if b_group:
        b_spec = pl.BlockSpec((b_group, tn, b_c), lambda i, j, kk: (kk, j, 0))
    elif b_slots and tb:
        b_spec = pl.BlockSpec((None, tn, tk), slot_map(b_c // tk, "k_cols_j"))
    elif b_slots:
        b_spec = pl.BlockSpec((None, tk, tn), slot_map(b_c // tn, "n_cols_k"))
    else:
        b_spec = pl.BlockSpec((tn, tk), lambda i, j, kk: (j, kk)) if tb else pl.BlockSpec((tk, tn), lambda i, j, kk: (kk, j))
    if epi:
        lead_spec = lambda l: pl.BlockSpec((l, tm, tn), lambda i, j, kk: (0, i, j))
        o_specs = [lead_spec(l) for l, _ in epi_outs]
        o_shapes = [jax.ShapeDtypeStruct((l, m, n), dt) for l, dt in epi_outs]
    elif out_slots:
        o_specs = [pl.BlockSpec((None, tm, tn), slot_map((n // out_slots) // tn, "n_cols_i"))]
        o_shapes = [jax.ShapeDtypeStruct((out_slots, m, n // out_slots), out_dtype)]
    else:
        o_specs = [pl.BlockSpec((tm, tn), lambda i, j, kk: (i, j))]
        o_shapes = [jax.ShapeDtypeStruct((m, n), out_dtype)]
    extra_ins = ([res] if has_res else []) + list(epi_ins)
    extra_specs = ([o_specs[0]] if has_res else []) + [pl.BlockSpec((e.shape[0], tm, tn), lambda i, j, kk: (0, i, j)) for e in epi_ins]
    n_in, n_out = 2 + len(extra_ins), len(o_specs)
    c_kind, c_arrays, c_specs, c_shapes, c_scratch = _carry_parts(carry)
    nc = len(c_arrays)
    last_step = ni * nj * nk - 1

    def body(*refs):
        a_ref, b_ref = refs[0], refs[1]
        e_refs = refs[2:n_in]
        x_refs = refs[n_in:n_in + nc]
        o_refs = refs[n_in + nc:n_in + nc + n_out]
        out_refs = refs[n_in + nc + n_out:n_in + 2 * nc + n_out]
        scratch = refs[n_in + 2 * nc + n_out:]
        acc = scratch[0] if nk > 1 else None
        kk = pl.program_id(2)
        step = (pl.program_id(0) * nj + pl.program_id(1)) * nk + kk
        after = _carry_hooks(c_kind, x_refs, out_refs, scratch[-3:], step, last_step)

        def emit(val):
            if has_res:
                val = val + e_refs[0][...].astype(F32)
            if epi:
                for o_ref, parts in zip(o_refs, epi_fn(val, *[e[...] for e in e_refs])):
                    for l, v in enumerate(parts):
                        o_ref[l] = v.astype(o_ref.dtype)
            else:
                o_refs[0][...] = val.astype(out_dtype)

        if epi and nk == 1 and not ta:
            rc = tm // V7X_EPI_ROW_CHUNKS
            for r in range(V7X_EPI_ROW_CHUNKS):
                rows = slice(r * rc, (r + 1) * rc)
                val = lax.dot_general(a_ref[rows, :], b_ref[...], dims, preferred_element_type=F32)
                for o_ref, parts in zip(o_refs, epi_fn(val, *[e[:, rows, :] for e in e_refs])):
                    for l, v in enumerate(parts):
                        o_ref[l, rows, :] = v.astype(o_ref.dtype)
            after()
            return
        if b_group:
            part = sum(lax.dot_general(a_ref[:, s * b_c:(s + 1) * b_c], b_ref[s], dims, preferred_element_type=F32)
                       for s in range(b_group))
        else:
            part = lax.dot_general(a_ref[...], b_ref[...], dims, preferred_element_type=F32)
        if nk == 1:
            emit(part)
        else:
            @pl.when(kk == 0)
            def _():
                acc[...] = part

            @pl.when(kk > 0)
            def _():
                acc[...] += part

            @pl.when(kk == nk - 1)
            def _():
                emit(acc[...])

        after()

    sem = ("arbitrary",) * 3 if nc else ("parallel", "parallel", "arbitrary")
    outs = pl.pallas_call(
        body, name=name,
        grid=(ni, nj, nk),
        in_specs=[a_spec, b_spec] + extra_specs + c_specs,
        out_specs=o_specs + c_specs,
        out_shape=o_shapes + c_shapes,
        scratch_shapes=([pltpu.VMEM((tm, tn), F32)] if nk > 1 else []) + c_scratch,
        compiler_params=_cparams(*sem),
    )(a, b, *extra_ins, *c_arrays)
    main = list(outs[:n_out]) if epi else outs[0]
    return (main, list(outs[n_out:])) if nc else main


def _ffn_in_fused(h2, w_s, *, carry=None, name):
    t, d = h2.shape
    n_slot, _, c = w_s.shape
    half = n_slot // 2
    tm = _tile(t, 512)
    c_kind, c_arrays, c_specs, c_shapes, c_scratch = _carry_parts(carry)
    nc = len(c_arrays)
    last_step = (t // tm) * half - 1

    def body(h_ref, wg_ref, wu_ref, *refs):
        x_refs, (gu_ref, act_ref), out_refs, sems = refs[:nc], refs[nc:nc + 2], refs[nc + 2:2 * nc + 2], refs[2 * nc + 2:]
        step = pl.program_id(0) * half + pl.program_id(1)
        after = _carry_hooks(c_kind, x_refs, out_refs, sems, step, last_step)
        h = h_ref[...]
        g = jnp.dot(h, wg_ref[...], preferred_element_type=F32)
        u = jnp.dot(h, wu_ref[...], preferred_element_type=F32)
        sg = _sigmoid(g)
        silu = g * sg
        gu_ref[0] = (u * (sg + silu - silu * sg)).astype(BF16)
        gu_ref[1] = silu.astype(BF16)
        act_ref[...] = (silu * u).astype(BF16)
        after()

    outs = pl.pallas_call(
        body, name=name,
        grid=(t // tm, half),
        in_specs=[pl.BlockSpec((tm, d), lambda i, j: (i, 0)),
                  pl.BlockSpec((None, d, c), lambda i, j: (j, 0, 0)),
                  pl.BlockSpec((None, d, c), lambda i, j: (half + j, 0, 0))] + c_specs,
        out_specs=[pl.BlockSpec((2, tm, c), lambda i, j: (0, i, j)), pl.BlockSpec((tm, c), lambda i, j: (i, j))] + c_specs,
        out_shape=[jax.ShapeDtypeStruct((2, t, half * c), BF16), jax.ShapeDtypeStruct((t, half * c), BF16)] + c_shapes,
        scratch_shapes=c_scratch,
        compiler_params=_cparams(*(("arbitrary",) * 2 if nc else ("parallel", "parallel"))),
    )(h2, w_s, w_s, *c_arrays)
    return (outs[0], outs[1], list(outs[2:])) if nc else (outs[0], outs[1])


def _rowwise(fn, ins, row_outs, acc_outs, *, rows, tb, name, carry=None):
    in_specs, args = [], []
    for spec in ins:
        kind, arr = spec[0], spec[1]
        if kind == "row":
            in_specs.append(pl.BlockSpec((tb, arr.shape[1]), lambda i: (i, 0)))
        elif kind == "win":
            width, cb = spec[2], spec[3]
            in_specs.append(pl.BlockSpec((tb, width), functools.partial(lambda i, cb: (i, cb), cb=cb)))
        else:
            in_specs.append(pl.BlockSpec(arr.shape, lambda i: (0, 0)))
        args.append(arr)
    out_specs = [pl.BlockSpec((tb, c), lambda i: (i, 0)) for c, _ in row_outs]
    out_specs += [pl.BlockSpec(shape, lambda i: (0, 0)) for shape in acc_outs]
    out_shape = [jax.ShapeDtypeStruct((rows, c), dt) for c, dt in row_outs]
    out_shape += [jax.ShapeDtypeStruct(shape, F32) for shape in acc_outs]
    n_in, n_row, n_out = len(ins), len(row_outs), len(row_outs) + len(acc_outs)
    c_kind, c_arrays, c_specs, c_shapes, c_scratch = _carry_parts(carry)
    nc = len(c_arrays)

    def body(*refs):
        after = _carry_hooks(c_kind, refs[n_in:n_in + nc], refs[n_in + nc + n_out:n_in + 2 * nc + n_out],
                             refs[n_in + 2 * nc + n_out:], pl.program_id(0), rows // tb - 1)
        vals = [r[...] for r in refs[:n_in]]
        outs = fn(*vals)
        if not isinstance(outs, (tuple, list)):
            outs = (outs,)
        out_refs = refs[n_in + nc:n_in + nc + n_out]
        for o_ref, val in zip(out_refs[:n_row], outs[:n_row]):
            o_ref[...] = val.astype(o_ref.dtype)
        first = pl.program_id(0) == 0
        for o_ref, val in zip(out_refs[n_row:], outs[n_row:]):
            @pl.when(first)
            def _(o_ref=o_ref):
                o_ref[...] = jnp.zeros_like(o_ref)
            o_ref[...] += val
        after()

    res = pl.pallas_call(
        body, name=name,
        grid=(rows // tb,),
        in_specs=in_specs + c_specs, out_specs=out_specs + c_specs, out_shape=out_shape + c_shapes,
        scratch_shapes=c_scratch,
        compiler_params=_cparams("arbitrary"),
    )(*args, *c_arrays)
    return (list(res[:n_out]), list(res[n_out:])) if nc else res


def _rms_fwd(x, g, name, carry=None):
    def fn(xv, gv):
        r = lax.rsqrt(jnp.mean(xv * xv, axis=-1, keepdims=True) + NORM_EPS)
        return (xv * r * gv,)
    res = _rowwise(fn, [("row", x), ("full", g)], [(x.shape[1], BF16)], [], rows=x.shape[0], tb=_tile(x.shape[0], 512),
                   name=name, carry=carry)
    return (res[0][0], res[1]) if carry is not None else res[0]


def _rms_bwd(x, g, dh, dres, name, want_bf16):
    d = x.shape[1]

    def fn(xv, gv, dhv, drv):
        r = lax.rsqrt(jnp.mean(xv * xv, axis=-1, keepdims=True) + NORM_EPS)
        xhat = xv * r
        dhv = dhv.astype(F32)
        dxhat = dhv * gv
        dx = drv + r * (dxhat - xhat * jnp.mean(dxhat * xhat, axis=-1, keepdims=True))
        dg = jnp.sum(dhv * xhat, axis=0, keepdims=True)
        return (dx, dx, dg) if want_bf16 else (dx, dg)

    row_outs = [(d, F32), (d, BF16)] if want_bf16 else [(d, F32)]
    return _rowwise(fn, [("row", x), ("full", g), ("row", dh), ("row", dres)], row_outs, [(1, d)],
                    rows=x.shape[0], tb=_tile(x.shape[0], 256), name=name)


def _loss_head(x2, g, target, name):
    d = x2.shape[1]

    def fn(xv, gv, tv):
        r = lax.rsqrt(jnp.mean(xv * xv, axis=-1, keepdims=True) + NORM_EPS)
        xhat = xv * r
        diff = xhat * gv - tv
        loss = 0.5 * jnp.sum(jnp.mean(diff * diff, axis=-1, keepdims=True), axis=0, keepdims=True)
        dy = diff * (1.0 / d)
        dxhat = dy * gv
        dx = r * (dxhat - xhat * jnp.mean(dxhat * xhat, axis=-1, keepdims=True))
        dg = jnp.sum(dy * xhat, axis=0, keepdims=True)
        return dx, dx, dg, jnp.broadcast_to(loss, (1, LANE))

    return _rowwise(fn, [("row", x2), ("full", g), ("row", target)], [(d, F32), (d, BF16)], [(1, d), (1, LANE)],
                    rows=x2.shape[0], tb=_tile(x2.shape[0], 256), name=name)


def _swiglu_bwd_tile(dact, dswiglu):
    return ((dact * dswiglu[0].astype(F32), dact * dswiglu[1].astype(F32)),)


def _mix_fwd(proj, pa, pb, d, name):
    def fn(ga, gb, av, bv):
        return (_sigmoid(ga.astype(F32)) * av.astype(F32) + _sigmoid(gb.astype(F32)) * bv.astype(F32),)
    return _rowwise(fn, [("win", proj, d, 0), ("win", proj, d, 1), ("row", pa), ("row", pb)], [(d, BF16)], [],
                    rows=pa.shape[0], tb=_tile(pa.shape[0], 512), name=name)[0]


def _mix_bwd(proj, pa, pb, dmix, d, name):
    def fn(ga, gb, av, bv, dm):
        dm = dm.astype(F32)
        sa, sb = _sigmoid(ga.astype(F32)), _sigmoid(gb.astype(F32))
        av, bv = av.astype(F32), bv.astype(F32)
        return dm * sa, dm * sb, dm * av * sa * (1.0 - sa), dm * bv * sb * (1.0 - sb)
    return _rowwise(fn, [("win", proj, d, 0), ("win", proj, d, 1), ("row", pa), ("row", pb), ("row", dmix)],
                    [(d, BF16)] * 4, [], rows=pa.shape[0], tb=_tile(pa.shape[0], 512), name=name)


def _chunk_masks(tb):
    r = lax.broadcasted_iota(jnp.int32, (tb, tb), 0)
    c = lax.broadcasted_iota(jnp.int32, (tb, tb), 1)
    same = lax.shift_right_logical(r, GLA_CHUNK_SHIFT) == lax.shift_right_logical(c, GLA_CHUNK_SHIFT)
    return same, same & (c <= r), same & (r <= c)


def _mask_bf16(mask):
    return jnp.where(mask, 1.0, 0.0).astype(BF16)


def _split_dot(mask_bf, x, terms):
    acc, rem = None, x
    for _ in range(terms):
        hi = rem.astype(BF16)
        part = jnp.dot(mask_bf, hi, preferred_element_type=F32)
        acc = part if acc is None else acc + part
        rem = rem - hi.astype(F32)
    return acc


def _gla_decay(al, wa2, ba2, same_bf, causal_bf):
    z = jnp.dot(al.astype(BF16), wa2, preferred_element_type=F32) + ba2
    la = (jnp.minimum(z, 0.0) - jnp.log(1.0 + jnp.exp(-jnp.abs(z)))) * (1.0 / GLA_TAU)
    bc = _split_dot(causal_bf, la, 3)
    bl = _split_dot(same_bf, la, 3)
    return z, bc, bl


def _dot_t(a, b, ca, cb):
    return lax.dot_general(a, b, (((ca,), (cb,)), ((), ())), preferred_element_type=F32)


def _gla_fwd(proj, alow, wa2, ba2, ghn, *, dk, dv, name, carry=None):
    t = proj.shape[0]
    tb = min(GLA_BLOCK, t)
    nch = tb // GLA_CHUNK
    hk, hv = dk // GLA_HEADS, dv // GLA_HEADS
    scale = hk ** -0.5
    v_cb, r_cb = (8 * dk) // dv, (8 * dk) // dv + 1
    q_cb, k_cb = (8 * dk + 2 * dv) // dk, (8 * dk + 2 * dv) // dk + 1
    c_kind, c_arrays, c_specs, c_shapes, c_scratch = _carry_parts(carry)
    nc = len(c_arrays)

    def body(q_ref, k_ref, v_ref, r_ref, al_ref, wa2_ref, ba2_ref, ghn_ref, *refs):
        x_refs, (oa_ref, opre_ref, s_ref), out_refs = refs[:nc], refs[nc:nc + 3], refs[nc + 3:2 * nc + 3]
        st_scr, sems = refs[2 * nc + 3], refs[2 * nc + 4:]
        after = _carry_hooks(c_kind, x_refs, out_refs, sems, pl.program_id(0), t // tb - 1)

        @pl.when(pl.program_id(0) == 0)
        def _():
            st_scr[...] = jnp.zeros_like(st_scr)

        same, causal, _ = _chunk_masks(tb)
        same_bf, causal_bf = _mask_bf16(same), _mask_bf16(causal)
        _, bc, bl = _gla_decay(al_ref[...], wa2_ref[...], ba2_ref[...], same_bf, causal_bf)
        q = q_ref[...].astype(F32) * scale
        k = k_ref[...].astype(F32)
        qd = (q * jnp.exp(bc)).astype(BF16)
        ki = (k * jnp.exp(-bc)).astype(BF16)
        ks = (k * jnp.exp(bl - bc)).astype(BF16)
        dl = jnp.exp(bl)
        ksls = [slice(h * hk, (h + 1) * hk) for h in range(GLA_HEADS)]
        vsls = [slice(h * hv, (h + 1) * hv) for h in range(GLA_HEADS)]
        v_hs = [v_ref[:, vsl] for vsl in vsls]
        o_intras = []
        for ksl, v_h in zip(ksls, v_hs):
            sc = jnp.where(causal, _dot_t(qd[:, ksl], ki[:, ksl], 1, 1), 0.0)
            o_intras.append(jnp.dot(sc.astype(BF16), v_h, preferred_element_type=F32))
        for c in range(nch):
            rows = slice(c * GLA_CHUNK, (c + 1) * GLA_CHUNK)
            for h, (ksl, vsl) in enumerate(zip(ksls, vsls)):
                st = st_scr[h]
                s_ref[c, h] = st
                opre_ref[rows, vsl] = o_intras[h][rows] + _dot_t(qd[rows, ksl], st.astype(BF16), 1, 1)
                st_scr[h] = dl[c * GLA_CHUNK:c * GLA_CHUNK + 1, ksl] * st + _dot_t(v_hs[h][rows], ks[rows, ksl], 0, 0)
        for h in range(GLA_HEADS):
            vsl = slice(h * hv, (h + 1) * hv)
            o = opre_ref[:, vsl]
            rs = lax.rsqrt(jnp.mean(o * o, axis=-1, keepdims=True) + NORM_EPS)
            rv = r_ref[:, vsl].astype(F32)
            oa_ref[:, vsl] = (rv * _sigmoid(rv) * (o * rs * ghn_ref[:, vsl])).astype(BF16)
        after()

    nchunks = t // GLA_CHUNK
    outs = pl.pallas_call(
        body, name=name,
        grid=(t // tb,),
        in_specs=[
            pl.BlockSpec((tb, dk), lambda i: (i, q_cb)),
            pl.BlockSpec((tb, dk), lambda i: (i, k_cb)),
            pl.BlockSpec((tb, dv), lambda i: (i, v_cb)),
            pl.BlockSpec((tb, dv), lambda i: (i, r_cb)),
            pl.BlockSpec((tb, LANE), lambda i: (i, 0)),
            pl.BlockSpec(wa2.shape, lambda i: (0, 0)),
            pl.BlockSpec(ba2.shape, lambda i: (0, 0)),
            pl.BlockSpec(ghn.shape, lambda i: (0, 0)),
        ] + c_specs,
        out_specs=[
            pl.BlockSpec((tb, dv), lambda i: (i, 0)),
            pl.BlockSpec((tb, dv), lambda i: (i, 0)),
            pl.BlockSpec((nch, GLA_HEADS, hv, hk), lambda i: (i, 0, 0, 0)),
        ] + c_specs,
        out_shape=[
            jax.ShapeDtypeStruct((t, dv), BF16),
            jax.ShapeDtypeStruct((t, dv), F32),
            jax.ShapeDtypeStruct((nchunks, GLA_HEADS, hv, hk), F32),
        ] + c_shapes,
        scratch_shapes=[pltpu.VMEM((GLA_HEADS, hv, hk), F32)] + c_scratch,
        compiler_params=_cparams("arbitrary"),
    )(proj, proj, proj, proj, alow, wa2, ba2, ghn, *c_arrays)
    return (outs[0], outs[1], outs[2], list(outs[3:])) if nc else tuple(outs)


def _gla_bwd(proj, alow, wa2, ba2, ghn, opre, states, doa, *, dk, dv, name):
    t = proj.shape[0]
    tb = min(GLA_BLOCK, t)
    nb = t // tb
    nch = tb // GLA_CHUNK
    hk, hv = dk // GLA_HEADS, dv // GLA_HEADS
    scale = hk ** -0.5
    v_cb, r_cb = (8 * dk) // dv, (8 * dk) // dv + 1
    q_cb, k_cb = (8 * dk + 2 * dv) // dk, (8 * dk + 2 * dv) // dk + 1

    def body(q_ref, k_ref, v_ref, r_ref, al_ref, wa2_ref, ba2_ref, ghn_ref, opre_ref, s_ref, doa_ref,
             dq_ref, dk_ref, dv_ref, dr_ref, dal_ref, dwa2_ref, dba2_ref, dghn_ref,
             dst_scr, dqd_scr, dki_scr, dks_scr, ddl_scr):
        @pl.when(pl.program_id(0) == 0)
        def _():
            dst_scr[...] = jnp.zeros_like(dst_scr)
            dwa2_ref[...] = jnp.zeros_like(dwa2_ref)
            dba2_ref[...] = jnp.zeros_like(dba2_ref)
            dghn_ref[...] = jnp.zeros_like(dghn_ref)

        same, causal, anti = _chunk_masks(tb)
        same_bf, causal_bf, anti_bf = _mask_bf16(same), _mask_bf16(causal), _mask_bf16(anti)
        al = al_ref[...]
        wa2v = wa2_ref[...]
        z, bc, bl = _gla_decay(al, wa2v, ba2_ref[...], same_bf, causal_bf)
        e_bc, e_nbc, e_st = jnp.exp(bc), jnp.exp(-bc), jnp.exp(bl - bc)
        q = q_ref[...].astype(F32) * scale
        k = k_ref[...].astype(F32)
        qd_f, ki_f, ks_f = q * e_bc, k * e_nbc, k * e_st
        qd, ki, ks = qd_f.astype(BF16), ki_f.astype(BF16), ks_f.astype(BF16)
        dl = jnp.exp(bl)
        per_head = []
        for h in range(GLA_HEADS):
            ksl = slice(h * hk, (h + 1) * hk)
            vsl = slice(h * hv, (h + 1) * hv)
            o = opre_ref[:, vsl]
            rs = lax.rsqrt(jnp.mean(o * o, axis=-1, keepdims=True) + NORM_EPS)
            ohat = o * rs
            g_h = ghn_ref[:, vsl]
            rv = r_ref[:, vsl].astype(F32)
            sg = _sigmoid(rv)
            d_oa = doa_ref[:, vsl].astype(F32)
            don = d_oa * (rv * sg)
            dr_ref[:, vsl] = (d_oa * (ohat * g_h) * (sg * (1.0 + rv * (1.0 - sg)))).astype(BF16)
            dghn_ref[:, vsl] += jnp.sum(don * ohat, axis=0, keepdims=True)
            dohat = don * g_h
            do_f = rs * (dohat - ohat * jnp.mean(dohat * ohat, axis=-1, keepdims=True))
            do = do_f.astype(BF16)
            v_h = v_ref[:, vsl]
            p = jnp.where(causal, _dot_t(do, v_h, 1, 1), 0.0).astype(BF16)
            dqd_intra = jnp.dot(p, ki[:, ksl], preferred_element_type=F32)
            dki_scr[:, ksl] = _dot_t(p, qd[:, ksl], 0, 0)
            sc = jnp.where(causal, _dot_t(qd[:, ksl], ki[:, ksl], 1, 1), 0.0).astype(BF16)
            dv_intra = _dot_t(sc, do, 0, 0)
            per_head.append((ksl, vsl, v_h, do, dqd_intra, dv_intra))
        for c in reversed(range(nch)):
            rows = slice(c * GLA_CHUNK, (c + 1) * GLA_CHUNK)
            for h, (ksl, vsl, v_h, do, dqd_intra, dv_intra) in enumerate(per_head):
                dst = dst_scr[h]
                st = s_ref[c, h]
                dst_bf = dst.astype(BF16)
                dv_ref[rows, vsl] = (dv_intra[rows] + _dot_t(ks[rows, ksl], dst_bf, 1, 1)).astype(BF16)
                dks_scr[rows, ksl] = jnp.dot(v_h[rows], dst_bf, preferred_element_type=F32)
                dl_c = dl[c * GLA_CHUNK:c * GLA_CHUNK + 1, ksl]
                ddl = jnp.sum(dst * st, axis=0, keepdims=True) * dl_c
                ddl_scr[rows, ksl] = jnp.broadcast_to(ddl, (GLA_CHUNK, hk))
                dqd_scr[rows, ksl] = dqd_intra[rows] + jnp.dot(do[rows], st.astype(BF16), preferred_element_type=F32)
                dst_scr[h] = dl_c * dst + _dot_t(do[rows], qd[rows, ksl], 0, 0)
        dqd, dki, dks = dqd_scr[...], dki_scr[...], dks_scr[...]
        dq_ref[...] = (dqd * (scale * e_bc)).astype(BF16)
        dk_ref[...] = (dki * e_nbc + dks * e_st).astype(BF16)
        dks_ks = dks * ks_f
        dbc = dqd * qd_f - dki * ki_f - dks_ks
        dla = _split_dot(anti_bf, dbc, 2) + _split_dot(same_bf, dks_ks, 2) + ddl_scr[...]
        dz = (dla * (1.0 / GLA_TAU) * (1.0 - _sigmoid(z)))
        dz_bf = dz.astype(BF16)
        dal_ref[...] = _dot_t(dz_bf, wa2v, 1, 1).astype(BF16)
        dwa2_ref[...] += _dot_t(al.astype(BF16), dz_bf, 0, 0)
        dba2_ref[...] += jnp.sum(dz, axis=0, keepdims=True)

    rev = lambda i: nb - 1 - i
    return pl.pallas_call(
        body, name=name,
        grid=(nb,),
        in_specs=[
            pl.BlockSpec((tb, dk), lambda i: (rev(i), q_cb)),
            pl.BlockSpec((tb, dk), lambda i: (rev(i), k_cb)),
            pl.BlockSpec((tb, dv), lambda i: (rev(i), v_cb)),
            pl.BlockSpec((tb, dv), lambda i: (rev(i), r_cb)),
            pl.BlockSpec((tb, LANE), lambda i: (rev(i), 0)),
            pl.BlockSpec(wa2.shape, lambda i: (0, 0)),
            pl.BlockSpec(ba2.shape, lambda i: (0, 0)),
            pl.BlockSpec(ghn.shape, lambda i: (0, 0)),
            pl.BlockSpec((tb, dv), lambda i: (rev(i), 0)),
            pl.BlockSpec((nch, GLA_HEADS, hv, hk), lambda i: (rev(i), 0, 0, 0)),
            pl.BlockSpec((tb, dv), lambda i: (rev(i), 0)),
        ],
        out_specs=[
            pl.BlockSpec((tb, dk), lambda i: (rev(i), 0)),
            pl.BlockSpec((tb, dk), lambda i: (rev(i), 0)),
            pl.BlockSpec((tb, dv), lambda i: (rev(i), 0)),
            pl.BlockSpec((tb, dv), lambda i: (rev(i), 0)),
            pl.BlockSpec((tb, LANE), lambda i: (rev(i), 0)),
            pl.BlockSpec(wa2.shape, lambda i: (0, 0)),
            pl.BlockSpec(ba2.shape, lambda i: (0, 0)),
            pl.BlockSpec(ghn.shape, lambda i: (0, 0)),
        ],
        out_shape=[
            jax.ShapeDtypeStruct((t, dk), BF16),
            jax.ShapeDtypeStruct((t, dk), BF16),
            jax.ShapeDtypeStruct((t, dv), BF16),
            jax.ShapeDtypeStruct((t, dv), BF16),
            jax.ShapeDtypeStruct((t, LANE), BF16),
            jax.ShapeDtypeStruct(wa2.shape, F32),
            jax.ShapeDtypeStruct(ba2.shape, F32),
            jax.ShapeDtypeStruct(ghn.shape, F32),
        ],
        scratch_shapes=[pltpu.VMEM((GLA_HEADS, hv, hk), F32)] + [pltpu.VMEM((tb, dk), F32)] * 4,
        compiler_params=_cparams("arbitrary"),
    )(proj, proj, proj, proj, alow, wa2, ba2, ghn, opre, states, doa)


def _s5_tables(lam_re, lam_im, log_dt, b_re, b_im, c_re, c_im):
    hp = lax.Precision.HIGHEST
    g, p = lam_re.shape
    ln = S5_L
    dt = jnp.exp(log_dt)[:, None]
    lr, li = lam_re, lam_im
    mag = jnp.exp(lr * dt)
    ar, ai = mag * jnp.cos(li * dt), mag * jnp.sin(li * dt)
    den = lr * lr + li * li
    am1 = ar - 1.0
    f_re = ((am1 * lr + ai * li) / den)[..., None]
    f_im = ((ai * lr - am1 * li) / den)[..., None]
    bb_re = f_re * b_re - f_im * b_im
    bb_im = f_re * b_im + f_im * b_re
    j = jnp.arange(ln + 1, dtype=F32)[None, :, None]
    pm = jnp.exp(j * (lr * dt)[:, None, :])
    ang = j * (li * dt)[:, None, :]
    pw_re, pw_im = pm * jnp.cos(ang), pm * jnp.sin(ang)
    cp_re = c_re[:, None] * pw_re[:, :, None, :] - c_im[:, None] * pw_im[:, :, None, :]
    cp_im = c_re[:, None] * pw_im[:, :, None, :] + c_im[:, None] * pw_re[:, :, None, :]
    kj = (jnp.einsum("gjcp,gpd->gjcd", cp_re[:, :ln], bb_re, precision=hp)
          - jnp.einsum("gjcp,gpd->gjcd", cp_im[:, :ln], bb_im, precision=hp))
    eye = jnp.eye(S5_TILE_G, dtype=F32)
    nt = g // S5_TILE_G
    k8 = jnp.einsum("jglcd,gh->jlgdhc", kj.reshape(nt, S5_TILE_G, ln, S5_GC, S5_GC), eye).reshape(nt, ln, LANE, LANE)
    rp_re, rp_im = pw_re[:, ln - 1::-1], pw_im[:, ln - 1::-1]
    bbt_re, bbt_im = bb_re.transpose(0, 2, 1)[:, None], bb_im.transpose(0, 2, 1)[:, None]
    bst = jnp.stack([rp_re[:, :, None, :] * bbt_re - rp_im[:, :, None, :] * bbt_im,
                     rp_re[:, :, None, :] * bbt_im + rp_im[:, :, None, :] * bbt_re], axis=3)
    bc = bst.reshape(nt, S5_TILE_G, ln, S5_GC, 2 * p).transpose(0, 2, 1, 3, 4).reshape(nt, ln, LANE, 2 * p)
    cst = jnp.stack([cp_re[:, 1:], -cp_im[:, 1:]], axis=2)
    cc = cst.reshape(nt, S5_TILE_G, ln, 2, S5_GC, p).transpose(0, 2, 3, 5, 1, 4).reshape(nt, ln, 2 * p, LANE)
    a8 = jnp.stack([jnp.concatenate([pw_re[:, ln], pw_re[:, ln]], axis=-1),
                    jnp.concatenate([-pw_im[:, ln], pw_im[:, ln]], axis=-1)], axis=1)
    a8 = a8.reshape(nt, S5_TILE_G, 2, 2 * p).transpose(0, 2, 1, 3).reshape(nt, 2, S5_TILE_G * 2 * p)
    return k8, bc, cc, a8


def _swap_re_im(x):
    w = x.shape[1]
    if w == LANE:
        return pltpu.roll(x, LANE // 2, 1)
    first_half = (lax.broadcasted_iota(jnp.int32, x.shape, 1) & (LANE // 2)) == 0
    return jnp.where(first_half, pltpu.roll(x, w - LANE // 2, 1), pltpu.roll(x, LANE // 2, 1))


def _s5_expand(bc, cc, w):
    reps = w // LANE
    mask_b = (lax.broadcasted_iota(jnp.int32, (LANE, w), 0) // S5_GC) == (lax.broadcasted_iota(jnp.int32, (LANE, w), 1) // LANE)
    mask_c = (lax.broadcasted_iota(jnp.int32, (w, LANE), 0) // LANE) == (lax.broadcasted_iota(jnp.int32, (w, LANE), 1) // S5_GC)
    b8 = None if bc is None else jnp.where(mask_b, jnp.concatenate([bc] * reps, axis=1), jnp.zeros((), bc.dtype))
    c8 = None if cc is None else jnp.where(mask_c, jnp.concatenate([cc] * reps, axis=0), jnp.zeros((), cc.dtype))
    return b8, c8, mask_b, mask_c


def _state_scan(v, pr, pi, reverse):
    n = v.shape[0]
    row = lax.broadcasted_iota(jnp.int32, v.shape, 0)
    z, s = v, 1
    while s < n:
        if reverse:
            zs = jnp.where(row < n - s, pltpu.roll(z, n - s, 0), 0.0)
        else:
            zs = jnp.where(row >= s, pltpu.roll(z, s, 0), 0.0)
        z = z + zs * pr + _swap_re_im(zs) * pi
        pr, pi = pr * pr - pi * pi, 2.0 * pr * pi
        s *= 2
    return z


def _s5_fwd(proj, u_cb, k8, bc, cc, a8, name):
    t = proj.shape[0]
    nt, ln = bc.shape[:2]
    w = a8.shape[2]
    nc = t // ln
    rb = min(t, S5_ROW_BLOCK)

    def body(u_ref, k_ref, b_ref, c_ref, a_ref, y_ref, x_ref, uf_ref):
        uf_ref[...] = u_ref[...].astype(F32)
        pos = lax.broadcasted_iota(jnp.int32, (rb, LANE), 0) & (ln - 1)
        for r0 in range(0, t, rb):
            u = uf_ref[r0:r0 + rb, :]
            acc = jnp.dot(u.astype(BF16), k_ref[0], preferred_element_type=F32)
            for lag in range(1, ln):
                us = jnp.where(pos >= lag, pltpu.roll(u, lag, 0), 0.0).astype(BF16)
                acc = acc + jnp.dot(us, k_ref[lag], preferred_element_type=F32)
            y_ref[r0:r0 + rb, :] = acc
        v = None
        for s in range(ln):
            part = jnp.dot(uf_ref[pl.ds(s, nc, stride=ln), :].astype(BF16), _s5_expand(b_ref[s], None, w)[0],
                           preferred_element_type=F32)
            v = part if v is None else v + part
        z = _state_scan(v, a_ref[0:1, :], a_ref[1:2, :], reverse=False)
        row = lax.broadcasted_iota(jnp.int32, z.shape, 0)
        x = jnp.where(row >= 1, pltpu.roll(z, 1, 0), 0.0)
        x_ref[...] = x
        x_bf = x.astype(BF16)
        for tt in range(ln):
            y_ref[pl.ds(tt, nc, stride=ln), :] += jnp.dot(x_bf, _s5_expand(None, c_ref[tt], w)[1], preferred_element_type=F32)

    tile = lambda shape: pl.BlockSpec((None,) + shape, lambda j: (j,) + (0,) * len(shape))
    return pl.pallas_call(
        body, name=name, grid=(nt,),
        in_specs=[pl.BlockSpec((t, LANE), lambda j: (0, u_cb + j)), tile((ln, LANE, LANE)), tile(bc.shape[1:]),
                  tile(cc.shape[1:]), tile((2, w))],
        out_specs=[pl.BlockSpec((t, LANE), lambda j: (0, j)), tile((nc, w))],
        out_shape=[jax.ShapeDtypeStruct((t, nt * LANE), F32), jax.ShapeDtypeStruct((nt, nc, w), F32)],
        scratch_shapes=[pltpu.VMEM((t, LANE), F32)],
        compiler_params=_cparams("parallel"),
    )(proj, k8, bc, cc, a8)


def _s5_bwd_data(dy, x_st, k8, bc, cc, a8, name):
    t = dy.shape[0]
    nt, ln = bc.shape[:2]
    w = a8.shape[2]
    nc = t // ln
    rb = min(t, S5_ROW_BLOCK)

    def body(dy_ref, x_ref, k_ref, b_ref, c_ref, a_ref, du_ref, dv_ref, da_ref, dyf_ref, duf_ref):
        dyf_ref[...] = dy_ref[...].astype(F32)
        pos = lax.broadcasted_iota(jnp.int32, (rb, LANE), 0) & (ln - 1)
        for r0 in range(0, t, rb):
            g = dyf_ref[r0:r0 + rb, :]
            acc = _dot_t(g.astype(BF16), k_ref[0], 1, 1)
            for lag in range(1, ln):
                gs = jnp.where(pos < ln - lag, pltpu.roll(g, rb - lag, 0), 0.0).astype(BF16)
                acc = acc + _dot_t(gs, k_ref[lag], 1, 1)
            duf_ref[r0:r0 + rb, :] = acc
        gx = None
        for tt in range(ln):
            part = _dot_t(dyf_ref[pl.ds(tt, nc, stride=ln), :].astype(BF16), _s5_expand(None, c_ref[tt], w)[1], 1, 1)
            gx = part if gx is None else gx + part
        rtot = _state_scan(gx, a_ref[0:1, :], -a_ref[1:2, :], reverse=True)
        row = lax.broadcasted_iota(jnp.int32, rtot.shape, 0)
        dv = jnp.where(row < nc - 1, pltpu.roll(rtot, nc - 1, 0), 0.0)
        dv_ref[...] = dv
        dv_bf = dv.astype(BF16)
        for s in range(ln):
            duf_ref[pl.ds(s, nc, stride=ln), :] += _dot_t(dv_bf, _s5_expand(b_ref[s], None, w)[0], 1, 1)
        du_ref[...] = duf_ref[...].astype(BF16)
        x = x_ref[...]
        da_ref[0:1, :] = jnp.sum(dv * x, axis=0, keepdims=True)
        da_ref[1:2, :] = jnp.sum(dv * _swap_re_im(x), axis=0, keepdims=True)

    tile = lambda shape: pl.BlockSpec((None,) + shape, lambda j: (j,) + (0,) * len(shape))
    return pl.pallas_call(
        body, name=name, grid=(nt,),
        in_specs=[pl.BlockSpec((t, LANE), lambda j: (0, j)), tile((nc, w)), tile((ln, LANE, LANE)), tile(bc.shape[1:]),
                  tile(cc.shape[1:]), tile((2, w))],
        out_specs=[pl.BlockSpec((t, LANE), lambda j: (0, j)), tile((nc, w)), tile((2, w))],
        out_shape=[jax.ShapeDtypeStruct((t, nt * LANE), BF16), jax.ShapeDtypeStruct((nt, nc, w), F32),
                   jax.ShapeDtypeStruct((nt, 2, w), F32)],
        scratch_shapes=[pltpu.VMEM((t, LANE), F32), pltpu.VMEM((t, LANE), F32)],
        compiler_params=_cparams("parallel"),
    )(dy, x_st, k8, bc, cc, a8)


def _s5_bwd_tables(dy, proj, u_cb, x_st, dv, ln, name):
    t = dy.shape[0]
    nt, nc, w = x_st.shape
    rb = min(t, S5_ROW_BLOCK)

    def body(dy_ref, u_ref, x_ref, dv_ref, dk_ref, db_ref, dc_ref, dyf_ref, uf_ref):
        s = pl.program_id(1)

        @pl.when(s == 0)
        def _():
            dyf_ref[...] = dy_ref[...].astype(F32)
            uf_ref[...] = u_ref[...].astype(F32)
            pos = lax.broadcasted_iota(jnp.int32, (rb, LANE), 0) & (ln - 1)
            for r0 in range(0, t, rb):
                u, g_bf = uf_ref[r0:r0 + rb, :], dy_ref[r0:r0 + rb, :]
                for lag in range(ln):
                    us = u if lag == 0 else jnp.where(pos >= lag, pltpu.roll(u, lag, 0), 0.0)
                    part = _dot_t(us.astype(BF16), g_bf, 0, 0)
                    if r0 == 0:
                        dk_ref[lag] = part
                    else:
                        dk_ref[lag] += part

        rows = pl.ds(s, nc, stride=ln)
        _, _, mask_b, mask_c = _s5_expand(None, None, w)
        db = jnp.where(mask_b, _dot_t(uf_ref[rows, :].astype(BF16), dv_ref[...].astype(BF16), 0, 0), 0.0)
        dc = jnp.where(mask_c, _dot_t(x_ref[...].astype(BF16), dyf_ref[rows, :].astype(BF16), 0, 0), 0.0)
        db_ref[...] = sum(db[:, h * LANE:(h + 1) * LANE] for h in range(w // LANE))
        dc_ref[...] = sum(dc[h * LANE:(h + 1) * LANE, :] for h in range(w // LANE))

    tile = lambda shape: pl.BlockSpec((None,) + shape, lambda j, s: (j,) + (0,) * len(shape))
    per_s = lambda shape: pl.BlockSpec((None, None) + shape, lambda j, s: (j, s, 0, 0))
    return pl.pallas_call(
        body, name=name, grid=(nt, ln),
        in_specs=[pl.BlockSpec((t, LANE), lambda j, s: (0, j)), pl.BlockSpec((t, LANE), lambda j, s: (0, u_cb + j)),
                  tile((nc, w)), tile((nc, w))],
        out_specs=[tile((ln, LANE, LANE)), per_s((LANE, LANE)), per_s((LANE, LANE))],
        out_shape=[jax.ShapeDtypeStruct((nt, ln, LANE, LANE), F32)] * 3,
        scratch_shapes=[pltpu.VMEM((t, LANE), F32), pltpu.VMEM((t, LANE), F32)],
        compiler_params=_cparams("parallel", "arbitrary"),
    )(dy, proj, x_st, dv)


def _gelu_parts(y):
    inner = GELU_C * (y + GELU_A * y * y * y)
    th = jnp.tanh(inner)
    return th, 0.5 * y * (1.0 + th)


def _s5_post_fwd(y_raw, proj, u_cb, s5d, wglu, bglu, name):
    w = y_raw.shape[1]

    def fn(yr, u, dsk, wg, bg):
        y = yr + dsk * u.astype(F32)
        _, h = _gelu_parts(y)
        gl = jnp.dot(h.astype(BF16), wg, preferred_element_type=F32) + bg
        return (h * _sigmoid(gl),)

    return _rowwise(fn, [("row", y_raw), ("win", proj, w, u_cb), ("full", s5d), ("full", wglu), ("full", bglu)],
                    [(w, BF16)], [], rows=y_raw.shape[0], tb=_tile(y_raw.shape[0], 512), name=name)[0]


def _s5_post_bwd(y_raw, proj, u_cb, s5d, wglu, bglu, dob, name):
    w = y_raw.shape[1]

    def fn(yr, u, dsk, wg, bg, dov):
        u = u.astype(F32)
        dov = dov.astype(F32)
        y = yr + dsk * u
        th, h = _gelu_parts(y)
        h_bf = h.astype(BF16)
        gl = jnp.dot(h_bf, wg, preferred_element_type=F32) + bg
        sg = _sigmoid(gl)
        dgl = dov * h * sg * (1.0 - sg)
        dgl_bf = dgl.astype(BF16)
        dh = dov * sg + _dot_t(dgl_bf, wg, 1, 1)
        dgelu = 0.5 * (1.0 + th) + 0.5 * y * (1.0 - th * th) * GELU_C * (1.0 + 3.0 * GELU_A * y * y)
        dy = dh * dgelu
        return (dy, dy * dsk,
                _dot_t(h_bf, dgl_bf, 0, 0), jnp.sum(dgl, axis=0, keepdims=True), jnp.sum(dy * u, axis=0, keepdims=True))

    return _rowwise(fn, [("row", y_raw), ("win", proj, w, u_cb), ("full", s5d), ("full", wglu), ("full", bglu), ("row", dob)],
                    [(w, BF16), (w, BF16)], [(w, w), (1, w), (1, w)], rows=y_raw.shape[0], tb=_tile(y_raw.shape[0], 512), name=name)


def _adamw(w, g, m, v, name):
    _, rows, cols = w.shape
    tr, tc = (_tile(rows, 256, align=16), cols) if rows % 16 == 0 else (rows, _tile(cols, 256))
    slots = isinstance(g, (list, tuple))
    gs = list(g) if slots else [g]
    c1 = 1.0 - ADAM_B1 ** ADAM_STEP
    c2 = 1.0 - ADAM_B2 ** ADAM_STEP

    def body(w_ref, m_ref, v_ref, *refs):
        g_refs, out_refs = refs[:len(gs)], refs[len(gs):]
        if slots:
            parts = [g_ref[s].astype(F32) for g_ref in g_refs for s in range(g_ref.shape[0])]
            gv = parts[0]
            for p in parts[1:]:
                gv = gv + p
            out_refs[0][...] = gv
        else:
            gv = g_refs[0][...]
        d_ref, nm_ref, nv_ref = out_refs[-3:]
        nm = ADAM_B1 * m_ref[...] + (1.0 - ADAM_B1) * gv
        nv = ADAM_B2 * v_ref[...] + (1.0 - ADAM_B2) * (gv * gv)
        d_ref[...] = -ADAM_LR * ((nm / c1) / (jnp.sqrt(nv / c2) + ADAM_EPS) + ADAM_WD * w_ref[...])
        nm_ref[...] = nm
        nv_ref[...] = nv

    spec = pl.BlockSpec((None, tr, tc), lambda i, j: (0, i, j))
    g_specs = [pl.BlockSpec((a.shape[0], tr, tc), lambda i, j: (0, i, j)) for a in gs] if slots else [pl.BlockSpec((tr, tc), lambda i, j: (i, j))]
    n_out = 4 if slots else 3
    return pl.pallas_call(
        body, name=name, grid=(rows // tr, cols // tc),
        in_specs=[spec, spec, spec] + g_specs, out_specs=[spec] * n_out,
        out_shape=[jax.ShapeDtypeStruct((1, rows, cols), F32)] * n_out,
        compiler_params=_cparams("parallel", "parallel"),
    )(w, m, v, *gs)


def _slot_sum(x, name):
    _, rows, cols = x.shape
    if rows % 8 == 0:
        tr, tc = _tile(rows, 512, align=8), cols
    else:
        tr, tc = rows, _tile(cols, 256)

    def body(x_ref, o_ref):
        acc = x_ref[0].astype(F32)
        for s in range(1, N_DEV):
            acc = acc + x_ref[s].astype(F32)
        o_ref[...] = acc

    return pl.pallas_call(
        body, name=name, grid=(rows // tr, cols // tc),
        in_specs=[pl.BlockSpec((N_DEV, tr, tc), lambda i, j: (0, i, j))],
        out_specs=pl.BlockSpec((tr, tc), lambda i, j: (i, j)),
        out_shape=jax.ShapeDtypeStruct((rows, cols), F32),
        compiler_params=_cparams("parallel", "parallel"),
    )(x)


_REST = (("w_a2", 1), ("w_glu", 0), ("w_branch_a", 1), ("w_branch_b", 1), ("w_out", 0), ("w_ffn_in", 1), ("w_ffn_out", 0))
_SMALL = ("norm1_g", "b_a2", "gla_norm_g", "lam_re", "lam_im", "log_dt", "s5_b_re", "s5_b_im", "s5_c_re", "s5_c_im",
          "s5_d", "b_glu", "norm2_g", "final_norm_g")
_ORDER = ("norm1_g", "w_in", "w_a2", "b_a2", "gla_norm_g", "lam_re", "lam_im", "log_dt", "s5_b_re", "s5_b_im", "s5_c_re",
          "s5_c_im", "s5_d", "w_glu", "b_glu", "w_branch_a", "w_branch_b", "w_out", "norm2_g", "w_ffn_in", "w_ffn_out", "final_norm_g")


def _join_slots(slots, axis):
    _, r, c = slots.shape
    if axis == 0:
        return slots.reshape(N_DEV * r, c)
    return slots.transpose(1, 0, 2).reshape(r, N_DEV * c)


def _to_slots(full, axis):
    r, c = full.shape
    if axis == 0:
        return full.reshape(N_DEV, r // N_DEV, c)
    return full.reshape(r, N_DEV, c // N_DEV).transpose(1, 0, 2)


def _local_step(x, target, w_in_t, small, rest):
    t, d = x.shape
    dk, dv, s5w = d // 4, d // 2, d // 4
    dist = not isinstance(rest, dict)
    if dist:
        h1, (w_in_slots,) = _rms_fwd(x, small["norm1_g"], "norm1_fwd", carry=("ag", [w_in_t]))
        w_in_t = w_in_slots.reshape(-1, d)
    else:
        h1 = _rms_fwd(x, small["norm1_g"], "norm1_fwd")
    o_q, o_k, o_v, o_r, o_al = 0, dk, 2 * dk, 2 * dk + dv, 2 * dk + 2 * dv
    o_u = o_al + GLA_RANK
    o_ga, o_gb = o_u + s5w, o_u + s5w + d
    rows = lambda a, o, n: a[o:o + n]
    w_al_t = jnp.pad(rows(w_in_t, o_al, GLA_RANK), ((0, LANE - GLA_RANK), (0, 0)))
    w_ext_t = jnp.concatenate([rows(w_in_t, o_ga, d), rows(w_in_t, o_gb, d), rows(w_in_t, o_v, dv), rows(w_in_t, o_r, dv),
                               rows(w_in_t, o_q, dk), rows(w_in_t, o_k, dk), rows(w_in_t, o_u, s5w), w_al_t], axis=0)
    n_main = 2 * d + 2 * dv + 2 * dk + s5w
    u_cb = (2 * d + 2 * dv + 2 * dk) // s5w

    if dist:
        proj, (a2_s, glu_s, w_ffn_in_s) = _mm(h1, w_ext_t, tb=True, n_limit=n_main, out_dtype=BF16,
                                              carry=("ag", [rest[0], rest[1], rest[5]]), name="in_proj")
        w = {"w_a2": _join_slots(a2_s, 1), "w_glu": _join_slots(glu_s, 0)}
    else:
        proj = _mm(h1, w_ext_t, tb=True, n_limit=n_main, out_dtype=BF16, name="in_proj")
        w = rest
        w_ffn_in_s = _to_slots(rest["w_ffn_in"], 1)
    wa2 = jnp.pad(w["w_a2"], ((0, LANE - GLA_RANK), (0, 0)))
    alow = _mm(h1, w_al_t, tb=True, out_dtype=BF16, name="in_proj_gate_rank")
    if dist:
        o_a, o_pre, states, got = _gla_fwd(proj, alow, wa2, small["b_a2"], small["gla_norm_g"], dk=dk, dv=dv, name="gla_fwd",
                                           carry=("ag", rest[2:5]))
        w.update({n: _join_slots(g, ax) for (n, ax), g in zip(_REST[2:5], got)})
    else:
        o_a, o_pre, states = _gla_fwd(proj, alow, wa2, small["b_a2"], small["gla_norm_g"], dk=dk, dv=dv, name="gla_fwd")

    s5_params = (small["lam_re"], small["lam_im"], small["log_dt"][0], small["s5_b_re"], small["s5_b_im"],
                 small["s5_c_re"], small["s5_c_im"])
    (k8, bc, cc, a8), tables_vjp = jax.vjp(_s5_tables, *s5_params)
    k8_bf, b8_bf, c8_bf = k8.astype(BF16), bc.astype(BF16), cc.astype(BF16)
    u_lane_cb = u_cb * s5w // LANE
    y_raw, x_st = _s5_fwd(proj, u_lane_cb, k8_bf, b8_bf, c8_bf, a8, "s5_scan_fwd")
    o_b = _s5_post_fwd(y_raw, proj, u_cb, small["s5_d"], w["w_glu"], small["b_glu"], "s5_post_fwd")

    pa = _mm(o_a, w["w_branch_a"], out_dtype=BF16, name="branch_a")
    pb = _mm(o_b, w["w_branch_b"], out_dtype=BF16, name="branch_b")
    mix = _mix_fwd(proj, pa, pb, d, "mix_fwd")
    x1 = _mm(mix, w["w_out"], res=x, name="out_proj")
    h2 = _rms_fwd(x1, small["norm2_g"], "norm2_fwd")
    if dist:
        gu, act, (w_ffn_out_s,) = _ffn_in_fused(h2, w_ffn_in_s, carry=("ag", rest[-1:]), name="ffn_in")
        w_ffn_out = _join_slots(w_ffn_out_s, 0)
    else:
        gu, act = _ffn_in_fused(h2, w_ffn_in_s, name="ffn_in")
        w_ffn_out = rest["w_ffn_out"]
    x2 = _mm(act, w_ffn_out, res=x1, name="ffn_out")
    dx2, dx2_bf, d_final_g, loss = _loss_head(x2, small["final_norm_g"], target, "loss_head")

    recv = {}
    dgu, = _mm(dx2_bf, w_ffn_out, tb=True, epi=(_swiglu_bwd_tile, [gu], [(2, BF16)]), name="d_act")
    g_ffn_out = _mm(act, dx2_bf, ta=True, out_dtype=BF16, name="g_w_ffn_out")
    if dist:
        g_ffn_in_s, recv["w_ffn_out"] = _mm(h2, dgu, ta=True, b_slots=True, out_dtype=BF16, out_slots=N_DEV,
                                            carry=("a2a", [_to_slots(g_ffn_out, 0)]), name="g_w_ffn_in")
        dh2, recv["w_ffn_in"] = _mm(dgu, w_ffn_in_s, tb=True, a_slots=True, b_slots=True, b_group=4, tm_cap=512,
                                    out_dtype=BF16, carry=("a2a", [g_ffn_in_s]), name="d_h2")
    else:
        g_ffn_in_s = _mm(h2, dgu, ta=True, b_slots=True, out_dtype=BF16, out_slots=N_DEV, name="g_w_ffn_in")
        dh2 = _mm(dgu, w_ffn_in_s, tb=True, a_slots=True, b_slots=True, b_group=4, tm_cap=512, out_dtype=BF16, name="d_h2")
    dx1, dx1_bf, d_norm2_g = _rms_bwd(x1, small["norm2_g"], dh2, dx2, "norm2_bwd", True)
    dmix = _mm(dx1_bf, w["w_out"], tb=True, out_dtype=BF16, name="d_mix")
    g_out = _mm(mix, dx1_bf, ta=True, out_dtype=BF16, name="g_w_out")
    dpa, dpb, dga, dgb = _mix_bwd(proj, pa, pb, dmix, d, "mix_bwd")
    doa = _mm(dpa, w["w_branch_a"], tb=True, out_dtype=BF16, name="d_o_a")
    dob = _mm(dpb, w["w_branch_b"], tb=True, out_dtype=BF16, name="d_o_b")
    g_branch_a = _mm(o_a, dpa, ta=True, out_dtype=BF16, name="g_w_branch_a")
    g_branch_b = _mm(o_b, dpb, ta=True, out_dtype=BF16, name="g_w_branch_b")

    dy_s5, du_direct, g_glu, g_bglu, g_s5d = _s5_post_bwd(y_raw, proj, u_cb, small["s5_d"], w["w_glu"], small["b_glu"], dob, "s5_post_bwd")
    du_scan, dv_st, d_a8 = _s5_bwd_data(dy_s5, x_st, k8_bf, b8_bf, c8_bf, a8, "s5_scan_bwd")
    d_k8, d_b8, d_c8 = _s5_bwd_tables(dy_s5, proj, u_lane_cb, x_st, dv_st, S5_L, "s5_scan_bwd_tables")
    g_lam_re, g_lam_im, g_log_dt, g_b_re, g_b_im, g_c_re, g_c_im = tables_vjp((d_k8, d_b8, d_c8, d_a8))
    du = du_scan + du_direct

    dq, dkk, dvv, dr, dal, g_wa2, g_ba2, g_ghn = _gla_bwd(proj, alow, wa2, small["b_a2"], small["gla_norm_g"], o_pre, states, doa,
                                                        dk=dk, dv=dv, name="gla_bwd")
    dproj = jnp.concatenate([dga, dgb, dvv, dr, dq, dkk, du, dal], axis=1)
    mid = {"w_out": g_out, "w_branch_a": g_branch_a, "w_branch_b": g_branch_b, "w_glu": g_glu.astype(BF16),
           "w_a2": g_wa2[:GLA_RANK].astype(BF16)}
    if dist:
        axes = dict(_REST)
        g_main_t, got = _mm(dproj, h1, ta=True, m_limit=n_main, out_dtype=BF16, name="g_w_in_main",
                            carry=("a2a", [_to_slots(mid[n], axes[n]) for n in mid]))
        recv.update(zip(mid, [[g] for g in got]))
    else:
        g_main_t = _mm(dproj, h1, ta=True, m_limit=n_main, out_dtype=BF16, name="g_w_in_main")
    g_al_t = _mm(dal, h1, ta=True, out_dtype=BF16, name="g_w_in_gate_rank")
    mrows = lambda o, n: g_main_t[o:o + n]
    g_w_in_t = jnp.concatenate([mrows(2 * d + 2 * dv, dk), mrows(2 * d + 2 * dv + dk, dk), mrows(2 * d, dv), mrows(2 * d + dv, dv),
                                g_al_t[:GLA_RANK], mrows(2 * d + 2 * dv + 2 * dk, s5w), mrows(0, d), mrows(d, d)], axis=0)
    if dist:
        dh1, recv["w_in"] = _mm(dproj, w_ext_t, out_dtype=BF16, carry=("a2a", [_to_slots(g_w_in_t, 0)]), name="d_h1")
    else:
        dh1 = _mm(dproj, w_ext_t, out_dtype=BF16, name="d_h1")
    grad_x, d_norm1_g = _rms_bwd(x, small["norm1_g"], dh1, dx1, "norm1_bwd", False)

    small_g = {
        "norm1_g": d_norm1_g, "b_a2": g_ba2, "gla_norm_g": g_ghn, "lam_re": g_lam_re, "lam_im": g_lam_im,
        "log_dt": g_log_dt[None], "s5_b_re": g_b_re, "s5_b_im": g_b_im, "s5_c_re": g_c_re, "s5_c_im": g_c_im,
        "s5_d": g_s5d, "b_glu": g_bglu, "norm2_g": d_norm2_g, "final_norm_g": d_final_g,
    }
    if not dist:
        recv = dict(mid, w_in=g_w_in_t, w_ffn_in=_join_slots(g_ffn_in_s, 1), w_ffn_out=g_ffn_out)
    return loss[0, 0], grad_x, recv, small_g


def _small_2d(name, a):
    a = a[0]
    return a[None] if a.ndim == 1 else a


def kernel(x, norm1_g, w_in, w_a2, b_a2, gla_norm_g, lam_re, lam_im, log_dt, s5_b_re, s5_b_im, s5_c_re, s5_c_im, s5_d, w_glu, b_glu, w_branch_a, w_branch_b, w_out, norm2_g, w_ffn_in, w_ffn_out, final_norm_g, loss_target, m_norm1_g, m_w_in, m_w_a2, m_b_a2, m_gla_norm_g, m_lam_re, m_lam_im, m_log_dt, m_s5_b_re, m_s5_b_im, m_s5_c_re, m_s5_c_im, m_s5_d, m_w_glu, m_b_glu, m_w_branch_a, m_w_branch_b, m_w_out, m_norm2_g, m_w_ffn_in, m_w_ffn_out, m_final_norm_g, v_norm1_g, v_w_in, v_w_a2, v_b_a2, v_gla_norm_g, v_lam_re, v_lam_im, v_log_dt, v_s5_b_re, v_s5_b_im, v_s5_c_re, v_s5_c_im, v_s5_d, v_w_glu, v_b_glu, v_w_branch_a, v_w_branch_b, v_w_out, v_norm2_g, v_w_ffn_in, v_w_ffn_out, v_final_norm_g):
    args = dict(locals())
    weights = {n: args[n] for n in _ORDER}
    m_in = {n: args["m_" + n] for n in _ORDER}
    v_in = {n: args["v_" + n] for n in _ORDER}
    transposed = lambda a: a[0].T[None]
    rest = [weights[n][0].astype(BF16) for n, _ in _REST]
    small = {n: _small_2d(n, weights[n]) for n in _SMALL}
    loss_local, grad_x, recv, small_g = _local_step(x[0], loss_target[0], transposed(weights["w_in"])[0].astype(BF16), small, rest)

    grads, delta, new_m, new_v = {}, {}, {}, {}
    for n, _ in _REST:
        grads[n], delta[n], new_m[n], new_v[n] = _adamw(weights[n], recv[n], m_in[n], v_in[n], "adamw_" + n)
    w_in_out = _adamw(transposed(weights["w_in"]), recv["w_in"], transposed(m_in["w_in"]), transposed(v_in["w_in"]), "adamw_w_in")
    grads["w_in"], delta["w_in"], new_m["w_in"], new_v["w_in"] = (transposed(a) for a in w_in_out)

    s_sizes = [small_g[n].size for n in _SMALL]
    s_offs = [sum(s_sizes[:i]) for i in range(len(s_sizes))]
    s_total = sum(s_sizes)
    s_rows = -(-(-(-(s_total + 1) // LANE)) // LANE) * LANE

    def pack_small(parts):
        flat = jnp.concatenate([p.reshape(-1) for p in parts])
        return jnp.pad(flat, (0, s_rows * LANE - flat.size)).reshape(s_rows, LANE)

    s_flat = pack_small([small_g[n] for n in _SMALL] + [loss_local])
    s_red = _slot_sum(_exchange("ag", [s_flat], "small_grads_all_gather")[0], "small_grads_slot_sum")
    loss = s_red.reshape(-1)[s_total]
    sd, sm, sv = _adamw(pack_small([weights[n] for n in _SMALL])[None], s_red, pack_small([m_in[n] for n in _SMALL])[None],
                        pack_small([v_in[n] for n in _SMALL])[None], "adamw_small")
    sd, sm, sv = sd[0], sm[0], sv[0]
    for n, o, s in zip(_SMALL, s_offs, s_sizes):
        shape = weights[n].shape[1:]
        grads[n], delta[n], new_m[n], new_v[n] = (a.reshape(-1)[o:o + s].reshape(shape) for a in (s_red, sd, sm, sv))

    out = [loss, grad_x[None]]
    for tree in (grads, delta, new_m, new_v):
        out += [tree[n].reshape(weights[n].shape) for n in _ORDER]
    return tuple(out)
```

```python
import functools
import math

import jax
import jax.numpy as jnp
from jax import lax
from jax.experimental import pallas as pl
from jax.experimental.pallas import tpu as pltpu

F32 = jnp.float32
BF16 = jnp.bfloat16

NORM_EPS = 1e-6
N_DEV = 8
N_PEER = N_DEV - 1
GLA_HEADS = 4
GLA_CHUNK = 32
GLA_CHUNK_SHIFT = 5
GLA_TAU = 16.0
GLA_RANK = 16
GLA_BLOCK = 256
S5_GC = 16
S5_P = 64
S5_L = 16
S5_TILE_G = 8
S5_ROW_BLOCK = 2048
LANE = 128
V7X_VMEM_LIMIT = 56 * 1024 * 1024
V7X_MM_VMEM_BUDGET = 40 * 1024 * 1024
V7X_MM_TILE_MN = 1408
V7X_MM_TILE_MN_WHOLE_K = 512
V7X_MM_TILE_K = 2048
V7X_EPI_ROW_CHUNKS = 4

ADAM_LR = 0.001
ADAM_B1 = 0.9
ADAM_B2 = 0.999
ADAM_EPS = 1e-08
ADAM_WD = 0.01
ADAM_STEP = 10

GELU_C = math.sqrt(2.0 / math.pi)
GELU_A = 0.044715

MESH = pl.DeviceIdType.MESH


def _cparams(*sem):
    return pltpu.CompilerParams(dimension_semantics=sem, vmem_limit_bytes=V7X_VMEM_LIMIT)


def _divisors_down(n, start, align=LANE):
    t = (min(start, n) // align) * align
    found = False
    while t >= align:
        if n % t == 0:
            found = True
            yield t
        t -= align
    if not found:
        yield n


def _tile(n, target, align=LANE):
    return next(_divisors_down(n, target, align))


def _sigmoid(x):
    return 1.0 / (1.0 + jnp.exp(-x))


_HBM_SPEC = pl.BlockSpec(memory_space=pltpu.HBM)


def _exchange_scratch(n):
    return [pltpu.SemaphoreType.DMA((n * N_PEER,)), pltpu.SemaphoreType.DMA((n * N_PEER,)), pltpu.SemaphoreType.DMA((n,))]


def _ag_phases(x_refs, out_refs, send_sems, recv_sems, local_sems):
    n = len(x_refs)
    x, y, c = lax.axis_index("x"), lax.axis_index("y"), lax.axis_index("c")
    me, sibling = (x, y, c), (x, y, 1 - c)
    chips = [(1 - x, y), (x, 1 - y), (1 - x, 1 - y)]

    def copy(a, k, block, to, from_input=False):
        dst = out_refs[a].at[4 * block[0] + 2 * block[1] + block[2]]
        return pltpu.make_async_remote_copy(
            src_ref=x_refs[a] if from_input else dst, dst_ref=dst,
            send_sem=send_sems.at[a * N_PEER + k], recv_sem=recv_sems.at[a * N_PEER + k], device_id=to, device_id_type=MESH)

    def local(a):
        return pltpu.make_async_copy(x_refs[a], out_refs[a].at[4 * x + 2 * y + c], local_sems.at[a])

    def first(a):
        return [copy(a, 0, me, sibling, True)] + [copy(a, 1 + j, me, (*chip, c), True) for j, chip in enumerate(chips)]

    def start():
        for a in range(n):
            local(a).start()
            for cp in first(a):
                cp.start()

    def relay():
        for j, chip in enumerate(chips):
            for a in range(n):
                copy(a, 1 + j, (*chip, c), me).wait_recv()
                copy(a, 4 + j, (*chip, c), sibling).start()

    def finish():
        for a in range(n):
            copy(a, 0, sibling, me).wait_recv()
            for j, chip in enumerate(chips):
                copy(a, 4 + j, (*chip, 1 - c), me).wait_recv()
        for a in range(n):
            for cp in first(a) + [copy(a, 4 + j, (*chip, c), sibling) for j, chip in enumerate(chips)]:
                cp.wait_send()
            local(a).wait()

    return start, relay, finish


_ALL_K = tuple(range(N_DEV))


def _a2a_phases(x_refs, out_refs, send_sems, recv_sems, local_sems, ks_list=None):
    n = len(x_refs)
    ks_list = ks_list or [_ALL_K] * n
    x, y, c = lax.axis_index("x"), lax.axis_index("y"), lax.axis_index("c")
    my = 4 * x + 2 * y + c

    def copy(a, k):
        px, py, pc = (1 - x if k & 4 else x), (1 - y if k & 2 else y), (1 - c if k & 1 else c)
        return pltpu.make_async_remote_copy(
            src_ref=x_refs[a].at[4 * px + 2 * py + pc], dst_ref=out_refs[a].at[ks_list[a].index(k)],
            send_sem=send_sems.at[a * N_PEER + k - 1], recv_sem=recv_sems.at[a * N_PEER + k - 1],
            device_id=(px, py, pc), device_id_type=MESH)

    def local(a):
        return pltpu.make_async_copy(x_refs[a].at[my], out_refs[a].at[ks_list[a].index(0)], local_sems.at[a])

    def start():
        for a in range(n):
            for k in ks_list[a]:
                (copy(a, k) if k else local(a)).start()

    def relay():
        pass

    def finish():
        for a in range(n):
            for k in ks_list[a]:
                if k:
                    copy(a, k).wait_recv()
        for a in range(n):
            for k in ks_list[a]:
                if k:
                    copy(a, k).wait_send()
                else:
                    local(a).wait()

    return start, relay, finish


def _exchange_out_shapes(kind, arrays, ks_list=None):
    if kind == "ag":
        return [jax.ShapeDtypeStruct((N_DEV,) + a.shape, a.dtype) for a in arrays]
    ks_list = ks_list or [_ALL_K] * len(arrays)
    return [jax.ShapeDtypeStruct((len(ks),) + a.shape[1:], a.dtype) for a, ks in zip(arrays, ks_list)]


def _exchange(kind, arrays, name):
    n = len(arrays)
    phases = _ag_phases if kind == "ag" else _a2a_phases

    def body(*refs):
        start, relay, finish = phases(refs[:n], refs[n:2 * n], *refs[2 * n:])
        start()
        relay()
        finish()

    return pl.pallas_call(
        body, name=name,
        out_shape=_exchange_out_shapes(kind, arrays),
        in_specs=[_HBM_SPEC] * n, out_specs=[_HBM_SPEC] * n,
        scratch_shapes=_exchange_scratch(n),
    )(*arrays)


def _mm_tiles(m, n_unit, k_unit, tile_bytes, small_tiles_ok=True):
    fits = lambda tm, tn, tk: 2 * 2 * (tm * tk + tk * tn) + tile_bytes * tm * tn <= V7X_MM_VMEM_BUDGET
    for cap in (V7X_MM_TILE_MN, V7X_MM_TILE_MN_WHOLE_K) if small_tiles_ok else (V7X_MM_TILE_MN,):
        tm, tn = _tile(m, cap), _tile(n_unit, cap)
        if fits(tm, tn, k_unit) and (tn >= V7X_MM_TILE_MN_WHOLE_K or tn == n_unit):
            return tm, tn, k_unit
    tm, tn = _tile(m, V7X_MM_TILE_MN), _tile(n_unit, V7X_MM_TILE_MN)
    for tk in _divisors_down(k_unit, V7X_MM_TILE_K):
        if fits(tm, tn, tk):
            return tm, tn, tk
    return tm, tn, _tile(k_unit, LANE)


def _carry_parts(carry):
    kind, arrays, ks_list = (tuple(carry) + (None,))[:3] if carry is not None else (None, [], None)
    n = len(arrays)
    kind = (kind, ks_list)
    return kind, arrays, [_HBM_SPEC] * n, _exchange_out_shapes(kind[0], arrays, ks_list), (_exchange_scratch(n) if n else [])


def _carry_hooks(kind, x_refs, out_refs, sems, step, last_step):
    if not x_refs:
        return lambda: None
    kind, ks_list = kind
    if kind == "ag":
        start, relay, finish = _ag_phases(x_refs, out_refs, *sems)
    else:
        start, relay, finish = _a2a_phases(x_refs, out_refs, *sems, ks_list=ks_list)
    pl.when(step == 0)(start)

    def after():
        if kind == "ag":
            pl.when(step == (last_step * 7) // 8)(relay)
        pl.when(step == last_step)(finish)

    return after


def _mm(a, b, *, ta=False, tb=False, out_dtype=F32, res=None, carry=None, a_slots=False, b_slots=False, b_group=0,
        out_slots=0, epi=None, m_limit=0, n_limit=0, tm_cap=0, name):
    if a_slots:
        assert not ta
        a_n, m, a_c = a.shape
        k = a_n * a_c
    else:
        m, k = (a.shape[1], a.shape[0]) if ta else a.shape
    if b_slots:
        b_n, b_r, b_c = b.shape
        k2, n = (b_n * b_c, b_r) if tb else (b_r, b_n * b_c)
    else:
        k2, n = (b.shape[1], b.shape[0]) if tb else b.shape
    assert k == k2, (a.shape, b.shape, ta, tb)
    m, n = m_limit or m, n_limit or n
    has_res = res is not None
    assert not (has_res and (out_slots or epi))
    n_units = [n] + ([n // out_slots] if out_slots else []) + ([b_c] if b_slots and not tb else [])
    k_units = [k] + ([a_c] if a_slots else []) + ([b_c] if b_slots and tb else [])
    n_unit, k_unit = min(n_units), min(k_units)
    assert all(u % n_unit == 0 for u in n_units) and all(u % k_unit == 0 for u in k_units)
    epi_fn, epi_ins, epi_outs = epi if epi is not None else (None, [], [])
    tile_bytes = 4 + (2 * res.dtype.itemsize if has_res else 0)
    tile_bytes += sum(2 * e.shape[0] * e.dtype.itemsize for e in epi_ins)
    tile_bytes += sum(2 * l * jnp.dtype(dt).itemsize for l, dt in epi_outs) if epi else 2 * jnp.dtype(out_dtype).itemsize
    tm, tn, tk = _mm_tiles(m, n_unit, k_unit, tile_bytes, small_tiles_ok=not epi)
    if tm_cap:
        tm = _tile(m, tm_cap)
    if b_group:
        tk = b_group * b_c
        assert b_slots and tb and k % tk == 0 and (not a_slots or a_c % tk == 0)
    ni, nj, nk = m // tm, n // tn, k // tk
    dims = (((0,) if ta else (1,), (1,) if tb else (0,)), ((), ()))

    def slot_map(per, pos):
        if pos == "k_cols":
            return lambda i, j, kk: (kk // per, i, kk % per)
        if pos == "k_cols_j":
            return lambda i, j, kk: (kk // per, j, kk % per)
        if pos == "n_cols_k":
            return lambda i, j, kk: (j // per, kk, j % per)
        return lambda i, j, kk: (j // per, i, j % per)

    if a_slots:
        a_spec = pl.BlockSpec((None, tm, tk), slot_map(a_c // tk, "k_cols"))
    else:
        a_spec = pl.BlockSpec((tk, tm), lambda i, j, kk: (kk, i)) if ta else pl.BlockSpec((tm, tk), lambda i, j, kk: (i, kk))
    if b_group:
        b_spec = pl.BlockSpec((b_group, tn, b_c), lambda i, j, kk: (kk, j, 0))
    elif b_slots and tb:
        b_spec = pl.BlockSpec((None, tn, tk), slot_map(b_c // tk, "k_cols_j"))
    elif b_slots:
        b_spec = pl.BlockSpec((None, tk, tn), slot_map(b_c // tn, "n_cols_k"))
    else:
        b_spec = pl.BlockSpec((tn, tk), lambda i, j, kk: (j, kk)) if tb else pl.BlockSpec((tk, tn), lambda i, j, kk: (kk, j))
    if epi:
        lead_spec = lambda l: pl.BlockSpec((l, tm, tn), lambda i, j, kk: (0, i, j))
        o_specs = [lead_spec(l) for l, _ in epi_outs]
        o_shapes = [jax.ShapeDtypeStruct((l, m, n), dt) for l, dt in epi_outs]
    elif out_slots:
        o_specs = [pl.BlockSpec((None, tm, tn), slot_map((n // out_slots) // tn, "n_cols_i"))]
        o_shapes = [jax.ShapeDtypeStruct((out_slots, m, n // out_slots), out_dtype)]
    else:
        o_specs = [pl.BlockSpec((tm, tn), lambda i, j, kk: (i, j))]
        o_shapes = [jax.ShapeDtypeStruct((m, n), out_dtype)]
    extra_ins = ([res] if has_res else []) + list(epi_ins)
    extra_specs = ([o_specs[0]] if has_res else []) + [pl.BlockSpec((e.shape[0], tm, tn), lambda i, j, kk: (0, i, j)) for e in epi_ins]
    n_in, n_out = 2 + len(extra_ins), len(o_specs)
    c_kind, c_arrays, c_specs, c_shapes, c_scratch = _carry_parts(carry)
    nc = len(c_arrays)
    last_step = ni * nj * nk - 1

    def body(*refs):
        a_ref, b_ref = refs[0], refs[1]
        e_refs = refs[2:n_in]
        x_refs = refs[n_in:n_in + nc]
        o_refs = refs[n_in + nc:n_in + nc + n_out]
        out_refs = refs[n_in + nc + n_out:n_in + 2 * nc + n_out]
        scratch = refs[n_in + 2 * nc + n_out:]
        acc = scratch[0] if nk > 1 else None
        kk = pl.program_id(2)
        step = (pl.program_id(0) * nj + pl.program_id(1)) * nk + kk
        after = _carry_hooks(c_kind, x_refs, out_refs, scratch[-3:], step, last_step)

        def emit(val):
            if has_res:
                val = val + e_refs[0][...].astype(F32)
            if epi:
                for o_ref, parts in zip(o_refs, epi_fn(val, *[e[...] for e in e_refs])):
                    for l, v in enumerate(parts):
                        o_ref[l] = v.astype(o_ref.dtype)
            else:
                o_refs[0][...] = val.astype(out_dtype)

        if epi and nk == 1 and not ta:
            rc = tm // V7X_EPI_ROW_CHUNKS
            for r in range(V7X_EPI_ROW_CHUNKS):
                rows = slice(r * rc, (r + 1) * rc)
                val = lax.dot_general(a_ref[rows, :], b_ref[...], dims, preferred_element_type=F32)
                for o_ref, parts in zip(o_refs, epi_fn(val, *[e[:, rows, :] for e in e_refs])):
                    for l, v in enumerate(parts):
                        o_ref[l, rows, :] = v.astype(o_ref.dtype)
            after()
            return
        if b_group:
            part = sum(lax.dot_general(a_ref[:, s * b_c:(s + 1) * b_c], b_ref[s], dims, preferred_element_type=F32)
                       for s in range(b_group))
        else:
            part = lax.dot_general(a_ref[...], b_ref[...], dims, preferred_element_type=F32)
        if nk == 1:
            emit(part)
        else:
            @pl.when(kk == 0)
            def _():
                acc[...] = part

            @pl.when(kk > 0)
            def _():
                acc[...] += part

            @pl.when(kk == nk - 1)
            def _():
                emit(acc[...])

        after()

    sem = ("arbitrary",) * 3 if nc else ("parallel", "parallel", "arbitrary")
    outs = pl.pallas_call(
        body, name=name,
        grid=(ni, nj, nk),
        in_specs=[a_spec, b_spec] + extra_specs + c_specs,
        out_specs=o_specs + c_specs,
        out_shape=o_shapes + c_shapes,
        scratch_shapes=([pltpu.VMEM((tm, tn), F32)] if nk > 1 else []) + c_scratch,
        compiler_params=_cparams(*sem),
    )(a, b, *extra_ins, *c_arrays)
    main = list(outs[:n_out]) if epi else outs[0]
    return (main, list(outs[n_out:])) if nc else main


def _ffn_in_fused(h2, w_s, *, carry=None, name):
    t, d = h2.shape
    n_slot, _, c = w_s.shape
    half = n_slot // 2
    tm = _tile(t, 512)
    c_kind, c_arrays, c_specs, c_shapes, c_scratch = _carry_parts(carry)
    nc = len(c_arrays)
    last_step = (t // tm) * half - 1

    def body(h_ref, wg_ref, wu_ref, *refs):
        x_refs, (gu_ref, act_ref), out_refs, sems = refs[:nc], refs[nc:nc + 2], refs[nc + 2:2 * nc + 2], refs[2 * nc + 2:]
        step = pl.program_id(0) * half + pl.program_id(1)
        after = _carry_hooks(c_kind, x_refs, out_refs, sems, step, last_step)
        h = h_ref[...]
        g = jnp.dot(h, wg_ref[...], preferred_element_type=F32)
        u = jnp.dot(h, wu_ref[...], preferred_element_type=F32)
        sg = _sigmoid(g)
        silu = g * sg
        gu_ref[0] = (u * (sg + silu - silu * sg)).astype(BF16)
        gu_ref[1] = silu.astype(BF16)
        act_ref[...] = (silu * u).astype(BF16)
        after()

    outs = pl.pallas_call(
        body, name=name,
        grid=(t // tm, half),
        in_specs=[pl.BlockSpec((tm, d), lambda i, j: (i, 0)),
                  pl.BlockSpec((None, d, c), lambda i, j: (j, 0, 0)),
                  pl.BlockSpec((None, d, c), lambda i, j: (half + j, 0, 0))] + c_specs,
        out_specs=[pl.BlockSpec((2, tm, c), lambda i, j: (0, i, j)), pl.BlockSpec((tm, c), lambda i, j: (i, j))] + c_specs,
        out_shape=[jax.ShapeDtypeStruct((2, t, half * c), BF16), jax.ShapeDtypeStruct((t, half * c), BF16)] + c_shapes,
        scratch_shapes=c_scratch,
        compiler_params=_cparams(*(("arbitrary",) * 2 if nc else ("parallel", "parallel"))),
    )(h2, w_s, w_s, *c_arrays)
    return (outs[0], outs[1], list(outs[2:])) if nc else (outs[0], outs[1])


def _rowwise(fn, ins, row_outs, acc_outs, *, rows, tb, name, carry=None):
    in_specs, args = [], []
    for spec in ins:
        kind, arr = spec[0], spec[1]
        if kind == "row":
            in_specs.append(pl.BlockSpec((tb, arr.shape[1]), lambda i: (i, 0)))
        elif kind == "win":
            width, cb = spec[2], spec[3]
            in_specs.append(pl.BlockSpec((tb, width), functools.partial(lambda i, cb: (i, cb), cb=cb)))
        else:
            in_specs.append(pl.BlockSpec(arr.shape, lambda i: (0, 0)))
        args.append(arr)
    out_specs = [pl.BlockSpec((tb, c), lambda i: (i, 0)) for c, _ in row_outs]
    out_specs += [pl.BlockSpec(shape, lambda i: (0, 0)) for shape in acc_outs]
    out_shape = [jax.ShapeDtypeStruct((rows, c), dt) for c, dt in row_outs]
    out_shape += [jax.ShapeDtypeStruct(shape, F32) for shape in acc_outs]
    n_in, n_row, n_out = len(ins), len(row_outs), len(row_outs) + len(acc_outs)
    c_kind, c_arrays, c_specs, c_shapes, c_scratch = _carry_parts(carry)
    nc = len(c_arrays)

    def body(*refs):
        after = _carry_hooks(c_kind, refs[n_in:n_in + nc], refs[n_in + nc + n_out:n_in + 2 * nc + n_out],
                             refs[n_in + 2 * nc + n_out:], pl.program_id(0), rows // tb - 1)
        vals = [r[...] for r in refs[:n_in]]
        outs = fn(*vals)
        if not isinstance(outs, (tuple, list)):
            outs = (outs,)
        out_refs = refs[n_in + nc:n_in + nc + n_out]
        for o_ref, val in zip(out_refs[:n_row], outs[:n_row]):
            o_ref[...] = val.astype(o_ref.dtype)
        first = pl.program_id(0) == 0
        for o_ref, val in zip(out_refs[n_row:], outs[n_row:]):
            @pl.when(first)
            def _(o_ref=o_ref):
                o_ref[...] = jnp.zeros_like(o_ref)
            o_ref[...] += val
        after()

    res = pl.pallas_call(
        body, name=name,
        grid=(rows // tb,),
        in_specs=in_specs + c_specs, out_specs=out_specs + c_specs, out_shape=out_shape + c_shapes,
        scratch_shapes=c_scratch,
        compiler_params=_cparams("arbitrary"),
    )(*args, *c_arrays)
    return (list(res[:n_out]), list(res[n_out:])) if nc else res


def _rms_fwd(x, g, name, carry=None):
    def fn(xv, gv):
        r = lax.rsqrt(jnp.mean(xv * xv, axis=-1, keepdims=True) + NORM_EPS)
        return (xv * r * gv,)
    res = _rowwise(fn, [("row", x), ("full", g)], [(x.shape[1], BF16)], [], rows=x.shape[0], tb=_tile(x.shape[0], 512),
                   name=name, carry=carry)
    return (res[0][0], res[1]) if carry is not None else res[0]


def _rms_bwd(x, g, dh, dres, name, want_bf16):
    d = x.shape[1]

    def fn(xv, gv, dhv, drv):
        r = lax.rsqrt(jnp.mean(xv * xv, axis=-1, keepdims=True) + NORM_EPS)
        xhat = xv * r
        dhv = dhv.astype(F32)
        dxhat = dhv * gv
        dx = drv + r * (dxhat - xhat * jnp.mean(dxhat * xhat, axis=-1, keepdims=True))
        dg = jnp.sum(dhv * xhat, axis=0, keepdims=True)
        return (dx, dx, dg) if want_bf16 else (dx, dg)

    row_outs = [(d, F32), (d, BF16)] if want_bf16 else [(d, F32)]
    return _rowwise(fn, [("row", x), ("full", g), ("row", dh), ("row", dres)], row_outs, [(1, d)],
                    rows=x.shape[0], tb=_tile(x.shape[0], 256), name=name)


def _loss_head(x2, g, target, name):
    d = x2.shape[1]

    def fn(xv, gv, tv):
        r = lax.rsqrt(jnp.mean(xv * xv, axis=-1, keepdims=True) + NORM_EPS)
        xhat = xv * r
        diff = xhat * gv - tv
        loss = 0.5 * jnp.sum(jnp.mean(diff * diff, axis=-1, keepdims=True), axis=0, keepdims=True)
        dy = diff * (1.0 / d)
        dxhat = dy * gv
        dx = r * (dxhat - xhat * jnp.mean(dxhat * xhat, axis=-1, keepdims=True))
        dg = jnp.sum(dy * xhat, axis=0, keepdims=True)
        return dx, dx, dg, jnp.broadcast_to(loss, (1, LANE))

    return _rowwise(fn, [("row", x2), ("full", g), ("row", target)], [(d, F32), (d, BF16)], [(1, d), (1, LANE)],
                    rows=x2.shape[0], tb=_tile(x2.shape[0], 256), name=name)


def _swiglu_bwd_tile(dact, dswiglu):
    return ((dact * dswiglu[0].astype(F32), dact * dswiglu[1].astype(F32)),)


def _mix_fwd(proj, pa, pb, d, name):
    def fn(ga, gb, av, bv):
        return (_sigmoid(ga.astype(F32)) * av.astype(F32) + _sigmoid(gb.astype(F32)) * bv.astype(F32),)
    return _rowwise(fn, [("win", proj, d, 0), ("win", proj, d, 1), ("row", pa), ("row", pb)], [(d, BF16)], [],
                    rows=pa.shape[0], tb=_tile(pa.shape[0], 512), name=name)[0]


def _mix_bwd(proj, pa, pb, dmix, d, name):
    def fn(ga, gb, av, bv, dm):
        dm = dm.astype(F32)
        sa, sb = _sigmoid(ga.astype(F32)), _sigmoid(gb.astype(F32))
        av, bv = av.astype(F32), bv.astype(F32)
        return dm * sa, dm * sb, dm * av * sa * (1.0 - sa), dm * bv * sb * (1.0 - sb)
    return _rowwise(fn, [("win", proj, d, 0), ("win", proj, d, 1), ("row", pa), ("row", pb), ("row", dmix)],
                    [(d, BF16)] * 4, [], rows=pa.shape[0], tb=_tile(pa.shape[0], 512), name=name)


def _chunk_masks(tb):
    r = lax.broadcasted_iota(jnp.int32, (tb, tb), 0)
    c = lax.broadcasted_iota(jnp.int32, (tb, tb), 1)
    same = lax.shift_right_logical(r, GLA_CHUNK_SHIFT) == lax.shift_right_logical(c, GLA_CHUNK_SHIFT)
    return same, same & (c <= r), same & (r <= c)


def _mask_bf16(mask):
    return jnp.where(mask, 1.0, 0.0).astype(BF16)


def _split_dot(mask_bf, x, terms):
    acc, rem = None, x
    for _ in range(terms):
        hi = rem.astype(BF16)
        part = jnp.dot(mask_bf, hi, preferred_element_type=F32)
        acc = part if acc is None else acc + part
        rem = rem - hi.astype(F32)
    return acc


def _gla_decay(al, wa2, ba2, same_bf, causal_bf):
    z = jnp.dot(al.astype(BF16), wa2, preferred_element_type=F32) + ba2
    la = (jnp.minimum(z, 0.0) - jnp.log(1.0 + jnp.exp(-jnp.abs(z)))) * (1.0 / GLA_TAU)
    bc = _split_dot(causal_bf, la, 3)
    bl = _split_dot(same_bf, la, 3)
    return z, bc, bl


def _dot_t(a, b, ca, cb):
    return lax.dot_general(a, b, (((ca,), (cb,)), ((), ())), preferred_element_type=F32)


def _gla_fwd(proj, alow, wa2, ba2, ghn, *, dk, dv, name, carry=None):
    t = proj.shape[0]
    tb = min(GLA_BLOCK, t)
    nch = tb // GLA_CHUNK
    hk, hv = dk // GLA_HEADS, dv // GLA_HEADS
    scale = hk ** -0.5
    v_cb, r_cb = (8 * dk) // dv, (8 * dk) // dv + 1
    q_cb, k_cb = (8 * dk + 2 * dv) // dk, (8 * dk + 2 * dv) // dk + 1
    c_kind, c_arrays, c_specs, c_shapes, c_scratch = _carry_parts(carry)
    nc = len(c_arrays)

    def body(q_ref, k_ref, v_ref, r_ref, al_ref, wa2_ref, ba2_ref, ghn_ref, *refs):
        x_refs, (oa_ref, opre_ref, s_ref), out_refs = refs[:nc], refs[nc:nc + 3], refs[nc + 3:2 * nc + 3]
        st_scr, sems = refs[2 * nc + 3], refs[2 * nc + 4:]
        after = _carry_hooks(c_kind, x_refs, out_refs, sems, pl.program_id(0), t // tb - 1)

        @pl.when(pl.program_id(0) == 0)
        def _():
            st_scr[...] = jnp.zeros_like(st_scr)

        same, causal, _ = _chunk_masks(tb)
        same_bf, causal_bf = _mask_bf16(same), _mask_bf16(causal)
        _, bc, bl = _gla_decay(al_ref[...], wa2_ref[...], ba2_ref[...], same_bf, causal_bf)
        q = q_ref[...].astype(F32) * scale
        k = k_ref[...].astype(F32)
        qd = (q * jnp.exp(bc)).astype(BF16)
        ki = (k * jnp.exp(-bc)).astype(BF16)
        ks = (k * jnp.exp(bl - bc)).astype(BF16)
        dl = jnp.exp(bl)
        ksls = [slice(h * hk, (h + 1) * hk) for h in range(GLA_HEADS)]
        vsls = [slice(h * hv, (h + 1) * hv) for h in range(GLA_HEADS)]
        v_hs = [v_ref[:, vsl] for vsl in vsls]
        o_intras = []
        for ksl, v_h in zip(ksls, v_hs):
            sc = jnp.where(causal, _dot_t(qd[:, ksl], ki[:, ksl], 1, 1), 0.0)
            o_intras.append(jnp.dot(sc.astype(BF16), v_h, preferred_element_type=F32))
        for c in range(nch):
            rows = slice(c * GLA_CHUNK, (c + 1) * GLA_CHUNK)
            for h, (ksl, vsl) in enumerate(zip(ksls, vsls)):
                st = st_scr[h]
                s_ref[c, h] = st
                opre_ref[rows, vsl] = o_intras[h][rows] + _dot_t(qd[rows, ksl], st.astype(BF16), 1, 1)
                st_scr[h] = dl[c * GLA_CHUNK:c * GLA_CHUNK + 1, ksl] * st + _dot_t(v_hs[h][rows], ks[rows, ksl], 0, 0)
        for h in range(GLA_HEADS):
            vsl = slice(h * hv, (h + 1) * hv)
            o = opre_ref[:, vsl]
            rs = lax.rsqrt(jnp.mean(o * o, axis=-1, keepdims=True) + NORM_EPS)
            rv = r_ref[:, vsl].astype(F32)
            oa_ref[:, vsl] = (rv * _sigmoid(rv) * (o * rs * ghn_ref[:, vsl])).astype(BF16)
        after()

    nchunks = t // GLA_CHUNK
    outs = pl.pallas_call(
        body, name=name,
        grid=(t // tb,),
        in_specs=[
            pl.BlockSpec((tb, dk), lambda i: (i, q_cb)),
            pl.BlockSpec((tb, dk), lambda i: (i, k_cb)),
            pl.BlockSpec((tb, dv), lambda i: (i, v_cb)),
            pl.BlockSpec((tb, dv), lambda i: (i, r_cb)),
            pl.BlockSpec((tb, LANE), lambda i: (i, 0)),
            pl.BlockSpec(wa2.shape, lambda i: (0, 0)),
            pl.BlockSpec(ba2.shape, lambda i: (0, 0)),
            pl.BlockSpec(ghn.shape, lambda i: (0, 0)),
        ] + c_specs,
        out_specs=[
            pl.BlockSpec((tb, dv), lambda i: (i, 0)),
            pl.BlockSpec((tb, dv), lambda i: (i, 0)),
            pl.BlockSpec((nch, GLA_HEADS, hv, hk), lambda i: (i, 0, 0, 0)),
        ] + c_specs,
        out_shape=[
            jax.ShapeDtypeStruct((t, dv), BF16),
            jax.ShapeDtypeStruct((t, dv), F32),
            jax.ShapeDtypeStruct((nchunks, GLA_HEADS, hv, hk), F32),
        ] + c_shapes,
        scratch_shapes=[pltpu.VMEM((GLA_HEADS, hv, hk), F32)] + c_scratch,
        compiler_params=_cparams("arbitrary"),
    )(proj, proj, proj, proj, alow, wa2, ba2, ghn, *c_arrays)
    return (outs[0], outs[1], outs[2], list(outs[3:])) if nc else tuple(outs)


def _gla_bwd(proj, alow, wa2, ba2, ghn, opre, states, doa, *, dk, dv, name):
    t = proj.shape[0]
    tb = min(GLA_BLOCK, t)
    nb = t // tb
    nch = tb // GLA_CHUNK
    hk, hv = dk // GLA_HEADS, dv // GLA_HEADS
    scale = hk ** -0.5
    v_cb, r_cb = (8 * dk) // dv, (8 * dk) // dv + 1
    q_cb, k_cb = (8 * dk + 2 * dv) // dk, (8 * dk + 2 * dv) // dk + 1

    def body(q_ref, k_ref, v_ref, r_ref, al_ref, wa2_ref, ba2_ref, ghn_ref, opre_ref, s_ref, doa_ref,
             dq_ref, dk_ref, dv_ref, dr_ref, dal_ref, dwa2_ref, dba2_ref, dghn_ref,
             dst_scr, dqd_scr, dki_scr, dks_scr, ddl_scr):
        @pl.when(pl.program_id(0) == 0)
        def _():
            dst_scr[...] = jnp.zeros_like(dst_scr)
            dwa2_ref[...] = jnp.zeros_like(dwa2_ref)
            dba2_ref[...] = jnp.zeros_like(dba2_ref)
            dghn_ref[...] = jnp.zeros_like(dghn_ref)

        same, causal, anti = _chunk_masks(tb)
        same_bf, causal_bf, anti_bf = _mask_bf16(same), _mask_bf16(causal), _mask_bf16(anti)
        al = al_ref[...]
        wa2v = wa2_ref[...]
        z, bc, bl = _gla_decay(al, wa2v, ba2_ref[...], same_bf, causal_bf)
        e_bc, e_nbc, e_st = jnp.exp(bc), jnp.exp(-bc), jnp.exp(bl - bc)
        q = q_ref[...].astype(F32) * scale
        k = k_ref[...].astype(F32)
        qd_f, ki_f, ks_f = q * e_bc, k * e_nbc, k * e_st
        qd, ki, ks = qd_f.astype(BF16), ki_f.astype(BF16), ks_f.astype(BF16)
        dl = jnp.exp(bl)
        per_head = []
        for h in range(GLA_HEADS):
            ksl = slice(h * hk, (h + 1) * hk)
            vsl = slice(h * hv, (h + 1) * hv)
            o = opre_ref[:, vsl]
            rs = lax.rsqrt(jnp.mean(o * o, axis=-1, keepdims=True) + NORM_EPS)
            ohat = o * rs
            g_h = ghn_ref[:, vsl]
            rv = r_ref[:, vsl].astype(F32)
            sg = _sigmoid(rv)
            d_oa = doa_ref[:, vsl].astype(F32)
            don = d_oa * (rv * sg)
            dr_ref[:, vsl] = (d_oa * (ohat * g_h) * (sg * (1.0 + rv * (1.0 - sg)))).astype(BF16)
            dghn_ref[:, vsl] += jnp.sum(don * ohat, axis=0, keepdims=True)
            dohat = don * g_h
            do_f = rs * (dohat - ohat * jnp.mean(dohat * ohat, axis=-1, keepdims=True))
            do = do_f.astype(BF16)
            v_h = v_ref[:, vsl]
            p = jnp.where(causal, _dot_t(do, v_h, 1, 1), 0.0).astype(BF16)
            dqd_intra = jnp.dot(p, ki[:, ksl], preferred_element_type=F32)
            dki_scr[:, ksl] = _dot_t(p, qd[:, ksl], 0, 0)
            sc = jnp.where(causal, _dot_t(qd[:, ksl], ki[:, ksl], 1, 1), 0.0).astype(BF16)
            dv_intra = _dot_t(sc, do, 0, 0)
            per_head.append((ksl, vsl, v_h, do, dqd_intra, dv_intra))
        for c in reversed(range(nch)):
            rows = slice(c * GLA_CHUNK, (c + 1) * GLA_CHUNK)
            for h, (ksl, vsl, v_h, do, dqd_intra, dv_intra) in enumerate(per_head):
                dst = dst_scr[h]
                st = s_ref[c, h]
                dst_bf = dst.astype(BF16)
                dv_ref[rows, vsl] = (dv_intra[rows] + _dot_t(ks[rows, ksl], dst_bf, 1, 1)).astype(BF16)
                dks_scr[rows, ksl] = jnp.dot(v_h[rows], dst_bf, preferred_element_type=F32)
                dl_c = dl[c * GLA_CHUNK:c * GLA_CHUNK + 1, ksl]
                ddl = jnp.sum(dst * st, axis=0, keepdims=True) * dl_c
                ddl_scr[rows, ksl] = jnp.broadcast_to(ddl, (GLA_CHUNK, hk))
                dqd_scr[rows, ksl] = dqd_intra[rows] + jnp.dot(do[rows], st.astype(BF16), preferred_element_type=F32)
                dst_scr[h] = dl_c * dst + _dot_t(do[rows], qd[rows, ksl], 0, 0)
        dqd, dki, dks = dqd_scr[...], dki_scr[...], dks_scr[...]
        dq_ref[...] = (dqd * (scale * e_bc)).astype(BF16)
        dk_ref[...] = (dki * e_nbc + dks * e_st).astype(BF16)
        dks_ks = dks * ks_f
        dbc = dqd * qd_f - dki * ki_f - dks_ks
        dla = _split_dot(anti_bf, dbc, 2) + _split_dot(same_bf, dks_ks, 2) + ddl_scr[...]
        dz = (dla * (1.0 / GLA_TAU) * (1.0 - _sigmoid(z)))
        dz_bf = dz.astype(BF16)
        dal_ref[...] = _dot_t(dz_bf, wa2v, 1, 1).astype(BF16)
        dwa2_ref[...] += _dot_t(al.astype(BF16), dz_bf, 0, 0)
        dba2_ref[...] += jnp.sum(dz, axis=0, keepdims=True)

    rev = lambda i: nb - 1 - i
    return pl.pallas_call(
        body, name=name,
        grid=(nb,),
        in_specs=[
            pl.BlockSpec((tb, dk), lambda i: (rev(i), q_cb)),
            pl.BlockSpec((tb, dk), lambda i: (rev(i), k_cb)),
            pl.BlockSpec((tb, dv), lambda i: (rev(i), v_cb)),
            pl.BlockSpec((tb, dv), lambda i: (rev(i), r_cb)),
            pl.BlockSpec((tb, LANE), lambda i: (rev(i), 0)),
            pl.BlockSpec(wa2.shape, lambda i: (0, 0)),
            pl.BlockSpec(ba2.shape, lambda i: (0, 0)),
            pl.BlockSpec(ghn.shape, lambda i: (0, 0)),
            pl.BlockSpec((tb, dv), lambda i: (rev(i), 0)),
            pl.BlockSpec((nch, GLA_HEADS, hv, hk), lambda i: (rev(i), 0, 0, 0)),
            pl.BlockSpec((tb, dv), lambda i: (rev(i), 0)),
        ],
        out_specs=[
            pl.BlockSpec((tb, dk), lambda i: (rev(i), 0)),
            pl.BlockSpec((tb, dk), lambda i: (rev(i), 0)),
            pl.BlockSpec((tb, dv), lambda i: (rev(i), 0)),
            pl.BlockSpec((tb, dv), lambda i: (rev(i), 0)),
            pl.BlockSpec((tb, LANE), lambda i: (rev(i), 0)),
            pl.BlockSpec(wa2.shape, lambda i: (0, 0)),
            pl.BlockSpec(ba2.shape, lambda i: (0, 0)),
            pl.BlockSpec(ghn.shape, lambda i: (0, 0)),
        ],
        out_shape=[
            jax.ShapeDtypeStruct((t, dk), BF16),
            jax.ShapeDtypeStruct((t, dk), BF16),
            jax.ShapeDtypeStruct((t, dv), BF16),
            jax.ShapeDtypeStruct((t, dv), BF16),
            jax.ShapeDtypeStruct((t, LANE), BF16),
            jax.ShapeDtypeStruct(wa2.shape, F32),
            jax.ShapeDtypeStruct(ba2.shape, F32),
            jax.ShapeDtypeStruct(ghn.shape, F32),
        ],
        scratch_shapes=[pltpu.VMEM((GLA_HEADS, hv, hk), F32)] + [pltpu.VMEM((tb, dk), F32)] * 4,
        compiler_params=_cparams("arbitrary"),
    )(proj, proj, proj, proj, alow, wa2, ba2, ghn, opre, states, doa)


def _s5_tables(lam_re, lam_im, log_dt, b_re, b_im, c_re, c_im):
    hp = lax.Precision.HIGHEST
    g, p = lam_re.shape
    ln = S5_L
    dt = jnp.exp(log_dt)[:, None]
    lr, li = lam_re, lam_im
    mag = jnp.exp(lr * dt)
    ar, ai = mag * jnp.cos(li * dt), mag * jnp.sin(li * dt)
    den = lr * lr + li * li
    am1 = ar - 1.0
    f_re = ((am1 * lr + ai * li) / den)[..., None]
    f_im = ((ai * lr - am1 * li) / den)[..., None]
    bb_re = f_re * b_re - f_im * b_im
    bb_im = f_re * b_im + f_im * b_re
    j = jnp.arange(ln + 1, dtype=F32)[None, :, None]
    pm = jnp.exp(j * (lr * dt)[:, None, :])
    ang = j * (li * dt)[:, None, :]
    pw_re, pw_im = pm * jnp.cos(ang), pm * jnp.sin(ang)
    cp_re = c_re[:, None] * pw_re[:, :, None, :] - c_im[:, None] * pw_im[:, :, None, :]
    cp_im = c_re[:, None] * pw_im[:, :, None, :] + c_im[:, None] * pw_re[:, :, None, :]
    kj = (jnp.einsum("gjcp,gpd->gjcd", cp_re[:, :ln], bb_re, precision=hp)
          - jnp.einsum("gjcp,gpd->gjcd", cp_im[:, :ln], bb_im, precision=hp))
    eye = jnp.eye(S5_TILE_G, dtype=F32)
    nt = g // S5_TILE_G
    k8 = jnp.einsum("jglcd,gh->jlgdhc", kj.reshape(nt, S5_TILE_G, ln, S5_GC, S5_GC), eye).reshape(nt, ln, LANE, LANE)
    rp_re, rp_im = pw_re[:, ln - 1::-1], pw_im[:, ln - 1::-1]
    bbt_re, bbt_im = bb_re.transpose(0, 2, 1)[:, None], bb_im.transpose(0, 2, 1)[:, None]
    bst = jnp.stack([rp_re[:, :, None, :] * bbt_re - rp_im[:, :, None, :] * bbt_im,
                     rp_re[:, :, None, :] * bbt_im + rp_im[:, :, None, :] * bbt_re], axis=3)
    bc = bst.reshape(nt, S5_TILE_G, ln, S5_GC, 2 * p).transpose(0, 2, 1, 3, 4).reshape(nt, ln, LANE, 2 * p)
    cst = jnp.stack([cp_re[:, 1:], -cp_im[:, 1:]], axis=2)
    cc = cst.reshape(nt, S5_TILE_G, ln, 2, S5_GC, p).transpose(0, 2, 3, 5, 1, 4).reshape(nt, ln, 2 * p, LANE)
    a8 = jnp.stack([jnp.concatenate([pw_re[:, ln], pw_re[:, ln]], axis=-1),
                    jnp.concatenate([-pw_im[:, ln], pw_im[:, ln]], axis=-1)], axis=1)
    a8 = a8.reshape(nt, S5_TILE_G, 2, 2 * p).transpose(0, 2, 1, 3).reshape(nt, 2, S5_TILE_G * 2 * p)
    return k8, bc, cc, a8


def _swap_re_im(x):
    w = x.shape[1]
    if w == LANE:
        return pltpu.roll(x, LANE // 2, 1)
    first_half = (lax.broadcasted_iota(jnp.int32, x.shape, 1) & (LANE // 2)) == 0
    return jnp.where(first_half, pltpu.roll(x, w - LANE // 2, 1), pltpu.roll(x, LANE // 2, 1))


def _s5_expand(bc, cc, w):
    reps = w // LANE
    mask_b = (lax.broadcasted_iota(jnp.int32, (LANE, w), 0) // S5_GC) == (lax.broadcasted_iota(jnp.int32, (LANE, w), 1) // LANE)
    mask_c = (lax.broadcasted_iota(jnp.int32, (w, LANE), 0) // LANE) == (lax.broadcasted_iota(jnp.int32, (w, LANE), 1) // S5_GC)
    b8 = None if bc is None else jnp.where(mask_b, jnp.concatenate([bc] * reps, axis=1), jnp.zeros((), bc.dtype))
    c8 = None if cc is None else jnp.where(mask_c, jnp.concatenate([cc] * reps, axis=0), jnp.zeros((), cc.dtype))
    return b8, c8, mask_b, mask_c


def _state_scan(v, pr, pi, reverse):
    n = v.shape[0]
    row = lax.broadcasted_iota(jnp.int32, v.shape, 0)
    z, s = v, 1
    while s < n:
        if reverse:
            zs = jnp.where(row < n - s, pltpu.roll(z, n - s, 0), 0.0)
        else:
            zs = jnp.where(row >= s, pltpu.roll(z, s, 0), 0.0)
        z = z + zs * pr + _swap_re_im(zs) * pi
        pr, pi = pr * pr - pi * pi, 2.0 * pr * pi
        s *= 2
    return z


def _s5_fwd(proj, u_cb, k8, bc, cc, a8, name):
    t = proj.shape[0]
    nt, ln = bc.shape[:2]
    w = a8.shape[2]
    nc = t // ln
    rb = min(t, S5_ROW_BLOCK)

    def body(u_ref, k_ref, b_ref, c_ref, a_ref, y_ref, x_ref, uf_ref):
        uf_ref[...] = u_ref[...].astype(F32)
        pos = lax.broadcasted_iota(jnp.int32, (rb, LANE), 0) & (ln - 1)
        for r0 in range(0, t, rb):
            u = uf_ref[r0:r0 + rb, :]
            acc = jnp.dot(u.astype(BF16), k_ref[0], preferred_element_type=F32)
            for lag in range(1, ln):
                us = jnp.where(pos >= lag, pltpu.roll(u, lag, 0), 0.0).astype(BF16)
                acc = acc + jnp.dot(us, k_ref[lag], preferred_element_type=F32)
            y_ref[r0:r0 + rb, :] = acc
        v = None
        for s in range(ln):
            part = jnp.dot(uf_ref[pl.ds(s, nc, stride=ln), :].astype(BF16), _s5_expand(b_ref[s], None, w)[0],
                           preferred_element_type=F32)
            v = part if v is None else v + part
        z = _state_scan(v, a_ref[0:1, :], a_ref[1:2, :], reverse=False)
        row = lax.broadcasted_iota(jnp.int32, z.shape, 0)
        x = jnp.where(row >= 1, pltpu.roll(z, 1, 0), 0.0)
        x_ref[...] = x
        x_bf = x.astype(BF16)
        for tt in range(ln):
            y_ref[pl.ds(tt, nc, stride=ln), :] += jnp.dot(x_bf, _s5_expand(None, c_ref[tt], w)[1], preferred_element_type=F32)

    tile = lambda shape: pl.BlockSpec((None,) + shape, lambda j: (j,) + (0,) * len(shape))
    return pl.pallas_call(
        body, name=name, grid=(nt,),
        in_specs=[pl.BlockSpec((t, LANE), lambda j: (0, u_cb + j)), tile((ln, LANE, LANE)), tile(bc.shape[1:]),
                  tile(cc.shape[1:]), tile((2, w))],
        out_specs=[pl.BlockSpec((t, LANE), lambda j: (0, j)), tile((nc, w))],
        out_shape=[jax.ShapeDtypeStruct((t, nt * LANE), F32), jax.ShapeDtypeStruct((nt, nc, w), F32)],
        scratch_shapes=[pltpu.VMEM((t, LANE), F32)],
        compiler_params=_cparams("parallel"),
    )(proj, k8, bc, cc, a8)


def _s5_bwd_data(dy, x_st, k8, bc, cc, a8, name):
    t = dy.shape[0]
    nt, ln = bc.shape[:2]
    w = a8.shape[2]
    nc = t // ln
    rb = min(t, S5_ROW_BLOCK)

    def body(dy_ref, x_ref, k_ref, b_ref, c_ref, a_ref, du_ref, dv_ref, da_ref, dyf_ref, duf_ref):
        dyf_ref[...] = dy_ref[...].astype(F32)
        pos = lax.broadcasted_iota(jnp.int32, (rb, LANE), 0) & (ln - 1)
        for r0 in range(0, t, rb):
            g = dyf_ref[r0:r0 + rb, :]
            acc = _dot_t(g.astype(BF16), k_ref[0], 1, 1)
            for lag in range(1, ln):
                gs = jnp.where(pos < ln - lag, pltpu.roll(g, rb - lag, 0), 0.0).astype(BF16)
                acc = acc + _dot_t(gs, k_ref[lag], 1, 1)
            duf_ref[r0:r0 + rb, :] = acc
        gx = None
        for tt in range(ln):
            part = _dot_t(dyf_ref[pl.ds(tt, nc, stride=ln), :].astype(BF16), _s5_expand(None, c_ref[tt], w)[1], 1, 1)
            gx = part if gx is None else gx + part
        rtot = _state_scan(gx, a_ref[0:1, :], -a_ref[1:2, :], reverse=True)
        row = lax.broadcasted_iota(jnp.int32, rtot.shape, 0)
        dv = jnp.where(row < nc - 1, pltpu.roll(rtot, nc - 1, 0), 0.0)
        dv_ref[...] = dv
        dv_bf = dv.astype(BF16)
        for s in range(ln):
            duf_ref[pl.ds(s, nc, stride=ln), :] += _dot_t(dv_bf, _s5_expand(b_ref[s], None, w)[0], 1, 1)
        du_ref[...] = duf_ref[...].astype(BF16)
        x = x_ref[...]
        da_ref[0:1, :] = jnp.sum(dv * x, axis=0, keepdims=True)
        da_ref[1:2, :] = jnp.sum(dv * _swap_re_im(x), axis=0, keepdims=True)

    tile = lambda shape: pl.BlockSpec((None,) + shape, lambda j: (j,) + (0,) * len(shape))
    return pl.pallas_call(
        body, name=name, grid=(nt,),
        in_specs=[pl.BlockSpec((t, LANE), lambda j: (0, j)), tile((nc, w)), tile((ln, LANE, LANE)), tile(bc.shape[1:]),
                  tile(cc.shape[1:]), tile((2, w))],
        out_specs=[pl.BlockSpec((t, LANE), lambda j: (0, j)), tile((nc, w)), tile((2, w))],
        out_shape=[jax.ShapeDtypeStruct((t, nt * LANE), BF16), jax.ShapeDtypeStruct((nt, nc, w), F32),
                   jax.ShapeDtypeStruct((nt, 2, w), F32)],
        scratch_shapes=[pltpu.VMEM((t, LANE), F32), pltpu.VMEM((t, LANE), F32)],
        compiler_params=_cparams("parallel"),
    )(dy, x_st, k8, bc, cc, a8)


def _s5_bwd_tables(dy, proj, u_cb, x_st, dv, ln, name):
    t = dy.shape[0]
    nt, nc, w = x_st.shape
    rb = min(t, S5_ROW_BLOCK)

    def body(dy_ref, u_ref, x_ref, dv_ref, dk_ref, db_ref, dc_ref, dyf_ref, uf_ref):
        s = pl.program_id(1)

        @pl.when(s == 0)
        def _():
            dyf_ref[...] = dy_ref[...].astype(F32)
            uf_ref[...] = u_ref[...].astype(F32)
            pos = lax.broadcasted_iota(jnp.int32, (rb, LANE), 0) & (ln - 1)
            for r0 in range(0, t, rb):
                u, g_bf = uf_ref[r0:r0 + rb, :], dy_ref[r0:r0 + rb, :]
                for lag in range(ln):
                    us = u if lag == 0 else jnp.where(pos >= lag, pltpu.roll(u, lag, 0), 0.0)
                    part = _dot_t(us.astype(BF16), g_bf, 0, 0)
                    if r0 == 0:
                        dk_ref[lag] = part
                    else:
                        dk_ref[lag] += part

        rows = pl.ds(s, nc, stride=ln)
        _, _, mask_b, mask_c = _s5_expand(None, None, w)
        db = jnp.where(mask_b, _dot_t(uf_ref[rows, :].astype(BF16), dv_ref[...].astype(BF16), 0, 0), 0.0)
        dc = jnp.where(mask_c, _dot_t(x_ref[...].astype(BF16), dyf_ref[rows, :].astype(BF16), 0, 0), 0.0)
        db_ref[...] = sum(db[:, h * LANE:(h + 1) * LANE] for h in range(w // LANE))
        dc_ref[...] = sum(dc[h * LANE:(h + 1) * LANE, :] for h in range(w // LANE))

    tile = lambda shape: pl.BlockSpec((None,) + shape, lambda j, s: (j,) + (0,) * len(shape))
    per_s = lambda shape: pl.BlockSpec((None, None) + shape, lambda j, s: (j, s, 0, 0))
    return pl.pallas_call(
        body, name=name, grid=(nt, ln),
        in_specs=[pl.BlockSpec((t, LANE), lambda j, s: (0, j)), pl.BlockSpec((t, LANE), lambda j, s: (0, u_cb + j)),
                  tile((nc, w)), tile((nc, w))],
        out_specs=[tile((ln, LANE, LANE)), per_s((LANE, LANE)), per_s((LANE, LANE))],
        out_shape=[jax.ShapeDtypeStruct((nt, ln, LANE, LANE), F32)] * 3,
        scratch_shapes=[pltpu.VMEM((t, LANE), F32), pltpu.VMEM((t, LANE), F32)],
        compiler_params=_cparams("parallel", "arbitrary"),
    )(dy, proj, x_st, dv)


def _gelu_parts(y):
    inner = GELU_C * (y + GELU_A * y * y * y)
    th = jnp.tanh(inner)
    return th, 0.5 * y * (1.0 + th)


def _s5_post_fwd(y_raw, proj, u_cb, s5d, wglu, bglu, name):
    w = y_raw.shape[1]

    def fn(yr, u, dsk, wg, bg):
        y = yr + dsk * u.astype(F32)
        _, h = _gelu_parts(y)
        gl = jnp.dot(h.astype(BF16), wg, preferred_element_type=F32) + bg
        return (h * _sigmoid(gl),)

    return _rowwise(fn, [("row", y_raw), ("win", proj, w, u_cb), ("full", s5d), ("full", wglu), ("full", bglu)],
                    [(w, BF16)], [], rows=y_raw.shape[0], tb=_tile(y_raw.shape[0], 512), name=name)[0]


def _s5_post_bwd(y_raw, proj, u_cb, s5d, wglu, bglu, dob, name):
    w = y_raw.shape[1]

    def fn(yr, u, dsk, wg, bg, dov):
        u = u.astype(F32)
        dov = dov.astype(F32)
        y = yr + dsk * u
        th, h = _gelu_parts(y)
        h_bf = h.astype(BF16)
        gl = jnp.dot(h_bf, wg, preferred_element_type=F32) + bg
        sg = _sigmoid(gl)
        dgl = dov * h * sg * (1.0 - sg)
        dgl_bf = dgl.astype(BF16)
        dh = dov * sg + _dot_t(dgl_bf, wg, 1, 1)
        dgelu = 0.5 * (1.0 + th) + 0.5 * y * (1.0 - th * th) * GELU_C * (1.0 + 3.0 * GELU_A * y * y)
        dy = dh * dgelu
        return (dy, dy * dsk,
                _dot_t(h_bf, dgl_bf, 0, 0), jnp.sum(dgl, axis=0, keepdims=True), jnp.sum(dy * u, axis=0, keepdims=True))

    return _rowwise(fn, [("row", y_raw), ("win", proj, w, u_cb), ("full", s5d), ("full", wglu), ("full", bglu), ("row", dob)],
                    [(w, BF16), (w, BF16)], [(w, w), (1, w), (1, w)], rows=y_raw.shape[0], tb=_tile(y_raw.shape[0], 512), name=name)


def _adamw(w, g, m, v, name):
    _, rows, cols = w.shape
    tr, tc = (_tile(rows, 256, align=16), cols) if rows % 16 == 0 else (rows, _tile(cols, 256))
    slots = isinstance(g, (list, tuple))
    gs = list(g) if slots else [g]
    c1 = 1.0 - ADAM_B1 ** ADAM_STEP
    c2 = 1.0 - ADAM_B2 ** ADAM_STEP

    def body(w_ref, m_ref, v_ref, *refs):
        g_refs, out_refs = refs[:len(gs)], refs[len(gs):]
        if slots:
            parts = [g_ref[s].astype(F32) for g_ref in g_refs for s in range(g_ref.shape[0])]
            gv = parts[0]
            for p in parts[1:]:
                gv = gv + p
            out_refs[0][...] = gv
        else:
            gv = g_refs[0][...]
        d_ref, nm_ref, nv_ref = out_refs[-3:]
        nm = ADAM_B1 * m_ref[...] + (1.0 - ADAM_B1) * gv
        nv = ADAM_B2 * v_ref[...] + (1.0 - ADAM_B2) * (gv * gv)
        d_ref[...] = -ADAM_LR * ((nm / c1) / (jnp.sqrt(nv / c2) + ADAM_EPS) + ADAM_WD * w_ref[...])
        nm_ref[...] = nm
        nv_ref[...] = nv

    spec = pl.BlockSpec((None, tr, tc), lambda i, j: (0, i, j))
    g_specs = [pl.BlockSpec((a.shape[0], tr, tc), lambda i, j: (0, i, j)) for a in gs] if slots else [pl.BlockSpec((tr, tc), lambda i, j: (i, j))]
    n_out = 4 if slots else 3
    return pl.pallas_call(
        body, name=name, grid=(rows // tr, cols // tc),
        in_specs=[spec, spec, spec] + g_specs, out_specs=[spec] * n_out,
        out_shape=[jax.ShapeDtypeStruct((1, rows, cols), F32)] * n_out,
        compiler_params=_cparams("parallel", "parallel"),
    )(w, m, v, *gs)


def _slot_sum(x, name):
    _, rows, cols = x.shape
    if rows % 8 == 0:
        tr, tc = _tile(rows, 512, align=8), cols
    else:
        tr, tc = rows, _tile(cols, 256)

    def body(x_ref, o_ref):
        acc = x_ref[0].astype(F32)
        for s in range(1, N_DEV):
            acc = acc + x_ref[s].astype(F32)
        o_ref[...] = acc

    return pl.pallas_call(
        body, name=name, grid=(rows // tr, cols // tc),
        in_specs=[pl.BlockSpec((N_DEV, tr, tc), lambda i, j: (0, i, j))],
        out_specs=pl.BlockSpec((tr, tc), lambda i, j: (i, j)),
        out_shape=jax.ShapeDtypeStruct((rows, cols), F32),
        compiler_params=_cparams("parallel", "parallel"),
    )(x)


_REST = (("w_a2", 1), ("w_glu", 0), ("w_branch_a", 1), ("w_branch_b", 1), ("w_out", 0), ("w_ffn_in", 1), ("w_ffn_out", 0))
_SMALL = ("norm1_g", "b_a2", "gla_norm_g", "lam_re", "lam_im", "log_dt", "s5_b_re", "s5_b_im", "s5_c_re", "s5_c_im",
          "s5_d", "b_glu", "norm2_g", "final_norm_g")
_ORDER = ("norm1_g", "w_in", "w_a2", "b_a2", "gla_norm_g", "lam_re", "lam_im", "log_dt", "s5_b_re", "s5_b_im", "s5_c_re",
          "s5_c_im", "s5_d", "w_glu", "b_glu", "w_branch_a", "w_branch_b", "w_out", "norm2_g", "w_ffn_in", "w_ffn_out", "final_norm_g")


def _join_slots(slots, axis):
    _, r, c = slots.shape
    if axis == 0:
        return slots.reshape(N_DEV * r, c)
    return slots.transpose(1, 0, 2).reshape(r, N_DEV * c)


def _to_slots(full, axis):
    r, c = full.shape
    if axis == 0:
        return full.reshape(N_DEV, r // N_DEV, c)
    return full.reshape(r, N_DEV, c // N_DEV).transpose(1, 0, 2)


def _local_step(x, target, w_in_t, small, rest):
    t, d = x.shape
    dk, dv, s5w = d // 4, d // 2, d // 4
    dist = not isinstance(rest, dict)
    if dist:
        h1, (w_in_slots,) = _rms_fwd(x, small["norm1_g"], "norm1_fwd", carry=("ag", [w_in_t]))
        w_in_t = w_in_slots.reshape(-1, d)
    else:
        h1 = _rms_fwd(x, small["norm1_g"], "norm1_fwd")
    o_q, o_k, o_v, o_r, o_al = 0, dk, 2 * dk, 2 * dk + dv, 2 * dk + 2 * dv
    o_u = o_al + GLA_RANK
    o_ga, o_gb = o_u + s5w, o_u + s5w + d
    rows = lambda a, o, n: a[o:o + n]
    w_al_t = jnp.pad(rows(w_in_t, o_al, GLA_RANK), ((0, LANE - GLA_RANK), (0, 0)))
    w_ext_t = jnp.concatenate([rows(w_in_t, o_ga, d), rows(w_in_t, o_gb, d), rows(w_in_t, o_v, dv), rows(w_in_t, o_r, dv),
                               rows(w_in_t, o_q, dk), rows(w_in_t, o_k, dk), rows(w_in_t, o_u, s5w), w_al_t], axis=0)
    n_main = 2 * d + 2 * dv + 2 * dk + s5w
    u_cb = (2 * d + 2 * dv + 2 * dk) // s5w

    if dist:
        proj, (a2_s, glu_s, w_ffn_in_s) = _mm(h1, w_ext_t, tb=True, n_limit=n_main, out_dtype=BF16,
                                              carry=("ag", [rest[0], rest[1], rest[5]]), name="in_proj")
        w = {"w_a2": _join_slots(a2_s, 1), "w_glu": _join_slots(glu_s, 0)}
    else:
        proj = _mm(h1, w_ext_t, tb=True, n_limit=n_main, out_dtype=BF16, name="in_proj")
        w = rest
        w_ffn_in_s = _to_slots(rest["w_ffn_in"], 1)
    wa2 = jnp.pad(w["w_a2"], ((0, LANE - GLA_RANK), (0, 0)))
    alow = _mm(h1, w_al_t, tb=True, out_dtype=BF16, name="in_proj_gate_rank")
    if dist:
        o_a, o_pre, states, got = _gla_fwd(proj, alow, wa2, small["b_a2"], small["gla_norm_g"], dk=dk, dv=dv, name="gla_fwd",
                                           carry=("ag", rest[2:5]))
        w.update({n: _join_slots(g, ax) for (n, ax), g in zip(_REST[2:5], got)})
    else:
        o_a, o_pre, states = _gla_fwd(proj, alow, wa2, small["b_a2"], small["gla_norm_g"], dk=dk, dv=dv, name="gla_fwd")

    s5_params = (small["lam_re"], small["lam_im"], small["log_dt"][0], small["s5_b_re"], small["s5_b_im"],
                 small["s5_c_re"], small["s5_c_im"])
    (k8, bc, cc, a8), tables_vjp = jax.vjp(_s5_tables, *s5_params)
    k8_bf, b8_bf, c8_bf = k8.astype(BF16), bc.astype(BF16), cc.astype(BF16)
    u_lane_cb = u_cb * s5w // LANE
    y_raw, x_st = _s5_fwd(proj, u_lane_cb, k8_bf, b8_bf, c8_bf, a8, "s5_scan_fwd")
    o_b = _s5_post_fwd(y_raw, proj, u_cb, small["s5_d"], w["w_glu"], small["b_glu"], "s5_post_fwd")

    pa = _mm(o_a, w["w_branch_a"], out_dtype=BF16, name="branch_a")
    pb = _mm(o_b, w["w_branch_b"], out_dtype=BF16, name="branch_b")
    mix = _mix_fwd(proj, pa, pb, d, "mix_fwd")
    x1 = _mm(mix, w["w_out"], res=x, name="out_proj")
    h2 = _rms_fwd(x1, small["norm2_g"], "norm2_fwd")
    if dist:
        gu, act, (w_ffn_out_s,) = _ffn_in_fused(h2, w_ffn_in_s, carry=("ag", rest[-1:]), name="ffn_in")
        w_ffn_out = _join_slots(w_ffn_out_s, 0)
    else:
        gu, act = _ffn_in_fused(h2, w_ffn_in_s, name="ffn_in")
        w_ffn_out = rest["w_ffn_out"]
    x2 = _mm(act, w_ffn_out, res=x1, name="ffn_out")
    dx2, dx2_bf, d_final_g, loss = _loss_head(x2, small["final_norm_g"], target, "loss_head")

    recv = {}
    dgu, = _mm(dx2_bf, w_ffn_out, tb=True, epi=(_swiglu_bwd_tile, [gu], [(2, BF16)]), name="d_act")
    g_ffn_out = _mm(act, dx2_bf, ta=True, out_dtype=BF16, name="g_w_ffn_out")
    if dist:
        g_ffn_in_s, recv["w_ffn_out"] = _mm(h2, dgu, ta=True, b_slots=True, out_dtype=BF16, out_slots=N_DEV,
                                            carry=("a2a", [_to_slots(g_ffn_out, 0)]), name="g_w_ffn_in")
        dh2, (recv_ffn_in,) = _mm(dgu, w_ffn_in_s, tb=True, a_slots=True, b_slots=True, b_group=4, tm_cap=512,
                                  out_dtype=BF16, carry=("a2a", [g_ffn_in_s], [_ALL_K[:-1]]), name="d_h2")
    else:
        g_ffn_in_s = _mm(h2, dgu, ta=True, b_slots=True, out_dtype=BF16, out_slots=N_DEV, name="g_w_ffn_in")
        dh2 = _mm(dgu, w_ffn_in_s, tb=True, a_slots=True, b_slots=True, b_group=4, tm_cap=512, out_dtype=BF16, name="d_h2")
    dx1, dx1_bf, d_norm2_g = _rms_bwd(x1, small["norm2_g"], dh2, dx2, "norm2_bwd", True)
    dmix = _mm(dx1_bf, w["w_out"], tb=True, out_dtype=BF16, name="d_mix")
    g_out = _mm(mix, dx1_bf, ta=True, out_dtype=BF16, name="g_w_out")
    dpa, dpb, dga, dgb = _mix_bwd(proj, pa, pb, dmix, d, "mix_bwd")
    doa = _mm(dpa, w["w_branch_a"], tb=True, out_dtype=BF16, name="d_o_a")
    dob = _mm(dpb, w["w_branch_b"], tb=True, out_dtype=BF16, name="d_o_b")
    g_branch_a = _mm(o_a, dpa, ta=True, out_dtype=BF16, name="g_w_branch_a")
    g_branch_b = _mm(o_b, dpb, ta=True, out_dtype=BF16, name="g_w_branch_b")

    dy_s5, du_direct, g_glu, g_bglu, g_s5d = _s5_post_bwd(y_raw, proj, u_cb, small["s5_d"], w["w_glu"], small["b_glu"], dob, "s5_post_bwd")
    du_scan, dv_st, d_a8 = _s5_bwd_data(dy_s5, x_st, k8_bf, b8_bf, c8_bf, a8, "s5_scan_bwd")
    d_k8, d_b8, d_c8 = _s5_bwd_tables(dy_s5, proj, u_lane_cb, x_st, dv_st, S5_L, "s5_scan_bwd_tables")
    g_lam_re, g_lam_im, g_log_dt, g_b_re, g_b_im, g_c_re, g_c_im = tables_vjp((d_k8, d_b8, d_c8, d_a8))
    du = du_scan + du_direct

    dq, dkk, dvv, dr, dal, g_wa2, g_ba2, g_ghn = _gla_bwd(proj, alow, wa2, small["b_a2"], small["gla_norm_g"], o_pre, states, doa,
                                                        dk=dk, dv=dv, name="gla_bwd")
    dproj = jnp.concatenate([dga, dgb, dvv, dr, dq, dkk, du, dal], axis=1)
    mid = {"w_out": g_out, "w_branch_a": g_branch_a, "w_branch_b": g_branch_b, "w_glu": g_glu.astype(BF16),
           "w_a2": g_wa2[:GLA_RANK].astype(BF16)}
    if dist:
        axes = dict(_REST)
        g_main_t, got = _mm(dproj, h1, ta=True, m_limit=n_main, out_dtype=BF16, name="g_w_in_main",
                            carry=("a2a", [_to_slots(mid[n], axes[n]) for n in mid] + [g_ffn_in_s],
                                   [_ALL_K] * len(mid) + [_ALL_K[-1:]]))
        recv.update(zip(mid, [[g] for g in got[:-1]]))
        recv["w_ffn_in"] = [recv_ffn_in, got[-1]]
    else:
        g_main_t = _mm(dproj, h1, ta=True, m_limit=n_main, out_dtype=BF16, name="g_w_in_main")
    g_al_t = _mm(dal, h1, ta=True, out_dtype=BF16, name="g_w_in_gate_rank")
    mrows = lambda o, n: g_main_t[o:o + n]
    g_w_in_t = jnp.concatenate([mrows(2 * d + 2 * dv, dk), mrows(2 * d + 2 * dv + dk, dk), mrows(2 * d, dv), mrows(2 * d + dv, dv),
                                g_al_t[:GLA_RANK], mrows(2 * d + 2 * dv + 2 * dk, s5w), mrows(0, d), mrows(d, d)], axis=0)
    if dist:
        dh1, recv["w_in"] = _mm(dproj, w_ext_t, out_dtype=BF16, carry=("a2a", [_to_slots(g_w_in_t, 0)]), name="d_h1")
    else:
        dh1 = _mm(dproj, w_ext_t, out_dtype=BF16, name="d_h1")
    grad_x, d_norm1_g = _rms_bwd(x, small["norm1_g"], dh1, dx1, "norm1_bwd", False)

    small_g = {
        "norm1_g": d_norm1_g, "b_a2": g_ba2, "gla_norm_g": g_ghn, "lam_re": g_lam_re, "lam_im": g_lam_im,
        "log_dt": g_log_dt[None], "s5_b_re": g_b_re, "s5_b_im": g_b_im, "s5_c_re": g_c_re, "s5_c_im": g_c_im,
        "s5_d": g_s5d, "b_glu": g_bglu, "norm2_g": d_norm2_g, "final_norm_g": d_final_g,
    }
    if not dist:
        recv = dict(mid, w_in=g_w_in_t, w_ffn_in=_join_slots(g_ffn_in_s, 1), w_ffn_out=g_ffn_out)
    return loss[0, 0], grad_x, recv, small_g


def _small_2d(name, a):
    a = a[0]
    return a[None] if a.ndim == 1 else a


def kernel(x, norm1_g, w_in, w_a2, b_a2, gla_norm_g, lam_re, lam_im, log_dt, s5_b_re, s5_b_im, s5_c_re, s5_c_im, s5_d, w_glu, b_glu, w_branch_a, w_branch_b, w_out, norm2_g, w_ffn_in, w_ffn_out, final_norm_g, loss_target, m_norm1_g, m_w_in, m_w_a2, m_b_a2, m_gla_norm_g, m_lam_re, m_lam_im, m_log_dt, m_s5_b_re, m_s5_b_im, m_s5_c_re, m_s5_c_im, m_s5_d, m_w_glu, m_b_glu, m_w_branch_a, m_w_branch_b, m_w_out, m_norm2_g, m_w_ffn_in, m_w_ffn_out, m_final_norm_g, v_norm1_g, v_w_in, v_w_a2, v_b_a2, v_gla_norm_g, v_lam_re, v_lam_im, v_log_dt, v_s5_b_re, v_s5_b_im, v_s5_c_re, v_s5_c_im, v_s5_d, v_w_glu, v_b_glu, v_w_branch_a, v_w_branch_b, v_w_out, v_norm2_g, v_w_ffn_in, v_w_ffn_out, v_final_norm_g):
    args = dict(locals())
    weights = {n: args[n] for n in _ORDER}
    m_in = {n: args["m_" + n] for n in _ORDER}
    v_in = {n: args["v_" + n] for n in _ORDER}
    transposed = lambda a: a[0].T[None]
    rest = [weights[n][0].astype(BF16) for n, _ in _REST]
    small = {n: _small_2d(n, weights[n]) for n in _SMALL}
    loss_local, grad_x, recv, small_g = _local_step(x[0], loss_target[0], transposed(weights["w_in"])[0].astype(BF16), small, rest)

    grads, delta, new_m, new_v = {}, {}, {}, {}
    for n, _ in _REST:
        grads[n], delta[n], new_m[n], new_v[n] = _adamw(weights[n], recv[n], m_in[n], v_in[n], "adamw_" + n)
    w_in_out = _adamw(transposed(weights["w_in"]), recv["w_in"], transposed(m_in["w_in"]), transposed(v_in["w_in"]), "adamw_w_in")
    grads["w_in"], delta["w_in"], new_m["w_in"], new_v["w_in"] = (transposed(a) for a in w_in_out)

    s_sizes = [small_g[n].size for n in _SMALL]
    s_offs = [sum(s_sizes[:i]) for i in range(len(s_sizes))]
    s_total = sum(s_sizes)
    s_rows = -(-(-(-(s_total + 1) // LANE)) // LANE) * LANE

    def pack_small(parts):
        flat = jnp.concatenate([p.reshape(-1) for p in parts])
        return jnp.pad(flat, (0, s_rows * LANE - flat.size)).reshape(s_rows, LANE)

    s_flat = pack_small([small_g[n] for n in _SMALL] + [loss_local])
    s_red = _slot_sum(_exchange("ag", [s_flat], "small_grads_all_gather")[0], "small_grads_slot_sum")
    loss = s_red.reshape(-1)[s_total]
    sd, sm, sv = _adamw(pack_small([weights[n] for n in _SMALL])[None], s_red, pack_small([m_in[n] for n in _SMALL])[None],
                        pack_small([v_in[n] for n in _SMALL])[None], "adamw_small")
    sd, sm, sv = sd[0], sm[0], sv[0]
    for n, o, s in zip(_SMALL, s_offs, s_sizes):
        shape = weights[n].shape[1:]
        grads[n], delta[n], new_m[n], new_v[n] = (a.reshape(-1)[o:o + s].reshape(shape) for a in (s_red, sd, sm, sv))

    out = [loss, grad_x[None]]
    for tree in (grads, delta, new_m, new_v):
        out += [tree[n].reshape(weights[n].shape) for n in _ORDER]
    return tuple(out)
```

```python
import functools
import math

import jax
import jax.numpy as jnp
from jax import lax
from jax.experimental import pallas as pl
from jax.experimental.pallas import tpu as pltpu

F32 = jnp.float32
BF16 = jnp.bfloat16

NORM_EPS = 1e-6
N_DEV = 8
N_PEER = N_DEV - 1
GLA_HEADS = 4
GLA_CHUNK = 32
GLA_CHUNK_SHIFT = 5
GLA_TAU = 16.0
GLA_RANK = 16
GLA_BLOCK = 256
S5_GC = 16
S5_P = 64
S5_L = 16
S5_TILE_G = 8
S5_ROW_BLOCK = 2048
LANE = 128
V7X_VMEM_LIMIT = 56 * 1024 * 1024
V7X_MM_VMEM_BUDGET = 40 * 1024 * 1024
V7X_MM_TILE_MN = 1408
V7X_MM_TILE_MN_WHOLE_K = 512
V7X_MM_TILE_K = 2048

ADAM_LR = 0.001
ADAM_B1 = 0.9
ADAM_B2 = 0.999
ADAM_EPS = 1e-08
ADAM_WD = 0.01
ADAM_STEP = 10

GELU_C = math.sqrt(2.0 / math.pi)
GELU_A = 0.044715

MESH = pl.DeviceIdType.MESH


def _cparams(*sem):
    return pltpu.CompilerParams(dimension_semantics=sem, vmem_limit_bytes=V7X_VMEM_LIMIT)


def _divisors_down(n, start, align=LANE):
    t = (min(start, n) // align) * align
    found = False
    while t >= align:
        if n % t == 0:
            found = True
            yield t
        t -= align
    if not found:
        yield n


def _tile(n, target, align=LANE):
    return next(_divisors_down(n, target, align))


def _sigmoid(x):
    return 1.0 / (1.0 + jnp.exp(-x))


_HBM_SPEC = pl.BlockSpec(memory_space=pltpu.HBM)


def _exchange_scratch(n):
    return [pltpu.SemaphoreType.DMA((n * N_PEER,)), pltpu.SemaphoreType.DMA((n * N_PEER,)), pltpu.SemaphoreType.DMA((n,))]


def _ag_phases(x_refs, out_refs, send_sems, recv_sems, local_sems):
    n = len(x_refs)
    x, y, c = lax.axis_index("x"), lax.axis_index("y"), lax.axis_index("c")
    me, sibling = (x, y, c), (x, y, 1 - c)
    chips = [(1 - x, y), (x, 1 - y), (1 - x, 1 - y)]

    def copy(a, k, block, to, from_input=False):
        dst = out_refs[a].at[4 * block[0] + 2 * block[1] + block[2]]
        return pltpu.make_async_remote_copy(
            src_ref=x_refs[a] if from_input else dst, dst_ref=dst,
            send_sem=send_sems.at[a * N_PEER + k], recv_sem=recv_sems.at[a * N_PEER + k], device_id=to, device_id_type=MESH)

    def local(a):
        return pltpu.make_async_copy(x_refs[a], out_refs[a].at[4 * x + 2 * y + c], local_sems.at[a])

    def first(a):
        return [copy(a, 0, me, sibling, True)] + [copy(a, 1 + j, me, (*chip, c), True) for j, chip in enumerate(chips)]

    def start():
        for a in range(n):
            local(a).start()
            for cp in first(a):
                cp.start()

    def relay():
        for j, chip in enumerate(chips):
            for a in range(n):
                copy(a, 1 + j, (*chip, c), me).wait_recv()
                copy(a, 4 + j, (*chip, c), sibling).start()

    def finish():
        for a in range(n):
            copy(a, 0, sibling, me).wait_recv()
            for j, chip in enumerate(chips):
                copy(a, 4 + j, (*chip, 1 - c), me).wait_recv()
        for a in range(n):
            for cp in first(a) + [copy(a, 4 + j, (*chip, c), sibling) for j, chip in enumerate(chips)]:
                cp.wait_send()
            local(a).wait()

    return start, relay, finish


_ALL_K = tuple(range(N_DEV))


def _a2a_phases(x_refs, out_refs, send_sems, recv_sems, local_sems, ks_list=None):
    n = len(x_refs)
    ks_list = ks_list or [_ALL_K] * n
    x, y, c = lax.axis_index("x"), lax.axis_index("y"), lax.axis_index("c")
    my = 4 * x + 2 * y + c

    def copy(a, k):
        px, py, pc = (1 - x if k & 4 else x), (1 - y if k & 2 else y), (1 - c if k & 1 else c)
        return pltpu.make_async_remote_copy(
            src_ref=x_refs[a].at[4 * px + 2 * py + pc], dst_ref=out_refs[a].at[ks_list[a].index(k)],
            send_sem=send_sems.at[a * N_PEER + k - 1], recv_sem=recv_sems.at[a * N_PEER + k - 1],
            device_id=(px, py, pc), device_id_type=MESH)

    def local(a):
        return pltpu.make_async_copy(x_refs[a].at[my], out_refs[a].at[ks_list[a].index(0)], local_sems.at[a])

    def start():
        for a in range(n):
            for k in ks_list[a]:
                (copy(a, k) if k else local(a)).start()

    def relay():
        pass

    def finish():
        for a in range(n):
            for k in ks_list[a]:
                if k:
                    copy(a, k).wait_recv()
        for a in range(n):
            for k in ks_list[a]:
                if k:
                    copy(a, k).wait_send()
                else:
                    local(a).wait()

    return start, relay, finish


def _exchange_out_shapes(kind, arrays, ks_list=None):
    if kind == "ag":
        return [jax.ShapeDtypeStruct((N_DEV,) + a.shape, a.dtype) for a in arrays]
    ks_list = ks_list or [_ALL_K] * len(arrays)
    return [jax.ShapeDtypeStruct((len(ks),) + a.shape[1:], a.dtype) for a, ks in zip(arrays, ks_list)]


def _exchange(kind, arrays, name):
    n = len(arrays)
    phases = _ag_phases if kind == "ag" else _a2a_phases

    def body(*refs):
        start, relay, finish = phases(refs[:n], refs[n:2 * n], *refs[2 * n:])
        start()
        relay()
        finish()

    return pl.pallas_call(
        body, name=name,
        out_shape=_exchange_out_shapes(kind, arrays),
        in_specs=[_HBM_SPEC] * n, out_specs=[_HBM_SPEC] * n,
        scratch_shapes=_exchange_scratch(n),
    )(*arrays)


def _mm_tiles(m, n_unit, k_unit, tile_bytes, small_tiles_ok=True, tm_cap=0):
    fits = lambda tm, tn, tk: 2 * 2 * (tm * tk + tk * tn) + tile_bytes * tm * tn <= V7X_MM_VMEM_BUDGET
    for cap in (V7X_MM_TILE_MN, V7X_MM_TILE_MN_WHOLE_K) if small_tiles_ok else (V7X_MM_TILE_MN,):
        tm, tn = _tile(m, min(cap, tm_cap or cap)), _tile(n_unit, cap)
        if fits(tm, tn, k_unit) and (tn >= V7X_MM_TILE_MN_WHOLE_K or tn == n_unit):
            return tm, tn, k_unit
    tm, tn = _tile(m, tm_cap or V7X_MM_TILE_MN), _tile(n_unit, V7X_MM_TILE_MN)
    for tk in _divisors_down(k_unit, k_unit if tm_cap else V7X_MM_TILE_K):
        if fits(tm, tn, tk):
            return tm, tn, tk
    return tm, tn, _tile(k_unit, LANE)


def _carry_parts(carry):
    kind, arrays, ks_list = (tuple(carry) + (None,))[:3] if carry is not None else (None, [], None)
    n = len(arrays)
    kind = (kind, ks_list)
    return kind, arrays, [_HBM_SPEC] * n, _exchange_out_shapes(kind[0], arrays, ks_list), (_exchange_scratch(n) if n else [])


def _carry_hooks(kind, x_refs, out_refs, sems, step, last_step):
    if not x_refs:
        return lambda: None
    kind, ks_list = kind
    if kind == "ag":
        start, relay, finish = _ag_phases(x_refs, out_refs, *sems)
    else:
        start, relay, finish = _a2a_phases(x_refs, out_refs, *sems, ks_list=ks_list)
    pl.when(step == 0)(start)

    def after():
        if kind == "ag":
            pl.when(step == (last_step * 7) // 8)(relay)
        pl.when(step == last_step)(finish)

    return after


def _mm(a, b, *, ta=False, tb=False, out_dtype=F32, res=None, carry=None, a_slots=False, b_slots=False, b_group=0,
        out_slots=0, epi=None, m_limit=0, n_limit=0, tm_cap=0, name):
    if a_slots:
        assert not ta
        a_n, m, a_c = a.shape
        k = a_n * a_c
    else:
        m, k = (a.shape[1], a.shape[0]) if ta else a.shape
    if b_slots:
        b_n, b_r, b_c = b.shape
        k2, n = (b_n * b_c, b_r) if tb else (b_r, b_n * b_c)
    else:
        k2, n = (b.shape[1], b.shape[0]) if tb else b.shape
    assert k == k2, (a.shape, b.shape, ta, tb)
    m, n = m_limit or m, n_limit or n
    has_res = res is not None
    assert not (has_res and (out_slots or epi))
    n_units = [n] + ([n // out_slots] if out_slots else []) + ([b_c] if b_slots and not tb else [])
    k_units = [k] + ([a_c] if a_slots else []) + ([b_c] if b_slots and tb else [])
    n_unit, k_unit = min(n_units), min(k_units)
    assert all(u % n_unit == 0 for u in n_units) and all(u % k_unit == 0 for u in k_units)
    epi_fn, epi_ins, epi_outs = epi if epi is not None else (None, [], [])
    tile_bytes = 4 + (2 * res.dtype.itemsize if has_res else 0)
    tile_bytes += sum(2 * e.shape[0] * e.dtype.itemsize for e in epi_ins)
    tile_bytes += sum(2 * l * jnp.dtype(dt).itemsize for l, dt in epi_outs) if epi else 2 * jnp.dtype(out_dtype).itemsize
    tm, tn, tk = _mm_tiles(m, n_unit, k_unit, tile_bytes, small_tiles_ok=not epi, tm_cap=tm_cap)
    if b_group:
        tk = b_group * b_c
        assert b_slots and tb and k % tk == 0 and (not a_slots or a_c % tk == 0)
    ni, nj, nk = m // tm, n // tn, k // tk
    dims = (((0,) if ta else (1,), (1,) if tb else (0,)), ((), ()))

    def slot_map(per, pos):
        if pos == "k_cols":
            return lambda i, j, kk: (kk // per, i, kk % per)
        if pos == "k_cols_j":
            return lambda i, j, kk: (kk // per, j, kk % per)
        if pos == "n_cols_k":
            return lambda i, j, kk: (j // per, kk, j % per)
        return lambda i, j, kk: (j // per, i, j % per)

    if a_slots:
        a_spec = pl.BlockSpec((None, tm, tk), slot_map(a_c // tk, "k_cols"))
    else:
        a_spec = pl.BlockSpec((tk, tm), lambda i, j, kk: (kk, i)) if ta else pl.BlockSpec((tm, tk), lambda i, j, kk: (i, kk))
    if b_group:
        b_spec = pl.BlockSpec((b_group, tn, b_c), lambda i, j, kk: (kk, j, 0))
    elif b_slots and tb:
        b_spec = pl.BlockSpec((None, tn, tk), slot_map(b_c // tk, "k_cols_j"))
    elif b_slots:
        b_spec = pl.BlockSpec((None, tk, tn), slot_map(b_c // tn, "n_cols_k"))
    else:
        b_spec = pl.BlockSpec((tn, tk), lambda i, j, kk: (j, kk)) if tb else pl.BlockSpec((tk, tn), lambda i, j, kk: (kk, j))
    if epi:
        lead_spec = lambda l: pl.BlockSpec((l, tm, tn), lambda i, j, kk: (0, i, j))
        o_specs = [lead_spec(l) for l, _ in epi_outs]
        o_shapes = [jax.ShapeDtypeStruct((l, m, n), dt) for l, dt in epi_outs]
    elif out_slots:
        o_specs = [pl.BlockSpec((None, tm, tn), slot_map((n // out_slots) // tn, "n_cols_i"))]
        o_shapes = [jax.ShapeDtypeStruct((out_slots, m, n // out_slots), out_dtype)]
    else:
        o_specs = [pl.BlockSpec((tm, tn), lambda i, j, kk: (i, j))]
        o_shapes = [jax.ShapeDtypeStruct((m, n), out_dtype)]
    extra_ins = ([res] if has_res else []) + list(epi_ins)
    extra_specs = ([o_specs[0]] if has_res else []) + [pl.BlockSpec((e.shape[0], tm, tn), lambda i, j, kk: (0, i, j)) for e in epi_ins]
    n_in, n_out = 2 + len(extra_ins), len(o_specs)
    c_kind, c_arrays, c_specs, c_shapes, c_scratch = _carry_parts(carry)
    nc = len(c_arrays)
    last_step = ni * nj * nk - 1

    def body(*refs):
        a_ref, b_ref = refs[0], refs[1]
        e_refs = refs[2:n_in]
        x_refs = refs[n_in:n_in + nc]
        o_refs = refs[n_in + nc:n_in + nc + n_out]
        out_refs = refs[n_in + nc + n_out:n_in + 2 * nc + n_out]
        scratch = refs[n_in + 2 * nc + n_out:]
        acc = scratch[0] if nk > 1 else None
        kk = pl.program_id(2)
        step = (pl.program_id(0) * nj + pl.program_id(1)) * nk + kk
        after = _carry_hooks(c_kind, x_refs, out_refs, scratch[-3:], step, last_step)

        def emit(val):
            if has_res:
                val = val + e_refs[0][...].astype(F32)
            if epi:
                for o_ref, parts in zip(o_refs, epi_fn(val, *[e[...] for e in e_refs])):
                    for l, v in enumerate(parts):
                        o_ref[l] = v.astype(o_ref.dtype)
            else:
                o_refs[0][...] = val.astype(out_dtype)

        if b_group:
            part = sum(lax.dot_general(a_ref[:, s * b_c:(s + 1) * b_c], b_ref[s], dims, preferred_element_type=F32)
                       for s in range(b_group))
        else:
            part = lax.dot_general(a_ref[...], b_ref[...], dims, preferred_element_type=F32)
        if nk == 1:
            emit(part)
        else:
            @pl.when(kk == 0)
            def _():
                acc[...] = part

            @pl.when(kk > 0)
            def _():
                acc[...] += part

            @pl.when(kk == nk - 1)
            def _():
                emit(acc[...])

        after()

    sem = ("arbitrary",) * 3 if nc else ("parallel", "parallel", "arbitrary")
    outs = pl.pallas_call(
        body, name=name,
        grid=(ni, nj, nk),
        in_specs=[a_spec, b_spec] + extra_specs + c_specs,
        out_specs=o_specs + c_specs,
        out_shape=o_shapes + c_shapes,
        scratch_shapes=([pltpu.VMEM((tm, tn), F32)] if nk > 1 else []) + c_scratch,
        compiler_params=_cparams(*sem),
    )(a, b, *extra_ins, *c_arrays)
    main = list(outs[:n_out]) if epi else outs[0]
    return (main, list(outs[n_out:])) if nc else main


def _ffn_in_fused(h2, w_s, *, carry=None, name):
    t, d = h2.shape
    n_slot, _, c = w_s.shape
    half = n_slot // 2
    tm = _tile(t, 512)
    c_kind, c_arrays, c_specs, c_shapes, c_scratch = _carry_parts(carry)
    nc = len(c_arrays)
    last_step = (t // tm) * half - 1

    def body(h_ref, wg_ref, wu_ref, *refs):
        x_refs, (gu_ref, act_ref), out_refs, sems = refs[:nc], refs[nc:nc + 2], refs[nc + 2:2 * nc + 2], refs[2 * nc + 2:]
        step = pl.program_id(0) * half + pl.program_id(1)
        after = _carry_hooks(c_kind, x_refs, out_refs, sems, step, last_step)
        h = h_ref[...]
        g = jnp.dot(h, wg_ref[...], preferred_element_type=F32)
        u = jnp.dot(h, wu_ref[...], preferred_element_type=F32)
        sg = _sigmoid(g)
        silu = g * sg
        gu_ref[0] = (u * (sg + silu - silu * sg)).astype(BF16)
        gu_ref[1] = silu.astype(BF16)
        act_ref[...] = (silu * u).astype(BF16)
        after()

    outs = pl.pallas_call(
        body, name=name,
        grid=(t // tm, half),
        in_specs=[pl.BlockSpec((tm, d), lambda i, j: (i, 0)),
                  pl.BlockSpec((None, d, c), lambda i, j: (j, 0, 0)),
                  pl.BlockSpec((None, d, c), lambda i, j: (half + j, 0, 0))] + c_specs,
        out_specs=[pl.BlockSpec((2, tm, c), lambda i, j: (0, i, j)), pl.BlockSpec((tm, c), lambda i, j: (i, j))] + c_specs,
        out_shape=[jax.ShapeDtypeStruct((2, t, half * c), BF16), jax.ShapeDtypeStruct((t, half * c), BF16)] + c_shapes,
        scratch_shapes=c_scratch,
        compiler_params=_cparams(*(("arbitrary",) * 2 if nc else ("parallel", "parallel"))),
    )(h2, w_s, w_s, *c_arrays)
    return (outs[0], outs[1], list(outs[2:])) if nc else (outs[0], outs[1])


def _rowwise(fn, ins, row_outs, acc_outs, *, rows, tb, name, carry=None):
    in_specs, args = [], []
    for spec in ins:
        kind, arr = spec[0], spec[1]
        if kind == "row":
            in_specs.append(pl.BlockSpec((tb, arr.shape[1]), lambda i: (i, 0)))
        elif kind == "win":
            width, cb = spec[2], spec[3]
            in_specs.append(pl.BlockSpec((tb, width), functools.partial(lambda i, cb: (i, cb), cb=cb)))
        else:
            in_specs.append(pl.BlockSpec(arr.shape, lambda i: (0, 0)))
        args.append(arr)
    out_specs = [pl.BlockSpec((tb, c), lambda i: (i, 0)) for c, _ in row_outs]
    out_specs += [pl.BlockSpec(shape, lambda i: (0, 0)) for shape in acc_outs]
    out_shape = [jax.ShapeDtypeStruct((rows, c), dt) for c, dt in row_outs]
    out_shape += [jax.ShapeDtypeStruct(shape, F32) for shape in acc_outs]
    n_in, n_row, n_out = len(ins), len(row_outs), len(row_outs) + len(acc_outs)
    c_kind, c_arrays, c_specs, c_shapes, c_scratch = _carry_parts(carry)
    nc = len(c_arrays)

    def body(*refs):
        after = _carry_hooks(c_kind, refs[n_in:n_in + nc], refs[n_in + nc + n_out:n_in + 2 * nc + n_out],
                             refs[n_in + 2 * nc + n_out:], pl.program_id(0), rows // tb - 1)
        vals = [r[...] for r in refs[:n_in]]
        outs = fn(*vals)
        if not isinstance(outs, (tuple, list)):
            outs = (outs,)
        out_refs = refs[n_in + nc:n_in + nc + n_out]
        for o_ref, val in zip(out_refs[:n_row], outs[:n_row]):
            o_ref[...] = val.astype(o_ref.dtype)
        first = pl.program_id(0) == 0
        for o_ref, val in zip(out_refs[n_row:], outs[n_row:]):
            @pl.when(first)
            def _(o_ref=o_ref):
                o_ref[...] = jnp.zeros_like(o_ref)
            o_ref[...] += val
        after()

    res = pl.pallas_call(
        body, name=name,
        grid=(rows // tb,),
        in_specs=in_specs + c_specs, out_specs=out_specs + c_specs, out_shape=out_shape + c_shapes,
        scratch_shapes=c_scratch,
        compiler_params=_cparams("arbitrary"),
    )(*args, *c_arrays)
    return (list(res[:n_out]), list(res[n_out:])) if nc else res


def _rms_fwd(x, g, name, carry=None):
    def fn(xv, gv):
        r = lax.rsqrt(jnp.mean(xv * xv, axis=-1, keepdims=True) + NORM_EPS)
        return (xv * r * gv,)
    res = _rowwise(fn, [("row", x), ("full", g)], [(x.shape[1], BF16)], [], rows=x.shape[0], tb=_tile(x.shape[0], 512),
                   name=name, carry=carry)
    return (res[0][0], res[1]) if carry is not None else res[0]


def _rms_bwd(x, g, dh, dres, name, want_bf16):
    d = x.shape[1]

    def fn(xv, gv, dhv, drv):
        r = lax.rsqrt(jnp.mean(xv * xv, axis=-1, keepdims=True) + NORM_EPS)
        xhat = xv * r
        dhv = dhv.astype(F32)
        dxhat = dhv * gv
        dx = drv + r * (dxhat - xhat * jnp.mean(dxhat * xhat, axis=-1, keepdims=True))
        dg = jnp.sum(dhv * xhat, axis=0, keepdims=True)
        return (dx, dx, dg) if want_bf16 else (dx, dg)

    row_outs = [(d, F32), (d, BF16)] if want_bf16 else [(d, F32)]
    return _rowwise(fn, [("row", x), ("full", g), ("row", dh), ("row", dres)], row_outs, [(1, d)],
                    rows=x.shape[0], tb=_tile(x.shape[0], 256), name=name)


def _loss_head(x2, g, target, name):
    d = x2.shape[1]

    def fn(xv, gv, tv):
        r = lax.rsqrt(jnp.mean(xv * xv, axis=-1, keepdims=True) + NORM_EPS)
        xhat = xv * r
        diff = xhat * gv - tv
        loss = 0.5 * jnp.sum(jnp.mean(diff * diff, axis=-1, keepdims=True), axis=0, keepdims=True)
        dy = diff * (1.0 / d)
        dxhat = dy * gv
        dx = r * (dxhat - xhat * jnp.mean(dxhat * xhat, axis=-1, keepdims=True))
        dg = jnp.sum(dy * xhat, axis=0, keepdims=True)
        return dx, dx, dg, jnp.broadcast_to(loss, (1, LANE))

    return _rowwise(fn, [("row", x2), ("full", g), ("row", target)], [(d, F32), (d, BF16)], [(1, d), (1, LANE)],
                    rows=x2.shape[0], tb=_tile(x2.shape[0], 256), name=name)


def _swiglu_bwd_tile(dact, dswiglu):
    return ((dact * dswiglu[0].astype(F32), dact * dswiglu[1].astype(F32)),)


def _mix_fwd(proj, pa, pb, d, name):
    def fn(ga, gb, av, bv):
        return (_sigmoid(ga.astype(F32)) * av.astype(F32) + _sigmoid(gb.astype(F32)) * bv.astype(F32),)
    return _rowwise(fn, [("win", proj, d, 0), ("win", proj, d, 1), ("row", pa), ("row", pb)], [(d, BF16)], [],
                    rows=pa.shape[0], tb=_tile(pa.shape[0], 512), name=name)[0]


def _mix_bwd(proj, pa, pb, dmix, d, name):
    def fn(ga, gb, av, bv, dm):
        dm = dm.astype(F32)
        sa, sb = _sigmoid(ga.astype(F32)), _sigmoid(gb.astype(F32))
        av, bv = av.astype(F32), bv.astype(F32)
        return dm * sa, dm * sb, dm * av * sa * (1.0 - sa), dm * bv * sb * (1.0 - sb)
    return _rowwise(fn, [("win", proj, d, 0), ("win", proj, d, 1), ("row", pa), ("row", pb), ("row", dmix)],
                    [(d, BF16)] * 4, [], rows=pa.shape[0], tb=_tile(pa.shape[0], 512), name=name)


def _chunk_masks(tb):
    r = lax.broadcasted_iota(jnp.int32, (tb, tb), 0)
    c = lax.broadcasted_iota(jnp.int32, (tb, tb), 1)
    same = lax.shift_right_logical(r, GLA_CHUNK_SHIFT) == lax.shift_right_logical(c, GLA_CHUNK_SHIFT)
    return same, same & (c <= r), same & (r <= c)


def _mask_bf16(mask):
    return jnp.where(mask, 1.0, 0.0).astype(BF16)


def _split_dot(mask_bf, x, terms):
    acc, rem = None, x
    for _ in range(terms):
        hi = rem.astype(BF16)
        part = jnp.dot(mask_bf, hi, preferred_element_type=F32)
        acc = part if acc is None else acc + part
        rem = rem - hi.astype(F32)
    return acc


def _gla_decay(al, wa2, ba2, same_bf, causal_bf):
    z = jnp.dot(al.astype(BF16), wa2, preferred_element_type=F32) + ba2
    la = (jnp.minimum(z, 0.0) - jnp.log(1.0 + jnp.exp(-jnp.abs(z)))) * (1.0 / GLA_TAU)
    bc = _split_dot(causal_bf, la, 3)
    bl = _split_dot(same_bf, la, 3)
    return z, bc, bl


def _dot_t(a, b, ca, cb):
    return lax.dot_general(a, b, (((ca,), (cb,)), ((), ())), preferred_element_type=F32)


def _gla_fwd(proj, alow, wa2, ba2, ghn, *, dk, dv, name, carry=None):
    t = proj.shape[0]
    tb = min(GLA_BLOCK, t)
    nch = tb // GLA_CHUNK
    hk, hv = dk // GLA_HEADS, dv // GLA_HEADS
    scale = hk ** -0.5
    v_cb, r_cb = (8 * dk) // dv, (8 * dk) // dv + 1
    q_cb, k_cb = (8 * dk + 2 * dv) // dk, (8 * dk + 2 * dv) // dk + 1
    c_kind, c_arrays, c_specs, c_shapes, c_scratch = _carry_parts(carry)
    nc = len(c_arrays)

    def body(q_ref, k_ref, v_ref, r_ref, al_ref, wa2_ref, ba2_ref, ghn_ref, *refs):
        x_refs, (oa_ref, opre_ref, s_ref), out_refs = refs[:nc], refs[nc:nc + 3], refs[nc + 3:2 * nc + 3]
        st_scr, sems = refs[2 * nc + 3], refs[2 * nc + 4:]
        after = _carry_hooks(c_kind, x_refs, out_refs, sems, pl.program_id(0), t // tb - 1)

        @pl.when(pl.program_id(0) == 0)
        def _():
            st_scr[...] = jnp.zeros_like(st_scr)

        same, causal, _ = _chunk_masks(tb)
        same_bf, causal_bf = _mask_bf16(same), _mask_bf16(causal)
        _, bc, bl = _gla_decay(al_ref[...], wa2_ref[...], ba2_ref[...], same_bf, causal_bf)
        q = q_ref[...].astype(F32) * scale
        k = k_ref[...].astype(F32)
        qd = (q * jnp.exp(bc)).astype(BF16)
        ki = (k * jnp.exp(-bc)).astype(BF16)
        ks = (k * jnp.exp(bl - bc)).astype(BF16)
        dl = jnp.exp(bl)
        ksls = [slice(h * hk, (h + 1) * hk) for h in range(GLA_HEADS)]
        vsls = [slice(h * hv, (h + 1) * hv) for h in range(GLA_HEADS)]
        v_hs = [v_ref[:, vsl] for vsl in vsls]
        o_intras = []
        for ksl, v_h in zip(ksls, v_hs):
            sc = jnp.where(causal, _dot_t(qd[:, ksl], ki[:, ksl], 1, 1), 0.0)
            o_intras.append(jnp.dot(sc.astype(BF16), v_h, preferred_element_type=F32))
        for c in range(nch):
            rows = slice(c * GLA_CHUNK, (c + 1) * GLA_CHUNK)
            for h, (ksl, vsl) in enumerate(zip(ksls, vsls)):
                st = st_scr[h]
                s_ref[c, h] = st
                opre_ref[rows, vsl] = o_intras[h][rows] + _dot_t(qd[rows, ksl], st.astype(BF16), 1, 1)
                st_scr[h] = dl[c * GLA_CHUNK:c * GLA_CHUNK + 1, ksl] * st + _dot_t(v_hs[h][rows], ks[rows, ksl], 0, 0)
        for h in range(GLA_HEADS):
            vsl = slice(h * hv, (h + 1) * hv)
            o = opre_ref[:, vsl]
            rs = lax.rsqrt(jnp.mean(o * o, axis=-1, keepdims=True) + NORM_EPS)
            rv = r_ref[:, vsl].astype(F32)
            oa_ref[:, vsl] = (rv * _sigmoid(rv) * (o * rs * ghn_ref[:, vsl])).astype(BF16)
        after()

    nchunks = t // GLA_CHUNK
    outs = pl.pallas_call(
        body, name=name,
        grid=(t // tb,),
        in_specs=[
            pl.BlockSpec((tb, dk), lambda i: (i, q_cb)),
            pl.BlockSpec((tb, dk), lambda i: (i, k_cb)),
            pl.BlockSpec((tb, dv), lambda i: (i, v_cb)),
            pl.BlockSpec((tb, dv), lambda i: (i, r_cb)),
            pl.BlockSpec((tb, LANE), lambda i: (i, 0)),
            pl.BlockSpec(wa2.shape, lambda i: (0, 0)),
            pl.BlockSpec(ba2.shape, lambda i: (0, 0)),
            pl.BlockSpec(ghn.shape, lambda i: (0, 0)),
        ] + c_specs,
        out_specs=[
            pl.BlockSpec((tb, dv), lambda i: (i, 0)),
            pl.BlockSpec((tb, dv), lambda i: (i, 0)),
            pl.BlockSpec((nch, GLA_HEADS, hv, hk), lambda i: (i, 0, 0, 0)),
        ] + c_specs,
        out_shape=[
            jax.ShapeDtypeStruct((t, dv), BF16),
            jax.ShapeDtypeStruct((t, dv), F32),
            jax.ShapeDtypeStruct((nchunks, GLA_HEADS, hv, hk), F32),
        ] + c_shapes,
        scratch_shapes=[pltpu.VMEM((GLA_HEADS, hv, hk), F32)] + c_scratch,
        compiler_params=_cparams("arbitrary"),
    )(proj, proj, proj, proj, alow, wa2, ba2, ghn, *c_arrays)
    return (outs[0], outs[1], outs[2], list(outs[3:])) if nc else tuple(outs)


def _gla_bwd(proj, alow, wa2, ba2, ghn, opre, states, doa, *, dk, dv, name):
    t = proj.shape[0]
    tb = min(GLA_BLOCK, t)
    nb = t // tb
    nch = tb // GLA_CHUNK
    hk, hv = dk // GLA_HEADS, dv // GLA_HEADS
    scale = hk ** -0.5
    v_cb, r_cb = (8 * dk) // dv, (8 * dk) // dv + 1
    q_cb, k_cb = (8 * dk + 2 * dv) // dk, (8 * dk + 2 * dv) // dk + 1

    def body(q_ref, k_ref, v_ref, r_ref, al_ref, wa2_ref, ba2_ref, ghn_ref, opre_ref, s_ref, doa_ref,
             dq_ref, dk_ref, dv_ref, dr_ref, dal_ref, dwa2_ref, dba2_ref, dghn_ref,
             dst_scr, dqd_scr, dki_scr, dks_scr, ddl_scr):
        @pl.when(pl.program_id(0) == 0)
        def _():
            dst_scr[...] = jnp.zeros_like(dst_scr)
            dwa2_ref[...] = jnp.zeros_like(dwa2_ref)
            dba2_ref[...] = jnp.zeros_like(dba2_ref)
            dghn_ref[...] = jnp.zeros_like(dghn_ref)

        same, causal, anti = _chunk_masks(tb)
        same_bf, causal_bf, anti_bf = _mask_bf16(same), _mask_bf16(causal), _mask_bf16(anti)
        al = al_ref[...]
        wa2v = wa2_ref[...]
        z, bc, bl = _gla_decay(al, wa2v, ba2_ref[...], same_bf, causal_bf)
        e_bc, e_nbc, e_st = jnp.exp(bc), jnp.exp(-bc), jnp.exp(bl - bc)
        q = q_ref[...].astype(F32) * scale
        k = k_ref[...].astype(F32)
        qd_f, ki_f, ks_f = q * e_bc, k * e_nbc, k * e_st
        qd, ki, ks = qd_f.astype(BF16), ki_f.astype(BF16), ks_f.astype(BF16)
        dl = jnp.exp(bl)
        per_head = []
        for h in range(GLA_HEADS):
            ksl = slice(h * hk, (h + 1) * hk)
            vsl = slice(h * hv, (h + 1) * hv)
            o = opre_ref[:, vsl]
            rs = lax.rsqrt(jnp.mean(o * o, axis=-1, keepdims=True) + NORM_EPS)
            ohat = o * rs
            g_h = ghn_ref[:, vsl]
            rv = r_ref[:, vsl].astype(F32)
            sg = _sigmoid(rv)
            d_oa = doa_ref[:, vsl].astype(F32)
            don = d_oa * (rv * sg)
            dr_ref[:, vsl] = (d_oa * (ohat * g_h) * (sg * (1.0 + rv * (1.0 - sg)))).astype(BF16)
            dghn_ref[:, vsl] += jnp.sum(don * ohat, axis=0, keepdims=True)
            dohat = don * g_h
            do_f = rs * (dohat - ohat * jnp.mean(dohat * ohat, axis=-1, keepdims=True))
            do = do_f.astype(BF16)
            v_h = v_ref[:, vsl]
            p = jnp.where(causal, _dot_t(do, v_h, 1, 1), 0.0).astype(BF16)
            dqd_intra = jnp.dot(p, ki[:, ksl], preferred_element_type=F32)
            dki_scr[:, ksl] = _dot_t(p, qd[:, ksl], 0, 0)
            sc = jnp.where(causal, _dot_t(qd[:, ksl], ki[:, ksl], 1, 1), 0.0).astype(BF16)
            dv_intra = _dot_t(sc, do, 0, 0)
            per_head.append((ksl, vsl, v_h, do, dqd_intra, dv_intra))
        for c in reversed(range(nch)):
            rows = slice(c * GLA_CHUNK, (c + 1) * GLA_CHUNK)
            for h, (ksl, vsl, v_h, do, dqd_intra, dv_intra) in enumerate(per_head):
                dst = dst_scr[h]
                st = s_ref[c, h]
                dst_bf = dst.astype(BF16)
                dv_ref[rows, vsl] = (dv_intra[rows] + _dot_t(ks[rows, ksl], dst_bf, 1, 1)).astype(BF16)
                dks_scr[rows, ksl] = jnp.dot(v_h[rows], dst_bf, preferred_element_type=F32)
                dl_c = dl[c * GLA_CHUNK:c * GLA_CHUNK + 1, ksl]
                ddl = jnp.sum(dst * st, axis=0, keepdims=True) * dl_c
                ddl_scr[rows, ksl] = jnp.broadcast_to(ddl, (GLA_CHUNK, hk))
                dqd_scr[rows, ksl] = dqd_intra[rows] + jnp.dot(do[rows], st.astype(BF16), preferred_element_type=F32)
                dst_scr[h] = dl_c * dst + _dot_t(do[rows], qd[rows, ksl], 0, 0)
        dqd, dki, dks = dqd_scr[...], dki_scr[...], dks_scr[...]
        dq_ref[...] = (dqd * (scale * e_bc)).astype(BF16)
        dk_ref[...] = (dki * e_nbc + dks * e_st).astype(BF16)
        dks_ks = dks * ks_f
        dbc = dqd * qd_f - dki * ki_f - dks_ks
        dla = _split_dot(anti_bf, dbc, 2) + _split_dot(same_bf, dks_ks, 2) + ddl_scr[...]
        dz = (dla * (1.0 / GLA_TAU) * (1.0 - _sigmoid(z)))
        dz_bf = dz.astype(BF16)
        dal_ref[...] = _dot_t(dz_bf, wa2v, 1, 1).astype(BF16)
        dwa2_ref[...] += _dot_t(al.astype(BF16), dz_bf, 0, 0)
        dba2_ref[...] += jnp.sum(dz, axis=0, keepdims=True)

    rev = lambda i: nb - 1 - i
    return pl.pallas_call(
        body, name=name,
        grid=(nb,),
        in_specs=[
            pl.BlockSpec((tb, dk), lambda i: (rev(i), q_cb)),
            pl.BlockSpec((tb, dk), lambda i: (rev(i), k_cb)),
            pl.BlockSpec((tb, dv), lambda i: (rev(i), v_cb)),
            pl.BlockSpec((tb, dv), lambda i: (rev(i), r_cb)),
            pl.BlockSpec((tb, LANE), lambda i: (rev(i), 0)),
            pl.BlockSpec(wa2.shape, lambda i: (0, 0)),
            pl.BlockSpec(ba2.shape, lambda i: (0, 0)),
            pl.BlockSpec(ghn.shape, lambda i: (0, 0)),
            pl.BlockSpec((tb, dv), lambda i: (rev(i), 0)),
            pl.BlockSpec((nch, GLA_HEADS, hv, hk), lambda i: (rev(i), 0, 0, 0)),
            pl.BlockSpec((tb, dv), lambda i: (rev(i), 0)),
        ],
        out_specs=[
            pl.BlockSpec((tb, dk), lambda i: (rev(i), 0)),
            pl.BlockSpec((tb, dk), lambda i: (rev(i), 0)),
            pl.BlockSpec((tb, dv), lambda i: (rev(i), 0)),
            pl.BlockSpec((tb, dv), lambda i: (rev(i), 0)),
            pl.BlockSpec((tb, LANE), lambda i: (rev(i), 0)),
            pl.BlockSpec(wa2.shape, lambda i: (0, 0)),
            pl.BlockSpec(ba2.shape, lambda i: (0, 0)),
            pl.BlockSpec(ghn.shape, lambda i: (0, 0)),
        ],
        out_shape=[
            jax.ShapeDtypeStruct((t, dk), BF16),
            jax.ShapeDtypeStruct((t, dk), BF16),
            jax.ShapeDtypeStruct((t, dv), BF16),
            jax.ShapeDtypeStruct((t, dv), BF16),
            jax.ShapeDtypeStruct((t, LANE), BF16),
            jax.ShapeDtypeStruct(wa2.shape, F32),
            jax.ShapeDtypeStruct(ba2.shape, F32),
            jax.ShapeDtypeStruct(ghn.shape, F32),
        ],
        scratch_shapes=[pltpu.VMEM((GLA_HEADS, hv, hk), F32)] + [pltpu.VMEM((tb, dk), F32)] * 4,
        compiler_params=_cparams("arbitrary"),
    )(proj, proj, proj, proj, alow, wa2, ba2, ghn, opre, states, doa)


def _s5_tables(lam_re, lam_im, log_dt, b_re, b_im, c_re, c_im):
    hp = lax.Precision.HIGHEST
    g, p = lam_re.shape
    ln = S5_L
    dt = jnp.exp(log_dt)[:, None]
    lr, li = lam_re, lam_im
    mag = jnp.exp(lr * dt)
    ar, ai = mag * jnp.cos(li * dt), mag * jnp.sin(li * dt)
    den = lr * lr + li * li
    am1 = ar - 1.0
    f_re = ((am1 * lr + ai * li) / den)[..., None]
    f_im = ((ai * lr - am1 * li) / den)[..., None]
    bb_re = f_re * b_re - f_im * b_im
    bb_im = f_re * b_im + f_im * b_re
    j = jnp.arange(ln + 1, dtype=F32)[None, :, None]
    pm = jnp.exp(j * (lr * dt)[:, None, :])
    ang = j * (li * dt)[:, None, :]
    pw_re, pw_im = pm * jnp.cos(ang), pm * jnp.sin(ang)
    cp_re = c_re[:, None] * pw_re[:, :, None, :] - c_im[:, None] * pw_im[:, :, None, :]
    cp_im = c_re[:, None] * pw_im[:, :, None, :] + c_im[:, None] * pw_re[:, :, None, :]
    kj = (jnp.einsum("gjcp,gpd->gjcd", cp_re[:, :ln], bb_re, precision=hp)
          - jnp.einsum("gjcp,gpd->gjcd", cp_im[:, :ln], bb_im, precision=hp))
    eye = jnp.eye(S5_TILE_G, dtype=F32)
    nt = g // S5_TILE_G
    k8 = jnp.einsum("jglcd,gh->jlgdhc", kj.reshape(nt, S5_TILE_G, ln, S5_GC, S5_GC), eye).reshape(nt, ln, LANE, LANE)
    rp_re, rp_im = pw_re[:, ln - 1::-1], pw_im[:, ln - 1::-1]
    bbt_re, bbt_im = bb_re.transpose(0, 2, 1)[:, None], bb_im.transpose(0, 2, 1)[:, None]
    bst = jnp.stack([rp_re[:, :, None, :] * bbt_re - rp_im[:, :, None, :] * bbt_im,
                     rp_re[:, :, None, :] * bbt_im + rp_im[:, :, None, :] * bbt_re], axis=3)
    bc = bst.reshape(nt, S5_TILE_G, ln, S5_GC, 2 * p).transpose(0, 2, 1, 3, 4).reshape(nt, ln, LANE, 2 * p)
    cst = jnp.stack([cp_re[:, 1:], -cp_im[:, 1:]], axis=2)
    cc = cst.reshape(nt, S5_TILE_G, ln, 2, S5_GC, p).transpose(0, 2, 3, 5, 1, 4).reshape(nt, ln, 2 * p, LANE)
    a8 = jnp.stack([jnp.concatenate([pw_re[:, ln], pw_re[:, ln]], axis=-1),
                    jnp.concatenate([-pw_im[:, ln], pw_im[:, ln]], axis=-1)], axis=1)
    a8 = a8.reshape(nt, S5_TILE_G, 2, 2 * p).transpose(0, 2, 1, 3).reshape(nt, 2, S5_TILE_G * 2 * p)
    return k8, bc, cc, a8


def _swap_re_im(x):
    w = x.shape[1]
    if w == LANE:
        return pltpu.roll(x, LANE // 2, 1)
    first_half = (lax.broadcasted_iota(jnp.int32, x.shape, 1) & (LANE // 2)) == 0
    return jnp.where(first_half, pltpu.roll(x, w - LANE // 2, 1), pltpu.roll(x, LANE // 2, 1))


def _s5_expand(bc, cc, w):
    reps = w // LANE
    mask_b = (lax.broadcasted_iota(jnp.int32, (LANE, w), 0) // S5_GC) == (lax.broadcasted_iota(jnp.int32, (LANE, w), 1) // LANE)
    mask_c = (lax.broadcasted_iota(jnp.int32, (w, LANE), 0) // LANE) == (lax.broadcasted_iota(jnp.int32, (w, LANE), 1) // S5_GC)
    b8 = None if bc is None else jnp.where(mask_b, jnp.concatenate([bc] * reps, axis=1), jnp.zeros((), bc.dtype))
    c8 = None if cc is None else jnp.where(mask_c, jnp.concatenate([cc] * reps, axis=0), jnp.zeros((), cc.dtype))
    return b8, c8, mask_b, mask_c


def _state_scan(v, pr, pi, reverse):
    n = v.shape[0]
    row = lax.broadcasted_iota(jnp.int32, v.shape, 0)
    z, s = v, 1
    while s < n:
        if reverse:
            zs = jnp.where(row < n - s, pltpu.roll(z, n - s, 0), 0.0)
        else:
            zs = jnp.where(row >= s, pltpu.roll(z, s, 0), 0.0)
        z = z + zs * pr + _swap_re_im(zs) * pi
        pr, pi = pr * pr - pi * pi, 2.0 * pr * pi
        s *= 2
    return z


def _s5_fwd(proj, u_cb, k8, bc, cc, a8, name):
    t = proj.shape[0]
    nt, ln = bc.shape[:2]
    w = a8.shape[2]
    nc = t // ln
    rb = min(t, S5_ROW_BLOCK)

    def body(u_ref, k_ref, b_ref, c_ref, a_ref, y_ref, x_ref, uf_ref):
        uf_ref[...] = u_ref[...].astype(F32)
        pos = lax.broadcasted_iota(jnp.int32, (rb, LANE), 0) & (ln - 1)
        for r0 in range(0, t, rb):
            u = uf_ref[r0:r0 + rb, :]
            acc = jnp.dot(u.astype(BF16), k_ref[0], preferred_element_type=F32)
            for lag in range(1, ln):
                us = jnp.where(pos >= lag, pltpu.roll(u, lag, 0), 0.0).astype(BF16)
                acc = acc + jnp.dot(us, k_ref[lag], preferred_element_type=F32)
            y_ref[r0:r0 + rb, :] = acc
        v = None
        for s in range(ln):
            part = jnp.dot(uf_ref[pl.ds(s, nc, stride=ln), :].astype(BF16), _s5_expand(b_ref[s], None, w)[0],
                           preferred_element_type=F32)
            v = part if v is None else v + part
        z = _state_scan(v, a_ref[0:1, :], a_ref[1:2, :], reverse=False)
        row = lax.broadcasted_iota(jnp.int32, z.shape, 0)
        x = jnp.where(row >= 1, pltpu.roll(z, 1, 0), 0.0)
        x_ref[...] = x
        x_bf = x.astype(BF16)
        for tt in range(ln):
            y_ref[pl.ds(tt, nc, stride=ln), :] += jnp.dot(x_bf, _s5_expand(None, c_ref[tt], w)[1], preferred_element_type=F32)

    tile = lambda shape: pl.BlockSpec((None,) + shape, lambda j: (j,) + (0,) * len(shape))
    return pl.pallas_call(
        body, name=name, grid=(nt,),
        in_specs=[pl.BlockSpec((t, LANE), lambda j: (0, u_cb + j)), tile((ln, LANE, LANE)), tile(bc.shape[1:]),
                  tile(cc.shape[1:]), tile((2, w))],
        out_specs=[pl.BlockSpec((t, LANE), lambda j: (0, j)), tile((nc, w))],
        out_shape=[jax.ShapeDtypeStruct((t, nt * LANE), F32), jax.ShapeDtypeStruct((nt, nc, w), F32)],
        scratch_shapes=[pltpu.VMEM((t, LANE), F32)],
        compiler_params=_cparams("parallel"),
    )(proj, k8, bc, cc, a8)


def _s5_bwd_data(dy, x_st, k8, bc, cc, a8, name):
    t = dy.shape[0]
    nt, ln = bc.shape[:2]
    w = a8.shape[2]
    nc = t // ln
    rb = min(t, S5_ROW_BLOCK)

    def body(dy_ref, x_ref, k_ref, b_ref, c_ref, a_ref, du_ref, dv_ref, da_ref, dyf_ref, duf_ref):
        dyf_ref[...] = dy_ref[...].astype(F32)
        pos = lax.broadcasted_iota(jnp.int32, (rb, LANE), 0) & (ln - 1)
        for r0 in range(0, t, rb):
            g = dyf_ref[r0:r0 + rb, :]
            acc = _dot_t(g.astype(BF16), k_ref[0], 1, 1)
            for lag in range(1, ln):
                gs = jnp.where(pos < ln - lag, pltpu.roll(g, rb - lag, 0), 0.0).astype(BF16)
                acc = acc + _dot_t(gs, k_ref[lag], 1, 1)
            duf_ref[r0:r0 + rb, :] = acc
        gx = None
        for tt in range(ln):
            part = _dot_t(dyf_ref[pl.ds(tt, nc, stride=ln), :].astype(BF16), _s5_expand(None, c_ref[tt], w)[1], 1, 1)
            gx = part if gx is None else gx + part
        rtot = _state_scan(gx, a_ref[0:1, :], -a_ref[1:2, :], reverse=True)
        row = lax.broadcasted_iota(jnp.int32, rtot.shape, 0)
        dv = jnp.where(row < nc - 1, pltpu.roll(rtot, nc - 1, 0), 0.0)
        dv_ref[...] = dv
        dv_bf = dv.astype(BF16)
        for s in range(ln):
            duf_ref[pl.ds(s, nc, stride=ln), :] += _dot_t(dv_bf, _s5_expand(b_ref[s], None, w)[0], 1, 1)
        du_ref[...] = duf_ref[...].astype(BF16)
        x = x_ref[...]
        da_ref[0:1, :] = jnp.sum(dv * x, axis=0, keepdims=True)
        da_ref[1:2, :] = jnp.sum(dv * _swap_re_im(x), axis=0, keepdims=True)

    tile = lambda shape: pl.BlockSpec((None,) + shape, lambda j: (j,) + (0,) * len(shape))
    return pl.pallas_call(
        body, name=name, grid=(nt,),
        in_specs=[pl.BlockSpec((t, LANE), lambda j: (0, j)), tile((nc, w)), tile((ln, LANE, LANE)), tile(bc.shape[1:]),
                  tile(cc.shape[1:]), tile((2, w))],
        out_specs=[pl.BlockSpec((t, LANE), lambda j: (0, j)), tile((nc, w)), tile((2, w))],
        out_shape=[jax.ShapeDtypeStruct((t, nt * LANE), BF16), jax.ShapeDtypeStruct((nt, nc, w), F32),
                   jax.ShapeDtypeStruct((nt, 2, w), F32)],
        scratch_shapes=[pltpu.VMEM((t, LANE), F32), pltpu.VMEM((t, LANE), F32)],
        compiler_params=_cparams("parallel"),
    )(dy, x_st, k8, bc, cc, a8)


def _s5_bwd_tables(dy, proj, u_cb, x_st, dv, ln, name):
    t = dy.shape[0]
    nt, nc, w = x_st.shape
    rb = min(t, S5_ROW_BLOCK)

    def body(dy_ref, u_ref, x_ref, dv_ref, dk_ref, db_ref, dc_ref, dyf_ref, uf_ref):
        s = pl.program_id(1)

        @pl.when(s == 0)
        def _():
            dyf_ref[...] = dy_ref[...].astype(F32)
            uf_ref[...] = u_ref[...].astype(F32)
            pos = lax.broadcasted_iota(jnp.int32, (rb, LANE), 0) & (ln - 1)
            for r0 in range(0, t, rb):
                u, g_bf = uf_ref[r0:r0 + rb, :], dy_ref[r0:r0 + rb, :]
                for lag in range(ln):
                    us = u if lag == 0 else jnp.where(pos >= lag, pltpu.roll(u, lag, 0), 0.0)
                    part = _dot_t(us.astype(BF16), g_bf, 0, 0)
                    if r0 == 0:
                        dk_ref[lag] = part
                    else:
                        dk_ref[lag] += part

        rows = pl.ds(s, nc, stride=ln)
        _, _, mask_b, mask_c = _s5_expand(None, None, w)
        db = jnp.where(mask_b, _dot_t(uf_ref[rows, :].astype(BF16), dv_ref[...].astype(BF16), 0, 0), 0.0)
        dc = jnp.where(mask_c, _dot_t(x_ref[...].astype(BF16), dyf_ref[rows, :].astype(BF16), 0, 0), 0.0)
        db_ref[...] = sum(db[:, h * LANE:(h + 1) * LANE] for h in range(w // LANE))
        dc_ref[...] = sum(dc[h * LANE:(h + 1) * LANE, :] for h in range(w // LANE))

    tile = lambda shape: pl.BlockSpec((None,) + shape, lambda j, s: (j,) + (0,) * len(shape))
    per_s = lambda shape: pl.BlockSpec((None, None) + shape, lambda j, s: (j, s, 0, 0))
    return pl.pallas_call(
        body, name=name, grid=(nt, ln),
        in_specs=[pl.BlockSpec((t, LANE), lambda j, s: (0, j)), pl.BlockSpec((t, LANE), lambda j, s: (0, u_cb + j)),
                  tile((nc, w)), tile((nc, w))],
        out_specs=[tile((ln, LANE, LANE)), per_s((LANE, LANE)), per_s((LANE, LANE))],
        out_shape=[jax.ShapeDtypeStruct((nt, ln, LANE, LANE), F32)] * 3,
        scratch_shapes=[pltpu.VMEM((t, LANE), F32), pltpu.VMEM((t, LANE), F32)],
        compiler_params=_cparams("parallel", "arbitrary"),
    )(dy, proj, x_st, dv)


def _gelu_parts(y):
    inner = GELU_C * (y + GELU_A * y * y * y)
    th = jnp.tanh(inner)
    return th, 0.5 * y * (1.0 + th)


def _s5_post_fwd(y_raw, proj, u_cb, s5d, wglu, bglu, name):
    w = y_raw.shape[1]

    def fn(yr, u, dsk, wg, bg):
        y = yr + dsk * u.astype(F32)
        _, h = _gelu_parts(y)
        gl = jnp.dot(h.astype(BF16), wg, preferred_element_type=F32) + bg
        return (h * _sigmoid(gl),)

    return _rowwise(fn, [("row", y_raw), ("win", proj, w, u_cb), ("full", s5d), ("full", wglu), ("full", bglu)],
                    [(w, BF16)], [], rows=y_raw.shape[0], tb=_tile(y_raw.shape[0], 512), name=name)[0]


def _s5_post_bwd(y_raw, proj, u_cb, s5d, wglu, bglu, dob, name):
    w = y_raw.shape[1]

    def fn(yr, u, dsk, wg, bg, dov):
        u = u.astype(F32)
        dov = dov.astype(F32)
        y = yr + dsk * u
        th, h = _gelu_parts(y)
        h_bf = h.astype(BF16)
        gl = jnp.dot(h_bf, wg, preferred_element_type=F32) + bg
        sg = _sigmoid(gl)
        dgl = dov * h * sg * (1.0 - sg)
        dgl_bf = dgl.astype(BF16)
        dh = dov * sg + _dot_t(dgl_bf, wg, 1, 1)
        dgelu = 0.5 * (1.0 + th) + 0.5 * y * (1.0 - th * th) * GELU_C * (1.0 + 3.0 * GELU_A * y * y)
        dy = dh * dgelu
        return (dy, dy * dsk,
                _dot_t(h_bf, dgl_bf, 0, 0), jnp.sum(dgl, axis=0, keepdims=True), jnp.sum(dy * u, axis=0, keepdims=True))

    return _rowwise(fn, [("row", y_raw), ("win", proj, w, u_cb), ("full", s5d), ("full", wglu), ("full", bglu), ("row", dob)],
                    [(w, BF16), (w, BF16)], [(w, w), (1, w), (1, w)], rows=y_raw.shape[0], tb=_tile(y_raw.shape[0], 512), name=name)


def _adamw(w, g, m, v, name):
    _, rows, cols = w.shape
    tr, tc = (_tile(rows, 256, align=16), cols) if rows % 16 == 0 else (rows, _tile(cols, 256))
    slots = isinstance(g, (list, tuple))
    gs = list(g) if slots else [g]
    c1 = 1.0 - ADAM_B1 ** ADAM_STEP
    c2 = 1.0 - ADAM_B2 ** ADAM_STEP

    def body(w_ref, m_ref, v_ref, *refs):
        g_refs, out_refs = refs[:len(gs)], refs[len(gs):]
        if slots:
            parts = [g_ref[s].astype(F32) for g_ref in g_refs for s in range(g_ref.shape[0])]
            gv = parts[0]
            for p in parts[1:]:
                gv = gv + p
            out_refs[0][...] = gv
        else:
            gv = g_refs[0][...]
        d_ref, nm_ref, nv_ref = out_refs[-3:]
        nm = ADAM_B1 * m_ref[...] + (1.0 - ADAM_B1) * gv
        nv = ADAM_B2 * v_ref[...] + (1.0 - ADAM_B2) * (gv * gv)
        d_ref[...] = -ADAM_LR * ((nm / c1) / (jnp.sqrt(nv / c2) + ADAM_EPS) + ADAM_WD * w_ref[...])
        nm_ref[...] = nm
        nv_ref[...] = nv

    spec = pl.BlockSpec((None, tr, tc), lambda i, j: (0, i, j))
    g_specs = [pl.BlockSpec((a.shape[0], tr, tc), lambda i, j: (0, i, j)) for a in gs] if slots else [pl.BlockSpec((tr, tc), lambda i, j: (i, j))]
    n_out = 4 if slots else 3
    return pl.pallas_call(
        body, name=name, grid=(rows // tr, cols // tc),
        in_specs=[spec, spec, spec] + g_specs, out_specs=[spec] * n_out,
        out_shape=[jax.ShapeDtypeStruct((1, rows, cols), F32)] * n_out,
        compiler_params=_cparams("parallel", "parallel"),
    )(w, m, v, *gs)


def _slot_sum(x, name):
    _, rows, cols = x.shape
    if rows % 8 == 0:
        tr, tc = _tile(rows, 512, align=8), cols
    else:
        tr, tc = rows, _tile(cols, 256)

    def body(x_ref, o_ref):
        acc = x_ref[0].astype(F32)
        for s in range(1, N_DEV):
            acc = acc + x_ref[s].astype(F32)
        o_ref[...] = acc

    return pl.pallas_call(
        body, name=name, grid=(rows // tr, cols // tc),
        in_specs=[pl.BlockSpec((N_DEV, tr, tc), lambda i, j: (0, i, j))],
        out_specs=pl.BlockSpec((tr, tc), lambda i, j: (i, j)),
        out_shape=jax.ShapeDtypeStruct((rows, cols), F32),
        compiler_params=_cparams("parallel", "parallel"),
    )(x)


_REST = (("w_a2", 1), ("w_glu", 0), ("w_branch_a", 1), ("w_branch_b", 1), ("w_out", 0), ("w_ffn_in", 1), ("w_ffn_out", 0))
_SMALL = ("norm1_g", "b_a2", "gla_norm_g", "lam_re", "lam_im", "log_dt", "s5_b_re", "s5_b_im", "s5_c_re", "s5_c_im",
          "s5_d", "b_glu", "norm2_g", "final_norm_g")
_ORDER = ("norm1_g", "w_in", "w_a2", "b_a2", "gla_norm_g", "lam_re", "lam_im", "log_dt", "s5_b_re", "s5_b_im", "s5_c_re",
          "s5_c_im", "s5_d", "w_glu", "b_glu", "w_branch_a", "w_branch_b", "w_out", "norm2_g", "w_ffn_in", "w_ffn_out", "final_norm_g")


def _join_slots(slots, axis):
    _, r, c = slots.shape
    if axis == 0:
        return slots.reshape(N_DEV * r, c)
    return slots.transpose(1, 0, 2).reshape(r, N_DEV * c)


def _to_slots(full, axis):
    r, c = full.shape
    if axis == 0:
        return full.reshape(N_DEV, r // N_DEV, c)
    return full.reshape(r, N_DEV, c // N_DEV).transpose(1, 0, 2)


def _local_step(x, target, w_in_t, small, rest):
    t, d = x.shape
    dk, dv, s5w = d // 4, d // 2, d // 4
    dist = not isinstance(rest, dict)
    if dist:
        h1, (w_in_slots,) = _rms_fwd(x, small["norm1_g"], "norm1_fwd", carry=("ag", [w_in_t]))
        w_in_t = w_in_slots.reshape(-1, d)
    else:
        h1 = _rms_fwd(x, small["norm1_g"], "norm1_fwd")
    o_q, o_k, o_v, o_r, o_al = 0, dk, 2 * dk, 2 * dk + dv, 2 * dk + 2 * dv
    o_u = o_al + GLA_RANK
    o_ga, o_gb = o_u + s5w, o_u + s5w + d
    rows = lambda a, o, n: a[o:o + n]
    w_al_t = jnp.pad(rows(w_in_t, o_al, GLA_RANK), ((0, LANE - GLA_RANK), (0, 0)))
    w_ext_t = jnp.concatenate([rows(w_in_t, o_ga, d), rows(w_in_t, o_gb, d), rows(w_in_t, o_v, dv), rows(w_in_t, o_r, dv),
                               rows(w_in_t, o_q, dk), rows(w_in_t, o_k, dk), rows(w_in_t, o_u, s5w), w_al_t], axis=0)
    n_main = 2 * d + 2 * dv + 2 * dk + s5w
    u_cb = (2 * d + 2 * dv + 2 * dk) // s5w

    if dist:
        proj, (a2_s, glu_s, w_ffn_in_s) = _mm(h1, w_ext_t, tb=True, n_limit=n_main, out_dtype=BF16,
                                              carry=("ag", [rest[0], rest[1], rest[5]]), name="in_proj")
        w = {"w_a2": _join_slots(a2_s, 1), "w_glu": _join_slots(glu_s, 0)}
    else:
        proj = _mm(h1, w_ext_t, tb=True, n_limit=n_main, out_dtype=BF16, name="in_proj")
        w = rest
        w_ffn_in_s = _to_slots(rest["w_ffn_in"], 1)
    wa2 = jnp.pad(w["w_a2"], ((0, LANE - GLA_RANK), (0, 0)))
    alow = _mm(h1, w_al_t, tb=True, out_dtype=BF16, name="in_proj_gate_rank")
    if dist:
        o_a, o_pre, states, got = _gla_fwd(proj, alow, wa2, small["b_a2"], small["gla_norm_g"], dk=dk, dv=dv, name="gla_fwd",
                                           carry=("ag", rest[2:5]))
        w.update({n: _join_slots(g, ax) for (n, ax), g in zip(_REST[2:5], got)})
    else:
        o_a, o_pre, states = _gla_fwd(proj, alow, wa2, small["b_a2"], small["gla_norm_g"], dk=dk, dv=dv, name="gla_fwd")

    s5_params = (small["lam_re"], small["lam_im"], small["log_dt"][0], small["s5_b_re"], small["s5_b_im"],
                 small["s5_c_re"], small["s5_c_im"])
    (k8, bc, cc, a8), tables_vjp = jax.vjp(_s5_tables, *s5_params)
    k8_bf, b8_bf, c8_bf = k8.astype(BF16), bc.astype(BF16), cc.astype(BF16)
    u_lane_cb = u_cb * s5w // LANE
    y_raw, x_st = _s5_fwd(proj, u_lane_cb, k8_bf, b8_bf, c8_bf, a8, "s5_scan_fwd")
    o_b = _s5_post_fwd(y_raw, proj, u_cb, small["s5_d"], w["w_glu"], small["b_glu"], "s5_post_fwd")

    pa = _mm(o_a, w["w_branch_a"], out_dtype=BF16, name="branch_a")
    pb = _mm(o_b, w["w_branch_b"], out_dtype=BF16, name="branch_b")
    mix = _mix_fwd(proj, pa, pb, d, "mix_fwd")
    x1 = _mm(mix, w["w_out"], res=x, name="out_proj")
    h2 = _rms_fwd(x1, small["norm2_g"], "norm2_fwd")
    if dist:
        gu, act, (w_ffn_out_s,) = _ffn_in_fused(h2, w_ffn_in_s, carry=("ag", rest[-1:]), name="ffn_in")
        w_ffn_out = _join_slots(w_ffn_out_s, 0)
    else:
        gu, act = _ffn_in_fused(h2, w_ffn_in_s, name="ffn_in")
        w_ffn_out = rest["w_ffn_out"]
    x2 = _mm(act, w_ffn_out, res=x1, name="ffn_out")
    dx2, dx2_bf, d_final_g, loss = _loss_head(x2, small["final_norm_g"], target, "loss_head")

    recv = {}
    dgu, = _mm(dx2_bf, w_ffn_out, tb=True, epi=(_swiglu_bwd_tile, [gu], [(2, BF16)]), name="d_act")
    g_ffn_out = _mm(act, dx2_bf, ta=True, out_dtype=BF16, name="g_w_ffn_out")
    if dist:
        g_ffn_in_s, recv["w_ffn_out"] = _mm(h2, dgu, ta=True, b_slots=True, out_dtype=BF16, out_slots=N_DEV, tm_cap=512,
                                            carry=("a2a", [_to_slots(g_ffn_out, 0)]), name="g_w_ffn_in")
        dh2, (recv_ffn_in,) = _mm(dgu, w_ffn_in_s, tb=True, a_slots=True, b_slots=True, b_group=4, tm_cap=512,
                                  out_dtype=BF16, carry=("a2a", [g_ffn_in_s], [_ALL_K[:-1]]), name="d_h2")
    else:
        g_ffn_in_s = _mm(h2, dgu, ta=True, b_slots=True, out_dtype=BF16, out_slots=N_DEV, tm_cap=512, name="g_w_ffn_in")
        dh2 = _mm(dgu, w_ffn_in_s, tb=True, a_slots=True, b_slots=True, b_group=4, tm_cap=512, out_dtype=BF16, name="d_h2")
    dx1, dx1_bf, d_norm2_g = _rms_bwd(x1, small["norm2_g"], dh2, dx2, "norm2_bwd", True)
    dmix = _mm(dx1_bf, w["w_out"], tb=True, out_dtype=BF16, name="d_mix")
    g_out = _mm(mix, dx1_bf, ta=True, out_dtype=BF16, name="g_w_out")
    dpa, dpb, dga, dgb = _mix_bwd(proj, pa, pb, dmix, d, "mix_bwd")
    doa = _mm(dpa, w["w_branch_a"], tb=True, out_dtype=BF16, name="d_o_a")
    dob = _mm(dpb, w["w_branch_b"], tb=True, out_dtype=BF16, name="d_o_b")
    g_branch_a = _mm(o_a, dpa, ta=True, out_dtype=BF16, name="g_w_branch_a")
    g_branch_b = _mm(o_b, dpb, ta=True, out_dtype=BF16, name="g_w_branch_b")

    dy_s5, du_direct, g_glu, g_bglu, g_s5d = _s5_post_bwd(y_raw, proj, u_cb, small["s5_d"], w["w_glu"], small["b_glu"], dob, "s5_post_bwd")
    du_scan, dv_st, d_a8 = _s5_bwd_data(dy_s5, x_st, k8_bf, b8_bf, c8_bf, a8, "s5_scan_bwd")
    d_k8, d_b8, d_c8 = _s5_bwd_tables(dy_s5, proj, u_lane_cb, x_st, dv_st, S5_L, "s5_scan_bwd_tables")
    g_lam_re, g_lam_im, g_log_dt, g_b_re, g_b_im, g_c_re, g_c_im = tables_vjp((d_k8, d_b8, d_c8, d_a8))
    du = du_scan + du_direct

    dq, dkk, dvv, dr, dal, g_wa2, g_ba2, g_ghn = _gla_bwd(proj, alow, wa2, small["b_a2"], small["gla_norm_g"], o_pre, states, doa,
                                                        dk=dk, dv=dv, name="gla_bwd")
    dproj = jnp.concatenate([dga, dgb, dvv, dr, dq, dkk, du, dal], axis=1)
    mid = {"w_out": g_out, "w_branch_a": g_branch_a, "w_branch_b": g_branch_b, "w_glu": g_glu.astype(BF16),
           "w_a2": g_wa2[:GLA_RANK].astype(BF16)}
    if dist:
        axes = dict(_REST)
        g_main_t, got = _mm(dproj, h1, ta=True, m_limit=n_main, out_dtype=BF16, name="g_w_in_main",
                            carry=("a2a", [_to_slots(mid[n], axes[n]) for n in mid] + [g_ffn_in_s],
                                   [_ALL_K] * len(mid) + [_ALL_K[-1:]]))
        recv.update(zip(mid, [[g] for g in got[:-1]]))
        recv["w_ffn_in"] = [recv_ffn_in, got[-1]]
    else:
        g_main_t = _mm(dproj, h1, ta=True, m_limit=n_main, out_dtype=BF16, name="g_w_in_main")
    g_al_t = _mm(dal, h1, ta=True, out_dtype=BF16, name="g_w_in_gate_rank")
    mrows = lambda o, n: g_main_t[o:o + n]
    g_w_in_t = jnp.concatenate([mrows(2 * d + 2 * dv, dk), mrows(2 * d + 2 * dv + dk, dk), mrows(2 * d, dv), mrows(2 * d + dv, dv),
                                g_al_t[:GLA_RANK], mrows(2 * d + 2 * dv + 2 * dk, s5w), mrows(0, d), mrows(d, d)], axis=0)
    if dist:
        dh1, recv["w_in"] = _mm(dproj, w_ext_t, out_dtype=BF16, carry=("a2a", [_to_slots(g_w_in_t, 0)]), name="d_h1")
    else:
        dh1 = _mm(dproj, w_ext_t, out_dtype=BF16, name="d_h1")
    grad_x, d_norm1_g = _rms_bwd(x, small["norm1_g"], dh1, dx1, "norm1_bwd", False)

    small_g = {
        "norm1_g": d_norm1_g, "b_a2": g_ba2, "gla_norm_g": g_ghn, "lam_re": g_lam_re, "lam_im": g_lam_im,
        "log_dt": g_log_dt[None], "s5_b_re": g_b_re, "s5_b_im": g_b_im, "s5_c_re": g_c_re, "s5_c_im": g_c_im,
        "s5_d": g_s5d, "b_glu": g_bglu, "norm2_g": d_norm2_g, "final_norm_g": d_final_g,
    }
    if not dist:
        recv = dict(mid, w_in=g_w_in_t, w_ffn_in=_join_slots(g_ffn_in_s, 1), w_ffn_out=g_ffn_out)
    return loss[0, 0], grad_x, recv, small_g


def _small_2d(name, a):
    a = a[0]
    return a[None] if a.ndim == 1 else a


def kernel(x, norm1_g, w_in, w_a2, b_a2, gla_norm_g, lam_re, lam_im, log_dt, s5_b_re, s5_b_im, s5_c_re, s5_c_im, s5_d, w_glu, b_glu, w_branch_a, w_branch_b, w_out, norm2_g, w_ffn_in, w_ffn_out, final_norm_g, loss_target, m_norm1_g, m_w_in, m_w_a2, m_b_a2, m_gla_norm_g, m_lam_re, m_lam_im, m_log_dt, m_s5_b_re, m_s5_b_im, m_s5_c_re, m_s5_c_im, m_s5_d, m_w_glu, m_b_glu, m_w_branch_a, m_w_branch_b, m_w_out, m_norm2_g, m_w_ffn_in, m_w_ffn_out, m_final_norm_g, v_norm1_g, v_w_in, v_w_a2, v_b_a2, v_gla_norm_g, v_lam_re, v_lam_im, v_log_dt, v_s5_b_re, v_s5_b_im, v_s5_c_re, v_s5_c_im, v_s5_d, v_w_glu, v_b_glu, v_w_branch_a, v_w_branch_b, v_w_out, v_norm2_g, v_w_ffn_in, v_w_ffn_out, v_final_norm_g):
    args = dict(locals())
    weights = {n: args[n] for n in _ORDER}
    m_in = {n: args["m_" + n] for n in _ORDER}
    v_in = {n: args["v_" + n] for n in _ORDER}
    transposed = lambda a: a[0].T[None]
    rest = [weights[n][0].astype(BF16) for n, _ in _REST]
    small = {n: _small_2d(n, weights[n]) for n in _SMALL}
    loss_local, grad_x, recv, small_g = _local_step(x[0], loss_target[0], transposed(weights["w_in"])[0].astype(BF16), small, rest)

    grads, delta, new_m, new_v = {}, {}, {}, {}
    for n, _ in _REST:
        grads[n], delta[n], new_m[n], new_v[n] = _adamw(weights[n], recv[n], m_in[n], v_in[n], "adamw_" + n)
    w_in_out = _adamw(transposed(weights["w_in"]), recv["w_in"], transposed(m_in["w_in"]), transposed(v_in["w_in"]), "adamw_w_in")
    grads["w_in"], delta["w_in"], new_m["w_in"], new_v["w_in"] = (transposed(a) for a in w_in_out)

    s_sizes = [small_g[n].size for n in _SMALL]
    s_offs = [sum(s_sizes[:i]) for i in range(len(s_sizes))]
    s_total = sum(s_sizes)
    s_rows = -(-(-(-(s_total + 1) // LANE)) // LANE) * LANE

    def pack_small(parts):
        flat = jnp.concatenate([p.reshape(-1) for p in parts])
        return jnp.pad(flat, (0, s_rows * LANE - flat.size)).reshape(s_rows, LANE)

    s_flat = pack_small([small_g[n] for n in _SMALL] + [loss_local])
    s_red = _slot_sum(_exchange("ag", [s_flat], "small_grads_all_gather")[0], "small_grads_slot_sum")
    loss = s_red.reshape(-1)[s_total]
    sd, sm, sv = _adamw(pack_small([weights[n] for n in _SMALL])[None], s_red, pack_small([m_in[n] for n in _SMALL])[None],
                        pack_small([v_in[n] for n in _SMALL])[None], "adamw_small")
    sd, sm, sv = sd[0], sm[0], sv[0]
    for n, o, s in zip(_SMALL, s_offs, s_sizes):
        shape = weights[n].shape[1:]
        grads[n], delta[n], new_m[n], new_v[n] = (a.reshape(-1)[o:o + s].reshape(shape) for a in (s_red, sd, sm, sv))

    out = [loss, grad_x[None]]
    for tree in (grads, delta, new_m, new_v):
        out += [tree[n].reshape(weights[n].shape) for n in _ORDER]
    return tuple(out)
```

```python
import functools
import math

import jax
import jax.numpy as jnp
from jax import lax
from jax.experimental import pallas as pl
from jax.experimental.pallas import tpu as pltpu

F32 = jnp.float32
BF16 = jnp.bfloat16

NORM_EPS = 1e-6
N_DEV = 8
N_PEER = N_DEV - 1
GLA_HEADS = 4
GLA_CHUNK = 32
GLA_CHUNK_SHIFT = 5
GLA_TAU = 16.0
GLA_RANK = 16
GLA_BLOCK = 256
S5_GC = 16
S5_P = 64
S5_L = 16
S5_TILE_G = 8
S5_ROW_BLOCK = 2048
LANE = 128
V7X_VMEM_LIMIT = 56 * 1024 * 1024
V7X_MM_VMEM_BUDGET = 40 * 1024 * 1024
V7X_MM_TILE_MN = 1408
V7X_MM_TILE_MN_WHOLE_K = 512
V7X_MM_TILE_K = 2048

ADAM_LR = 0.001
ADAM_B1 = 0.9
ADAM_B2 = 0.999
ADAM_EPS = 1e-08
ADAM_WD = 0.01
ADAM_STEP = 10

GELU_C = math.sqrt(2.0 / math.pi)
GELU_A = 0.044715

MESH = pl.DeviceIdType.MESH


def _cparams(*sem):
    return pltpu.CompilerParams(dimension_semantics=sem, vmem_limit_bytes=V7X_VMEM_LIMIT)


def _divisors_down(n, start, align=LANE):
    t = (min(start, n) // align) * align
    found = False
    while t >= align:
        if n % t == 0:
            found = True
            yield t
        t -= align
    if not found:
        yield n


def _tile(n, target, align=LANE):
    return next(_divisors_down(n, target, align))


def _sigmoid(x):
    return 1.0 / (1.0 + jnp.exp(-x))


_HBM_SPEC = pl.BlockSpec(memory_space=pltpu.HBM)


def _exchange_scratch(n):
    return [pltpu.SemaphoreType.DMA((n * N_PEER,)), pltpu.SemaphoreType.DMA((n * N_PEER,)), pltpu.SemaphoreType.DMA((n,))]


def _ag_phases(x_refs, out_refs, send_sems, recv_sems, local_sems):
    n = len(x_refs)
    x, y, c = lax.axis_index("x"), lax.axis_index("y"), lax.axis_index("c")
    me, sibling = (x, y, c), (x, y, 1 - c)
    chips = [(1 - x, y), (x, 1 - y), (1 - x, 1 - y)]

    def copy(a, k, block, to, from_input=False):
        dst = out_refs[a].at[4 * block[0] + 2 * block[1] + block[2]]
        return pltpu.make_async_remote_copy(
            src_ref=x_refs[a] if from_input else dst, dst_ref=dst,
            send_sem=send_sems.at[a * N_PEER + k], recv_sem=recv_sems.at[a * N_PEER + k], device_id=to, device_id_type=MESH)

    def local(a):
        return pltpu.make_async_copy(x_refs[a], out_refs[a].at[4 * x + 2 * y + c], local_sems.at[a])

    def first(a):
        return [copy(a, 0, me, sibling, True)] + [copy(a, 1 + j, me, (*chip, c), True) for j, chip in enumerate(chips)]

    def start():
        for a in range(n):
            local(a).start()
            for cp in first(a):
                cp.start()

    def relay():
        for j, chip in enumerate(chips):
            for a in range(n):
                copy(a, 1 + j, (*chip, c), me).wait_recv()
                copy(a, 4 + j, (*chip, c), sibling).start()

    def finish():
        for a in range(n):
            copy(a, 0, sibling, me).wait_recv()
            for j, chip in enumerate(chips):
                copy(a, 4 + j, (*chip, 1 - c), me).wait_recv()
        for a in range(n):
            for cp in first(a) + [copy(a, 4 + j, (*chip, c), sibling) for j, chip in enumerate(chips)]:
                cp.wait_send()
            local(a).wait()

    return start, relay, finish


_ALL_K = tuple(range(N_DEV))


def _a2a_phases(x_refs, out_refs, send_sems, recv_sems, local_sems, ks_list=None):
    n = len(x_refs)
    ks_list = ks_list or [_ALL_K] * n
    x, y, c = lax.axis_index("x"), lax.axis_index("y"), lax.axis_index("c")
    my = 4 * x + 2 * y + c

    def copy(a, k):
        px, py, pc = (1 - x if k & 4 else x), (1 - y if k & 2 else y), (1 - c if k & 1 else c)
        return pltpu.make_async_remote_copy(
            src_ref=x_refs[a].at[4 * px + 2 * py + pc], dst_ref=out_refs[a].at[ks_list[a].index(k)],
            send_sem=send_sems.at[a * N_PEER + k - 1], recv_sem=recv_sems.at[a * N_PEER + k - 1],
            device_id=(px, py, pc), device_id_type=MESH)

    def local(a):
        return pltpu.make_async_copy(x_refs[a].at[my], out_refs[a].at[ks_list[a].index(0)], local_sems.at[a])

    def start():
        for a in range(n):
            for k in ks_list[a]:
                (copy(a, k) if k else local(a)).start()

    def relay():
        pass

    def finish():
        for a in range(n):
            for k in ks_list[a]:
                if k:
                    copy(a, k).wait_recv()
        for a in range(n):
            for k in ks_list[a]:
                if k:
                    copy(a, k).wait_send()
                else:
                    local(a).wait()

    return start, relay, finish


def _exchange_out_shapes(kind, arrays, ks_list=None):
    if kind == "ag":
        return [jax.ShapeDtypeStruct((N_DEV,) + a.shape, a.dtype) for a in arrays]
    ks_list = ks_list or [_ALL_K] * len(arrays)
    return [jax.ShapeDtypeStruct((len(ks),) + a.shape[1:], a.dtype) for a, ks in zip(arrays, ks_list)]


def _exchange(kind, arrays, name):
    n = len(arrays)
    phases = _ag_phases if kind == "ag" else _a2a_phases

    def body(*refs):
        start, relay, finish = phases(refs[:n], refs[n:2 * n], *refs[2 * n:])
        start()
        relay()
        finish()

    return pl.pallas_call(
        body, name=name,
        out_shape=_exchange_out_shapes(kind, arrays),
        in_specs=[_HBM_SPEC] * n, out_specs=[_HBM_SPEC] * n,
        scratch_shapes=_exchange_scratch(n),
    )(*arrays)


def _mm_tiles(m, n_unit, k_unit, tile_bytes, small_tiles_ok=True, tm_cap=0):
    fits = lambda tm, tn, tk: 2 * 2 * (tm * tk + tk * tn) + tile_bytes * tm * tn <= V7X_MM_VMEM_BUDGET
    for cap in (V7X_MM_TILE_MN, V7X_MM_TILE_MN_WHOLE_K) if small_tiles_ok else (V7X_MM_TILE_MN,):
        tm, tn = _tile(m, min(cap, tm_cap or cap)), _tile(n_unit, cap)
        if fits(tm, tn, k_unit) and (tn >= V7X_MM_TILE_MN_WHOLE_K or tn == n_unit):
            return tm, tn, k_unit
    tm, tn = _tile(m, tm_cap or V7X_MM_TILE_MN), _tile(n_unit, V7X_MM_TILE_MN)
    for tk in _divisors_down(k_unit, k_unit if tm_cap else V7X_MM_TILE_K):
        if fits(tm, tn, tk):
            return tm, tn, tk
    return tm, tn, _tile(k_unit, LANE)


def _carry_parts(carry):
    kind, arrays, ks_list = (tuple(carry) + (None,))[:3] if carry is not None else (None, [], None)
    n = len(arrays)
    kind = (kind, ks_list)
    return kind, arrays, [_HBM_SPEC] * n, _exchange_out_shapes(kind[0], arrays, ks_list), (_exchange_scratch(n) if n else [])


def _carry_hooks(kind, x_refs, out_refs, sems, step, last_step):
    if not x_refs:
        return lambda: None
    kind, ks_list = kind
    if kind == "ag":
        start, relay, finish = _ag_phases(x_refs, out_refs, *sems)
    else:
        start, relay, finish = _a2a_phases(x_refs, out_refs, *sems, ks_list=ks_list)
    pl.when(step == 0)(start)

    def after():
        if kind == "ag":
            pl.when(step == (last_step * 7) // 8)(relay)
        pl.when(step == last_step)(finish)

    return after


def _mm(a, b, *, ta=False, tb=False, out_dtype=F32, res=None, carry=None, a_slots=False, b_slots=False, b_group=0,
        out_slots=0, epi=None, m_limit=0, n_limit=0, tm_cap=0, name):
    if a_slots:
        assert not ta
        a_n, m, a_c = a.shape
        k = a_n * a_c
    else:
        m, k = (a.shape[1], a.shape[0]) if ta else a.shape
    if b_slots:
        b_n, b_r, b_c = b.shape
        k2, n = (b_n * b_c, b_r) if tb else (b_r, b_n * b_c)
    else:
        k2, n = (b.shape[1], b.shape[0]) if tb else b.shape
    assert k == k2, (a.shape, b.shape, ta, tb)
    m, n = m_limit or m, n_limit or n
    has_res = res is not None
    assert not (has_res and (out_slots or epi))
    n_units = [n] + ([n // out_slots] if out_slots else []) + ([b_c] if b_slots and not tb else [])
    k_units = [k] + ([a_c] if a_slots else []) + ([b_c] if b_slots and tb else [])
    n_unit, k_unit = min(n_units), min(k_units)
    assert all(u % n_unit == 0 for u in n_units) and all(u % k_unit == 0 for u in k_units)
    epi_fn, epi_ins, epi_outs = epi if epi is not None else (None, [], [])
    tile_bytes = 4 + (2 * res.dtype.itemsize if has_res else 0)
    tile_bytes += sum(2 * e.shape[0] * e.dtype.itemsize for e in epi_ins)
    tile_bytes += sum(2 * l * jnp.dtype(dt).itemsize for l, dt in epi_outs) if epi else 2 * jnp.dtype(out_dtype).itemsize
    tm, tn, tk = _mm_tiles(m, n_unit, k_unit, tile_bytes, small_tiles_ok=not epi, tm_cap=tm_cap)
    if b_group:
        tk = b_group * b_c
        assert b_slots and tb and k % tk == 0 and (not a_slots or a_c % tk == 0)
    ni, nj, nk = m // tm, n // tn, k // tk
    dims = (((0,) if ta else (1,), (1,) if tb else (0,)), ((), ()))

    def slot_map(per, pos):
        if pos == "k_cols":
            return lambda i, j, kk: (kk // per, i, kk % per)
        if pos == "k_cols_j":
            return lambda i, j, kk: (kk // per, j, kk % per)
        if pos == "n_cols_k":
            return lambda i, j, kk: (j // per, kk, j % per)
        return lambda i, j, kk: (j // per, i, j % per)

    if a_slots:
        a_spec = pl.BlockSpec((None, tm, tk), slot_map(a_c // tk, "k_cols"))
    else:
        a_spec = pl.BlockSpec((tk, tm), lambda i, j, kk: (kk, i)) if ta else pl.BlockSpec((tm, tk), lambda i, j, kk: (i, kk))
    if b_group:
        b_spec = pl.BlockSpec((b_group, tn, b_c), lambda i, j, kk: (kk, j, 0))
    elif b_slots and tb:
        b_spec = pl.BlockSpec((None, tn, tk), slot_map(b_c // tk, "k_cols_j"))
    elif b_slots:
        b_spec = pl.BlockSpec((None, tk, tn), slot_map(b_c // tn, "n_cols_k"))
    else:
        b_spec = pl.BlockSpec((tn, tk), lambda i, j, kk: (j, kk)) if tb else pl.BlockSpec((tk, tn), lambda i, j, kk: (kk, j))
    if epi:
        lead_spec = lambda l: pl.BlockSpec((l, tm, tn), lambda i, j, kk: (0, i, j))
        o_specs = [lead_spec(l) for l, _ in epi_outs]
        o_shapes = [jax.ShapeDtypeStruct((l, m, n), dt) for l, dt in epi_outs]
    elif out_slots:
        o_specs = [pl.BlockSpec((None, tm, tn), slot_map((n // out_slots) // tn, "n_cols_i"))]
        o_shapes = [jax.ShapeDtypeStruct((out_slots, m, n // out_slots), out_dtype)]
    else:
        o_specs = [pl.BlockSpec((tm, tn), lambda i, j, kk: (i, j))]
        o_shapes = [jax.ShapeDtypeStruct((m, n), out_dtype)]
    extra_ins = ([res] if has_res else []) + list(epi_ins)
    extra_specs = ([o_specs[0]] if has_res else []) + [pl.BlockSpec((e.shape[0], tm, tn), lambda i, j, kk: (0, i, j)) for e in epi_ins]
    n_in, n_out = 2 + len(extra_ins), len(o_specs)
    c_kind, c_arrays, c_specs, c_shapes, c_scratch = _carry_parts(carry)
    nc = len(c_arrays)
    last_step = ni * nj * nk - 1

    def body(*refs):
        a_ref, b_ref = refs[0], refs[1]
        e_refs = refs[2:n_in]
        x_refs = refs[n_in:n_in + nc]
        o_refs = refs[n_in + nc:n_in + nc + n_out]
        out_refs = refs[n_in + nc + n_out:n_in + 2 * nc + n_out]
        scratch = refs[n_in + 2 * nc + n_out:]
        acc = scratch[0] if nk > 1 else None
        kk = pl.program_id(2)
        step = (pl.program_id(0) * nj + pl.program_id(1)) * nk + kk
        after = _carry_hooks(c_kind, x_refs, out_refs, scratch[-3:], step, last_step)

        def emit(val):
            if has_res:
                val = val + e_refs[0][...].astype(F32)
            if epi:
                for o_ref, parts in zip(o_refs, epi_fn(val, *[e[...] for e in e_refs])):
                    for l, v in enumerate(parts):
                        o_ref[l] = v.astype(o_ref.dtype)
            else:
                o_refs[0][...] = val.astype(out_dtype)

        if b_group:
            part = sum(lax.dot_general(a_ref[:, s * b_c:(s + 1) * b_c], b_ref[s], dims, preferred_element_type=F32)
                       for s in range(b_group))
        else:
            part = lax.dot_general(a_ref[...], b_ref[...], dims, preferred_element_type=F32)
        if nk == 1:
            emit(part)
        else:
            @pl.when(kk == 0)
            def _():
                acc[...] = part

            @pl.when((kk > 0) & (kk < nk - 1))
            def _():
                acc[...] += part

            @pl.when(kk == nk - 1)
            def _():
                emit(acc[...] + part)

        after()

    sem = ("arbitrary",) * 3 if nc else ("parallel", "parallel", "arbitrary")
    outs = pl.pallas_call(
        body, name=name,
        grid=(ni, nj, nk),
        in_specs=[a_spec, b_spec] + extra_specs + c_specs,
        out_specs=o_specs + c_specs,
        out_shape=o_shapes + c_shapes,
        scratch_shapes=([pltpu.VMEM((tm, tn), F32)] if nk > 1 else []) + c_scratch,
        compiler_params=_cparams(*sem),
    )(a, b, *extra_ins, *c_arrays)
    main = list(outs[:n_out]) if epi else outs[0]
    return (main, list(outs[n_out:])) if nc else main


def _ffn_in_fused(h2, w_s, *, carry=None, name):
    t, d = h2.shape
    n_slot, _, c = w_s.shape
    half = n_slot // 2
    tm = _tile(t, 512)
    c_kind, c_arrays, c_specs, c_shapes, c_scratch = _carry_parts(carry)
    nc = len(c_arrays)
    last_step = (t // tm) * half - 1

    def body(h_ref, wg_ref, wu_ref, *refs):
        x_refs, (gu_ref, act_ref), out_refs, sems = refs[:nc], refs[nc:nc + 2], refs[nc + 2:2 * nc + 2], refs[2 * nc + 2:]
        step = pl.program_id(0) * half + pl.program_id(1)
        after = _carry_hooks(c_kind, x_refs, out_refs, sems, step, last_step)
        h = h_ref[...]
        g = jnp.dot(h, wg_ref[...], preferred_element_type=F32)
        u = jnp.dot(h, wu_ref[...], preferred_element_type=F32)
        sg = _sigmoid(g)
        silu = g * sg
        gu_ref[0] = (u * (sg + silu - silu * sg)).astype(BF16)
        gu_ref[1] = silu.astype(BF16)
        act_ref[...] = (silu * u).astype(BF16)
        after()

    outs = pl.pallas_call(
        body, name=name,
        grid=(t // tm, half),
        in_specs=[pl.BlockSpec((tm, d), lambda i, j: (i, 0)),
                  pl.BlockSpec((None, d, c), lambda i, j: (j, 0, 0)),
                  pl.BlockSpec((None, d, c), lambda i, j: (half + j, 0, 0))] + c_specs,
        out_specs=[pl.BlockSpec((2, tm, c), lambda i, j: (0, i, j)), pl.BlockSpec((tm, c), lambda i, j: (i, j))] + c_specs,
        out_shape=[jax.ShapeDtypeStruct((2, t, half * c), BF16), jax.ShapeDtypeStruct((t, half * c), BF16)] + c_shapes,
        scratch_shapes=c_scratch,
        compiler_params=_cparams(*(("arbitrary",) * 2 if nc else ("parallel", "parallel"))),
    )(h2, w_s, w_s, *c_arrays)
    return (outs[0], outs[1], list(outs[2:])) if nc else (outs[0], outs[1])


def _rowwise(fn, ins, row_outs, acc_outs, *, rows, tb, name, carry=None):
    in_specs, args = [], []
    for spec in ins:
        kind, arr = spec[0], spec[1]
        if kind == "row":
            in_specs.append(pl.BlockSpec((tb, arr.shape[1]), lambda i: (i, 0)))
        elif kind == "win":
            width, cb = spec[2], spec[3]
            in_specs.append(pl.BlockSpec((tb, width), functools.partial(lambda i, cb: (i, cb), cb=cb)))
        else:
            in_specs.append(pl.BlockSpec(arr.shape, lambda i: (0, 0)))
        args.append(arr)
    out_specs = [pl.BlockSpec((tb, c), lambda i: (i, 0)) for c, _ in row_outs]
    out_specs += [pl.BlockSpec(shape, lambda i: (0, 0)) for shape in acc_outs]
    out_shape = [jax.ShapeDtypeStruct((rows, c), dt) for c, dt in row_outs]
    out_shape += [jax.ShapeDtypeStruct(shape, F32) for shape in acc_outs]
    n_in, n_row, n_out = len(ins), len(row_outs), len(row_outs) + len(acc_outs)
    c_kind, c_arrays, c_specs, c_shapes, c_scratch = _carry_parts(carry)
    nc = len(c_arrays)

    def body(*refs):
        after = _carry_hooks(c_kind, refs[n_in:n_in + nc], refs[n_in + nc + n_out:n_in + 2 * nc + n_out],
                             refs[n_in + 2 * nc + n_out:], pl.program_id(0), rows // tb - 1)
        vals = [r[...] for r in refs[:n_in]]
        outs = fn(*vals)
        if not isinstance(outs, (tuple, list)):
            outs = (outs,)
        out_refs = refs[n_in + nc:n_in + nc + n_out]
        for o_ref, val in zip(out_refs[:n_row], outs[:n_row]):
            o_ref[...] = val.astype(o_ref.dtype)
        first = pl.program_id(0) == 0
        for o_ref, val in zip(out_refs[n_row:], outs[n_row:]):
            @pl.when(first)
            def _(o_ref=o_ref):
                o_ref[...] = jnp.zeros_like(o_ref)
            o_ref[...] += val
        after()

    res = pl.pallas_call(
        body, name=name,
        grid=(rows // tb,),
        in_specs=in_specs + c_specs, out_specs=out_specs + c_specs, out_shape=out_shape + c_shapes,
        scratch_shapes=c_scratch,
        compiler_params=_cparams("arbitrary"),
    )(*args, *c_arrays)
    return (list(res[:n_out]), list(res[n_out:])) if nc else res


def _rms_fwd(x, g, name, carry=None):
    def fn(xv, gv):
        r = lax.rsqrt(jnp.mean(xv * xv, axis=-1, keepdims=True) + NORM_EPS)
        return (xv * r * gv,)
    res = _rowwise(fn, [("row", x), ("full", g)], [(x.shape[1], BF16)], [], rows=x.shape[0], tb=_tile(x.shape[0], 512),
                   name=name, carry=carry)
    return (res[0][0], res[1]) if carry is not None else res[0]


def _rms_bwd(x, g, dh, dres, name, want_bf16):
    d = x.shape[1]

    def fn(xv, gv, dhv, drv):
        r = lax.rsqrt(jnp.mean(xv * xv, axis=-1, keepdims=True) + NORM_EPS)
        xhat = xv * r
        dhv = dhv.astype(F32)
        dxhat = dhv * gv
        dx = drv + r * (dxhat - xhat * jnp.mean(dxhat * xhat, axis=-1, keepdims=True))
        dg = jnp.sum(dhv * xhat, axis=0, keepdims=True)
        return (dx, dx, dg) if want_bf16 else (dx, dg)

    row_outs = [(d, F32), (d, BF16)] if want_bf16 else [(d, F32)]
    return _rowwise(fn, [("row", x), ("full", g), ("row", dh), ("row", dres)], row_outs, [(1, d)],
                    rows=x.shape[0], tb=_tile(x.shape[0], 256), name=name)


def _loss_head(x2, g, target, name):
    d = x2.shape[1]

    def fn(xv, gv, tv):
        r = lax.rsqrt(jnp.mean(xv * xv, axis=-1, keepdims=True) + NORM_EPS)
        xhat = xv * r
        diff = xhat * gv - tv
        loss = 0.5 * jnp.sum(jnp.mean(diff * diff, axis=-1, keepdims=True), axis=0, keepdims=True)
        dy = diff * (1.0 / d)
        dxhat = dy * gv
        dx = r * (dxhat - xhat * jnp.mean(dxhat * xhat, axis=-1, keepdims=True))
        dg = jnp.sum(dy * xhat, axis=0, keepdims=True)
        return dx, dx, dg, jnp.broadcast_to(loss, (1, LANE))

    return _rowwise(fn, [("row", x2), ("full", g), ("row", target)], [(d, F32), (d, BF16)], [(1, d), (1, LANE)],
                    rows=x2.shape[0], tb=_tile(x2.shape[0], 256), name=name)


def _swiglu_bwd_tile(dact, dswiglu):
    return ((dact * dswiglu[0].astype(F32), dact * dswiglu[1].astype(F32)),)


def _mix_fwd(proj, pa, pb, d, name):
    def fn(ga, gb, av, bv):
        return (_sigmoid(ga.astype(F32)) * av.astype(F32) + _sigmoid(gb.astype(F32)) * bv.astype(F32),)
    return _rowwise(fn, [("win", proj, d, 0), ("win", proj, d, 1), ("row", pa), ("row", pb)], [(d, BF16)], [],
                    rows=pa.shape[0], tb=_tile(pa.shape[0], 512), name=name)[0]


def _mix_bwd(proj, pa, pb, dmix, d, name):
    def fn(ga, gb, av, bv, dm):
        dm = dm.astype(F32)
        sa, sb = _sigmoid(ga.astype(F32)), _sigmoid(gb.astype(F32))
        av, bv = av.astype(F32), bv.astype(F32)
        return dm * sa, dm * sb, dm * av * sa * (1.0 - sa), dm * bv * sb * (1.0 - sb)
    return _rowwise(fn, [("win", proj, d, 0), ("win", proj, d, 1), ("row", pa), ("row", pb), ("row", dmix)],
                    [(d, BF16)] * 4, [], rows=pa.shape[0], tb=_tile(pa.shape[0], 512), name=name)


def _chunk_masks(tb):
    r = lax.broadcasted_iota(jnp.int32, (tb, tb), 0)
    c = lax.broadcasted_iota(jnp.int32, (tb, tb), 1)
    same = lax.shift_right_logical(r, GLA_CHUNK_SHIFT) == lax.shift_right_logical(c, GLA_CHUNK_SHIFT)
    return same, same & (c <= r), same & (r <= c)


def _mask_bf16(mask):
    return jnp.where(mask, 1.0, 0.0).astype(BF16)


def _split_dot(mask_bf, x, terms):
    acc, rem = None, x
    for _ in range(terms):
        hi = rem.astype(BF16)
        part = jnp.dot(mask_bf, hi, preferred_element_type=F32)
        acc = part if acc is None else acc + part
        rem = rem - hi.astype(F32)
    return acc


def _gla_decay(al, wa2, ba2, same_bf, causal_bf):
    z = jnp.dot(al.astype(BF16), wa2, preferred_element_type=F32) + ba2
    la = (jnp.minimum(z, 0.0) - jnp.log(1.0 + jnp.exp(-jnp.abs(z)))) * (1.0 / GLA_TAU)
    bc = _split_dot(causal_bf, la, 3)
    bl = _split_dot(same_bf, la, 3)
    return z, bc, bl


def _dot_t(a, b, ca, cb):
    return lax.dot_general(a, b, (((ca,), (cb,)), ((), ())), preferred_element_type=F32)


def _gla_fwd(proj, alow, wa2, ba2, ghn, *, dk, dv, name, carry=None):
    t = proj.shape[0]
    tb = min(GLA_BLOCK, t)
    nch = tb // GLA_CHUNK
    hk, hv = dk // GLA_HEADS, dv // GLA_HEADS
    scale = hk ** -0.5
    v_cb, r_cb = (8 * dk) // dv, (8 * dk) // dv + 1
    q_cb, k_cb = (8 * dk + 2 * dv) // dk, (8 * dk + 2 * dv) // dk + 1
    c_kind, c_arrays, c_specs, c_shapes, c_scratch = _carry_parts(carry)
    nc = len(c_arrays)

    def body(q_ref, k_ref, v_ref, r_ref, al_ref, wa2_ref, ba2_ref, ghn_ref, *refs):
        x_refs, (oa_ref, opre_ref, s_ref), out_refs = refs[:nc], refs[nc:nc + 3], refs[nc + 3:2 * nc + 3]
        st_scr, sems = refs[2 * nc + 3], refs[2 * nc + 4:]
        after = _carry_hooks(c_kind, x_refs, out_refs, sems, pl.program_id(0), t // tb - 1)

        @pl.when(pl.program_id(0) == 0)
        def _():
            st_scr[...] = jnp.zeros_like(st_scr)

        same, causal, _ = _chunk_masks(tb)
        same_bf, causal_bf = _mask_bf16(same), _mask_bf16(causal)
        _, bc, bl = _gla_decay(al_ref[...], wa2_ref[...], ba2_ref[...], same_bf, causal_bf)
        q = q_ref[...].astype(F32) * scale
        k = k_ref[...].astype(F32)
        qd = (q * jnp.exp(bc)).astype(BF16)
        ki = (k * jnp.exp(-bc)).astype(BF16)
        ks = (k * jnp.exp(bl - bc)).astype(BF16)
        dl = jnp.exp(bl)
        ksls = [slice(h * hk, (h + 1) * hk) for h in range(GLA_HEADS)]
        vsls = [slice(h * hv, (h + 1) * hv) for h in range(GLA_HEADS)]
        v_hs = [v_ref[:, vsl] for vsl in vsls]
        o_intras = []
        for ksl, v_h in zip(ksls, v_hs):
            sc = jnp.where(causal, _dot_t(qd[:, ksl], ki[:, ksl], 1, 1), 0.0)
            o_intras.append(jnp.dot(sc.astype(BF16), v_h, preferred_element_type=F32))
        for c in range(nch):
            rows = slice(c * GLA_CHUNK, (c + 1) * GLA_CHUNK)
            for h, (ksl, vsl) in enumerate(zip(ksls, vsls)):
                st = st_scr[h]
                s_ref[c, h] = st
                opre_ref[rows, vsl] = o_intras[h][rows] + _dot_t(qd[rows, ksl], st.astype(BF16), 1, 1)
                st_scr[h] = dl[c * GLA_CHUNK:c * GLA_CHUNK + 1, ksl] * st + _dot_t(v_hs[h][rows], ks[rows, ksl], 0, 0)
        for h in range(GLA_HEADS):
            vsl = slice(h * hv, (h + 1) * hv)
            o = opre_ref[:, vsl]
            rs = lax.rsqrt(jnp.mean(o * o, axis=-1, keepdims=True) + NORM_EPS)
            rv = r_ref[:, vsl].astype(F32)
            oa_ref[:, vsl] = (rv * _sigmoid(rv) * (o * rs * ghn_ref[:, vsl])).astype(BF16)
        after()

    nchunks = t // GLA_CHUNK
    outs = pl.pallas_call(
        body, name=name,
        grid=(t // tb,),
        in_specs=[
            pl.BlockSpec((tb, dk), lambda i: (i, q_cb)),
            pl.BlockSpec((tb, dk), lambda i: (i, k_cb)),
            pl.BlockSpec((tb, dv), lambda i: (i, v_cb)),
            pl.BlockSpec((tb, dv), lambda i: (i, r_cb)),
            pl.BlockSpec((tb, LANE), lambda i: (i, 0)),
            pl.BlockSpec(wa2.shape, lambda i: (0, 0)),
            pl.BlockSpec(ba2.shape, lambda i: (0, 0)),
            pl.BlockSpec(ghn.shape, lambda i: (0, 0)),
        ] + c_specs,
        out_specs=[
            pl.BlockSpec((tb, dv), lambda i: (i, 0)),
            pl.BlockSpec((tb, dv), lambda i: (i, 0)),
            pl.BlockSpec((nch, GLA_HEADS, hv, hk), lambda i: (i, 0, 0, 0)),
        ] + c_specs,
        out_shape=[
            jax.ShapeDtypeStruct((t, dv), BF16),
            jax.ShapeDtypeStruct((t, dv), F32),
            jax.ShapeDtypeStruct((nchunks, GLA_HEADS, hv, hk), F32),
        ] + c_shapes,
        scratch_shapes=[pltpu.VMEM((GLA_HEADS, hv, hk), F32)] + c_scratch,
        compiler_params=_cparams("arbitrary"),
    )(proj, proj, proj, proj, alow, wa2, ba2, ghn, *c_arrays)
    return (outs[0], outs[1], outs[2], list(outs[3:])) if nc else tuple(outs)


def _gla_bwd(proj, alow, wa2, ba2, ghn, opre, states, doa, *, dk, dv, name):
    t = proj.shape[0]
    tb = min(GLA_BLOCK, t)
    nb = t // tb
    nch = tb // GLA_CHUNK
    hk, hv = dk // GLA_HEADS, dv // GLA_HEADS
    scale = hk ** -0.5
    v_cb, r_cb = (8 * dk) // dv, (8 * dk) // dv + 1
    q_cb, k_cb = (8 * dk + 2 * dv) // dk, (8 * dk + 2 * dv) // dk + 1

    def body(q_ref, k_ref, v_ref, r_ref, al_ref, wa2_ref, ba2_ref, ghn_ref, opre_ref, s_ref, doa_ref,
             dq_ref, dk_ref, dv_ref, dr_ref, dal_ref, dwa2_ref, dba2_ref, dghn_ref,
             dst_scr, dqd_scr, dki_scr, dks_scr, ddl_scr):
        @pl.when(pl.program_id(0) == 0)
        def _():
            dst_scr[...] = jnp.zeros_like(dst_scr)
            dwa2_ref[...] = jnp.zeros_like(dwa2_ref)
            dba2_ref[...] = jnp.zeros_like(dba2_ref)
            dghn_ref[...] = jnp.zeros_like(dghn_ref)

        same, causal, anti = _chunk_masks(tb)
        same_bf, causal_bf, anti_bf = _mask_bf16(same), _mask_bf16(causal), _mask_bf16(anti)
        al = al_ref[...]
        wa2v = wa2_ref[...]
        z, bc, bl = _gla_decay(al, wa2v, ba2_ref[...], same_bf, causal_bf)
        e_bc, e_nbc, e_st = jnp.exp(bc), jnp.exp(-bc), jnp.exp(bl - bc)
        q = q_ref[...].astype(F32) * scale
        k = k_ref[...].astype(F32)
        qd_f, ki_f, ks_f = q * e_bc, k * e_nbc, k * e_st
        qd, ki, ks = qd_f.astype(BF16), ki_f.astype(BF16), ks_f.astype(BF16)
        dl = jnp.exp(bl)
        per_head = []
        for h in range(GLA_HEADS):
            ksl = slice(h * hk, (h + 1) * hk)
            vsl = slice(h * hv, (h + 1) * hv)
            o = opre_ref[:, vsl]
            rs = lax.rsqrt(jnp.mean(o * o, axis=-1, keepdims=True) + NORM_EPS)
            ohat = o * rs
            g_h = ghn_ref[:, vsl]
            rv = r_ref[:, vsl].astype(F32)
            sg = _sigmoid(rv)
            d_oa = doa_ref[:, vsl].astype(F32)
            don = d_oa * (rv * sg)
            dr_ref[:, vsl] = (d_oa * (ohat * g_h) * (sg * (1.0 + rv * (1.0 - sg)))).astype(BF16)
            dghn_ref[:, vsl] += jnp.sum(don * ohat, axis=0, keepdims=True)
            dohat = don * g_h
            do_f = rs * (dohat - ohat * jnp.mean(dohat * ohat, axis=-1, keepdims=True))
            do = do_f.astype(BF16)
            v_h = v_ref[:, vsl]
            p = jnp.where(causal, _dot_t(do, v_h, 1, 1), 0.0).astype(BF16)
            dqd_intra = jnp.dot(p, ki[:, ksl], preferred_element_type=F32)
            dki_scr[:, ksl] = _dot_t(p, qd[:, ksl], 0, 0)
            sc = jnp.where(causal, _dot_t(qd[:, ksl], ki[:, ksl], 1, 1), 0.0).astype(BF16)
            dv_intra = _dot_t(sc, do, 0, 0)
            per_head.append((ksl, vsl, v_h, do, dqd_intra, dv_intra))
        for c in reversed(range(nch)):
            rows = slice(c * GLA_CHUNK, (c + 1) * GLA_CHUNK)
            for h, (ksl, vsl, v_h, do, dqd_intra, dv_intra) in enumerate(per_head):
                dst = dst_scr[h]
                st = s_ref[c, h]
                dst_bf = dst.astype(BF16)
                dv_ref[rows, vsl] = (dv_intra[rows] + _dot_t(ks[rows, ksl], dst_bf, 1, 1)).astype(BF16)
                dks_scr[rows, ksl] = jnp.dot(v_h[rows], dst_bf, preferred_element_type=F32)
                dl_c = dl[c * GLA_CHUNK:c * GLA_CHUNK + 1, ksl]
                ddl = jnp.sum(dst * st, axis=0, keepdims=True) * dl_c
                ddl_scr[rows, ksl] = jnp.broadcast_to(ddl, (GLA_CHUNK, hk))
                dqd_scr[rows, ksl] = dqd_intra[rows] + jnp.dot(do[rows], st.astype(BF16), preferred_element_type=F32)
                dst_scr[h] = dl_c * dst + _dot_t(do[rows], qd[rows, ksl], 0, 0)
        dqd, dki, dks = dqd_scr[...], dki_scr[...], dks_scr[...]
        dq_ref[...] = (dqd * (scale * e_bc)).astype(BF16)
        dk_ref[...] = (dki * e_nbc + dks * e_st).astype(BF16)
        dks_ks = dks * ks_f
        dbc = dqd * qd_f - dki * ki_f - dks_ks
        dla = _split_dot(anti_bf, dbc, 2) + _split_dot(same_bf, dks_ks, 2) + ddl_scr[...]
        dz = (dla * (1.0 / GLA_TAU) * (1.0 - _sigmoid(z)))
        dz_bf = dz.astype(BF16)
        dal_ref[...] = _dot_t(dz_bf, wa2v, 1, 1).astype(BF16)
        dwa2_ref[...] += _dot_t(al.astype(BF16), dz_bf, 0, 0)
        dba2_ref[...] += jnp.sum(dz, axis=0, keepdims=True)

    rev = lambda i: nb - 1 - i
    return pl.pallas_call(
        body, name=name,
        grid=(nb,),
        in_specs=[
            pl.BlockSpec((tb, dk), lambda i: (rev(i), q_cb)),
            pl.BlockSpec((tb, dk), lambda i: (rev(i), k_cb)),
            pl.BlockSpec((tb, dv), lambda i: (rev(i), v_cb)),
            pl.BlockSpec((tb, dv), lambda i: (rev(i), r_cb)),
            pl.BlockSpec((tb, LANE), lambda i: (rev(i), 0)),
            pl.BlockSpec(wa2.shape, lambda i: (0, 0)),
            pl.BlockSpec(ba2.shape, lambda i: (0, 0)),
            pl.BlockSpec(ghn.shape, lambda i: (0, 0)),
            pl.BlockSpec((tb, dv), lambda i: (rev(i), 0)),
            pl.BlockSpec((nch, GLA_HEADS, hv, hk), lambda i: (rev(i), 0, 0, 0)),
            pl.BlockSpec((tb, dv), lambda i: (rev(i), 0)),
        ],
        out_specs=[
            pl.BlockSpec((tb, dk), lambda i: (rev(i), 0)),
            pl.BlockSpec((tb, dk), lambda i: (rev(i), 0)),
            pl.BlockSpec((tb, dv), lambda i: (rev(i), 0)),
            pl.BlockSpec((tb, dv), lambda i: (rev(i), 0)),
            pl.BlockSpec((tb, LANE), lambda i: (rev(i), 0)),
            pl.BlockSpec(wa2.shape, lambda i: (0, 0)),
            pl.BlockSpec(ba2.shape, lambda i: (0, 0)),
            pl.BlockSpec(ghn.shape, lambda i: (0, 0)),
        ],
        out_shape=[
            jax.ShapeDtypeStruct((t, dk), BF16),
            jax.ShapeDtypeStruct((t, dk), BF16),
            jax.ShapeDtypeStruct((t, dv), BF16),
            jax.ShapeDtypeStruct((t, dv), BF16),
            jax.ShapeDtypeStruct((t, LANE), BF16),
            jax.ShapeDtypeStruct(wa2.shape, F32),
            jax.ShapeDtypeStruct(ba2.shape, F32),
            jax.ShapeDtypeStruct(ghn.shape, F32),
        ],
        scratch_shapes=[pltpu.VMEM((GLA_HEADS, hv, hk), F32)] + [pltpu.VMEM((tb, dk), F32)] * 4,
        compiler_params=_cparams("arbitrary"),
    )(proj, proj, proj, proj, alow, wa2, ba2, ghn, opre, states, doa)


def _s5_tables(lam_re, lam_im, log_dt, b_re, b_im, c_re, c_im):
    hp = lax.Precision.HIGHEST
    g, p = lam_re.shape
    ln = S5_L
    dt = jnp.exp(log_dt)[:, None]
    lr, li = lam_re, lam_im
    mag = jnp.exp(lr * dt)
    ar, ai = mag * jnp.cos(li * dt), mag * jnp.sin(li * dt)
    den = lr * lr + li * li
    am1 = ar - 1.0
    f_re = ((am1 * lr + ai * li) / den)[..., None]
    f_im = ((ai * lr - am1 * li) / den)[..., None]
    bb_re = f_re * b_re - f_im * b_im
    bb_im = f_re * b_im + f_im * b_re
    j = jnp.arange(ln + 1, dtype=F32)[None, :, None]
    pm = jnp.exp(j * (lr * dt)[:, None, :])
    ang = j * (li * dt)[:, None, :]
    pw_re, pw_im = pm * jnp.cos(ang), pm * jnp.sin(ang)
    cp_re = c_re[:, None] * pw_re[:, :, None, :] - c_im[:, None] * pw_im[:, :, None, :]
    cp_im = c_re[:, None] * pw_im[:, :, None, :] + c_im[:, None] * pw_re[:, :, None, :]
    kj = (jnp.einsum("gjcp,gpd->gjcd", cp_re[:, :ln], bb_re, precision=hp)
          - jnp.einsum("gjcp,gpd->gjcd", cp_im[:, :ln], bb_im, precision=hp))
    eye = jnp.eye(S5_TILE_G, dtype=F32)
    nt = g // S5_TILE_G
    k8 = jnp.einsum("jglcd,gh->jlgdhc", kj.reshape(nt, S5_TILE_G, ln, S5_GC, S5_GC), eye).reshape(nt, ln, LANE, LANE)
    rp_re, rp_im = pw_re[:, ln - 1::-1], pw_im[:, ln - 1::-1]
    bbt_re, bbt_im = bb_re.transpose(0, 2, 1)[:, None], bb_im.transpose(0, 2, 1)[:, None]
    bst = jnp.stack([rp_re[:, :, None, :] * bbt_re - rp_im[:, :, None, :] * bbt_im,
                     rp_re[:, :, None, :] * bbt_im + rp_im[:, :, None, :] * bbt_re], axis=3)
    bc = bst.reshape(nt, S5_TILE_G, ln, S5_GC, 2 * p).transpose(0, 2, 1, 3, 4).reshape(nt, ln, LANE, 2 * p)
    cst = jnp.stack([cp_re[:, 1:], -cp_im[:, 1:]], axis=2)
    cc = cst.reshape(nt, S5_TILE_G, ln, 2, S5_GC, p).transpose(0, 2, 3, 5, 1, 4).reshape(nt, ln, 2 * p, LANE)
    a8 = jnp.stack([jnp.concatenate([pw_re[:, ln], pw_re[:, ln]], axis=-1),
                    jnp.concatenate([-pw_im[:, ln], pw_im[:, ln]], axis=-1)], axis=1)
    a8 = a8.reshape(nt, S5_TILE_G, 2, 2 * p).transpose(0, 2, 1, 3).reshape(nt, 2, S5_TILE_G * 2 * p)
    return k8, bc, cc, a8


def _swap_re_im(x):
    w = x.shape[1]
    if w == LANE:
        return pltpu.roll(x, LANE // 2, 1)
    first_half = (lax.broadcasted_iota(jnp.int32, x.shape, 1) & (LANE // 2)) == 0
    return jnp.where(first_half, pltpu.roll(x, w - LANE // 2, 1), pltpu.roll(x, LANE // 2, 1))


def _s5_expand(bc, cc, w):
    reps = w // LANE
    mask_b = (lax.broadcasted_iota(jnp.int32, (LANE, w), 0) // S5_GC) == (lax.broadcasted_iota(jnp.int32, (LANE, w), 1) // LANE)
    mask_c = (lax.broadcasted_iota(jnp.int32, (w, LANE), 0) // LANE) == (lax.broadcasted_iota(jnp.int32, (w, LANE), 1) // S5_GC)
    b8 = None if bc is None else jnp.where(mask_b, jnp.concatenate([bc] * reps, axis=1), jnp.zeros((), bc.dtype))
    c8 = None if cc is None else jnp.where(mask_c, jnp.concatenate([cc] * reps, axis=0), jnp.zeros((), cc.dtype))
    return b8, c8, mask_b, mask_c


def _state_scan(v, pr, pi, reverse):
    n = v.shape[0]
    row = lax.broadcasted_iota(jnp.int32, v.shape, 0)
    z, s = v, 1
    while s < n:
        if reverse:
            zs = jnp.where(row < n - s, pltpu.roll(z, n - s, 0), 0.0)
        else:
            zs = jnp.where(row >= s, pltpu.roll(z, s, 0), 0.0)
        z = z + zs * pr + _swap_re_im(zs) * pi
        pr, pi = pr * pr - pi * pi, 2.0 * pr * pi
        s *= 2
    return z


def _s5_fwd(proj, u_cb, k8, bc, cc, a8, name):
    t = proj.shape[0]
    nt, ln = bc.shape[:2]
    w = a8.shape[2]
    nc = t // ln
    rb = min(t, S5_ROW_BLOCK)

    def body(u_ref, k_ref, b_ref, c_ref, a_ref, y_ref, x_ref, uf_ref):
        uf_ref[...] = u_ref[...].astype(F32)
        pos = lax.broadcasted_iota(jnp.int32, (rb, LANE), 0) & (ln - 1)
        for r0 in range(0, t, rb):
            u = uf_ref[r0:r0 + rb, :]
            acc = jnp.dot(u.astype(BF16), k_ref[0], preferred_element_type=F32)
            for lag in range(1, ln):
                us = jnp.where(pos >= lag, pltpu.roll(u, lag, 0), 0.0).astype(BF16)
                acc = acc + jnp.dot(us, k_ref[lag], preferred_element_type=F32)
            y_ref[r0:r0 + rb, :] = acc
        v = None
        for s in range(ln):
            part = jnp.dot(uf_ref[pl.ds(s, nc, stride=ln), :].astype(BF16), _s5_expand(b_ref[s], None, w)[0],
                           preferred_element_type=F32)
            v = part if v is None else v + part
        z = _state_scan(v, a_ref[0:1, :], a_ref[1:2, :], reverse=False)
        row = lax.broadcasted_iota(jnp.int32, z.shape, 0)
        x = jnp.where(row >= 1, pltpu.roll(z, 1, 0), 0.0)
        x_ref[...] = x
        x_bf = x.astype(BF16)
        for tt in range(ln):
            y_ref[pl.ds(tt, nc, stride=ln), :] += jnp.dot(x_bf, _s5_expand(None, c_ref[tt], w)[1], preferred_element_type=F32)

    tile = lambda shape: pl.BlockSpec((None,) + shape, lambda j: (j,) + (0,) * len(shape))
    return pl.pallas_call(
        body, name=name, grid=(nt,),
        in_specs=[pl.BlockSpec((t, LANE), lambda j: (0, u_cb + j)), tile((ln, LANE, LANE)), tile(bc.shape[1:]),
                  tile(cc.shape[1:]), tile((2, w))],
        out_specs=[pl.BlockSpec((t, LANE), lambda j: (0, j)), tile((nc, w))],
        out_shape=[jax.ShapeDtypeStruct((t, nt * LANE), F32), jax.ShapeDtypeStruct((nt, nc, w), F32)],
        scratch_shapes=[pltpu.VMEM((t, LANE), F32)],
        compiler_params=_cparams("parallel"),
    )(proj, k8, bc, cc, a8)


def _s5_bwd_data(dy, x_st, k8, bc, cc, a8, name):
    t = dy.shape[0]
    nt, ln = bc.shape[:2]
    w = a8.shape[2]
    nc = t // ln
    rb = min(t, S5_ROW_BLOCK)

    def body(dy_ref, x_ref, k_ref, b_ref, c_ref, a_ref, du_ref, dv_ref, da_ref, dyf_ref, duf_ref):
        dyf_ref[...] = dy_ref[...].astype(F32)
        pos = lax.broadcasted_iota(jnp.int32, (rb, LANE), 0) & (ln - 1)
        for r0 in range(0, t, rb):
            g = dyf_ref[r0:r0 + rb, :]
            acc = _dot_t(g.astype(BF16), k_ref[0], 1, 1)
            for lag in range(1, ln):
                gs = jnp.where(pos < ln - lag, pltpu.roll(g, rb - lag, 0), 0.0).astype(BF16)
                acc = acc + _dot_t(gs, k_ref[lag], 1, 1)
            duf_ref[r0:r0 + rb, :] = acc
        gx = None
        for tt in range(ln):
            part = _dot_t(dyf_ref[pl.ds(tt, nc, stride=ln), :].astype(BF16), _s5_expand(None, c_ref[tt], w)[1], 1, 1)
            gx = part if gx is None else gx + part
        rtot = _state_scan(gx, a_ref[0:1, :], -a_ref[1:2, :], reverse=True)
        row = lax.broadcasted_iota(jnp.int32, rtot.shape, 0)
        dv = jnp.where(row < nc - 1, pltpu.roll(rtot, nc - 1, 0), 0.0)
        dv_ref[...] = dv
        dv_bf = dv.astype(BF16)
        for s in range(ln):
            duf_ref[pl.ds(s, nc, stride=ln), :] += _dot_t(dv_bf, _s5_expand(b_ref[s], None, w)[0], 1, 1)
        du_ref[...] = duf_ref[...].astype(BF16)
        x = x_ref[...]
        da_ref[0:1, :] = jnp.sum(dv * x, axis=0, keepdims=True)
        da_ref[1:2, :] = jnp.sum(dv * _swap_re_im(x), axis=0, keepdims=True)

    tile = lambda shape: pl.BlockSpec((None,) + shape, lambda j: (j,) + (0,) * len(shape))
    return pl.pallas_call(
        body, name=name, grid=(nt,),
        in_specs=[pl.BlockSpec((t, LANE), lambda j: (0, j)), tile((nc, w)), tile((ln, LANE, LANE)), tile(bc.shape[1:]),
                  tile(cc.shape[1:]), tile((2, w))],
        out_specs=[pl.BlockSpec((t, LANE), lambda j: (0, j)), tile((nc, w)), tile((2, w))],
        out_shape=[jax.ShapeDtypeStruct((t, nt * LANE), BF16), jax.ShapeDtypeStruct((nt, nc, w), F32),
                   jax.ShapeDtypeStruct((nt, 2, w), F32)],
        scratch_shapes=[pltpu.VMEM((t, LANE), F32), pltpu.VMEM((t, LANE), F32)],
        compiler_params=_cparams("parallel"),
    )(dy, x_st, k8, bc, cc, a8)


def _s5_bwd_tables(dy, proj, u_cb, x_st, dv, ln, name):
    t = dy.shape[0]
    nt, nc, w = x_st.shape
    rb = min(t, S5_ROW_BLOCK)

    def body(dy_ref, u_ref, x_ref, dv_ref, dk_ref, db_ref, dc_ref, dyf_ref, uf_ref):
        s = pl.program_id(1)

        @pl.when(s == 0)
        def _():
            dyf_ref[...] = dy_ref[...].astype(F32)
            uf_ref[...] = u_ref[...].astype(F32)
            pos = lax.broadcasted_iota(jnp.int32, (rb, LANE), 0) & (ln - 1)
            for r0 in range(0, t, rb):
                u, g_bf = uf_ref[r0:r0 + rb, :], dy_ref[r0:r0 + rb, :]
                for lag in range(ln):
                    us = u if lag == 0 else jnp.where(pos >= lag, pltpu.roll(u, lag, 0), 0.0)
                    part = _dot_t(us.astype(BF16), g_bf, 0, 0)
                    if r0 == 0:
                        dk_ref[lag] = part
                    else:
                        dk_ref[lag] += part

        rows = pl.ds(s, nc, stride=ln)
        _, _, mask_b, mask_c = _s5_expand(None, None, w)
        db = jnp.where(mask_b, _dot_t(uf_ref[rows, :].astype(BF16), dv_ref[...].astype(BF16), 0, 0), 0.0)
        dc = jnp.where(mask_c, _dot_t(x_ref[...].astype(BF16), dyf_ref[rows, :].astype(BF16), 0, 0), 0.0)
        db_ref[...] = sum(db[:, h * LANE:(h + 1) * LANE] for h in range(w // LANE))
        dc_ref[...] = sum(dc[h * LANE:(h + 1) * LANE, :] for h in range(w // LANE))

    tile = lambda shape: pl.BlockSpec((None,) + shape, lambda j, s: (j,) + (0,) * len(shape))
    per_s = lambda shape: pl.BlockSpec((None, None) + shape, lambda j, s: (j, s, 0, 0))
    return pl.pallas_call(
        body, name=name, grid=(nt, ln),
        in_specs=[pl.BlockSpec((t, LANE), lambda j, s: (0, j)), pl.BlockSpec((t, LANE), lambda j, s: (0, u_cb + j)),
                  tile((nc, w)), tile((nc, w))],
        out_specs=[tile((ln, LANE, LANE)), per_s((LANE, LANE)), per_s((LANE, LANE))],
        out_shape=[jax.ShapeDtypeStruct((nt, ln, LANE, LANE), F32)] * 3,
        scratch_shapes=[pltpu.VMEM((t, LANE), F32), pltpu.VMEM((t, LANE), F32)],
        compiler_params=_cparams("parallel", "arbitrary"),
    )(dy, proj, x_st, dv)


def _gelu_parts(y):
    inner = GELU_C * (y + GELU_A * y * y * y)
    th = jnp.tanh(inner)
    return th, 0.5 * y * (1.0 + th)


def _s5_post_fwd(y_raw, proj, u_cb, s5d, wglu, bglu, name):
    w = y_raw.shape[1]

    def fn(yr, u, dsk, wg, bg):
        y = yr + dsk * u.astype(F32)
        _, h = _gelu_parts(y)
        gl = jnp.dot(h.astype(BF16), wg, preferred_element_type=F32) + bg
        return (h * _sigmoid(gl),)

    return _rowwise(fn, [("row", y_raw), ("win", proj, w, u_cb), ("full", s5d), ("full", wglu), ("full", bglu)],
                    [(w, BF16)], [], rows=y_raw.shape[0], tb=_tile(y_raw.shape[0], 512), name=name)[0]


def _s5_post_bwd(y_raw, proj, u_cb, s5d, wglu, bglu, dob, name):
    w = y_raw.shape[1]

    def fn(yr, u, dsk, wg, bg, dov):
        u = u.astype(F32)
        dov = dov.astype(F32)
        y = yr + dsk * u
        th, h = _gelu_parts(y)
        h_bf = h.astype(BF16)
        gl = jnp.dot(h_bf, wg, preferred_element_type=F32) + bg
        sg = _sigmoid(gl)
        dgl = dov * h * sg * (1.0 - sg)
        dgl_bf = dgl.astype(BF16)
        dh = dov * sg + _dot_t(dgl_bf, wg, 1, 1)
        dgelu = 0.5 * (1.0 + th) + 0.5 * y * (1.0 - th * th) * GELU_C * (1.0 + 3.0 * GELU_A * y * y)
        dy = dh * dgelu
        return (dy, dy * dsk,
                _dot_t(h_bf, dgl_bf, 0, 0), jnp.sum(dgl, axis=0, keepdims=True), jnp.sum(dy * u, axis=0, keepdims=True))

    return _rowwise(fn, [("row", y_raw), ("win", proj, w, u_cb), ("full", s5d), ("full", wglu), ("full", bglu), ("row", dob)],
                    [(w, BF16), (w, BF16)], [(w, w), (1, w), (1, w)], rows=y_raw.shape[0], tb=_tile(y_raw.shape[0], 512), name=name)


def _adamw(w, g, m, v, name):
    _, rows, cols = w.shape
    tr, tc = (_tile(rows, 256, align=16), cols) if rows % 16 == 0 else (rows, _tile(cols, 256))
    slots = isinstance(g, (list, tuple))
    gs = list(g) if slots else [g]
    c1 = 1.0 - ADAM_B1 ** ADAM_STEP
    c2 = 1.0 - ADAM_B2 ** ADAM_STEP

    def body(w_ref, m_ref, v_ref, *refs):
        g_refs, out_refs = refs[:len(gs)], refs[len(gs):]
        if slots:
            parts = [g_ref[s].astype(F32) for g_ref in g_refs for s in range(g_ref.shape[0])]
            gv = parts[0]
            for p in parts[1:]:
                gv = gv + p
            out_refs[0][...] = gv
        else:
            gv = g_refs[0][...]
        d_ref, nm_ref, nv_ref = out_refs[-3:]
        nm = ADAM_B1 * m_ref[...] + (1.0 - ADAM_B1) * gv
        nv = ADAM_B2 * v_ref[...] + (1.0 - ADAM_B2) * (gv * gv)
        d_ref[...] = -ADAM_LR * ((nm / c1) / (jnp.sqrt(nv / c2) + ADAM_EPS) + ADAM_WD * w_ref[...])
        nm_ref[...] = nm
        nv_ref[...] = nv

    spec = pl.BlockSpec((None, tr, tc), lambda i, j: (0, i, j))
    g_specs = [pl.BlockSpec((a.shape[0], tr, tc), lambda i, j: (0, i, j)) for a in gs] if slots else [pl.BlockSpec((tr, tc), lambda i, j: (i, j))]
    n_out = 4 if slots else 3
    return pl.pallas_call(
        body, name=name, grid=(rows // tr, cols // tc),
        in_specs=[spec, spec, spec] + g_specs, out_specs=[spec] * n_out,
        out_shape=[jax.ShapeDtypeStruct((1, rows, cols), F32)] * n_out,
        compiler_params=_cparams("parallel", "parallel"),
    )(w, m, v, *gs)


def _slot_sum(x, name):
    _, rows, cols = x.shape
    if rows % 8 == 0:
        tr, tc = _tile(rows, 512, align=8), cols
    else:
        tr, tc = rows, _tile(cols, 256)

    def body(x_ref, o_ref):
        acc = x_ref[0].astype(F32)
        for s in range(1, N_DEV):
            acc = acc + x_ref[s].astype(F32)
        o_ref[...] = acc

    return pl.pallas_call(
        body, name=name, grid=(rows // tr, cols // tc),
        in_specs=[pl.BlockSpec((N_DEV, tr, tc), lambda i, j: (0, i, j))],
        out_specs=pl.BlockSpec((tr, tc), lambda i, j: (i, j)),
        out_shape=jax.ShapeDtypeStruct((rows, cols), F32),
        compiler_params=_cparams("parallel", "parallel"),
    )(x)


_REST = (("w_a2", 1), ("w_glu", 0), ("w_branch_a", 1), ("w_branch_b", 1), ("w_out", 0), ("w_ffn_in", 1), ("w_ffn_out", 0))
_SMALL = ("norm1_g", "b_a2", "gla_norm_g", "lam_re", "lam_im", "log_dt", "s5_b_re", "s5_b_im", "s5_c_re", "s5_c_im",
          "s5_d", "b_glu", "norm2_g", "final_norm_g")
_ORDER = ("norm1_g", "w_in", "w_a2", "b_a2", "gla_norm_g", "lam_re", "lam_im", "log_dt", "s5_b_re", "s5_b_im", "s5_c_re",
          "s5_c_im", "s5_d", "w_glu", "b_glu", "w_branch_a", "w_branch_b", "w_out", "norm2_g", "w_ffn_in", "w_ffn_out", "final_norm_g")


def _join_slots(slots, axis):
    _, r, c = slots.shape
    if axis == 0:
        return slots.reshape(N_DEV * r, c)
    return slots.transpose(1, 0, 2).reshape(r, N_DEV * c)


def _to_slots(full, axis):
    r, c = full.shape
    if axis == 0:
        return full.reshape(N_DEV, r // N_DEV, c)
    return full.reshape(r, N_DEV, c // N_DEV).transpose(1, 0, 2)


def _local_step(x, target, w_in_t, small, rest):
    t, d = x.shape
    dk, dv, s5w = d // 4, d // 2, d // 4
    dist = not isinstance(rest, dict)
    if dist:
        h1, (w_in_slots,) = _rms_fwd(x, small["norm1_g"], "norm1_fwd", carry=("ag", [w_in_t]))
        w_in_t = w_in_slots.reshape(-1, d)
    else:
        h1 = _rms_fwd(x, small["norm1_g"], "norm1_fwd")
    o_q, o_k, o_v, o_r, o_al = 0, dk, 2 * dk, 2 * dk + dv, 2 * dk + 2 * dv
    o_u = o_al + GLA_RANK
    o_ga, o_gb = o_u + s5w, o_u + s5w + d
    rows = lambda a, o, n: a[o:o + n]
    w_al_t = jnp.pad(rows(w_in_t, o_al, GLA_RANK), ((0, LANE - GLA_RANK), (0, 0)))
    w_ext_t = jnp.concatenate([rows(w_in_t, o_ga, d), rows(w_in_t, o_gb, d), rows(w_in_t, o_v, dv), rows(w_in_t, o_r, dv),
                               rows(w_in_t, o_q, dk), rows(w_in_t, o_k, dk), rows(w_in_t, o_u, s5w), w_al_t], axis=0)
    n_main = 2 * d + 2 * dv + 2 * dk + s5w
    u_cb = (2 * d + 2 * dv + 2 * dk) // s5w

    if dist:
        proj, (a2_s, glu_s, w_ffn_in_s) = _mm(h1, w_ext_t, tb=True, n_limit=n_main, out_dtype=BF16,
                                              carry=("ag", [rest[0], rest[1], rest[5]]), name="in_proj")
        w = {"w_a2": _join_slots(a2_s, 1), "w_glu": _join_slots(glu_s, 0)}
    else:
        proj = _mm(h1, w_ext_t, tb=True, n_limit=n_main, out_dtype=BF16, name="in_proj")
        w = rest
        w_ffn_in_s = _to_slots(rest["w_ffn_in"], 1)
    wa2 = jnp.pad(w["w_a2"], ((0, LANE - GLA_RANK), (0, 0)))
    alow = _mm(h1, w_al_t, tb=True, out_dtype=BF16, name="in_proj_gate_rank")
    if dist:
        o_a, o_pre, states, got = _gla_fwd(proj, alow, wa2, small["b_a2"], small["gla_norm_g"], dk=dk, dv=dv, name="gla_fwd",
                                           carry=("ag", rest[2:5]))
        w.update({n: _join_slots(g, ax) for (n, ax), g in zip(_REST[2:5], got)})
    else:
        o_a, o_pre, states = _gla_fwd(proj, alow, wa2, small["b_a2"], small["gla_norm_g"], dk=dk, dv=dv, name="gla_fwd")

    s5_params = (small["lam_re"], small["lam_im"], small["log_dt"][0], small["s5_b_re"], small["s5_b_im"],
                 small["s5_c_re"], small["s5_c_im"])
    (k8, bc, cc, a8), tables_vjp = jax.vjp(_s5_tables, *s5_params)
    k8_bf, b8_bf, c8_bf = k8.astype(BF16), bc.astype(BF16), cc.astype(BF16)
    u_lane_cb = u_cb * s5w // LANE
    y_raw, x_st = _s5_fwd(proj, u_lane_cb, k8_bf, b8_bf, c8_bf, a8, "s5_scan_fwd")
    o_b = _s5_post_fwd(y_raw, proj, u_cb, small["s5_d"], w["w_glu"], small["b_glu"], "s5_post_fwd")

    pa = _mm(o_a, w["w_branch_a"], out_dtype=BF16, name="branch_a")
    pb = _mm(o_b, w["w_branch_b"], out_dtype=BF16, name="branch_b")
    mix = _mix_fwd(proj, pa, pb, d, "mix_fwd")
    x1 = _mm(mix, w["w_out"], res=x, name="out_proj")
    h2 = _rms_fwd(x1, small["norm2_g"], "norm2_fwd")
    if dist:
        gu, act, (w_ffn_out_s,) = _ffn_in_fused(h2, w_ffn_in_s, carry=("ag", rest[-1:]), name="ffn_in")
        w_ffn_out = _join_slots(w_ffn_out_s, 0)
    else:
        gu, act = _ffn_in_fused(h2, w_ffn_in_s, name="ffn_in")
        w_ffn_out = rest["w_ffn_out"]
    x2 = _mm(act, w_ffn_out, res=x1, name="ffn_out")
    dx2, dx2_bf, d_final_g, loss = _loss_head(x2, small["final_norm_g"], target, "loss_head")

    recv = {}
    dgu, = _mm(dx2_bf, w_ffn_out, tb=True, epi=(_swiglu_bwd_tile, [gu], [(2, BF16)]), name="d_act")
    g_ffn_out = _mm(act, dx2_bf, ta=True, out_dtype=BF16, name="g_w_ffn_out")
    if dist:
        g_ffn_in_s, recv["w_ffn_out"] = _mm(h2, dgu, ta=True, b_slots=True, out_dtype=BF16, out_slots=N_DEV, tm_cap=512,
                                            carry=("a2a", [_to_slots(g_ffn_out, 0)]), name="g_w_ffn_in")
        dh2, (recv_ffn_in,) = _mm(dgu, w_ffn_in_s, tb=True, a_slots=True, b_slots=True, b_group=4, tm_cap=512,
                                  out_dtype=BF16, carry=("a2a", [g_ffn_in_s], [_ALL_K[:-1]]), name="d_h2")
    else:
        g_ffn_in_s = _mm(h2, dgu, ta=True, b_slots=True, out_dtype=BF16, out_slots=N_DEV, tm_cap=512, name="g_w_ffn_in")
        dh2 = _mm(dgu, w_ffn_in_s, tb=True, a_slots=True, b_slots=True, b_group=4, tm_cap=512, out_dtype=BF16, name="d_h2")
    dx1, dx1_bf, d_norm2_g = _rms_bwd(x1, small["norm2_g"], dh2, dx2, "norm2_bwd", True)
    dmix = _mm(dx1_bf, w["w_out"], tb=True, out_dtype=BF16, name="d_mix")
    g_out = _mm(mix, dx1_bf, ta=True, out_dtype=BF16, name="g_w_out")
    dpa, dpb, dga, dgb = _mix_bwd(proj, pa, pb, dmix, d, "mix_bwd")
    doa = _mm(dpa, w["w_branch_a"], tb=True, out_dtype=BF16, name="d_o_a")
    dob = _mm(dpb, w["w_branch_b"], tb=True, out_dtype=BF16, name="d_o_b")
    g_branch_a = _mm(o_a, dpa, ta=True, out_dtype=BF16, name="g_w_branch_a")
    g_branch_b = _mm(o_b, dpb, ta=True, out_dtype=BF16, name="g_w_branch_b")

    dy_s5, du_direct, g_glu, g_bglu, g_s5d = _s5_post_bwd(y_raw, proj, u_cb, small["s5_d"], w["w_glu"], small["b_glu"], dob, "s5_post_bwd")
    du_scan, dv_st, d_a8 = _s5_bwd_data(dy_s5, x_st, k8_bf, b8_bf, c8_bf, a8, "s5_scan_bwd")
    d_k8, d_b8, d_c8 = _s5_bwd_tables(dy_s5, proj, u_lane_cb, x_st, dv_st, S5_L, "s5_scan_bwd_tables")
    g_lam_re, g_lam_im, g_log_dt, g_b_re, g_b_im, g_c_re, g_c_im = tables_vjp((d_k8, d_b8, d_c8, d_a8))
    du = du_scan + du_direct

    dq, dkk, dvv, dr, dal, g_wa2, g_ba2, g_ghn = _gla_bwd(proj, alow, wa2, small["b_a2"], small["gla_norm_g"], o_pre, states, doa,
                                                        dk=dk, dv=dv, name="gla_bwd")
    dproj = jnp.concatenate([dga, dgb, dvv, dr, dq, dkk, du, dal], axis=1)
    mid = {"w_out": g_out, "w_branch_a": g_branch_a, "w_branch_b": g_branch_b, "w_glu": g_glu.astype(BF16),
           "w_a2": g_wa2[:GLA_RANK].astype(BF16)}
    if dist:
        axes = dict(_REST)
        g_main_t, got = _mm(dproj, h1, ta=True, m_limit=n_main, out_dtype=BF16, name="g_w_in_main",
                            carry=("a2a", [_to_slots(mid[n], axes[n]) for n in mid] + [g_ffn_in_s],
                                   [_ALL_K] * len(mid) + [_ALL_K[-1:]]))
        recv.update(zip(mid, [[g] for g in got[:-1]]))
        recv["w_ffn_in"] = [recv_ffn_in, got[-1]]
    else:
        g_main_t = _mm(dproj, h1, ta=True, m_limit=n_main, out_dtype=BF16, name="g_w_in_main")
    g_al_t = _mm(dal, h1, ta=True, out_dtype=BF16, name="g_w_in_gate_rank")
    mrows = lambda o, n: g_main_t[o:o + n]
    g_w_in_t = jnp.concatenate([mrows(2 * d + 2 * dv, dk), mrows(2 * d + 2 * dv + dk, dk), mrows(2 * d, dv), mrows(2 * d + dv, dv),
                                g_al_t[:GLA_RANK], mrows(2 * d + 2 * dv + 2 * dk, s5w), mrows(0, d), mrows(d, d)], axis=0)
    if dist:
        dh1, recv["w_in"] = _mm(dproj, w_ext_t, out_dtype=BF16, carry=("a2a", [_to_slots(g_w_in_t, 0)]), name="d_h1")
    else:
        dh1 = _mm(dproj, w_ext_t, out_dtype=BF16, name="d_h1")
    grad_x, d_norm1_g = _rms_bwd(x, small["norm1_g"], dh1, dx1, "norm1_bwd", False)

    small_g = {
        "norm1_g": d_norm1_g, "b_a2": g_ba2, "gla_norm_g": g_ghn, "lam_re": g_lam_re, "lam_im": g_lam_im,
        "log_dt": g_log_dt[None], "s5_b_re": g_b_re, "s5_b_im": g_b_im, "s5_c_re": g_c_re, "s5_c_im": g_c_im,
        "s5_d": g_s5d, "b_glu": g_bglu, "norm2_g": d_norm2_g, "final_norm_g": d_final_g,
    }
    if not dist:
        recv = dict(mid, w_in=g_w_in_t, w_ffn_in=_join_slots(g_ffn_in_s, 1), w_ffn_out=g_ffn_out)
    return loss[0, 0], grad_x, recv, small_g


def _small_2d(name, a):
    a = a[0]
    return a[None] if a.ndim == 1 else a


def kernel(x, norm1_g, w_in, w_a2, b_a2, gla_norm_g, lam_re, lam_im, log_dt, s5_b_re, s5_b_im, s5_c_re, s5_c_im, s5_d, w_glu, b_glu, w_branch_a, w_branch_b, w_out, norm2_g, w_ffn_in, w_ffn_out, final_norm_g, loss_target, m_norm1_g, m_w_in, m_w_a2, m_b_a2, m_gla_norm_g, m_lam_re, m_lam_im, m_log_dt, m_s5_b_re, m_s5_b_im, m_s5_c_re, m_s5_c_im, m_s5_d, m_w_glu, m_b_glu, m_w_branch_a, m_w_branch_b, m_w_out, m_norm2_g, m_w_ffn_in, m_w_ffn_out, m_final_norm_g, v_norm1_g, v_w_in, v_w_a2, v_b_a2, v_gla_norm_g, v_lam_re, v_lam_im, v_log_dt, v_s5_b_re, v_s5_b_im, v_s5_c_re, v_s5_c_im, v_s5_d, v_w_glu, v_b_glu, v_w_branch_a, v_w_branch_b, v_w_out, v_norm2_g, v_w_ffn_in, v_w_ffn_out, v_final_norm_g):
    args = dict(locals())
    weights = {n: args[n] for n in _ORDER}
    m_in = {n: args["m_" + n] for n in _ORDER}
    v_in = {n: args["v_" + n] for n in _ORDER}
    transposed = lambda a: a[0].T[None]
    rest = [weights[n][0].astype(BF16) for n, _ in _REST]
    small = {n: _small_2d(n, weights[n]) for n in _SMALL}
    loss_local, grad_x, recv, small_g = _local_step(x[0], loss_target[0], transposed(weights["w_in"])[0].astype(BF16), small, rest)

    grads, delta, new_m, new_v = {}, {}, {}, {}
    for n, _ in _REST:
        grads[n], delta[n], new_m[n], new_v[n] = _adamw(weights[n], recv[n], m_in[n], v_in[n], "adamw_" + n)
    w_in_out = _adamw(transposed(weights["w_in"]), recv["w_in"], transposed(m_in["w_in"]), transposed(v_in["w_in"]), "adamw_w_in")
    grads["w_in"], delta["w_in"], new_m["w_in"], new_v["w_in"] = (transposed(a) for a in w_in_out)

    s_sizes = [small_g[n].size for n in _SMALL]
    s_offs = [sum(s_sizes[:i]) for i in range(len(s_sizes))]
    s_total = sum(s_sizes)
    s_rows = -(-(-(-(s_total + 1) // LANE)) // LANE) * LANE

    def pack_small(parts):
        flat = jnp.concatenate([p.reshape(-1) for p in parts])
        return jnp.pad(flat, (0, s_rows * LANE - flat.size)).reshape(s_rows, LANE)

    s_flat = pack_small([small_g[n] for n in _SMALL] + [loss_local])
    s_red = _slot_sum(_exchange("ag", [s_flat], "small_grads_all_gather")[0], "small_grads_slot_sum")
    loss = s_red.reshape(-1)[s_total]
    sd, sm, sv = _adamw(pack_small([weights[n] for n in _SMALL])[None], s_red, pack_small([m_in[n] for n in _SMALL])[None],
                        pack_small([v_in[n] for n in _SMALL])[None], "adamw_small")
    sd, sm, sv = sd[0], sm[0], sv[0]
    for n, o, s in zip(_SMALL, s_offs, s_sizes):
        shape = weights[n].shape[1:]
        grads[n], delta[n], new_m[n], new_v[n] = (a.reshape(-1)[o:o + s].reshape(shape) for a in (s_red, sd, sm, sv))

    out = [loss, grad_x[None]]
    for tree in (grads, delta, new_m, new_v):
        out += [tree[n].reshape(weights[n].shape) for n in _ORDER]
    return tuple(out)
```

```python
import functools
import math

import jax
import jax.numpy as jnp
from jax import lax
from jax.experimental import pallas as pl
from jax.experimental.pallas import tpu as pltpu

F32 = jnp.float32
BF16 = jnp.bfloat16

NORM_EPS = 1e-6
N_DEV = 8
N_PEER = N_DEV - 1
GLA_HEADS = 4
GLA_CHUNK = 32
GLA_CHUNK_SHIFT = 5
GLA_TAU = 16.0
GLA_RANK = 16
GLA_BLOCK = 256
S5_GC = 16
S5_P = 64
S5_L = 16
S5_TILE_G = 8
S5_ROW_BLOCK = 2048
LANE = 128
V7X_VMEM_LIMIT = 56 * 1024 * 1024
V7X_MM_VMEM_BUDGET = 40 * 1024 * 1024
V7X_MM_TILE_MN = 1408
V7X_MM_TILE_MN_WHOLE_K = 512
V7X_MM_TILE_K = 2048

ADAM_LR = 0.001
ADAM_B1 = 0.9
ADAM_B2 = 0.999
ADAM_EPS = 1e-08
ADAM_WD = 0.01
ADAM_STEP = 10

GELU_C = math.sqrt(2.0 / math.pi)
GELU_A = 0.044715

MESH = pl.DeviceIdType.MESH


def _cparams(*sem):
    return pltpu.CompilerParams(dimension_semantics=sem, vmem_limit_bytes=V7X_VMEM_LIMIT)


def _divisors_down(n, start, align=LANE):
    t = (min(start, n) // align) * align
    found = False
    while t >= align:
        if n % t == 0:
            found = True
            yield t
        t -= align
    if not found:
        yield n


def _tile(n, target, align=LANE):
    return next(_divisors_down(n, target, align))


def _sigmoid(x):
    return 1.0 / (1.0 + jnp.exp(-x))


_HBM_SPEC = pl.BlockSpec(memory_space=pltpu.HBM)


def _exchange_scratch(n):
    return [pltpu.SemaphoreType.DMA((n * N_PEER,)), pltpu.SemaphoreType.DMA((n * N_PEER,)), pltpu.SemaphoreType.DMA((n,))]


def _ag_phases(x_refs, out_refs, send_sems, recv_sems, local_sems):
    n = len(x_refs)
    x, y, c = lax.axis_index("x"), lax.axis_index("y"), lax.axis_index("c")
    me, sibling = (x, y, c), (x, y, 1 - c)
    chips = [(1 - x, y), (x, 1 - y), (1 - x, 1 - y)]

    def copy(a, k, block, to, from_input=False):
        dst = out_refs[a].at[4 * block[0] + 2 * block[1] + block[2]]
        return pltpu.make_async_remote_copy(
            src_ref=x_refs[a] if from_input else dst, dst_ref=dst,
            send_sem=send_sems.at[a * N_PEER + k], recv_sem=recv_sems.at[a * N_PEER + k], device_id=to, device_id_type=MESH)

    def local(a):
        return pltpu.make_async_copy(x_refs[a], out_refs[a].at[4 * x + 2 * y + c], local_sems.at[a])

    def first(a):
        return [copy(a, 0, me, sibling, True)] + [copy(a, 1 + j, me, (*chip, c), True) for j, chip in enumerate(chips)]

    def start():
        for a in range(n):
            local(a).start()
            for cp in first(a):
                cp.start()

    def relay():
        for j, chip in enumerate(chips):
            for a in range(n):
                copy(a, 1 + j, (*chip, c), me).wait_recv()
                copy(a, 4 + j, (*chip, c), sibling).start()

    def finish():
        for a in range(n):
            copy(a, 0, sibling, me).wait_recv()
            for j, chip in enumerate(chips):
                copy(a, 4 + j, (*chip, 1 - c), me).wait_recv()
        for a in range(n):
            for cp in first(a) + [copy(a, 4 + j, (*chip, c), sibling) for j, chip in enumerate(chips)]:
                cp.wait_send()
            local(a).wait()

    return start, relay, finish


_ALL_K = tuple(range(N_DEV))


def _a2a_phases(x_refs, out_refs, send_sems, recv_sems, local_sems, ks_list=None):
    n = len(x_refs)
    ks_list = ks_list or [_ALL_K] * n
    x, y, c = lax.axis_index("x"), lax.axis_index("y"), lax.axis_index("c")
    my = 4 * x + 2 * y + c

    def copy(a, k):
        px, py, pc = (1 - x if k & 4 else x), (1 - y if k & 2 else y), (1 - c if k & 1 else c)
        return pltpu.make_async_remote_copy(
            src_ref=x_refs[a].at[4 * px + 2 * py + pc], dst_ref=out_refs[a].at[ks_list[a].index(k)],
            send_sem=send_sems.at[a * N_PEER + k - 1], recv_sem=recv_sems.at[a * N_PEER + k - 1],
            device_id=(px, py, pc), device_id_type=MESH)

    def local(a):
        return pltpu.make_async_copy(x_refs[a].at[my], out_refs[a].at[ks_list[a].index(0)], local_sems.at[a])

    def start():
        for a in range(n):
            for k in ks_list[a]:
                (copy(a, k) if k else local(a)).start()

    def relay():
        pass

    def finish():
        for a in range(n):
            for k in ks_list[a]:
                if k:
                    copy(a, k).wait_recv()
        for a in range(n):
            for k in ks_list[a]:
                if k:
                    copy(a, k).wait_send()
                else:
                    local(a).wait()

    return start, relay, finish


def _exchange_out_shapes(kind, arrays, ks_list=None):
    if kind == "ag":
        return [jax.ShapeDtypeStruct((N_DEV,) + a.shape, a.dtype) for a in arrays]
    ks_list = ks_list or [_ALL_K] * len(arrays)
    return [jax.ShapeDtypeStruct((len(ks),) + a.shape[1:], a.dtype) for a, ks in zip(arrays, ks_list)]


def _exchange(kind, arrays, name):
    n = len(arrays)
    phases = _ag_phases if kind == "ag" else _a2a_phases

    def body(*refs):
        start, relay, finish = phases(refs[:n], refs[n:2 * n], *refs[2 * n:])
        start()
        relay()
        finish()

    return pl.pallas_call(
        body, name=name,
        out_shape=_exchange_out_shapes(kind, arrays),
        in_specs=[_HBM_SPEC] * n, out_specs=[_HBM_SPEC] * n,
        scratch_shapes=_exchange_scratch(n),
    )(*arrays)


def _mm_tiles(m, n_unit, k_unit, tile_bytes, small_tiles_ok=True, tm_cap=0):
    fits = lambda tm, tn, tk: 2 * 2 * (tm * tk + tk * tn) + tile_bytes * tm * tn <= V7X_MM_VMEM_BUDGET
    for cap in (V7X_MM_TILE_MN, V7X_MM_TILE_MN_WHOLE_K) if small_tiles_ok else (V7X_MM_TILE_MN,):
        tm, tn = _tile(m, min(cap, tm_cap or cap)), _tile(n_unit, cap)
        if fits(tm, tn, k_unit) and (tn >= V7X_MM_TILE_MN_WHOLE_K or tn == n_unit):
            return tm, tn, k_unit
    tm, tn = _tile(m, tm_cap or V7X_MM_TILE_MN), _tile(n_unit, V7X_MM_TILE_MN)
    for tk in _divisors_down(k_unit, k_unit if tm_cap else V7X_MM_TILE_K):
        if fits(tm, tn, tk):
            return tm, tn, tk
    return tm, tn, _tile(k_unit, LANE)


def _carry_parts(carry):
    kind, arrays, ks_list = (tuple(carry) + (None,))[:3] if carry is not None else (None, [], None)
    n = len(arrays)
    kind = (kind, ks_list)
    return kind, arrays, [_HBM_SPEC] * n, _exchange_out_shapes(kind[0], arrays, ks_list), (_exchange_scratch(n) if n else [])


def _carry_hooks(kind, x_refs, out_refs, sems, step, last_step):
    if not x_refs:
        return lambda: None
    kind, ks_list = kind
    if kind == "ag":
        start, relay, finish = _ag_phases(x_refs, out_refs, *sems)
    else:
        start, relay, finish = _a2a_phases(x_refs, out_refs, *sems, ks_list=ks_list)
    pl.when(step == 0)(start)

    def after():
        if kind == "ag":
            pl.when(step == (last_step * 7) // 8)(relay)
        pl.when(step == last_step)(finish)

    return after


def _mm(a, b, *, ta=False, tb=False, out_dtype=F32, res=None, carry=None, a_slots=False, b_slots=False, b_group=0,
        out_slots=0, epi=None, m_limit=0, n_limit=0, tm_cap=0, name):
    if a_slots:
        assert not ta
        a_n, m, a_c = a.shape
        k = a_n * a_c
    else:
        m, k = (a.shape[1], a.shape[0]) if ta else a.shape
    if b_slots:
        b_n, b_r, b_c = b.shape
        k2, n = (b_n * b_c, b_r) if tb else (b_r, b_n * b_c)
    else:
        k2, n = (b.shape[1], b.shape[0]) if tb else b.shape
    assert k == k2, (a.shape, b.shape, ta, tb)
    m, n = m_limit or m, n_limit or n
    has_res = res is not None
    assert not (has_res and (out_slots or epi))
    n_units = [n] + ([n // out_slots] if out_slots else []) + ([b_c] if b_slots and not tb else [])
    k_units = [k] + ([a_c] if a_slots else []) + ([b_c] if b_slots and tb else [])
    n_unit, k_unit = min(n_units), min(k_units)
    assert all(u % n_unit == 0 for u in n_units) and all(u % k_unit == 0 for u in k_units)
    epi_fn, epi_ins, epi_outs = epi if epi is not None else (None, [], [])
    tile_bytes = 4 + (2 * res.dtype.itemsize if has_res else 0)
    tile_bytes += sum(2 * e.shape[0] * e.dtype.itemsize for e in epi_ins)
    tile_bytes += sum(2 * l * jnp.dtype(dt).itemsize for l, dt in epi_outs) if epi else 2 * jnp.dtype(out_dtype).itemsize
    tm, tn, tk = _mm_tiles(m, n_unit, k_unit, tile_bytes, small_tiles_ok=not epi, tm_cap=tm_cap)
    if b_group:
        tk = b_group * b_c
        assert b_slots and tb and k % tk == 0 and (not a_slots or a_c % tk == 0)
    ni, nj, nk = m // tm, n // tn, k // tk
    dims = (((0,) if ta else (1,), (1,) if tb else (0,)), ((), ()))

    def slot_map(per, pos):
        if pos == "k_cols":
            return lambda i, j, kk: (kk // per, i, kk % per)
        if pos == "k_cols_j":
            return lambda i, j, kk: (kk // per, j, kk % per)
        if pos == "n_cols_k":
            return lambda i, j, kk: (j // per, kk, j % per)
        return lambda i, j, kk: (j // per, i, j % per)

    if a_slots:
        a_spec = pl.BlockSpec((None, tm, tk), slot_map(a_c // tk, "k_cols"))
    else:
        a_spec = pl.BlockSpec((tk, tm), lambda i, j, kk: (kk, i)) if ta else pl.BlockSpec((tm, tk), lambda i, j, kk: (i, kk))
    if b_group:
        b_spec = pl.BlockSpec((b_group, tn, b_c), lambda i, j, kk: (kk, j, 0))
    elif b_slots and tb:
        b_spec = pl.BlockSpec((None, tn, tk), slot_map(b_c // tk, "k_cols_j"))
    elif b_slots:
        b_spec = pl.BlockSpec((None, tk, tn), slot_map(b_c // tn, "n_cols_k"))
    else:
        b_spec = pl.BlockSpec((tn, tk), lambda i, j, kk: (j, kk)) if tb else pl.BlockSpec((tk, tn), lambda i, j, kk: (kk, j))
    if epi:
        lead_spec = lambda l: pl.BlockSpec((l, tm, tn), lambda i, j, kk: (0, i, j))
        o_specs = [lead_spec(l) for l, _ in epi_outs]
        o_shapes = [jax.ShapeDtypeStruct((l, m, n), dt) for l, dt in epi_outs]
    elif out_slots:
        o_specs = [pl.BlockSpec((None, tm, tn), slot_map((n // out_slots) // tn, "n_cols_i"))]
        o_shapes = [jax.ShapeDtypeStruct((out_slots, m, n // out_slots), out_dtype)]
    else:
        o_specs = [pl.BlockSpec((tm, tn), lambda i, j, kk: (i, j))]
        o_shapes = [jax.ShapeDtypeStruct((m, n), out_dtype)]
    extra_ins = ([res] if has_res else []) + list(epi_ins)
    extra_specs = ([o_specs[0]] if has_res else []) + [pl.BlockSpec((e.shape[0], tm, tn), lambda i, j, kk: (0, i, j)) for e in epi_ins]
    n_in, n_out = 2 + len(extra_ins), len(o_specs)
    c_kind, c_arrays, c_specs, c_shapes, c_scratch = _carry_parts(carry)
    nc = len(c_arrays)
    last_step = ni * nj * nk - 1

    def body(*refs):
        a_ref, b_ref = refs[0], refs[1]
        e_refs = refs[2:n_in]
        x_refs = refs[n_in:n_in + nc]
        o_refs = refs[n_in + nc:n_in + nc + n_out]
        out_refs = refs[n_in + nc + n_out:n_in + 2 * nc + n_out]
        scratch = refs[n_in + 2 * nc + n_out:]
        acc = scratch[0] if nk > 1 else None
        kk = pl.program_id(2)
        step = (pl.program_id(0) * nj + pl.program_id(1)) * nk + kk
        after = _carry_hooks(c_kind, x_refs, out_refs, scratch[-3:], step, last_step)

        def emit(val):
            if has_res:
                val = val + e_refs[0][...].astype(F32)
            if epi:
                for o_ref, parts in zip(o_refs, epi_fn(val, *[e[...] for e in e_refs])):
                    for l, v in enumerate(parts):
                        o_ref[l] = v.astype(o_ref.dtype)
            else:
                o_refs[0][...] = val.astype(out_dtype)

        if b_group:
            part = sum(lax.dot_general(a_ref[:, s * b_c:(s + 1) * b_c], b_ref[s], dims, preferred_element_type=F32)
                       for s in range(b_group))
        else:
            part = lax.dot_general(a_ref[...], b_ref[...], dims, preferred_element_type=F32)
        if nk == 1:
            emit(part)
        else:
            @pl.when(kk == 0)
            def _():
                acc[...] = part

            @pl.when((kk > 0) & (kk < nk - 1))
            def _():
                acc[...] += part

            @pl.when(kk == nk - 1)
            def _():
                emit(acc[...] + part)

        after()

    sem = ("arbitrary",) * 3 if nc else ("parallel", "parallel", "arbitrary")
    outs = pl.pallas_call(
        body, name=name,
        grid=(ni, nj, nk),
        in_specs=[a_spec, b_spec] + extra_specs + c_specs,
        out_specs=o_specs + c_specs,
        out_shape=o_shapes + c_shapes,
        scratch_shapes=([pltpu.VMEM((tm, tn), F32)] if nk > 1 else []) + c_scratch,
        compiler_params=_cparams(*sem),
    )(a, b, *extra_ins, *c_arrays)
    main = list(outs[:n_out]) if epi else outs[0]
    return (main, list(outs[n_out:])) if nc else main


def _ffn_in_fused(h2, w_s, *, carry=None, name):
    t, d = h2.shape
    n_slot, _, c = w_s.shape
    half = n_slot // 2
    tm = _tile(t, 512)
    c_kind, c_arrays, c_specs, c_shapes, c_scratch = _carry_parts(carry)
    nc = len(c_arrays)
    last_step = (t // tm) * half - 1

    def body(h_ref, wg_ref, wu_ref, *refs):
        x_refs, (gu_ref, act_ref), out_refs, sems = refs[:nc], refs[nc:nc + 2], refs[nc + 2:2 * nc + 2], refs[2 * nc + 2:]
        step = pl.program_id(0) * half + pl.program_id(1)
        after = _carry_hooks(c_kind, x_refs, out_refs, sems, step, last_step)
        h = h_ref[...]
        g = jnp.dot(h, wg_ref[...], preferred_element_type=F32)
        u = jnp.dot(h, wu_ref[...], preferred_element_type=F32)
        sg = _sigmoid(g)
        silu = g * sg
        gu_ref[0] = (u * (sg + silu - silu * sg)).astype(BF16)
        gu_ref[1] = silu.astype(BF16)
        act_ref[...] = (silu * u).astype(BF16)
        after()

    outs = pl.pallas_call(
        body, name=name,
        grid=(t // tm, half),
        in_specs=[pl.BlockSpec((tm, d), lambda i, j: (i, 0)),
                  pl.BlockSpec((None, d, c), lambda i, j: (j, 0, 0)),
                  pl.BlockSpec((None, d, c), lambda i, j: (half + j, 0, 0))] + c_specs,
        out_specs=[pl.BlockSpec((2, tm, c), lambda i, j: (0, i, j)), pl.BlockSpec((tm, c), lambda i, j: (i, j))] + c_specs,
        out_shape=[jax.ShapeDtypeStruct((2, t, half * c), BF16), jax.ShapeDtypeStruct((t, half * c), BF16)] + c_shapes,
        scratch_shapes=c_scratch,
        compiler_params=_cparams(*(("arbitrary",) * 2 if nc else ("parallel", "parallel"))),
    )(h2, w_s, w_s, *c_arrays)
    return (outs[0], outs[1], list(outs[2:])) if nc else (outs[0], outs[1])


def _rowwise(fn, ins, row_outs, acc_outs, *, rows, tb, name, carry=None):
    in_specs, args = [], []
    for spec in ins:
        kind, arr = spec[0], spec[1]
        if kind == "row":
            in_specs.append(pl.BlockSpec((tb, arr.shape[1]), lambda i: (i, 0)))
        elif kind == "win":
            width, cb = spec[2], spec[3]
            in_specs.append(pl.BlockSpec((tb, width), functools.partial(lambda i, cb: (i, cb), cb=cb)))
        else:
            in_specs.append(pl.BlockSpec(arr.shape, lambda i: (0, 0)))
        args.append(arr)
    out_specs = [pl.BlockSpec((tb, c), lambda i: (i, 0)) for c, _ in row_outs]
    out_specs += [pl.BlockSpec(shape, lambda i: (0, 0)) for shape in acc_outs]
    out_shape = [jax.ShapeDtypeStruct((rows, c), dt) for c, dt in row_outs]
    out_shape += [jax.ShapeDtypeStruct(shape, F32) for shape in acc_outs]
    n_in, n_row, n_out = len(ins), len(row_outs), len(row_outs) + len(acc_outs)
    c_kind, c_arrays, c_specs, c_shapes, c_scratch = _carry_parts(carry)
    nc = len(c_arrays)

    def body(*refs):
        after = _carry_hooks(c_kind, refs[n_in:n_in + nc], refs[n_in + nc + n_out:n_in + 2 * nc + n_out],
                             refs[n_in + 2 * nc + n_out:], pl.program_id(0), rows // tb - 1)
        vals = [r[...] for r in refs[:n_in]]
        outs = fn(*vals)
        if not isinstance(outs, (tuple, list)):
            outs = (outs,)
        out_refs = refs[n_in + nc:n_in + nc + n_out]
        for o_ref, val in zip(out_refs[:n_row], outs[:n_row]):
            o_ref[...] = val.astype(o_ref.dtype)
        first = pl.program_id(0) == 0
        for o_ref, val in zip(out_refs[n_row:], outs[n_row:]):
            @pl.when(first)
            def _(o_ref=o_ref):
                o_ref[...] = jnp.zeros_like(o_ref)
            o_ref[...] += val
        after()

    res = pl.pallas_call(
        body, name=name,
        grid=(rows // tb,),
        in_specs=in_specs + c_specs, out_specs=out_specs + c_specs, out_shape=out_shape + c_shapes,
        scratch_shapes=c_scratch,
        compiler_params=_cparams("arbitrary"),
    )(*args, *c_arrays)
    return (list(res[:n_out]), list(res[n_out:])) if nc else res


def _rms_fwd(x, g, name, carry=None):
    def fn(xv, gv):
        r = lax.rsqrt(jnp.mean(xv * xv, axis=-1, keepdims=True) + NORM_EPS)
        return (xv * r * gv,)
    res = _rowwise(fn, [("row", x), ("full", g)], [(x.shape[1], BF16)], [], rows=x.shape[0], tb=_tile(x.shape[0], 512),
                   name=name, carry=carry)
    return (res[0][0], res[1]) if carry is not None else res[0]


def _rms_bwd(x, g, dh, dres, name, out_dtype):
    d = x.shape[1]

    def fn(xv, gv, dhv, drv):
        r = lax.rsqrt(jnp.mean(xv * xv, axis=-1, keepdims=True) + NORM_EPS)
        xhat = xv * r
        dhv = dhv.astype(F32)
        dxhat = dhv * gv
        dx = drv.astype(F32) + r * (dxhat - xhat * jnp.mean(dxhat * xhat, axis=-1, keepdims=True))
        dg = jnp.sum(dhv * xhat, axis=0, keepdims=True)
        return dx, dg

    return _rowwise(fn, [("row", x), ("full", g), ("row", dh), ("row", dres)], [(d, out_dtype)], [(1, d)],
                    rows=x.shape[0], tb=_tile(x.shape[0], 256), name=name)


def _loss_head(x2, g, target, name):
    d = x2.shape[1]

    def fn(xv, gv, tv):
        r = lax.rsqrt(jnp.mean(xv * xv, axis=-1, keepdims=True) + NORM_EPS)
        xhat = xv * r
        diff = xhat * gv - tv
        loss = 0.5 * jnp.sum(jnp.mean(diff * diff, axis=-1, keepdims=True), axis=0, keepdims=True)
        dy = diff * (1.0 / d)
        dxhat = dy * gv
        dx = r * (dxhat - xhat * jnp.mean(dxhat * xhat, axis=-1, keepdims=True))
        dg = jnp.sum(dy * xhat, axis=0, keepdims=True)
        return dx, dg, jnp.broadcast_to(loss, (1, LANE))

    return _rowwise(fn, [("row", x2), ("full", g), ("row", target)], [(d, BF16)], [(1, d), (1, LANE)],
                    rows=x2.shape[0], tb=_tile(x2.shape[0], 256), name=name)


def _swiglu_bwd_tile(dact, dswiglu):
    return ((dact * dswiglu[0].astype(F32), dact * dswiglu[1].astype(F32)),)


def _mix_fwd(proj, pa, pb, d, name):
    def fn(ga, gb, av, bv):
        return (_sigmoid(ga.astype(F32)) * av.astype(F32) + _sigmoid(gb.astype(F32)) * bv.astype(F32),)
    return _rowwise(fn, [("win", proj, d, 0), ("win", proj, d, 1), ("row", pa), ("row", pb)], [(d, BF16)], [],
                    rows=pa.shape[0], tb=_tile(pa.shape[0], 512), name=name)[0]


def _mix_bwd(proj, pa, pb, dmix, d, name):
    def fn(ga, gb, av, bv, dm):
        dm = dm.astype(F32)
        sa, sb = _sigmoid(ga.astype(F32)), _sigmoid(gb.astype(F32))
        av, bv = av.astype(F32), bv.astype(F32)
        return dm * sa, dm * sb, dm * av * sa * (1.0 - sa), dm * bv * sb * (1.0 - sb)
    return _rowwise(fn, [("win", proj, d, 0), ("win", proj, d, 1), ("row", pa), ("row", pb), ("row", dmix)],
                    [(d, BF16)] * 4, [], rows=pa.shape[0], tb=_tile(pa.shape[0], 512), name=name)


def _chunk_masks(tb):
    r = lax.broadcasted_iota(jnp.int32, (tb, tb), 0)
    c = lax.broadcasted_iota(jnp.int32, (tb, tb), 1)
    same = lax.shift_right_logical(r, GLA_CHUNK_SHIFT) == lax.shift_right_logical(c, GLA_CHUNK_SHIFT)
    return same, same & (c <= r), same & (r <= c)


def _mask_bf16(mask):
    return jnp.where(mask, 1.0, 0.0).astype(BF16)


def _split_dot(mask_bf, x, terms):
    acc, rem = None, x
    for _ in range(terms):
        hi = rem.astype(BF16)
        part = jnp.dot(mask_bf, hi, preferred_element_type=F32)
        acc = part if acc is None else acc + part
        rem = rem - hi.astype(F32)
    return acc


def _gla_decay(al, wa2, ba2, same_bf, causal_bf):
    z = jnp.dot(al.astype(BF16), wa2, preferred_element_type=F32) + ba2
    la = (jnp.minimum(z, 0.0) - jnp.log(1.0 + jnp.exp(-jnp.abs(z)))) * (1.0 / GLA_TAU)
    bc = _split_dot(causal_bf, la, 3)
    bl = _split_dot(same_bf, la, 3)
    return z, bc, bl


def _dot_t(a, b, ca, cb):
    return lax.dot_general(a, b, (((ca,), (cb,)), ((), ())), preferred_element_type=F32)


def _gla_fwd(proj, alow, wa2, ba2, ghn, *, dk, dv, name, carry=None):
    t = proj.shape[0]
    tb = min(GLA_BLOCK, t)
    nch = tb // GLA_CHUNK
    hk, hv = dk // GLA_HEADS, dv // GLA_HEADS
    scale = hk ** -0.5
    v_cb, r_cb = (8 * dk) // dv, (8 * dk) // dv + 1
    q_cb, k_cb = (8 * dk + 2 * dv) // dk, (8 * dk + 2 * dv) // dk + 1
    c_kind, c_arrays, c_specs, c_shapes, c_scratch = _carry_parts(carry)
    nc = len(c_arrays)

    def body(q_ref, k_ref, v_ref, r_ref, al_ref, wa2_ref, ba2_ref, ghn_ref, *refs):
        x_refs, (oa_ref, opre_ref, s_ref), out_refs = refs[:nc], refs[nc:nc + 3], refs[nc + 3:2 * nc + 3]
        st_scr, sems = refs[2 * nc + 3], refs[2 * nc + 4:]
        after = _carry_hooks(c_kind, x_refs, out_refs, sems, pl.program_id(0), t // tb - 1)

        @pl.when(pl.program_id(0) == 0)
        def _():
            st_scr[...] = jnp.zeros_like(st_scr)

        same, causal, _ = _chunk_masks(tb)
        same_bf, causal_bf = _mask_bf16(same), _mask_bf16(causal)
        _, bc, bl = _gla_decay(al_ref[...], wa2_ref[...], ba2_ref[...], same_bf, causal_bf)
        q = q_ref[...].astype(F32) * scale
        k = k_ref[...].astype(F32)
        qd = (q * jnp.exp(bc)).astype(BF16)
        ki = (k * jnp.exp(-bc)).astype(BF16)
        ks = (k * jnp.exp(bl - bc)).astype(BF16)
        dl = jnp.exp(bl)
        ksls = [slice(h * hk, (h + 1) * hk) for h in range(GLA_HEADS)]
        vsls = [slice(h * hv, (h + 1) * hv) for h in range(GLA_HEADS)]
        v_hs = [v_ref[:, vsl] for vsl in vsls]
        o_intras = []
        for ksl, v_h in zip(ksls, v_hs):
            sc = jnp.where(causal, _dot_t(qd[:, ksl], ki[:, ksl], 1, 1), 0.0)
            o_intras.append(jnp.dot(sc.astype(BF16), v_h, preferred_element_type=F32))
        for c in range(nch):
            rows = slice(c * GLA_CHUNK, (c + 1) * GLA_CHUNK)
            for h, (ksl, vsl) in enumerate(zip(ksls, vsls)):
                st = st_scr[h]
                s_ref[c, h] = st
                opre_ref[rows, vsl] = o_intras[h][rows] + _dot_t(qd[rows, ksl], st.astype(BF16), 1, 1)
                st_scr[h] = dl[c * GLA_CHUNK:c * GLA_CHUNK + 1, ksl] * st + _dot_t(v_hs[h][rows], ks[rows, ksl], 0, 0)
        for h in range(GLA_HEADS):
            vsl = slice(h * hv, (h + 1) * hv)
            o = opre_ref[:, vsl]
            rs = lax.rsqrt(jnp.mean(o * o, axis=-1, keepdims=True) + NORM_EPS)
            rv = r_ref[:, vsl].astype(F32)
            oa_ref[:, vsl] = (rv * _sigmoid(rv) * (o * rs * ghn_ref[:, vsl])).astype(BF16)
        after()

    nchunks = t // GLA_CHUNK
    outs = pl.pallas_call(
        body, name=name,
        grid=(t // tb,),
        in_specs=[
            pl.BlockSpec((tb, dk), lambda i: (i, q_cb)),
            pl.BlockSpec((tb, dk), lambda i: (i, k_cb)),
            pl.BlockSpec((tb, dv), lambda i: (i, v_cb)),
            pl.BlockSpec((tb, dv), lambda i: (i, r_cb)),
            pl.BlockSpec((tb, LANE), lambda i: (i, 0)),
            pl.BlockSpec(wa2.shape, lambda i: (0, 0)),
            pl.BlockSpec(ba2.shape, lambda i: (0, 0)),
            pl.BlockSpec(ghn.shape, lambda i: (0, 0)),
        ] + c_specs,
        out_specs=[
            pl.BlockSpec((tb, dv), lambda i: (i, 0)),
            pl.BlockSpec((tb, dv), lambda i: (i, 0)),
            pl.BlockSpec((nch, GLA_HEADS, hv, hk), lambda i: (i, 0, 0, 0)),
        ] + c_specs,
        out_shape=[
            jax.ShapeDtypeStruct((t, dv), BF16),
            jax.ShapeDtypeStruct((t, dv), F32),
            jax.ShapeDtypeStruct((nchunks, GLA_HEADS, hv, hk), F32),
        ] + c_shapes,
        scratch_shapes=[pltpu.VMEM((GLA_HEADS, hv, hk), F32)] + c_scratch,
        compiler_params=_cparams("arbitrary"),
    )(proj, proj, proj, proj, alow, wa2, ba2, ghn, *c_arrays)
    return (outs[0], outs[1], outs[2], list(outs[3:])) if nc else tuple(outs)


def _gla_bwd(proj, alow, wa2, ba2, ghn, opre, states, doa, *, dk, dv, name):
    t = proj.shape[0]
    tb = min(GLA_BLOCK, t)
    nb = t // tb
    nch = tb // GLA_CHUNK
    hk, hv = dk // GLA_HEADS, dv // GLA_HEADS
    scale = hk ** -0.5
    v_cb, r_cb = (8 * dk) // dv, (8 * dk) // dv + 1
    q_cb, k_cb = (8 * dk + 2 * dv) // dk, (8 * dk + 2 * dv) // dk + 1

    def body(q_ref, k_ref, v_ref, r_ref, al_ref, wa2_ref, ba2_ref, ghn_ref, opre_ref, s_ref, doa_ref,
             dq_ref, dk_ref, dv_ref, dr_ref, dal_ref, dwa2_ref, dba2_ref, dghn_ref,
             dst_scr, dqd_scr, dki_scr, dks_scr, ddl_scr):
        @pl.when(pl.program_id(0) == 0)
        def _():
            dst_scr[...] = jnp.zeros_like(dst_scr)
            dwa2_ref[...] = jnp.zeros_like(dwa2_ref)
            dba2_ref[...] = jnp.zeros_like(dba2_ref)
            dghn_ref[...] = jnp.zeros_like(dghn_ref)

        same, causal, anti = _chunk_masks(tb)
        same_bf, causal_bf, anti_bf = _mask_bf16(same), _mask_bf16(causal), _mask_bf16(anti)
        al = al_ref[...]
        wa2v = wa2_ref[...]
        z, bc, bl = _gla_decay(al, wa2v, ba2_ref[...], same_bf, causal_bf)
        e_bc, e_nbc, e_st = jnp.exp(bc), jnp.exp(-bc), jnp.exp(bl - bc)
        q = q_ref[...].astype(F32) * scale
        k = k_ref[...].astype(F32)
        qd_f, ki_f, ks_f = q * e_bc, k * e_nbc, k * e_st
        qd, ki, ks = qd_f.astype(BF16), ki_f.astype(BF16), ks_f.astype(BF16)
        dl = jnp.exp(bl)
        per_head = []
        for h in range(GLA_HEADS):
            ksl = slice(h * hk, (h + 1) * hk)
            vsl = slice(h * hv, (h + 1) * hv)
            o = opre_ref[:, vsl]
            rs = lax.rsqrt(jnp.mean(o * o, axis=-1, keepdims=True) + NORM_EPS)
            ohat = o * rs
            g_h = ghn_ref[:, vsl]
            rv = r_ref[:, vsl].astype(F32)
            sg = _sigmoid(rv)
            d_oa = doa_ref[:, vsl].astype(F32)
            don = d_oa * (rv * sg)
            dr_ref[:, vsl] = (d_oa * (ohat * g_h) * (sg * (1.0 + rv * (1.0 - sg)))).astype(BF16)
            dghn_ref[:, vsl] += jnp.sum(don * ohat, axis=0, keepdims=True)
            dohat = don * g_h
            do_f = rs * (dohat - ohat * jnp.mean(dohat * ohat, axis=-1, keepdims=True))
            do = do_f.astype(BF16)
            v_h = v_ref[:, vsl]
            p = jnp.where(causal, _dot_t(do, v_h, 1, 1), 0.0).astype(BF16)
            dqd_intra = jnp.dot(p, ki[:, ksl], preferred_element_type=F32)
            dki_scr[:, ksl] = _dot_t(p, qd[:, ksl], 0, 0)
            sc = jnp.where(causal, _dot_t(qd[:, ksl], ki[:, ksl], 1, 1), 0.0).astype(BF16)
            dv_intra = _dot_t(sc, do, 0, 0)
            per_head.append((ksl, vsl, v_h, do, dqd_intra, dv_intra))
        for c in reversed(range(nch)):
            rows = slice(c * GLA_CHUNK, (c + 1) * GLA_CHUNK)
            for h, (ksl, vsl, v_h, do, dqd_intra, dv_intra) in enumerate(per_head):
                dst = dst_scr[h]
                st = s_ref[c, h]
                dst_bf = dst.astype(BF16)
                dv_ref[rows, vsl] = (dv_intra[rows] + _dot_t(ks[rows, ksl], dst_bf, 1, 1)).astype(BF16)
                dks_scr[rows, ksl] = jnp.dot(v_h[rows], dst_bf, preferred_element_type=F32)
                dl_c = dl[c * GLA_CHUNK:c * GLA_CHUNK + 1, ksl]
                ddl = jnp.sum(dst * st, axis=0, keepdims=True) * dl_c
                ddl_scr[rows, ksl] = jnp.broadcast_to(ddl, (GLA_CHUNK, hk))
                dqd_scr[rows, ksl] = dqd_intra[rows] + jnp.dot(do[rows], st.astype(BF16), preferred_element_type=F32)
                dst_scr[h] = dl_c * dst + _dot_t(do[rows], qd[rows, ksl], 0, 0)
        dqd, dki, dks = dqd_scr[...], dki_scr[...], dks_scr[...]
        dq_ref[...] = (dqd * (scale * e_bc)).astype(BF16)
        dk_ref[...] = (dki * e_nbc + dks * e_st).astype(BF16)
        dks_ks = dks * ks_f
        dbc = dqd * qd_f - dki * ki_f - dks_ks
        dla = _split_dot(anti_bf, dbc, 2) + _split_dot(same_bf, dks_ks, 2) + ddl_scr[...]
        dz = (dla * (1.0 / GLA_TAU) * (1.0 - _sigmoid(z)))
        dz_bf = dz.astype(BF16)
        dal_ref[...] = _dot_t(dz_bf, wa2v, 1, 1).astype(BF16)
        dwa2_ref[...] += _dot_t(al.astype(BF16), dz_bf, 0, 0)
        dba2_ref[...] += jnp.sum(dz, axis=0, keepdims=True)

    rev = lambda i: nb - 1 - i
    return pl.pallas_call(
        body, name=name,
        grid=(nb,),
        in_specs=[
            pl.BlockSpec((tb, dk), lambda i: (rev(i), q_cb)),
            pl.BlockSpec((tb, dk), lambda i: (rev(i), k_cb)),
            pl.BlockSpec((tb, dv), lambda i: (rev(i), v_cb)),
            pl.BlockSpec((tb, dv), lambda i: (rev(i), r_cb)),
            pl.BlockSpec((tb, LANE), lambda i: (rev(i), 0)),
            pl.BlockSpec(wa2.shape, lambda i: (0, 0)),
            pl.BlockSpec(ba2.shape, lambda i: (0, 0)),
            pl.BlockSpec(ghn.shape, lambda i: (0, 0)),
            pl.BlockSpec((tb, dv), lambda i: (rev(i), 0)),
            pl.BlockSpec((nch, GLA_HEADS, hv, hk), lambda i: (rev(i), 0, 0, 0)),
            pl.BlockSpec((tb, dv), lambda i: (rev(i), 0)),
        ],
        out_specs=[
            pl.BlockSpec((tb, dk), lambda i: (rev(i), 0)),
            pl.BlockSpec((tb, dk), lambda i: (rev(i), 0)),
            pl.BlockSpec((tb, dv), lambda i: (rev(i), 0)),
            pl.BlockSpec((tb, dv), lambda i: (rev(i), 0)),
            pl.BlockSpec((tb, LANE), lambda i: (rev(i), 0)),
            pl.BlockSpec(wa2.shape, lambda i: (0, 0)),
            pl.BlockSpec(ba2.shape, lambda i: (0, 0)),
            pl.BlockSpec(ghn.shape, lambda i: (0, 0)),
        ],
        out_shape=[
            jax.ShapeDtypeStruct((t, dk), BF16),
            jax.ShapeDtypeStruct((t, dk), BF16),
            jax.ShapeDtypeStruct((t, dv), BF16),
            jax.ShapeDtypeStruct((t, dv), BF16),
            jax.ShapeDtypeStruct((t, LANE), BF16),
            jax.ShapeDtypeStruct(wa2.shape, F32),
            jax.ShapeDtypeStruct(ba2.shape, F32),
            jax.ShapeDtypeStruct(ghn.shape, F32),
        ],
        scratch_shapes=[pltpu.VMEM((GLA_HEADS, hv, hk), F32)] + [pltpu.VMEM((tb, dk), F32)] * 4,
        compiler_params=_cparams("arbitrary"),
    )(proj, proj, proj, proj, alow, wa2, ba2, ghn, opre, states, doa)


def _s5_tables(lam_re, lam_im, log_dt, b_re, b_im, c_re, c_im):
    hp = lax.Precision.HIGHEST
    g, p = lam_re.shape
    ln = S5_L
    dt = jnp.exp(log_dt)[:, None]
    lr, li = lam_re, lam_im
    mag = jnp.exp(lr * dt)
    ar, ai = mag * jnp.cos(li * dt), mag * jnp.sin(li * dt)
    den = lr * lr + li * li
    am1 = ar - 1.0
    f_re = ((am1 * lr + ai * li) / den)[..., None]
    f_im = ((ai * lr - am1 * li) / den)[..., None]
    bb_re = f_re * b_re - f_im * b_im
    bb_im = f_re * b_im + f_im * b_re
    j = jnp.arange(ln + 1, dtype=F32)[None, :, None]
    pm = jnp.exp(j * (lr * dt)[:, None, :])
    ang = j * (li * dt)[:, None, :]
    pw_re, pw_im = pm * jnp.cos(ang), pm * jnp.sin(ang)
    cp_re = c_re[:, None] * pw_re[:, :, None, :] - c_im[:, None] * pw_im[:, :, None, :]
    cp_im = c_re[:, None] * pw_im[:, :, None, :] + c_im[:, None] * pw_re[:, :, None, :]
    kj = (jnp.einsum("gjcp,gpd->gjcd", cp_re[:, :ln], bb_re, precision=hp)
          - jnp.einsum("gjcp,gpd->gjcd", cp_im[:, :ln], bb_im, precision=hp))
    eye = jnp.eye(S5_TILE_G, dtype=F32)
    nt = g // S5_TILE_G
    k8 = jnp.einsum("jglcd,gh->jlgdhc", kj.reshape(nt, S5_TILE_G, ln, S5_GC, S5_GC), eye).reshape(nt, ln, LANE, LANE)
    rp_re, rp_im = pw_re[:, ln - 1::-1], pw_im[:, ln - 1::-1]
    bbt_re, bbt_im = bb_re.transpose(0, 2, 1)[:, None], bb_im.transpose(0, 2, 1)[:, None]
    bst = jnp.stack([rp_re[:, :, None, :] * bbt_re - rp_im[:, :, None, :] * bbt_im,
                     rp_re[:, :, None, :] * bbt_im + rp_im[:, :, None, :] * bbt_re], axis=3)
    bc = bst.reshape(nt, S5_TILE_G, ln, S5_GC, 2 * p).transpose(0, 2, 1, 3, 4).reshape(nt, ln, LANE, 2 * p)
    cst = jnp.stack([cp_re[:, 1:], -cp_im[:, 1:]], axis=2)
    cc = cst.reshape(nt, S5_TILE_G, ln, 2, S5_GC, p).transpose(0, 2, 3, 5, 1, 4).reshape(nt, ln, 2 * p, LANE)
    a8 = jnp.stack([jnp.concatenate([pw_re[:, ln], pw_re[:, ln]], axis=-1),
                    jnp.concatenate([-pw_im[:, ln], pw_im[:, ln]], axis=-1)], axis=1)
    a8 = a8.reshape(nt, S5_TILE_G, 2, 2 * p).transpose(0, 2, 1, 3).reshape(nt, 2, S5_TILE_G * 2 * p)
    return k8, bc, cc, a8


def _swap_re_im(x):
    w = x.shape[1]
    if w == LANE:
        return pltpu.roll(x, LANE // 2, 1)
    first_half = (lax.broadcasted_iota(jnp.int32, x.shape, 1) & (LANE // 2)) == 0
    return jnp.where(first_half, pltpu.roll(x, w - LANE // 2, 1), pltpu.roll(x, LANE // 2, 1))


def _s5_expand(bc, cc, w):
    reps = w // LANE
    mask_b = (lax.broadcasted_iota(jnp.int32, (LANE, w), 0) // S5_GC) == (lax.broadcasted_iota(jnp.int32, (LANE, w), 1) // LANE)
    mask_c = (lax.broadcasted_iota(jnp.int32, (w, LANE), 0) // LANE) == (lax.broadcasted_iota(jnp.int32, (w, LANE), 1) // S5_GC)
    b8 = None if bc is None else jnp.where(mask_b, jnp.concatenate([bc] * reps, axis=1), jnp.zeros((), bc.dtype))
    c8 = None if cc is None else jnp.where(mask_c, jnp.concatenate([cc] * reps, axis=0), jnp.zeros((), cc.dtype))
    return b8, c8, mask_b, mask_c


def _state_scan(v, pr, pi, reverse):
    n = v.shape[0]
    row = lax.broadcasted_iota(jnp.int32, v.shape, 0)
    z, s = v, 1
    while s < n:
        if reverse:
            zs = jnp.where(row < n - s, pltpu.roll(z, n - s, 0), 0.0)
        else:
            zs = jnp.where(row >= s, pltpu.roll(z, s, 0), 0.0)
        z = z + zs * pr + _swap_re_im(zs) * pi
        pr, pi = pr * pr - pi * pi, 2.0 * pr * pi
        s *= 2
    return z


def _s5_fwd(proj, u_cb, k8, bc, cc, a8, name):
    t = proj.shape[0]
    nt, ln = bc.shape[:2]
    w = a8.shape[2]
    nc = t // ln
    rb = min(t, S5_ROW_BLOCK)

    def body(u_ref, k_ref, b_ref, c_ref, a_ref, y_ref, x_ref, uf_ref):
        uf_ref[...] = u_ref[...].astype(F32)
        pos = lax.broadcasted_iota(jnp.int32, (rb, LANE), 0) & (ln - 1)
        for r0 in range(0, t, rb):
            u = uf_ref[r0:r0 + rb, :]
            acc = jnp.dot(u.astype(BF16), k_ref[0], preferred_element_type=F32)
            for lag in range(1, ln):
                us = jnp.where(pos >= lag, pltpu.roll(u, lag, 0), 0.0).astype(BF16)
                acc = acc + jnp.dot(us, k_ref[lag], preferred_element_type=F32)
            y_ref[r0:r0 + rb, :] = acc
        v = None
        for s in range(ln):
            part = jnp.dot(uf_ref[pl.ds(s, nc, stride=ln), :].astype(BF16), _s5_expand(b_ref[s], None, w)[0],
                           preferred_element_type=F32)
            v = part if v is None else v + part
        z = _state_scan(v, a_ref[0:1, :], a_ref[1:2, :], reverse=False)
        row = lax.broadcasted_iota(jnp.int32, z.shape, 0)
        x = jnp.where(row >= 1, pltpu.roll(z, 1, 0), 0.0)
        x_ref[...] = x
        x_bf = x.astype(BF16)
        for tt in range(ln):
            y_ref[pl.ds(tt, nc, stride=ln), :] += jnp.dot(x_bf, _s5_expand(None, c_ref[tt], w)[1], preferred_element_type=F32)

    tile = lambda shape: pl.BlockSpec((None,) + shape, lambda j: (j,) + (0,) * len(shape))
    return pl.pallas_call(
        body, name=name, grid=(nt,),
        in_specs=[pl.BlockSpec((t, LANE), lambda j: (0, u_cb + j)), tile((ln, LANE, LANE)), tile(bc.shape[1:]),
                  tile(cc.shape[1:]), tile((2, w))],
        out_specs=[pl.BlockSpec((t, LANE), lambda j: (0, j)), tile((nc, w))],
        out_shape=[jax.ShapeDtypeStruct((t, nt * LANE), F32), jax.ShapeDtypeStruct((nt, nc, w), F32)],
        scratch_shapes=[pltpu.VMEM((t, LANE), F32)],
        compiler_params=_cparams("parallel"),
    )(proj, k8, bc, cc, a8)


def _s5_bwd_data(dy, x_st, k8, bc, cc, a8, name):
    t = dy.shape[0]
    nt, ln = bc.shape[:2]
    w = a8.shape[2]
    nc = t // ln
    rb = min(t, S5_ROW_BLOCK)

    def body(dy_ref, x_ref, k_ref, b_ref, c_ref, a_ref, du_ref, dv_ref, da_ref, dyf_ref, duf_ref):
        dyf_ref[...] = dy_ref[...].astype(F32)
        pos = lax.broadcasted_iota(jnp.int32, (rb, LANE), 0) & (ln - 1)
        for r0 in range(0, t, rb):
            g = dyf_ref[r0:r0 + rb, :]
            acc = _dot_t(g.astype(BF16), k_ref[0], 1, 1)
            for lag in range(1, ln):
                gs = jnp.where(pos < ln - lag, pltpu.roll(g, rb - lag, 0), 0.0).astype(BF16)
                acc = acc + _dot_t(gs, k_ref[lag], 1, 1)
            duf_ref[r0:r0 + rb, :] = acc
        gx = None
        for tt in range(ln):
            part = _dot_t(dyf_ref[pl.ds(tt, nc, stride=ln), :].astype(BF16), _s5_expand(None, c_ref[tt], w)[1], 1, 1)
            gx = part if gx is None else gx + part
        rtot = _state_scan(gx, a_ref[0:1, :], -a_ref[1:2, :], reverse=True)
        row = lax.broadcasted_iota(jnp.int32, rtot.shape, 0)
        dv = jnp.where(row < nc - 1, pltpu.roll(rtot, nc - 1, 0), 0.0)
        dv_ref[...] = dv
        dv_bf = dv.astype(BF16)
        for s in range(ln):
            duf_ref[pl.ds(s, nc, stride=ln), :] += _dot_t(dv_bf, _s5_expand(b_ref[s], None, w)[0], 1, 1)
        du_ref[...] = duf_ref[...].astype(BF16)
        x = x_ref[...]
        da_ref[0:1, :] = jnp.sum(dv * x, axis=0, keepdims=True)
        da_ref[1:2, :] = jnp.sum(dv * _swap_re_im(x), axis=0, keepdims=True)

    tile = lambda shape: pl.BlockSpec((None,) + shape, lambda j: (j,) + (0,) * len(shape))
    return pl.pallas_call(
        body, name=name, grid=(nt,),
        in_specs=[pl.BlockSpec((t, LANE), lambda j: (0, j)), tile((nc, w)), tile((ln, LANE, LANE)), tile(bc.shape[1:]),
                  tile(cc.shape[1:]), tile((2, w))],
        out_specs=[pl.BlockSpec((t, LANE), lambda j: (0, j)), tile((nc, w)), tile((2, w))],
        out_shape=[jax.ShapeDtypeStruct((t, nt * LANE), BF16), jax.ShapeDtypeStruct((nt, nc, w), F32),
                   jax.ShapeDtypeStruct((nt, 2, w), F32)],
        scratch_shapes=[pltpu.VMEM((t, LANE), F32), pltpu.VMEM((t, LANE), F32)],
        compiler_params=_cparams("parallel"),
    )(dy, x_st, k8, bc, cc, a8)


def _s5_bwd_tables(dy, proj, u_cb, x_st, dv, ln, name):
    t = dy.shape[0]
    nt, nc, w = x_st.shape
    rb = min(t, S5_ROW_BLOCK)

    def body(dy_ref, u_ref, x_ref, dv_ref, dk_ref, db_ref, dc_ref, dyf_ref, uf_ref):
        s = pl.program_id(1)

        @pl.when(s == 0)
        def _():
            dyf_ref[...] = dy_ref[...].astype(F32)
            uf_ref[...] = u_ref[...].astype(F32)
            pos = lax.broadcasted_iota(jnp.int32, (rb, LANE), 0) & (ln - 1)
            for r0 in range(0, t, rb):
                u, g_bf = uf_ref[r0:r0 + rb, :], dy_ref[r0:r0 + rb, :]
                for lag in range(ln):
                    us = u if lag == 0 else jnp.where(pos >= lag, pltpu.roll(u, lag, 0), 0.0)
                    part = _dot_t(us.astype(BF16), g_bf, 0, 0)
                    if r0 == 0:
                        dk_ref[lag] = part
                    else:
                        dk_ref[lag] += part

        rows = pl.ds(s, nc, stride=ln)
        _, _, mask_b, mask_c = _s5_expand(None, None, w)
        db = jnp.where(mask_b, _dot_t(uf_ref[rows, :].astype(BF16), dv_ref[...].astype(BF16), 0, 0), 0.0)
        dc = jnp.where(mask_c, _dot_t(x_ref[...].astype(BF16), dyf_ref[rows, :].astype(BF16), 0, 0), 0.0)
        db_ref[...] = sum(db[:, h * LANE:(h + 1) * LANE] for h in range(w // LANE))
        dc_ref[...] = sum(dc[h * LANE:(h + 1) * LANE, :] for h in range(w // LANE))

    tile = lambda shape: pl.BlockSpec((None,) + shape, lambda j, s: (j,) + (0,) * len(shape))
    per_s = lambda shape: pl.BlockSpec((None, None) + shape, lambda j, s: (j, s, 0, 0))
    return pl.pallas_call(
        body, name=name, grid=(nt, ln),
        in_specs=[pl.BlockSpec((t, LANE), lambda j, s: (0, j)), pl.BlockSpec((t, LANE), lambda j, s: (0, u_cb + j)),
                  tile((nc, w)), tile((nc, w))],
        out_specs=[tile((ln, LANE, LANE)), per_s((LANE, LANE)), per_s((LANE, LANE))],
        out_shape=[jax.ShapeDtypeStruct((nt, ln, LANE, LANE), F32)] * 3,
        scratch_shapes=[pltpu.VMEM((t, LANE), F32), pltpu.VMEM((t, LANE), F32)],
        compiler_params=_cparams("parallel", "arbitrary"),
    )(dy, proj, x_st, dv)


def _gelu_parts(y):
    inner = GELU_C * (y + GELU_A * y * y * y)
    th = jnp.tanh(inner)
    return th, 0.5 * y * (1.0 + th)


def _s5_post_fwd(y_raw, proj, u_cb, s5d, wglu, bglu, name):
    w = y_raw.shape[1]

    def fn(yr, u, dsk, wg, bg):
        y = yr + dsk * u.astype(F32)
        _, h = _gelu_parts(y)
        gl = jnp.dot(h.astype(BF16), wg, preferred_element_type=F32) + bg
        return (h * _sigmoid(gl),)

    return _rowwise(fn, [("row", y_raw), ("win", proj, w, u_cb), ("full", s5d), ("full", wglu), ("full", bglu)],
                    [(w, BF16)], [], rows=y_raw.shape[0], tb=_tile(y_raw.shape[0], 512), name=name)[0]


def _s5_post_bwd(y_raw, proj, u_cb, s5d, wglu, bglu, dob, name):
    w = y_raw.shape[1]

    def fn(yr, u, dsk, wg, bg, dov):
        u = u.astype(F32)
        dov = dov.astype(F32)
        y = yr + dsk * u
        th, h = _gelu_parts(y)
        h_bf = h.astype(BF16)
        gl = jnp.dot(h_bf, wg, preferred_element_type=F32) + bg
        sg = _sigmoid(gl)
        dgl = dov * h * sg * (1.0 - sg)
        dgl_bf = dgl.astype(BF16)
        dh = dov * sg + _dot_t(dgl_bf, wg, 1, 1)
        dgelu = 0.5 * (1.0 + th) + 0.5 * y * (1.0 - th * th) * GELU_C * (1.0 + 3.0 * GELU_A * y * y)
        dy = dh * dgelu
        return (dy, dy * dsk,
                _dot_t(h_bf, dgl_bf, 0, 0), jnp.sum(dgl, axis=0, keepdims=True), jnp.sum(dy * u, axis=0, keepdims=True))

    return _rowwise(fn, [("row", y_raw), ("win", proj, w, u_cb), ("full", s5d), ("full", wglu), ("full", bglu), ("row", dob)],
                    [(w, BF16), (w, BF16)], [(w, w), (1, w), (1, w)], rows=y_raw.shape[0], tb=_tile(y_raw.shape[0], 512), name=name)


def _adamw(w, g, m, v, name):
    _, rows, cols = w.shape
    tr, tc = (_tile(rows, 256, align=16), cols) if rows % 16 == 0 else (rows, _tile(cols, 256))
    slots = isinstance(g, (list, tuple))
    gs = list(g) if slots else [g]
    c1 = 1.0 - ADAM_B1 ** ADAM_STEP
    c2 = 1.0 - ADAM_B2 ** ADAM_STEP

    def body(w_ref, m_ref, v_ref, *refs):
        g_refs, out_refs = refs[:len(gs)], refs[len(gs):]
        if slots:
            parts = [g_ref[s].astype(F32) for g_ref in g_refs for s in range(g_ref.shape[0])]
            gv = parts[0]
            for p in parts[1:]:
                gv = gv + p
            out_refs[0][...] = gv
        else:
            gv = g_refs[0][...]
        d_ref, nm_ref, nv_ref = out_refs[-3:]
        nm = ADAM_B1 * m_ref[...] + (1.0 - ADAM_B1) * gv
        nv = ADAM_B2 * v_ref[...] + (1.0 - ADAM_B2) * (gv * gv)
        d_ref[...] = -ADAM_LR * ((nm / c1) / (jnp.sqrt(nv / c2) + ADAM_EPS) + ADAM_WD * w_ref[...])
        nm_ref[...] = nm
        nv_ref[...] = nv

    spec = pl.BlockSpec((None, tr, tc), lambda i, j: (0, i, j))
    g_specs = [pl.BlockSpec((a.shape[0], tr, tc), lambda i, j: (0, i, j)) for a in gs] if slots else [pl.BlockSpec((tr, tc), lambda i, j: (i, j))]
    n_out = 4 if slots else 3
    return pl.pallas_call(
        body, name=name, grid=(rows // tr, cols // tc),
        in_specs=[spec, spec, spec] + g_specs, out_specs=[spec] * n_out,
        out_shape=[jax.ShapeDtypeStruct((1, rows, cols), F32)] * n_out,
        compiler_params=_cparams("parallel", "parallel"),
    )(w, m, v, *gs)


def _slot_sum(x, name):
    _, rows, cols = x.shape
    if rows % 8 == 0:
        tr, tc = _tile(rows, 512, align=8), cols
    else:
        tr, tc = rows, _tile(cols, 256)

    def body(x_ref, o_ref):
        acc = x_ref[0].astype(F32)
        for s in range(1, N_DEV):
            acc = acc + x_ref[s].astype(F32)
        o_ref[...] = acc

    return pl.pallas_call(
        body, name=name, grid=(rows // tr, cols // tc),
        in_specs=[pl.BlockSpec((N_DEV, tr, tc), lambda i, j: (0, i, j))],
        out_specs=pl.BlockSpec((tr, tc), lambda i, j: (i, j)),
        out_shape=jax.ShapeDtypeStruct((rows, cols), F32),
        compiler_params=_cparams("parallel", "parallel"),
    )(x)


_REST = (("w_a2", 1), ("w_glu", 0), ("w_branch_a", 1), ("w_branch_b", 1), ("w_out", 0), ("w_ffn_in", 1), ("w_ffn_out", 0))
_SMALL = ("norm1_g", "b_a2", "gla_norm_g", "lam_re", "lam_im", "log_dt", "s5_b_re", "s5_b_im", "s5_c_re", "s5_c_im",
          "s5_d", "b_glu", "norm2_g", "final_norm_g")
_ORDER = ("norm1_g", "w_in", "w_a2", "b_a2", "gla_norm_g", "lam_re", "lam_im", "log_dt", "s5_b_re", "s5_b_im", "s5_c_re",
          "s5_c_im", "s5_d", "w_glu", "b_glu", "w_branch_a", "w_branch_b", "w_out", "norm2_g", "w_ffn_in", "w_ffn_out", "final_norm_g")


def _join_slots(slots, axis):
    _, r, c = slots.shape
    if axis == 0:
        return slots.reshape(N_DEV * r, c)
    return slots.transpose(1, 0, 2).reshape(r, N_DEV * c)


def _to_slots(full, axis):
    r, c = full.shape
    if axis == 0:
        return full.reshape(N_DEV, r // N_DEV, c)
    return full.reshape(r, N_DEV, c // N_DEV).transpose(1, 0, 2)


def _local_step(x, target, w_in_t, small, rest):
    t, d = x.shape
    dk, dv, s5w = d // 4, d // 2, d // 4
    dist = not isinstance(rest, dict)
    if dist:
        h1, (w_in_slots,) = _rms_fwd(x, small["norm1_g"], "norm1_fwd", carry=("ag", [w_in_t]))
        w_in_t = w_in_slots.reshape(-1, d)
    else:
        h1 = _rms_fwd(x, small["norm1_g"], "norm1_fwd")
    o_q, o_k, o_v, o_r, o_al = 0, dk, 2 * dk, 2 * dk + dv, 2 * dk + 2 * dv
    o_u = o_al + GLA_RANK
    o_ga, o_gb = o_u + s5w, o_u + s5w + d
    rows = lambda a, o, n: a[o:o + n]
    w_al_t = jnp.pad(rows(w_in_t, o_al, GLA_RANK), ((0, LANE - GLA_RANK), (0, 0)))
    w_ext_t = jnp.concatenate([rows(w_in_t, o_ga, d), rows(w_in_t, o_gb, d), rows(w_in_t, o_v, dv), rows(w_in_t, o_r, dv),
                               rows(w_in_t, o_q, dk), rows(w_in_t, o_k, dk), rows(w_in_t, o_u, s5w), w_al_t], axis=0)
    n_main = 2 * d + 2 * dv + 2 * dk + s5w
    u_cb = (2 * d + 2 * dv + 2 * dk) // s5w

    if dist:
        proj, (a2_s, glu_s, w_ffn_in_s) = _mm(h1, w_ext_t, tb=True, n_limit=n_main, out_dtype=BF16,
                                              carry=("ag", [rest[0], rest[1], rest[5]]), name="in_proj")
        w = {"w_a2": _join_slots(a2_s, 1), "w_glu": _join_slots(glu_s, 0)}
    else:
        proj = _mm(h1, w_ext_t, tb=True, n_limit=n_main, out_dtype=BF16, name="in_proj")
        w = rest
        w_ffn_in_s = _to_slots(rest["w_ffn_in"], 1)
    wa2 = jnp.pad(w["w_a2"], ((0, LANE - GLA_RANK), (0, 0)))
    alow = _mm(h1, w_al_t, tb=True, out_dtype=BF16, name="in_proj_gate_rank")
    if dist:
        o_a, o_pre, states, got = _gla_fwd(proj, alow, wa2, small["b_a2"], small["gla_norm_g"], dk=dk, dv=dv, name="gla_fwd",
                                           carry=("ag", rest[2:5]))
        w.update({n: _join_slots(g, ax) for (n, ax), g in zip(_REST[2:5], got)})
    else:
        o_a, o_pre, states = _gla_fwd(proj, alow, wa2, small["b_a2"], small["gla_norm_g"], dk=dk, dv=dv, name="gla_fwd")

    s5_params = (small["lam_re"], small["lam_im"], small["log_dt"][0], small["s5_b_re"], small["s5_b_im"],
                 small["s5_c_re"], small["s5_c_im"])
    (k8, bc, cc, a8), tables_vjp = jax.vjp(_s5_tables, *s5_params)
    k8_bf, b8_bf, c8_bf = k8.astype(BF16), bc.astype(BF16), cc.astype(BF16)
    u_lane_cb = u_cb * s5w // LANE
    y_raw, x_st = _s5_fwd(proj, u_lane_cb, k8_bf, b8_bf, c8_bf, a8, "s5_scan_fwd")
    o_b = _s5_post_fwd(y_raw, proj, u_cb, small["s5_d"], w["w_glu"], small["b_glu"], "s5_post_fwd")

    pa = _mm(o_a, w["w_branch_a"], out_dtype=BF16, name="branch_a")
    pb = _mm(o_b, w["w_branch_b"], out_dtype=BF16, name="branch_b")
    mix = _mix_fwd(proj, pa, pb, d, "mix_fwd")
    x1 = _mm(mix, w["w_out"], res=x, name="out_proj")
    h2 = _rms_fwd(x1, small["norm2_g"], "norm2_fwd")
    if dist:
        gu, act, (w_ffn_out_s,) = _ffn_in_fused(h2, w_ffn_in_s, carry=("ag", rest[-1:]), name="ffn_in")
        w_ffn_out = _join_slots(w_ffn_out_s, 0)
    else:
        gu, act = _ffn_in_fused(h2, w_ffn_in_s, name="ffn_in")
        w_ffn_out = rest["w_ffn_out"]
    x2 = _mm(act, w_ffn_out, res=x1, name="ffn_out")
    dx2_bf, d_final_g, loss = _loss_head(x2, small["final_norm_g"], target, "loss_head")

    recv = {}
    dgu, = _mm(dx2_bf, w_ffn_out, tb=True, epi=(_swiglu_bwd_tile, [gu], [(2, BF16)]), name="d_act")
    g_ffn_out = _mm(act, dx2_bf, ta=True, out_dtype=BF16, name="g_w_ffn_out")
    if dist:
        g_ffn_in_s, recv["w_ffn_out"] = _mm(h2, dgu, ta=True, b_slots=True, out_dtype=BF16, out_slots=N_DEV, tm_cap=512,
                                            carry=("a2a", [_to_slots(g_ffn_out, 0)]), name="g_w_ffn_in")
        dh2, (recv_ffn_in,) = _mm(dgu, w_ffn_in_s, tb=True, a_slots=True, b_slots=True, b_group=4, tm_cap=512,
                                  out_dtype=BF16, carry=("a2a", [g_ffn_in_s], [_ALL_K[:-1]]), name="d_h2")
    else:
        g_ffn_in_s = _mm(h2, dgu, ta=True, b_slots=True, out_dtype=BF16, out_slots=N_DEV, tm_cap=512, name="g_w_ffn_in")
        dh2 = _mm(dgu, w_ffn_in_s, tb=True, a_slots=True, b_slots=True, b_group=4, tm_cap=512, out_dtype=BF16, name="d_h2")
    dx1_bf, d_norm2_g = _rms_bwd(x1, small["norm2_g"], dh2, dx2_bf, "norm2_bwd", BF16)
    dmix = _mm(dx1_bf, w["w_out"], tb=True, out_dtype=BF16, name="d_mix")
    g_out = _mm(mix, dx1_bf, ta=True, out_dtype=BF16, name="g_w_out")
    dpa, dpb, dga, dgb = _mix_bwd(proj, pa, pb, dmix, d, "mix_bwd")
    doa = _mm(dpa, w["w_branch_a"], tb=True, out_dtype=BF16, name="d_o_a")
    dob = _mm(dpb, w["w_branch_b"], tb=True, out_dtype=BF16, name="d_o_b")
    g_branch_a = _mm(o_a, dpa, ta=True, out_dtype=BF16, name="g_w_branch_a")
    g_branch_b = _mm(o_b, dpb, ta=True, out_dtype=BF16, name="g_w_branch_b")

    dy_s5, du_direct, g_glu, g_bglu, g_s5d = _s5_post_bwd(y_raw, proj, u_cb, small["s5_d"], w["w_glu"], small["b_glu"], dob, "s5_post_bwd")
    du_scan, dv_st, d_a8 = _s5_bwd_data(dy_s5, x_st, k8_bf, b8_bf, c8_bf, a8, "s5_scan_bwd")
    d_k8, d_b8, d_c8 = _s5_bwd_tables(dy_s5, proj, u_lane_cb, x_st, dv_st, S5_L, "s5_scan_bwd_tables")
    g_lam_re, g_lam_im, g_log_dt, g_b_re, g_b_im, g_c_re, g_c_im = tables_vjp((d_k8, d_b8, d_c8, d_a8))
    du = du_scan + du_direct

    dq, dkk, dvv, dr, dal, g_wa2, g_ba2, g_ghn = _gla_bwd(proj, alow, wa2, small["b_a2"], small["gla_norm_g"], o_pre, states, doa,
                                                        dk=dk, dv=dv, name="gla_bwd")
    dproj = jnp.concatenate([dga, dgb, dvv, dr, dq, dkk, du, dal], axis=1)
    mid = {"w_out": g_out, "w_branch_a": g_branch_a, "w_branch_b": g_branch_b, "w_glu": g_glu.astype(BF16),
           "w_a2": g_wa2[:GLA_RANK].astype(BF16)}
    if dist:
        axes = dict(_REST)
        g_main_t, got = _mm(dproj, h1, ta=True, m_limit=n_main, out_dtype=BF16, name="g_w_in_main",
                            carry=("a2a", [_to_slots(mid[n], axes[n]) for n in mid] + [g_ffn_in_s],
                                   [_ALL_K] * len(mid) + [_ALL_K[-1:]]))
        recv.update(zip(mid, [[g] for g in got[:-1]]))
        recv["w_ffn_in"] = [recv_ffn_in, got[-1]]
    else:
        g_main_t = _mm(dproj, h1, ta=True, m_limit=n_main, out_dtype=BF16, name="g_w_in_main")
    g_al_t = _mm(dal, h1, ta=True, out_dtype=BF16, name="g_w_in_gate_rank")
    mrows = lambda o, n: g_main_t[o:o + n]
    g_w_in_t = jnp.concatenate([mrows(2 * d + 2 * dv, dk), mrows(2 * d + 2 * dv + dk, dk), mrows(2 * d, dv), mrows(2 * d + dv, dv),
                                g_al_t[:GLA_RANK], mrows(2 * d + 2 * dv + 2 * dk, s5w), mrows(0, d), mrows(d, d)], axis=0)
    if dist:
        dh1, recv["w_in"] = _mm(dproj, w_ext_t, out_dtype=BF16, carry=("a2a", [_to_slots(g_w_in_t, 0)]), name="d_h1")
    else:
        dh1 = _mm(dproj, w_ext_t, out_dtype=BF16, name="d_h1")
    grad_x, d_norm1_g = _rms_bwd(x, small["norm1_g"], dh1, dx1_bf, "norm1_bwd", F32)

    small_g = {
        "norm1_g": d_norm1_g, "b_a2": g_ba2, "gla_norm_g": g_ghn, "lam_re": g_lam_re, "lam_im": g_lam_im,
        "log_dt": g_log_dt[None], "s5_b_re": g_b_re, "s5_b_im": g_b_im, "s5_c_re": g_c_re, "s5_c_im": g_c_im,
        "s5_d": g_s5d, "b_glu": g_bglu, "norm2_g": d_norm2_g, "final_norm_g": d_final_g,
    }
    if not dist:
        recv = dict(mid, w_in=g_w_in_t, w_ffn_in=_join_slots(g_ffn_in_s, 1), w_ffn_out=g_ffn_out)
    return loss[0, 0], grad_x, recv, small_g


def _small_2d(name, a):
    a = a[0]
    return a[None] if a.ndim == 1 else a


def kernel(x, norm1_g, w_in, w_a2, b_a2, gla_norm_g, lam_re, lam_im, log_dt, s5_b_re, s5_b_im, s5_c_re, s5_c_im, s5_d, w_glu, b_glu, w_branch_a, w_branch_b, w_out, norm2_g, w_ffn_in, w_ffn_out, final_norm_g, loss_target, m_norm1_g, m_w_in, m_w_a2, m_b_a2, m_gla_norm_g, m_lam_re, m_lam_im, m_log_dt, m_s5_b_re, m_s5_b_im, m_s5_c_re, m_s5_c_im, m_s5_d, m_w_glu, m_b_glu, m_w_branch_a, m_w_branch_b, m_w_out, m_norm2_g, m_w_ffn_in, m_w_ffn_out, m_final_norm_g, v_norm1_g, v_w_in, v_w_a2, v_b_a2, v_gla_norm_g, v_lam_re, v_lam_im, v_log_dt, v_s5_b_re, v_s5_b_im, v_s5_c_re, v_s5_c_im, v_s5_d, v_w_glu, v_b_glu, v_w_branch_a, v_w_branch_b, v_w_out, v_norm2_g, v_w_ffn_in, v_w_ffn_out, v_final_norm_g):
    args = dict(locals())
    weights = {n: args[n] for n in _ORDER}
    m_in = {n: args["m_" + n] for n in _ORDER}
    v_in = {n: args["v_" + n] for n in _ORDER}
    transposed = lambda a: a[0].T[None]
    rest = [weights[n][0].astype(BF16) for n, _ in _REST]
    small = {n: _small_2d(n, weights[n]) for n in _SMALL}
    loss_local, grad_x, recv, small_g = _local_step(x[0], loss_target[0], transposed(weights["w_in"])[0].astype(BF16), small, rest)

    grads, delta, new_m, new_v = {}, {}, {}, {}
    for n, _ in _REST:
        grads[n], delta[n], new_m[n], new_v[n] = _adamw(weights[n], recv[n], m_in[n], v_in[n], "adamw_" + n)
    w_in_out = _adamw(transposed(weights["w_in"]), recv["w_in"], transposed(m_in["w_in"]), transposed(v_in["w_in"]), "adamw_w_in")
    grads["w_in"], delta["w_in"], new_m["w_in"], new_v["w_in"] = (transposed(a) for a in w_in_out)

    s_sizes = [small_g[n].size for n in _SMALL]
    s_offs = [sum(s_sizes[:i]) for i in range(len(s_sizes))]
    s_total = sum(s_sizes)
    s_rows = -(-(-(-(s_total + 1) // LANE)) // LANE) * LANE

    def pack_small(parts):
        flat = jnp.concatenate([p.reshape(-1) for p in parts])
        return jnp.pad(flat, (0, s_rows * LANE - flat.size)).reshape(s_rows, LANE)

    s_flat = pack_small([small_g[n] for n in _SMALL] + [loss_local])
    s_red = _slot_sum(_exchange("ag", [s_flat], "small_grads_all_gather")[0], "small_grads_slot_sum")
    loss = s_red.reshape(-1)[s_total]
    sd, sm, sv = _adamw(pack_small([weights[n] for n in _SMALL])[None], s_red, pack_small([m_in[n] for n in _SMALL])[None],
                        pack_small([v_in[n] for n in _SMALL])[None], "adamw_small")
    sd, sm, sv = sd[0], sm[0], sv[0]
    for n, o, s in zip(_SMALL, s_offs, s_sizes):
        shape = weights[n].shape[1:]
        grads[n], delta[n], new_m[n], new_v[n] = (a.reshape(-1)[o:o + s].reshape(shape) for a in (s_red, sd, sm, sv))

    out = [loss, grad_x[None]]
    for tree in (grads, delta, new_m, new_v):
        out += [tree[n].reshape(weights[n].shape) for n in _ORDER]
    return tuple(out)
```

```python
import functools
import math

import jax
import jax.numpy as jnp
from jax import lax
from jax.experimental import pallas as pl
from jax.experimental.pallas import tpu as pltpu

F32 = jnp.float32
BF16 = jnp.bfloat16

NORM_EPS = 1e-6
N_DEV = 8
N_PEER = N_DEV - 1
GLA_HEADS = 4
GLA_CHUNK = 32
GLA_CHUNK_SHIFT = 5
GLA_TAU = 16.0
GLA_RANK = 16
GLA_BLOCK = 256
S5_GC = 16
S5_P = 64
S5_L = 16
S5_TILE_G = 8
S5_ROW_BLOCK = 2048
LANE = 128
V7X_VMEM_LIMIT = 56 * 1024 * 1024
V7X_MM_VMEM_BUDGET = 40 * 1024 * 1024
V7X_MM_TILE_MN = 1408
V7X_MM_TILE_MN_WHOLE_K = 512
V7X_MM_TILE_K = 2048

ADAM_LR = 0.001
ADAM_B1 = 0.9
ADAM_B2 = 0.999
ADAM_EPS = 1e-08
ADAM_WD = 0.01
ADAM_STEP = 10

GELU_C = math.sqrt(2.0 / math.pi)
GELU_A = 0.044715

MESH = pl.DeviceIdType.MESH


def _cparams(*sem):
    return pltpu.CompilerParams(dimension_semantics=sem, vmem_limit_bytes=V7X_VMEM_LIMIT)


def _divisors_down(n, start, align=LANE):
    t = (min(start, n) // align) * align
    found = False
    while t >= align:
        if n % t == 0:
            found = True
            yield t
        t -= align
    if not found:
        yield n


def _tile(n, target, align=LANE):
    return next(_divisors_down(n, target, align))


def _sigmoid(x):
    return 1.0 / (1.0 + jnp.exp(-x))


_HBM_SPEC = pl.BlockSpec(memory_space=pltpu.HBM)


def _exchange_scratch(n):
    return [pltpu.SemaphoreType.DMA((n * N_PEER,)), pltpu.SemaphoreType.DMA((n * N_PEER,)), pltpu.SemaphoreType.DMA((n,))]


def _ag_phases(x_refs, out_refs, send_sems, recv_sems, local_sems):
    n = len(x_refs)
    x, y, c = lax.axis_index("x"), lax.axis_index("y"), lax.axis_index("c")
    me, sibling = (x, y, c), (x, y, 1 - c)
    chips = [(1 - x, y), (x, 1 - y), (1 - x, 1 - y)]

    def copy(a, k, block, to, from_input=False):
        dst = out_refs[a].at[4 * block[0] + 2 * block[1] + block[2]]
        return pltpu.make_async_remote_copy(
            src_ref=x_refs[a] if from_input else dst, dst_ref=dst,
            send_sem=send_sems.at[a * N_PEER + k], recv_sem=recv_sems.at[a * N_PEER + k], device_id=to, device_id_type=MESH)

    def local(a):
        return pltpu.make_async_copy(x_refs[a], out_refs[a].at[4 * x + 2 * y + c], local_sems.at[a])

    def first(a):
        return [copy(a, 0, me, sibling, True)] + [copy(a, 1 + j, me, (*chip, c), True) for j, chip in enumerate(chips)]

    def start():
        for a in range(n):
            local(a).start()
            for cp in first(a):
                cp.start()

    def relay():
        for j, chip in enumerate(chips):
            for a in range(n):
                copy(a, 1 + j, (*chip, c), me).wait_recv()
                copy(a, 4 + j, (*chip, c), sibling).start()

    def finish():
        for a in range(n):
            copy(a, 0, sibling, me).wait_recv()
            for j, chip in enumerate(chips):
                copy(a, 4 + j, (*chip, 1 - c), me).wait_recv()
        for a in range(n):
            for cp in first(a) + [copy(a, 4 + j, (*chip, c), sibling) for j, chip in enumerate(chips)]:
                cp.wait_send()
            local(a).wait()

    return start, relay, finish


_ALL_K = tuple(range(N_DEV))


def _a2a_phases(x_refs, out_refs, send_sems, recv_sems, local_sems, ks_list=None):
    n = len(x_refs)
    ks_list = ks_list or [_ALL_K] * n
    x, y, c = lax.axis_index("x"), lax.axis_index("y"), lax.axis_index("c")
    my = 4 * x + 2 * y + c

    def copy(a, k):
        px, py, pc = (1 - x if k & 4 else x), (1 - y if k & 2 else y), (1 - c if k & 1 else c)
        return pltpu.make_async_remote_copy(
            src_ref=x_refs[a].at[4 * px + 2 * py + pc], dst_ref=out_refs[a].at[ks_list[a].index(k)],
            send_sem=send_sems.at[a * N_PEER + k - 1], recv_sem=recv_sems.at[a * N_PEER + k - 1],
            device_id=(px, py, pc), device_id_type=MESH)

    def local(a):
        return pltpu.make_async_copy(x_refs[a].at[my], out_refs[a].at[ks_list[a].index(0)], local_sems.at[a])

    def start():
        for a in range(n):
            for k in ks_list[a]:
                (copy(a, k) if k else local(a)).start()

    def relay():
        pass

    def finish():
        for a in range(n):
            for k in ks_list[a]:
                if k:
                    copy(a, k).wait_recv()
        for a in range(n):
            for k in ks_list[a]:
                if k:
                    copy(a, k).wait_send()
                else:
                    local(a).wait()

    return start, relay, finish


def _exchange_out_shapes(kind, arrays, ks_list=None):
    if kind == "ag":
        return [jax.ShapeDtypeStruct((N_DEV,) + a.shape, a.dtype) for a in arrays]
    ks_list = ks_list or [_ALL_K] * len(arrays)
    return [jax.ShapeDtypeStruct((len(ks),) + a.shape[1:], a.dtype) for a, ks in zip(arrays, ks_list)]


def _exchange(kind, arrays, name):
    n = len(arrays)
    phases = _ag_phases if kind == "ag" else _a2a_phases

    def body(*refs):
        start, relay, finish = phases(refs[:n], refs[n:2 * n], *refs[2 * n:])
        start()
        relay()
        finish()

    return pl.pallas_call(
        body, name=name,
        out_shape=_exchange_out_shapes(kind, arrays),
        in_specs=[_HBM_SPEC] * n, out_specs=[_HBM_SPEC] * n,
        scratch_shapes=_exchange_scratch(n),
    )(*arrays)


def _mm_tiles(m, n_unit, k_unit, tile_bytes, small_tiles_ok=True, tm_cap=0):
    fits = lambda tm, tn, tk: 2 * 2 * (tm * tk + tk * tn) + tile_bytes * tm * tn <= V7X_MM_VMEM_BUDGET
    for cap in (V7X_MM_TILE_MN, V7X_MM_TILE_MN_WHOLE_K) if small_tiles_ok else (V7X_MM_TILE_MN,):
        tm, tn = _tile(m, min(cap, tm_cap or cap)), _tile(n_unit, cap)
        if fits(tm, tn, k_unit) and (tn >= V7X_MM_TILE_MN_WHOLE_K or tn == n_unit):
            return tm, tn, k_unit
    tm, tn = _tile(m, tm_cap or V7X_MM_TILE_MN), _tile(n_unit, V7X_MM_TILE_MN)
    for tk in _divisors_down(k_unit, k_unit if tm_cap else V7X_MM_TILE_K):
        if fits(tm, tn, tk):
            return tm, tn, tk
    return tm, tn, _tile(k_unit, LANE)


def _carry_parts(carry):
    kind, arrays, ks_list = (tuple(carry) + (None,))[:3] if carry is not None else (None, [], None)
    n = len(arrays)
    kind = (kind, ks_list)
    return kind, arrays, [_HBM_SPEC] * n, _exchange_out_shapes(kind[0], arrays, ks_list), (_exchange_scratch(n) if n else [])


def _carry_hooks(kind, x_refs, out_refs, sems, step, last_step):
    if not x_refs:
        return lambda: None
    kind, ks_list = kind
    if kind == "ag":
        start, relay, finish = _ag_phases(x_refs, out_refs, *sems)
    else:
        start, relay, finish = _a2a_phases(x_refs, out_refs, *sems, ks_list=ks_list)
    pl.when(step == 0)(start)

    def after():
        if kind == "ag":
            pl.when(step == (last_step * 7) // 8)(relay)
        pl.when(step == last_step)(finish)

    return after


def _mm(a, b, *, ta=False, tb=False, out_dtype=F32, res=None, carry=None, a_slots=False, b_slots=False, b_group=0,
        out_slots=0, epi=None, m_limit=0, n_limit=0, tm_cap=0, name):
    if a_slots:
        assert not ta
        a_n, m, a_c = a.shape
        k = a_n * a_c
    else:
        m, k = (a.shape[1], a.shape[0]) if ta else a.shape
    if b_slots:
        b_n, b_r, b_c = b.shape
        k2, n = (b_n * b_c, b_r) if tb else (b_r, b_n * b_c)
    else:
        k2, n = (b.shape[1], b.shape[0]) if tb else b.shape
    assert k == k2, (a.shape, b.shape, ta, tb)
    m, n = m_limit or m, n_limit or n
    has_res = res is not None
    assert not (has_res and (out_slots or epi))
    n_units = [n] + ([n // out_slots] if out_slots else []) + ([b_c] if b_slots and not tb else [])
    k_units = [k] + ([a_c] if a_slots else []) + ([b_c] if b_slots and tb else [])
    n_unit, k_unit = min(n_units), min(k_units)
    assert all(u % n_unit == 0 for u in n_units) and all(u % k_unit == 0 for u in k_units)
    epi_fn, epi_ins, epi_outs = epi if epi is not None else (None, [], [])
    tile_bytes = 4 + (2 * res.dtype.itemsize if has_res else 0)
    tile_bytes += sum(2 * e.shape[0] * e.dtype.itemsize for e in epi_ins)
    tile_bytes += sum(2 * l * jnp.dtype(dt).itemsize for l, dt in epi_outs) if epi else 2 * jnp.dtype(out_dtype).itemsize
    tm, tn, tk = _mm_tiles(m, n_unit, k_unit, tile_bytes, small_tiles_ok=not epi, tm_cap=tm_cap)
    if b_group:
        tk = b_group * b_c
        assert b_slots and tb and k % tk == 0 and (not a_slots or a_c % tk == 0)
    ni, nj, nk = m // tm, n // tn, k // tk
    dims = (((0,) if ta else (1,), (1,) if tb else (0,)), ((), ()))

    def slot_map(per, pos):
        if pos == "k_cols":
            return lambda i, j, kk: (kk // per, i, kk % per)
        if pos == "k_cols_j":
            return lambda i, j, kk: (kk // per, j, kk % per)
        if pos == "n_cols_k":
            return lambda i, j, kk: (j // per, kk, j % per)
        return lambda i, j, kk: (j // per, i, j % per)

    if a_slots:
        a_spec = pl.BlockSpec((None, tm, tk), slot_map(a_c // tk, "k_cols"))
    else:
        a_spec = pl.BlockSpec((tk, tm), lambda i, j, kk: (kk, i)) if ta else pl.BlockSpec((tm, tk), lambda i, j, kk: (i, kk))
    if b_group:
        b_spec = pl.BlockSpec((b_group, tn, b_c), lambda i, j, kk: (kk, j, 0))
    elif b_slots and tb:
        b_spec = pl.BlockSpec((None, tn, tk), slot_map(b_c // tk, "k_cols_j"))
    elif b_slots:
        b_spec = pl.BlockSpec((None, tk, tn), slot_map(b_c // tn, "n_cols_k"))
    else:
        b_spec = pl.BlockSpec((tn, tk), lambda i, j, kk: (j, kk)) if tb else pl.BlockSpec((tk, tn), lambda i, j, kk: (kk, j))
    if epi:
        lead_spec = lambda l: pl.BlockSpec((l, tm, tn), lambda i, j, kk: (0, i, j))
        o_specs = [lead_spec(l) for l, _ in epi_outs]
        o_shapes = [jax.ShapeDtypeStruct((l, m, n), dt) for l, dt in epi_outs]
    elif out_slots:
        o_specs = [pl.BlockSpec((None, tm, tn), slot_map((n // out_slots) // tn, "n_cols_i"))]
        o_shapes = [jax.ShapeDtypeStruct((out_slots, m, n // out_slots), out_dtype)]
    else:
        o_specs = [pl.BlockSpec((tm, tn), lambda i, j, kk: (i, j))]
        o_shapes = [jax.ShapeDtypeStruct((m, n), out_dtype)]
    extra_ins = ([res] if has_res else []) + list(epi_ins)
    extra_specs = ([o_specs[0]] if has_res else []) + [pl.BlockSpec((e.shape[0], tm, tn), lambda i, j, kk: (0, i, j)) for e in epi_ins]
    n_in, n_out = 2 + len(extra_ins), len(o_specs)
    c_kind, c_arrays, c_specs, c_shapes, c_scratch = _carry_parts(carry)
    nc = len(c_arrays)
    last_step = ni * nj * nk - 1

    def body(*refs):
        a_ref, b_ref = refs[0], refs[1]
        e_refs = refs[2:n_in]
        x_refs = refs[n_in:n_in + nc]
        o_refs = refs[n_in + nc:n_in + nc + n_out]
        out_refs = refs[n_in + nc + n_out:n_in + 2 * nc + n_out]
        scratch = refs[n_in + 2 * nc + n_out:]
        acc = scratch[0] if nk > 1 else None
        kk = pl.program_id(2)
        step = (pl.program_id(0) * nj + pl.program_id(1)) * nk + kk
        after = _carry_hooks(c_kind, x_refs, out_refs, scratch[-3:], step, last_step)

        def emit(val):
            if has_res:
                val = val + e_refs[0][...].astype(F32)
            if epi:
                for o_ref, parts in zip(o_refs, epi_fn(val, *[e[...] for e in e_refs])):
                    for l, v in enumerate(parts):
                        o_ref[l] = v.astype(o_ref.dtype)
            else:
                o_refs[0][...] = val.astype(out_dtype)

        if b_group:
            part = sum(lax.dot_general(a_ref[:, s * b_c:(s + 1) * b_c], b_ref[s], dims, preferred_element_type=F32)
                       for s in range(b_group))
        else:
            part = lax.dot_general(a_ref[...], b_ref[...], dims, preferred_element_type=F32)
        if nk == 1:
            emit(part)
        else:
            @pl.when(kk == 0)
            def _():
                acc[...] = part

            @pl.when((kk > 0) & (kk < nk - 1))
            def _():
                acc[...] += part

            @pl.when(kk == nk - 1)
            def _():
                emit(acc[...] + part)

        after()

    sem = ("arbitrary",) * 3 if nc else ("parallel", "parallel", "arbitrary")
    outs = pl.pallas_call(
        body, name=name,
        grid=(ni, nj, nk),
        in_specs=[a_spec, b_spec] + extra_specs + c_specs,
        out_specs=o_specs + c_specs,
        out_shape=o_shapes + c_shapes,
        scratch_shapes=([pltpu.VMEM((tm, tn), F32)] if nk > 1 else []) + c_scratch,
        compiler_params=_cparams(*sem),
    )(a, b, *extra_ins, *c_arrays)
    main = list(outs[:n_out]) if epi else outs[0]
    return (main, list(outs[n_out:])) if nc else main


def _ffn_in_fused(h2, w_s, *, carry=None, name):
    t, d = h2.shape
    n_slot, _, c = w_s.shape
    half = n_slot // 2
    tm = _tile(t, 512)
    c_kind, c_arrays, c_specs, c_shapes, c_scratch = _carry_parts(carry)
    nc = len(c_arrays)
    last_step = (t // tm) * half - 1

    def body(h_ref, wg_ref, wu_ref, *refs):
        x_refs, (gu_ref, act_ref), out_refs, sems = refs[:nc], refs[nc:nc + 2], refs[nc + 2:2 * nc + 2], refs[2 * nc + 2:]
        step = pl.program_id(0) * half + pl.program_id(1)
        after = _carry_hooks(c_kind, x_refs, out_refs, sems, step, last_step)
        h = h_ref[...]
        g = jnp.dot(h, wg_ref[...], preferred_element_type=F32)
        u = jnp.dot(h, wu_ref[...], preferred_element_type=F32)
        sg = _sigmoid(g)
        silu = g * sg
        gu_ref[0] = (u * (sg + silu - silu * sg)).astype(BF16)
        gu_ref[1] = silu.astype(BF16)
        act_ref[...] = (silu * u).astype(BF16)
        after()

    outs = pl.pallas_call(
        body, name=name,
        grid=(t // tm, half),
        in_specs=[pl.BlockSpec((tm, d), lambda i, j: (i, 0)),
                  pl.BlockSpec((None, d, c), lambda i, j: (j, 0, 0)),
                  pl.BlockSpec((None, d, c), lambda i, j: (half + j, 0, 0))] + c_specs,
        out_specs=[pl.BlockSpec((2, tm, c), lambda i, j: (0, i, j)), pl.BlockSpec((tm, c), lambda i, j: (i, j))] + c_specs,
        out_shape=[jax.ShapeDtypeStruct((2, t, half * c), BF16), jax.ShapeDtypeStruct((t, half * c), BF16)] + c_shapes,
        scratch_shapes=c_scratch,
        compiler_params=_cparams(*(("arbitrary",) * 2 if nc else ("parallel", "parallel"))),
    )(h2, w_s, w_s, *c_arrays)
    return (outs[0], outs[1], list(outs[2:])) if nc else (outs[0], outs[1])


def _rowwise(fn, ins, row_outs, acc_outs, *, rows, tb, name, carry=None):
    in_specs, args = [], []
    for spec in ins:
        kind, arr = spec[0], spec[1]
        if kind == "row":
            in_specs.append(pl.BlockSpec((tb, arr.shape[1]), lambda i: (i, 0)))
        elif kind == "win":
            width, cb = spec[2], spec[3]
            in_specs.append(pl.BlockSpec((tb, width), functools.partial(lambda i, cb: (i, cb), cb=cb)))
        else:
            in_specs.append(pl.BlockSpec(arr.shape, lambda i: (0, 0)))
        args.append(arr)
    out_specs = [pl.BlockSpec((tb, c), lambda i: (i, 0)) for c, _ in row_outs]
    out_specs += [pl.BlockSpec(shape, lambda i: (0, 0)) for shape in acc_outs]
    out_shape = [jax.ShapeDtypeStruct((rows, c), dt) for c, dt in row_outs]
    out_shape += [jax.ShapeDtypeStruct(shape, F32) for shape in acc_outs]
    n_in, n_row, n_out = len(ins), len(row_outs), len(row_outs) + len(acc_outs)
    c_kind, c_arrays, c_specs, c_shapes, c_scratch = _carry_parts(carry)
    nc = len(c_arrays)

    def body(*refs):
        after = _carry_hooks(c_kind, refs[n_in:n_in + nc], refs[n_in + nc + n_out:n_in + 2 * nc + n_out],
                             refs[n_in + 2 * nc + n_out:], pl.program_id(0), rows // tb - 1)
        vals = [r[...] for r in refs[:n_in]]
        outs = fn(*vals)
        if not isinstance(outs, (tuple, list)):
            outs = (outs,)
        out_refs = refs[n_in + nc:n_in + nc + n_out]
        for o_ref, val in zip(out_refs[:n_row], outs[:n_row]):
            o_ref[...] = val.astype(o_ref.dtype)
        first = pl.program_id(0) == 0
        for o_ref, val in zip(out_refs[n_row:], outs[n_row:]):
            @pl.when(first)
            def _(o_ref=o_ref):
                o_ref[...] = jnp.zeros_like(o_ref)
            o_ref[...] += val
        after()

    res = pl.pallas_call(
        body, name=name,
        grid=(rows // tb,),
        in_specs=in_specs + c_specs, out_specs=out_specs + c_specs, out_shape=out_shape + c_shapes,
        scratch_shapes=c_scratch,
        compiler_params=_cparams("arbitrary"),
    )(*args, *c_arrays)
    return (list(res[:n_out]), list(res[n_out:])) if nc else res


def _rms_fwd(x, g, name, carry=None):
    def fn(xv, gv):
        r = lax.rsqrt(jnp.mean(xv * xv, axis=-1, keepdims=True) + NORM_EPS)
        return (xv * r * gv,)
    res = _rowwise(fn, [("row", x), ("full", g)], [(x.shape[1], BF16)], [], rows=x.shape[0], tb=_tile(x.shape[0], 512),
                   name=name, carry=carry)
    return (res[0][0], res[1]) if carry is not None else res[0]


def _rms_bwd(x, g, dh, dres, name, out_dtype):
    d = x.shape[1]

    def fn(xv, gv, dhv, drv):
        r = lax.rsqrt(jnp.mean(xv * xv, axis=-1, keepdims=True) + NORM_EPS)
        xhat = xv * r
        dhv = dhv.astype(F32)
        dxhat = dhv * gv
        dx = drv.astype(F32) + r * (dxhat - xhat * jnp.mean(dxhat * xhat, axis=-1, keepdims=True))
        dg = jnp.sum(dhv * xhat, axis=0, keepdims=True)
        return dx, dg

    return _rowwise(fn, [("row", x), ("full", g), ("row", dh), ("row", dres)], [(d, out_dtype)], [(1, d)],
                    rows=x.shape[0], tb=_tile(x.shape[0], 256), name=name)


def _loss_head(x2, g, target, name):
    d = x2.shape[1]

    def fn(xv, gv, tv):
        r = lax.rsqrt(jnp.mean(xv * xv, axis=-1, keepdims=True) + NORM_EPS)
        xhat = xv * r
        diff = xhat * gv - tv
        loss = 0.5 * jnp.sum(jnp.mean(diff * diff, axis=-1, keepdims=True), axis=0, keepdims=True)
        dy = diff * (1.0 / d)
        dxhat = dy * gv
        dx = r * (dxhat - xhat * jnp.mean(dxhat * xhat, axis=-1, keepdims=True))
        dg = jnp.sum(dy * xhat, axis=0, keepdims=True)
        return dx, dg, jnp.broadcast_to(loss, (1, LANE))

    return _rowwise(fn, [("row", x2), ("full", g), ("row", target)], [(d, BF16)], [(1, d), (1, LANE)],
                    rows=x2.shape[0], tb=_tile(x2.shape[0], 256), name=name)


def _swiglu_bwd_tile(dact, dswiglu):
    return ((dact * dswiglu[0].astype(F32), dact * dswiglu[1].astype(F32)),)


def _mix_fwd(proj, pa, pb, d, name):
    def fn(ga, gb, av, bv):
        return (_sigmoid(ga.astype(F32)) * av.astype(F32) + _sigmoid(gb.astype(F32)) * bv.astype(F32),)
    return _rowwise(fn, [("win", proj, d, 0), ("win", proj, d, 1), ("row", pa), ("row", pb)], [(d, BF16)], [],
                    rows=pa.shape[0], tb=_tile(pa.shape[0], 512), name=name)[0]


def _mix_bwd(proj, pa, pb, dmix, d, name):
    def fn(ga, gb, av, bv, dm):
        dm = dm.astype(F32)
        sa, sb = _sigmoid(ga.astype(F32)), _sigmoid(gb.astype(F32))
        av, bv = av.astype(F32), bv.astype(F32)
        return dm * sa, dm * sb, dm * av * sa * (1.0 - sa), dm * bv * sb * (1.0 - sb)
    return _rowwise(fn, [("win", proj, d, 0), ("win", proj, d, 1), ("row", pa), ("row", pb), ("row", dmix)],
                    [(d, BF16)] * 4, [], rows=pa.shape[0], tb=_tile(pa.shape[0], 512), name=name)


def _chunk_masks(tb):
    r = lax.broadcasted_iota(jnp.int32, (tb, tb), 0)
    c = lax.broadcasted_iota(jnp.int32, (tb, tb), 1)
    same = lax.shift_right_logical(r, GLA_CHUNK_SHIFT) == lax.shift_right_logical(c, GLA_CHUNK_SHIFT)
    return same, same & (c <= r), same & (r <= c)


def _mask_bf16(mask):
    return jnp.where(mask, 1.0, 0.0).astype(BF16)


def _split_dot(mask_bf, x, terms):
    acc, rem = None, x
    for _ in range(terms):
        hi = rem.astype(BF16)
        part = jnp.dot(mask_bf, hi, preferred_element_type=F32)
        acc = part if acc is None else acc + part
        rem = rem - hi.astype(F32)
    return acc


def _gla_decay(al, wa2, ba2, same_bf, causal_bf):
    z = jnp.dot(al.astype(BF16), wa2, preferred_element_type=F32) + ba2
    la = (jnp.minimum(z, 0.0) - jnp.log(1.0 + jnp.exp(-jnp.abs(z)))) * (1.0 / GLA_TAU)
    bc = _split_dot(causal_bf, la, 3)
    bl = _split_dot(same_bf, la, 3)
    return z, bc, bl


def _dot_t(a, b, ca, cb):
    return lax.dot_general(a, b, (((ca,), (cb,)), ((), ())), preferred_element_type=F32)


def _gla_fwd(proj, alow, wa2, ba2, ghn, *, dk, dv, name, carry=None):
    t = proj.shape[0]
    tb = min(GLA_BLOCK, t)
    nch = tb // GLA_CHUNK
    hk, hv = dk // GLA_HEADS, dv // GLA_HEADS
    scale = hk ** -0.5
    v_cb, r_cb = (8 * dk) // dv, (8 * dk) // dv + 1
    q_cb, k_cb = (8 * dk + 2 * dv) // dk, (8 * dk + 2 * dv) // dk + 1
    c_kind, c_arrays, c_specs, c_shapes, c_scratch = _carry_parts(carry)
    nc = len(c_arrays)

    def body(q_ref, k_ref, v_ref, r_ref, al_ref, wa2_ref, ba2_ref, ghn_ref, *refs):
        x_refs, (oa_ref, opre_ref, s_ref), out_refs = refs[:nc], refs[nc:nc + 3], refs[nc + 3:2 * nc + 3]
        st_scr, sems = refs[2 * nc + 3], refs[2 * nc + 4:]
        after = _carry_hooks(c_kind, x_refs, out_refs, sems, pl.program_id(0), t // tb - 1)

        @pl.when(pl.program_id(0) == 0)
        def _():
            st_scr[...] = jnp.zeros_like(st_scr)

        same, causal, _ = _chunk_masks(tb)
        same_bf, causal_bf = _mask_bf16(same), _mask_bf16(causal)
        _, bc, bl = _gla_decay(al_ref[...], wa2_ref[...], ba2_ref[...], same_bf, causal_bf)
        q = q_ref[...].astype(F32) * scale
        k = k_ref[...].astype(F32)
        qd = (q * jnp.exp(bc)).astype(BF16)
        ki = (k * jnp.exp(-bc)).astype(BF16)
        ks = (k * jnp.exp(bl - bc)).astype(BF16)
        dl = jnp.exp(bl)
        ksls = [slice(h * hk, (h + 1) * hk) for h in range(GLA_HEADS)]
        vsls = [slice(h * hv, (h + 1) * hv) for h in range(GLA_HEADS)]
        v_hs = [v_ref[:, vsl] for vsl in vsls]
        o_intras = []
        for ksl, v_h in zip(ksls, v_hs):
            sc = jnp.where(causal, _dot_t(qd[:, ksl], ki[:, ksl], 1, 1), 0.0)
            o_intras.append(jnp.dot(sc.astype(BF16), v_h, preferred_element_type=F32))
        for c in range(nch):
            rows = slice(c * GLA_CHUNK, (c + 1) * GLA_CHUNK)
            for h, (ksl, vsl) in enumerate(zip(ksls, vsls)):
                st = st_scr[h]
                s_ref[c, h] = st.astype(BF16)
                opre_ref[rows, vsl] = o_intras[h][rows] + _dot_t(qd[rows, ksl], st.astype(BF16), 1, 1)
                st_scr[h] = dl[c * GLA_CHUNK:c * GLA_CHUNK + 1, ksl] * st + _dot_t(v_hs[h][rows], ks[rows, ksl], 0, 0)
        for h in range(GLA_HEADS):
            vsl = slice(h * hv, (h + 1) * hv)
            o = opre_ref[:, vsl]
            rs = lax.rsqrt(jnp.mean(o * o, axis=-1, keepdims=True) + NORM_EPS)
            rv = r_ref[:, vsl].astype(F32)
            oa_ref[:, vsl] = (rv * _sigmoid(rv) * (o * rs * ghn_ref[:, vsl])).astype(BF16)
        after()

    nchunks = t // GLA_CHUNK
    outs = pl.pallas_call(
        body, name=name,
        grid=(t // tb,),
        in_specs=[
            pl.BlockSpec((tb, dk), lambda i: (i, q_cb)),
            pl.BlockSpec((tb, dk), lambda i: (i, k_cb)),
            pl.BlockSpec((tb, dv), lambda i: (i, v_cb)),
            pl.BlockSpec((tb, dv), lambda i: (i, r_cb)),
            pl.BlockSpec((tb, LANE), lambda i: (i, 0)),
            pl.BlockSpec(wa2.shape, lambda i: (0, 0)),
            pl.BlockSpec(ba2.shape, lambda i: (0, 0)),
            pl.BlockSpec(ghn.shape, lambda i: (0, 0)),
        ] + c_specs,
        out_specs=[
            pl.BlockSpec((tb, dv), lambda i: (i, 0)),
            pl.BlockSpec((tb, dv), lambda i: (i, 0)),
            pl.BlockSpec((nch, GLA_HEADS, hv, hk), lambda i: (i, 0, 0, 0)),
        ] + c_specs,
        out_shape=[
            jax.ShapeDtypeStruct((t, dv), BF16),
            jax.ShapeDtypeStruct((t, dv), F32),
            jax.ShapeDtypeStruct((nchunks, GLA_HEADS, hv, hk), BF16),
        ] + c_shapes,
        scratch_shapes=[pltpu.VMEM((GLA_HEADS, hv, hk), F32)] + c_scratch,
        compiler_params=_cparams("arbitrary"),
    )(proj, proj, proj, proj, alow, wa2, ba2, ghn, *c_arrays)
    return (outs[0], outs[1], outs[2], list(outs[3:])) if nc else tuple(outs)


def _gla_bwd(proj, alow, wa2, ba2, ghn, opre, states, doa, *, dk, dv, name):
    t = proj.shape[0]
    tb = min(GLA_BLOCK, t)
    nb = t // tb
    nch = tb // GLA_CHUNK
    hk, hv = dk // GLA_HEADS, dv // GLA_HEADS
    scale = hk ** -0.5
    v_cb, r_cb = (8 * dk) // dv, (8 * dk) // dv + 1
    q_cb, k_cb = (8 * dk + 2 * dv) // dk, (8 * dk + 2 * dv) // dk + 1

    def body(q_ref, k_ref, v_ref, r_ref, al_ref, wa2_ref, ba2_ref, ghn_ref, opre_ref, s_ref, doa_ref,
             dq_ref, dk_ref, dv_ref, dr_ref, dal_ref, dwa2_ref, dba2_ref, dghn_ref,
             dst_scr, dqd_scr, dki_scr, dks_scr, ddl_scr):
        @pl.when(pl.program_id(0) == 0)
        def _():
            dst_scr[...] = jnp.zeros_like(dst_scr)
            dwa2_ref[...] = jnp.zeros_like(dwa2_ref)
            dba2_ref[...] = jnp.zeros_like(dba2_ref)
            dghn_ref[...] = jnp.zeros_like(dghn_ref)

        same, causal, anti = _chunk_masks(tb)
        same_bf, causal_bf, anti_bf = _mask_bf16(same), _mask_bf16(causal), _mask_bf16(anti)
        al = al_ref[...]
        wa2v = wa2_ref[...]
        z, bc, bl = _gla_decay(al, wa2v, ba2_ref[...], same_bf, causal_bf)
        e_bc, e_nbc, e_st = jnp.exp(bc), jnp.exp(-bc), jnp.exp(bl - bc)
        q = q_ref[...].astype(F32) * scale
        k = k_ref[...].astype(F32)
        qd_f, ki_f, ks_f = q * e_bc, k * e_nbc, k * e_st
        qd, ki, ks = qd_f.astype(BF16), ki_f.astype(BF16), ks_f.astype(BF16)
        dl = jnp.exp(bl)
        per_head = []
        for h in range(GLA_HEADS):
            ksl = slice(h * hk, (h + 1) * hk)
            vsl = slice(h * hv, (h + 1) * hv)
            o = opre_ref[:, vsl]
            rs = lax.rsqrt(jnp.mean(o * o, axis=-1, keepdims=True) + NORM_EPS)
            ohat = o * rs
            g_h = ghn_ref[:, vsl]
            rv = r_ref[:, vsl].astype(F32)
            sg = _sigmoid(rv)
            d_oa = doa_ref[:, vsl].astype(F32)
            don = d_oa * (rv * sg)
            dr_ref[:, vsl] = (d_oa * (ohat * g_h) * (sg * (1.0 + rv * (1.0 - sg)))).astype(BF16)
            dghn_ref[:, vsl] += jnp.sum(don * ohat, axis=0, keepdims=True)
            dohat = don * g_h
            do_f = rs * (dohat - ohat * jnp.mean(dohat * ohat, axis=-1, keepdims=True))
            do = do_f.astype(BF16)
            v_h = v_ref[:, vsl]
            p = jnp.where(causal, _dot_t(do, v_h, 1, 1), 0.0).astype(BF16)
            dqd_intra = jnp.dot(p, ki[:, ksl], preferred_element_type=F32)
            dki_scr[:, ksl] = _dot_t(p, qd[:, ksl], 0, 0)
            sc = jnp.where(causal, _dot_t(qd[:, ksl], ki[:, ksl], 1, 1), 0.0).astype(BF16)
            dv_intra = _dot_t(sc, do, 0, 0)
            per_head.append((ksl, vsl, v_h, do, dqd_intra, dv_intra))
        for c in reversed(range(nch)):
            rows = slice(c * GLA_CHUNK, (c + 1) * GLA_CHUNK)
            for h, (ksl, vsl, v_h, do, dqd_intra, dv_intra) in enumerate(per_head):
                dst = dst_scr[h]
                st = s_ref[c, h].astype(F32)
                dst_bf = dst.astype(BF16)
                dv_ref[rows, vsl] = (dv_intra[rows] + _dot_t(ks[rows, ksl], dst_bf, 1, 1)).astype(BF16)
                dks_scr[rows, ksl] = jnp.dot(v_h[rows], dst_bf, preferred_element_type=F32)
                dl_c = dl[c * GLA_CHUNK:c * GLA_CHUNK + 1, ksl]
                ddl = jnp.sum(dst * st, axis=0, keepdims=True) * dl_c
                ddl_scr[rows, ksl] = jnp.broadcast_to(ddl, (GLA_CHUNK, hk))
                dqd_scr[rows, ksl] = dqd_intra[rows] + jnp.dot(do[rows], st.astype(BF16), preferred_element_type=F32)
                dst_scr[h] = dl_c * dst + _dot_t(do[rows], qd[rows, ksl], 0, 0)
        dqd, dki, dks = dqd_scr[...], dki_scr[...], dks_scr[...]
        dq_ref[...] = (dqd * (scale * e_bc)).astype(BF16)
        dk_ref[...] = (dki * e_nbc + dks * e_st).astype(BF16)
        dks_ks = dks * ks_f
        dbc = dqd * qd_f - dki * ki_f - dks_ks
        dla = _split_dot(anti_bf, dbc, 2) + _split_dot(same_bf, dks_ks, 2) + ddl_scr[...]
        dz = (dla * (1.0 / GLA_TAU) * (1.0 - _sigmoid(z)))
        dz_bf = dz.astype(BF16)
        dal_ref[...] = _dot_t(dz_bf, wa2v, 1, 1).astype(BF16)
        dwa2_ref[...] += _dot_t(al.astype(BF16), dz_bf, 0, 0)
        dba2_ref[...] += jnp.sum(dz, axis=0, keepdims=True)

    rev = lambda i: nb - 1 - i
    return pl.pallas_call(
        body, name=name,
        grid=(nb,),
        in_specs=[
            pl.BlockSpec((tb, dk), lambda i: (rev(i), q_cb)),
            pl.BlockSpec((tb, dk), lambda i: (rev(i), k_cb)),
            pl.BlockSpec((tb, dv), lambda i: (rev(i), v_cb)),
            pl.BlockSpec((tb, dv), lambda i: (rev(i), r_cb)),
            pl.BlockSpec((tb, LANE), lambda i: (rev(i), 0)),
            pl.BlockSpec(wa2.shape, lambda i: (0, 0)),
            pl.BlockSpec(ba2.shape, lambda i: (0, 0)),
            pl.BlockSpec(ghn.shape, lambda i: (0, 0)),
            pl.BlockSpec((tb, dv), lambda i: (rev(i), 0)),
            pl.BlockSpec((nch, GLA_HEADS, hv, hk), lambda i: (rev(i), 0, 0, 0)),
            pl.BlockSpec((tb, dv), lambda i: (rev(i), 0)),
        ],
        out_specs=[
            pl.BlockSpec((tb, dk), lambda i: (rev(i), 0)),
            pl.BlockSpec((tb, dk), lambda i: (rev(i), 0)),
            pl.BlockSpec((tb, dv), lambda i: (rev(i), 0)),
            pl.BlockSpec((tb, dv), lambda i: (rev(i), 0)),
            pl.BlockSpec((tb, LANE), lambda i: (rev(i), 0)),
            pl.BlockSpec(wa2.shape, lambda i: (0, 0)),
            pl.BlockSpec(ba2.shape, lambda i: (0, 0)),
            pl.BlockSpec(ghn.shape, lambda i: (0, 0)),
        ],
        out_shape=[
            jax.ShapeDtypeStruct((t, dk), BF16),
            jax.ShapeDtypeStruct((t, dk), BF16),
            jax.ShapeDtypeStruct((t, dv), BF16),
            jax.ShapeDtypeStruct((t, dv), BF16),
            jax.ShapeDtypeStruct((t, LANE), BF16),
            jax.ShapeDtypeStruct(wa2.shape, F32),
            jax.ShapeDtypeStruct(ba2.shape, F32),
            jax.ShapeDtypeStruct(ghn.shape, F32),
        ],
        scratch_shapes=[pltpu.VMEM((GLA_HEADS, hv, hk), F32)] + [pltpu.VMEM((tb, dk), F32)] * 4,
        compiler_params=_cparams("arbitrary"),
    )(proj, proj, proj, proj, alow, wa2, ba2, ghn, opre, states, doa)


def _s5_tables(lam_re, lam_im, log_dt, b_re, b_im, c_re, c_im):
    hp = lax.Precision.HIGHEST
    g, p = lam_re.shape
    ln = S5_L
    dt = jnp.exp(log_dt)[:, None]
    lr, li = lam_re, lam_im
    mag = jnp.exp(lr * dt)
    ar, ai = mag * jnp.cos(li * dt), mag * jnp.sin(li * dt)
    den = lr * lr + li * li
    am1 = ar - 1.0
    f_re = ((am1 * lr + ai * li) / den)[..., None]
    f_im = ((ai * lr - am1 * li) / den)[..., None]
    bb_re = f_re * b_re - f_im * b_im
    bb_im = f_re * b_im + f_im * b_re
    j = jnp.arange(ln + 1, dtype=F32)[None, :, None]
    pm = jnp.exp(j * (lr * dt)[:, None, :])
    ang = j * (li * dt)[:, None, :]
    pw_re, pw_im = pm * jnp.cos(ang), pm * jnp.sin(ang)
    cp_re = c_re[:, None] * pw_re[:, :, None, :] - c_im[:, None] * pw_im[:, :, None, :]
    cp_im = c_re[:, None] * pw_im[:, :, None, :] + c_im[:, None] * pw_re[:, :, None, :]
    kj = (jnp.einsum("gjcp,gpd->gjcd", cp_re[:, :ln], bb_re, precision=hp)
          - jnp.einsum("gjcp,gpd->gjcd", cp_im[:, :ln], bb_im, precision=hp))
    eye = jnp.eye(S5_TILE_G, dtype=F32)
    nt = g // S5_TILE_G
    k8 = jnp.einsum("jglcd,gh->jlgdhc", kj.reshape(nt, S5_TILE_G, ln, S5_GC, S5_GC), eye).reshape(nt, ln, LANE, LANE)
    rp_re, rp_im = pw_re[:, ln - 1::-1], pw_im[:, ln - 1::-1]
    bbt_re, bbt_im = bb_re.transpose(0, 2, 1)[:, None], bb_im.transpose(0, 2, 1)[:, None]
    bst = jnp.stack([rp_re[:, :, None, :] * bbt_re - rp_im[:, :, None, :] * bbt_im,
                     rp_re[:, :, None, :] * bbt_im + rp_im[:, :, None, :] * bbt_re], axis=3)
    bc = bst.reshape(nt, S5_TILE_G, ln, S5_GC, 2 * p).transpose(0, 2, 1, 3, 4).reshape(nt, ln, LANE, 2 * p)
    cst = jnp.stack([cp_re[:, 1:], -cp_im[:, 1:]], axis=2)
    cc = cst.reshape(nt, S5_TILE_G, ln, 2, S5_GC, p).transpose(0, 2, 3, 5, 1, 4).reshape(nt, ln, 2 * p, LANE)
    a8 = jnp.stack([jnp.concatenate([pw_re[:, ln], pw_re[:, ln]], axis=-1),
                    jnp.concatenate([-pw_im[:, ln], pw_im[:, ln]], axis=-1)], axis=1)
    a8 = a8.reshape(nt, S5_TILE_G, 2, 2 * p).transpose(0, 2, 1, 3).reshape(nt, 2, S5_TILE_G * 2 * p)
    return k8, bc, cc, a8


def _swap_re_im(x):
    w = x.shape[1]
    if w == LANE:
        return pltpu.roll(x, LANE // 2, 1)
    first_half = (lax.broadcasted_iota(jnp.int32, x.shape, 1) & (LANE // 2)) == 0
    return jnp.where(first_half, pltpu.roll(x, w - LANE // 2, 1), pltpu.roll(x, LANE // 2, 1))


def _s5_expand(bc, cc, w):
    reps = w // LANE
    mask_b = (lax.broadcasted_iota(jnp.int32, (LANE, w), 0) // S5_GC) == (lax.broadcasted_iota(jnp.int32, (LANE, w), 1) // LANE)
    mask_c = (lax.broadcasted_iota(jnp.int32, (w, LANE), 0) // LANE) == (lax.broadcasted_iota(jnp.int32, (w, LANE), 1) // S5_GC)
    b8 = None if bc is None else jnp.where(mask_b, jnp.concatenate([bc] * reps, axis=1), jnp.zeros((), bc.dtype))
    c8 = None if cc is None else jnp.where(mask_c, jnp.concatenate([cc] * reps, axis=0), jnp.zeros((), cc.dtype))
    return b8, c8, mask_b, mask_c


def _state_scan(v, pr, pi, reverse):
    n = v.shape[0]
    row = lax.broadcasted_iota(jnp.int32, v.shape, 0)
    z, s = v, 1
    while s < n:
        if reverse:
            zs = jnp.where(row < n - s, pltpu.roll(z, n - s, 0), 0.0)
        else:
            zs = jnp.where(row >= s, pltpu.roll(z, s, 0), 0.0)
        z = z + zs * pr + _swap_re_im(zs) * pi
        pr, pi = pr * pr - pi * pi, 2.0 * pr * pi
        s *= 2
    return z


def _s5_fwd(proj, u_cb, k8, bc, cc, a8, name):
    t = proj.shape[0]
    nt, ln = bc.shape[:2]
    w = a8.shape[2]
    nc = t // ln
    rb = min(t, S5_ROW_BLOCK)

    def body(u_ref, k_ref, b_ref, c_ref, a_ref, y_ref, x_ref, uf_ref):
        uf_ref[...] = u_ref[...].astype(F32)
        pos = lax.broadcasted_iota(jnp.int32, (rb, LANE), 0) & (ln - 1)
        for r0 in range(0, t, rb):
            u = uf_ref[r0:r0 + rb, :]
            acc = jnp.dot(u.astype(BF16), k_ref[0], preferred_element_type=F32)
            for lag in range(1, ln):
                us = jnp.where(pos >= lag, pltpu.roll(u, lag, 0), 0.0).astype(BF16)
                acc = acc + jnp.dot(us, k_ref[lag], preferred_element_type=F32)
            y_ref[r0:r0 + rb, :] = acc
        v = None
        for s in range(ln):
            part = jnp.dot(uf_ref[pl.ds(s, nc, stride=ln), :].astype(BF16), _s5_expand(b_ref[s], None, w)[0],
                           preferred_element_type=F32)
            v = part if v is None else v + part
        z = _state_scan(v, a_ref[0:1, :], a_ref[1:2, :], reverse=False)
        row = lax.broadcasted_iota(jnp.int32, z.shape, 0)
        x = jnp.where(row >= 1, pltpu.roll(z, 1, 0), 0.0)
        x_ref[...] = x
        x_bf = x.astype(BF16)
        for tt in range(ln):
            y_ref[pl.ds(tt, nc, stride=ln), :] += jnp.dot(x_bf, _s5_expand(None, c_ref[tt], w)[1], preferred_element_type=F32)

    tile = lambda shape: pl.BlockSpec((None,) + shape, lambda j: (j,) + (0,) * len(shape))
    return pl.pallas_call(
        body, name=name, grid=(nt,),
        in_specs=[pl.BlockSpec((t, LANE), lambda j: (0, u_cb + j)), tile((ln, LANE, LANE)), tile(bc.shape[1:]),
                  tile(cc.shape[1:]), tile((2, w))],
        out_specs=[pl.BlockSpec((t, LANE), lambda j: (0, j)), tile((nc, w))],
        out_shape=[jax.ShapeDtypeStruct((t, nt * LANE), F32), jax.ShapeDtypeStruct((nt, nc, w), F32)],
        scratch_shapes=[pltpu.VMEM((t, LANE), F32)],
        compiler_params=_cparams("parallel"),
    )(proj, k8, bc, cc, a8)


def _s5_bwd_data(dy, x_st, k8, bc, cc, a8, name):
    t = dy.shape[0]
    nt, ln = bc.shape[:2]
    w = a8.shape[2]
    nc = t // ln
    rb = min(t, S5_ROW_BLOCK)

    def body(dy_ref, x_ref, k_ref, b_ref, c_ref, a_ref, du_ref, dv_ref, da_ref, dyf_ref, duf_ref):
        dyf_ref[...] = dy_ref[...].astype(F32)
        pos = lax.broadcasted_iota(jnp.int32, (rb, LANE), 0) & (ln - 1)
        for r0 in range(0, t, rb):
            g = dyf_ref[r0:r0 + rb, :]
            acc = _dot_t(g.astype(BF16), k_ref[0], 1, 1)
            for lag in range(1, ln):
                gs = jnp.where(pos < ln - lag, pltpu.roll(g, rb - lag, 0), 0.0).astype(BF16)
                acc = acc + _dot_t(gs, k_ref[lag], 1, 1)
            duf_ref[r0:r0 + rb, :] = acc
        gx = None
        for tt in range(ln):
            part = _dot_t(dyf_ref[pl.ds(tt, nc, stride=ln), :].astype(BF16), _s5_expand(None, c_ref[tt], w)[1], 1, 1)
            gx = part if gx is None else gx + part
        rtot = _state_scan(gx, a_ref[0:1, :], -a_ref[1:2, :], reverse=True)
        row = lax.broadcasted_iota(jnp.int32, rtot.shape, 0)
        dv = jnp.where(row < nc - 1, pltpu.roll(rtot, nc - 1, 0), 0.0)
        dv_ref[...] = dv
        dv_bf = dv.astype(BF16)
        for s in range(ln):
            duf_ref[pl.ds(s, nc, stride=ln), :] += _dot_t(dv_bf, _s5_expand(b_ref[s], None, w)[0], 1, 1)
        du_ref[...] = duf_ref[...].astype(BF16)
        x = x_ref[...]
        da_ref[0:1, :] = jnp.sum(dv * x, axis=0, keepdims=True)
        da_ref[1:2, :] = jnp.sum(dv * _swap_re_im(x), axis=0, keepdims=True)

    tile = lambda shape: pl.BlockSpec((None,) + shape, lambda j: (j,) + (0,) * len(shape))
    return pl.pallas_call(
        body, name=name, grid=(nt,),
        in_specs=[pl.BlockSpec((t, LANE), lambda j: (0, j)), tile((nc, w)), tile((ln, LANE, LANE)), tile(bc.shape[1:]),
                  tile(cc.shape[1:]), tile((2, w))],
        out_specs=[pl.BlockSpec((t, LANE), lambda j: (0, j)), tile((nc, w)), tile((2, w))],
        out_shape=[jax.ShapeDtypeStruct((t, nt * LANE), BF16), jax.ShapeDtypeStruct((nt, nc, w), F32),
                   jax.ShapeDtypeStruct((nt, 2, w), F32)],
        scratch_shapes=[pltpu.VMEM((t, LANE), F32), pltpu.VMEM((t, LANE), F32)],
        compiler_params=_cparams("parallel"),
    )(dy, x_st, k8, bc, cc, a8)


def _s5_bwd_tables(dy, proj, u_cb, x_st, dv, ln, name):
    t = dy.shape[0]
    nt, nc, w = x_st.shape
    rb = min(t, S5_ROW_BLOCK)

    def body(dy_ref, u_ref, x_ref, dv_ref, dk_ref, db_ref, dc_ref, dyf_ref, uf_ref):
        s = pl.program_id(1)

        @pl.when(s == 0)
        def _():
            dyf_ref[...] = dy_ref[...].astype(F32)
            uf_ref[...] = u_ref[...].astype(F32)
            pos = lax.broadcasted_iota(jnp.int32, (rb, LANE), 0) & (ln - 1)
            for r0 in range(0, t, rb):
                u, g_bf = uf_ref[r0:r0 + rb, :], dy_ref[r0:r0 + rb, :]
                for lag in range(ln):
                    us = u if lag == 0 else jnp.where(pos >= lag, pltpu.roll(u, lag, 0), 0.0)
                    part = _dot_t(us.astype(BF16), g_bf, 0, 0)
                    if r0 == 0:
                        dk_ref[lag] = part
                    else:
                        dk_ref[lag] += part

        rows = pl.ds(s, nc, stride=ln)
        _, _, mask_b, mask_c = _s5_expand(None, None, w)
        db = jnp.where(mask_b, _dot_t(uf_ref[rows, :].astype(BF16), dv_ref[...].astype(BF16), 0, 0), 0.0)
        dc = jnp.where(mask_c, _dot_t(x_ref[...].astype(BF16), dyf_ref[rows, :].astype(BF16), 0, 0), 0.0)
        db_ref[...] = sum(db[:, h * LANE:(h + 1) * LANE] for h in range(w // LANE))
        dc_ref[...] = sum(dc[h * LANE:(h + 1) * LANE, :] for h in range(w // LANE))

    tile = lambda shape: pl.BlockSpec((None,) + shape, lambda j, s: (j,) + (0,) * len(shape))
    per_s = lambda shape: pl.BlockSpec((None, None) + shape, lambda j, s: (j, s, 0, 0))
    return pl.pallas_call(
        body, name=name, grid=(nt, ln),
        in_specs=[pl.BlockSpec((t, LANE), lambda j, s: (0, j)), pl.BlockSpec((t, LANE), lambda j, s: (0, u_cb + j)),
                  tile((nc, w)), tile((nc, w))],
        out_specs=[tile((ln, LANE, LANE)), per_s((LANE, LANE)), per_s((LANE, LANE))],
        out_shape=[jax.ShapeDtypeStruct((nt, ln, LANE, LANE), F32)] * 3,
        scratch_shapes=[pltpu.VMEM((t, LANE), F32), pltpu.VMEM((t, LANE), F32)],
        compiler_params=_cparams("parallel", "arbitrary"),
    )(dy, proj, x_st, dv)


def _gelu_parts(y):
    inner = GELU_C * (y + GELU_A * y * y * y)
    th = jnp.tanh(inner)
    return th, 0.5 * y * (1.0 + th)


def _s5_post_fwd(y_raw, proj, u_cb, s5d, wglu, bglu, name):
    w = y_raw.shape[1]

    def fn(yr, u, dsk, wg, bg):
        y = yr + dsk * u.astype(F32)
        _, h = _gelu_parts(y)
        gl = jnp.dot(h.astype(BF16), wg, preferred_element_type=F32) + bg
        return (h * _sigmoid(gl),)

    return _rowwise(fn, [("row", y_raw), ("win", proj, w, u_cb), ("full", s5d), ("full", wglu), ("full", bglu)],
                    [(w, BF16)], [], rows=y_raw.shape[0], tb=_tile(y_raw.shape[0], 512), name=name)[0]


def _s5_post_bwd(y_raw, proj, u_cb, s5d, wglu, bglu, dob, name):
    w = y_raw.shape[1]

    def fn(yr, u, dsk, wg, bg, dov):
        u = u.astype(F32)
        dov = dov.astype(F32)
        y = yr + dsk * u
        th, h = _gelu_parts(y)
        h_bf = h.astype(BF16)
        gl = jnp.dot(h_bf, wg, preferred_element_type=F32) + bg
        sg = _sigmoid(gl)
        dgl = dov * h * sg * (1.0 - sg)
        dgl_bf = dgl.astype(BF16)
        dh = dov * sg + _dot_t(dgl_bf, wg, 1, 1)
        dgelu = 0.5 * (1.0 + th) + 0.5 * y * (1.0 - th * th) * GELU_C * (1.0 + 3.0 * GELU_A * y * y)
        dy = dh * dgelu
        return (dy, dy * dsk,
                _dot_t(h_bf, dgl_bf, 0, 0), jnp.sum(dgl, axis=0, keepdims=True), jnp.sum(dy * u, axis=0, keepdims=True))

    return _rowwise(fn, [("row", y_raw), ("win", proj, w, u_cb), ("full", s5d), ("full", wglu), ("full", bglu), ("row", dob)],
                    [(w, BF16), (w, BF16)], [(w, w), (1, w), (1, w)], rows=y_raw.shape[0], tb=_tile(y_raw.shape[0], 512), name=name)


def _adamw(w, g, m, v, name):
    _, rows, cols = w.shape
    tr, tc = (_tile(rows, 256, align=16), cols) if rows % 16 == 0 else (rows, _tile(cols, 256))
    slots = isinstance(g, (list, tuple))
    gs = list(g) if slots else [g]
    c1 = 1.0 - ADAM_B1 ** ADAM_STEP
    c2 = 1.0 - ADAM_B2 ** ADAM_STEP

    def body(w_ref, m_ref, v_ref, *refs):
        g_refs, out_refs = refs[:len(gs)], refs[len(gs):]
        if slots:
            parts = [g_ref[s].astype(F32) for g_ref in g_refs for s in range(g_ref.shape[0])]
            gv = parts[0]
            for p in parts[1:]:
                gv = gv + p
            out_refs[0][...] = gv
        else:
            gv = g_refs[0][...]
        d_ref, nm_ref, nv_ref = out_refs[-3:]
        nm = ADAM_B1 * m_ref[...] + (1.0 - ADAM_B1) * gv
        nv = ADAM_B2 * v_ref[...] + (1.0 - ADAM_B2) * (gv * gv)
        d_ref[...] = -ADAM_LR * ((nm / c1) / (jnp.sqrt(nv / c2) + ADAM_EPS) + ADAM_WD * w_ref[...])
        nm_ref[...] = nm
        nv_ref[...] = nv

    spec = pl.BlockSpec((None, tr, tc), lambda i, j: (0, i, j))
    g_specs = [pl.BlockSpec((a.shape[0], tr, tc), lambda i, j: (0, i, j)) for a in gs] if slots else [pl.BlockSpec((tr, tc), lambda i, j: (i, j))]
    n_out = 4 if slots else 3
    return pl.pallas_call(
        body, name=name, grid=(rows // tr, cols // tc),
        in_specs=[spec, spec, spec] + g_specs, out_specs=[spec] * n_out,
        out_shape=[jax.ShapeDtypeStruct((1, rows, cols), F32)] * n_out,
        compiler_params=_cparams("parallel", "parallel"),
    )(w, m, v, *gs)


def _slot_sum(x, name):
    _, rows, cols = x.shape
    if rows % 8 == 0:
        tr, tc = _tile(rows, 512, align=8), cols
    else:
        tr, tc = rows, _tile(cols, 256)

    def body(x_ref, o_ref):
        acc = x_ref[0].astype(F32)
        for s in range(1, N_DEV):
            acc = acc + x_ref[s].astype(F32)
        o_ref[...] = acc

    return pl.pallas_call(
        body, name=name, grid=(rows // tr, cols // tc),
        in_specs=[pl.BlockSpec((N_DEV, tr, tc), lambda i, j: (0, i, j))],
        out_specs=pl.BlockSpec((tr, tc), lambda i, j: (i, j)),
        out_shape=jax.ShapeDtypeStruct((rows, cols), F32),
        compiler_params=_cparams("parallel", "parallel"),
    )(x)


_REST = (("w_a2", 1), ("w_glu", 0), ("w_branch_a", 1), ("w_branch_b", 1), ("w_out", 0), ("w_ffn_in", 1), ("w_ffn_out", 0))
_SMALL = ("norm1_g", "b_a2", "gla_norm_g", "lam_re", "lam_im", "log_dt", "s5_b_re", "s5_b_im", "s5_c_re", "s5_c_im",
          "s5_d", "b_glu", "norm2_g", "final_norm_g")
_ORDER = ("norm1_g", "w_in", "w_a2", "b_a2", "gla_norm_g", "lam_re", "lam_im", "log_dt", "s5_b_re", "s5_b_im", "s5_c_re",
          "s5_c_im", "s5_d", "w_glu", "b_glu", "w_branch_a", "w_branch_b", "w_out", "norm2_g", "w_ffn_in", "w_ffn_out", "final_norm_g")


def _join_slots(slots, axis):
    _, r, c = slots.shape
    if axis == 0:
        return slots.reshape(N_DEV * r, c)
    return slots.transpose(1, 0, 2).reshape(r, N_DEV * c)


def _to_slots(full, axis):
    r, c = full.shape
    if axis == 0:
        return full.reshape(N_DEV, r // N_DEV, c)
    return full.reshape(r, N_DEV, c // N_DEV).transpose(1, 0, 2)


def _local_step(x, target, w_in_t, small, rest):
    t, d = x.shape
    dk, dv, s5w = d // 4, d // 2, d // 4
    dist = not isinstance(rest, dict)
    if dist:
        h1, (w_in_slots,) = _rms_fwd(x, small["norm1_g"], "norm1_fwd", carry=("ag", [w_in_t]))
        w_in_t = w_in_slots.reshape(-1, d)
    else:
        h1 = _rms_fwd(x, small["norm1_g"], "norm1_fwd")
    o_q, o_k, o_v, o_r, o_al = 0, dk, 2 * dk, 2 * dk + dv, 2 * dk + 2 * dv
    o_u = o_al + GLA_RANK
    o_ga, o_gb = o_u + s5w, o_u + s5w + d
    rows = lambda a, o, n: a[o:o + n]
    w_al_t = jnp.pad(rows(w_in_t, o_al, GLA_RANK), ((0, LANE - GLA_RANK), (0, 0)))
    w_ext_t = jnp.concatenate([rows(w_in_t, o_ga, d), rows(w_in_t, o_gb, d), rows(w_in_t, o_v, dv), rows(w_in_t, o_r, dv),
                               rows(w_in_t, o_q, dk), rows(w_in_t, o_k, dk), rows(w_in_t, o_u, s5w), w_al_t], axis=0)
    n_main = 2 * d + 2 * dv + 2 * dk + s5w
    u_cb = (2 * d + 2 * dv + 2 * dk) // s5w

    if dist:
        proj, (a2_s, glu_s, w_ffn_in_s) = _mm(h1, w_ext_t, tb=True, n_limit=n_main, out_dtype=BF16,
                                              carry=("ag", [rest[0], rest[1], rest[5]]), name="in_proj")
        w = {"w_a2": _join_slots(a2_s, 1), "w_glu": _join_slots(glu_s, 0)}
    else:
        proj = _mm(h1, w_ext_t, tb=True, n_limit=n_main, out_dtype=BF16, name="in_proj")
        w = rest
        w_ffn_in_s = _to_slots(rest["w_ffn_in"], 1)
    wa2 = jnp.pad(w["w_a2"], ((0, LANE - GLA_RANK), (0, 0)))
    alow = _mm(h1, w_al_t, tb=True, out_dtype=BF16, name="in_proj_gate_rank")
    if dist:
        o_a, o_pre, states, got = _gla_fwd(proj, alow, wa2, small["b_a2"], small["gla_norm_g"], dk=dk, dv=dv, name="gla_fwd",
                                           carry=("ag", rest[2:5]))
        w.update({n: _join_slots(g, ax) for (n, ax), g in zip(_REST[2:5], got)})
    else:
        o_a, o_pre, states = _gla_fwd(proj, alow, wa2, small["b_a2"], small["gla_norm_g"], dk=dk, dv=dv, name="gla_fwd")

    s5_params = (small["lam_re"], small["lam_im"], small["log_dt"][0], small["s5_b_re"], small["s5_b_im"],
                 small["s5_c_re"], small["s5_c_im"])
    (k8, bc, cc, a8), tables_vjp = jax.vjp(_s5_tables, *s5_params)
    k8_bf, b8_bf, c8_bf = k8.astype(BF16), bc.astype(BF16), cc.astype(BF16)
    u_lane_cb = u_cb * s5w // LANE
    y_raw, x_st = _s5_fwd(proj, u_lane_cb, k8_bf, b8_bf, c8_bf, a8, "s5_scan_fwd")
    o_b = _s5_post_fwd(y_raw, proj, u_cb, small["s5_d"], w["w_glu"], small["b_glu"], "s5_post_fwd")

    pa = _mm(o_a, w["w_branch_a"], out_dtype=BF16, name="branch_a")
    pb = _mm(o_b, w["w_branch_b"], out_dtype=BF16, name="branch_b")
    mix = _mix_fwd(proj, pa, pb, d, "mix_fwd")
    x1 = _mm(mix, w["w_out"], res=x, name="out_proj")
    h2 = _rms_fwd(x1, small["norm2_g"], "norm2_fwd")
    if dist:
        gu, act, (w_ffn_out_s,) = _ffn_in_fused(h2, w_ffn_in_s, carry=("ag", rest[-1:]), name="ffn_in")
        w_ffn_out = _join_slots(w_ffn_out_s, 0)
    else:
        gu, act = _ffn_in_fused(h2, w_ffn_in_s, name="ffn_in")
        w_ffn_out = rest["w_ffn_out"]
    x2 = _mm(act, w_ffn_out, res=x1, name="ffn_out")
    dx2_bf, d_final_g, loss = _loss_head(x2, small["final_norm_g"], target, "loss_head")

    recv = {}
    dgu, = _mm(dx2_bf, w_ffn_out, tb=True, epi=(_swiglu_bwd_tile, [gu], [(2, BF16)]), name="d_act")
    g_ffn_out = _mm(act, dx2_bf, ta=True, out_dtype=BF16, name="g_w_ffn_out")
    if dist:
        g_ffn_in_s, recv["w_ffn_out"] = _mm(h2, dgu, ta=True, b_slots=True, out_dtype=BF16, out_slots=N_DEV, tm_cap=512,
                                            carry=("a2a", [_to_slots(g_ffn_out, 0)]), name="g_w_ffn_in")
        dh2, (recv_ffn_in,) = _mm(dgu, w_ffn_in_s, tb=True, a_slots=True, b_slots=True, b_group=4, tm_cap=512,
                                  out_dtype=BF16, carry=("a2a", [g_ffn_in_s], [_ALL_K[:-1]]), name="d_h2")
    else:
        g_ffn_in_s = _mm(h2, dgu, ta=True, b_slots=True, out_dtype=BF16, out_slots=N_DEV, tm_cap=512, name="g_w_ffn_in")
        dh2 = _mm(dgu, w_ffn_in_s, tb=True, a_slots=True, b_slots=True, b_group=4, tm_cap=512, out_dtype=BF16, name="d_h2")
    dx1_bf, d_norm2_g = _rms_bwd(x1, small["norm2_g"], dh2, dx2_bf, "norm2_bwd", BF16)
    dmix = _mm(dx1_bf, w["w_out"], tb=True, out_dtype=BF16, name="d_mix")
    g_out = _mm(mix, dx1_bf, ta=True, out_dtype=BF16, name="g_w_out")
    dpa, dpb, dga, dgb = _mix_bwd(proj, pa, pb, dmix, d, "mix_bwd")
    doa = _mm(dpa, w["w_branch_a"], tb=True, out_dtype=BF16, name="d_o_a")
    dob = _mm(dpb, w["w_branch_b"], tb=True, out_dtype=BF16, name="d_o_b")
    g_branch_a = _mm(o_a, dpa, ta=True, out_dtype=BF16, name="g_w_branch_a")
    g_branch_b = _mm(o_b, dpb, ta=True, out_dtype=BF16, name="g_w_branch_b")

    dy_s5, du_direct, g_glu, g_bglu, g_s5d = _s5_post_bwd(y_raw, proj, u_cb, small["s5_d"], w["w_glu"], small["b_glu"], dob, "s5_post_bwd")
    du_scan, dv_st, d_a8 = _s5_bwd_data(dy_s5, x_st, k8_bf, b8_bf, c8_bf, a8, "s5_scan_bwd")
    d_k8, d_b8, d_c8 = _s5_bwd_tables(dy_s5, proj, u_lane_cb, x_st, dv_st, S5_L, "s5_scan_bwd_tables")
    g_lam_re, g_lam_im, g_log_dt, g_b_re, g_b_im, g_c_re, g_c_im = tables_vjp((d_k8, d_b8, d_c8, d_a8))
    du = du_scan + du_direct

    dq, dkk, dvv, dr, dal, g_wa2, g_ba2, g_ghn = _gla_bwd(proj, alow, wa2, small["b_a2"], small["gla_norm_g"], o_pre, states, doa,
                                                        dk=dk, dv=dv, name="gla_bwd")
    dproj = jnp.concatenate([dga, dgb, dvv, dr, dq, dkk, du, dal], axis=1)
    mid = {"w_out": g_out, "w_branch_a": g_branch_a, "w_branch_b": g_branch_b, "w_glu": g_glu.astype(BF16),
           "w_a2": g_wa2[:GLA_RANK].astype(BF16)}
    if dist:
        axes = dict(_REST)
        g_main_t, got = _mm(dproj, h1, ta=True, m_limit=n_main, out_dtype=BF16, name="g_w_in_main",
                            carry=("a2a", [_to_slots(mid[n], axes[n]) for n in mid] + [g_ffn_in_s],
                                   [_ALL_K] * len(mid) + [_ALL_K[-1:]]))
        recv.update(zip(mid, [[g] for g in got[:-1]]))
        recv["w_ffn_in"] = [recv_ffn_in, got[-1]]
    else:
        g_main_t = _mm(dproj, h1, ta=True, m_limit=n_main, out_dtype=BF16, name="g_w_in_main")
    g_al_t = _mm(dal, h1, ta=True, out_dtype=BF16, name="g_w_in_gate_rank")
    mrows = lambda o, n: g_main_t[o:o + n]
    g_w_in_t = jnp.concatenate([mrows(2 * d + 2 * dv, dk), mrows(2 * d + 2 * dv + dk, dk), mrows(2 * d, dv), mrows(2 * d + dv, dv),
                                g_al_t[:GLA_RANK], mrows(2 * d + 2 * dv + 2 * dk, s5w), mrows(0, d), mrows(d, d)], axis=0)
    if dist:
        dh1, recv["w_in"] = _mm(dproj, w_ext_t, out_dtype=BF16, carry=("a2a", [_to_slots(g_w_in_t, 0)]), name="d_h1")
    else:
        dh1 = _mm(dproj, w_ext_t, out_dtype=BF16, name="d_h1")
    grad_x, d_norm1_g = _rms_bwd(x, small["norm1_g"], dh1, dx1_bf, "norm1_bwd", F32)

    small_g = {
        "norm1_g": d_norm1_g, "b_a2": g_ba2, "gla_norm_g": g_ghn, "lam_re": g_lam_re, "lam_im": g_lam_im,
        "log_dt": g_log_dt[None], "s5_b_re": g_b_re, "s5_b_im": g_b_im, "s5_c_re": g_c_re, "s5_c_im": g_c_im,
        "s5_d": g_s5d, "b_glu": g_bglu, "norm2_g": d_norm2_g, "final_norm_g": d_final_g,
    }
    if not dist:
        recv = dict(mid, w_in=g_w_in_t, w_ffn_in=_join_slots(g_ffn_in_s, 1), w_ffn_out=g_ffn_out)
    return loss[0, 0], grad_x, recv, small_g


def _small_2d(name, a):
    a = a[0]
    return a[None] if a.ndim == 1 else a


def kernel(x, norm1_g, w_in, w_a2, b_a2, gla_norm_g, lam_re, lam_im, log_dt, s5_b_re, s5_b_im, s5_c_re, s5_c_im, s5_d, w_glu, b_glu, w_branch_a, w_branch_b, w_out, norm2_g, w_ffn_in, w_ffn_out, final_norm_g, loss_target, m_norm1_g, m_w_in, m_w_a2, m_b_a2, m_gla_norm_g, m_lam_re, m_lam_im, m_log_dt, m_s5_b_re, m_s5_b_im, m_s5_c_re, m_s5_c_im, m_s5_d, m_w_glu, m_b_glu, m_w_branch_a, m_w_branch_b, m_w_out, m_norm2_g, m_w_ffn_in, m_w_ffn_out, m_final_norm_g, v_norm1_g, v_w_in, v_w_a2, v_b_a2, v_gla_norm_g, v_lam_re, v_lam_im, v_log_dt, v_s5_b_re, v_s5_b_im, v_s5_c_re, v_s5_c_im, v_s5_d, v_w_glu, v_b_glu, v_w_branch_a, v_w_branch_b, v_w_out, v_norm2_g, v_w_ffn_in, v_w_ffn_out, v_final_norm_g):
    args = dict(locals())
    weights = {n: args[n] for n in _ORDER}
    m_in = {n: args["m_" + n] for n in _ORDER}
    v_in = {n: args["v_" + n] for n in _ORDER}
    transposed = lambda a: a[0].T[None]
    rest = [weights[n][0].astype(BF16) for n, _ in _REST]
    small = {n: _small_2d(n, weights[n]) for n in _SMALL}
    loss_local, grad_x, recv, small_g = _local_step(x[0], loss_target[0], transposed(weights["w_in"])[0].astype(BF16), small, rest)

    grads, delta, new_m, new_v = {}, {}, {}, {}
    for n, _ in _REST:
        grads[n], delta[n], new_m[n], new_v[n] = _adamw(weights[n], recv[n], m_in[n], v_in[n], "adamw_" + n)
    w_in_out = _adamw(transposed(weights["w_in"]), recv["w_in"], transposed(m_in["w_in"]), transposed(v_in["w_in"]), "adamw_w_in")
    grads["w_in"], delta["w_in"], new_m["w_in"], new_v["w_in"] = (transposed(a) for a in w_in_out)

    s_sizes = [small_g[n].size for n in _SMALL]
    s_offs = [sum(s_sizes[:i]) for i in range(len(s_sizes))]
    s_total = sum(s_sizes)
    s_rows = -(-(-(-(s_total + 1) // LANE)) // LANE) * LANE

    def pack_small(parts):
        flat = jnp.concatenate([p.reshape(-1) for p in parts])
        return jnp.pad(flat, (0, s_rows * LANE - flat.size)).reshape(s_rows, LANE)

    s_flat = pack_small([small_g[n] for n in _SMALL] + [loss_local])
    s_red = _slot_sum(_exchange("ag", [s_flat], "small_grads_all_gather")[0], "small_grads_slot_sum")
    loss = s_red.reshape(-1)[s_total]
    sd, sm, sv = _adamw(pack_small([weights[n] for n in _SMALL])[None], s_red, pack_small([m_in[n] for n in _SMALL])[None],
                        pack_small([v_in[n] for n in _SMALL])[None], "adamw_small")
    sd, sm, sv = sd[0], sm[0], sv[0]
    for n, o, s in zip(_SMALL, s_offs, s_sizes):
        shape = weights[n].shape[1:]
        grads[n], delta[n], new_m[n], new_v[n] = (a.reshape(-1)[o:o + s].reshape(shape) for a in (s_red, sd, sm, sv))

    out = [loss, grad_x[None]]
    for tree in (grads, delta, new_m, new_v):
        out += [tree[n].reshape(weights[n].shape) for n in _ORDER]
    return tuple(out)
```
